```python
import math
import jax, jax.numpy as jnp
from jax import lax
import numpy as np

D_MODEL = 1024
BATCH = 16
SEQ = 2048
DEPTH = 4

N_MIXERS = 3
N_A = (DEPTH + 2) // 3
N_B = (DEPTH + 1) // 3
N_C = DEPTH // 3
EPS = 1e-6
CONV_W = 4
N_MEM = 256

M_D_INNER = 2 * D_MODEL
M_HEAD_DIM = 64
M_HEADS = M_D_INNER // M_HEAD_DIM
M_GROUPS = 8
M_STATE = 128
M_CONV_DIM = M_D_INNER + 2 * M_GROUPS * M_STATE
M_IN = M_D_INNER + M_CONV_DIM + M_HEADS
SSD_CHUNK = 64
DT_MIN = 1e-3
DT_MAX = 1e-1

H_EXPAND = 128
H_HEADS = D_MODEL // H_EXPAND
H_DV = D_MODEL // H_HEADS
HGRN_CHUNK = 32

G_HEAD_DIM = 128
G_QK_HEADS = D_MODEL // G_HEAD_DIM
G_V_HEADS = 2 * G_QK_HEADS
G_KEY_DIM = G_QK_HEADS * G_HEAD_DIM
G_VAL_DIM = G_V_HEADS * G_HEAD_DIM
G_CONV_DIM = 2 * G_KEY_DIM + G_VAL_DIM
G_IN = G_CONV_DIM + G_VAL_DIM + 2 * G_V_HEADS
GDN_CHUNK = 64

X_HEADS = 4
X_HEAD_DIM = D_MODEL // X_HEADS

D_FF = 2816
FFN_CONV_W = 3

kernel_name = 'hybrid_ssd_hgrn2_gdn_memxattn_block'


def rmsnorm(x, w):
    x32 = x.astype(jnp.float32)
    y = x32 * lax.rsqrt(jnp.mean(x32 * x32, axis=-1, keepdims=True) + EPS)
    return (y * w.astype(jnp.float32)).astype(x.dtype)


def l2norm(x):
    x32 = x.astype(jnp.float32)
    return x32 * lax.rsqrt(jnp.sum(x32 * x32, axis=-1, keepdims=True) + EPS)


def causal_dwconv(x, w):
    width, ch = w.shape
    return lax.conv_general_dilated(x, w[:, None, :].astype(x.dtype), window_strides=(1,),
                                    padding=[(width - 1, 0)],
                                    dimension_numbers=('NWC', 'WIO', 'NWC'),
                                    feature_group_count=ch)


def ssd_chunk(x, dt, a, bm, cm, chunk):
    bsz, seq, nh, p = x.shape
    ng, ns = bm.shape[2], bm.shape[3]
    r = nh // ng
    nc = seq // chunk
    f32 = jnp.float32
    xc = (x * dt[..., None]).astype(f32).reshape(bsz, nc, chunk, ng, r, p)
    acum = jnp.cumsum((dt * a).astype(f32).reshape(bsz, nc, chunk, ng, r), axis=2)
    bc = bm.astype(f32).reshape(bsz, nc, chunk, ng, ns)
    cc = cm.astype(f32).reshape(bsz, nc, chunk, ng, ns)
    causal = jnp.tril(jnp.ones((chunk, chunk), dtype=bool))[:, :, None, None]
    decay = jnp.exp(jnp.where(causal, acum[:, :, :, None] - acum[:, :, None, :], -jnp.inf))
    cb = jnp.einsum('bnlgk,bnsgk->bnlsg', cc, bc)
    y_diag = jnp.einsum('bnlsgr,bnsgrp->bnlgrp', cb[..., None] * decay, xc)

    def step(state, inp):
        c_, b_, x_, ac_ = inp
        y = jnp.einsum('blgk,bgrpk->blgrp', c_, state) * jnp.exp(ac_)[..., None]
        last = ac_[:, -1]
        ds = jnp.einsum('bsgk,bsgrp->bgrpk', b_, x_ * jnp.exp(last[:, None] - ac_)[..., None])
        state = state * jnp.exp(last)[..., None, None] + ds
        return state, y

    s0 = jnp.zeros((bsz, ng, r, p, ns), f32)
    xs = tuple(jnp.moveaxis(t, 1, 0) for t in (cc, bc, xc, acum))
    _, y_off = lax.scan(step, s0, xs)
    return (y_diag + jnp.moveaxis(y_off, 0, 1)).reshape(bsz, seq, nh, p)


def gla_chunk(q, k, v, log_f, chunk):
    bsz, seq, nh, dk = q.shape
    nc = seq // chunk

    def blocks(t):
        return t.astype(jnp.float32).reshape(bsz, nc, chunk, nh, t.shape[-1]).transpose(0, 1, 3, 2, 4)

    q, k, v = blocks(q), blocks(k), blocks(v)
    gc = jnp.cumsum(blocks(log_f), axis=3)
    g_last = gc[:, :, :, -1]
    q_dec = q * jnp.exp(gc)
    k_inv = k * jnp.exp(-gc)
    k_end = k * jnp.exp(g_last[:, :, :, None] - gc)
    causal = jnp.tril(jnp.ones((chunk, chunk), dtype=bool))
    att = jnp.where(causal, jnp.einsum('bnhlk,bnhsk->bnhls', q_dec, k_inv), 0.0)
    o_intra = jnp.einsum('bnhls,bnhsv->bnhlv', att, v)

    def step(state, inp):
        qd, ke, vv, gl = inp
        o = jnp.einsum('bhlk,bhkv->bhlv', qd, state)
        state = state * jnp.exp(gl)[..., None] + jnp.einsum('bhsk,bhsv->bhkv', ke, vv)
        return state, o

    s0 = jnp.zeros((bsz, nh, dk, v.shape[-1]), jnp.float32)
    xs = tuple(jnp.moveaxis(t, 1, 0) for t in (q_dec, k_end, v, g_last))
    _, o_inter = lax.scan(step, s0, xs)
    o = o_intra + jnp.moveaxis(o_inter, 0, 1)
    return o.transpose(0, 1, 3, 2, 4).reshape(bsz, seq, nh, -1)


def gated_delta_chunk(q, k, v, g, beta, chunk):
    bsz, seq, nh, dk = q.shape
    dv = v.shape[-1]
    nc = seq // chunk

    def blocks(t):
        return t.astype(jnp.float32).reshape(bsz, nc, chunk, nh, t.shape[-1]).transpose(0, 1, 3, 2, 4)

    def blocks_s(t):
        return t.astype(jnp.float32).reshape(bsz, nc, chunk, nh).transpose(0, 1, 3, 2)

    q, k, v = blocks(q), blocks(k), blocks(v)
    beta = blocks_s(beta)
    gc = jnp.cumsum(blocks_s(g), axis=-1)
    incl = jnp.tril(jnp.ones((chunk, chunk), dtype=bool))
    strict = jnp.tril(jnp.ones((chunk, chunk), dtype=bool), k=-1)
    decay = jnp.exp(jnp.where(incl, gc[..., :, None] - gc[..., None, :], -jnp.inf))
    kb = k * beta[..., None]
    m = jnp.where(strict, jnp.einsum('bnhlk,bnhsk->bnhls', kb, k) * decay, 0.0)
    a_mat = m + jnp.eye(chunk, dtype=jnp.float32)
    rhs = jnp.concatenate([v * beta[..., None], kb * jnp.exp(gc)[..., None]], axis=-1)
    sol = lax.linalg.triangular_solve(a_mat, rhs, left_side=True, lower=True, unit_diagonal=True)
    u, w = sol[..., :dv], sol[..., dv:]
    att = jnp.einsum('bnhlk,bnhsk->bnhls', q, k) * decay
    q_dec = q * jnp.exp(gc)[..., None]
    g_last = gc[..., -1]
    k_end = k * jnp.exp(g_last[..., None] - gc)[..., None]

    def step(state, inp):
        qd, aa, uu, ww, ke, gl = inp
        v_new = uu - jnp.einsum('bhlk,bhkv->bhlv', ww, state)
        o = jnp.einsum('bhlk,bhkv->bhlv', qd, state) + jnp.einsum('bhls,bhsv->bhlv', aa, v_new)
        state = state * jnp.exp(gl)[..., None, None] + jnp.einsum('bhsk,bhsv->bhkv', ke, v_new)
        return state, o

    s0 = jnp.zeros((bsz, nh, dk, dv), jnp.float32)
    xs = tuple(jnp.moveaxis(t, 1, 0) for t in (q_dec, att, u, w, k_end, g_last))
    _, o = lax.scan(step, s0, xs)
    o = jnp.moveaxis(o, 0, 1)
    return o.transpose(0, 1, 3, 2, 4).reshape(bsz, seq, nh, dv)


def mamba2_mixer(h, in_w, conv_w, conv_b, dt_bias, a_log, d_skip, norm_w, out_w):
    bsz, seq, _ = h.shape
    f32 = jnp.float32
    proj = h @ in_w
    z = proj[..., :M_D_INNER]
    xbc = jax.nn.silu(causal_dwconv(proj[..., M_D_INNER:M_D_INNER + M_CONV_DIM], conv_w) + conv_b)
    dt = jax.nn.softplus(proj[..., M_D_INNER + M_CONV_DIM:].astype(f32) + dt_bias.astype(f32))
    xs = xbc[..., :M_D_INNER].reshape(bsz, seq, M_HEADS, M_HEAD_DIM)
    bm = xbc[..., M_D_INNER:M_D_INNER + M_GROUPS * M_STATE].reshape(bsz, seq, M_GROUPS, M_STATE)
    cm = xbc[..., M_D_INNER + M_GROUPS * M_STATE:].reshape(bsz, seq, M_GROUPS, M_STATE)
    a = -jnp.exp(a_log.astype(f32))
    y = ssd_chunk(xs, dt, a, bm, cm, SSD_CHUNK) + d_skip.astype(f32)[:, None] * xs.astype(f32)
    y = y.reshape(bsz, seq, M_D_INNER).astype(h.dtype) * jax.nn.silu(z)
    gs = M_D_INNER // M_GROUPS
    y = rmsnorm(y.reshape(bsz, seq, M_GROUPS, gs), norm_w.reshape(M_GROUPS, gs))
    return y.reshape(bsz, seq, M_D_INNER) @ out_w


def hgrn2_mixer(h, in_w, lower_bound, norm_w, out_w):
    bsz, seq, _ = h.shape
    q, f, i, g = jnp.split(h @ in_w, 4, axis=-1)

    def heads(t):
        return t.reshape(bsz, seq, H_HEADS, -1)

    lb = lower_bound.astype(jnp.float32)
    forget = lb + (1.0 - lb) * jax.nn.sigmoid(f.astype(jnp.float32))
    o = gla_chunk(heads(jax.nn.silu(q)) * H_EXPAND ** -0.5, heads(1.0 - forget), heads(i),
                  heads(jnp.log(forget)), HGRN_CHUNK)
    o = rmsnorm(o.astype(h.dtype), norm_w) * jax.nn.silu(heads(g))
    return o.reshape(bsz, seq, D_MODEL) @ out_w


def gated_deltanet_mixer(h, in_w, conv_w, a_log, dt_bias, norm_w, out_w):
    bsz, seq, _ = h.shape
    f32 = jnp.float32
    proj = h @ in_w
    qkv = jax.nn.silu(causal_dwconv(proj[..., :G_CONV_DIM], conv_w))
    z = proj[..., G_CONV_DIM:G_CONV_DIM + G_VAL_DIM]
    b = proj[..., G_CONV_DIM + G_VAL_DIM:G_CONV_DIM + G_VAL_DIM + G_V_HEADS]
    a = proj[..., G_CONV_DIM + G_VAL_DIM + G_V_HEADS:]
    q = l2norm(qkv[..., :G_KEY_DIM].reshape(bsz, seq, G_QK_HEADS, G_HEAD_DIM))
    k = l2norm(qkv[..., G_KEY_DIM:2 * G_KEY_DIM].reshape(bsz, seq, G_QK_HEADS, G_HEAD_DIM))
    v = qkv[..., 2 * G_KEY_DIM:].reshape(bsz, seq, G_V_HEADS, G_HEAD_DIM)
    rep = G_V_HEADS // G_QK_HEADS
    q = jnp.repeat(q, rep, axis=2) * G_HEAD_DIM ** -0.5
    k = jnp.repeat(k, rep, axis=2)
    beta = jax.nn.sigmoid(b.astype(f32))
    g = -jnp.exp(a_log.astype(f32)) * jax.nn.softplus(a.astype(f32) + dt_bias.astype(f32))
    o = gated_delta_chunk(q, k, v, g, beta, GDN_CHUNK)
    o = rmsnorm(o.astype(h.dtype), norm_w) * jax.nn.silu(z.reshape(bsz, seq, G_V_HEADS, G_HEAD_DIM))
    return o.reshape(bsz, seq, G_VAL_DIM) @ out_w


def memory_cross_attention(h, mem_n, wq, wkv, wo):
    bsz, seq, _ = h.shape
    q = (h @ wq).reshape(bsz, seq, X_HEADS, X_HEAD_DIM)
    k, v = jnp.split(mem_n @ wkv, 2, axis=-1)
    k = k.reshape(bsz, -1, X_HEADS, X_HEAD_DIM)
    v = v.reshape(bsz, -1, X_HEADS, X_HEAD_DIM)
    s = jnp.einsum('blhd,bmhd->bhlm', q, k).astype(jnp.float32) * X_HEAD_DIM ** -0.5
    p = jax.nn.softmax(s, axis=-1).astype(h.dtype)
    o = jnp.einsum('bhlm,bmhd->blhd', p, v).reshape(bsz, seq, D_MODEL)
    return o @ wo


def conv_glu_ffn(h, up_w, conv_w, conv_b, down_w):
    gate, up = jnp.split(h @ up_w, 2, axis=-1)
    gate = causal_dwconv(gate, conv_w) + conv_b
    return (jax.nn.silu(gate) * up) @ down_w


def _fwd_setup_inputs(seed: int = 0) -> dict:
    key = jax.random.key(seed)
    keys = iter(jax.random.split(key, 48))
    d = D_MODEL
    out_scale = (2 * DEPTH) ** -0.5

    def normal(shape, scale):
        return scale * jax.random.normal(next(keys), shape, jnp.float32)

    def gain(shape):
        return 1.0 + 0.05 * jax.random.normal(next(keys), shape, jnp.float32)

    def dt_bias(shape):
        u = jax.random.uniform(next(keys), shape, jnp.float32)
        dt = jnp.exp(u * (math.log(DT_MAX) - math.log(DT_MIN)) + math.log(DT_MIN))
        return dt + jnp.log(-jnp.expm1(-dt))

    def a_log(shape):
        return jnp.log(jax.random.uniform(next(keys), shape, jnp.float32, 1.0, 16.0))

    return {
        'x': normal((BATCH, SEQ, d), 1.0),
        'mem': normal((BATCH, N_MEM, d), 1.0),
        'ln_mix': gain((DEPTH, d)),
        'ln_xattn': gain((DEPTH, d)),
        'ln_mem': gain((DEPTH, d)),
        'ln_ffn': gain((DEPTH, d)),
        'final_norm': gain((d,)),
        'm_in_w': normal((N_A, d, M_IN), d ** -0.5),
        'm_conv_w': normal((N_A, CONV_W, M_CONV_DIM), CONV_W ** -0.5),
        'm_conv_b': normal((N_A, M_CONV_DIM), 0.02),
        'm_dt_bias': dt_bias((N_A, M_HEADS)),
        'm_a_log': a_log((N_A, M_HEADS)),
        'm_d': gain((N_A, M_HEADS)),
        'm_norm_w': gain((N_A, M_D_INNER)),
        'm_out_w': normal((N_A, M_D_INNER, d), M_D_INNER ** -0.5 * out_scale),
        'h_in_w': normal((N_B, d, 4 * d), d ** -0.5),
        'h_lower_bounds': normal((DEPTH, d), 0.1),
        'h_norm_w': gain((N_B, H_DV)),
        'h_out_w': normal((N_B, d, d), d ** -0.5 * out_scale),
        'g_in_w': normal((N_C, d, G_IN), d ** -0.5),
        'g_conv_w': normal((N_C, CONV_W, G_CONV_DIM), CONV_W ** -0.5),
        'g_a_log': a_log((N_C, G_V_HEADS)),
        'g_dt_bias': dt_bias((N_C, G_V_HEADS)),
        'g_norm_w': gain((N_C, G_HEAD_DIM)),
        'g_out_w': normal((N_C, G_VAL_DIM, d), G_VAL_DIM ** -0.5 * out_scale),
        'xa_q': normal((DEPTH, d, d), d ** -0.5),
        'xa_kv': normal((DEPTH, d, 2 * d), d ** -0.5),
        'xa_o': normal((DEPTH, d, d), d ** -0.5 * out_scale),
        'f_up': normal((DEPTH, d, 2 * D_FF), d ** -0.5),
        'f_conv_w': normal((DEPTH, FFN_CONV_W, D_FF), FFN_CONV_W ** -0.5),
        'f_conv_b': normal((DEPTH, D_FF), 0.02),
        'f_down': normal((DEPTH, D_FF, d), D_FF ** -0.5 * out_scale),
    }


def _fwd_reference(x, mem, ln_mix, ln_xattn, ln_mem, ln_ffn, final_norm,
              m_in_w, m_conv_w, m_conv_b, m_dt_bias, m_a_log, m_d, m_norm_w, m_out_w,
              h_in_w, h_lower_bounds, h_norm_w, h_out_w,
              g_in_w, g_conv_w, g_a_log, g_dt_bias, g_norm_w, g_out_w,
              xa_q, xa_kv, xa_o, f_up, f_conv_w, f_conv_b, f_down):
    lb = jnp.cumsum(jax.nn.softmax(h_lower_bounds.astype(jnp.float32), axis=0), axis=0)
    lb = lb - lb[:1]
    ia = 0
    ib = 0
    ic = 0
    for i in range(DEPTH):
        hn = rmsnorm(x, ln_mix[i])
        if i % N_MIXERS == 0:
            mix = mamba2_mixer(hn, m_in_w[ia], m_conv_w[ia], m_conv_b[ia], m_dt_bias[ia],
                               m_a_log[ia], m_d[ia], m_norm_w[ia], m_out_w[ia])
            ia += 1
        elif i % N_MIXERS == 1:
            mix = hgrn2_mixer(hn, h_in_w[ib], lb[i], h_norm_w[ib], h_out_w[ib])
            ib += 1
        else:
            mix = gated_deltanet_mixer(hn, g_in_w[ic], g_conv_w[ic], g_a_log[ic], g_dt_bias[ic],
                                       g_norm_w[ic], g_out_w[ic])
            ic += 1
        x = x + mix.astype(x.dtype)
        x = x + memory_cross_attention(rmsnorm(x, ln_xattn[i]), rmsnorm(mem, ln_mem[i]),
                                       xa_q[i], xa_kv[i], xa_o[i]).astype(x.dtype)
        x = x + conv_glu_ffn(rmsnorm(x, ln_ffn[i]), f_up[i], f_conv_w[i], f_conv_b[i],
                             f_down[i]).astype(x.dtype)
    return rmsnorm(x, final_norm)


import jax as _jax
import jax.numpy as _jnp

TWIN_FORMAT = 'train_step'
FWD_PARAMS = ['x', 'mem', 'ln_mix', 'ln_xattn', 'ln_mem', 'ln_ffn', 'final_norm', 'm_in_w', 'm_conv_w', 'm_conv_b', 'm_dt_bias', 'm_a_log', 'm_d', 'm_norm_w', 'm_out_w', 'h_in_w', 'h_lower_bounds', 'h_norm_w', 'h_out_w', 'g_in_w', 'g_conv_w', 'g_a_log', 'g_dt_bias', 'g_norm_w', 'g_out_w', 'xa_q', 'xa_kv', 'xa_o', 'f_up', 'f_conv_w', 'f_conv_b', 'f_down']
TWIN_WEIGHTS = ['ln_mix', 'ln_xattn', 'ln_mem', 'ln_ffn', 'final_norm', 'm_in_w', 'm_conv_w', 'm_conv_b', 'm_dt_bias', 'm_a_log', 'm_d', 'm_norm_w', 'm_out_w', 'h_in_w', 'h_lower_bounds', 'h_norm_w', 'h_out_w', 'g_in_w', 'g_conv_w', 'g_a_log', 'g_dt_bias', 'g_norm_w', 'g_out_w', 'xa_q', 'xa_kv', 'xa_o', 'f_up', 'f_conv_w', 'f_conv_b', 'f_down']
TWIN_DIFF_INPUT = 'x'
TWIN_INPUTS = ['x', 'mem', 'ln_mix', 'ln_xattn', 'ln_mem', 'ln_ffn', 'final_norm', 'm_in_w', 'm_conv_w', 'm_conv_b', 'm_dt_bias', 'm_a_log', 'm_d', 'm_norm_w', 'm_out_w', 'h_in_w', 'h_lower_bounds', 'h_norm_w', 'h_out_w', 'g_in_w', 'g_conv_w', 'g_a_log', 'g_dt_bias', 'g_norm_w', 'g_out_w', 'xa_q', 'xa_kv', 'xa_o', 'f_up', 'f_conv_w', 'f_conv_b', 'f_down', 'loss_target', 'm_ln_mix', 'm_ln_xattn', 'm_ln_mem', 'm_ln_ffn', 'm_final_norm', 'm_m_in_w', 'm_m_conv_w', 'm_m_conv_b', 'm_m_dt_bias', 'm_m_a_log', 'm_m_d', 'm_m_norm_w', 'm_m_out_w', 'm_h_in_w', 'm_h_lower_bounds', 'm_h_norm_w', 'm_h_out_w', 'm_g_in_w', 'm_g_conv_w', 'm_g_a_log', 'm_g_dt_bias', 'm_g_norm_w', 'm_g_out_w', 'm_xa_q', 'm_xa_kv', 'm_xa_o', 'm_f_up', 'm_f_conv_w', 'm_f_conv_b', 'm_f_down', 'v_ln_mix', 'v_ln_xattn', 'v_ln_mem', 'v_ln_ffn', 'v_final_norm', 'v_m_in_w', 'v_m_conv_w', 'v_m_conv_b', 'v_m_dt_bias', 'v_m_a_log', 'v_m_d', 'v_m_norm_w', 'v_m_out_w', 'v_h_in_w', 'v_h_lower_bounds', 'v_h_norm_w', 'v_h_out_w', 'v_g_in_w', 'v_g_conv_w', 'v_g_a_log', 'v_g_dt_bias', 'v_g_norm_w', 'v_g_out_w', 'v_xa_q', 'v_xa_kv', 'v_xa_o', 'v_f_up', 'v_f_conv_w', 'v_f_conv_b', 'v_f_down']
TWIN_OUTPUTS = ['loss', 'grad_x', 'grad_ln_mix', 'grad_ln_xattn', 'grad_ln_mem', 'grad_ln_ffn', 'grad_final_norm', 'grad_m_in_w', 'grad_m_conv_w', 'grad_m_conv_b', 'grad_m_dt_bias', 'grad_m_a_log', 'grad_m_d', 'grad_m_norm_w', 'grad_m_out_w', 'grad_h_in_w', 'grad_h_lower_bounds', 'grad_h_norm_w', 'grad_h_out_w', 'grad_g_in_w', 'grad_g_conv_w', 'grad_g_a_log', 'grad_g_dt_bias', 'grad_g_norm_w', 'grad_g_out_w', 'grad_xa_q', 'grad_xa_kv', 'grad_xa_o', 'grad_f_up', 'grad_f_conv_w', 'grad_f_conv_b', 'grad_f_down', 'delta_ln_mix', 'delta_ln_xattn', 'delta_ln_mem', 'delta_ln_ffn', 'delta_final_norm', 'delta_m_in_w', 'delta_m_conv_w', 'delta_m_conv_b', 'delta_m_dt_bias', 'delta_m_a_log', 'delta_m_d', 'delta_m_norm_w', 'delta_m_out_w', 'delta_h_in_w', 'delta_h_lower_bounds', 'delta_h_norm_w', 'delta_h_out_w', 'delta_g_in_w', 'delta_g_conv_w', 'delta_g_a_log', 'delta_g_dt_bias', 'delta_g_norm_w', 'delta_g_out_w', 'delta_xa_q', 'delta_xa_kv', 'delta_xa_o', 'delta_f_up', 'delta_f_conv_w', 'delta_f_conv_b', 'delta_f_down', 'new_m_ln_mix', 'new_m_ln_xattn', 'new_m_ln_mem', 'new_m_ln_ffn', 'new_m_final_norm', 'new_m_m_in_w', 'new_m_m_conv_w', 'new_m_m_conv_b', 'new_m_m_dt_bias', 'new_m_m_a_log', 'new_m_m_d', 'new_m_m_norm_w', 'new_m_m_out_w', 'new_m_h_in_w', 'new_m_h_lower_bounds', 'new_m_h_norm_w', 'new_m_h_out_w', 'new_m_g_in_w', 'new_m_g_conv_w', 'new_m_g_a_log', 'new_m_g_dt_bias', 'new_m_g_norm_w', 'new_m_g_out_w', 'new_m_xa_q', 'new_m_xa_kv', 'new_m_xa_o', 'new_m_f_up', 'new_m_f_conv_w', 'new_m_f_conv_b', 'new_m_f_down', 'new_v_ln_mix', 'new_v_ln_xattn', 'new_v_ln_mem', 'new_v_ln_ffn', 'new_v_final_norm', 'new_v_m_in_w', 'new_v_m_conv_w', 'new_v_m_conv_b', 'new_v_m_dt_bias', 'new_v_m_a_log', 'new_v_m_d', 'new_v_m_norm_w', 'new_v_m_out_w', 'new_v_h_in_w', 'new_v_h_lower_bounds', 'new_v_h_norm_w', 'new_v_h_out_w', 'new_v_g_in_w', 'new_v_g_conv_w', 'new_v_g_a_log', 'new_v_g_dt_bias', 'new_v_g_norm_w', 'new_v_g_out_w', 'new_v_xa_q', 'new_v_xa_kv', 'new_v_xa_o', 'new_v_f_up', 'new_v_f_conv_w', 'new_v_f_conv_b', 'new_v_f_down']
TWIN_LEAF_KINDS = {'loss': 'loss', 'grad_x': 'grad_x', 'grad_ln_mix': 'grad_w', 'grad_ln_xattn': 'grad_w', 'grad_ln_mem': 'grad_w', 'grad_ln_ffn': 'grad_w', 'grad_final_norm': 'grad_w', 'grad_m_in_w': 'grad_w', 'grad_m_conv_w': 'grad_w', 'grad_m_conv_b': 'grad_w', 'grad_m_dt_bias': 'grad_w', 'grad_m_a_log': 'grad_w', 'grad_m_d': 'grad_w', 'grad_m_norm_w': 'grad_w', 'grad_m_out_w': 'grad_w', 'grad_h_in_w': 'grad_w', 'grad_h_lower_bounds': 'grad_w', 'grad_h_norm_w': 'grad_w', 'grad_h_out_w': 'grad_w', 'grad_g_in_w': 'grad_w', 'grad_g_conv_w': 'grad_w', 'grad_g_a_log': 'grad_w', 'grad_g_dt_bias': 'grad_w', 'grad_g_norm_w': 'grad_w', 'grad_g_out_w': 'grad_w', 'grad_xa_q': 'grad_w', 'grad_xa_kv': 'grad_w', 'grad_xa_o': 'grad_w', 'grad_f_up': 'grad_w', 'grad_f_conv_w': 'grad_w', 'grad_f_conv_b': 'grad_w', 'grad_f_down': 'grad_w', 'delta_ln_mix': 'delta_w', 'delta_ln_xattn': 'delta_w', 'delta_ln_mem': 'delta_w', 'delta_ln_ffn': 'delta_w', 'delta_final_norm': 'delta_w', 'delta_m_in_w': 'delta_w', 'delta_m_conv_w': 'delta_w', 'delta_m_conv_b': 'delta_w', 'delta_m_dt_bias': 'delta_w', 'delta_m_a_log': 'delta_w', 'delta_m_d': 'delta_w', 'delta_m_norm_w': 'delta_w', 'delta_m_out_w': 'delta_w', 'delta_h_in_w': 'delta_w', 'delta_h_lower_bounds': 'delta_w', 'delta_h_norm_w': 'delta_w', 'delta_h_out_w': 'delta_w', 'delta_g_in_w': 'delta_w', 'delta_g_conv_w': 'delta_w', 'delta_g_a_log': 'delta_w', 'delta_g_dt_bias': 'delta_w', 'delta_g_norm_w': 'delta_w', 'delta_g_out_w': 'delta_w', 'delta_xa_q': 'delta_w', 'delta_xa_kv': 'delta_w', 'delta_xa_o': 'delta_w', 'delta_f_up': 'delta_w', 'delta_f_conv_w': 'delta_w', 'delta_f_conv_b': 'delta_w', 'delta_f_down': 'delta_w', 'new_m_ln_mix': 'new_m', 'new_m_ln_xattn': 'new_m', 'new_m_ln_mem': 'new_m', 'new_m_ln_ffn': 'new_m', 'new_m_final_norm': 'new_m', 'new_m_m_in_w': 'new_m', 'new_m_m_conv_w': 'new_m', 'new_m_m_conv_b': 'new_m', 'new_m_m_dt_bias': 'new_m', 'new_m_m_a_log': 'new_m', 'new_m_m_d': 'new_m', 'new_m_m_norm_w': 'new_m', 'new_m_m_out_w': 'new_m', 'new_m_h_in_w': 'new_m', 'new_m_h_lower_bounds': 'new_m', 'new_m_h_norm_w': 'new_m', 'new_m_h_out_w': 'new_m', 'new_m_g_in_w': 'new_m', 'new_m_g_conv_w': 'new_m', 'new_m_g_a_log': 'new_m', 'new_m_g_dt_bias': 'new_m', 'new_m_g_norm_w': 'new_m', 'new_m_g_out_w': 'new_m', 'new_m_xa_q': 'new_m', 'new_m_xa_kv': 'new_m', 'new_m_xa_o': 'new_m', 'new_m_f_up': 'new_m', 'new_m_f_conv_w': 'new_m', 'new_m_f_conv_b': 'new_m', 'new_m_f_down': 'new_m', 'new_v_ln_mix': 'new_v', 'new_v_ln_xattn': 'new_v', 'new_v_ln_mem': 'new_v', 'new_v_ln_ffn': 'new_v', 'new_v_final_norm': 'new_v', 'new_v_m_in_w': 'new_v', 'new_v_m_conv_w': 'new_v', 'new_v_m_conv_b': 'new_v', 'new_v_m_dt_bias': 'new_v', 'new_v_m_a_log': 'new_v', 'new_v_m_d': 'new_v', 'new_v_m_norm_w': 'new_v', 'new_v_m_out_w': 'new_v', 'new_v_h_in_w': 'new_v', 'new_v_h_lower_bounds': 'new_v', 'new_v_h_norm_w': 'new_v', 'new_v_h_out_w': 'new_v', 'new_v_g_in_w': 'new_v', 'new_v_g_conv_w': 'new_v', 'new_v_g_a_log': 'new_v', 'new_v_g_dt_bias': 'new_v', 'new_v_g_norm_w': 'new_v', 'new_v_g_out_w': 'new_v', 'new_v_xa_q': 'new_v', 'new_v_xa_kv': 'new_v', 'new_v_xa_o': 'new_v', 'new_v_f_up': 'new_v', 'new_v_f_conv_w': 'new_v', 'new_v_f_conv_b': 'new_v', 'new_v_f_down': 'new_v'}


def _forward(args):
    return _fwd_reference(*[args[k] for k in FWD_PARAMS])


def _output_shape():
    out = _jax.eval_shape(lambda: _forward(_fwd_setup_inputs(0)))
    return out.shape, out.dtype

N_MICROBATCH = 1
ADAM_LR = 0.001
ADAM_B1 = 0.9
ADAM_B2 = 0.999
ADAM_EPS = 1e-08
ADAM_WD = 0.01
ADAM_STEP = 10
PER_EXAMPLE_BATCH_AXIS = {'x': 0, 'mem': 0, 'loss_target': 0}
SHARED_INPUTS = []
_WEIGHT_DTYPES = {'ln_mix': _jnp.float32, 'ln_xattn': _jnp.float32, 'ln_mem': _jnp.float32, 'ln_ffn': _jnp.float32, 'final_norm': _jnp.float32, 'm_in_w': _jnp.float32, 'm_conv_w': _jnp.float32, 'm_conv_b': _jnp.float32, 'm_dt_bias': _jnp.float32, 'm_a_log': _jnp.float32, 'm_d': _jnp.float32, 'm_norm_w': _jnp.float32, 'm_out_w': _jnp.float32, 'h_in_w': _jnp.float32, 'h_lower_bounds': _jnp.float32, 'h_norm_w': _jnp.float32, 'h_out_w': _jnp.float32, 'g_in_w': _jnp.float32, 'g_conv_w': _jnp.float32, 'g_a_log': _jnp.float32, 'g_dt_bias': _jnp.float32, 'g_norm_w': _jnp.float32, 'g_out_w': _jnp.float32, 'xa_q': _jnp.float32, 'xa_kv': _jnp.float32, 'xa_o': _jnp.float32, 'f_up': _jnp.float32, 'f_conv_w': _jnp.float32, 'f_conv_b': _jnp.float32, 'f_down': _jnp.float32}
MOMENT_SCALE = {'ln_mix': 7.837701e-02, 'ln_xattn': 6.519764e-03, 'ln_mem': 9.681923e-03, 'ln_ffn': 5.479901e-02, 'final_norm': 3.197094e+01, 'm_in_w': 3.836329e-02, 'm_conv_w': 3.401142e-02, 'm_conv_b': 4.927725e-02, 'm_dt_bias': 9.715213e-02, 'm_a_log': 1.560667e-01, 'm_d': 1.963190e-01, 'm_norm_w': 4.435203e-02, 'm_out_w': 1.792715e-01, 'h_in_w': 2.840309e-02, 'h_lower_bounds': 1.199896e-03, 'h_norm_w': 1.430363e-01, 'h_out_w': 1.108612e-01, 'g_in_w': 2.474509e-02, 'g_conv_w': 2.453450e-02, 'g_a_log': 9.529286e-02, 'g_dt_bias': 9.243833e-02, 'g_norm_w': 9.684918e-02, 'g_out_w': 1.010727e-01, 'xa_q': 6.543385e-03, 'xa_kv': 6.686109e-03, 'xa_o': 1.929350e-02, 'f_up': 2.266669e-02, 'f_conv_w': 2.309544e-02, 'f_conv_b': 2.202032e-02, 'f_down': 1.048745e-01}


def _to_microbatches(a, axis):
    t = _jnp.moveaxis(a, axis, 0)
    t = t.reshape((N_MICROBATCH, t.shape[0] // N_MICROBATCH) + t.shape[1:])
    return _jnp.moveaxis(t, 1, axis + 1)


def setup_inputs(seed: int = 0) -> dict:
    inp = _fwd_setup_inputs(seed)
    key = _jax.random.fold_in(_jax.random.key(seed), 7919)
    shape, _ = _output_shape()
    out = dict(inp)
    out["loss_target"] = _jax.random.normal(_jax.random.fold_in(key, 0), shape, _jnp.float32)
    for i, name in enumerate(TWIN_WEIGHTS):
        w = inp[name].astype(_jnp.float32)
        if MOMENT_SCALE is None:
            s = _jnp.sqrt(_jnp.mean(_jnp.square(w)) + 1e-30)
        else:
            s = MOMENT_SCALE[name]
        km, kv = _jax.random.split(_jax.random.fold_in(key, i + 1))
        out[name] = w
        out["m_" + name] = s * _jax.random.normal(km, w.shape, _jnp.float32)
        out["v_" + name] = (s * s) * _jax.random.uniform(kv, w.shape, _jnp.float32, 0.5, 1.5)
    if N_MICROBATCH > 1:
        for name, axis in PER_EXAMPLE_BATCH_AXIS.items():
            out[name] = _to_microbatches(out[name], axis)
    return {'x': out['x'], 'mem': out['mem'], 'ln_mix': out['ln_mix'], 'ln_xattn': out['ln_xattn'], 'ln_mem': out['ln_mem'], 'ln_ffn': out['ln_ffn'], 'final_norm': out['final_norm'], 'm_in_w': out['m_in_w'], 'm_conv_w': out['m_conv_w'], 'm_conv_b': out['m_conv_b'], 'm_dt_bias': out['m_dt_bias'], 'm_a_log': out['m_a_log'], 'm_d': out['m_d'], 'm_norm_w': out['m_norm_w'], 'm_out_w': out['m_out_w'], 'h_in_w': out['h_in_w'], 'h_lower_bounds': out['h_lower_bounds'], 'h_norm_w': out['h_norm_w'], 'h_out_w': out['h_out_w'], 'g_in_w': out['g_in_w'], 'g_conv_w': out['g_conv_w'], 'g_a_log': out['g_a_log'], 'g_dt_bias': out['g_dt_bias'], 'g_norm_w': out['g_norm_w'], 'g_out_w': out['g_out_w'], 'xa_q': out['xa_q'], 'xa_kv': out['xa_kv'], 'xa_o': out['xa_o'], 'f_up': out['f_up'], 'f_conv_w': out['f_conv_w'], 'f_conv_b': out['f_conv_b'], 'f_down': out['f_down'], 'loss_target': out['loss_target'], 'm_ln_mix': out['m_ln_mix'], 'm_ln_xattn': out['m_ln_xattn'], 'm_ln_mem': out['m_ln_mem'], 'm_ln_ffn': out['m_ln_ffn'], 'm_final_norm': out['m_final_norm'], 'm_m_in_w': out['m_m_in_w'], 'm_m_conv_w': out['m_m_conv_w'], 'm_m_conv_b': out['m_m_conv_b'], 'm_m_dt_bias': out['m_m_dt_bias'], 'm_m_a_log': out['m_m_a_log'], 'm_m_d': out['m_m_d'], 'm_m_norm_w': out['m_m_norm_w'], 'm_m_out_w': out['m_m_out_w'], 'm_h_in_w': out['m_h_in_w'], 'm_h_lower_bounds': out['m_h_lower_bounds'], 'm_h_norm_w': out['m_h_norm_w'], 'm_h_out_w': out['m_h_out_w'], 'm_g_in_w': out['m_g_in_w'], 'm_g_conv_w': out['m_g_conv_w'], 'm_g_a_log': out['m_g_a_log'], 'm_g_dt_bias': out['m_g_dt_bias'], 'm_g_norm_w': out['m_g_norm_w'], 'm_g_out_w': out['m_g_out_w'], 'm_xa_q': out['m_xa_q'], 'm_xa_kv': out['m_xa_kv'], 'm_xa_o': out['m_xa_o'], 'm_f_up': out['m_f_up'], 'm_f_conv_w': out['m_f_conv_w'], 'm_f_conv_b': out['m_f_conv_b'], 'm_f_down': out['m_f_down'], 'v_ln_mix': out['v_ln_mix'], 'v_ln_xattn': out['v_ln_xattn'], 'v_ln_mem': out['v_ln_mem'], 'v_ln_ffn': out['v_ln_ffn'], 'v_final_norm': out['v_final_norm'], 'v_m_in_w': out['v_m_in_w'], 'v_m_conv_w': out['v_m_conv_w'], 'v_m_conv_b': out['v_m_conv_b'], 'v_m_dt_bias': out['v_m_dt_bias'], 'v_m_a_log': out['v_m_a_log'], 'v_m_d': out['v_m_d'], 'v_m_norm_w': out['v_m_norm_w'], 'v_m_out_w': out['v_m_out_w'], 'v_h_in_w': out['v_h_in_w'], 'v_h_lower_bounds': out['v_h_lower_bounds'], 'v_h_norm_w': out['v_h_norm_w'], 'v_h_out_w': out['v_h_out_w'], 'v_g_in_w': out['v_g_in_w'], 'v_g_conv_w': out['v_g_conv_w'], 'v_g_a_log': out['v_g_a_log'], 'v_g_dt_bias': out['v_g_dt_bias'], 'v_g_norm_w': out['v_g_norm_w'], 'v_g_out_w': out['v_g_out_w'], 'v_xa_q': out['v_xa_q'], 'v_xa_kv': out['v_xa_kv'], 'v_xa_o': out['v_xa_o'], 'v_f_up': out['v_f_up'], 'v_f_conv_w': out['v_f_conv_w'], 'v_f_conv_b': out['v_f_conv_b'], 'v_f_down': out['v_f_down']}


def _loss(weights, diff, rest, loss_target):
    with _jax.named_scope("forward"):
        args = {**rest, TWIN_DIFF_INPUT: diff, **{k: w.astype(_WEIGHT_DTYPES[k]) for k, w in weights.items()}}
        y = _forward(args)
    with _jax.named_scope("loss_head"):
        err = _jnp.square(y.astype(_jnp.float32) - loss_target)
        return 0.5 * _jnp.sum(_jnp.mean(err, axis=-1)) if err.ndim else 0.5 * err


def _adamw(w, g, m, v):
    m = ADAM_B1 * m + (1.0 - ADAM_B1) * g
    v = ADAM_B2 * v + (1.0 - ADAM_B2) * _jnp.square(g)
    m_hat = m / (1.0 - ADAM_B1 ** ADAM_STEP)
    v_hat = v / (1.0 - ADAM_B2 ** ADAM_STEP)
    delta = -ADAM_LR * (m_hat / (_jnp.sqrt(v_hat) + ADAM_EPS) + ADAM_WD * w)
    return delta, m, v


def reference(x, mem, ln_mix, ln_xattn, ln_mem, ln_ffn, final_norm, m_in_w, m_conv_w, m_conv_b, m_dt_bias, m_a_log, m_d, m_norm_w, m_out_w, h_in_w, h_lower_bounds, h_norm_w, h_out_w, g_in_w, g_conv_w, g_a_log, g_dt_bias, g_norm_w, g_out_w, xa_q, xa_kv, xa_o, f_up, f_conv_w, f_conv_b, f_down, loss_target, m_ln_mix, m_ln_xattn, m_ln_mem, m_ln_ffn, m_final_norm, m_m_in_w, m_m_conv_w, m_m_conv_b, m_m_dt_bias, m_m_a_log, m_m_d, m_m_norm_w, m_m_out_w, m_h_in_w, m_h_lower_bounds, m_h_norm_w, m_h_out_w, m_g_in_w, m_g_conv_w, m_g_a_log, m_g_dt_bias, m_g_norm_w, m_g_out_w, m_xa_q, m_xa_kv, m_xa_o, m_f_up, m_f_conv_w, m_f_conv_b, m_f_down, v_ln_mix, v_ln_xattn, v_ln_mem, v_ln_ffn, v_final_norm, v_m_in_w, v_m_conv_w, v_m_conv_b, v_m_dt_bias, v_m_a_log, v_m_d, v_m_norm_w, v_m_out_w, v_h_in_w, v_h_lower_bounds, v_h_norm_w, v_h_out_w, v_g_in_w, v_g_conv_w, v_g_a_log, v_g_dt_bias, v_g_norm_w, v_g_out_w, v_xa_q, v_xa_kv, v_xa_o, v_f_up, v_f_conv_w, v_f_conv_b, v_f_down):
    given = dict(x=x, mem=mem, ln_mix=ln_mix, ln_xattn=ln_xattn, ln_mem=ln_mem, ln_ffn=ln_ffn, final_norm=final_norm, m_in_w=m_in_w, m_conv_w=m_conv_w, m_conv_b=m_conv_b, m_dt_bias=m_dt_bias, m_a_log=m_a_log, m_d=m_d, m_norm_w=m_norm_w, m_out_w=m_out_w, h_in_w=h_in_w, h_lower_bounds=h_lower_bounds, h_norm_w=h_norm_w, h_out_w=h_out_w, g_in_w=g_in_w, g_conv_w=g_conv_w, g_a_log=g_a_log, g_dt_bias=g_dt_bias, g_norm_w=g_norm_w, g_out_w=g_out_w, xa_q=xa_q, xa_kv=xa_kv, xa_o=xa_o, f_up=f_up, f_conv_w=f_conv_w, f_conv_b=f_conv_b, f_down=f_down, loss_target=loss_target, m_ln_mix=m_ln_mix, m_ln_xattn=m_ln_xattn, m_ln_mem=m_ln_mem, m_ln_ffn=m_ln_ffn, m_final_norm=m_final_norm, m_m_in_w=m_m_in_w, m_m_conv_w=m_m_conv_w, m_m_conv_b=m_m_conv_b, m_m_dt_bias=m_m_dt_bias, m_m_a_log=m_m_a_log, m_m_d=m_m_d, m_m_norm_w=m_m_norm_w, m_m_out_w=m_m_out_w, m_h_in_w=m_h_in_w, m_h_lower_bounds=m_h_lower_bounds, m_h_norm_w=m_h_norm_w, m_h_out_w=m_h_out_w, m_g_in_w=m_g_in_w, m_g_conv_w=m_g_conv_w, m_g_a_log=m_g_a_log, m_g_dt_bias=m_g_dt_bias, m_g_norm_w=m_g_norm_w, m_g_out_w=m_g_out_w, m_xa_q=m_xa_q, m_xa_kv=m_xa_kv, m_xa_o=m_xa_o, m_f_up=m_f_up, m_f_conv_w=m_f_conv_w, m_f_conv_b=m_f_conv_b, m_f_down=m_f_down, v_ln_mix=v_ln_mix, v_ln_xattn=v_ln_xattn, v_ln_mem=v_ln_mem, v_ln_ffn=v_ln_ffn, v_final_norm=v_final_norm, v_m_in_w=v_m_in_w, v_m_conv_w=v_m_conv_w, v_m_conv_b=v_m_conv_b, v_m_dt_bias=v_m_dt_bias, v_m_a_log=v_m_a_log, v_m_d=v_m_d, v_m_norm_w=v_m_norm_w, v_m_out_w=v_m_out_w, v_h_in_w=v_h_in_w, v_h_lower_bounds=v_h_lower_bounds, v_h_norm_w=v_h_norm_w, v_h_out_w=v_h_out_w, v_g_in_w=v_g_in_w, v_g_conv_w=v_g_conv_w, v_g_a_log=v_g_a_log, v_g_dt_bias=v_g_dt_bias, v_g_norm_w=v_g_norm_w, v_g_out_w=v_g_out_w, v_xa_q=v_xa_q, v_xa_kv=v_xa_kv, v_xa_o=v_xa_o, v_f_up=v_f_up, v_f_conv_w=v_f_conv_w, v_f_conv_b=v_f_conv_b, v_f_down=v_f_down)
    weights = {n: given[n] for n in TWIN_WEIGHTS}
    shared = {n: given[n] for n in SHARED_INPUTS}
    per_example = {n: given[n] for n in ['x', 'mem']}
    grad_fn = _jax.value_and_grad(_loss, argnums=(0, 1))

    def one_microbatch(ex, loss_target):
        ex = dict(ex)
        diff = ex.pop(TWIN_DIFF_INPUT)
        return grad_fn(weights, diff, {**shared, **ex}, loss_target)

    if N_MICROBATCH == 1:
        loss, (grad_w, grad_x) = one_microbatch(per_example, given["loss_target"])
    else:
        def body(carry, xs):
            loss_sum, grad_sum = carry
            l_k, (gw_k, gx_k) = one_microbatch(xs[0], xs[1])
            with _jax.named_scope("update"):
                return (loss_sum + l_k, _jax.tree.map(_jnp.add, grad_sum, gw_k)), gx_k

        init = (_jnp.zeros((), _jnp.float32), _jax.tree.map(_jnp.zeros_like, weights))
        (loss, grad_w), grad_x = _jax.lax.scan(body, init, (per_example, given["loss_target"]))
    with _jax.named_scope("update"):
        delta_w, new_m, new_v = {}, {}, {}
        for n in TWIN_WEIGHTS:
            delta_w[n], new_m[n], new_v[n] = _adamw(weights[n], grad_w[n], given["m_" + n], given["v_" + n])
    return (loss, grad_x, *[grad_w[n] for n in TWIN_WEIGHTS], *[delta_w[n] for n in TWIN_WEIGHTS],
            *[new_m[n] for n in TWIN_WEIGHTS], *[new_v[n] for n in TWIN_WEIGHTS])
```

```python
import functools
import math

import jax
import jax.numpy as jnp
from jax import lax
from jax.experimental import pallas as pl
from jax.experimental.pallas import tpu as pltpu

F32 = jnp.float32
BF16 = jnp.bfloat16
HIGHEST = lax.Precision.HIGHEST
NN = (((1,), (0,)), ((), ()))
NT = (((1,), (1,)), ((), ()))
TN = (((0,), (0,)), ((), ()))

D_MODEL = 1024
DEPTH = 4
EPS = 1e-6
N_MEM = 256
M_D_INNER = 2048
M_HEADS = 32
M_GROUPS = 8
M_STATE = 128
M_CONV_DIM = 4096
M_IN = 6176
M_IN_PAD = 6272
SSD_CHUNK = 64
H_HEADS = 8
HGRN_CHUNK = 32
HGRN_ROWS = 128
G_QK_HEADS = 8
G_V_HEADS = 16
G_KEY_DIM = 1024
G_VAL_DIM = 2048
G_CONV_DIM = 4096
G_IN = 6176
G_IN_PAD = 6272
GDN_CHUNK = 64
X_HEADS = 4
X_HEAD_DIM = 256
D_FF = 2816
ADAM_LR = 0.001
ADAM_B1 = 0.9
ADAM_B2 = 0.999
ADAM_EPS = 1e-08
ADAM_WD = 0.01
ADAM_STEP = 10

N_DEV = 8
LANE = 128
VMEM_LIMIT = 56 * 1024 * 1024

WEIGHTS = ['ln_mix', 'ln_xattn', 'ln_mem', 'ln_ffn', 'final_norm', 'm_in_w', 'm_conv_w', 'm_conv_b', 'm_dt_bias',
           'm_a_log', 'm_d', 'm_norm_w', 'm_out_w', 'h_in_w', 'h_lower_bounds', 'h_norm_w', 'h_out_w', 'g_in_w',
           'g_conv_w', 'g_a_log', 'g_dt_bias', 'g_norm_w', 'g_out_w', 'xa_q', 'xa_kv', 'xa_o', 'f_up', 'f_conv_w',
           'f_conv_b', 'f_down']
SHARD_AXIS = {'m_in_w': 2, 'm_conv_w': 2, 'm_conv_b': 1, 'm_norm_w': 1, 'm_out_w': 1, 'h_in_w': 2, 'h_out_w': 1,
              'g_in_w': 2, 'g_conv_w': 2, 'g_out_w': 1, 'xa_q': 1, 'xa_kv': 2, 'xa_o': 1, 'f_up': 2, 'f_conv_w': 2,
              'f_down': 1}
MATMUL_WEIGHTS = ['m_in_w', 'm_out_w', 'h_in_w', 'h_out_w', 'g_in_w', 'g_out_w', 'xa_q', 'xa_kv', 'xa_o', 'f_up',
                  'f_down']
SMALL_SHARDED = ['m_conv_w', 'm_conv_b', 'm_norm_w', 'g_conv_w', 'f_conv_w']


def _cparams():
    return pltpu.CompilerParams(vmem_limit_bytes=VMEM_LIMIT)


def bdot(a, b, dims=NN):
    return lax.dot_general(a.astype(BF16), b.astype(BF16), dims, preferred_element_type=F32)


def hdot(a, b, dims=NN):
    return lax.dot_general(a, b, dims, precision=HIGHEST, preferred_element_type=F32)


def _tile(dim, cap):
    if dim <= cap:
        return dim
    best = None
    for t in range(LANE, cap + 1, LANE):
        if dim % t == 0:
            best = t
    assert best is not None, dim
    return best


def matmul(a, b, *, ta=False, tb=False, out_dtype=F32, name="mm"):
    if ta:
        k, m = a.shape
    else:
        m, k = a.shape
    if tb:
        n, k2 = b.shape
    else:
        k2, n = b.shape
    assert k == k2, (a.shape, b.shape, ta, tb)
    tm = _tile(m, 512)
    tn = _tile(n, 1408)
    tk = _tile(k, 1408)
    nk = k // tk
    dims = (((0 if ta else 1,), (1 if tb else 0,)), ((), ()))

    def body(a_ref, b_ref, o_ref, acc_ref):
        @pl.when(pl.program_id(2) == 0)
        def _():
            acc_ref[...] = jnp.zeros_like(acc_ref)

        acc_ref[...] += lax.dot_general(a_ref[...].astype(BF16), b_ref[...].astype(BF16), dims,
                                        preferred_element_type=F32)

        @pl.when(pl.program_id(2) == nk - 1)
        def _():
            o_ref[...] = acc_ref[...].astype(o_ref.dtype)

    a_spec = pl.BlockSpec((tk, tm), lambda i, j, kk: (kk, i)) if ta else pl.BlockSpec((tm, tk), lambda i, j, kk: (i, kk))
    b_spec = pl.BlockSpec((tn, tk), lambda i, j, kk: (j, kk)) if tb else pl.BlockSpec((tk, tn), lambda i, j, kk: (kk, j))
    return pl.pallas_call(
        body, name=name, grid=(m // tm, n // tn, nk),
        in_specs=[a_spec, b_spec],
        out_specs=pl.BlockSpec((tm, tn), lambda i, j, kk: (i, j)),
        out_shape=jax.ShapeDtypeStruct((m, n), out_dtype),
        scratch_shapes=[pltpu.VMEM((tm, tn), F32)],
        compiler_params=_cparams(),
    )(a, b)


def make_linear(name):
    @jax.custom_vjp
    def linear(a, w, wg):
        return matmul(a, w, name=name + "_fwd")

    def fwd(a, w, wg):
        return matmul(a, w, name=name + "_fwd"), (a, w)

    def bwd(res, dy):
        a, w = res
        da = matmul(dy, w, tb=True, out_dtype=a.dtype, name=name + "_bwd_da")
        dw = matmul(a, dy, ta=True, name=name + "_bwd_dw")
        return da, jnp.zeros_like(w), dw

    linear.defvjp(fwd, bwd)
    return linear


class In:
    def __init__(self, block, imap, kind='blk', inner=(), cols=None):
        self.block, self.imap, self.kind, self.inner, self.cols = block, imap, kind, inner, cols


class Out:
    def __init__(self, shape, dtype, block, imap):
        self.shape, self.dtype, self.block, self.imap = shape, dtype, block, imap


def make_op(name, fn, grid, ins, outs, state_shape=None, seq_axis=None):
    n_in, n_out = len(ins), len(outs)
    has_state = state_shape is not None
    nd = len(grid)
    diff_idx = [i for i, s in enumerate(ins) if s.kind != 'const']

    def in_spec(s, reverse):
        off = 0
        if s.cols is not None:
            assert s.cols[0] % s.block[-1] == 0
            off = s.cols[0] // s.block[-1]

        def imap(*ids):
            ids = rev(ids) if reverse else ids
            idx = tuple(s.imap(*ids))
            return idx[:-1] + (idx[-1] + off,) if off else idx

        return pl.BlockSpec(s.block, imap)

    def rel_spec(block, f, reverse):
        return pl.BlockSpec(block, (lambda *ids: f(*rev(ids))) if reverse else f)

    def rev(ids):
        if not has_state:
            return ids
        ids = list(ids)
        ids[seq_axis] = grid[seq_axis] - 1 - ids[seq_axis]
        return tuple(ids)

    save_shape = tuple(grid) + tuple(state_shape) if has_state else None
    save_block = (None,) * nd + tuple(state_shape) if has_state else None

    def save_imap(*ids):
        return tuple(ids) + (0,) * len(state_shape)

    def fwd_call(*xs):
        def body(*refs):
            in_refs = refs[:n_in]
            out_refs = refs[n_in:n_in + n_out]
            vals = [r[...] for r in in_refs]
            if has_state:
                save_ref, st_ref = refs[n_in + n_out], refs[n_in + n_out + 1]

                @pl.when(pl.program_id(seq_axis) == 0)
                def _():
                    st_ref[...] = jnp.zeros(state_shape, F32)

                st = st_ref[...]
                save_ref[...] = st
                res = fn(*vals, st)
                st_ref[...] = res[-1]
                res = res[:-1]
            else:
                res = fn(*vals)
            for o, v in zip(out_refs, res):
                o[...] = v.astype(o.dtype)

        out_shape = [jax.ShapeDtypeStruct(o.shape, o.dtype) for o in outs]
        out_specs = [pl.BlockSpec(o.block, o.imap) for o in outs]
        scratch = []
        if has_state:
            out_shape.append(jax.ShapeDtypeStruct(save_shape, F32))
            out_specs.append(pl.BlockSpec(save_block, save_imap))
            scratch.append(pltpu.VMEM(state_shape, F32))
        return pl.pallas_call(
            body, name=name + "_fwd", grid=grid,
            in_specs=[in_spec(s, False) for s in ins],
            out_specs=out_specs, out_shape=out_shape, scratch_shapes=scratch,
            compiler_params=_cparams(),
        )(*xs)

    def grad_shape(s, x):
        if s.cols is not None:
            return (x.shape[0], s.cols[1])
        return x.shape

    def bwd_call(xs, save, cts):
        n_diff = len(diff_idx)

        def body(*refs):
            in_refs = refs[:n_in]
            p = n_in
            if has_state:
                save_ref = refs[p]
                p += 1
            ct_refs = refs[p:p + n_out]
            p += n_out
            g_refs = refs[p:p + n_diff]
            p += n_diff
            vals = [r[...] for r in in_refs]

            def g(*dv):
                full = list(vals)
                for i, v in zip(diff_idx, dv):
                    full[i] = v
                if has_state:
                    return tuple(fn(*full, dv[-1]))
                return tuple(fn(*full))

            prim = [vals[i] for i in diff_idx]
            ct = tuple(r[...].astype(F32) for r in ct_refs)
            if has_state:
                dst_ref = refs[p]

                @pl.when(pl.program_id(seq_axis) == 0)
                def _():
                    dst_ref[...] = jnp.zeros(state_shape, F32)

                prim = prim + [save_ref[...]]
                ct = ct + (dst_ref[...],)
            _, vjp = jax.vjp(g, *prim)
            grads = vjp(ct)
            for k, i in enumerate(diff_idx):
                s = ins[i]
                if s.kind == 'blk':
                    g_refs[k][...] = grads[k].astype(g_refs[k].dtype)
                else:
                    first = None
                    for ax in s.inner:
                        c = pl.program_id(ax) == 0
                        first = c if first is None else jnp.logical_and(first, c)

                    @pl.when(first)
                    def _(k=k):
                        g_refs[k][...] = jnp.zeros_like(g_refs[k])

                    g_refs[k][...] += grads[k].astype(g_refs[k].dtype)
            if has_state:
                dst_ref[...] = grads[-1]

        in_specs = [in_spec(s, True) for s in ins]
        args = list(xs)
        if has_state:
            in_specs.append(rel_spec(save_block, save_imap, True))
            args.append(save)
        for o, c in zip(outs, cts):
            in_specs.append(rel_spec(o.block, o.imap, True))
            args.append(c)
        out_shape, out_specs = [], []
        for i in diff_idx:
            s = ins[i]
            out_shape.append(jax.ShapeDtypeStruct(grad_shape(s, xs[i]), xs[i].dtype))
            out_specs.append(rel_spec(s.block, s.imap, True))
        scratch = [pltpu.VMEM(state_shape, F32)] if has_state else []
        return pl.pallas_call(
            body, name=name + "_bwd", grid=grid,
            in_specs=in_specs, out_specs=out_specs, out_shape=out_shape, scratch_shapes=scratch,
            compiler_params=_cparams(),
        )(*args)

    @jax.custom_vjp
    def op(*xs):
        return tuple(fwd_call(*xs)[:n_out])

    def op_fwd(*xs):
        res = fwd_call(*xs)
        return tuple(res[:n_out]), (xs, res[n_out] if has_state else None)

    def op_bwd(resid, cts):
        xs, save = resid
        grads = bwd_call(xs, save, cts)
        out = []
        k = 0
        for i, s in enumerate(ins):
            if s.kind == 'const':
                out.append(jnp.zeros_like(xs[i]))
                continue
            g = grads[k]
            k += 1
            if s.cols is not None:
                g = jnp.pad(g, ((0, 0), (s.cols[0], xs[i].shape[1] - s.cols[0] - s.cols[1])))
            out.append(g)
        return tuple(out)

    op.defvjp(op_fwd, op_bwd)
    return op


def _rms(x, w):
    return x * lax.rsqrt(jnp.mean(x * x, axis=-1, keepdims=True) + EPS) * w


def _silu(x):
    return x * jax.nn.sigmoid(x)


def rmsnorm_op(name, t, out_dtype):
    tm = _tile(t, 512)
    return make_op(
        name, lambda x, w: (_rms(x, w),), (t // tm,),
        [In((tm, D_MODEL), lambda i: (i, 0)), In((1, D_MODEL), lambda i: (0, 0), 'acc', (0,))],
        [Out((t, D_MODEL), out_dtype, (tm, D_MODEL), lambda i: (i, 0))])


def _tri(q):
    ii = lax.broadcasted_iota(jnp.int32, (q, q), 0)
    jj = lax.broadcasted_iota(jnp.int32, (q, q), 1)
    return ii >= jj, ii > jj


def _ssd_fn(z, x, bm, cm, dtr, dtb, alog, dsk, nw, state):
    q = x.shape[0]
    incl, _ = _tri(q)
    tril = incl.astype(F32)
    dt = jax.nn.softplus(dtr + dtb)
    da = dt * (-jnp.exp(alog))
    acum = hdot(tril, da)
    acum_t = hdot(da, tril, (((0,), (1,)), ((), ())))
    cb = bdot(cm, bm, NT)
    ys, sts = [], []
    for r in range(4):
        col = acum[:, r:r + 1]
        row = acum_t[r:r + 1, :]
        decay = jnp.exp(jnp.where(incl, col - row, -jnp.inf))
        xr = x[:, 64 * r:64 * r + 64]
        xc = xr * dt[:, r:r + 1]
        st = state[64 * r:64 * r + 64, :]
        y = bdot(cb * decay, xc) + bdot(cm, st, NT) * jnp.exp(col)
        last = acum[q - 1:q, r:r + 1]
        ds = bdot(xc * jnp.exp(last - col), bm, TN)
        sts.append(st * jnp.exp(last) + ds)
        ys.append(y + dsk[:, r:r + 1] * xr)
    y = jnp.concatenate(ys, axis=1)
    yz = y * _silu(z)
    return _rms(yz, nw), jnp.concatenate(sts, axis=0)


def ssd_op(name, bl, seq):
    q = SSD_CHUNK
    nc = seq // q
    t = bl * seq
    row = lambda g, b, n: b * nc + n
    small = lambda g, b, n: (g, 0, 0)
    ins = [
        In((q, 256), lambda g, b, n: (row(g, b, n), g), cols=(0, M_D_INNER)),
        In((q, 256), lambda g, b, n: (row(g, b, n), g), cols=(0, M_D_INNER)),
        In((q, 128), lambda g, b, n: (row(g, b, n), g), cols=(M_D_INNER, 1024)),
        In((q, 128), lambda g, b, n: (row(g, b, n), g), cols=(M_D_INNER + 1024, 1024)),
        In((None, q, 4), lambda g, b, n: (g, row(g, b, n), 0)),
        In((None, 1, 4), small, 'acc', (1, 2)),
        In((None, 1, 4), small, 'acc', (1, 2)),
        In((None, 1, 4), small, 'acc', (1, 2)),
        In((None, 1, 256), small, 'acc', (1, 2)),
    ]
    outs = [Out((t, M_D_INNER), F32, (q, 256), lambda g, b, n: (row(g, b, n), g))]
    return make_op(name, _ssd_fn, (M_GROUPS, bl, nc), ins, outs, state_shape=(256, 128), seq_axis=2)


def _gla_fn(layer, qr, fr, ir, gr, lbp, nw, state_t):
    rows = qr.shape[0]
    c = HGRN_CHUNK
    e = jnp.exp(lbp - jnp.max(lbp, axis=0, keepdims=True))
    sm = e / jnp.sum(e, axis=0, keepdims=True)
    lb = jnp.sum(sm[1:layer + 1, :], axis=0, keepdims=True) if layer > 0 else jnp.zeros((1, lbp.shape[1]), F32)
    qq = _silu(qr) * (128 ** -0.5)
    forget = lb + (1.0 - lb) * jax.nn.sigmoid(fr)
    kk = 1.0 - forget
    logf = jnp.log(forget)
    incl, _ = _tri(c)
    tril = incl.astype(F32)
    os_ = []
    for j in range(rows // c):
        sl = slice(c * j, c * j + c)
        gc = hdot(tril, logf[sl])
        glast = gc[c - 1:c, :]
        q_dec = qq[sl] * jnp.exp(gc)
        k_inv = kk[sl] * jnp.exp(-gc)
        k_end = kk[sl] * jnp.exp(glast - gc)
        att = jnp.where(incl, bdot(q_dec, k_inv, NT), 0.0)
        os_.append(bdot(att, ir[sl]) + bdot(q_dec, state_t, NT))
        state_t = state_t * jnp.exp(glast) + bdot(ir[sl], k_end, TN)
    o = jnp.concatenate(os_, axis=0)
    return _rms(o, nw) * _silu(gr), state_t


def gla_op(name, layer, bl, seq):
    r = HGRN_ROWS
    ns = seq // r
    t = bl * seq
    row = lambda h, b, n: b * ns + n
    blk = lambda k: In((r, 128), lambda h, b, n: (row(h, b, n), h), cols=(1024 * k, 1024))
    ins = [blk(0), blk(1), blk(2), blk(3),
           In((DEPTH, 128), lambda h, b, n: (0, h), 'acc', (1, 2)),
           In((1, 128), lambda h, b, n: (0, 0), 'acc', (0, 1, 2))]
    outs = [Out((t, D_MODEL), F32, (r, 128), lambda h, b, n: (row(h, b, n), h))]
    return make_op(name, functools.partial(_gla_fn, layer), (H_HEADS, bl, ns), ins, outs,
                   state_shape=(128, 128), seq_axis=2)


def _unit_lower_inverse(m):
    q = m.shape[0]
    ii = lax.broadcasted_iota(jnp.int32, (q, q), 0)
    jj = lax.broadcasted_iota(jnp.int32, (q, q), 1)
    eye = (ii == jj).astype(F32)
    p = -m
    inv = eye + p
    steps = int(math.log2(q)) - 1
    for _ in range(steps):
        p = hdot(p, p)
        inv = inv + hdot(inv, p)
    return inv


def _gdn_fn(qc, kc, vc, zc, br, ar, alog, dtb, nw, state):
    q = qc.shape[0]
    incl, strict = _tri(q)
    tril = incl.astype(F32)
    qn = qc * lax.rsqrt(jnp.sum(qc * qc, axis=-1, keepdims=True) + EPS) * (128 ** -0.5)
    kn = kc * lax.rsqrt(jnp.sum(kc * kc, axis=-1, keepdims=True) + EPS)
    beta = jax.nn.sigmoid(br)
    g = -jnp.exp(alog) * jax.nn.softplus(ar + dtb)
    gc = hdot(tril, g)
    gc_t = hdot(g, tril, (((0,), (1,)), ((), ())))
    qk = bdot(qn, kn, NT)
    os_, sts = [], []
    for j in range(2):
        col = gc[:, j:j + 1]
        row = gc_t[j:j + 1, :]
        decay = jnp.exp(jnp.where(incl, col - row, -jnp.inf))
        bj = beta[:, j:j + 1]
        kb = kn * bj
        m = jnp.where(strict, bdot(kb, kn, NT) * decay, 0.0)
        inv = _unit_lower_inverse(m)
        vj = vc[:, 128 * j:128 * j + 128]
        u = hdot(inv, vj * bj)
        w = hdot(inv, kb * jnp.exp(col))
        att = qk * decay
        q_dec = qn * jnp.exp(col)
        glast = gc[q - 1:q, j:j + 1]
        k_end = kn * jnp.exp(glast - col)
        st = state[128 * j:128 * j + 128, :]
        v_new = u - bdot(w, st)
        o = bdot(q_dec, st) + bdot(att, v_new)
        sts.append(st * jnp.exp(glast) + bdot(k_end, v_new, TN))
        os_.append(_rms(o, nw) * _silu(zc[:, 128 * j:128 * j + 128]))
    return jnp.concatenate(os_, axis=1), jnp.concatenate(sts, axis=0)


def gdn_op(name, bl, seq):
    q = GDN_CHUNK
    nc = seq // q
    t = bl * seq
    row = lambda h, b, n: b * nc + n
    small = lambda h, b, n: (h, 0, 0)
    ins = [
        In((q, 128), lambda h, b, n: (row(h, b, n), h), cols=(0, G_KEY_DIM)),
        In((q, 128), lambda h, b, n: (row(h, b, n), h), cols=(G_KEY_DIM, G_KEY_DIM)),
        In((q, 256), lambda h, b, n: (row(h, b, n), h), cols=(2 * G_KEY_DIM, G_VAL_DIM)),
        In((q, 256), lambda h, b, n: (row(h, b, n), h), cols=(G_CONV_DIM, G_VAL_DIM)),
        In((None, q, 2), lambda h, b, n: (h, row(h, b, n), 0)),
        In((None, q, 2), lambda h, b, n: (h, row(h, b, n), 0)),
        In((None, 1, 2), small, 'acc', (1, 2)),
        In((None, 1, 2), small, 'acc', (1, 2)),
        In((1, 128), lambda h, b, n: (0, 0), 'acc', (0, 1, 2)),
    ]
    outs = [Out((t, G_VAL_DIM), F32, (q, 256), lambda h, b, n: (row(h, b, n), h))]
    return make_op(name, _gdn_fn, (G_QK_HEADS, bl, nc), ins, outs, state_shape=(256, 128), seq_axis=2)


def _xattn_fn(q, k, v):
    s = bdot(q, k, NT) * (X_HEAD_DIM ** -0.5)
    s = s - jnp.max(s, axis=-1, keepdims=True)
    p = jnp.exp(s)
    p = p / jnp.sum(p, axis=-1, keepdims=True)
    return (bdot(p, v),)


def xattn_op(name, bl, seq):
    tq = _tile(seq, 512)
    nq = seq // tq
    t = bl * seq
    ins = [
        In((tq, X_HEAD_DIM), lambda b, h, i: (b * nq + i, h)),
        In((N_MEM, X_HEAD_DIM), lambda b, h, i: (b, h), 'acc', (2,), cols=(0, D_MODEL)),
        In((N_MEM, X_HEAD_DIM), lambda b, h, i: (b, h), 'acc', (2,), cols=(D_MODEL, D_MODEL)),
    ]
    outs = [Out((t, D_MODEL), F32, (tq, X_HEAD_DIM), lambda b, h, i: (b * nq + i, h))]
    return make_op(name, _xattn_fn, (bl, X_HEADS, nq), ins, outs)


CONV_PAD = 8


def make_conv(name, bl, seq, width, ch, x_col0, up_col0=None):
    cb = 256
    assert ch % cb == 0 and x_col0 % cb == 0 and (up_col0 is None or up_col0 % cb == 0)
    nb = ch // cb
    t = bl * seq
    has_up = up_col0 is not None
    grid = (nb, bl)
    x_spec = pl.BlockSpec((seq, cb), lambda c, b: (b, x_col0 // cb + c))
    up_specs = [pl.BlockSpec((seq, cb), lambda c, b: (b, up_col0 // cb + c))] if has_up else []
    w_spec = pl.BlockSpec((width, cb), lambda c, b: (0, c))
    b_spec = pl.BlockSpec((1, cb), lambda c, b: (0, c))
    o_spec = pl.BlockSpec((seq, cb), lambda c, b: (b, c))
    taps = [CONV_PAD - (width - 1) + j for j in range(width)]

    def pre_activation(x_ref, w_ref, b_ref, pad_ref):
        pad_ref[0:CONV_PAD, :] = jnp.zeros((CONV_PAD, cb), F32)
        pad_ref[CONV_PAD:CONV_PAD + seq, :] = x_ref[...]
        w = w_ref[...]
        y = b_ref[...] + jnp.zeros((seq, cb), F32)
        for j in range(width):
            y = y + w[j:j + 1, :] * pad_ref[pl.ds(taps[j], seq), :]
        return y

    def fwd_call(x, w, b):
        def body(*refs):
            x_ref, w_ref, b_ref = refs[:3]
            o_ref, pad_ref = refs[-2:]
            y = _silu(pre_activation(x_ref, w_ref, b_ref, pad_ref))
            if has_up:
                y = y * refs[3][...]
            o_ref[...] = y

        return pl.pallas_call(
            body, name=name + "_fwd", grid=grid,
            in_specs=[x_spec, w_spec, b_spec] + up_specs, out_specs=o_spec,
            out_shape=jax.ShapeDtypeStruct((t, ch), F32),
            scratch_shapes=[pltpu.VMEM((seq + CONV_PAD, cb), F32)],
            compiler_params=_cparams(),
        )(*([x, w, b] + ([x] if has_up else [])))

    def bwd_call(x, w, b, do):
        n_in = 4 + (1 if has_up else 0)

        def body(*refs):
            x_ref, w_ref, b_ref = refs[:3]
            do_ref = refs[n_in - 1]
            dx_ref, dw_ref, db_ref = refs[n_in:n_in + 3]
            pad_ref, gpad_ref = refs[-2:]
            y = pre_activation(x_ref, w_ref, b_ref, pad_ref)
            s = jax.nn.sigmoid(y)
            act = y * s
            do = do_ref[...]
            if has_up:
                refs[n_in + 3][...] = do * act
                do = do * refs[3][...]
            dy = do * (s + act * (1.0 - s))
            gpad_ref[0:seq, :] = dy
            gpad_ref[seq:seq + CONV_PAD, :] = jnp.zeros((CONV_PAD, cb), F32)
            w_ = w_ref[...]
            dx = jnp.zeros((seq, cb), F32)
            dws = []
            for j in range(width):
                dx = dx + w_[j:j + 1, :] * gpad_ref[pl.ds(width - 1 - j, seq), :]
                dws.append(jnp.sum(dy * pad_ref[pl.ds(taps[j], seq), :], axis=0, keepdims=True))
            dx_ref[...] = dx

            @pl.when(pl.program_id(1) == 0)
            def _():
                dw_ref[...] = jnp.zeros_like(dw_ref)
                db_ref[...] = jnp.zeros_like(db_ref)

            dw_ref[...] += jnp.concatenate(dws, axis=0)
            db_ref[...] += jnp.sum(dy, axis=0, keepdims=True)

        big = jax.ShapeDtypeStruct((t, ch), F32)
        return pl.pallas_call(
            body, name=name + "_bwd", grid=grid,
            in_specs=[x_spec, w_spec, b_spec] + up_specs + [o_spec],
            out_specs=[o_spec, w_spec, b_spec] + ([o_spec] if has_up else []),
            out_shape=[big, jax.ShapeDtypeStruct((width, ch), F32), jax.ShapeDtypeStruct((1, ch), F32)]
            + ([big] if has_up else []),
            scratch_shapes=[pltpu.VMEM((seq + CONV_PAD, cb), F32), pltpu.VMEM((seq + CONV_PAD, cb), F32)],
            compiler_params=_cparams(),
        )(*([x, w, b] + ([x] if has_up else []) + [do]))

    @jax.custom_vjp
    def conv(x, w, b):
        return fwd_call(x, w, b)

    def conv_fwd(x, w, b):
        return fwd_call(x, w, b), (x, w, b)

    def conv_bwd(res, do):
        x, w, b = res
        got = bwd_call(x, w, b, do)
        dx = jnp.pad(got[0], ((0, 0), (x_col0, x.shape[1] - x_col0 - ch)))
        if has_up:
            dx = dx + jnp.pad(got[3], ((0, 0), (up_col0, x.shape[1] - up_col0 - ch)))
        return dx, got[1], got[2]

    conv.defvjp(conv_fwd, conv_bwd)

    def apply(x, w, b=None):
        if b is None:
            b = jnp.zeros((ch,), F32)
        return conv(x, w, b.reshape(1, ch))

    return apply


def loss_head(x, w, target):
    t = x.shape[0]
    tm = _tile(t, 512)

    def fn(xb, wb, tb):
        err = _rms(xb, wb) - tb
        return 0.5 * jnp.sum(err * err) * (1.0 / D_MODEL)

    def body(x_ref, w_ref, t_ref, loss_ref, dx_ref, dw_ref):
        @pl.when(pl.program_id(0) == 0)
        def _():
            loss_ref[...] = jnp.zeros_like(loss_ref)
            dw_ref[...] = jnp.zeros_like(dw_ref)

        tb = t_ref[...]
        val, vjp = jax.vjp(lambda a, b: fn(a, b, tb), x_ref[...], w_ref[...])
        dx, dw = vjp(jnp.ones((), F32))
        dx_ref[...] = dx
        dw_ref[...] += dw
        loss_ref[...] += jnp.full(loss_ref.shape, val, F32)

    row = pl.BlockSpec((tm, D_MODEL), lambda i: (i, 0))
    vec = pl.BlockSpec((1, D_MODEL), lambda i: (0, 0))
    loss, dx, dw = pl.pallas_call(
        body, name="loss_head", grid=(t // tm,),
        in_specs=[row, vec, row],
        out_specs=[pl.BlockSpec((8, LANE), lambda i: (0, 0)), row, vec],
        out_shape=[jax.ShapeDtypeStruct((8, LANE), F32), jax.ShapeDtypeStruct((t, D_MODEL), F32),
                   jax.ShapeDtypeStruct((1, D_MODEL), F32)],
        compiler_params=_cparams(),
    )(x, w.reshape(1, D_MODEL), target)
    return loss[0, 0], dx, dw.reshape(D_MODEL)


PACK_W = 1024
PACK_ROWS = 256


def reduce_adamw(slots, w, m, v):
    r = w.shape[0]
    tr = PACK_ROWS
    assert r % tr == 0
    c1 = 1.0 - ADAM_B1 ** ADAM_STEP
    c2 = 1.0 - ADAM_B2 ** ADAM_STEP

    def body(s_ref, w_ref, m_ref, v_ref, g_ref, d_ref, nm_ref, nv_ref):
        g = s_ref[0]
        for k in range(1, N_DEV):
            g = g + s_ref[k]
        nm = ADAM_B1 * m_ref[...] + (1.0 - ADAM_B1) * g
        nv = ADAM_B2 * v_ref[...] + (1.0 - ADAM_B2) * (g * g)
        m_hat = nm / c1
        v_hat = nv / c2
        d_ref[...] = -ADAM_LR * (m_hat / (jnp.sqrt(v_hat) + ADAM_EPS) + ADAM_WD * w_ref[...])
        g_ref[...] = g
        nm_ref[...] = nm
        nv_ref[...] = nv

    blk = pl.BlockSpec((tr, PACK_W), lambda i: (i, 0))
    shp = jax.ShapeDtypeStruct((r, PACK_W), F32)
    return pl.pallas_call(
        body, name="reduce_adamw", grid=(r // tr,),
        in_specs=[pl.BlockSpec((N_DEV, tr, PACK_W), lambda i: (0, i, 0)), blk, blk, blk],
        out_specs=[blk, blk, blk, blk], out_shape=[shp, shp, shp, shp],
        compiler_params=_cparams(),
    )(slots, w, m, v)


def _position():
    return lax.axis_index("x"), lax.axis_index("y"), lax.axis_index("c")


def all_gather(block, name):
    r, wd = block.shape

    def body(x_ref, out_ref, send_sems, recv_sems, local_sem):
        x, y, c = _position()
        me, sibling = (x, y, c), (x, y, 1 - c)
        chips = [(1 - x, y), (x, 1 - y), (1 - x, 1 - y)]

        def slot(px, py, pc):
            return out_ref.at[4 * px + 2 * py + pc]

        def copy(k, owner, to, src=None):
            return pltpu.make_async_remote_copy(
                src_ref=slot(*owner) if src is None else src, dst_ref=slot(*owner),
                send_sem=send_sems.at[k], recv_sem=recv_sems.at[k],
                device_id=to, device_id_type=pl.DeviceIdType.MESH)

        mine = pltpu.make_async_copy(x_ref, slot(*me), local_sem)
        mine.start()
        first = [copy(0, me, sibling, src=x_ref)]
        first += [copy(1 + j, me, (*chip, c), src=x_ref) for j, chip in enumerate(chips)]
        for cp in first:
            cp.start()
        passed = [copy(4 + j, (*chip, c), sibling) for j, chip in enumerate(chips)]
        for j, chip in enumerate(chips):
            copy(1 + j, (*chip, c), me).wait_recv()
            passed[j].start()
        copy(0, sibling, me).wait_recv()
        for j, chip in enumerate(chips):
            copy(4 + j, (*chip, 1 - c), me).wait_recv()
        for cp in first + passed:
            cp.wait_send()
        mine.wait()

    return pl.pallas_call(
        body, name=name,
        out_shape=jax.ShapeDtypeStruct((N_DEV, r, wd), block.dtype),
        in_specs=[pl.BlockSpec(memory_space=pl.ANY)],
        out_specs=pl.BlockSpec(memory_space=pl.ANY),
        scratch_shapes=[pltpu.SemaphoreType.DMA((7,)), pltpu.SemaphoreType.DMA((7,)), pltpu.SemaphoreType.DMA],
    )(block)


def exchange_slabs(slabs, name):
    _, r, wd = slabs.shape

    def body(in_ref, out_ref, send_sems, recv_sems, local_sem):
        x, y, c = _position()
        my = 4 * x + 2 * y + c
        mine = pltpu.make_async_copy(in_ref.at[my], out_ref.at[my], local_sem)
        mine.start()
        copies = []
        for k in range(1, N_DEV):
            dx, dy, dc = (k >> 2) & 1, (k >> 1) & 1, k & 1
            px = x if dx == 0 else 1 - x
            py = y if dy == 0 else 1 - y
            pc = c if dc == 0 else 1 - c
            cp = pltpu.make_async_remote_copy(
                src_ref=in_ref.at[4 * px + 2 * py + pc], dst_ref=out_ref.at[my],
                send_sem=send_sems.at[k - 1], recv_sem=recv_sems.at[k - 1],
                device_id=(px, py, pc), device_id_type=pl.DeviceIdType.MESH)
            cp.start()
            copies.append(cp)
        for cp in copies:
            cp.wait()
        mine.wait()

    return pl.pallas_call(
        body, name=name,
        out_shape=jax.ShapeDtypeStruct(slabs.shape, slabs.dtype),
        in_specs=[pl.BlockSpec(memory_space=pl.ANY)],
        out_specs=pl.BlockSpec(memory_space=pl.ANY),
        scratch_shapes=[pltpu.SemaphoreType.DMA((7,)), pltpu.SemaphoreType.DMA((7,)), pltpu.SemaphoreType.DMA],
    )(slabs)


def _pack(arrays, dtype, row_multiple):
    flat = jnp.concatenate([a.astype(dtype).reshape(-1) for a in arrays])
    n = flat.shape[0]
    per = PACK_W * row_multiple
    total = -(-n // per) * per
    flat = jnp.pad(flat, (0, total - n))
    return flat.reshape(total // PACK_W, PACK_W)


def _unpack(flat2d, shapes, lead=()):
    flat = flat2d.reshape(lead + (-1,))
    out, off = [], 0
    for shp in shapes:
        n = math.prod(shp)
        out.append(flat[..., off:off + n].reshape(lead + tuple(shp)))
        off += n
    return out


def _full_from_gathered(g, axis):
    g = jnp.moveaxis(g, 0, axis)
    shp = list(g.shape)
    shp[axis:axis + 2] = [shp[axis] * shp[axis + 1]]
    return g.reshape(shp)


def _shards_of_full(full, axis):
    shp = list(full.shape)
    shp[axis:axis + 1] = [N_DEV, shp[axis] // N_DEV]
    return jnp.moveaxis(full.reshape(shp), axis, 0)


def _trunk(p, x, mem, bl, seq):
    t = bl * seq
    ia = ib = ic = 0

    def lin(name, a, wname, idx):
        w, wg = p[wname]
        return make_linear(name)(a, w[idx], wg[idx])

    for i in range(DEPTH):
        hn = rmsnorm_op(f"ln_mix{i}", t, F32)(x, p['ln_mix'][i:i + 1])[0]
        kind = i % 3
        if kind == 0:
            proj = lin(f"m_in{i}", hn, 'm_in_w', ia)
            xbc = make_conv(f"m_conv{i}", bl, seq, 4, M_CONV_DIM, M_D_INNER)(
                proj, p['m_conv_w'][ia], p['m_conv_b'][ia])
            dt = proj[:, M_D_INNER + M_CONV_DIM:M_IN].reshape(t, M_GROUPS, 4).transpose(1, 0, 2)
            grp = lambda a, n=4: a.reshape(M_GROUPS, 1, n)
            y = ssd_op(f"ssd{i}", bl, seq)(
                proj, xbc, xbc, xbc, dt, grp(p['m_dt_bias'][ia]), grp(p['m_a_log'][ia]), grp(p['m_d'][ia]),
                grp(p['m_norm_w'][ia], 256))[0]
            mix = lin(f"m_out{i}", y, 'm_out_w', ia)
            ia += 1
        elif kind == 1:
            proj = lin(f"h_in{i}", hn, 'h_in_w', ib)
            y = gla_op(f"gla{i}", i, bl, seq)(proj, proj, proj, proj, p['h_lower_bounds'], p['h_norm_w'][ib:ib + 1])[0]
            mix = lin(f"h_out{i}", y, 'h_out_w', ib)
            ib += 1
        else:
            proj = lin(f"g_in{i}", hn, 'g_in_w', ic)
            qkv = make_conv(f"g_conv{i}", bl, seq, 4, G_CONV_DIM, 0)(proj, p['g_conv_w'][ic])
            c0 = G_CONV_DIM + G_VAL_DIM
            braw = proj[:, c0:c0 + G_V_HEADS].reshape(t, G_QK_HEADS, 2).transpose(1, 0, 2)
            araw = proj[:, c0 + G_V_HEADS:c0 + 2 * G_V_HEADS].reshape(t, G_QK_HEADS, 2).transpose(1, 0, 2)
            grp = lambda a: a.reshape(G_QK_HEADS, 1, 2)
            y = gdn_op(f"gdn{i}", bl, seq)(
                qkv, qkv, qkv, proj, braw, araw, grp(p['g_a_log'][ic]), grp(p['g_dt_bias'][ic]),
                p['g_norm_w'][ic:ic + 1])[0]
            mix = lin(f"g_out{i}", y, 'g_out_w', ic)
            ic += 1
        x = x + mix
        hq = rmsnorm_op(f"ln_xattn{i}", t, F32)(x, p['ln_xattn'][i:i + 1])[0]
        mn = rmsnorm_op(f"ln_mem{i}", bl * N_MEM, F32)(mem, p['ln_mem'][i:i + 1])[0]
        qx = lin(f"xa_q{i}", hq, 'xa_q', i)
        kv = lin(f"xa_kv{i}", mn, 'xa_kv', i)
        ao = xattn_op(f"xattn{i}", bl, seq)(qx, kv, kv)[0]
        x = x + lin(f"xa_o{i}", ao, 'xa_o', i)
        hf = rmsnorm_op(f"ln_ffn{i}", t, F32)(x, p['ln_ffn'][i:i + 1])[0]
        up = lin(f"f_up{i}", hf, 'f_up', i)
        act = make_conv(f"f_conv{i}", bl, seq, 3, D_FF, 0, up_col0=D_FF)(up, p['f_conv_w'][i], p['f_conv_b'][i])
        x = x + lin(f"f_down{i}", act, 'f_down', i)
    return x


def _pad_cols(w, axis, to):
    pad = [(0, 0)] * w.ndim
    pad[axis] = (0, to - w.shape[axis])
    return jnp.pad(w, pad)


def kernel(x, mem, ln_mix, ln_xattn, ln_mem, ln_ffn, final_norm, m_in_w, m_conv_w, m_conv_b, m_dt_bias, m_a_log, m_d, m_norm_w, m_out_w, h_in_w, h_lower_bounds, h_norm_w, h_out_w, g_in_w, g_conv_w, g_a_log, g_dt_bias, g_norm_w, g_out_w, xa_q, xa_kv, xa_o, f_up, f_conv_w, f_conv_b, f_down, loss_target, m_ln_mix, m_ln_xattn, m_ln_mem, m_ln_ffn, m_final_norm, m_m_in_w, m_m_conv_w, m_m_conv_b, m_m_dt_bias, m_m_a_log, m_m_d, m_m_norm_w, m_m_out_w, m_h_in_w, m_h_lower_bounds, m_h_norm_w, m_h_out_w, m_g_in_w, m_g_conv_w, m_g_a_log, m_g_dt_bias, m_g_norm_w, m_g_out_w, m_xa_q, m_xa_kv, m_xa_o, m_f_up, m_f_conv_w, m_f_conv_b, m_f_down, v_ln_mix, v_ln_xattn, v_ln_mem, v_ln_ffn, v_final_norm, v_m_in_w, v_m_conv_w, v_m_conv_b, v_m_dt_bias, v_m_a_log, v_m_d, v_m_norm_w, v_m_out_w, v_h_in_w, v_h_lower_bounds, v_h_norm_w, v_h_out_w, v_g_in_w, v_g_conv_w, v_g_a_log, v_g_dt_bias, v_g_norm_w, v_g_out_w, v_xa_q, v_xa_kv, v_xa_o, v_f_up, v_f_conv_w, v_f_conv_b, v_f_down):
    local = dict(ln_mix=ln_mix, ln_xattn=ln_xattn, ln_mem=ln_mem, ln_ffn=ln_ffn, final_norm=final_norm, m_in_w=m_in_w, m_conv_w=m_conv_w, m_conv_b=m_conv_b, m_dt_bias=m_dt_bias, m_a_log=m_a_log, m_d=m_d, m_norm_w=m_norm_w, m_out_w=m_out_w, h_in_w=h_in_w, h_lower_bounds=h_lower_bounds, h_norm_w=h_norm_w, h_out_w=h_out_w, g_in_w=g_in_w, g_conv_w=g_conv_w, g_a_log=g_a_log, g_dt_bias=g_dt_bias, g_norm_w=g_norm_w, g_out_w=g_out_w, xa_q=xa_q, xa_kv=xa_kv, xa_o=xa_o, f_up=f_up, f_conv_w=f_conv_w, f_conv_b=f_conv_b, f_down=f_down)
    mom_m = dict(ln_mix=m_ln_mix, ln_xattn=m_ln_xattn, ln_mem=m_ln_mem, ln_ffn=m_ln_ffn, final_norm=m_final_norm, m_in_w=m_m_in_w, m_conv_w=m_m_conv_w, m_conv_b=m_m_conv_b, m_dt_bias=m_m_dt_bias, m_a_log=m_m_a_log, m_d=m_m_d, m_norm_w=m_m_norm_w, m_out_w=m_m_out_w, h_in_w=m_h_in_w, h_lower_bounds=m_h_lower_bounds, h_norm_w=m_h_norm_w, h_out_w=m_h_out_w, g_in_w=m_g_in_w, g_conv_w=m_g_conv_w, g_a_log=m_g_a_log, g_dt_bias=m_g_dt_bias, g_norm_w=m_g_norm_w, g_out_w=m_g_out_w, xa_q=m_xa_q, xa_kv=m_xa_kv, xa_o=m_xa_o, f_up=m_f_up, f_conv_w=m_f_conv_w, f_conv_b=m_f_conv_b, f_down=m_f_down)
    mom_v = dict(ln_mix=v_ln_mix, ln_xattn=v_ln_xattn, ln_mem=v_ln_mem, ln_ffn=v_ln_ffn, final_norm=v_final_norm, m_in_w=v_m_in_w, m_conv_w=v_m_conv_w, m_conv_b=v_m_conv_b, m_dt_bias=v_m_dt_bias, m_a_log=v_m_a_log, m_d=v_m_d, m_norm_w=v_m_norm_w, m_out_w=v_m_out_w, h_in_w=v_h_in_w, h_lower_bounds=v_h_lower_bounds, h_norm_w=v_h_norm_w, h_out_w=v_h_out_w, g_in_w=v_g_in_w, g_conv_w=v_g_conv_w, g_a_log=v_g_a_log, g_dt_bias=v_g_dt_bias, g_norm_w=v_g_norm_w, g_out_w=v_g_out_w, xa_q=v_xa_q, xa_kv=v_xa_kv, xa_o=v_xa_o, f_up=v_f_up, f_conv_w=v_f_conv_w, f_conv_b=v_f_conv_b, f_down=v_f_down)

    bl, seq, _ = x.shape
    t = bl * seq

    big = all_gather(_pack([local[n] for n in MATMUL_WEIGHTS], BF16, 16), "gather_matmul_weights")
    small = all_gather(_pack([local[n] for n in SMALL_SHARDED], F32, 8), "gather_small_weights")
    p = {n: local[n] for n in WEIGHTS if n not in SHARD_AXIS}
    for n, g in zip(MATMUL_WEIGHTS, _unpack(big, [local[n].shape for n in MATMUL_WEIGHTS], (N_DEV,))):
        p[n] = _full_from_gathered(g, SHARD_AXIS[n])
    for n, g in zip(SMALL_SHARDED, _unpack(small, [local[n].shape for n in SMALL_SHARDED], (N_DEV,))):
        p[n] = _full_from_gathered(g, SHARD_AXIS[n])
    p['m_in_w'] = _pad_cols(p['m_in_w'], 2, M_IN_PAD)
    p['g_in_w'] = _pad_cols(p['g_in_w'], 2, G_IN_PAD)

    diff = {n: (jnp.zeros(p[n].shape, F32) if n in MATMUL_WEIGHTS else p[n]) for n in WEIGHTS if n != 'final_norm'}

    def run(dp, xin):
        full = {n: ((p[n], dp[n]) if n in MATMUL_WEIGHTS else dp[n]) for n in dp}
        return _trunk(full, xin, mem.reshape(bl * N_MEM, D_MODEL), bl, seq)

    x_out, vjp = jax.vjp(run, diff, x.reshape(t, D_MODEL))
    loss_part, dx_out, d_final = loss_head(x_out, final_norm, loss_target.reshape(t, D_MODEL))
    grads, dx = vjp(dx_out)
    grads = dict(grads)
    grads['final_norm'] = d_final
    grads['m_in_w'] = grads['m_in_w'][:, :, :M_IN]
    grads['g_in_w'] = grads['g_in_w'][:, :, :G_IN]
    loss = lax.psum(loss_part, ("x", "y", "c"))

    sharded = [n for n in WEIGHTS if n in SHARD_AXIS]
    replicated = [n for n in WEIGHTS if n not in SHARD_AXIS]
    by_dest = [_shards_of_full(grads[n], SHARD_AXIS[n]).reshape(N_DEV, -1) for n in sharded]
    by_dest += [jnp.broadcast_to(grads[n].reshape(1, -1), (N_DEV, grads[n].size)) for n in replicated]
    flat = jnp.concatenate(by_dest, axis=1)
    n_flat = flat.shape[1]
    per = PACK_W * PACK_ROWS
    total = -(-n_flat // per) * per
    slabs = jnp.pad(flat, ((0, 0), (0, total - n_flat))).reshape(N_DEV, total // PACK_W, PACK_W)
    slots = exchange_slabs(slabs, "exchange_gradients")

    order = sharded + replicated
    pk = lambda d: _pack([d[n] for n in order], F32, PACK_ROWS)
    g_sum, delta, new_m, new_v = reduce_adamw(slots, pk(local), pk(mom_m), pk(mom_v))
    shapes = [local[n].shape for n in order]
    outs = {}
    for kind, buf in (('grad', g_sum), ('delta', delta), ('new_m', new_m), ('new_v', new_v)):
        for n, a in zip(order, _unpack(buf, shapes)):
            outs[kind, n] = a
    result = [loss, dx.reshape(bl, seq, D_MODEL)]
    for kind in ('grad', 'delta', 'new_m', 'new_v'):
        result += [outs[kind, n] for n in WEIGHTS]
    return tuple(result)
```

```python
import functools
import math

import jax
import jax.numpy as jnp
from jax import lax
from jax.experimental import pallas as pl
from jax.experimental.pallas import tpu as pltpu

F32 = jnp.float32
BF16 = jnp.bfloat16
HIGHEST = lax.Precision.HIGHEST
NN = (((1,), (0,)), ((), ()))
NT = (((1,), (1,)), ((), ()))
TN = (((0,), (0,)), ((), ()))

D_MODEL = 1024
DEPTH = 4
EPS = 1e-6
N_MEM = 256
M_D_INNER = 2048
M_HEADS = 32
M_GROUPS = 8
M_STATE = 128
M_CONV_DIM = 4096
M_IN = 6176
M_IN_PAD = 6272
SSD_CHUNK = 64
H_HEADS = 8
HGRN_CHUNK = 32
HGRN_ROWS = 128
G_QK_HEADS = 8
G_V_HEADS = 16
G_KEY_DIM = 1024
G_VAL_DIM = 2048
G_CONV_DIM = 4096
G_IN = 6176
G_IN_PAD = 6272
GDN_CHUNK = 64
X_HEADS = 4
X_HEAD_DIM = 256
D_FF = 2816
ADAM_LR = 0.001
ADAM_B1 = 0.9
ADAM_B2 = 0.999
ADAM_EPS = 1e-08
ADAM_WD = 0.01
ADAM_STEP = 10

N_DEV = 8
LANE = 128
KINDS = ('grad', 'delta', 'new_m', 'new_v')
VMEM_LIMIT = 56 * 1024 * 1024

WEIGHTS = ['ln_mix', 'ln_xattn', 'ln_mem', 'ln_ffn', 'final_norm', 'm_in_w', 'm_conv_w', 'm_conv_b', 'm_dt_bias',
           'm_a_log', 'm_d', 'm_norm_w', 'm_out_w', 'h_in_w', 'h_lower_bounds', 'h_norm_w', 'h_out_w', 'g_in_w',
           'g_conv_w', 'g_a_log', 'g_dt_bias', 'g_norm_w', 'g_out_w', 'xa_q', 'xa_kv', 'xa_o', 'f_up', 'f_conv_w',
           'f_conv_b', 'f_down']
SHARD_AXIS = {'m_in_w': 2, 'm_conv_w': 2, 'm_conv_b': 1, 'm_norm_w': 1, 'm_out_w': 1, 'h_in_w': 2, 'h_out_w': 1,
              'g_in_w': 2, 'g_conv_w': 2, 'g_out_w': 1, 'xa_q': 1, 'xa_kv': 2, 'xa_o': 1, 'f_up': 2, 'f_conv_w': 2,
              'f_down': 1}
MATMUL_WEIGHTS = ['m_in_w', 'm_out_w', 'h_in_w', 'h_out_w', 'g_in_w', 'g_out_w', 'xa_q', 'xa_kv', 'xa_o', 'f_up',
                  'f_down']
SMALL_SHARDED = ['m_conv_w', 'm_conv_b', 'm_norm_w', 'g_conv_w', 'f_conv_w']


def _cparams():
    return pltpu.CompilerParams(vmem_limit_bytes=VMEM_LIMIT)


def bdot(a, b, dims=NN):
    return lax.dot_general(a.astype(BF16), b.astype(BF16), dims, preferred_element_type=F32)


def _split(a):
    hi = a.astype(BF16)
    return hi, (a - hi.astype(F32)).astype(BF16)


def _h3(a, b, dims):
    ah, al = _split(a)
    bh, bl = _split(b)
    d = functools.partial(lax.dot_general, dimension_numbers=dims, preferred_element_type=F32)
    return d(ah, bh) + (d(ah, bl) + d(al, bh))


@jax.custom_vjp
def h3dot(a, b):
    return _h3(a, b, NN)


h3dot.defvjp(lambda a, b: (_h3(a, b, NN), (a, b)),
             lambda res, ct: (_h3(ct, res[1], NT), _h3(res[0], ct, TN)))

T_ROWS = (((0,), (1,)), ((), ()))


def _tri_times(tri, x, dims, tri_first):
    t = tri.astype(BF16)
    x0 = x.astype(BF16)
    r1 = x - x0.astype(F32)
    x1 = r1.astype(BF16)
    x2 = (r1 - x1.astype(F32)).astype(BF16)
    if tri_first:
        d = lambda xx: lax.dot_general(t, xx, dims, preferred_element_type=F32)
    else:
        d = lambda xx: lax.dot_general(xx, t, dims, preferred_element_type=F32)
    return d(x0) + (d(x1) + d(x2))


@jax.custom_vjp
def cumdot(tri, x):
    return _tri_times(tri, x, NN, True)


cumdot.defvjp(lambda tri, x: (_tri_times(tri, x, NN, True), tri),
              lambda tri, ct: (jnp.zeros_like(tri), _tri_times(tri, ct, TN, True)))


@jax.custom_vjp
def cumdot_t(tri, x):
    return _tri_times(tri, x, T_ROWS, False)


cumdot_t.defvjp(lambda tri, x: (_tri_times(tri, x, T_ROWS, False), tri),
                lambda tri, ct: (jnp.zeros_like(tri), _tri_times(tri, ct, T_ROWS, True)))


def _tile(dim, cap):
    if dim <= cap:
        return dim
    best = None
    for t in range(LANE, cap + 1, LANE):
        if dim % t == 0:
            best = t
    assert best is not None, dim
    return best


def matmul(a, b, *, ta=False, tb=False, out_dtype=F32, name="mm"):
    if ta:
        k, m = a.shape
    else:
        m, k = a.shape
    if tb:
        n, k2 = b.shape
    else:
        k2, n = b.shape
    assert k == k2, (a.shape, b.shape, ta, tb)
    tm = _tile(m, 512)
    tn = _tile(n, 1408)
    tk = _tile(k, 1408)
    nk = k // tk
    dims = (((0 if ta else 1,), (1 if tb else 0,)), ((), ()))

    def body(a_ref, b_ref, o_ref, acc_ref):
        @pl.when(pl.program_id(2) == 0)
        def _():
            acc_ref[...] = jnp.zeros_like(acc_ref)

        acc_ref[...] += lax.dot_general(a_ref[...].astype(BF16), b_ref[...].astype(BF16), dims,
                                        preferred_element_type=F32)

        @pl.when(pl.program_id(2) == nk - 1)
        def _():
            o_ref[...] = acc_ref[...].astype(o_ref.dtype)

    a_spec = pl.BlockSpec((tk, tm), lambda i, j, kk: (kk, i)) if ta else pl.BlockSpec((tm, tk), lambda i, j, kk: (i, kk))
    b_spec = pl.BlockSpec((tn, tk), lambda i, j, kk: (j, kk)) if tb else pl.BlockSpec((tk, tn), lambda i, j, kk: (kk, j))
    return pl.pallas_call(
        body, name=name, grid=(m // tm, n // tn, nk),
        in_specs=[a_spec, b_spec],
        out_specs=pl.BlockSpec((tm, tn), lambda i, j, kk: (i, j)),
        out_shape=jax.ShapeDtypeStruct((m, n), out_dtype),
        scratch_shapes=[pltpu.VMEM((tm, tn), F32)],
        compiler_params=_cparams(),
    )(a, b)


def make_linear(name):
    @jax.custom_vjp
    def linear(a, w, wg):
        return matmul(a, w, name=name + "_fwd")

    def fwd(a, w, wg):
        return matmul(a, w, name=name + "_fwd"), (a, w)

    def bwd(res, dy):
        a, w = res
        da = matmul(dy, w, tb=True, out_dtype=a.dtype, name=name + "_bwd_da")
        dw = matmul(a, dy, ta=True, name=name + "_bwd_dw")
        return da, jnp.zeros_like(w), dw

    linear.defvjp(fwd, bwd)
    return linear


class In:
    def __init__(self, block, imap, kind='blk', inner=(), cols=None):
        self.block, self.imap, self.kind, self.inner, self.cols = block, imap, kind, inner, cols


class Out:
    def __init__(self, shape, dtype, block, imap):
        self.shape, self.dtype, self.block, self.imap = shape, dtype, block, imap


def make_op(name, fn, grid, ins, outs, state_shape=None, seq_axis=None):
    n_in, n_out = len(ins), len(outs)
    has_state = state_shape is not None
    nd = len(grid)
    diff_idx = [i for i, s in enumerate(ins) if s.kind != 'const']

    def in_spec(s, reverse):
        off = 0
        if s.cols is not None:
            assert s.cols[0] % s.block[-1] == 0
            off = s.cols[0] // s.block[-1]

        def imap(*ids):
            ids = rev(ids) if reverse else ids
            idx = tuple(s.imap(*ids))
            return idx[:-1] + (idx[-1] + off,) if off else idx

        return pl.BlockSpec(s.block, imap)

    def rel_spec(block, f, reverse):
        return pl.BlockSpec(block, (lambda *ids: f(*rev(ids))) if reverse else f)

    def rev(ids):
        if not has_state:
            return ids
        ids = list(ids)
        ids[seq_axis] = grid[seq_axis] - 1 - ids[seq_axis]
        return tuple(ids)

    save_shape = tuple(grid) + tuple(state_shape) if has_state else None
    save_block = (None,) * nd + tuple(state_shape) if has_state else None

    def save_imap(*ids):
        return tuple(ids) + (0,) * len(state_shape)

    def fwd_call(*xs):
        def body(*refs):
            in_refs = refs[:n_in]
            out_refs = refs[n_in:n_in + n_out]
            vals = [r[...] for r in in_refs]
            if has_state:
                save_ref, st_ref = refs[n_in + n_out], refs[n_in + n_out + 1]

                @pl.when(pl.program_id(seq_axis) == 0)
                def _():
                    st_ref[...] = jnp.zeros(state_shape, F32)

                st = st_ref[...]
                save_ref[...] = st
                res = fn(*vals, st)
                st_ref[...] = res[-1]
                res = res[:-1]
            else:
                res = fn(*vals)
            for o, v in zip(out_refs, res):
                o[...] = v.astype(o.dtype)

        out_shape = [jax.ShapeDtypeStruct(o.shape, o.dtype) for o in outs]
        out_specs = [pl.BlockSpec(o.block, o.imap) for o in outs]
        scratch = []
        if has_state:
            out_shape.append(jax.ShapeDtypeStruct(save_shape, F32))
            out_specs.append(pl.BlockSpec(save_block, save_imap))
            scratch.append(pltpu.VMEM(state_shape, F32))
        return pl.pallas_call(
            body, name=name + "_fwd", grid=grid,
            in_specs=[in_spec(s, False) for s in ins],
            out_specs=out_specs, out_shape=out_shape, scratch_shapes=scratch,
            compiler_params=_cparams(),
        )(*xs)

    def grad_shape(s, x):
        if s.cols is not None:
            return (x.shape[0], s.cols[1])
        return x.shape

    def bwd_call(xs, save, cts):
        n_diff = len(diff_idx)

        def body(*refs):
            in_refs = refs[:n_in]
            p = n_in
            if has_state:
                save_ref = refs[p]
                p += 1
            ct_refs = refs[p:p + n_out]
            p += n_out
            g_refs = refs[p:p + n_diff]
            p += n_diff
            vals = [r[...] for r in in_refs]

            def g(*dv):
                full = list(vals)
                for i, v in zip(diff_idx, dv):
                    full[i] = v
                if has_state:
                    return tuple(fn(*full, dv[-1]))
                return tuple(fn(*full))

            prim = [vals[i] for i in diff_idx]
            ct = tuple(r[...].astype(F32) for r in ct_refs)
            if has_state:
                dst_ref = refs[p]

                @pl.when(pl.program_id(seq_axis) == 0)
                def _():
                    dst_ref[...] = jnp.zeros(state_shape, F32)

                prim = prim + [save_ref[...]]
                ct = ct + (dst_ref[...],)
            _, vjp = jax.vjp(g, *prim)
            grads = vjp(ct)
            for k, i in enumerate(diff_idx):
                s = ins[i]
                if s.kind == 'blk':
                    g_refs[k][...] = grads[k].astype(g_refs[k].dtype)
                else:
                    first = None
                    for ax in s.inner:
                        c = pl.program_id(ax) == 0
                        first = c if first is None else jnp.logical_and(first, c)

                    @pl.when(first)
                    def _(k=k):
                        g_refs[k][...] = jnp.zeros_like(g_refs[k])

                    g_refs[k][...] += grads[k].astype(g_refs[k].dtype)
            if has_state:
                dst_ref[...] = grads[-1]

        in_specs = [in_spec(s, True) for s in ins]
        args = list(xs)
        if has_state:
            in_specs.append(rel_spec(save_block, save_imap, True))
            args.append(save)
        for o, c in zip(outs, cts):
            in_specs.append(rel_spec(o.block, o.imap, True))
            args.append(c)
        out_shape, out_specs = [], []
        for i in diff_idx:
            s = ins[i]
            out_shape.append(jax.ShapeDtypeStruct(grad_shape(s, xs[i]), xs[i].dtype))
            out_specs.append(rel_spec(s.block, s.imap, True))
        scratch = [pltpu.VMEM(state_shape, F32)] if has_state else []
        return pl.pallas_call(
            body, name=name + "_bwd", grid=grid,
            in_specs=in_specs, out_specs=out_specs, out_shape=out_shape, scratch_shapes=scratch,
            compiler_params=_cparams(),
        )(*args)

    @jax.custom_vjp
    def op(*xs):
        return tuple(fwd_call(*xs)[:n_out])

    def op_fwd(*xs):
        res = fwd_call(*xs)
        return tuple(res[:n_out]), (xs, res[n_out] if has_state else None)

    def op_bwd(resid, cts):
        xs, save = resid
        grads = bwd_call(xs, save, cts)
        out = []
        k = 0
        for i, s in enumerate(ins):
            if s.kind == 'const':
                out.append(jnp.zeros_like(xs[i]))
                continue
            g = grads[k]
            k += 1
            if s.cols is not None:
                g = jnp.pad(g, ((0, 0), (s.cols[0], xs[i].shape[1] - s.cols[0] - s.cols[1])))
            out.append(g)
        return tuple(out)

    op.defvjp(op_fwd, op_bwd)
    return op


def _rms(x, w):
    return x * lax.rsqrt(jnp.mean(x * x, axis=-1, keepdims=True) + EPS) * w


def _silu(x):
    return x * jax.nn.sigmoid(x)


def rmsnorm_op(name, t, out_dtype):
    tm = _tile(t, 512)
    return make_op(
        name, lambda x, w: (_rms(x, w),), (t // tm,),
        [In((tm, D_MODEL), lambda i: (i, 0)), In((1, D_MODEL), lambda i: (0, 0), 'acc', (0,))],
        [Out((t, D_MODEL), out_dtype, (tm, D_MODEL), lambda i: (i, 0))])


def _tri(q):
    ii = lax.broadcasted_iota(jnp.int32, (q, q), 0)
    jj = lax.broadcasted_iota(jnp.int32, (q, q), 1)
    return ii >= jj, ii > jj


def _ssd_fn(z, x, bm, cm, dtr, dtb, alog, dsk, nw, state):
    q = x.shape[0]
    incl, _ = _tri(q)
    tril = incl.astype(F32)
    dt = jax.nn.softplus(dtr + dtb)
    da = dt * (-jnp.exp(alog))
    acum = cumdot(tril, da)
    acum_t = cumdot_t(tril, da)
    cb = bdot(cm, bm, NT)
    ys, sts = [], []
    for r in range(4):
        col = acum[:, r:r + 1]
        row = acum_t[r:r + 1, :]
        decay = jnp.exp(jnp.where(incl, col - row, -jnp.inf))
        xr = x[:, 64 * r:64 * r + 64]
        xc = xr * dt[:, r:r + 1]
        st = state[64 * r:64 * r + 64, :]
        y = bdot(cb * decay, xc) + bdot(cm, st, NT) * jnp.exp(col)
        last = acum[q - 1:q, r:r + 1]
        ds = bdot(xc * jnp.exp(last - col), bm, TN)
        sts.append(st * jnp.exp(last) + ds)
        ys.append(y + dsk[:, r:r + 1] * xr)
    y = jnp.concatenate(ys, axis=1)
    yz = y * _silu(z)
    return _rms(yz, nw), jnp.concatenate(sts, axis=0)


def ssd_op(name, bl, seq):
    q = SSD_CHUNK
    nc = seq // q
    t = bl * seq
    row = lambda g, b, n: b * nc + n
    small = lambda g, b, n: (g, 0, 0)
    ins = [
        In((q, 256), lambda g, b, n: (row(g, b, n), g), cols=(0, M_D_INNER)),
        In((q, 256), lambda g, b, n: (row(g, b, n), g), cols=(0, M_D_INNER)),
        In((q, 128), lambda g, b, n: (row(g, b, n), g), cols=(M_D_INNER, 1024)),
        In((q, 128), lambda g, b, n: (row(g, b, n), g), cols=(M_D_INNER + 1024, 1024)),
        In((None, q, 4), lambda g, b, n: (g, row(g, b, n), 0)),
        In((None, 1, 4), small, 'acc', (1, 2)),
        In((None, 1, 4), small, 'acc', (1, 2)),
        In((None, 1, 4), small, 'acc', (1, 2)),
        In((None, 1, 256), small, 'acc', (1, 2)),
    ]
    outs = [Out((t, M_D_INNER), F32, (q, 256), lambda g, b, n: (row(g, b, n), g))]
    return make_op(name, _ssd_fn, (M_GROUPS, bl, nc), ins, outs, state_shape=(256, 128), seq_axis=2)


def _gla_fn(layer, qr, fr, ir, gr, lbp, nw, state_t):
    rows = qr.shape[0]
    c = HGRN_CHUNK
    e = jnp.exp(lbp - jnp.max(lbp, axis=0, keepdims=True))
    sm = e / jnp.sum(e, axis=0, keepdims=True)
    lb = jnp.sum(sm[1:layer + 1, :], axis=0, keepdims=True) if layer > 0 else jnp.zeros((1, lbp.shape[1]), F32)
    qq = _silu(qr) * (128 ** -0.5)
    forget = lb + (1.0 - lb) * jax.nn.sigmoid(fr)
    kk = 1.0 - forget
    logf = jnp.log(forget)
    incl, _ = _tri(c)
    tril = incl.astype(F32)
    os_ = []
    for j in range(rows // c):
        sl = slice(c * j, c * j + c)
        gc = cumdot(tril, logf[sl])
        glast = gc[c - 1:c, :]
        q_dec = qq[sl] * jnp.exp(gc)
        k_inv = kk[sl] * jnp.exp(-gc)
        k_end = kk[sl] * jnp.exp(glast - gc)
        att = jnp.where(incl, bdot(q_dec, k_inv, NT), 0.0)
        os_.append(bdot(att, ir[sl]) + bdot(q_dec, state_t, NT))
        state_t = state_t * jnp.exp(glast) + bdot(ir[sl], k_end, TN)
    o = jnp.concatenate(os_, axis=0)
    return _rms(o, nw) * _silu(gr), state_t


def gla_op(name, layer, bl, seq):
    r = HGRN_ROWS
    ns = seq // r
    t = bl * seq
    row = lambda h, b, n: b * ns + n
    blk = lambda k: In((r, 128), lambda h, b, n: (row(h, b, n), h), cols=(1024 * k, 1024))
    ins = [blk(0), blk(1), blk(2), blk(3),
           In((DEPTH, 128), lambda h, b, n: (0, h), 'acc', (1, 2)),
           In((1, 128), lambda h, b, n: (0, 0), 'acc', (0, 1, 2))]
    outs = [Out((t, D_MODEL), F32, (r, 128), lambda h, b, n: (row(h, b, n), h))]
    return make_op(name, functools.partial(_gla_fn, layer), (H_HEADS, bl, ns), ins, outs,
                   state_shape=(128, 128), seq_axis=2)


def _unit_lower_inverse(m, nilpotent):
    q = m.shape[0]
    ii = lax.broadcasted_iota(jnp.int32, (q, q), 0)
    jj = lax.broadcasted_iota(jnp.int32, (q, q), 1)
    eye = (ii == jj).astype(F32)
    p = -m
    inv = eye + p
    steps = int(math.log2(nilpotent)) - 1
    for _ in range(steps):
        p = h3dot(p, p)
        inv = inv + h3dot(inv, p)
    return inv


def _gdn_fn(qc, kc, vc, zc, br, ar, alog, dtb, nw, state):
    q = qc.shape[0]
    incl, strict = _tri(q)
    tril = incl.astype(F32)
    qn = qc * lax.rsqrt(jnp.sum(qc * qc, axis=-1, keepdims=True) + EPS) * (128 ** -0.5)
    kn = kc * lax.rsqrt(jnp.sum(kc * kc, axis=-1, keepdims=True) + EPS)
    beta = jax.nn.sigmoid(br)
    g = -jnp.exp(alog) * jax.nn.softplus(ar + dtb)
    gc = cumdot(tril, g)
    gc_t = cumdot_t(tril, g)
    qk = bdot(qn, kn, NT)
    decays, ms, rhs = [], [], []
    for j in range(2):
        col = gc[:, j:j + 1]
        decay = jnp.exp(jnp.where(incl, col - gc_t[j:j + 1, :], -jnp.inf))
        bj = beta[:, j:j + 1]
        kb = kn * bj
        decays.append(decay)
        ms.append(jnp.where(strict, bdot(kb, kn, NT) * decay, 0.0))
        rhs.append(jnp.concatenate([vc[:, 128 * j:128 * j + 128] * bj, kb * jnp.exp(col)], axis=1))
    zero = jnp.zeros((q, q), F32)
    m2 = jnp.concatenate([jnp.concatenate([ms[0], zero], axis=1), jnp.concatenate([zero, ms[1]], axis=1)], axis=0)
    sol = h3dot(_unit_lower_inverse(m2, q), jnp.concatenate(rhs, axis=0))
    os_, sts = [], []
    for j in range(2):
        col = gc[:, j:j + 1]
        decay = decays[j]
        u = sol[q * j:q * j + q, :128]
        w = sol[q * j:q * j + q, 128:]
        att = qk * decay
        q_dec = qn * jnp.exp(col)
        glast = gc[q - 1:q, j:j + 1]
        k_end = kn * jnp.exp(glast - col)
        st = state[128 * j:128 * j + 128, :]
        v_new = u - bdot(w, st)
        o = bdot(q_dec, st) + bdot(att, v_new)
        sts.append(st * jnp.exp(glast) + bdot(k_end, v_new, TN))
        os_.append(_rms(o, nw) * _silu(zc[:, 128 * j:128 * j + 128]))
    return jnp.concatenate(os_, axis=1), jnp.concatenate(sts, axis=0)


def gdn_op(name, bl, seq):
    q = GDN_CHUNK
    nc = seq // q
    t = bl * seq
    row = lambda h, b, n: b * nc + n
    small = lambda h, b, n: (h, 0, 0)
    ins = [
        In((q, 128), lambda h, b, n: (row(h, b, n), h), cols=(0, G_KEY_DIM)),
        In((q, 128), lambda h, b, n: (row(h, b, n), h), cols=(G_KEY_DIM, G_KEY_DIM)),
        In((q, 256), lambda h, b, n: (row(h, b, n), h), cols=(2 * G_KEY_DIM, G_VAL_DIM)),
        In((q, 256), lambda h, b, n: (row(h, b, n), h), cols=(G_CONV_DIM, G_VAL_DIM)),
        In((None, q, 2), lambda h, b, n: (h, row(h, b, n), 0)),
        In((None, q, 2), lambda h, b, n: (h, row(h, b, n), 0)),
        In((None, 1, 2), small, 'acc', (1, 2)),
        In((None, 1, 2), small, 'acc', (1, 2)),
        In((1, 128), lambda h, b, n: (0, 0), 'acc', (0, 1, 2)),
    ]
    outs = [Out((t, G_VAL_DIM), F32, (q, 256), lambda h, b, n: (row(h, b, n), h))]
    return make_op(name, _gdn_fn, (G_QK_HEADS, bl, nc), ins, outs, state_shape=(256, 128), seq_axis=2)


def _xattn_fn(q, k, v):
    s = bdot(q, k, NT) * (X_HEAD_DIM ** -0.5)
    s = s - jnp.max(s, axis=-1, keepdims=True)
    p = jnp.exp(s)
    p = p / jnp.sum(p, axis=-1, keepdims=True)
    return (bdot(p, v),)


def xattn_op(name, bl, seq):
    tq = _tile(seq, 512)
    nq = seq // tq
    t = bl * seq
    ins = [
        In((tq, X_HEAD_DIM), lambda b, h, i: (b * nq + i, h)),
        In((N_MEM, X_HEAD_DIM), lambda b, h, i: (b, h), 'acc', (2,), cols=(0, D_MODEL)),
        In((N_MEM, X_HEAD_DIM), lambda b, h, i: (b, h), 'acc', (2,), cols=(D_MODEL, D_MODEL)),
    ]
    outs = [Out((t, D_MODEL), F32, (tq, X_HEAD_DIM), lambda b, h, i: (b * nq + i, h))]
    return make_op(name, _xattn_fn, (bl, X_HEADS, nq), ins, outs)


CONV_PAD = 8


def make_conv(name, bl, seq, width, ch, x_col0, up_col0=None):
    cb = 256
    assert ch % cb == 0 and x_col0 % cb == 0 and (up_col0 is None or up_col0 % cb == 0)
    nb = ch // cb
    t = bl * seq
    has_up = up_col0 is not None
    grid = (nb, bl)
    x_spec = pl.BlockSpec((seq, cb), lambda c, b: (b, x_col0 // cb + c))
    up_specs = [pl.BlockSpec((seq, cb), lambda c, b: (b, up_col0 // cb + c))] if has_up else []
    w_spec = pl.BlockSpec((width, cb), lambda c, b: (0, c))
    b_spec = pl.BlockSpec((1, cb), lambda c, b: (0, c))
    o_spec = pl.BlockSpec((seq, cb), lambda c, b: (b, c))
    taps = [CONV_PAD - (width - 1) + j for j in range(width)]

    def pre_activation(x_ref, w_ref, b_ref, pad_ref):
        pad_ref[0:CONV_PAD, :] = jnp.zeros((CONV_PAD, cb), F32)
        pad_ref[CONV_PAD:CONV_PAD + seq, :] = x_ref[...]
        w = w_ref[...]
        y = b_ref[...] + jnp.zeros((seq, cb), F32)
        for j in range(width):
            y = y + w[j:j + 1, :] * pad_ref[pl.ds(taps[j], seq), :]
        return y

    def fwd_call(x, w, b):
        def body(*refs):
            x_ref, w_ref, b_ref = refs[:3]
            o_ref, pad_ref = refs[-2:]
            y = _silu(pre_activation(x_ref, w_ref, b_ref, pad_ref))
            if has_up:
                y = y * refs[3][...]
            o_ref[...] = y

        return pl.pallas_call(
            body, name=name + "_fwd", grid=grid,
            in_specs=[x_spec, w_spec, b_spec] + up_specs, out_specs=o_spec,
            out_shape=jax.ShapeDtypeStruct((t, ch), F32),
            scratch_shapes=[pltpu.VMEM((seq + CONV_PAD, cb), F32)],
            compiler_params=_cparams(),
        )(*([x, w, b] + ([x] if has_up else [])))

    def bwd_call(x, w, b, do):
        n_in = 4 + (1 if has_up else 0)

        def body(*refs):
            x_ref, w_ref, b_ref = refs[:3]
            do_ref = refs[n_in - 1]
            dx_ref, dw_ref, db_ref = refs[n_in:n_in + 3]
            pad_ref, gpad_ref = refs[-2:]
            y = pre_activation(x_ref, w_ref, b_ref, pad_ref)
            s = jax.nn.sigmoid(y)
            act = y * s
            do = do_ref[...]
            if has_up:
                refs[n_in + 3][...] = do * act
                do = do * refs[3][...]
            dy = do * (s + act * (1.0 - s))
            gpad_ref[0:seq, :] = dy
            gpad_ref[seq:seq + CONV_PAD, :] = jnp.zeros((CONV_PAD, cb), F32)
            w_ = w_ref[...]
            dx = jnp.zeros((seq, cb), F32)
            dws = []
            for j in range(width):
                dx = dx + w_[j:j + 1, :] * gpad_ref[pl.ds(width - 1 - j, seq), :]
                dws.append(jnp.sum(dy * pad_ref[pl.ds(taps[j], seq), :], axis=0, keepdims=True))
            dx_ref[...] = dx

            @pl.when(pl.program_id(1) == 0)
            def _():
                dw_ref[...] = jnp.zeros_like(dw_ref)
                db_ref[...] = jnp.zeros_like(db_ref)

            dw_ref[...] += jnp.concatenate(dws, axis=0)
            db_ref[...] += jnp.sum(dy, axis=0, keepdims=True)

        big = jax.ShapeDtypeStruct((t, ch), F32)
        return pl.pallas_call(
            body, name=name + "_bwd", grid=grid,
            in_specs=[x_spec, w_spec, b_spec] + up_specs + [o_spec],
            out_specs=[o_spec, w_spec, b_spec] + ([o_spec] if has_up else []),
            out_shape=[big, jax.ShapeDtypeStruct((width, ch), F32), jax.ShapeDtypeStruct((1, ch), F32)]
            + ([big] if has_up else []),
            scratch_shapes=[pltpu.VMEM((seq + CONV_PAD, cb), F32), pltpu.VMEM((seq + CONV_PAD, cb), F32)],
            compiler_params=_cparams(),
        )(*([x, w, b] + ([x] if has_up else []) + [do]))

    @jax.custom_vjp
    def conv(x, w, b):
        return fwd_call(x, w, b)

    def conv_fwd(x, w, b):
        return fwd_call(x, w, b), (x, w, b)

    def conv_bwd(res, do):
        x, w, b = res
        got = bwd_call(x, w, b, do)
        dx = jnp.pad(got[0], ((0, 0), (x_col0, x.shape[1] - x_col0 - ch)))
        if has_up:
            dx = dx + jnp.pad(got[3], ((0, 0), (up_col0, x.shape[1] - up_col0 - ch)))
        return dx, got[1], got[2]

    conv.defvjp(conv_fwd, conv_bwd)

    def apply(x, w, b=None):
        if b is None:
            b = jnp.zeros((ch,), F32)
        return conv(x, w, b.reshape(1, ch))

    return apply


def loss_head(x, w, target):
    t = x.shape[0]
    tm = _tile(t, 512)

    def fn(xb, wb, tb):
        err = _rms(xb, wb) - tb
        return 0.5 * jnp.sum(err * err) * (1.0 / D_MODEL)

    def body(x_ref, w_ref, t_ref, loss_ref, dx_ref, dw_ref):
        @pl.when(pl.program_id(0) == 0)
        def _():
            loss_ref[...] = jnp.zeros_like(loss_ref)
            dw_ref[...] = jnp.zeros_like(dw_ref)

        tb = t_ref[...]
        val, vjp = jax.vjp(lambda a, b: fn(a, b, tb), x_ref[...], w_ref[...])
        dx, dw = vjp(jnp.ones((), F32))
        dx_ref[...] = dx
        dw_ref[...] += dw
        loss_ref[...] += jnp.full(loss_ref.shape, val, F32)

    row = pl.BlockSpec((tm, D_MODEL), lambda i: (i, 0))
    vec = pl.BlockSpec((1, D_MODEL), lambda i: (0, 0))
    loss, dx, dw = pl.pallas_call(
        body, name="loss_head", grid=(t // tm,),
        in_specs=[row, vec, row],
        out_specs=[pl.BlockSpec((8, LANE), lambda i: (0, 0)), row, vec],
        out_shape=[jax.ShapeDtypeStruct((8, LANE), F32), jax.ShapeDtypeStruct((t, D_MODEL), F32),
                   jax.ShapeDtypeStruct((1, D_MODEL), F32)],
        compiler_params=_cparams(),
    )(x, w.reshape(1, D_MODEL), target)
    return loss[0, 0], dx, dw.reshape(D_MODEL)


PACK_W = 1024
ADAM_BLOCK_BYTES = 512 * 1024


def _rows_tile(r, c):
    if r * c * 4 <= ADAM_BLOCK_BYTES or r % 8:
        return r
    best = 8
    for t in range(8, r + 1, 8):
        if r % t == 0 and t * c * 4 <= ADAM_BLOCK_BYTES:
            best = t
    return best


def reduce_adamw(slots, w, m, v, name):
    r, wd = w.shape
    tr = _rows_tile(r, wd)
    c1 = 1.0 - ADAM_B1 ** ADAM_STEP
    c2 = 1.0 - ADAM_B2 ** ADAM_STEP

    def body(s_ref, w_ref, m_ref, v_ref, g_ref, d_ref, nm_ref, nv_ref):
        g = s_ref[0]
        for k in range(1, N_DEV):
            g = g + s_ref[k]
        nm = ADAM_B1 * m_ref[...] + (1.0 - ADAM_B1) * g
        nv = ADAM_B2 * v_ref[...] + (1.0 - ADAM_B2) * (g * g)
        m_hat = nm / c1
        v_hat = nv / c2
        d_ref[...] = -ADAM_LR * (m_hat / (jnp.sqrt(v_hat) + ADAM_EPS) + ADAM_WD * w_ref[...])
        g_ref[...] = g
        nm_ref[...] = nm
        nv_ref[...] = nv

    blk = pl.BlockSpec((tr, wd), lambda i: (i, 0))
    shp = jax.ShapeDtypeStruct((r, wd), F32)
    return pl.pallas_call(
        body, name=name, grid=(r // tr,),
        in_specs=[pl.BlockSpec((N_DEV, tr, wd), lambda i: (0, i, 0)), blk, blk, blk],
        out_specs=[blk, blk, blk, blk], out_shape=[shp, shp, shp, shp],
        compiler_params=_cparams(),
    )(slots, w, m, v)


def _position():
    return lax.axis_index("x"), lax.axis_index("y"), lax.axis_index("c")


def all_gather(block, name):
    def body(x_ref, out_ref, send_sems, recv_sems, local_sem):
        x, y, c = _position()
        me, sibling = (x, y, c), (x, y, 1 - c)
        chips = [(1 - x, y), (x, 1 - y), (1 - x, 1 - y)]

        def slot(px, py, pc):
            return out_ref.at[4 * px + 2 * py + pc]

        def copy(k, owner, to, src=None):
            return pltpu.make_async_remote_copy(
                src_ref=slot(*owner) if src is None else src, dst_ref=slot(*owner),
                send_sem=send_sems.at[k], recv_sem=recv_sems.at[k],
                device_id=to, device_id_type=pl.DeviceIdType.MESH)

        mine = pltpu.make_async_copy(x_ref, slot(*me), local_sem)
        mine.start()
        first = [copy(0, me, sibling, src=x_ref)]
        first += [copy(1 + j, me, (*chip, c), src=x_ref) for j, chip in enumerate(chips)]
        for cp in first:
            cp.start()
        passed = [copy(4 + j, (*chip, c), sibling) for j, chip in enumerate(chips)]
        for j, chip in enumerate(chips):
            copy(1 + j, (*chip, c), me).wait_recv()
            passed[j].start()
        copy(0, sibling, me).wait_recv()
        for j, chip in enumerate(chips):
            copy(4 + j, (*chip, 1 - c), me).wait_recv()
        for cp in first + passed:
            cp.wait_send()
        mine.wait()

    return pl.pallas_call(
        body, name=name,
        out_shape=jax.ShapeDtypeStruct((N_DEV,) + block.shape, block.dtype),
        in_specs=[pl.BlockSpec(memory_space=pl.ANY)],
        out_specs=pl.BlockSpec(memory_space=pl.ANY),
        scratch_shapes=[pltpu.SemaphoreType.DMA((7,)), pltpu.SemaphoreType.DMA((7,)), pltpu.SemaphoreType.DMA],
    )(block)


def exchange_slabs(slabs, name):
    def body(in_ref, out_ref, send_sems, recv_sems, local_sem):
        x, y, c = _position()
        my = 4 * x + 2 * y + c
        mine = pltpu.make_async_copy(in_ref.at[my], out_ref.at[my], local_sem)
        mine.start()
        copies = []
        for k in range(1, N_DEV):
            dx, dy, dc = (k >> 2) & 1, (k >> 1) & 1, k & 1
            px = x if dx == 0 else 1 - x
            py = y if dy == 0 else 1 - y
            pc = c if dc == 0 else 1 - c
            cp = pltpu.make_async_remote_copy(
                src_ref=in_ref.at[4 * px + 2 * py + pc], dst_ref=out_ref.at[my],
                send_sem=send_sems.at[k - 1], recv_sem=recv_sems.at[k - 1],
                device_id=(px, py, pc), device_id_type=pl.DeviceIdType.MESH)
            cp.start()
            copies.append(cp)
        for cp in copies:
            cp.wait()
        mine.wait()

    return pl.pallas_call(
        body, name=name,
        out_shape=jax.ShapeDtypeStruct(slabs.shape, slabs.dtype),
        in_specs=[pl.BlockSpec(memory_space=pl.ANY)],
        out_specs=pl.BlockSpec(memory_space=pl.ANY),
        scratch_shapes=[pltpu.SemaphoreType.DMA((7,)), pltpu.SemaphoreType.DMA((7,)), pltpu.SemaphoreType.DMA],
    )(slabs)


def _pack(arrays, dtype, row_multiple):
    flat = jnp.concatenate([a.astype(dtype).reshape(-1) for a in arrays])
    n = flat.shape[0]
    per = PACK_W * row_multiple
    total = -(-n // per) * per
    flat = jnp.pad(flat, (0, total - n))
    return flat.reshape(total // PACK_W, PACK_W)


def _unpack(flat2d, shapes, lead=()):
    flat = flat2d.reshape(lead + (-1,))
    out, off = [], 0
    for shp in shapes:
        n = math.prod(shp)
        out.append(flat[..., off:off + n].reshape(lead + tuple(shp)))
        off += n
    return out


def _full_from_gathered(g, axis):
    g = jnp.moveaxis(g, 0, axis)
    shp = list(g.shape)
    shp[axis:axis + 2] = [shp[axis] * shp[axis + 1]]
    return g.reshape(shp)


def _shards_of_full(full, axis):
    shp = list(full.shape)
    shp[axis:axis + 1] = [N_DEV, shp[axis] // N_DEV]
    return jnp.moveaxis(full.reshape(shp), axis, 0)


def _trunk(p, x, mem, bl, seq):
    t = bl * seq
    ia = ib = ic = 0

    def lin(name, a, wname, idx):
        w, wg = p[wname][idx]
        return make_linear(name)(a, w, wg)

    for i in range(DEPTH):
        hn = rmsnorm_op(f"ln_mix{i}", t, F32)(x, p['ln_mix'][i:i + 1])[0]
        kind = i % 3
        if kind == 0:
            proj = lin(f"m_in{i}", hn, 'm_in_w', ia)
            xbc = make_conv(f"m_conv{i}", bl, seq, 4, M_CONV_DIM, M_D_INNER)(
                proj, p['m_conv_w'][ia], p['m_conv_b'][ia])
            dt = proj[:, M_D_INNER + M_CONV_DIM:M_IN].reshape(t, M_GROUPS, 4).transpose(1, 0, 2)
            grp = lambda a, n=4: a.reshape(M_GROUPS, 1, n)
            y = ssd_op(f"ssd{i}", bl, seq)(
                proj, xbc, xbc, xbc, dt, grp(p['m_dt_bias'][ia]), grp(p['m_a_log'][ia]), grp(p['m_d'][ia]),
                grp(p['m_norm_w'][ia], 256))[0]
            mix = lin(f"m_out{i}", y, 'm_out_w', ia)
            ia += 1
        elif kind == 1:
            proj = lin(f"h_in{i}", hn, 'h_in_w', ib)
            y = gla_op(f"gla{i}", i, bl, seq)(proj, proj, proj, proj, p['h_lower_bounds'], p['h_norm_w'][ib:ib + 1])[0]
            mix = lin(f"h_out{i}", y, 'h_out_w', ib)
            ib += 1
        else:
            proj = lin(f"g_in{i}", hn, 'g_in_w', ic)
            qkv = make_conv(f"g_conv{i}", bl, seq, 4, G_CONV_DIM, 0)(proj, p['g_conv_w'][ic])
            c0 = G_CONV_DIM + G_VAL_DIM
            braw = proj[:, c0:c0 + G_V_HEADS].reshape(t, G_QK_HEADS, 2).transpose(1, 0, 2)
            araw = proj[:, c0 + G_V_HEADS:c0 + 2 * G_V_HEADS].reshape(t, G_QK_HEADS, 2).transpose(1, 0, 2)
            grp = lambda a: a.reshape(G_QK_HEADS, 1, 2)
            y = gdn_op(f"gdn{i}", bl, seq)(
                qkv, qkv, qkv, proj, braw, araw, grp(p['g_a_log'][ic]), grp(p['g_dt_bias'][ic]),
                p['g_norm_w'][ic:ic + 1])[0]
            mix = lin(f"g_out{i}", y, 'g_out_w', ic)
            ic += 1
        x = x + mix
        hq = rmsnorm_op(f"ln_xattn{i}", t, F32)(x, p['ln_xattn'][i:i + 1])[0]
        mn = rmsnorm_op(f"ln_mem{i}", bl * N_MEM, F32)(mem, p['ln_mem'][i:i + 1])[0]
        qx = lin(f"xa_q{i}", hq, 'xa_q', i)
        kv = lin(f"xa_kv{i}", mn, 'xa_kv', i)
        ao = xattn_op(f"xattn{i}", bl, seq)(qx, kv, kv)[0]
        x = x + lin(f"xa_o{i}", ao, 'xa_o', i)
        hf = rmsnorm_op(f"ln_ffn{i}", t, F32)(x, p['ln_ffn'][i:i + 1])[0]
        up = lin(f"f_up{i}", hf, 'f_up', i)
        act = make_conv(f"f_conv{i}", bl, seq, 3, D_FF, 0, up_col0=D_FF)(up, p['f_conv_w'][i], p['f_conv_b'][i])
        x = x + lin(f"f_down{i}", act, 'f_down', i)
    return x


def _pad_cols(w, axis, to):
    pad = [(0, 0)] * w.ndim
    pad[axis] = (0, to - w.shape[axis])
    return jnp.pad(w, pad)


def kernel(x, mem, ln_mix, ln_xattn, ln_mem, ln_ffn, final_norm, m_in_w, m_conv_w, m_conv_b, m_dt_bias, m_a_log, m_d, m_norm_w, m_out_w, h_in_w, h_lower_bounds, h_norm_w, h_out_w, g_in_w, g_conv_w, g_a_log, g_dt_bias, g_norm_w, g_out_w, xa_q, xa_kv, xa_o, f_up, f_conv_w, f_conv_b, f_down, loss_target, m_ln_mix, m_ln_xattn, m_ln_mem, m_ln_ffn, m_final_norm, m_m_in_w, m_m_conv_w, m_m_conv_b, m_m_dt_bias, m_m_a_log, m_m_d, m_m_norm_w, m_m_out_w, m_h_in_w, m_h_lower_bounds, m_h_norm_w, m_h_out_w, m_g_in_w, m_g_conv_w, m_g_a_log, m_g_dt_bias, m_g_norm_w, m_g_out_w, m_xa_q, m_xa_kv, m_xa_o, m_f_up, m_f_conv_w, m_f_conv_b, m_f_down, v_ln_mix, v_ln_xattn, v_ln_mem, v_ln_ffn, v_final_norm, v_m_in_w, v_m_conv_w, v_m_conv_b, v_m_dt_bias, v_m_a_log, v_m_d, v_m_norm_w, v_m_out_w, v_h_in_w, v_h_lower_bounds, v_h_norm_w, v_h_out_w, v_g_in_w, v_g_conv_w, v_g_a_log, v_g_dt_bias, v_g_norm_w, v_g_out_w, v_xa_q, v_xa_kv, v_xa_o, v_f_up, v_f_conv_w, v_f_conv_b, v_f_down):
    local = dict(ln_mix=ln_mix, ln_xattn=ln_xattn, ln_mem=ln_mem, ln_ffn=ln_ffn, final_norm=final_norm, m_in_w=m_in_w, m_conv_w=m_conv_w, m_conv_b=m_conv_b, m_dt_bias=m_dt_bias, m_a_log=m_a_log, m_d=m_d, m_norm_w=m_norm_w, m_out_w=m_out_w, h_in_w=h_in_w, h_lower_bounds=h_lower_bounds, h_norm_w=h_norm_w, h_out_w=h_out_w, g_in_w=g_in_w, g_conv_w=g_conv_w, g_a_log=g_a_log, g_dt_bias=g_dt_bias, g_norm_w=g_norm_w, g_out_w=g_out_w, xa_q=xa_q, xa_kv=xa_kv, xa_o=xa_o, f_up=f_up, f_conv_w=f_conv_w, f_conv_b=f_conv_b, f_down=f_down)
    mom_m = dict(ln_mix=m_ln_mix, ln_xattn=m_ln_xattn, ln_mem=m_ln_mem, ln_ffn=m_ln_ffn, final_norm=m_final_norm, m_in_w=m_m_in_w, m_conv_w=m_m_conv_w, m_conv_b=m_m_conv_b, m_dt_bias=m_m_dt_bias, m_a_log=m_m_a_log, m_d=m_m_d, m_norm_w=m_m_norm_w, m_out_w=m_m_out_w, h_in_w=m_h_in_w, h_lower_bounds=m_h_lower_bounds, h_norm_w=m_h_norm_w, h_out_w=m_h_out_w, g_in_w=m_g_in_w, g_conv_w=m_g_conv_w, g_a_log=m_g_a_log, g_dt_bias=m_g_dt_bias, g_norm_w=m_g_norm_w, g_out_w=m_g_out_w, xa_q=m_xa_q, xa_kv=m_xa_kv, xa_o=m_xa_o, f_up=m_f_up, f_conv_w=m_f_conv_w, f_conv_b=m_f_conv_b, f_down=m_f_down)
    mom_v = dict(ln_mix=v_ln_mix, ln_xattn=v_ln_xattn, ln_mem=v_ln_mem, ln_ffn=v_ln_ffn, final_norm=v_final_norm, m_in_w=v_m_in_w, m_conv_w=v_m_conv_w, m_conv_b=v_m_conv_b, m_dt_bias=v_m_dt_bias, m_a_log=v_m_a_log, m_d=v_m_d, m_norm_w=v_m_norm_w, m_out_w=v_m_out_w, h_in_w=v_h_in_w, h_lower_bounds=v_h_lower_bounds, h_norm_w=v_h_norm_w, h_out_w=v_h_out_w, g_in_w=v_g_in_w, g_conv_w=v_g_conv_w, g_a_log=v_g_a_log, g_dt_bias=v_g_dt_bias, g_norm_w=v_g_norm_w, g_out_w=v_g_out_w, xa_q=v_xa_q, xa_kv=v_xa_kv, xa_o=v_xa_o, f_up=v_f_up, f_conv_w=v_f_conv_w, f_conv_b=v_f_conv_b, f_down=v_f_down)

    bl, seq, _ = x.shape
    t = bl * seq

    p = {n: local[n] for n in WEIGHTS if n not in SHARD_AXIS}
    for n in SMALL_SHARDED:
        p[n] = _full_from_gathered(all_gather(local[n], f"gather_{n}"), SHARD_AXIS[n])
    padded = {'m_in_w': M_IN_PAD, 'g_in_w': G_IN_PAD}
    for n in MATMUL_WEIGHTS:
        p[n] = []
        for l in range(local[n].shape[0]):
            w = _full_from_gathered(all_gather(local[n][l].astype(BF16), f"gather_{n}{l}"), SHARD_AXIS[n] - 1)
            p[n].append(_pad_cols(w, 1, padded[n]) if n in padded else w)

    diff = {n: ([jnp.zeros(w.shape, F32) for w in p[n]] if n in MATMUL_WEIGHTS else p[n])
            for n in WEIGHTS if n != 'final_norm'}

    def run(dp, xin):
        full = {n: (list(zip(p[n], dp[n])) if n in MATMUL_WEIGHTS else dp[n]) for n in dp}
        return _trunk(full, xin, mem.reshape(bl * N_MEM, D_MODEL), bl, seq)

    x_out, vjp = jax.vjp(run, diff, x.reshape(t, D_MODEL))
    loss_part, dx_out, d_final = loss_head(x_out, final_norm, loss_target.reshape(t, D_MODEL))
    grads, dx = vjp(dx_out)
    grads = dict(grads)
    grads['final_norm'] = d_final
    loss = lax.psum(loss_part, ("x", "y", "c"))

    outs = {}

    def update(name, n, slots, shape, sel=lambda a: a):
        two_d = lambda a: sel(a).reshape(slots.shape[1:])
        got = reduce_adamw(slots, two_d(local[n]), two_d(mom_m[n]), two_d(mom_v[n]), name)
        return [g.reshape(shape) for g in got]

    for n in SMALL_SHARDED:
        slots = exchange_slabs(_shards_of_full(grads[n], SHARD_AXIS[n]), f"exchange_{n}")
        slots = slots.reshape(N_DEV, -1, slots.shape[-1])
        for kind, a in zip(KINDS, update(f"adamw_{n}", n, slots, local[n].shape)):
            outs[kind, n] = a
    for n in MATMUL_WEIGHTS:
        per_layer = []
        for l, g in enumerate(grads[n]):
            g = g[:, :local[n].shape[2] * N_DEV] if n in padded else g
            slots = exchange_slabs(_shards_of_full(g, SHARD_AXIS[n] - 1), f"exchange_{n}{l}")
            per_layer.append(update(f"adamw_{n}{l}", n, slots, local[n].shape[1:], lambda a, l=l: a[l]))
        for k, kind in enumerate(KINDS):
            outs[kind, n] = jnp.stack([got[k] for got in per_layer])
    replicated = [n for n in WEIGHTS if n not in SHARD_AXIS]
    pk = lambda d: _pack([d[n] for n in replicated], F32, 8)
    got = reduce_adamw(all_gather(pk(grads), "gather_replicated_grads"), pk(local), pk(mom_m), pk(mom_v),
                       "adamw_replicated")
    shapes = [local[n].shape for n in replicated]
    for kind, buf in zip(KINDS, got):
        for n, a in zip(replicated, _unpack(buf, shapes)):
            outs[kind, n] = a
    result = [loss, dx.reshape(bl, seq, D_MODEL)]
    for kind in KINDS:
        result += [outs[kind, n] for n in WEIGHTS]
    return tuple(result)
```

```python
import functools
import math

import jax
import jax.numpy as jnp
from jax import lax
from jax.experimental import pallas as pl
from jax.experimental.pallas import tpu as pltpu

F32 = jnp.float32
BF16 = jnp.bfloat16
HIGHEST = lax.Precision.HIGHEST
NN = (((1,), (0,)), ((), ()))
NT = (((1,), (1,)), ((), ()))
TN = (((0,), (0,)), ((), ()))

D_MODEL = 1024
DEPTH = 4
EPS = 1e-6
N_MEM = 256
M_D_INNER = 2048
M_HEADS = 32
M_GROUPS = 8
M_STATE = 128
M_CONV_DIM = 4096
M_IN = 6176
M_IN_PAD = 6272
SSD_CHUNK = 256
H_HEADS = 8
HGRN_CHUNK = 32
HGRN_ROWS = 128
G_QK_HEADS = 8
G_V_HEADS = 16
G_KEY_DIM = 1024
G_VAL_DIM = 2048
G_CONV_DIM = 4096
G_IN = 6176
G_IN_PAD = 6272
GDN_CHUNK = 64
X_HEADS = 4
X_HEAD_DIM = 256
D_FF = 2816
ADAM_LR = 0.001
ADAM_B1 = 0.9
ADAM_B2 = 0.999
ADAM_EPS = 1e-08
ADAM_WD = 0.01
ADAM_STEP = 10

N_DEV = 8
LANE = 128
KINDS = ('grad', 'delta', 'new_m', 'new_v')
VMEM_LIMIT = 56 * 1024 * 1024

WEIGHTS = ['ln_mix', 'ln_xattn', 'ln_mem', 'ln_ffn', 'final_norm', 'm_in_w', 'm_conv_w', 'm_conv_b', 'm_dt_bias',
           'm_a_log', 'm_d', 'm_norm_w', 'm_out_w', 'h_in_w', 'h_lower_bounds', 'h_norm_w', 'h_out_w', 'g_in_w',
           'g_conv_w', 'g_a_log', 'g_dt_bias', 'g_norm_w', 'g_out_w', 'xa_q', 'xa_kv', 'xa_o', 'f_up', 'f_conv_w',
           'f_conv_b', 'f_down']
SHARD_AXIS = {'m_in_w': 2, 'm_conv_w': 2, 'm_conv_b': 1, 'm_norm_w': 1, 'm_out_w': 1, 'h_in_w': 2, 'h_out_w': 1,
              'g_in_w': 2, 'g_conv_w': 2, 'g_out_w': 1, 'xa_q': 1, 'xa_kv': 2, 'xa_o': 1, 'f_up': 2, 'f_conv_w': 2,
              'f_down': 1}
MATMUL_WEIGHTS = ['m_in_w', 'm_out_w', 'h_in_w', 'h_out_w', 'g_in_w', 'g_out_w', 'xa_q', 'xa_kv', 'xa_o', 'f_up',
                  'f_down']
SMALL_SHARDED = ['m_conv_w', 'm_conv_b', 'm_norm_w', 'g_conv_w', 'f_conv_w']


def _cparams():
    return pltpu.CompilerParams(vmem_limit_bytes=VMEM_LIMIT)


def bdot(a, b, dims=NN):
    return lax.dot_general(a.astype(BF16), b.astype(BF16), dims, preferred_element_type=F32)


def _split(a):
    hi = a.astype(BF16)
    return hi, (a - hi.astype(F32)).astype(BF16)


def _h3(a, b, dims):
    ah, al = _split(a)
    bh, bl = _split(b)
    d = functools.partial(lax.dot_general, dimension_numbers=dims, preferred_element_type=F32)
    return d(ah, bh) + (d(ah, bl) + d(al, bh))


@jax.custom_vjp
def h3dot(a, b):
    return _h3(a, b, NN)


h3dot.defvjp(lambda a, b: (_h3(a, b, NN), (a, b)),
             lambda res, ct: (_h3(ct, res[1], NT), _h3(res[0], ct, TN)))

T_ROWS = (((0,), (1,)), ((), ()))


def _tri_times(tri, x, dims, tri_first):
    t = tri.astype(BF16)
    x0 = x.astype(BF16)
    r1 = x - x0.astype(F32)
    x1 = r1.astype(BF16)
    x2 = (r1 - x1.astype(F32)).astype(BF16)
    if tri_first:
        d = lambda xx: lax.dot_general(t, xx, dims, preferred_element_type=F32)
    else:
        d = lambda xx: lax.dot_general(xx, t, dims, preferred_element_type=F32)
    return d(x0) + (d(x1) + d(x2))


@jax.custom_vjp
def cumdot(tri, x):
    return _tri_times(tri, x, NN, True)


cumdot.defvjp(lambda tri, x: (_tri_times(tri, x, NN, True), tri),
              lambda tri, ct: (jnp.zeros_like(tri), _tri_times(tri, ct, TN, True)))


@jax.custom_vjp
def cumdot_t(tri, x):
    return _tri_times(tri, x, T_ROWS, False)


cumdot_t.defvjp(lambda tri, x: (_tri_times(tri, x, T_ROWS, False), tri),
                lambda tri, ct: (jnp.zeros_like(tri), _tri_times(tri, ct, T_ROWS, True)))


def _tile(dim, cap):
    if dim <= cap:
        return dim
    best = None
    for t in range(LANE, cap + 1, LANE):
        if dim % t == 0:
            best = t
    assert best is not None, dim
    return best


def matmul(a, b, *, ta=False, tb=False, out_dtype=F32, name="mm"):
    if ta:
        k, m = a.shape
    else:
        m, k = a.shape
    if tb:
        n, k2 = b.shape
    else:
        k2, n = b.shape
    assert k == k2, (a.shape, b.shape, ta, tb)
    tm = _tile(m, 512)
    tn = _tile(n, 1408)
    tk = _tile(k, 1408)
    nk = k // tk
    dims = (((0 if ta else 1,), (1 if tb else 0,)), ((), ()))

    def body(a_ref, b_ref, o_ref, acc_ref):
        @pl.when(pl.program_id(2) == 0)
        def _():
            acc_ref[...] = jnp.zeros_like(acc_ref)

        acc_ref[...] += lax.dot_general(a_ref[...].astype(BF16), b_ref[...].astype(BF16), dims,
                                        preferred_element_type=F32)

        @pl.when(pl.program_id(2) == nk - 1)
        def _():
            o_ref[...] = acc_ref[...].astype(o_ref.dtype)

    a_spec = pl.BlockSpec((tk, tm), lambda i, j, kk: (kk, i)) if ta else pl.BlockSpec((tm, tk), lambda i, j, kk: (i, kk))
    b_spec = pl.BlockSpec((tn, tk), lambda i, j, kk: (j, kk)) if tb else pl.BlockSpec((tk, tn), lambda i, j, kk: (kk, j))
    return pl.pallas_call(
        body, name=name, grid=(m // tm, n // tn, nk),
        in_specs=[a_spec, b_spec],
        out_specs=pl.BlockSpec((tm, tn), lambda i, j, kk: (i, j)),
        out_shape=jax.ShapeDtypeStruct((m, n), out_dtype),
        scratch_shapes=[pltpu.VMEM((tm, tn), F32)],
        compiler_params=_cparams(),
    )(a, b)


def make_linear(name):
    @jax.custom_vjp
    def linear(a, w, wg):
        return matmul(a, w, name=name + "_fwd")

    def fwd(a, w, wg):
        return matmul(a, w, name=name + "_fwd"), (a, w)

    def bwd(res, dy):
        a, w = res
        da = matmul(dy, w, tb=True, out_dtype=a.dtype, name=name + "_bwd_da")
        dw = matmul(a, dy, ta=True, name=name + "_bwd_dw")
        return da, jnp.zeros_like(w), dw

    linear.defvjp(fwd, bwd)
    return linear


class In:
    def __init__(self, block, imap, kind='blk', inner=(), cols=None):
        self.block, self.imap, self.kind, self.inner, self.cols = block, imap, kind, inner, cols


class Out:
    def __init__(self, shape, dtype, block, imap):
        self.shape, self.dtype, self.block, self.imap = shape, dtype, block, imap


def make_op(name, fn, grid, ins, outs, state_shape=None, seq_axis=None):
    n_in, n_out = len(ins), len(outs)
    has_state = state_shape is not None
    nd = len(grid)
    diff_idx = [i for i, s in enumerate(ins) if s.kind != 'const']

    def in_spec(s, reverse):
        off = 0
        if s.cols is not None:
            assert s.cols[0] % s.block[-1] == 0
            off = s.cols[0] // s.block[-1]

        def imap(*ids):
            ids = rev(ids) if reverse else ids
            idx = tuple(s.imap(*ids))
            return idx[:-1] + (idx[-1] + off,) if off else idx

        return pl.BlockSpec(s.block, imap)

    def rel_spec(block, f, reverse):
        return pl.BlockSpec(block, (lambda *ids: f(*rev(ids))) if reverse else f)

    def rev(ids):
        if not has_state:
            return ids
        ids = list(ids)
        ids[seq_axis] = grid[seq_axis] - 1 - ids[seq_axis]
        return tuple(ids)

    save_shape = tuple(grid) + tuple(state_shape) if has_state else None
    save_block = (None,) * nd + tuple(state_shape) if has_state else None

    def save_imap(*ids):
        return tuple(ids) + (0,) * len(state_shape)

    def fwd_call(*xs):
        def body(*refs):
            in_refs = refs[:n_in]
            out_refs = refs[n_in:n_in + n_out]
            vals = [r[...] for r in in_refs]
            if has_state:
                save_ref, st_ref = refs[n_in + n_out], refs[n_in + n_out + 1]

                @pl.when(pl.program_id(seq_axis) == 0)
                def _():
                    st_ref[...] = jnp.zeros(state_shape, F32)

                st = st_ref[...]
                save_ref[...] = st
                res = fn(*vals, st)
                st_ref[...] = res[-1]
                res = res[:-1]
            else:
                res = fn(*vals)
            for o, v in zip(out_refs, res):
                o[...] = v.astype(o.dtype)

        out_shape = [jax.ShapeDtypeStruct(o.shape, o.dtype) for o in outs]
        out_specs = [pl.BlockSpec(o.block, o.imap) for o in outs]
        scratch = []
        if has_state:
            out_shape.append(jax.ShapeDtypeStruct(save_shape, F32))
            out_specs.append(pl.BlockSpec(save_block, save_imap))
            scratch.append(pltpu.VMEM(state_shape, F32))
        return pl.pallas_call(
            body, name=name + "_fwd", grid=grid,
            in_specs=[in_spec(s, False) for s in ins],
            out_specs=out_specs, out_shape=out_shape, scratch_shapes=scratch,
            compiler_params=_cparams(),
        )(*xs)

    def grad_shape(s, x):
        if s.cols is not None:
            return x.shape[:-1] + (s.cols[1],)
        return x.shape

    def bwd_call(xs, save, cts):
        n_diff = len(diff_idx)

        def body(*refs):
            in_refs = refs[:n_in]
            p = n_in
            if has_state:
                save_ref = refs[p]
                p += 1
            ct_refs = refs[p:p + n_out]
            p += n_out
            g_refs = refs[p:p + n_diff]
            p += n_diff
            vals = [r[...] for r in in_refs]

            def g(*dv):
                full = list(vals)
                for i, v in zip(diff_idx, dv):
                    full[i] = v
                if has_state:
                    return tuple(fn(*full, dv[-1]))
                return tuple(fn(*full))

            prim = [vals[i] for i in diff_idx]
            ct = tuple(r[...].astype(F32) for r in ct_refs)
            if has_state:
                dst_ref = refs[p]

                @pl.when(pl.program_id(seq_axis) == 0)
                def _():
                    dst_ref[...] = jnp.zeros(state_shape, F32)

                prim = prim + [save_ref[...]]
                ct = ct + (dst_ref[...],)
            _, vjp = jax.vjp(g, *prim)
            grads = vjp(ct)
            for k, i in enumerate(diff_idx):
                s = ins[i]
                if s.kind == 'blk':
                    g_refs[k][...] = grads[k].astype(g_refs[k].dtype)
                else:
                    first = None
                    for ax in s.inner:
                        c = pl.program_id(ax) == 0
                        first = c if first is None else jnp.logical_and(first, c)

                    @pl.when(first)
                    def _(k=k):
                        g_refs[k][...] = jnp.zeros_like(g_refs[k])

                    g_refs[k][...] += grads[k].astype(g_refs[k].dtype)
            if has_state:
                dst_ref[...] = grads[-1]

        in_specs = [in_spec(s, True) for s in ins]
        args = list(xs)
        if has_state:
            in_specs.append(rel_spec(save_block, save_imap, True))
            args.append(save)
        for o, c in zip(outs, cts):
            in_specs.append(rel_spec(o.block, o.imap, True))
            args.append(c)
        out_shape, out_specs = [], []
        for i in diff_idx:
            s = ins[i]
            out_shape.append(jax.ShapeDtypeStruct(grad_shape(s, xs[i]), xs[i].dtype))
            out_specs.append(rel_spec(s.block, s.imap, True))
        scratch = [pltpu.VMEM(state_shape, F32)] if has_state else []
        return pl.pallas_call(
            body, name=name + "_bwd", grid=grid,
            in_specs=in_specs, out_specs=out_specs, out_shape=out_shape, scratch_shapes=scratch,
            compiler_params=_cparams(),
        )(*args)

    @jax.custom_vjp
    def op(*xs):
        return tuple(fwd_call(*xs)[:n_out])

    def op_fwd(*xs):
        res = fwd_call(*xs)
        return tuple(res[:n_out]), (xs, res[n_out] if has_state else None)

    def op_bwd(resid, cts):
        xs, save = resid
        grads = bwd_call(xs, save, cts)
        out = []
        k = 0
        for i, s in enumerate(ins):
            if s.kind == 'const':
                out.append(jnp.zeros_like(xs[i]))
                continue
            g = grads[k]
            k += 1
            if s.cols is not None:
                g = jnp.pad(g, ((0, 0),) * (g.ndim - 1) + ((s.cols[0], xs[i].shape[-1] - s.cols[0] - s.cols[1]),))
            out.append(g)
        return tuple(out)

    op.defvjp(op_fwd, op_bwd)
    return op


def _rms(x, w):
    return x * lax.rsqrt(jnp.mean(x * x, axis=-1, keepdims=True) + EPS) * w


def _silu(x):
    return x * jax.nn.sigmoid(x)


def rmsnorm_op(name, t, out_dtype):
    tm = _tile(t, 512)
    return make_op(
        name, lambda x, w: (_rms(x, w),), (t // tm,),
        [In((tm, D_MODEL), lambda i: (i, 0)), In((1, D_MODEL), lambda i: (0, 0), 'acc', (0,))],
        [Out((t, D_MODEL), out_dtype, (tm, D_MODEL), lambda i: (i, 0))])


def _tri(q):
    ii = lax.broadcasted_iota(jnp.int32, (q, q), 0)
    jj = lax.broadcasted_iota(jnp.int32, (q, q), 1)
    return ii >= jj, ii > jj


def _ssd_fn(z, x, bm, cm, dtr, dtb, alog, dsk, nw, state):
    q = x.shape[0]
    incl, _ = _tri(q)
    tril = incl.astype(F32)
    dt = jax.nn.softplus(dtr + dtb)
    da = dt * (-jnp.exp(alog))
    acum = cumdot(tril, da)
    acum_t = cumdot_t(tril, da)
    cb = bdot(cm, bm, NT)
    ys, sts = [], []
    for r in range(4):
        col = acum[:, r:r + 1]
        row = acum_t[r:r + 1, :]
        decay = jnp.exp(jnp.where(incl, col - row, -jnp.inf))
        xr = x[:, 64 * r:64 * r + 64]
        xc = xr * dt[:, r:r + 1]
        st = state[64 * r:64 * r + 64, :]
        y = bdot(cb * decay, xc) + bdot(cm, st, NT) * jnp.exp(col)
        last = acum[q - 1:q, r:r + 1]
        ds = bdot(xc * jnp.exp(last - col), bm, TN)
        sts.append(st * jnp.exp(last) + ds)
        ys.append(y + dsk[:, r:r + 1] * xr)
    y = jnp.concatenate(ys, axis=1)
    yz = y * _silu(z)
    return _rms(yz, nw), jnp.concatenate(sts, axis=0)


def _per_sequence(fn, n_seq_args, bl):
    def f(*args):
        *ins, state = args
        res = [fn(*[a[b] for a in ins[:n_seq_args]], *ins[n_seq_args:], state[b]) for b in range(bl)]
        return tuple(jnp.stack([r[k] for r in res]) for k in range(len(res[0])))

    return f


def ssd_op(name, bl, seq):
    q = SSD_CHUNK
    nc = seq // q
    blk = lambda w, c0, cw: In((bl, q, w), lambda g, n: (0, n, g), cols=(c0, cw))
    small = lambda g, n: (g, 0, 0)
    ins = [
        blk(256, 0, M_D_INNER),
        blk(256, 0, M_D_INNER),
        blk(128, M_D_INNER, 1024),
        blk(128, M_D_INNER + 1024, 1024),
        In((None, bl, q, 4), lambda g, n: (g, 0, n, 0)),
        In((None, 1, 4), small, 'acc', (1,)),
        In((None, 1, 4), small, 'acc', (1,)),
        In((None, 1, 4), small, 'acc', (1,)),
        In((None, 1, 256), small, 'acc', (1,)),
    ]
    outs = [Out((bl, seq, M_D_INNER), F32, (bl, q, 256), lambda g, n: (0, n, g))]
    return make_op(name, _per_sequence(_ssd_fn, 5, bl), (M_GROUPS, nc), ins, outs,
                   state_shape=(bl, 256, 128), seq_axis=1)


def _gla_fn(layer, qr, fr, ir, gr, lbp, nw, state_t):
    rows = qr.shape[0]
    c = HGRN_CHUNK
    e = jnp.exp(lbp - jnp.max(lbp, axis=0, keepdims=True))
    sm = e / jnp.sum(e, axis=0, keepdims=True)
    lb = jnp.sum(sm[1:layer + 1, :], axis=0, keepdims=True) if layer > 0 else jnp.zeros((1, lbp.shape[1]), F32)
    qq = _silu(qr) * (128 ** -0.5)
    forget = lb + (1.0 - lb) * jax.nn.sigmoid(fr)
    kk = 1.0 - forget
    logf = jnp.log(forget)
    incl, _ = _tri(c)
    tril = incl.astype(F32)
    os_ = []
    for j in range(rows // c):
        sl = slice(c * j, c * j + c)
        gc = cumdot(tril, logf[sl])
        glast = gc[c - 1:c, :]
        q_dec = qq[sl] * jnp.exp(gc)
        k_inv = kk[sl] * jnp.exp(-gc)
        k_end = kk[sl] * jnp.exp(glast - gc)
        att = jnp.where(incl, bdot(q_dec, k_inv, NT), 0.0)
        os_.append(bdot(att, ir[sl]) + bdot(q_dec, state_t, NT))
        state_t = state_t * jnp.exp(glast) + bdot(ir[sl], k_end, TN)
    o = jnp.concatenate(os_, axis=0)
    return _rms(o, nw) * _silu(gr), state_t


def gla_op(name, layer, bl, seq):
    r = HGRN_ROWS
    ns = seq // r
    blk = lambda k: In((bl, r, 128), lambda h, n: (0, n, h), cols=(1024 * k, 1024))
    ins = [blk(0), blk(1), blk(2), blk(3),
           In((DEPTH, 128), lambda h, n: (0, h), 'acc', (1,)),
           In((1, 128), lambda h, n: (0, 0), 'acc', (0, 1))]
    outs = [Out((bl, seq, D_MODEL), F32, (bl, r, 128), lambda h, n: (0, n, h))]
    return make_op(name, _per_sequence(functools.partial(_gla_fn, layer), 4, bl), (H_HEADS, ns), ins, outs,
                   state_shape=(bl, 128, 128), seq_axis=1)


def _unit_lower_inverse(m, nilpotent):
    q = m.shape[0]
    ii = lax.broadcasted_iota(jnp.int32, (q, q), 0)
    jj = lax.broadcasted_iota(jnp.int32, (q, q), 1)
    eye = (ii == jj).astype(F32)
    p = -m
    inv = eye + p
    steps = int(math.log2(nilpotent)) - 1
    for _ in range(steps):
        p = h3dot(p, p)
        inv = inv + h3dot(inv, p)
    return inv


def _gdn_fn(qc, kc, vc, zc, br, ar, alog, dtb, nw, state):
    bl, q = qc.shape[0], qc.shape[1]
    incl, strict = _tri(q)
    tril = incl.astype(F32)
    g = jnp.concatenate([-jnp.exp(alog) * jax.nn.softplus(ar[b] + dtb) for b in range(bl)], axis=1)
    gc = cumdot(tril, g)
    gc_t = cumdot_t(tril, g)
    heads, ms, rhs = [], [], []
    for b in range(bl):
        qn = qc[b] * lax.rsqrt(jnp.sum(qc[b] * qc[b], axis=-1, keepdims=True) + EPS) * (128 ** -0.5)
        kn = kc[b] * lax.rsqrt(jnp.sum(kc[b] * kc[b], axis=-1, keepdims=True) + EPS)
        beta = jax.nn.sigmoid(br[b])
        qk = bdot(qn, kn, NT)
        for j in range(2):
            i = 2 * b + j
            col = gc[:, i:i + 1]
            decay = jnp.exp(jnp.where(incl, col - gc_t[i:i + 1, :], -jnp.inf))
            bj = beta[:, j:j + 1]
            kb = kn * bj
            ms.append(jnp.where(strict, bdot(kb, kn, NT) * decay, 0.0))
            rhs.append(jnp.concatenate([vc[b][:, 128 * j:128 * j + 128] * bj, kb * jnp.exp(col)], axis=1))
            heads.append((qn, kn, qk * decay, col, gc[q - 1:q, i:i + 1]))
    n = len(ms)
    zero = jnp.zeros((q, q), F32)
    m_all = jnp.concatenate(
        [jnp.concatenate([ms[i] if i == k else zero for i in range(n)], axis=1) for k in range(n)], axis=0)
    sol = h3dot(_unit_lower_inverse(m_all, q), jnp.concatenate(rhs, axis=0))
    outs, states = [], []
    for b in range(bl):
        os_, sts = [], []
        for j in range(2):
            i = 2 * b + j
            qn, kn, att, col, glast = heads[i]
            u = sol[q * i:q * i + q, :128]
            w = sol[q * i:q * i + q, 128:]
            st = state[b][128 * j:128 * j + 128, :]
            v_new = u - bdot(w, st)
            o = bdot(qn * jnp.exp(col), st) + bdot(att, v_new)
            sts.append(st * jnp.exp(glast) + bdot(kn * jnp.exp(glast - col), v_new, TN))
            os_.append(_rms(o, nw) * _silu(zc[b][:, 128 * j:128 * j + 128]))
        outs.append(jnp.concatenate(os_, axis=1))
        states.append(jnp.concatenate(sts, axis=0))
    return jnp.stack(outs), jnp.stack(states)


def gdn_op(name, bl, seq):
    q = GDN_CHUNK
    nc = seq // q
    blk = lambda w, c0, cw: In((bl, q, w), lambda h, n: (0, n, h), cols=(c0, cw))
    small = lambda h, n: (h, 0, 0)
    ins = [
        blk(128, 0, G_KEY_DIM),
        blk(128, G_KEY_DIM, G_KEY_DIM),
        blk(256, 2 * G_KEY_DIM, G_VAL_DIM),
        blk(256, G_CONV_DIM, G_VAL_DIM),
        In((None, bl, q, 2), lambda h, n: (h, 0, n, 0)),
        In((None, bl, q, 2), lambda h, n: (h, 0, n, 0)),
        In((None, 1, 2), small, 'acc', (1,)),
        In((None, 1, 2), small, 'acc', (1,)),
        In((1, 128), lambda h, n: (0, 0), 'acc', (0, 1)),
    ]
    outs = [Out((bl, seq, G_VAL_DIM), F32, (bl, q, 256), lambda h, n: (0, n, h))]
    return make_op(name, _gdn_fn, (G_QK_HEADS, nc), ins, outs,
                   state_shape=(bl, 256, 128), seq_axis=1)


def _xattn_fn(q, k, v):
    s = bdot(q, k, NT) * (X_HEAD_DIM ** -0.5)
    s = s - jnp.max(s, axis=-1, keepdims=True)
    p = jnp.exp(s)
    p = p / jnp.sum(p, axis=-1, keepdims=True)
    return (bdot(p, v),)


def xattn_op(name, bl, seq):
    tq = _tile(seq, 512)
    nq = seq // tq
    t = bl * seq
    ins = [
        In((tq, X_HEAD_DIM), lambda b, h, i: (b * nq + i, h)),
        In((N_MEM, X_HEAD_DIM), lambda b, h, i: (b, h), 'acc', (2,), cols=(0, D_MODEL)),
        In((N_MEM, X_HEAD_DIM), lambda b, h, i: (b, h), 'acc', (2,), cols=(D_MODEL, D_MODEL)),
    ]
    outs = [Out((t, D_MODEL), F32, (tq, X_HEAD_DIM), lambda b, h, i: (b * nq + i, h))]
    return make_op(name, _xattn_fn, (bl, X_HEADS, nq), ins, outs)


CONV_PAD = 8


def make_conv(name, bl, seq, width, ch, x_col0, up_col0=None):
    cb = 256
    assert ch % cb == 0 and x_col0 % cb == 0 and (up_col0 is None or up_col0 % cb == 0)
    nb = ch // cb
    t = bl * seq
    has_up = up_col0 is not None
    grid = (nb, bl)
    x_spec = pl.BlockSpec((seq, cb), lambda c, b: (b, x_col0 // cb + c))
    up_specs = [pl.BlockSpec((seq, cb), lambda c, b: (b, up_col0 // cb + c))] if has_up else []
    w_spec = pl.BlockSpec((width, cb), lambda c, b: (0, c))
    b_spec = pl.BlockSpec((1, cb), lambda c, b: (0, c))
    o_spec = pl.BlockSpec((seq, cb), lambda c, b: (b, c))
    taps = [CONV_PAD - (width - 1) + j for j in range(width)]

    def pre_activation(x_ref, w_ref, b_ref, pad_ref):
        pad_ref[0:CONV_PAD, :] = jnp.zeros((CONV_PAD, cb), F32)
        pad_ref[CONV_PAD:CONV_PAD + seq, :] = x_ref[...]
        w = w_ref[...]
        y = b_ref[...] + jnp.zeros((seq, cb), F32)
        for j in range(width):
            y = y + w[j:j + 1, :] * pad_ref[pl.ds(taps[j], seq), :]
        return y

    def fwd_call(x, w, b):
        def body(*refs):
            x_ref, w_ref, b_ref = refs[:3]
            o_ref, pad_ref = refs[-2:]
            y = _silu(pre_activation(x_ref, w_ref, b_ref, pad_ref))
            if has_up:
                y = y * refs[3][...]
            o_ref[...] = y

        return pl.pallas_call(
            body, name=name + "_fwd", grid=grid,
            in_specs=[x_spec, w_spec, b_spec] + up_specs, out_specs=o_spec,
            out_shape=jax.ShapeDtypeStruct((t, ch), F32),
            scratch_shapes=[pltpu.VMEM((seq + CONV_PAD, cb), F32)],
            compiler_params=_cparams(),
        )(*([x, w, b] + ([x] if has_up else [])))

    def bwd_call(x, w, b, do):
        n_in = 4 + (1 if has_up else 0)

        def body(*refs):
            x_ref, w_ref, b_ref = refs[:3]
            do_ref = refs[n_in - 1]
            dx_ref, dw_ref, db_ref = refs[n_in:n_in + 3]
            pad_ref, gpad_ref = refs[-2:]
            y = pre_activation(x_ref, w_ref, b_ref, pad_ref)
            s = jax.nn.sigmoid(y)
            act = y * s
            do = do_ref[...]
            if has_up:
                refs[n_in + 3][...] = do * act
                do = do * refs[3][...]
            dy = do * (s + act * (1.0 - s))
            gpad_ref[0:seq, :] = dy
            gpad_ref[seq:seq + CONV_PAD, :] = jnp.zeros((CONV_PAD, cb), F32)
            w_ = w_ref[...]
            dx = jnp.zeros((seq, cb), F32)
            dws = []
            for j in range(width):
                dx = dx + w_[j:j + 1, :] * gpad_ref[pl.ds(width - 1 - j, seq), :]
                dws.append(jnp.sum(dy * pad_ref[pl.ds(taps[j], seq), :], axis=0, keepdims=True))
            dx_ref[...] = dx

            @pl.when(pl.program_id(1) == 0)
            def _():
                dw_ref[...] = jnp.zeros_like(dw_ref)
                db_ref[...] = jnp.zeros_like(db_ref)

            dw_ref[...] += jnp.concatenate(dws, axis=0)
            db_ref[...] += jnp.sum(dy, axis=0, keepdims=True)

        big = jax.ShapeDtypeStruct((t, ch), F32)
        return pl.pallas_call(
            body, name=name + "_bwd", grid=grid,
            in_specs=[x_spec, w_spec, b_spec] + up_specs + [o_spec],
            out_specs=[o_spec, w_spec, b_spec] + ([o_spec] if has_up else []),
            out_shape=[big, jax.ShapeDtypeStruct((width, ch), F32), jax.ShapeDtypeStruct((1, ch), F32)]
            + ([big] if has_up else []),
            scratch_shapes=[pltpu.VMEM((seq + CONV_PAD, cb), F32), pltpu.VMEM((seq + CONV_PAD, cb), F32)],
            compiler_params=_cparams(),
        )(*([x, w, b] + ([x] if has_up else []) + [do]))

    @jax.custom_vjp
    def conv(x, w, b):
        return fwd_call(x, w, b)

    def conv_fwd(x, w, b):
        return fwd_call(x, w, b), (x, w, b)

    def conv_bwd(res, do):
        x, w, b = res
        got = bwd_call(x, w, b, do)
        dx = jnp.pad(got[0], ((0, 0), (x_col0, x.shape[1] - x_col0 - ch)))
        if has_up:
            dx = dx + jnp.pad(got[3], ((0, 0), (up_col0, x.shape[1] - up_col0 - ch)))
        return dx, got[1], got[2]

    conv.defvjp(conv_fwd, conv_bwd)

    def apply(x, w, b=None):
        if b is None:
            b = jnp.zeros((ch,), F32)
        return conv(x, w, b.reshape(1, ch))

    return apply


def loss_head(x, w, target):
    t = x.shape[0]
    tm = _tile(t, 512)

    def fn(xb, wb, tb):
        err = _rms(xb, wb) - tb
        return 0.5 * jnp.sum(err * err) * (1.0 / D_MODEL)

    def body(x_ref, w_ref, t_ref, loss_ref, dx_ref, dw_ref):
        @pl.when(pl.program_id(0) == 0)
        def _():
            loss_ref[...] = jnp.zeros_like(loss_ref)
            dw_ref[...] = jnp.zeros_like(dw_ref)

        tb = t_ref[...]
        val, vjp = jax.vjp(lambda a, b: fn(a, b, tb), x_ref[...], w_ref[...])
        dx, dw = vjp(jnp.ones((), F32))
        dx_ref[...] = dx
        dw_ref[...] += dw
        loss_ref[...] += jnp.full(loss_ref.shape, val, F32)

    row = pl.BlockSpec((tm, D_MODEL), lambda i: (i, 0))
    vec = pl.BlockSpec((1, D_MODEL), lambda i: (0, 0))
    loss, dx, dw = pl.pallas_call(
        body, name="loss_head", grid=(t // tm,),
        in_specs=[row, vec, row],
        out_specs=[pl.BlockSpec((8, LANE), lambda i: (0, 0)), row, vec],
        out_shape=[jax.ShapeDtypeStruct((8, LANE), F32), jax.ShapeDtypeStruct((t, D_MODEL), F32),
                   jax.ShapeDtypeStruct((1, D_MODEL), F32)],
        compiler_params=_cparams(),
    )(x, w.reshape(1, D_MODEL), target)
    return loss[0, 0], dx, dw.reshape(D_MODEL)


PACK_W = 1024
ADAM_BLOCK_BYTES = 512 * 1024


def _rows_tile(r, c):
    if r * c * 4 <= ADAM_BLOCK_BYTES or r % 8:
        return r
    best = 8
    for t in range(8, r + 1, 8):
        if r % t == 0 and t * c * 4 <= ADAM_BLOCK_BYTES:
            best = t
    return best


def reduce_adamw(slots, w, m, v, name):
    r, wd = w.shape
    tr = _rows_tile(r, wd)
    c1 = 1.0 - ADAM_B1 ** ADAM_STEP
    c2 = 1.0 - ADAM_B2 ** ADAM_STEP

    def body(s_ref, w_ref, m_ref, v_ref, g_ref, d_ref, nm_ref, nv_ref):
        g = s_ref[0].astype(F32)
        for k in range(1, N_DEV):
            g = g + s_ref[k].astype(F32)
        nm = ADAM_B1 * m_ref[...] + (1.0 - ADAM_B1) * g
        nv = ADAM_B2 * v_ref[...] + (1.0 - ADAM_B2) * (g * g)
        m_hat = nm / c1
        v_hat = nv / c2
        d_ref[...] = -ADAM_LR * (m_hat / (jnp.sqrt(v_hat) + ADAM_EPS) + ADAM_WD * w_ref[...])
        g_ref[...] = g
        nm_ref[...] = nm
        nv_ref[...] = nv

    blk = pl.BlockSpec((tr, wd), lambda i: (i, 0))
    shp = jax.ShapeDtypeStruct((r, wd), F32)
    return pl.pallas_call(
        body, name=name, grid=(r // tr,),
        in_specs=[pl.BlockSpec((N_DEV, tr, wd), lambda i: (0, i, 0)), blk, blk, blk],
        out_specs=[blk, blk, blk, blk], out_shape=[shp, shp, shp, shp],
        compiler_params=_cparams(),
    )(slots, w, m, v)


def _position():
    return lax.axis_index("x"), lax.axis_index("y"), lax.axis_index("c")


def all_gather(block, name):
    def body(x_ref, out_ref, send_sems, recv_sems, local_sem):
        x, y, c = _position()
        me, sibling = (x, y, c), (x, y, 1 - c)
        chips = [(1 - x, y), (x, 1 - y), (1 - x, 1 - y)]

        def slot(px, py, pc):
            return out_ref.at[4 * px + 2 * py + pc]

        def copy(k, owner, to, src=None):
            return pltpu.make_async_remote_copy(
                src_ref=slot(*owner) if src is None else src, dst_ref=slot(*owner),
                send_sem=send_sems.at[k], recv_sem=recv_sems.at[k],
                device_id=to, device_id_type=pl.DeviceIdType.MESH)

        mine = pltpu.make_async_copy(x_ref, slot(*me), local_sem)
        mine.start()
        first = [copy(0, me, sibling, src=x_ref)]
        first += [copy(1 + j, me, (*chip, c), src=x_ref) for j, chip in enumerate(chips)]
        for cp in first:
            cp.start()
        passed = [copy(4 + j, (*chip, c), sibling) for j, chip in enumerate(chips)]
        for j, chip in enumerate(chips):
            copy(1 + j, (*chip, c), me).wait_recv()
            passed[j].start()
        copy(0, sibling, me).wait_recv()
        for j, chip in enumerate(chips):
            copy(4 + j, (*chip, 1 - c), me).wait_recv()
        for cp in first + passed:
            cp.wait_send()
        mine.wait()

    return pl.pallas_call(
        body, name=name,
        out_shape=jax.ShapeDtypeStruct((N_DEV,) + block.shape, block.dtype),
        in_specs=[pl.BlockSpec(memory_space=pl.ANY)],
        out_specs=pl.BlockSpec(memory_space=pl.ANY),
        scratch_shapes=[pltpu.SemaphoreType.DMA((7,)), pltpu.SemaphoreType.DMA((7,)), pltpu.SemaphoreType.DMA],
    )(block)


def exchange_slabs(slabs, name):
    def body(in_ref, out_ref, send_sems, recv_sems, local_sem):
        x, y, c = _position()
        my = 4 * x + 2 * y + c
        mine = pltpu.make_async_copy(in_ref.at[my], out_ref.at[my], local_sem)
        mine.start()
        copies = []
        for k in range(1, N_DEV):
            dx, dy, dc = (k >> 2) & 1, (k >> 1) & 1, k & 1
            px = x if dx == 0 else 1 - x
            py = y if dy == 0 else 1 - y
            pc = c if dc == 0 else 1 - c
            cp = pltpu.make_async_remote_copy(
                src_ref=in_ref.at[4 * px + 2 * py + pc], dst_ref=out_ref.at[my],
                send_sem=send_sems.at[k - 1], recv_sem=recv_sems.at[k - 1],
                device_id=(px, py, pc), device_id_type=pl.DeviceIdType.MESH)
            cp.start()
            copies.append(cp)
        for cp in copies:
            cp.wait()
        mine.wait()

    return pl.pallas_call(
        body, name=name,
        out_shape=jax.ShapeDtypeStruct(slabs.shape, slabs.dtype),
        in_specs=[pl.BlockSpec(memory_space=pl.ANY)],
        out_specs=pl.BlockSpec(memory_space=pl.ANY),
        scratch_shapes=[pltpu.SemaphoreType.DMA((7,)), pltpu.SemaphoreType.DMA((7,)), pltpu.SemaphoreType.DMA],
    )(slabs)


def _pack(arrays, dtype, row_multiple):
    flat = jnp.concatenate([a.astype(dtype).reshape(-1) for a in arrays])
    n = flat.shape[0]
    per = PACK_W * row_multiple
    total = -(-n // per) * per
    flat = jnp.pad(flat, (0, total - n))
    return flat.reshape(total // PACK_W, PACK_W)


def _unpack(flat2d, shapes, lead=()):
    flat = flat2d.reshape(lead + (-1,))
    out, off = [], 0
    for shp in shapes:
        n = math.prod(shp)
        out.append(flat[..., off:off + n].reshape(lead + tuple(shp)))
        off += n
    return out


def _full_from_gathered(g, axis):
    g = jnp.moveaxis(g, 0, axis)
    shp = list(g.shape)
    shp[axis:axis + 2] = [shp[axis] * shp[axis + 1]]
    return g.reshape(shp)


def _shards_of_full(full, axis):
    shp = list(full.shape)
    shp[axis:axis + 1] = [N_DEV, shp[axis] // N_DEV]
    return jnp.moveaxis(full.reshape(shp), axis, 0)


def _trunk(p, x, mem, bl, seq):
    t = bl * seq
    ia = ib = ic = 0

    def lin(name, a, wname, idx):
        w, wg = p[wname][idx]
        return make_linear(name)(a, w, wg)

    by_seq = lambda a: a.reshape(bl, seq, a.shape[-1])

    for i in range(DEPTH):
        hn = rmsnorm_op(f"ln_mix{i}", t, F32)(x, p['ln_mix'][i:i + 1])[0]
        kind = i % 3
        if kind == 0:
            proj = lin(f"m_in{i}", hn, 'm_in_w', ia)
            xbc = make_conv(f"m_conv{i}", bl, seq, 4, M_CONV_DIM, M_D_INNER)(
                proj, p['m_conv_w'][ia], p['m_conv_b'][ia])
            dt = proj[:, M_D_INNER + M_CONV_DIM:M_IN].reshape(bl, seq, M_GROUPS, 4).transpose(2, 0, 1, 3)
            grp = lambda a, n=4: a.reshape(M_GROUPS, 1, n)
            proj3, xbc3 = by_seq(proj), by_seq(xbc)
            y = ssd_op(f"ssd{i}", bl, seq)(
                proj3, xbc3, xbc3, xbc3, dt, grp(p['m_dt_bias'][ia]), grp(p['m_a_log'][ia]), grp(p['m_d'][ia]),
                grp(p['m_norm_w'][ia], 256))[0]
            mix = lin(f"m_out{i}", y.reshape(t, M_D_INNER), 'm_out_w', ia)
            ia += 1
        elif kind == 1:
            proj3 = by_seq(lin(f"h_in{i}", hn, 'h_in_w', ib))
            y = gla_op(f"gla{i}", i, bl, seq)(
                proj3, proj3, proj3, proj3, p['h_lower_bounds'], p['h_norm_w'][ib:ib + 1])[0]
            mix = lin(f"h_out{i}", y.reshape(t, D_MODEL), 'h_out_w', ib)
            ib += 1
        else:
            proj = lin(f"g_in{i}", hn, 'g_in_w', ic)
            qkv = make_conv(f"g_conv{i}", bl, seq, 4, G_CONV_DIM, 0)(proj, p['g_conv_w'][ic])
            c0 = G_CONV_DIM + G_VAL_DIM
            heads = lambda a: a.reshape(bl, seq, G_QK_HEADS, 2).transpose(2, 0, 1, 3)
            braw = heads(proj[:, c0:c0 + G_V_HEADS])
            araw = heads(proj[:, c0 + G_V_HEADS:c0 + 2 * G_V_HEADS])
            grp = lambda a: a.reshape(G_QK_HEADS, 1, 2)
            qkv3 = by_seq(qkv)
            y = gdn_op(f"gdn{i}", bl, seq)(
                qkv3, qkv3, qkv3, by_seq(proj), braw, araw, grp(p['g_a_log'][ic]), grp(p['g_dt_bias'][ic]),
                p['g_norm_w'][ic:ic + 1])[0]
            mix = lin(f"g_out{i}", y.reshape(t, G_VAL_DIM), 'g_out_w', ic)
            ic += 1
        x = x + mix
        hq = rmsnorm_op(f"ln_xattn{i}", t, F32)(x, p['ln_xattn'][i:i + 1])[0]
        mn = rmsnorm_op(f"ln_mem{i}", bl * N_MEM, F32)(mem, p['ln_mem'][i:i + 1])[0]
        qx = lin(f"xa_q{i}", hq, 'xa_q', i)
        kv = lin(f"xa_kv{i}", mn, 'xa_kv', i)
        ao = xattn_op(f"xattn{i}", bl, seq)(qx, kv, kv)[0]
        x = x + lin(f"xa_o{i}", ao, 'xa_o', i)
        hf = rmsnorm_op(f"ln_ffn{i}", t, F32)(x, p['ln_ffn'][i:i + 1])[0]
        up = lin(f"f_up{i}", hf, 'f_up', i)
        act = make_conv(f"f_conv{i}", bl, seq, 3, D_FF, 0, up_col0=D_FF)(up, p['f_conv_w'][i], p['f_conv_b'][i])
        x = x + lin(f"f_down{i}", act, 'f_down', i)
    return x


def _pad_cols(w, axis, to):
    pad = [(0, 0)] * w.ndim
    pad[axis] = (0, to - w.shape[axis])
    return jnp.pad(w, pad)


def kernel(x, mem, ln_mix, ln_xattn, ln_mem, ln_ffn, final_norm, m_in_w, m_conv_w, m_conv_b, m_dt_bias, m_a_log, m_d, m_norm_w, m_out_w, h_in_w, h_lower_bounds, h_norm_w, h_out_w, g_in_w, g_conv_w, g_a_log, g_dt_bias, g_norm_w, g_out_w, xa_q, xa_kv, xa_o, f_up, f_conv_w, f_conv_b, f_down, loss_target, m_ln_mix, m_ln_xattn, m_ln_mem, m_ln_ffn, m_final_norm, m_m_in_w, m_m_conv_w, m_m_conv_b, m_m_dt_bias, m_m_a_log, m_m_d, m_m_norm_w, m_m_out_w, m_h_in_w, m_h_lower_bounds, m_h_norm_w, m_h_out_w, m_g_in_w, m_g_conv_w, m_g_a_log, m_g_dt_bias, m_g_norm_w, m_g_out_w, m_xa_q, m_xa_kv, m_xa_o, m_f_up, m_f_conv_w, m_f_conv_b, m_f_down, v_ln_mix, v_ln_xattn, v_ln_mem, v_ln_ffn, v_final_norm, v_m_in_w, v_m_conv_w, v_m_conv_b, v_m_dt_bias, v_m_a_log, v_m_d, v_m_norm_w, v_m_out_w, v_h_in_w, v_h_lower_bounds, v_h_norm_w, v_h_out_w, v_g_in_w, v_g_conv_w, v_g_a_log, v_g_dt_bias, v_g_norm_w, v_g_out_w, v_xa_q, v_xa_kv, v_xa_o, v_f_up, v_f_conv_w, v_f_conv_b, v_f_down):
    local = dict(ln_mix=ln_mix, ln_xattn=ln_xattn, ln_mem=ln_mem, ln_ffn=ln_ffn, final_norm=final_norm, m_in_w=m_in_w, m_conv_w=m_conv_w, m_conv_b=m_conv_b, m_dt_bias=m_dt_bias, m_a_log=m_a_log, m_d=m_d, m_norm_w=m_norm_w, m_out_w=m_out_w, h_in_w=h_in_w, h_lower_bounds=h_lower_bounds, h_norm_w=h_norm_w, h_out_w=h_out_w, g_in_w=g_in_w, g_conv_w=g_conv_w, g_a_log=g_a_log, g_dt_bias=g_dt_bias, g_norm_w=g_norm_w, g_out_w=g_out_w, xa_q=xa_q, xa_kv=xa_kv, xa_o=xa_o, f_up=f_up, f_conv_w=f_conv_w, f_conv_b=f_conv_b, f_down=f_down)
    mom_m = dict(ln_mix=m_ln_mix, ln_xattn=m_ln_xattn, ln_mem=m_ln_mem, ln_ffn=m_ln_ffn, final_norm=m_final_norm, m_in_w=m_m_in_w, m_conv_w=m_m_conv_w, m_conv_b=m_m_conv_b, m_dt_bias=m_m_dt_bias, m_a_log=m_m_a_log, m_d=m_m_d, m_norm_w=m_m_norm_w, m_out_w=m_m_out_w, h_in_w=m_h_in_w, h_lower_bounds=m_h_lower_bounds, h_norm_w=m_h_norm_w, h_out_w=m_h_out_w, g_in_w=m_g_in_w, g_conv_w=m_g_conv_w, g_a_log=m_g_a_log, g_dt_bias=m_g_dt_bias, g_norm_w=m_g_norm_w, g_out_w=m_g_out_w, xa_q=m_xa_q, xa_kv=m_xa_kv, xa_o=m_xa_o, f_up=m_f_up, f_conv_w=m_f_conv_w, f_conv_b=m_f_conv_b, f_down=m_f_down)
    mom_v = dict(ln_mix=v_ln_mix, ln_xattn=v_ln_xattn, ln_mem=v_ln_mem, ln_ffn=v_ln_ffn, final_norm=v_final_norm, m_in_w=v_m_in_w, m_conv_w=v_m_conv_w, m_conv_b=v_m_conv_b, m_dt_bias=v_m_dt_bias, m_a_log=v_m_a_log, m_d=v_m_d, m_norm_w=v_m_norm_w, m_out_w=v_m_out_w, h_in_w=v_h_in_w, h_lower_bounds=v_h_lower_bounds, h_norm_w=v_h_norm_w, h_out_w=v_h_out_w, g_in_w=v_g_in_w, g_conv_w=v_g_conv_w, g_a_log=v_g_a_log, g_dt_bias=v_g_dt_bias, g_norm_w=v_g_norm_w, g_out_w=v_g_out_w, xa_q=v_xa_q, xa_kv=v_xa_kv, xa_o=v_xa_o, f_up=v_f_up, f_conv_w=v_f_conv_w, f_conv_b=v_f_conv_b, f_down=v_f_down)

    bl, seq, _ = x.shape
    t = bl * seq

    p = {n: local[n] for n in WEIGHTS if n not in SHARD_AXIS}
    for n in SMALL_SHARDED:
        p[n] = _full_from_gathered(all_gather(local[n], f"gather_{n}"), SHARD_AXIS[n])
    padded = {'m_in_w': M_IN_PAD, 'g_in_w': G_IN_PAD}
    for n in MATMUL_WEIGHTS:
        p[n] = []
        for l in range(local[n].shape[0]):
            w = _full_from_gathered(all_gather(local[n][l].astype(BF16), f"gather_{n}{l}"), SHARD_AXIS[n] - 1)
            p[n].append(_pad_cols(w, 1, padded[n]) if n in padded else w)

    diff = {n: ([jnp.zeros(w.shape, F32) for w in p[n]] if n in MATMUL_WEIGHTS else p[n])
            for n in WEIGHTS if n != 'final_norm'}

    def run(dp, xin):
        full = {n: (list(zip(p[n], dp[n])) if n in MATMUL_WEIGHTS else dp[n]) for n in dp}
        return _trunk(full, xin, mem.reshape(bl * N_MEM, D_MODEL), bl, seq)

    x_out, vjp = jax.vjp(run, diff, x.reshape(t, D_MODEL))
    loss_part, dx_out, d_final = loss_head(x_out, final_norm, loss_target.reshape(t, D_MODEL))
    grads, dx = vjp(dx_out)
    grads = dict(grads)
    grads['final_norm'] = d_final
    loss = lax.psum(loss_part, ("x", "y", "c"))

    outs = {}

    def update(name, n, slots, shape, sel=lambda a: a):
        two_d = lambda a: sel(a).reshape(slots.shape[1:])
        got = reduce_adamw(slots, two_d(local[n]), two_d(mom_m[n]), two_d(mom_v[n]), name)
        return [g.reshape(shape) for g in got]

    for n in SMALL_SHARDED:
        slots = exchange_slabs(_shards_of_full(grads[n], SHARD_AXIS[n]), f"exchange_{n}")
        slots = slots.reshape(N_DEV, -1, slots.shape[-1])
        for kind, a in zip(KINDS, update(f"adamw_{n}", n, slots, local[n].shape)):
            outs[kind, n] = a
    for n in MATMUL_WEIGHTS:
        per_layer = []
        for l, g in enumerate(grads[n]):
            g = g[:, :local[n].shape[2] * N_DEV] if n in padded else g
            slots = exchange_slabs(_shards_of_full(g, SHARD_AXIS[n] - 1).astype(BF16), f"exchange_{n}{l}")
            per_layer.append(update(f"adamw_{n}{l}", n, slots, local[n].shape[1:], lambda a, l=l: a[l]))
        for k, kind in enumerate(KINDS):
            outs[kind, n] = jnp.stack([got[k] for got in per_layer])
    replicated = [n for n in WEIGHTS if n not in SHARD_AXIS]
    pk = lambda d: _pack([d[n] for n in replicated], F32, 8)
    got = reduce_adamw(all_gather(pk(grads), "gather_replicated_grads"), pk(local), pk(mom_m), pk(mom_v),
                       "adamw_replicated")
    shapes = [local[n].shape for n in replicated]
    for kind, buf in zip(KINDS, got):
        for n, a in zip(replicated, _unpack(buf, shapes)):
            outs[kind, n] = a
    result = [loss, dx.reshape(bl, seq, D_MODEL)]
    for kind in KINDS:
        result += [outs[kind, n] for n in WEIGHTS]
    return tuple(result)
```

```python
import functools
import math

import jax
import jax.numpy as jnp
from jax import lax
from jax.experimental import pallas as pl
from jax.experimental.pallas import tpu as pltpu

F32 = jnp.float32
BF16 = jnp.bfloat16
HIGHEST = lax.Precision.HIGHEST
NN = (((1,), (0,)), ((), ()))
NT = (((1,), (1,)), ((), ()))
TN = (((0,), (0,)), ((), ()))

D_MODEL = 1024
DEPTH = 4
EPS = 1e-6
N_MEM = 256
M_D_INNER = 2048
M_HEADS = 32
M_GROUPS = 8
M_STATE = 128
M_CONV_DIM = 4096
M_IN = 6176
M_IN_PAD = 6272
SSD_CHUNK = 256
H_HEADS = 8
HGRN_CHUNK = 32
HGRN_ROWS = 128
G_QK_HEADS = 8
G_V_HEADS = 16
G_KEY_DIM = 1024
G_VAL_DIM = 2048
G_CONV_DIM = 4096
G_IN = 6176
G_IN_PAD = 6272
GDN_CHUNK = 64
X_HEADS = 4
X_HEAD_DIM = 256
D_FF = 2816
ADAM_LR = 0.001
ADAM_B1 = 0.9
ADAM_B2 = 0.999
ADAM_EPS = 1e-08
ADAM_WD = 0.01
ADAM_STEP = 10

N_DEV = 8
LANE = 128
KINDS = ('grad', 'delta', 'new_m', 'new_v')
VMEM_LIMIT = 56 * 1024 * 1024

WEIGHTS = ['ln_mix', 'ln_xattn', 'ln_mem', 'ln_ffn', 'final_norm', 'm_in_w', 'm_conv_w', 'm_conv_b', 'm_dt_bias',
           'm_a_log', 'm_d', 'm_norm_w', 'm_out_w', 'h_in_w', 'h_lower_bounds', 'h_norm_w', 'h_out_w', 'g_in_w',
           'g_conv_w', 'g_a_log', 'g_dt_bias', 'g_norm_w', 'g_out_w', 'xa_q', 'xa_kv', 'xa_o', 'f_up', 'f_conv_w',
           'f_conv_b', 'f_down']
SHARD_AXIS = {'m_in_w': 2, 'm_conv_w': 2, 'm_conv_b': 1, 'm_norm_w': 1, 'm_out_w': 1, 'h_in_w': 2, 'h_out_w': 1,
              'g_in_w': 2, 'g_conv_w': 2, 'g_out_w': 1, 'xa_q': 1, 'xa_kv': 2, 'xa_o': 1, 'f_up': 2, 'f_conv_w': 2,
              'f_down': 1}
MATMUL_WEIGHTS = ['m_in_w', 'm_out_w', 'h_in_w', 'h_out_w', 'g_in_w', 'g_out_w', 'xa_q', 'xa_kv', 'xa_o', 'f_up',
                  'f_down']
SMALL_SHARDED = ['m_conv_w', 'm_conv_b', 'm_norm_w', 'g_conv_w', 'f_conv_w']


def _cparams():
    return pltpu.CompilerParams(vmem_limit_bytes=VMEM_LIMIT)


def bdot(a, b, dims=NN):
    return lax.dot_general(a.astype(BF16), b.astype(BF16), dims, preferred_element_type=F32)


def _split(a):
    hi = a.astype(BF16)
    return hi, (a - hi.astype(F32)).astype(BF16)


def _h3(a, b, dims):
    ah, al = _split(a)
    bh, bl = _split(b)
    d = functools.partial(lax.dot_general, dimension_numbers=dims, preferred_element_type=F32)
    return d(ah, bh) + (d(ah, bl) + d(al, bh))


@jax.custom_vjp
def h3dot(a, b):
    return _h3(a, b, NN)


h3dot.defvjp(lambda a, b: (_h3(a, b, NN), (a, b)),
             lambda res, ct: (_h3(ct, res[1], NT), _h3(res[0], ct, TN)))

T_ROWS = (((0,), (1,)), ((), ()))


def _tri_times(tri, x, dims, tri_first):
    t = tri.astype(BF16)
    x0 = x.astype(BF16)
    r1 = x - x0.astype(F32)
    x1 = r1.astype(BF16)
    x2 = (r1 - x1.astype(F32)).astype(BF16)
    if tri_first:
        d = lambda xx: lax.dot_general(t, xx, dims, preferred_element_type=F32)
    else:
        d = lambda xx: lax.dot_general(xx, t, dims, preferred_element_type=F32)
    return d(x0) + (d(x1) + d(x2))


@jax.custom_vjp
def cumdot(tri, x):
    return _tri_times(tri, x, NN, True)


cumdot.defvjp(lambda tri, x: (_tri_times(tri, x, NN, True), tri),
              lambda tri, ct: (jnp.zeros_like(tri), _tri_times(tri, ct, TN, True)))


@jax.custom_vjp
def cumdot_t(tri, x):
    return _tri_times(tri, x, T_ROWS, False)


cumdot_t.defvjp(lambda tri, x: (_tri_times(tri, x, T_ROWS, False), tri),
                lambda tri, ct: (jnp.zeros_like(tri), _tri_times(tri, ct, T_ROWS, True)))


def _tile(dim, cap):
    if dim <= cap:
        return dim
    best = None
    for t in range(LANE, cap + 1, LANE):
        if dim % t == 0:
            best = t
    assert best is not None, dim
    return best


def _position():
    return lax.axis_index("x"), lax.axis_index("y"), lax.axis_index("c")


def _direct_copies(kind, src_ref, dst_ref, send_sems, recv_sems, local_sem):
    x, y, c = _position()
    me = 4 * x + 2 * y + c
    local_src = src_ref if kind == 'gather' else src_ref.at[me]
    copies = [pltpu.make_async_copy(local_src, dst_ref.at[me], local_sem)]
    for k in range(1, N_DEV):
        px = 1 - x if (k >> 2) & 1 else x
        py = 1 - y if (k >> 1) & 1 else y
        pc = 1 - c if k & 1 else c
        copies.append(pltpu.make_async_remote_copy(
            src_ref=src_ref if kind == 'gather' else src_ref.at[4 * px + 2 * py + pc], dst_ref=dst_ref.at[me],
            send_sem=send_sems.at[k - 1], recv_sem=recv_sems.at[k - 1],
            device_id=(px, py, pc), device_id_type=pl.DeviceIdType.MESH))
    return copies


COMM_SCRATCH = [pltpu.SemaphoreType.DMA((N_DEV - 1,)), pltpu.SemaphoreType.DMA((N_DEV - 1,)), pltpu.SemaphoreType.DMA]


def matmul(a, b, *, ta=False, tb=False, out_dtype=F32, name="mm", carry=None):
    if ta:
        k, m = a.shape
    else:
        m, k = a.shape
    if tb:
        n, k2 = b.shape
    else:
        k2, n = b.shape
    assert k == k2, (a.shape, b.shape, ta, tb)
    tm = _tile(m, 512)
    tn = _tile(n, 1408)
    tk = _tile(k, 1408)
    grid = (m // tm, n // tn, k // tk)
    nk = grid[2]
    dims = (((0 if ta else 1,), (1 if tb else 0,)), ((), ()))

    def at_step(which):
        conds = [pl.program_id(ax) == (0 if which == 'first' else grid[ax] - 1) for ax in range(3)]
        return jnp.logical_and(jnp.logical_and(conds[0], conds[1]), conds[2])

    def body(*refs):
        if carry is None:
            a_ref, b_ref, o_ref, acc_ref = refs
        else:
            a_ref, b_ref, src_ref, o_ref, dst_ref, acc_ref, send_sems, recv_sems, local_sem = refs
            copies = lambda: _direct_copies(carry[0], src_ref, dst_ref, send_sems, recv_sems, local_sem)

            @pl.when(at_step('first'))
            def _():
                for cp in copies():
                    cp.start()

        @pl.when(pl.program_id(2) == 0)
        def _():
            acc_ref[...] = jnp.zeros_like(acc_ref)

        acc_ref[...] += lax.dot_general(a_ref[...].astype(BF16), b_ref[...].astype(BF16), dims,
                                        preferred_element_type=F32)

        @pl.when(pl.program_id(2) == nk - 1)
        def _():
            o_ref[...] = acc_ref[...].astype(o_ref.dtype)

        if carry is not None:
            @pl.when(at_step('last'))
            def _():
                for cp in copies():
                    cp.wait()

    a_spec = pl.BlockSpec((tk, tm), lambda i, j, kk: (kk, i)) if ta else pl.BlockSpec((tm, tk), lambda i, j, kk: (i, kk))
    b_spec = pl.BlockSpec((tn, tk), lambda i, j, kk: (j, kk)) if tb else pl.BlockSpec((tk, tn), lambda i, j, kk: (kk, j))
    o_spec = pl.BlockSpec((tm, tn), lambda i, j, kk: (i, j))
    o_shape = jax.ShapeDtypeStruct((m, n), out_dtype)
    acc = pltpu.VMEM((tm, tn), F32)
    if carry is None:
        return pl.pallas_call(
            body, name=name, grid=grid, in_specs=[a_spec, b_spec], out_specs=o_spec, out_shape=o_shape,
            scratch_shapes=[acc], compiler_params=_cparams(),
        )(a, b)
    kind, src = carry
    got = jax.ShapeDtypeStruct(((N_DEV,) + src.shape) if kind == 'gather' else src.shape, src.dtype)
    hbm = pl.BlockSpec(memory_space=pl.ANY)
    return pl.pallas_call(
        body, name=name, grid=grid, in_specs=[a_spec, b_spec, hbm], out_specs=[o_spec, hbm],
        out_shape=[o_shape, got], scratch_shapes=[acc] + COMM_SCRATCH, compiler_params=_cparams(),
    )(a, b, src)


def make_linear(name, shard_axis, n_real):
    def forward(a, w, nxt):
        if nxt:
            y, got = matmul(a, w, name=name + "_fwd", carry=('gather', nxt[0]))
            return y, (got,)
        return matmul(a, w, name=name + "_fwd"), ()

    @jax.custom_vjp
    def linear(a, w, wg, nxt):
        return forward(a, w, nxt)

    def fwd(a, w, wg, nxt):
        return forward(a, w, nxt), (a, w, nxt)

    def bwd(res, cts):
        a, w, nxt = res
        dy = cts[0]
        dw = matmul(a, dy, ta=True, name=name + "_bwd_dw")
        slabs = _shards_of_full(dw[:, :n_real], shard_axis).astype(BF16)
        da, slots = matmul(dy, w, tb=True, out_dtype=a.dtype, name=name + "_bwd_da", carry=('exchange', slabs))
        return da, jnp.zeros_like(w), slots, tuple(jnp.zeros_like(b) for b in nxt)

    linear.defvjp(fwd, bwd)
    return linear


class In:
    def __init__(self, block, imap, kind='blk', inner=(), cols=None):
        self.block, self.imap, self.kind, self.inner, self.cols = block, imap, kind, inner, cols


class Out:
    def __init__(self, shape, dtype, block, imap):
        self.shape, self.dtype, self.block, self.imap = shape, dtype, block, imap


def make_op(name, fn, grid, ins, outs, state_shape=None, seq_axis=None):
    n_in, n_out = len(ins), len(outs)
    has_state = state_shape is not None
    nd = len(grid)
    diff_idx = [i for i, s in enumerate(ins) if s.kind != 'const']

    def in_spec(s, reverse):
        off = 0
        if s.cols is not None:
            assert s.cols[0] % s.block[-1] == 0
            off = s.cols[0] // s.block[-1]

        def imap(*ids):
            ids = rev(ids) if reverse else ids
            idx = tuple(s.imap(*ids))
            return idx[:-1] + (idx[-1] + off,) if off else idx

        return pl.BlockSpec(s.block, imap)

    def rel_spec(block, f, reverse):
        return pl.BlockSpec(block, (lambda *ids: f(*rev(ids))) if reverse else f)

    def rev(ids):
        if not has_state:
            return ids
        ids = list(ids)
        ids[seq_axis] = grid[seq_axis] - 1 - ids[seq_axis]
        return tuple(ids)

    save_shape = tuple(grid) + tuple(state_shape) if has_state else None
    save_block = (None,) * nd + tuple(state_shape) if has_state else None

    def save_imap(*ids):
        return tuple(ids) + (0,) * len(state_shape)

    def fwd_call(*xs):
        def body(*refs):
            in_refs = refs[:n_in]
            out_refs = refs[n_in:n_in + n_out]
            vals = [r[...] for r in in_refs]
            if has_state:
                save_ref, st_ref = refs[n_in + n_out], refs[n_in + n_out + 1]

                @pl.when(pl.program_id(seq_axis) == 0)
                def _():
                    st_ref[...] = jnp.zeros(state_shape, F32)

                st = st_ref[...]
                save_ref[...] = st
                res = fn(*vals, st)
                st_ref[...] = res[-1]
                res = res[:-1]
            else:
                res = fn(*vals)
            for o, v in zip(out_refs, res):
                o[...] = v.astype(o.dtype)

        out_shape = [jax.ShapeDtypeStruct(o.shape, o.dtype) for o in outs]
        out_specs = [pl.BlockSpec(o.block, o.imap) for o in outs]
        scratch = []
        if has_state:
            out_shape.append(jax.ShapeDtypeStruct(save_shape, F32))
            out_specs.append(pl.BlockSpec(save_block, save_imap))
            scratch.append(pltpu.VMEM(state_shape, F32))
        return pl.pallas_call(
            body, name=name + "_fwd", grid=grid,
            in_specs=[in_spec(s, False) for s in ins],
            out_specs=out_specs, out_shape=out_shape, scratch_shapes=scratch,
            compiler_params=_cparams(),
        )(*xs)

    def grad_shape(s, x):
        if s.cols is not None:
            return x.shape[:-1] + (s.cols[1],)
        return x.shape

    def bwd_call(xs, save, cts):
        n_diff = len(diff_idx)

        def body(*refs):
            in_refs = refs[:n_in]
            p = n_in
            if has_state:
                save_ref = refs[p]
                p += 1
            ct_refs = refs[p:p + n_out]
            p += n_out
            g_refs = refs[p:p + n_diff]
            p += n_diff
            vals = [r[...] for r in in_refs]

            def g(*dv):
                full = list(vals)
                for i, v in zip(diff_idx, dv):
                    full[i] = v
                if has_state:
                    return tuple(fn(*full, dv[-1]))
                return tuple(fn(*full))

            prim = [vals[i] for i in diff_idx]
            ct = tuple(r[...].astype(F32) for r in ct_refs)
            if has_state:
                dst_ref = refs[p]

                @pl.when(pl.program_id(seq_axis) == 0)
                def _():
                    dst_ref[...] = jnp.zeros(state_shape, F32)

                prim = prim + [save_ref[...]]
                ct = ct + (dst_ref[...],)
            _, vjp = jax.vjp(g, *prim)
            grads = vjp(ct)
            for k, i in enumerate(diff_idx):
                s = ins[i]
                if s.kind == 'blk':
                    g_refs[k][...] = grads[k].astype(g_refs[k].dtype)
                else:
                    first = None
                    for ax in s.inner:
                        c = pl.program_id(ax) == 0
                        first = c if first is None else jnp.logical_and(first, c)

                    @pl.when(first)
                    def _(k=k):
                        g_refs[k][...] = jnp.zeros_like(g_refs[k])

                    g_refs[k][...] += grads[k].astype(g_refs[k].dtype)
            if has_state:
                dst_ref[...] = grads[-1]

        in_specs = [in_spec(s, True) for s in ins]
        args = list(xs)
        if has_state:
            in_specs.append(rel_spec(save_block, save_imap, True))
            args.append(save)
        for o, c in zip(outs, cts):
            in_specs.append(rel_spec(o.block, o.imap, True))
            args.append(c)
        out_shape, out_specs = [], []
        for i in diff_idx:
            s = ins[i]
            out_shape.append(jax.ShapeDtypeStruct(grad_shape(s, xs[i]), xs[i].dtype))
            out_specs.append(rel_spec(s.block, s.imap, True))
        scratch = [pltpu.VMEM(state_shape, F32)] if has_state else []
        return pl.pallas_call(
            body, name=name + "_bwd", grid=grid,
            in_specs=in_specs, out_specs=out_specs, out_shape=out_shape, scratch_shapes=scratch,
            compiler_params=_cparams(),
        )(*args)

    @jax.custom_vjp
    def op(*xs):
        return tuple(fwd_call(*xs)[:n_out])

    def op_fwd(*xs):
        res = fwd_call(*xs)
        return tuple(res[:n_out]), (xs, res[n_out] if has_state else None)

    def op_bwd(resid, cts):
        xs, save = resid
        grads = bwd_call(xs, save, cts)
        out = []
        k = 0
        for i, s in enumerate(ins):
            if s.kind == 'const':
                out.append(jnp.zeros_like(xs[i]))
                continue
            g = grads[k]
            k += 1
            if s.cols is not None:
                g = jnp.pad(g, ((0, 0),) * (g.ndim - 1) + ((s.cols[0], xs[i].shape[-1] - s.cols[0] - s.cols[1]),))
            out.append(g)
        return tuple(out)

    op.defvjp(op_fwd, op_bwd)
    return op


def _rms(x, w):
    return x * lax.rsqrt(jnp.mean(x * x, axis=-1, keepdims=True) + EPS) * w


def _silu(x):
    return x * jax.nn.sigmoid(x)


def rmsnorm_op(name, t, out_dtype):
    tm = _tile(t, 512)
    return make_op(
        name, lambda x, w: (_rms(x, w),), (t // tm,),
        [In((tm, D_MODEL), lambda i: (i, 0)), In((1, D_MODEL), lambda i: (0, 0), 'acc', (0,))],
        [Out((t, D_MODEL), out_dtype, (tm, D_MODEL), lambda i: (i, 0))])


def _tri(q):
    ii = lax.broadcasted_iota(jnp.int32, (q, q), 0)
    jj = lax.broadcasted_iota(jnp.int32, (q, q), 1)
    return ii >= jj, ii > jj


def _ssd_fn(z, x, bm, cm, dtr, dtb, alog, dsk, nw, state):
    q = x.shape[0]
    incl, _ = _tri(q)
    tril = incl.astype(F32)
    dt = jax.nn.softplus(dtr + dtb)
    da = dt * (-jnp.exp(alog))
    acum = cumdot(tril, da)
    acum_t = cumdot_t(tril, da)
    cb = bdot(cm, bm, NT)
    ys, sts = [], []
    for r in range(4):
        col = acum[:, r:r + 1]
        row = acum_t[r:r + 1, :]
        decay = jnp.exp(jnp.where(incl, col - row, -jnp.inf))
        xr = x[:, 64 * r:64 * r + 64]
        xc = xr * dt[:, r:r + 1]
        st = state[64 * r:64 * r + 64, :]
        y = bdot(cb * decay, xc) + bdot(cm, st, NT) * jnp.exp(col)
        last = acum[q - 1:q, r:r + 1]
        ds = bdot(xc * jnp.exp(last - col), bm, TN)
        sts.append(st * jnp.exp(last) + ds)
        ys.append(y + dsk[:, r:r + 1] * xr)
    y = jnp.concatenate(ys, axis=1)
    yz = y * _silu(z)
    return _rms(yz, nw), jnp.concatenate(sts, axis=0)


def _per_sequence(fn, n_seq_args, bl):
    def f(*args):
        *ins, state = args
        res = [fn(*[a[b] for a in ins[:n_seq_args]], *ins[n_seq_args:], state[b]) for b in range(bl)]
        return tuple(jnp.concatenate([r[k][None] for r in res]) for k in range(len(res[0])))

    return f


def ssd_op(name, bl, seq):
    q = SSD_CHUNK
    nc = seq // q
    blk = lambda w, c0, cw: In((bl, q, w), lambda g, n: (0, n, g), cols=(c0, cw))
    small = lambda g, n: (g, 0, 0)
    ins = [
        blk(256, 0, M_D_INNER),
        blk(256, 0, M_D_INNER),
        blk(128, M_D_INNER, 1024),
        blk(128, M_D_INNER + 1024, 1024),
        In((None, bl, q, 4), lambda g, n: (g, 0, n, 0)),
        In((None, 1, 4), small, 'acc', (1,)),
        In((None, 1, 4), small, 'acc', (1,)),
        In((None, 1, 4), small, 'acc', (1,)),
        In((None, 1, 256), small, 'acc', (1,)),
    ]
    outs = [Out((bl, seq, M_D_INNER), F32, (bl, q, 256), lambda g, n: (0, n, g))]
    return make_op(name, _per_sequence(_ssd_fn, 5, bl), (M_GROUPS, nc), ins, outs,
                   state_shape=(bl, 256, 128), seq_axis=1)


def _gla_fn(layer, qr, fr, ir, gr, lbp, nw, state_t):
    rows = qr.shape[0]
    c = HGRN_CHUNK
    e = jnp.exp(lbp - jnp.max(lbp, axis=0, keepdims=True))
    sm = e / jnp.sum(e, axis=0, keepdims=True)
    lb = jnp.sum(sm[1:layer + 1, :], axis=0, keepdims=True) if layer > 0 else jnp.zeros((1, lbp.shape[1]), F32)
    qq = _silu(qr) * (128 ** -0.5)
    forget = lb + (1.0 - lb) * jax.nn.sigmoid(fr)
    kk = 1.0 - forget
    logf = jnp.log(forget)
    incl, _ = _tri(c)
    tril = incl.astype(F32)
    os_ = []
    for j in range(rows // c):
        sl = slice(c * j, c * j + c)
        gc = cumdot(tril, logf[sl])
        glast = gc[c - 1:c, :]
        q_dec = qq[sl] * jnp.exp(gc)
        k_inv = kk[sl] * jnp.exp(-gc)
        k_end = kk[sl] * jnp.exp(glast - gc)
        att = jnp.where(incl, bdot(q_dec, k_inv, NT), 0.0)
        os_.append(bdot(att, ir[sl]) + bdot(q_dec, state_t, NT))
        state_t = state_t * jnp.exp(glast) + bdot(ir[sl], k_end, TN)
    o = jnp.concatenate(os_, axis=0)
    return _rms(o, nw) * _silu(gr), state_t


def gla_op(name, layer, bl, seq):
    r = HGRN_ROWS
    ns = seq // r
    blk = lambda k: In((bl, r, 128), lambda h, n: (0, n, h), cols=(1024 * k, 1024))
    ins = [blk(0), blk(1), blk(2), blk(3),
           In((DEPTH, 128), lambda h, n: (0, h), 'acc', (1,)),
           In((1, 128), lambda h, n: (0, 0), 'acc', (0, 1))]
    outs = [Out((bl, seq, D_MODEL), F32, (bl, r, 128), lambda h, n: (0, n, h))]
    return make_op(name, _per_sequence(functools.partial(_gla_fn, layer), 4, bl), (H_HEADS, ns), ins, outs,
                   state_shape=(bl, 128, 128), seq_axis=1)


def _unit_lower_inverse(m, nilpotent):
    q = m.shape[0]
    ii = lax.broadcasted_iota(jnp.int32, (q, q), 0)
    jj = lax.broadcasted_iota(jnp.int32, (q, q), 1)
    eye = (ii == jj).astype(F32)
    p = -m
    inv = eye + p
    steps = int(math.log2(nilpotent)) - 1
    for _ in range(steps):
        p = h3dot(p, p)
        inv = inv + h3dot(inv, p)
    return inv


def _gdn_fn(qc, kc, vc, zc, br, ar, alog, dtb, nw, state):
    bl, q = qc.shape[0], qc.shape[1]
    incl, strict = _tri(q)
    tril = incl.astype(F32)
    g = jnp.concatenate([-jnp.exp(alog) * jax.nn.softplus(ar[b] + dtb) for b in range(bl)], axis=1)
    gc = cumdot(tril, g)
    gc_t = cumdot_t(tril, g)
    heads, ms, rhs = [], [], []
    for b in range(bl):
        qn = qc[b] * lax.rsqrt(jnp.sum(qc[b] * qc[b], axis=-1, keepdims=True) + EPS) * (128 ** -0.5)
        kn = kc[b] * lax.rsqrt(jnp.sum(kc[b] * kc[b], axis=-1, keepdims=True) + EPS)
        beta = jax.nn.sigmoid(br[b])
        qk = bdot(qn, kn, NT)
        for j in range(2):
            i = 2 * b + j
            col = gc[:, i:i + 1]
            decay = jnp.exp(jnp.where(incl, col - gc_t[i:i + 1, :], -jnp.inf))
            bj = beta[:, j:j + 1]
            kb = kn * bj
            ms.append(jnp.where(strict, bdot(kb, kn, NT) * decay, 0.0))
            rhs.append(jnp.concatenate([vc[b][:, 128 * j:128 * j + 128] * bj, kb * jnp.exp(col)], axis=1))
            heads.append((qn, kn, qk * decay, col, gc[q - 1:q, i:i + 1]))
    n = len(ms)
    zero = jnp.zeros((q, q), F32)
    m_all = jnp.concatenate(
        [jnp.concatenate([ms[i] if i == k else zero for i in range(n)], axis=1) for k in range(n)], axis=0)
    sol = h3dot(_unit_lower_inverse(m_all, q), jnp.concatenate(rhs, axis=0))
    outs, states = [], []
    for b in range(bl):
        os_, sts = [], []
        for j in range(2):
            i = 2 * b + j
            qn, kn, att, col, glast = heads[i]
            u = sol[q * i:q * i + q, :128]
            w = sol[q * i:q * i + q, 128:]
            st = state[b][128 * j:128 * j + 128, :]
            v_new = u - bdot(w, st)
            o = bdot(qn * jnp.exp(col), st) + bdot(att, v_new)
            sts.append(st * jnp.exp(glast) + bdot(kn * jnp.exp(glast - col), v_new, TN))
            os_.append(_rms(o, nw) * _silu(zc[b][:, 128 * j:128 * j + 128]))
        outs.append(jnp.concatenate(os_, axis=1))
        states.append(jnp.concatenate(sts, axis=0))
    return jnp.concatenate([o[None] for o in outs]), jnp.concatenate([st[None] for st in states])


def gdn_op(name, bl, seq):
    q = GDN_CHUNK
    nc = seq // q
    blk = lambda w, c0, cw: In((bl, q, w), lambda h, n: (0, n, h), cols=(c0, cw))
    small = lambda h, n: (h, 0, 0)
    ins = [
        blk(128, 0, G_KEY_DIM),
        blk(128, G_KEY_DIM, G_KEY_DIM),
        blk(256, 2 * G_KEY_DIM, G_VAL_DIM),
        blk(256, G_CONV_DIM, G_VAL_DIM),
        In((None, bl, q, 2), lambda h, n: (h, 0, n, 0)),
        In((None, bl, q, 2), lambda h, n: (h, 0, n, 0)),
        In((None, 1, 2), small, 'acc', (1,)),
        In((None, 1, 2), small, 'acc', (1,)),
        In((1, 128), lambda h, n: (0, 0), 'acc', (0, 1)),
    ]
    outs = [Out((bl, seq, G_VAL_DIM), F32, (bl, q, 256), lambda h, n: (0, n, h))]
    return make_op(name, _gdn_fn, (G_QK_HEADS, nc), ins, outs,
                   state_shape=(bl, 256, 128), seq_axis=1)


def _xattn_fn(q, k, v):
    s = bdot(q, k, NT) * (X_HEAD_DIM ** -0.5)
    s = s - jnp.max(s, axis=-1, keepdims=True)
    p = jnp.exp(s)
    p = p / jnp.sum(p, axis=-1, keepdims=True)
    return (bdot(p, v),)


def xattn_op(name, bl, seq):
    tq = _tile(seq, 512)
    nq = seq // tq
    t = bl * seq
    ins = [
        In((tq, X_HEAD_DIM), lambda b, h, i: (b * nq + i, h)),
        In((N_MEM, X_HEAD_DIM), lambda b, h, i: (b, h), 'acc', (2,), cols=(0, D_MODEL)),
        In((N_MEM, X_HEAD_DIM), lambda b, h, i: (b, h), 'acc', (2,), cols=(D_MODEL, D_MODEL)),
    ]
    outs = [Out((t, D_MODEL), F32, (tq, X_HEAD_DIM), lambda b, h, i: (b * nq + i, h))]
    return make_op(name, _xattn_fn, (bl, X_HEADS, nq), ins, outs)


CONV_PAD = 8


def make_conv(name, bl, seq, width, ch, x_col0, up_col0=None):
    cb = 256
    assert ch % cb == 0 and x_col0 % cb == 0 and (up_col0 is None or up_col0 % cb == 0)
    nb = ch // cb
    t = bl * seq
    has_up = up_col0 is not None
    grid = (nb, bl)
    x_spec = pl.BlockSpec((seq, cb), lambda c, b: (b, x_col0 // cb + c))
    up_specs = [pl.BlockSpec((seq, cb), lambda c, b: (b, up_col0 // cb + c))] if has_up else []
    w_spec = pl.BlockSpec((width, cb), lambda c, b: (0, c))
    b_spec = pl.BlockSpec((1, cb), lambda c, b: (0, c))
    o_spec = pl.BlockSpec((seq, cb), lambda c, b: (b, c))
    taps = [CONV_PAD - (width - 1) + j for j in range(width)]

    def pre_activation(x_ref, w_ref, b_ref, pad_ref):
        pad_ref[0:CONV_PAD, :] = jnp.zeros((CONV_PAD, cb), F32)
        pad_ref[CONV_PAD:CONV_PAD + seq, :] = x_ref[...]
        w = w_ref[...]
        y = b_ref[...] + jnp.zeros((seq, cb), F32)
        for j in range(width):
            y = y + w[j:j + 1, :] * pad_ref[pl.ds(taps[j], seq), :]
        return y

    def fwd_call(x, w, b):
        def body(*refs):
            x_ref, w_ref, b_ref = refs[:3]
            o_ref, pad_ref = refs[-2:]
            y = _silu(pre_activation(x_ref, w_ref, b_ref, pad_ref))
            if has_up:
                y = y * refs[3][...]
            o_ref[...] = y

        return pl.pallas_call(
            body, name=name + "_fwd", grid=grid,
            in_specs=[x_spec, w_spec, b_spec] + up_specs, out_specs=o_spec,
            out_shape=jax.ShapeDtypeStruct((t, ch), F32),
            scratch_shapes=[pltpu.VMEM((seq + CONV_PAD, cb), F32)],
            compiler_params=_cparams(),
        )(*([x, w, b] + ([x] if has_up else [])))

    def bwd_call(x, w, b, do):
        n_in = 4 + (1 if has_up else 0)

        def body(*refs):
            x_ref, w_ref, b_ref = refs[:3]
            do_ref = refs[n_in - 1]
            dx_ref, dw_ref, db_ref = refs[n_in:n_in + 3]
            pad_ref, gpad_ref = refs[-2:]
            y = pre_activation(x_ref, w_ref, b_ref, pad_ref)
            s = jax.nn.sigmoid(y)
            act = y * s
            do = do_ref[...]
            if has_up:
                refs[n_in + 3][...] = do * act
                do = do * refs[3][...]
            dy = do * (s + act * (1.0 - s))
            gpad_ref[0:seq, :] = dy
            gpad_ref[seq:seq + CONV_PAD, :] = jnp.zeros((CONV_PAD, cb), F32)
            w_ = w_ref[...]
            dx = jnp.zeros((seq, cb), F32)
            dws = []
            for j in range(width):
                dx = dx + w_[j:j + 1, :] * gpad_ref[pl.ds(width - 1 - j, seq), :]
                dws.append(jnp.sum(dy * pad_ref[pl.ds(taps[j], seq), :], axis=0, keepdims=True))
            dx_ref[...] = dx

            @pl.when(pl.program_id(1) == 0)
            def _():
                dw_ref[...] = jnp.zeros_like(dw_ref)
                db_ref[...] = jnp.zeros_like(db_ref)

            dw_ref[...] += jnp.concatenate(dws, axis=0)
            db_ref[...] += jnp.sum(dy, axis=0, keepdims=True)

        big = jax.ShapeDtypeStruct((t, ch), F32)
        return pl.pallas_call(
            body, name=name + "_bwd", grid=grid,
            in_specs=[x_spec, w_spec, b_spec] + up_specs + [o_spec],
            out_specs=[o_spec, w_spec, b_spec] + ([o_spec] if has_up else []),
            out_shape=[big, jax.ShapeDtypeStruct((width, ch), F32), jax.ShapeDtypeStruct((1, ch), F32)]
            + ([big] if has_up else []),
            scratch_shapes=[pltpu.VMEM((seq + CONV_PAD, cb), F32), pltpu.VMEM((seq + CONV_PAD, cb), F32)],
            compiler_params=_cparams(),
        )(*([x, w, b] + ([x] if has_up else []) + [do]))

    @jax.custom_vjp
    def conv(x, w, b):
        return fwd_call(x, w, b)

    def conv_fwd(x, w, b):
        return fwd_call(x, w, b), (x, w, b)

    def conv_bwd(res, do):
        x, w, b = res
        got = bwd_call(x, w, b, do)
        dx = jnp.pad(got[0], ((0, 0), (x_col0, x.shape[1] - x_col0 - ch)))
        if has_up:
            dx = dx + jnp.pad(got[3], ((0, 0), (up_col0, x.shape[1] - up_col0 - ch)))
        return dx, got[1], got[2]

    conv.defvjp(conv_fwd, conv_bwd)

    def apply(x, w, b=None):
        if b is None:
            b = jnp.zeros((ch,), F32)
        return conv(x, w, b.reshape(1, ch))

    return apply


def loss_head(x, w, target):
    t = x.shape[0]
    tm = _tile(t, 512)

    def fn(xb, wb, tb):
        err = _rms(xb, wb) - tb
        return 0.5 * jnp.sum(err * err) * (1.0 / D_MODEL)

    def body(x_ref, w_ref, t_ref, loss_ref, dx_ref, dw_ref):
        @pl.when(pl.program_id(0) == 0)
        def _():
            loss_ref[...] = jnp.zeros_like(loss_ref)
            dw_ref[...] = jnp.zeros_like(dw_ref)

        tb = t_ref[...]
        val, vjp = jax.vjp(lambda a, b: fn(a, b, tb), x_ref[...], w_ref[...])
        dx, dw = vjp(jnp.ones((), F32))
        dx_ref[...] = dx
        dw_ref[...] += dw
        loss_ref[...] += jnp.full(loss_ref.shape, val, F32)

    row = pl.BlockSpec((tm, D_MODEL), lambda i: (i, 0))
    vec = pl.BlockSpec((1, D_MODEL), lambda i: (0, 0))
    loss, dx, dw = pl.pallas_call(
        body, name="loss_head", grid=(t // tm,),
        in_specs=[row, vec, row],
        out_specs=[pl.BlockSpec((8, LANE), lambda i: (0, 0)), row, vec],
        out_shape=[jax.ShapeDtypeStruct((8, LANE), F32), jax.ShapeDtypeStruct((t, D_MODEL), F32),
                   jax.ShapeDtypeStruct((1, D_MODEL), F32)],
        compiler_params=_cparams(),
    )(x, w.reshape(1, D_MODEL), target)
    return loss[0, 0], dx, dw.reshape(D_MODEL)


PACK_W = 1024
ADAM_BLOCK_BYTES = 512 * 1024


def _rows_tile(r, c):
    if r * c * 4 <= ADAM_BLOCK_BYTES or r % 8:
        return r
    best = 8
    for t in range(8, r + 1, 8):
        if r % t == 0 and t * c * 4 <= ADAM_BLOCK_BYTES:
            best = t
    return best


def reduce_adamw(slots, w, m, v, name):
    r, wd = w.shape
    tr = _rows_tile(r, wd)
    c1 = 1.0 - ADAM_B1 ** ADAM_STEP
    c2 = 1.0 - ADAM_B2 ** ADAM_STEP

    def body(s_ref, w_ref, m_ref, v_ref, g_ref, d_ref, nm_ref, nv_ref):
        g = s_ref[0].astype(F32)
        for k in range(1, N_DEV):
            g = g + s_ref[k].astype(F32)
        nm = ADAM_B1 * m_ref[...] + (1.0 - ADAM_B1) * g
        nv = ADAM_B2 * v_ref[...] + (1.0 - ADAM_B2) * (g * g)
        m_hat = nm / c1
        v_hat = nv / c2
        d_ref[...] = -ADAM_LR * (m_hat / (jnp.sqrt(v_hat) + ADAM_EPS) + ADAM_WD * w_ref[...])
        g_ref[...] = g
        nm_ref[...] = nm
        nv_ref[...] = nv

    blk = pl.BlockSpec((tr, wd), lambda i: (i, 0))
    shp = jax.ShapeDtypeStruct((r, wd), F32)
    return pl.pallas_call(
        body, name=name, grid=(r // tr,),
        in_specs=[pl.BlockSpec((N_DEV, tr, wd), lambda i: (0, i, 0)), blk, blk, blk],
        out_specs=[blk, blk, blk, blk], out_shape=[shp, shp, shp, shp],
        compiler_params=_cparams(),
    )(slots, w, m, v)


def all_gather(block, name):
    def body(x_ref, out_ref, send_sems, recv_sems, local_sem):
        x, y, c = _position()
        me, sibling = (x, y, c), (x, y, 1 - c)
        chips = [(1 - x, y), (x, 1 - y), (1 - x, 1 - y)]

        def slot(px, py, pc):
            return out_ref.at[4 * px + 2 * py + pc]

        def copy(k, owner, to, src=None):
            return pltpu.make_async_remote_copy(
                src_ref=slot(*owner) if src is None else src, dst_ref=slot(*owner),
                send_sem=send_sems.at[k], recv_sem=recv_sems.at[k],
                device_id=to, device_id_type=pl.DeviceIdType.MESH)

        mine = pltpu.make_async_copy(x_ref, slot(*me), local_sem)
        mine.start()
        first = [copy(0, me, sibling, src=x_ref)]
        first += [copy(1 + j, me, (*chip, c), src=x_ref) for j, chip in enumerate(chips)]
        for cp in first:
            cp.start()
        passed = [copy(4 + j, (*chip, c), sibling) for j, chip in enumerate(chips)]
        for j, chip in enumerate(chips):
            copy(1 + j, (*chip, c), me).wait_recv()
            passed[j].start()
        copy(0, sibling, me).wait_recv()
        for j, chip in enumerate(chips):
            copy(4 + j, (*chip, 1 - c), me).wait_recv()
        for cp in first + passed:
            cp.wait_send()
        mine.wait()

    return pl.pallas_call(
        body, name=name,
        out_shape=jax.ShapeDtypeStruct((N_DEV,) + block.shape, block.dtype),
        in_specs=[pl.BlockSpec(memory_space=pl.ANY)],
        out_specs=pl.BlockSpec(memory_space=pl.ANY),
        scratch_shapes=[pltpu.SemaphoreType.DMA((7,)), pltpu.SemaphoreType.DMA((7,)), pltpu.SemaphoreType.DMA],
    )(block)


def exchange_slabs(slabs, name):
    def body(in_ref, out_ref, send_sems, recv_sems, local_sem):
        x, y, c = _position()
        my = 4 * x + 2 * y + c
        mine = pltpu.make_async_copy(in_ref.at[my], out_ref.at[my], local_sem)
        mine.start()
        copies = []
        for k in range(1, N_DEV):
            dx, dy, dc = (k >> 2) & 1, (k >> 1) & 1, k & 1
            px = x if dx == 0 else 1 - x
            py = y if dy == 0 else 1 - y
            pc = c if dc == 0 else 1 - c
            cp = pltpu.make_async_remote_copy(
                src_ref=in_ref.at[4 * px + 2 * py + pc], dst_ref=out_ref.at[my],
                send_sem=send_sems.at[k - 1], recv_sem=recv_sems.at[k - 1],
                device_id=(px, py, pc), device_id_type=pl.DeviceIdType.MESH)
            cp.start()
            copies.append(cp)
        for cp in copies:
            cp.wait()
        mine.wait()

    return pl.pallas_call(
        body, name=name,
        out_shape=jax.ShapeDtypeStruct(slabs.shape, slabs.dtype),
        in_specs=[pl.BlockSpec(memory_space=pl.ANY)],
        out_specs=pl.BlockSpec(memory_space=pl.ANY),
        scratch_shapes=[pltpu.SemaphoreType.DMA((7,)), pltpu.SemaphoreType.DMA((7,)), pltpu.SemaphoreType.DMA],
    )(slabs)


def _pack(arrays, dtype, row_multiple):
    flat = jnp.concatenate([a.astype(dtype).reshape(-1) for a in arrays])
    n = flat.shape[0]
    per = PACK_W * row_multiple
    total = -(-n // per) * per
    flat = jnp.pad(flat, (0, total - n))
    return flat.reshape(total // PACK_W, PACK_W)


def _unpack(flat2d, shapes, lead=()):
    flat = flat2d.reshape(lead + (-1,))
    out, off = [], 0
    for shp in shapes:
        n = math.prod(shp)
        out.append(flat[..., off:off + n].reshape(lead + tuple(shp)))
        off += n
    return out


def _full_from_gathered(g, axis):
    g = jnp.moveaxis(g, 0, axis)
    shp = list(g.shape)
    shp[axis:axis + 2] = [shp[axis] * shp[axis + 1]]
    return g.reshape(shp)


def _shards_of_full(full, axis):
    shp = list(full.shape)
    shp[axis:axis + 1] = [N_DEV, shp[axis] // N_DEV]
    return jnp.moveaxis(full.reshape(shp), axis, 0)


def layer_units(i):
    mixer = [('m_in_w', 'm_out_w'), ('h_in_w', 'h_out_w'), ('g_in_w', 'g_out_w')][i % 3]
    return [(mixer[0], i // 3), (mixer[1], i // 3), ('xa_q', i), ('xa_kv', i), ('xa_o', i), ('f_up', i), ('f_down', i)]


PADDED_COLS = {'m_in_w': M_IN_PAD, 'g_in_w': G_IN_PAD}


def whole_weight(name, gathered):
    w = _full_from_gathered(lax.stop_gradient(gathered), SHARD_AXIS[name] - 1)
    return _pad_cols(w, 1, PADDED_COLS[name]) if name in PADDED_COLS else w


def _trunk(p, weights, blocks, standins, x, mem, bl, seq):
    t = bl * seq
    ia = ib = ic = 0
    weights = dict(weights)
    state = {}

    def lin(name, a, wname, idx):
        unit = (wname, idx)
        pos = state['units'].index(unit)
        later = state['next'][pos] if state['next'] else None
        nxt = (blocks[later],) if later in blocks else ()
        n_real = N_DEV * standins[unit].shape[2]
        y, got = make_linear(name, SHARD_AXIS[wname] - 1, n_real)(a, weights[unit], standins[unit], nxt)
        if nxt:
            weights[later] = whole_weight(later[0], got[0])
        return y

    by_seq = lambda a: a.reshape(bl, seq, a.shape[-1])

    for i in range(DEPTH):
        state['units'] = layer_units(i)
        state['next'] = layer_units(i + 1) if i + 1 < DEPTH else None
        hn = rmsnorm_op(f"ln_mix{i}", t, F32)(x, p['ln_mix'][i:i + 1])[0]
        kind = i % 3
        if kind == 0:
            proj = lin(f"m_in{i}", hn, 'm_in_w', ia)
            xbc = make_conv(f"m_conv{i}", bl, seq, 4, M_CONV_DIM, M_D_INNER)(
                proj, p['m_conv_w'][ia], p['m_conv_b'][ia])
            dt = proj[:, M_D_INNER + M_CONV_DIM:M_IN].reshape(bl, seq, M_GROUPS, 4).transpose(2, 0, 1, 3)
            grp = lambda a, n=4: a.reshape(M_GROUPS, 1, n)
            proj3, xbc3 = by_seq(proj), by_seq(xbc)
            y = ssd_op(f"ssd{i}", bl, seq)(
                proj3, xbc3, xbc3, xbc3, dt, grp(p['m_dt_bias'][ia]), grp(p['m_a_log'][ia]), grp(p['m_d'][ia]),
                grp(p['m_norm_w'][ia], 256))[0]
            mix = lin(f"m_out{i}", y.reshape(t, M_D_INNER), 'm_out_w', ia)
            ia += 1
        elif kind == 1:
            proj3 = by_seq(lin(f"h_in{i}", hn, 'h_in_w', ib))
            y = gla_op(f"gla{i}", i, bl, seq)(
                proj3, proj3, proj3, proj3, p['h_lower_bounds'], p['h_norm_w'][ib:ib + 1])[0]
            mix = lin(f"h_out{i}", y.reshape(t, D_MODEL), 'h_out_w', ib)
            ib += 1
        else:
            proj = lin(f"g_in{i}", hn, 'g_in_w', ic)
            qkv = make_conv(f"g_conv{i}", bl, seq, 4, G_CONV_DIM, 0)(proj, p['g_conv_w'][ic])
            c0 = G_CONV_DIM + G_VAL_DIM
            heads = lambda a: a.reshape(bl, seq, G_QK_HEADS, 2).transpose(2, 0, 1, 3)
            braw = heads(proj[:, c0:c0 + G_V_HEADS])
            araw = heads(proj[:, c0 + G_V_HEADS:c0 + 2 * G_V_HEADS])
            grp = lambda a: a.reshape(G_QK_HEADS, 1, 2)
            qkv3 = by_seq(qkv)
            y = gdn_op(f"gdn{i}", bl, seq)(
                qkv3, qkv3, qkv3, by_seq(proj), braw, araw, grp(p['g_a_log'][ic]), grp(p['g_dt_bias'][ic]),
                p['g_norm_w'][ic:ic + 1])[0]
            mix = lin(f"g_out{i}", y.reshape(t, G_VAL_DIM), 'g_out_w', ic)
            ic += 1
        x = x + mix
        hq = rmsnorm_op(f"ln_xattn{i}", t, F32)(x, p['ln_xattn'][i:i + 1])[0]
        mn = rmsnorm_op(f"ln_mem{i}", bl * N_MEM, F32)(mem, p['ln_mem'][i:i + 1])[0]
        qx = lin(f"xa_q{i}", hq, 'xa_q', i)
        kv = lin(f"xa_kv{i}", mn, 'xa_kv', i)
        ao = xattn_op(f"xattn{i}", bl, seq)(qx, kv, kv)[0]
        x = x + lin(f"xa_o{i}", ao, 'xa_o', i)
        hf = rmsnorm_op(f"ln_ffn{i}", t, F32)(x, p['ln_ffn'][i:i + 1])[0]
        up = lin(f"f_up{i}", hf, 'f_up', i)
        act = make_conv(f"f_conv{i}", bl, seq, 3, D_FF, 0, up_col0=D_FF)(up, p['f_conv_w'][i], p['f_conv_b'][i])
        x = x + lin(f"f_down{i}", act, 'f_down', i)
    return x


def _pad_cols(w, axis, to):
    pad = [(0, 0)] * w.ndim
    pad[axis] = (0, to - w.shape[axis])
    return jnp.pad(w, pad)


def kernel(x, mem, ln_mix, ln_xattn, ln_mem, ln_ffn, final_norm, m_in_w, m_conv_w, m_conv_b, m_dt_bias, m_a_log, m_d, m_norm_w, m_out_w, h_in_w, h_lower_bounds, h_norm_w, h_out_w, g_in_w, g_conv_w, g_a_log, g_dt_bias, g_norm_w, g_out_w, xa_q, xa_kv, xa_o, f_up, f_conv_w, f_conv_b, f_down, loss_target, m_ln_mix, m_ln_xattn, m_ln_mem, m_ln_ffn, m_final_norm, m_m_in_w, m_m_conv_w, m_m_conv_b, m_m_dt_bias, m_m_a_log, m_m_d, m_m_norm_w, m_m_out_w, m_h_in_w, m_h_lower_bounds, m_h_norm_w, m_h_out_w, m_g_in_w, m_g_conv_w, m_g_a_log, m_g_dt_bias, m_g_norm_w, m_g_out_w, m_xa_q, m_xa_kv, m_xa_o, m_f_up, m_f_conv_w, m_f_conv_b, m_f_down, v_ln_mix, v_ln_xattn, v_ln_mem, v_ln_ffn, v_final_norm, v_m_in_w, v_m_conv_w, v_m_conv_b, v_m_dt_bias, v_m_a_log, v_m_d, v_m_norm_w, v_m_out_w, v_h_in_w, v_h_lower_bounds, v_h_norm_w, v_h_out_w, v_g_in_w, v_g_conv_w, v_g_a_log, v_g_dt_bias, v_g_norm_w, v_g_out_w, v_xa_q, v_xa_kv, v_xa_o, v_f_up, v_f_conv_w, v_f_conv_b, v_f_down):
    local = dict(ln_mix=ln_mix, ln_xattn=ln_xattn, ln_mem=ln_mem, ln_ffn=ln_ffn, final_norm=final_norm, m_in_w=m_in_w, m_conv_w=m_conv_w, m_conv_b=m_conv_b, m_dt_bias=m_dt_bias, m_a_log=m_a_log, m_d=m_d, m_norm_w=m_norm_w, m_out_w=m_out_w, h_in_w=h_in_w, h_lower_bounds=h_lower_bounds, h_norm_w=h_norm_w, h_out_w=h_out_w, g_in_w=g_in_w, g_conv_w=g_conv_w, g_a_log=g_a_log, g_dt_bias=g_dt_bias, g_norm_w=g_norm_w, g_out_w=g_out_w, xa_q=xa_q, xa_kv=xa_kv, xa_o=xa_o, f_up=f_up, f_conv_w=f_conv_w, f_conv_b=f_conv_b, f_down=f_down)
    mom_m = dict(ln_mix=m_ln_mix, ln_xattn=m_ln_xattn, ln_mem=m_ln_mem, ln_ffn=m_ln_ffn, final_norm=m_final_norm, m_in_w=m_m_in_w, m_conv_w=m_m_conv_w, m_conv_b=m_m_conv_b, m_dt_bias=m_m_dt_bias, m_a_log=m_m_a_log, m_d=m_m_d, m_norm_w=m_m_norm_w, m_out_w=m_m_out_w, h_in_w=m_h_in_w, h_lower_bounds=m_h_lower_bounds, h_norm_w=m_h_norm_w, h_out_w=m_h_out_w, g_in_w=m_g_in_w, g_conv_w=m_g_conv_w, g_a_log=m_g_a_log, g_dt_bias=m_g_dt_bias, g_norm_w=m_g_norm_w, g_out_w=m_g_out_w, xa_q=m_xa_q, xa_kv=m_xa_kv, xa_o=m_xa_o, f_up=m_f_up, f_conv_w=m_f_conv_w, f_conv_b=m_f_conv_b, f_down=m_f_down)
    mom_v = dict(ln_mix=v_ln_mix, ln_xattn=v_ln_xattn, ln_mem=v_ln_mem, ln_ffn=v_ln_ffn, final_norm=v_final_norm, m_in_w=v_m_in_w, m_conv_w=v_m_conv_w, m_conv_b=v_m_conv_b, m_dt_bias=v_m_dt_bias, m_a_log=v_m_a_log, m_d=v_m_d, m_norm_w=v_m_norm_w, m_out_w=v_m_out_w, h_in_w=v_h_in_w, h_lower_bounds=v_h_lower_bounds, h_norm_w=v_h_norm_w, h_out_w=v_h_out_w, g_in_w=v_g_in_w, g_conv_w=v_g_conv_w, g_a_log=v_g_a_log, g_dt_bias=v_g_dt_bias, g_norm_w=v_g_norm_w, g_out_w=v_g_out_w, xa_q=v_xa_q, xa_kv=v_xa_kv, xa_o=v_xa_o, f_up=v_f_up, f_conv_w=v_f_conv_w, f_conv_b=v_f_conv_b, f_down=v_f_down)

    bl, seq, _ = x.shape
    t = bl * seq

    p = {n: local[n] for n in WEIGHTS if n not in SHARD_AXIS}
    for n in SMALL_SHARDED:
        p[n] = _full_from_gathered(all_gather(local[n], f"gather_{n}"), SHARD_AXIS[n])
    units = [(n, l) for n in MATMUL_WEIGHTS for l in range(local[n].shape[0])]
    block = lambda u: local[u[0]][u[1]].astype(BF16)
    weights = {u: whole_weight(u[0], all_gather(block(u), f"gather_{u[0]}{u[1]}")) for u in layer_units(0)}
    blocks = {u: block(u) for u in units if u not in weights}
    standins = {u: jnp.zeros((N_DEV,) + local[u[0]].shape[1:], BF16) for u in units}
    small = {n: p[n] for n in WEIGHTS if n not in MATMUL_WEIGHTS and n != 'final_norm'}

    def run(small_w, standins_, xin):
        return _trunk(small_w, weights, blocks, standins_, xin, mem.reshape(bl * N_MEM, D_MODEL), bl, seq)

    x_out, vjp = jax.vjp(run, small, standins, x.reshape(t, D_MODEL))
    loss_part, dx_out, d_final = loss_head(x_out, final_norm, loss_target.reshape(t, D_MODEL))
    grads, received, dx = vjp(dx_out)
    grads = dict(grads)
    grads['final_norm'] = d_final
    loss = lax.psum(loss_part, ("x", "y", "c"))

    outs = {}

    def update(name, n, slots, shape, sel=lambda a: a):
        two_d = lambda a: sel(a).reshape(slots.shape[1:])
        got = reduce_adamw(slots, two_d(local[n]), two_d(mom_m[n]), two_d(mom_v[n]), name)
        return [g.reshape(shape) for g in got]

    for n in SMALL_SHARDED:
        slots = exchange_slabs(_shards_of_full(grads[n], SHARD_AXIS[n]), f"exchange_{n}")
        slots = slots.reshape(N_DEV, -1, slots.shape[-1])
        for kind, a in zip(KINDS, update(f"adamw_{n}", n, slots, local[n].shape)):
            outs[kind, n] = a
    for n in MATMUL_WEIGHTS:
        per_layer = [update(f"adamw_{n}{l}", n, received[n, l], local[n].shape[1:], lambda a, l=l: a[l])
                     for l in range(local[n].shape[0])]
        for k, kind in enumerate(KINDS):
            outs[kind, n] = jnp.concatenate([got[k][None] for got in per_layer])
    replicated = [n for n in WEIGHTS if n not in SHARD_AXIS]
    pk = lambda d: _pack([d[n] for n in replicated], F32, 8)
    got = reduce_adamw(all_gather(pk(grads), "gather_replicated_grads"), pk(local), pk(mom_m), pk(mom_v),
                       "adamw_replicated")
    shapes = [local[n].shape for n in replicated]
    for kind, buf in zip(KINDS, got):
        for n, a in zip(replicated, _unpack(buf, shapes)):
            outs[kind, n] = a
    result = [loss, dx.reshape(bl, seq, D_MODEL)]
    for kind in KINDS:
        result += [outs[kind, n] for n in WEIGHTS]
    return tuple(result)
```

```python
import functools
import math

import jax
import jax.numpy as jnp
from jax import lax
from jax.experimental import pallas as pl
from jax.experimental.pallas import tpu as pltpu

F32 = jnp.float32
BF16 = jnp.bfloat16
HIGHEST = lax.Precision.HIGHEST
NN = (((1,), (0,)), ((), ()))
NT = (((1,), (1,)), ((), ()))
TN = (((0,), (0,)), ((), ()))

D_MODEL = 1024
DEPTH = 4
EPS = 1e-6
N_MEM = 256
M_D_INNER = 2048
M_HEADS = 32
M_GROUPS = 8
M_STATE = 128
M_CONV_DIM = 4096
M_IN = 6176
M_IN_PAD = 6272
SSD_CHUNK = 256
H_HEADS = 8
HGRN_CHUNK = 32
HGRN_ROWS = 128
G_QK_HEADS = 8
G_V_HEADS = 16
G_KEY_DIM = 1024
G_VAL_DIM = 2048
G_CONV_DIM = 4096
G_IN = 6176
G_IN_PAD = 6272
GDN_CHUNK = 64
X_HEADS = 4
X_HEAD_DIM = 256
D_FF = 2816
ADAM_LR = 0.001
ADAM_B1 = 0.9
ADAM_B2 = 0.999
ADAM_EPS = 1e-08
ADAM_WD = 0.01
ADAM_STEP = 10

N_DEV = 8
LANE = 128
KINDS = ('grad', 'delta', 'new_m', 'new_v')
VMEM_LIMIT = 56 * 1024 * 1024

WEIGHTS = ['ln_mix', 'ln_xattn', 'ln_mem', 'ln_ffn', 'final_norm', 'm_in_w', 'm_conv_w', 'm_conv_b', 'm_dt_bias',
           'm_a_log', 'm_d', 'm_norm_w', 'm_out_w', 'h_in_w', 'h_lower_bounds', 'h_norm_w', 'h_out_w', 'g_in_w',
           'g_conv_w', 'g_a_log', 'g_dt_bias', 'g_norm_w', 'g_out_w', 'xa_q', 'xa_kv', 'xa_o', 'f_up', 'f_conv_w',
           'f_conv_b', 'f_down']
SHARD_AXIS = {'m_in_w': 2, 'm_conv_w': 2, 'm_conv_b': 1, 'm_norm_w': 1, 'm_out_w': 1, 'h_in_w': 2, 'h_out_w': 1,
              'g_in_w': 2, 'g_conv_w': 2, 'g_out_w': 1, 'xa_q': 1, 'xa_kv': 2, 'xa_o': 1, 'f_up': 2, 'f_conv_w': 2,
              'f_down': 1}
MATMUL_WEIGHTS = ['m_in_w', 'm_out_w', 'h_in_w', 'h_out_w', 'g_in_w', 'g_out_w', 'xa_q', 'xa_kv', 'xa_o', 'f_up',
                  'f_down']
SMALL_SHARDED = ['m_conv_w', 'm_conv_b', 'm_norm_w', 'g_conv_w', 'f_conv_w']


def _cparams():
    return pltpu.CompilerParams(vmem_limit_bytes=VMEM_LIMIT)


def bdot(a, b, dims=NN):
    return lax.dot_general(a.astype(BF16), b.astype(BF16), dims, preferred_element_type=F32)


def _split(a):
    hi = a.astype(BF16)
    return hi, (a - hi.astype(F32)).astype(BF16)


def _h3(a, b, dims):
    ah, al = _split(a)
    bh, bl = _split(b)
    d = functools.partial(lax.dot_general, dimension_numbers=dims, preferred_element_type=F32)
    return d(ah, bh) + (d(ah, bl) + d(al, bh))


@jax.custom_vjp
def h3dot(a, b):
    return _h3(a, b, NN)


h3dot.defvjp(lambda a, b: (_h3(a, b, NN), (a, b)),
             lambda res, ct: (_h3(ct, res[1], NT), _h3(res[0], ct, TN)))

T_ROWS = (((0,), (1,)), ((), ()))


def _tri_times(tri, x, dims, tri_first):
    t = tri.astype(BF16)
    x0 = x.astype(BF16)
    r1 = x - x0.astype(F32)
    x1 = r1.astype(BF16)
    x2 = (r1 - x1.astype(F32)).astype(BF16)
    if tri_first:
        d = lambda xx: lax.dot_general(t, xx, dims, preferred_element_type=F32)
    else:
        d = lambda xx: lax.dot_general(xx, t, dims, preferred_element_type=F32)
    return d(x0) + (d(x1) + d(x2))


@jax.custom_vjp
def cumdot(tri, x):
    return _tri_times(tri, x, NN, True)


cumdot.defvjp(lambda tri, x: (_tri_times(tri, x, NN, True), tri),
              lambda tri, ct: (jnp.zeros_like(tri), _tri_times(tri, ct, TN, True)))


@jax.custom_vjp
def cumdot_t(tri, x):
    return _tri_times(tri, x, T_ROWS, False)


cumdot_t.defvjp(lambda tri, x: (_tri_times(tri, x, T_ROWS, False), tri),
                lambda tri, ct: (jnp.zeros_like(tri), _tri_times(tri, ct, T_ROWS, True)))


def _tile(dim, cap):
    if dim <= cap:
        return dim
    best = None
    for t in range(LANE, cap + 1, LANE):
        if dim % t == 0:
            best = t
    assert best is not None, dim
    return best


def _position():
    return lax.axis_index("x"), lax.axis_index("y"), lax.axis_index("c")


def _direct_copies(kind, src_ref, dst_ref, send_sems, recv_sems, local_sem):
    x, y, c = _position()
    me = 4 * x + 2 * y + c
    local_src = src_ref if kind == 'gather' else src_ref.at[me]
    copies = [pltpu.make_async_copy(local_src, dst_ref.at[me], local_sem)]
    for k in range(1, N_DEV):
        px = 1 - x if (k >> 2) & 1 else x
        py = 1 - y if (k >> 1) & 1 else y
        pc = 1 - c if k & 1 else c
        copies.append(pltpu.make_async_remote_copy(
            src_ref=src_ref if kind == 'gather' else src_ref.at[4 * px + 2 * py + pc], dst_ref=dst_ref.at[me],
            send_sem=send_sems.at[k - 1], recv_sem=recv_sems.at[k - 1],
            device_id=(px, py, pc), device_id_type=pl.DeviceIdType.MESH))
    return copies


COMM_SCRATCH = [pltpu.SemaphoreType.DMA((N_DEV - 1,)), pltpu.SemaphoreType.DMA((N_DEV - 1,)), pltpu.SemaphoreType.DMA]


def matmul(a, b, *, ta=False, tb=False, out_dtype=F32, name="mm", carry=None, residual=None):
    if ta:
        k, m = a.shape
    else:
        m, k = a.shape
    if tb:
        n, k2 = b.shape
    else:
        k2, n = b.shape
    assert k == k2, (a.shape, b.shape, ta, tb)
    tm = _tile(m, 512)
    tn = _tile(n, 1408)
    tk = _tile(k, 1408)
    grid = (m // tm, n // tn, k // tk)
    nk = grid[2]
    dims = (((0 if ta else 1,), (1 if tb else 0,)), ((), ()))

    def at_step(which):
        conds = [pl.program_id(ax) == (0 if which == 'first' else grid[ax] - 1) for ax in range(3)]
        return jnp.logical_and(jnp.logical_and(conds[0], conds[1]), conds[2])

    def body(*refs):
        r_ref = None
        if residual is not None:
            r_ref, refs = refs[2], refs[:2] + refs[3:]
        if carry is None:
            a_ref, b_ref, o_ref, acc_ref = refs
        else:
            a_ref, b_ref, src_ref, o_ref, dst_ref, acc_ref, send_sems, recv_sems, local_sem = refs
            copies = lambda: _direct_copies(carry[0], src_ref, dst_ref, send_sems, recv_sems, local_sem)

            @pl.when(at_step('first'))
            def _():
                for cp in copies():
                    cp.start()

        @pl.when(pl.program_id(2) == 0)
        def _():
            acc_ref[...] = jnp.zeros_like(acc_ref)

        acc_ref[...] += lax.dot_general(a_ref[...].astype(BF16), b_ref[...].astype(BF16), dims,
                                        preferred_element_type=F32)

        @pl.when(pl.program_id(2) == nk - 1)
        def _():
            out = acc_ref[...] if r_ref is None else acc_ref[...] + r_ref[...]
            o_ref[...] = out.astype(o_ref.dtype)

        if carry is not None:
            @pl.when(at_step('last'))
            def _():
                for cp in copies():
                    cp.wait()

    a_spec = pl.BlockSpec((tk, tm), lambda i, j, kk: (kk, i)) if ta else pl.BlockSpec((tm, tk), lambda i, j, kk: (i, kk))
    b_spec = pl.BlockSpec((tn, tk), lambda i, j, kk: (j, kk)) if tb else pl.BlockSpec((tk, tn), lambda i, j, kk: (kk, j))
    o_spec = pl.BlockSpec((tm, tn), lambda i, j, kk: (i, j))
    o_shape = jax.ShapeDtypeStruct((m, n), out_dtype)
    acc = pltpu.VMEM((tm, tn), F32)
    ins, in_specs = [a, b], [a_spec, b_spec]
    if residual is not None:
        ins.append(residual)
        in_specs.append(o_spec)
    if carry is None:
        return pl.pallas_call(
            body, name=name, grid=grid, in_specs=in_specs, out_specs=o_spec, out_shape=o_shape,
            scratch_shapes=[acc], compiler_params=_cparams(),
        )(*ins)
    kind, src = carry
    got = jax.ShapeDtypeStruct(((N_DEV,) + src.shape) if kind == 'gather' else src.shape, src.dtype)
    hbm = pl.BlockSpec(memory_space=pl.ANY)
    return pl.pallas_call(
        body, name=name, grid=grid, in_specs=in_specs + [hbm], out_specs=[o_spec, hbm],
        out_shape=[o_shape, got], scratch_shapes=[acc] + COMM_SCRATCH, compiler_params=_cparams(),
    )(*ins, src)


def make_linear(name, shard_axis, n_real, has_res=False):
    def forward(a, w, nxt, res):
        r = res[0] if res else None
        if nxt:
            y, got = matmul(a, w, name=name + "_fwd", carry=('gather', nxt[0]), residual=r)
            return y, (got,)
        return matmul(a, w, name=name + "_fwd", residual=r), ()

    @jax.custom_vjp
    def linear(a, w, wg, nxt, res):
        return forward(a, w, nxt, res)

    def fwd(a, w, wg, nxt, res):
        return forward(a, w, nxt, res), (a, w, nxt)

    def bwd(saved, cts):
        a, w, nxt = saved
        dy = cts[0]
        dw = matmul(a, dy, ta=True, name=name + "_bwd_dw")
        slabs = _shards_of_full(dw[:, :n_real], shard_axis).astype(BF16)
        da, slots = matmul(dy, w, tb=True, out_dtype=a.dtype, name=name + "_bwd_da", carry=('exchange', slabs))
        return da, jnp.zeros_like(w), slots, tuple(jnp.zeros_like(b) for b in nxt), ((dy,) if has_res else ())

    linear.defvjp(fwd, bwd)
    return linear


class In:
    def __init__(self, block, imap, kind='blk', inner=(), cols=None):
        self.block, self.imap, self.kind, self.inner, self.cols = block, imap, kind, inner, cols


class Out:
    def __init__(self, shape, dtype, block, imap):
        self.shape, self.dtype, self.block, self.imap = shape, dtype, block, imap


def make_op(name, fn, grid, ins, outs, state_shape=None, seq_axis=None, passthrough=()):
    n_in, n_out = len(ins), len(outs)
    has_state = state_shape is not None
    nd = len(grid)
    diff_idx = [i for i, s in enumerate(ins) if s.kind != 'const']

    def in_spec(s, reverse):
        off = 0
        if s.cols is not None:
            assert s.cols[0] % s.block[-1] == 0
            off = s.cols[0] // s.block[-1]

        def imap(*ids):
            ids = rev(ids) if reverse else ids
            idx = tuple(s.imap(*ids))
            return idx[:-1] + (idx[-1] + off,) if off else idx

        return pl.BlockSpec(s.block, imap)

    def rel_spec(block, f, reverse):
        return pl.BlockSpec(block, (lambda *ids: f(*rev(ids))) if reverse else f)

    def rev(ids):
        if not has_state:
            return ids
        ids = list(ids)
        ids[seq_axis] = grid[seq_axis] - 1 - ids[seq_axis]
        return tuple(ids)

    save_shape = tuple(grid) + tuple(state_shape) if has_state else None
    save_block = (None,) * nd + tuple(state_shape) if has_state else None

    def save_imap(*ids):
        return tuple(ids) + (0,) * len(state_shape)

    def fwd_call(*xs):
        def body(*refs):
            in_refs = refs[:n_in]
            out_refs = refs[n_in:n_in + n_out]
            vals = [r[...] for r in in_refs]
            if has_state:
                save_ref, st_ref = refs[n_in + n_out], refs[n_in + n_out + 1]

                @pl.when(pl.program_id(seq_axis) == 0)
                def _():
                    st_ref[...] = jnp.zeros(state_shape, F32)

                st = st_ref[...]
                save_ref[...] = st
                res = fn(*vals, st)
                st_ref[...] = res[-1]
                res = res[:-1]
            else:
                res = fn(*vals)
            for o, v in zip(out_refs, res):
                o[...] = v.astype(o.dtype)

        out_shape = [jax.ShapeDtypeStruct(o.shape, o.dtype) for o in outs]
        out_specs = [pl.BlockSpec(o.block, o.imap) for o in outs]
        scratch = []
        if has_state:
            out_shape.append(jax.ShapeDtypeStruct(save_shape, F32))
            out_specs.append(pl.BlockSpec(save_block, save_imap))
            scratch.append(pltpu.VMEM(state_shape, F32))
        return pl.pallas_call(
            body, name=name + "_fwd", grid=grid,
            in_specs=[in_spec(s, False) for s in ins],
            out_specs=out_specs, out_shape=out_shape, scratch_shapes=scratch,
            compiler_params=_cparams(),
        )(*xs)

    def grad_shape(s, x):
        if s.cols is not None:
            return x.shape[:-1] + (s.cols[1],)
        return x.shape

    def bwd_call(xs, save, cts, pass_cts=()):
        n_diff = len(diff_idx)

        def body(*refs):
            in_refs = refs[:n_in]
            p = n_in
            if has_state:
                save_ref = refs[p]
                p += 1
            ct_refs = refs[p:p + n_out]
            p += n_out
            pass_refs = dict(zip(passthrough, refs[p:p + len(passthrough)]))
            p += len(passthrough)
            g_refs = refs[p:p + n_diff]
            p += n_diff
            vals = [r[...] for r in in_refs]

            def g(*dv):
                full = list(vals)
                for i, v in zip(diff_idx, dv):
                    full[i] = v
                if has_state:
                    return tuple(fn(*full, dv[-1]))
                return tuple(fn(*full))

            prim = [vals[i] for i in diff_idx]
            ct = tuple(r[...].astype(F32) for r in ct_refs)
            if has_state:
                dst_ref = refs[p]

                @pl.when(pl.program_id(seq_axis) == 0)
                def _():
                    dst_ref[...] = jnp.zeros(state_shape, F32)

                prim = prim + [save_ref[...]]
                ct = ct + (dst_ref[...],)
            _, vjp = jax.vjp(g, *prim)
            grads = vjp(ct)
            for k, i in enumerate(diff_idx):
                s = ins[i]
                if s.kind == 'blk':
                    g = grads[k] + pass_refs[i][...] if i in pass_refs else grads[k]
                    g_refs[k][...] = g.astype(g_refs[k].dtype)
                else:
                    first = None
                    for ax in s.inner:
                        c = pl.program_id(ax) == 0
                        first = c if first is None else jnp.logical_and(first, c)

                    @pl.when(first)
                    def _(k=k):
                        g_refs[k][...] = jnp.zeros_like(g_refs[k])

                    g_refs[k][...] += grads[k].astype(g_refs[k].dtype)
            if has_state:
                dst_ref[...] = grads[-1]

        in_specs = [in_spec(s, True) for s in ins]
        args = list(xs)
        if has_state:
            in_specs.append(rel_spec(save_block, save_imap, True))
            args.append(save)
        for o, c in zip(outs, cts):
            in_specs.append(rel_spec(o.block, o.imap, True))
            args.append(c)
        for i, c in zip(passthrough, pass_cts):
            assert ins[i].kind == 'blk' and ins[i].cols is None
            in_specs.append(rel_spec(ins[i].block, ins[i].imap, True))
            args.append(c)
        out_shape, out_specs = [], []
        for i in diff_idx:
            s = ins[i]
            out_shape.append(jax.ShapeDtypeStruct(grad_shape(s, xs[i]), xs[i].dtype))
            out_specs.append(rel_spec(s.block, s.imap, True))
        scratch = [pltpu.VMEM(state_shape, F32)] if has_state else []
        return pl.pallas_call(
            body, name=name + "_bwd", grid=grid,
            in_specs=in_specs, out_specs=out_specs, out_shape=out_shape, scratch_shapes=scratch,
            compiler_params=_cparams(),
        )(*args)

    @jax.custom_vjp
    def op(*xs):
        return tuple(fwd_call(*xs)[:n_out]) + tuple(xs[i] for i in passthrough)

    def op_fwd(*xs):
        res = fwd_call(*xs)
        return tuple(res[:n_out]) + tuple(xs[i] for i in passthrough), (xs, res[n_out] if has_state else None)

    def op_bwd(resid, cts):
        xs, save = resid
        grads = bwd_call(xs, save, cts[:n_out], cts[n_out:])
        out = []
        k = 0
        for i, s in enumerate(ins):
            if s.kind == 'const':
                out.append(jnp.zeros_like(xs[i]))
                continue
            g = grads[k]
            k += 1
            if s.cols is not None:
                g = jnp.pad(g, ((0, 0),) * (g.ndim - 1) + ((s.cols[0], xs[i].shape[-1] - s.cols[0] - s.cols[1]),))
            out.append(g)
        return tuple(out)

    op.defvjp(op_fwd, op_bwd)
    return op


def _rms(x, w):
    return x * lax.rsqrt(jnp.mean(x * x, axis=-1, keepdims=True) + EPS) * w


def _silu(x):
    return x * jax.nn.sigmoid(x)


def rmsnorm_op(name, t, out_dtype, residual=False):
    tm = _tile(t, 512)
    return make_op(
        name, lambda x, w: (_rms(x, w),), (t // tm,),
        [In((tm, D_MODEL), lambda i: (i, 0)), In((1, D_MODEL), lambda i: (0, 0), 'acc', (0,))],
        [Out((t, D_MODEL), out_dtype, (tm, D_MODEL), lambda i: (i, 0))], passthrough=(0,) if residual else ())


def _tri(q):
    ii = lax.broadcasted_iota(jnp.int32, (q, q), 0)
    jj = lax.broadcasted_iota(jnp.int32, (q, q), 1)
    return ii >= jj, ii > jj


def _ssd_fn(z, x, bm, cm, dtr, dtb, alog, dsk, nw, state):
    q = x.shape[0]
    incl, _ = _tri(q)
    tril = incl.astype(F32)
    dt = jax.nn.softplus(dtr + dtb)
    da = dt * (-jnp.exp(alog))
    acum = cumdot(tril, da)
    acum_t = cumdot_t(tril, da)
    cb = bdot(cm, bm, NT)
    heads = range(4)
    wide = lambda a: jnp.concatenate([jnp.broadcast_to(a[:, r:r + 1], (a.shape[0], 64)) for r in heads], axis=1)
    last = acum[q - 1:q, :]
    xc = x * wide(dt)
    y = bdot(cm, state, NT) * wide(jnp.exp(acum)) + wide(dsk) * x
    ds = bdot(xc * wide(jnp.exp(last - acum)), bm, TN)
    e_last = jnp.exp(last)
    new_state = state * jnp.concatenate([jnp.broadcast_to(e_last[:, r:r + 1], (64, 1)) for r in heads], axis=0) + ds
    diag = []
    for r in heads:
        decay = jnp.exp(jnp.where(incl, acum[:, r:r + 1] - acum_t[r:r + 1, :], -jnp.inf))
        diag.append(bdot(cb * decay, xc[:, 64 * r:64 * r + 64]))
    y = y + jnp.concatenate(diag, axis=1)
    yz = y * _silu(z)
    return _rms(yz, nw), new_state


def _per_sequence(fn, n_seq_args, bl):
    def f(*args):
        *ins, state = args
        res = [fn(*[a[b] for a in ins[:n_seq_args]], *ins[n_seq_args:], state[b]) for b in range(bl)]
        return tuple(jnp.concatenate([r[k][None] for r in res]) for k in range(len(res[0])))

    return f


def ssd_op(name, bl, seq):
    q = SSD_CHUNK
    nc = seq // q
    blk = lambda w, c0, cw: In((bl, q, w), lambda g, n: (0, n, g), cols=(c0, cw))
    small = lambda g, n: (g, 0, 0)
    ins = [
        blk(256, 0, M_D_INNER),
        blk(256, 0, M_D_INNER),
        blk(128, M_D_INNER, 1024),
        blk(128, M_D_INNER + 1024, 1024),
        In((None, bl, q, 4), lambda g, n: (g, 0, n, 0)),
        In((None, 1, 4), small, 'acc', (1,)),
        In((None, 1, 4), small, 'acc', (1,)),
        In((None, 1, 4), small, 'acc', (1,)),
        In((None, 1, 256), small, 'acc', (1,)),
    ]
    outs = [Out((bl, seq, M_D_INNER), F32, (bl, q, 256), lambda g, n: (0, n, g))]
    return make_op(name, _per_sequence(_ssd_fn, 5, bl), (M_GROUPS, nc), ins, outs,
                   state_shape=(bl, 256, 128), seq_axis=1)


def _gla_fn(layer, qr, fr, ir, gr, lbp, nw, state_t):
    rows = qr.shape[0]
    c = HGRN_CHUNK
    e = jnp.exp(lbp - jnp.max(lbp, axis=0, keepdims=True))
    sm = e / jnp.sum(e, axis=0, keepdims=True)
    lb = jnp.sum(sm[1:layer + 1, :], axis=0, keepdims=True) if layer > 0 else jnp.zeros((1, lbp.shape[1]), F32)
    qq = _silu(qr) * (128 ** -0.5)
    forget = lb + (1.0 - lb) * jax.nn.sigmoid(fr)
    kk = 1.0 - forget
    logf = jnp.log(forget)
    incl, _ = _tri(c)
    tril = incl.astype(F32)
    os_ = []
    for j in range(rows // c):
        sl = slice(c * j, c * j + c)
        gc = cumdot(tril, logf[sl])
        glast = gc[c - 1:c, :]
        q_dec = qq[sl] * jnp.exp(gc)
        k_inv = kk[sl] * jnp.exp(-gc)
        k_end = kk[sl] * jnp.exp(glast - gc)
        att = jnp.where(incl, bdot(q_dec, k_inv, NT), 0.0)
        os_.append(bdot(att, ir[sl]) + bdot(q_dec, state_t, NT))
        state_t = state_t * jnp.exp(glast) + bdot(ir[sl], k_end, TN)
    o = jnp.concatenate(os_, axis=0)
    return _rms(o, nw) * _silu(gr), state_t


def gla_op(name, layer, bl, seq):
    r = HGRN_ROWS
    ns = seq // r
    blk = lambda k: In((bl, r, 128), lambda h, n: (0, n, h), cols=(1024 * k, 1024))
    ins = [blk(0), blk(1), blk(2), blk(3),
           In((DEPTH, 128), lambda h, n: (0, h), 'acc', (1,)),
           In((1, 128), lambda h, n: (0, 0), 'acc', (0, 1))]
    outs = [Out((bl, seq, D_MODEL), F32, (bl, r, 128), lambda h, n: (0, n, h))]
    return make_op(name, _per_sequence(functools.partial(_gla_fn, layer), 4, bl), (H_HEADS, ns), ins, outs,
                   state_shape=(bl, 128, 128), seq_axis=1)


def _unit_lower_inverse(m, nilpotent):
    q = m.shape[0]
    ii = lax.broadcasted_iota(jnp.int32, (q, q), 0)
    jj = lax.broadcasted_iota(jnp.int32, (q, q), 1)
    eye = (ii == jj).astype(F32)
    p = -m
    inv = eye + p
    steps = int(math.log2(nilpotent)) - 1
    for _ in range(steps):
        p = h3dot(p, p)
        inv = inv + h3dot(inv, p)
    return inv


def _gdn_fn(qc, kc, vc, zc, br, ar, alog, dtb, nw, state):
    bl, q = qc.shape[0], qc.shape[1]
    incl, strict = _tri(q)
    tril = incl.astype(F32)
    g = jnp.concatenate([-jnp.exp(alog) * jax.nn.softplus(ar[b] + dtb) for b in range(bl)], axis=1)
    gc = cumdot(tril, g)
    gc_t = cumdot_t(tril, g)
    heads, ms, rhs = [], [], []
    for b in range(bl):
        qn = qc[b] * lax.rsqrt(jnp.sum(qc[b] * qc[b], axis=-1, keepdims=True) + EPS) * (128 ** -0.5)
        kn = kc[b] * lax.rsqrt(jnp.sum(kc[b] * kc[b], axis=-1, keepdims=True) + EPS)
        beta = jax.nn.sigmoid(br[b])
        qk = bdot(qn, kn, NT)
        for j in range(2):
            i = 2 * b + j
            col = gc[:, i:i + 1]
            decay = jnp.exp(jnp.where(incl, col - gc_t[i:i + 1, :], -jnp.inf))
            bj = beta[:, j:j + 1]
            kb = kn * bj
            ms.append(jnp.where(strict, bdot(kb, kn, NT) * decay, 0.0))
            rhs.append(jnp.concatenate([vc[b][:, 128 * j:128 * j + 128] * bj, kb * jnp.exp(col)], axis=1))
            heads.append((qn, kn, qk * decay, col, gc[q - 1:q, i:i + 1]))
    n = len(ms)
    zero = jnp.zeros((q, q), F32)
    m_all = jnp.concatenate(
        [jnp.concatenate([ms[i] if i == k else zero for i in range(n)], axis=1) for k in range(n)], axis=0)
    sol = h3dot(_unit_lower_inverse(m_all, q), jnp.concatenate(rhs, axis=0))
    outs, states = [], []
    for b in range(bl):
        os_, sts = [], []
        for j in range(2):
            i = 2 * b + j
            qn, kn, att, col, glast = heads[i]
            u = sol[q * i:q * i + q, :128]
            w = sol[q * i:q * i + q, 128:]
            st = state[b][128 * j:128 * j + 128, :]
            v_new = u - bdot(w, st)
            o = bdot(qn * jnp.exp(col), st) + bdot(att, v_new)
            sts.append(st * jnp.exp(glast) + bdot(kn * jnp.exp(glast - col), v_new, TN))
            os_.append(_rms(o, nw) * _silu(zc[b][:, 128 * j:128 * j + 128]))
        outs.append(jnp.concatenate(os_, axis=1))
        states.append(jnp.concatenate(sts, axis=0))
    return jnp.concatenate([o[None] for o in outs]), jnp.concatenate([st[None] for st in states])


def gdn_op(name, bl, seq):
    q = GDN_CHUNK
    nc = seq // q
    blk = lambda w, c0, cw: In((bl, q, w), lambda h, n: (0, n, h), cols=(c0, cw))
    small = lambda h, n: (h, 0, 0)
    ins = [
        blk(128, 0, G_KEY_DIM),
        blk(128, G_KEY_DIM, G_KEY_DIM),
        blk(256, 2 * G_KEY_DIM, G_VAL_DIM),
        blk(256, G_CONV_DIM, G_VAL_DIM),
        In((None, bl, q, 2), lambda h, n: (h, 0, n, 0)),
        In((None, bl, q, 2), lambda h, n: (h, 0, n, 0)),
        In((None, 1, 2), small, 'acc', (1,)),
        In((None, 1, 2), small, 'acc', (1,)),
        In((1, 128), lambda h, n: (0, 0), 'acc', (0, 1)),
    ]
    outs = [Out((bl, seq, G_VAL_DIM), F32, (bl, q, 256), lambda h, n: (0, n, h))]
    return make_op(name, _gdn_fn, (G_QK_HEADS, nc), ins, outs,
                   state_shape=(bl, 256, 128), seq_axis=1)


def _xattn_fn(q, k, v):
    s = bdot(q, k, NT) * (X_HEAD_DIM ** -0.5)
    s = s - jnp.max(s, axis=-1, keepdims=True)
    p = jnp.exp(s)
    p = p / jnp.sum(p, axis=-1, keepdims=True)
    return (bdot(p, v),)


def xattn_op(name, bl, seq):
    tq = _tile(seq, 512)
    nq = seq // tq
    t = bl * seq
    ins = [
        In((tq, X_HEAD_DIM), lambda b, h, i: (b * nq + i, h)),
        In((N_MEM, X_HEAD_DIM), lambda b, h, i: (b, h), 'acc', (2,), cols=(0, D_MODEL)),
        In((N_MEM, X_HEAD_DIM), lambda b, h, i: (b, h), 'acc', (2,), cols=(D_MODEL, D_MODEL)),
    ]
    outs = [Out((t, D_MODEL), F32, (tq, X_HEAD_DIM), lambda b, h, i: (b * nq + i, h))]
    return make_op(name, _xattn_fn, (bl, X_HEADS, nq), ins, outs)


CONV_PAD = 8
CONV_ROWS = 64


def make_conv(name, bl, seq, width, ch, x_col0, up_col0=None):
    cb = 256
    rt = CONV_ROWS
    assert ch % cb == 0 and x_col0 % cb == 0 and (up_col0 is None or up_col0 % cb == 0) and seq % rt == 0
    nb = ch // cb
    n_tiles = seq // rt
    t = bl * seq
    has_up = up_col0 is not None
    grid = (nb, bl)
    x_spec = pl.BlockSpec((seq, cb), lambda c, b: (b, x_col0 // cb + c))
    up_specs = [pl.BlockSpec((seq, cb), lambda c, b: (b, up_col0 // cb + c))] if has_up else []
    w_spec = pl.BlockSpec((width, cb), lambda c, b: (0, c))
    b_spec = pl.BlockSpec((1, cb), lambda c, b: (0, c))
    o_spec = pl.BlockSpec((seq, cb), lambda c, b: (b, c))
    taps = [CONV_PAD - (width - 1) + j for j in range(width)]

    def window(x_ref, i):
        if isinstance(i, int) and i == 0:
            return jnp.concatenate([jnp.zeros((CONV_PAD, cb), F32), x_ref[0:rt, :]], axis=0)
        return x_ref[pl.ds(pl.multiple_of(i * rt - CONV_PAD, CONV_PAD), rt + CONV_PAD), :]

    def rows(i):
        return pl.ds(i * rt, rt) if isinstance(i, int) else pl.ds(pl.multiple_of(i * rt, rt), rt)

    def shifted(win):
        return [win[tp:tp + rt, :] for tp in taps]

    def pre_activation(views, w, b):
        y = b + w[0:1, :] * views[0]
        for j in range(1, width):
            y = y + w[j:j + 1, :] * views[j]
        return y

    def over_tiles(step, carry):
        carry = step(0, carry)
        return lax.fori_loop(1, n_tiles, step, carry)

    def fwd_call(x, w, b):
        def body(*refs):
            x_ref, w_ref, b_ref = refs[:3]
            o_ref = refs[-1]
            w_, b_ = w_ref[...], b_ref[...]

            def step(i, carry):
                y = _silu(pre_activation(shifted(window(x_ref, i)), w_, b_))
                if has_up:
                    y = y * refs[3][rows(i), :]
                o_ref[rows(i), :] = y
                return carry

            over_tiles(step, 0)

        return pl.pallas_call(
            body, name=name + "_fwd", grid=grid,
            in_specs=[x_spec, w_spec, b_spec] + up_specs, out_specs=o_spec,
            out_shape=jax.ShapeDtypeStruct((t, ch), F32),
            compiler_params=_cparams(),
        )(*([x, w, b] + ([x] if has_up else [])))

    def bwd_call(x, w, b, do):
        n_in = 4 + (1 if has_up else 0)

        def body(*refs):
            x_ref, w_ref, b_ref = refs[:3]
            do_ref = refs[n_in - 1]
            dx_ref, dw_ref, db_ref = refs[n_in:n_in + 3]
            gpad_ref = refs[-1]
            w_, b_ = w_ref[...], b_ref[...]

            def fold(a):
                acc = a[0:8, :]
                for k in range(1, rt // 8):
                    acc = acc + a[8 * k:8 * k + 8, :]
                return acc

            def grad_pre(i, sums):
                views = shifted(window(x_ref, i))
                y = pre_activation(views, w_, b_)
                s = jax.nn.sigmoid(y)
                act = y * s
                do_ = do_ref[rows(i), :]
                if has_up:
                    refs[n_in + 3][rows(i), :] = do_ * act
                    do_ = do_ * refs[3][rows(i), :]
                dy = do_ * (s + act * (1.0 - s))
                gpad_ref[rows(i), :] = dy
                new = [sums[j] + fold(dy * views[j]) for j in range(width)]
                return tuple(new) + (sums[width] + fold(dy),)

            zero8 = jnp.zeros((8, cb), F32)
            sums = over_tiles(grad_pre, (zero8,) * (width + 1))
            gpad_ref[seq:seq + CONV_PAD, :] = jnp.zeros((CONV_PAD, cb), F32)

            def grad_x(i, carry):
                if isinstance(i, int):
                    gwin = gpad_ref[0:rt + CONV_PAD, :]
                else:
                    gwin = gpad_ref[pl.ds(pl.multiple_of(i * rt, rt), rt + CONV_PAD), :]
                dx = w_[0:1, :] * gwin[width - 1:width - 1 + rt, :]
                for j in range(1, width):
                    dx = dx + w_[j:j + 1, :] * gwin[width - 1 - j:width - 1 - j + rt, :]
                dx_ref[rows(i), :] = dx
                return carry

            over_tiles(grad_x, 0)

            @pl.when(pl.program_id(1) == 0)
            def _():
                dw_ref[...] = jnp.zeros_like(dw_ref)
                db_ref[...] = jnp.zeros_like(db_ref)

            dw_ref[...] += jnp.concatenate([jnp.sum(sums[j], axis=0, keepdims=True) for j in range(width)], axis=0)
            db_ref[...] += jnp.sum(sums[width], axis=0, keepdims=True)

        big = jax.ShapeDtypeStruct((t, ch), F32)
        return pl.pallas_call(
            body, name=name + "_bwd", grid=grid,
            in_specs=[x_spec, w_spec, b_spec] + up_specs + [o_spec],
            out_specs=[o_spec, w_spec, b_spec] + ([o_spec] if has_up else []),
            out_shape=[big, jax.ShapeDtypeStruct((width, ch), F32), jax.ShapeDtypeStruct((1, ch), F32)]
            + ([big] if has_up else []),
            scratch_shapes=[pltpu.VMEM((seq + CONV_PAD, cb), F32)],
            compiler_params=_cparams(),
        )(*([x, w, b] + ([x] if has_up else []) + [do]))

    @jax.custom_vjp
    def conv(x, w, b):
        return fwd_call(x, w, b)

    def conv_fwd(x, w, b):
        return fwd_call(x, w, b), (x, w, b)

    def conv_bwd(res, do):
        x, w, b = res
        got = bwd_call(x, w, b, do)
        dx = jnp.pad(got[0], ((0, 0), (x_col0, x.shape[1] - x_col0 - ch)))
        if has_up:
            dx = dx + jnp.pad(got[3], ((0, 0), (up_col0, x.shape[1] - up_col0 - ch)))
        return dx, got[1], got[2]

    conv.defvjp(conv_fwd, conv_bwd)

    def apply(x, w, b=None):
        if b is None:
            b = jnp.zeros((ch,), F32)
        return conv(x, w, b.reshape(1, ch))

    return apply


def loss_head(x, w, target):
    t = x.shape[0]
    tm = _tile(t, 512)

    def fn(xb, wb, tb):
        err = _rms(xb, wb) - tb
        return 0.5 * jnp.sum(err * err) * (1.0 / D_MODEL)

    def body(x_ref, w_ref, t_ref, loss_ref, dx_ref, dw_ref):
        @pl.when(pl.program_id(0) == 0)
        def _():
            loss_ref[...] = jnp.zeros_like(loss_ref)
            dw_ref[...] = jnp.zeros_like(dw_ref)

        tb = t_ref[...]
        val, vjp = jax.vjp(lambda a, b: fn(a, b, tb), x_ref[...], w_ref[...])
        dx, dw = vjp(jnp.ones((), F32))
        dx_ref[...] = dx
        dw_ref[...] += dw
        loss_ref[...] += jnp.full(loss_ref.shape, val, F32)

    row = pl.BlockSpec((tm, D_MODEL), lambda i: (i, 0))
    vec = pl.BlockSpec((1, D_MODEL), lambda i: (0, 0))
    loss, dx, dw = pl.pallas_call(
        body, name="loss_head", grid=(t // tm,),
        in_specs=[row, vec, row],
        out_specs=[pl.BlockSpec((8, LANE), lambda i: (0, 0)), row, vec],
        out_shape=[jax.ShapeDtypeStruct((8, LANE), F32), jax.ShapeDtypeStruct((t, D_MODEL), F32),
                   jax.ShapeDtypeStruct((1, D_MODEL), F32)],
        compiler_params=_cparams(),
    )(x, w.reshape(1, D_MODEL), target)
    return loss[0, 0], dx, dw.reshape(D_MODEL)


PACK_W = 1024
ADAM_BLOCK_BYTES = 512 * 1024


def _rows_tile(r, c):
    if r * c * 4 <= ADAM_BLOCK_BYTES or r % 8:
        return r
    best = 8
    for t in range(8, r + 1, 8):
        if r % t == 0 and t * c * 4 <= ADAM_BLOCK_BYTES:
            best = t
    return best


def reduce_adamw(slots, w, m, v, name):
    r, wd = w.shape
    tr = _rows_tile(r, wd)
    c1 = 1.0 - ADAM_B1 ** ADAM_STEP
    c2 = 1.0 - ADAM_B2 ** ADAM_STEP

    def body(s_ref, w_ref, m_ref, v_ref, g_ref, d_ref, nm_ref, nv_ref):
        g = s_ref[0].astype(F32)
        for k in range(1, N_DEV):
            g = g + s_ref[k].astype(F32)
        nm = ADAM_B1 * m_ref[...] + (1.0 - ADAM_B1) * g
        nv = ADAM_B2 * v_ref[...] + (1.0 - ADAM_B2) * (g * g)
        m_hat = nm / c1
        v_hat = nv / c2
        d_ref[...] = -ADAM_LR * (m_hat / (jnp.sqrt(v_hat) + ADAM_EPS) + ADAM_WD * w_ref[...])
        g_ref[...] = g
        nm_ref[...] = nm
        nv_ref[...] = nv

    blk = pl.BlockSpec((tr, wd), lambda i: (i, 0))
    shp = jax.ShapeDtypeStruct((r, wd), F32)
    return pl.pallas_call(
        body, name=name, grid=(r // tr,),
        in_specs=[pl.BlockSpec((N_DEV, tr, wd), lambda i: (0, i, 0)), blk, blk, blk],
        out_specs=[blk, blk, blk, blk], out_shape=[shp, shp, shp, shp],
        compiler_params=_cparams(),
    )(slots, w, m, v)


def all_gather(block, name):
    def body(x_ref, out_ref, send_sems, recv_sems, local_sem):
        x, y, c = _position()
        me, sibling = (x, y, c), (x, y, 1 - c)
        chips = [(1 - x, y), (x, 1 - y), (1 - x, 1 - y)]

        def slot(px, py, pc):
            return out_ref.at[4 * px + 2 * py + pc]

        def copy(k, owner, to, src=None):
            return pltpu.make_async_remote_copy(
                src_ref=slot(*owner) if src is None else src, dst_ref=slot(*owner),
                send_sem=send_sems.at[k], recv_sem=recv_sems.at[k],
                device_id=to, device_id_type=pl.DeviceIdType.MESH)

        mine = pltpu.make_async_copy(x_ref, slot(*me), local_sem)
        mine.start()
        first = [copy(0, me, sibling, src=x_ref)]
        first += [copy(1 + j, me, (*chip, c), src=x_ref) for j, chip in enumerate(chips)]
        for cp in first:
            cp.start()
        passed = [copy(4 + j, (*chip, c), sibling) for j, chip in enumerate(chips)]
        for j, chip in enumerate(chips):
            copy(1 + j, (*chip, c), me).wait_recv()
            passed[j].start()
        copy(0, sibling, me).wait_recv()
        for j, chip in enumerate(chips):
            copy(4 + j, (*chip, 1 - c), me).wait_recv()
        for cp in first + passed:
            cp.wait_send()
        mine.wait()

    return pl.pallas_call(
        body, name=name,
        out_shape=jax.ShapeDtypeStruct((N_DEV,) + block.shape, block.dtype),
        in_specs=[pl.BlockSpec(memory_space=pl.ANY)],
        out_specs=pl.BlockSpec(memory_space=pl.ANY),
        scratch_shapes=[pltpu.SemaphoreType.DMA((7,)), pltpu.SemaphoreType.DMA((7,)), pltpu.SemaphoreType.DMA],
    )(block)


def exchange_slabs(slabs, name):
    def body(in_ref, out_ref, send_sems, recv_sems, local_sem):
        x, y, c = _position()
        my = 4 * x + 2 * y + c
        mine = pltpu.make_async_copy(in_ref.at[my], out_ref.at[my], local_sem)
        mine.start()
        copies = []
        for k in range(1, N_DEV):
            dx, dy, dc = (k >> 2) & 1, (k >> 1) & 1, k & 1
            px = x if dx == 0 else 1 - x
            py = y if dy == 0 else 1 - y
            pc = c if dc == 0 else 1 - c
            cp = pltpu.make_async_remote_copy(
                src_ref=in_ref.at[4 * px + 2 * py + pc], dst_ref=out_ref.at[my],
                send_sem=send_sems.at[k - 1], recv_sem=recv_sems.at[k - 1],
                device_id=(px, py, pc), device_id_type=pl.DeviceIdType.MESH)
            cp.start()
            copies.append(cp)
        for cp in copies:
            cp.wait()
        mine.wait()

    return pl.pallas_call(
        body, name=name,
        out_shape=jax.ShapeDtypeStruct(slabs.shape, slabs.dtype),
        in_specs=[pl.BlockSpec(memory_space=pl.ANY)],
        out_specs=pl.BlockSpec(memory_space=pl.ANY),
        scratch_shapes=[pltpu.SemaphoreType.DMA((7,)), pltpu.SemaphoreType.DMA((7,)), pltpu.SemaphoreType.DMA],
    )(slabs)


def _pack(arrays, dtype, row_multiple):
    flat = jnp.concatenate([a.astype(dtype).reshape(-1) for a in arrays])
    n = flat.shape[0]
    per = PACK_W * row_multiple
    total = -(-n // per) * per
    flat = jnp.pad(flat, (0, total - n))
    return flat.reshape(total // PACK_W, PACK_W)


def _unpack(flat2d, shapes, lead=()):
    flat = flat2d.reshape(lead + (-1,))
    out, off = [], 0
    for shp in shapes:
        n = math.prod(shp)
        out.append(flat[..., off:off + n].reshape(lead + tuple(shp)))
        off += n
    return out


def _full_from_gathered(g, axis):
    g = jnp.moveaxis(g, 0, axis)
    shp = list(g.shape)
    shp[axis:axis + 2] = [shp[axis] * shp[axis + 1]]
    return g.reshape(shp)


def _shards_of_full(full, axis):
    shp = list(full.shape)
    shp[axis:axis + 1] = [N_DEV, shp[axis] // N_DEV]
    return jnp.moveaxis(full.reshape(shp), axis, 0)


def layer_units(i):
    mixer = [('m_in_w', 'm_out_w'), ('h_in_w', 'h_out_w'), ('g_in_w', 'g_out_w')][i % 3]
    return [(mixer[0], i // 3), (mixer[1], i // 3), ('xa_q', i), ('xa_kv', i), ('xa_o', i), ('f_up', i), ('f_down', i)]


PADDED_COLS = {'m_in_w': M_IN_PAD, 'g_in_w': G_IN_PAD}


def whole_weight(name, gathered):
    w = _full_from_gathered(lax.stop_gradient(gathered), SHARD_AXIS[name] - 1)
    return _pad_cols(w, 1, PADDED_COLS[name]) if name in PADDED_COLS else w


def _trunk(p, weights, blocks, standins, x, mem, bl, seq):
    t = bl * seq
    ia = ib = ic = 0
    weights = dict(weights)
    state = {}

    def lin(name, a, wname, idx, residual=None):
        unit = (wname, idx)
        pos = state['units'].index(unit)
        later = state['next'][pos] if state['next'] else None
        nxt = (blocks[later],) if later in blocks else ()
        n_real = N_DEV * standins[unit].shape[2]
        res = () if residual is None else (residual,)
        y, got = make_linear(name, SHARD_AXIS[wname] - 1, n_real, bool(res))(
            a, weights[unit], standins[unit], nxt, res)
        if nxt:
            weights[later] = whole_weight(later[0], got[0])
        return y

    by_seq = lambda a: a.reshape(bl, seq, a.shape[-1])

    for i in range(DEPTH):
        state['units'] = layer_units(i)
        state['next'] = layer_units(i + 1) if i + 1 < DEPTH else None
        hn, x = rmsnorm_op(f"ln_mix{i}", t, F32, residual=True)(x, p['ln_mix'][i:i + 1])
        kind = i % 3
        if kind == 0:
            proj = lin(f"m_in{i}", hn, 'm_in_w', ia)
            xbc = make_conv(f"m_conv{i}", bl, seq, 4, M_CONV_DIM, M_D_INNER)(
                proj, p['m_conv_w'][ia], p['m_conv_b'][ia])
            dt = proj[:, M_D_INNER + M_CONV_DIM:M_IN].reshape(bl, seq, M_GROUPS, 4).transpose(2, 0, 1, 3)
            grp = lambda a, n=4: a.reshape(M_GROUPS, 1, n)
            proj3, xbc3 = by_seq(proj), by_seq(xbc)
            y = ssd_op(f"ssd{i}", bl, seq)(
                proj3, xbc3, xbc3, xbc3, dt, grp(p['m_dt_bias'][ia]), grp(p['m_a_log'][ia]), grp(p['m_d'][ia]),
                grp(p['m_norm_w'][ia], 256))[0]
            x = lin(f"m_out{i}", y.reshape(t, M_D_INNER), 'm_out_w', ia, residual=x)
            ia += 1
        elif kind == 1:
            proj3 = by_seq(lin(f"h_in{i}", hn, 'h_in_w', ib))
            y = gla_op(f"gla{i}", i, bl, seq)(
                proj3, proj3, proj3, proj3, p['h_lower_bounds'], p['h_norm_w'][ib:ib + 1])[0]
            x = lin(f"h_out{i}", y.reshape(t, D_MODEL), 'h_out_w', ib, residual=x)
            ib += 1
        else:
            proj = lin(f"g_in{i}", hn, 'g_in_w', ic)
            qkv = make_conv(f"g_conv{i}", bl, seq, 4, G_CONV_DIM, 0)(proj, p['g_conv_w'][ic])
            c0 = G_CONV_DIM + G_VAL_DIM
            heads = lambda a: a.reshape(bl, seq, G_QK_HEADS, 2).transpose(2, 0, 1, 3)
            braw = heads(proj[:, c0:c0 + G_V_HEADS])
            araw = heads(proj[:, c0 + G_V_HEADS:c0 + 2 * G_V_HEADS])
            grp = lambda a: a.reshape(G_QK_HEADS, 1, 2)
            qkv3 = by_seq(qkv)
            y = gdn_op(f"gdn{i}", bl, seq)(
                qkv3, qkv3, qkv3, by_seq(proj), braw, araw, grp(p['g_a_log'][ic]), grp(p['g_dt_bias'][ic]),
                p['g_norm_w'][ic:ic + 1])[0]
            x = lin(f"g_out{i}", y.reshape(t, G_VAL_DIM), 'g_out_w', ic, residual=x)
            ic += 1
        hq, x = rmsnorm_op(f"ln_xattn{i}", t, F32, residual=True)(x, p['ln_xattn'][i:i + 1])
        mn = rmsnorm_op(f"ln_mem{i}", bl * N_MEM, F32)(mem, p['ln_mem'][i:i + 1])[0]
        qx = lin(f"xa_q{i}", hq, 'xa_q', i)
        kv = lin(f"xa_kv{i}", mn, 'xa_kv', i)
        ao = xattn_op(f"xattn{i}", bl, seq)(qx, kv, kv)[0]
        x = lin(f"xa_o{i}", ao, 'xa_o', i, residual=x)
        hf, x = rmsnorm_op(f"ln_ffn{i}", t, F32, residual=True)(x, p['ln_ffn'][i:i + 1])
        up = lin(f"f_up{i}", hf, 'f_up', i)
        act = make_conv(f"f_conv{i}", bl, seq, 3, D_FF, 0, up_col0=D_FF)(up, p['f_conv_w'][i], p['f_conv_b'][i])
        x = lin(f"f_down{i}", act, 'f_down', i, residual=x)
    return x


def _pad_cols(w, axis, to):
    pad = [(0, 0)] * w.ndim
    pad[axis] = (0, to - w.shape[axis])
    return jnp.pad(w, pad)


def kernel(x, mem, ln_mix, ln_xattn, ln_mem, ln_ffn, final_norm, m_in_w, m_conv_w, m_conv_b, m_dt_bias, m_a_log, m_d, m_norm_w, m_out_w, h_in_w, h_lower_bounds, h_norm_w, h_out_w, g_in_w, g_conv_w, g_a_log, g_dt_bias, g_norm_w, g_out_w, xa_q, xa_kv, xa_o, f_up, f_conv_w, f_conv_b, f_down, loss_target, m_ln_mix, m_ln_xattn, m_ln_mem, m_ln_ffn, m_final_norm, m_m_in_w, m_m_conv_w, m_m_conv_b, m_m_dt_bias, m_m_a_log, m_m_d, m_m_norm_w, m_m_out_w, m_h_in_w, m_h_lower_bounds, m_h_norm_w, m_h_out_w, m_g_in_w, m_g_conv_w, m_g_a_log, m_g_dt_bias, m_g_norm_w, m_g_out_w, m_xa_q, m_xa_kv, m_xa_o, m_f_up, m_f_conv_w, m_f_conv_b, m_f_down, v_ln_mix, v_ln_xattn, v_ln_mem, v_ln_ffn, v_final_norm, v_m_in_w, v_m_conv_w, v_m_conv_b, v_m_dt_bias, v_m_a_log, v_m_d, v_m_norm_w, v_m_out_w, v_h_in_w, v_h_lower_bounds, v_h_norm_w, v_h_out_w, v_g_in_w, v_g_conv_w, v_g_a_log, v_g_dt_bias, v_g_norm_w, v_g_out_w, v_xa_q, v_xa_kv, v_xa_o, v_f_up, v_f_conv_w, v_f_conv_b, v_f_down):
    local = dict(ln_mix=ln_mix, ln_xattn=ln_xattn, ln_mem=ln_mem, ln_ffn=ln_ffn, final_norm=final_norm, m_in_w=m_in_w, m_conv_w=m_conv_w, m_conv_b=m_conv_b, m_dt_bias=m_dt_bias, m_a_log=m_a_log, m_d=m_d, m_norm_w=m_norm_w, m_out_w=m_out_w, h_in_w=h_in_w, h_lower_bounds=h_lower_bounds, h_norm_w=h_norm_w, h_out_w=h_out_w, g_in_w=g_in_w, g_conv_w=g_conv_w, g_a_log=g_a_log, g_dt_bias=g_dt_bias, g_norm_w=g_norm_w, g_out_w=g_out_w, xa_q=xa_q, xa_kv=xa_kv, xa_o=xa_o, f_up=f_up, f_conv_w=f_conv_w, f_conv_b=f_conv_b, f_down=f_down)
    mom_m = dict(ln_mix=m_ln_mix, ln_xattn=m_ln_xattn, ln_mem=m_ln_mem, ln_ffn=m_ln_ffn, final_norm=m_final_norm, m_in_w=m_m_in_w, m_conv_w=m_m_conv_w, m_conv_b=m_m_conv_b, m_dt_bias=m_m_dt_bias, m_a_log=m_m_a_log, m_d=m_m_d, m_norm_w=m_m_norm_w, m_out_w=m_m_out_w, h_in_w=m_h_in_w, h_lower_bounds=m_h_lower_bounds, h_norm_w=m_h_norm_w, h_out_w=m_h_out_w, g_in_w=m_g_in_w, g_conv_w=m_g_conv_w, g_a_log=m_g_a_log, g_dt_bias=m_g_dt_bias, g_norm_w=m_g_norm_w, g_out_w=m_g_out_w, xa_q=m_xa_q, xa_kv=m_xa_kv, xa_o=m_xa_o, f_up=m_f_up, f_conv_w=m_f_conv_w, f_conv_b=m_f_conv_b, f_down=m_f_down)
    mom_v = dict(ln_mix=v_ln_mix, ln_xattn=v_ln_xattn, ln_mem=v_ln_mem, ln_ffn=v_ln_ffn, final_norm=v_final_norm, m_in_w=v_m_in_w, m_conv_w=v_m_conv_w, m_conv_b=v_m_conv_b, m_dt_bias=v_m_dt_bias, m_a_log=v_m_a_log, m_d=v_m_d, m_norm_w=v_m_norm_w, m_out_w=v_m_out_w, h_in_w=v_h_in_w, h_lower_bounds=v_h_lower_bounds, h_norm_w=v_h_norm_w, h_out_w=v_h_out_w, g_in_w=v_g_in_w, g_conv_w=v_g_conv_w, g_a_log=v_g_a_log, g_dt_bias=v_g_dt_bias, g_norm_w=v_g_norm_w, g_out_w=v_g_out_w, xa_q=v_xa_q, xa_kv=v_xa_kv, xa_o=v_xa_o, f_up=v_f_up, f_conv_w=v_f_conv_w, f_conv_b=v_f_conv_b, f_down=v_f_down)

    bl, seq, _ = x.shape
    t = bl * seq

    p = {n: local[n] for n in WEIGHTS if n not in SHARD_AXIS}
    for n in SMALL_SHARDED:
        p[n] = _full_from_gathered(all_gather(local[n], f"gather_{n}"), SHARD_AXIS[n])
    units = [(n, l) for n in MATMUL_WEIGHTS for l in range(local[n].shape[0])]
    block = lambda u: local[u[0]][u[1]].astype(BF16)
    weights = {u: whole_weight(u[0], all_gather(block(u), f"gather_{u[0]}{u[1]}")) for u in layer_units(0)}
    blocks = {u: block(u) for u in units if u not in weights}
    standins = {u: jnp.zeros((N_DEV,) + local[u[0]].shape[1:], BF16) for u in units}
    small = {n: p[n] for n in WEIGHTS if n not in MATMUL_WEIGHTS and n != 'final_norm'}

    def run(small_w, standins_, xin):
        return _trunk(small_w, weights, blocks, standins_, xin, mem.reshape(bl * N_MEM, D_MODEL), bl, seq)

    x_out, vjp = jax.vjp(run, small, standins, x.reshape(t, D_MODEL))
    loss_part, dx_out, d_final = loss_head(x_out, final_norm, loss_target.reshape(t, D_MODEL))
    grads, received, dx = vjp(dx_out)
    grads = dict(grads)
    grads['final_norm'] = d_final
    loss = lax.psum(loss_part, ("x", "y", "c"))

    outs = {}

    def update(name, n, slots, shape, sel=lambda a: a):
        two_d = lambda a: sel(a).reshape(slots.shape[1:])
        got = reduce_adamw(slots, two_d(local[n]), two_d(mom_m[n]), two_d(mom_v[n]), name)
        return [g.reshape(shape) for g in got]

    for n in SMALL_SHARDED:
        slots = exchange_slabs(_shards_of_full(grads[n], SHARD_AXIS[n]), f"exchange_{n}")
        slots = slots.reshape(N_DEV, -1, slots.shape[-1])
        for kind, a in zip(KINDS, update(f"adamw_{n}", n, slots, local[n].shape)):
            outs[kind, n] = a
    for n in MATMUL_WEIGHTS:
        per_layer = [update(f"adamw_{n}{l}", n, received[n, l], local[n].shape[1:], lambda a, l=l: a[l])
                     for l in range(local[n].shape[0])]
        for k, kind in enumerate(KINDS):
            outs[kind, n] = jnp.concatenate([got[k][None] for got in per_layer])
    replicated = [n for n in WEIGHTS if n not in SHARD_AXIS]
    pk = lambda d: _pack([d[n] for n in replicated], F32, 8)
    got = reduce_adamw(all_gather(pk(grads), "gather_replicated_grads"), pk(local), pk(mom_m), pk(mom_v),
                       "adamw_replicated")
    shapes = [local[n].shape for n in replicated]
    for kind, buf in zip(KINDS, got):
        for n, a in zip(replicated, _unpack(buf, shapes)):
            outs[kind, n] = a
    result = [loss, dx.reshape(bl, seq, D_MODEL)]
    for kind in KINDS:
        result += [outs[kind, n] for n in WEIGHTS]
    return tuple(result)
```

```python
import functools
import math

import jax
import jax.numpy as jnp
from jax import lax
from jax.experimental import pallas as pl
from jax.experimental.pallas import tpu as pltpu

F32 = jnp.float32
BF16 = jnp.bfloat16
NN = (((1,), (0,)), ((), ()))
NT = (((1,), (1,)), ((), ()))
TN = (((0,), (0,)), ((), ()))

D_MODEL = 1024
DEPTH = 4
EPS = 1e-6
N_MEM = 256
M_D_INNER = 2048
M_HEADS = 32
M_GROUPS = 8
M_STATE = 128
M_CONV_DIM = 4096
M_IN = 6176
M_IN_PAD = 6272
SSD_CHUNK = 256
H_HEADS = 8
HGRN_CHUNK = 32
HGRN_ROWS = 128
G_QK_HEADS = 8
G_V_HEADS = 16
G_KEY_DIM = 1024
G_VAL_DIM = 2048
G_CONV_DIM = 4096
G_IN = 6176
G_IN_PAD = 6272
GDN_CHUNK = 64
X_HEADS = 4
X_HEAD_DIM = 256
D_FF = 2816
ADAM_LR = 0.001
ADAM_B1 = 0.9
ADAM_B2 = 0.999
ADAM_EPS = 1e-08
ADAM_WD = 0.01
ADAM_STEP = 10

N_DEV = 8
LANE = 128
KINDS = ('grad', 'delta', 'new_m', 'new_v')
VMEM_LIMIT = 56 * 1024 * 1024

WEIGHTS = ['ln_mix', 'ln_xattn', 'ln_mem', 'ln_ffn', 'final_norm', 'm_in_w', 'm_conv_w', 'm_conv_b', 'm_dt_bias',
           'm_a_log', 'm_d', 'm_norm_w', 'm_out_w', 'h_in_w', 'h_lower_bounds', 'h_norm_w', 'h_out_w', 'g_in_w',
           'g_conv_w', 'g_a_log', 'g_dt_bias', 'g_norm_w', 'g_out_w', 'xa_q', 'xa_kv', 'xa_o', 'f_up', 'f_conv_w',
           'f_conv_b', 'f_down']
SHARD_AXIS = {'m_in_w': 2, 'm_conv_w': 2, 'm_conv_b': 1, 'm_norm_w': 1, 'm_out_w': 1, 'h_in_w': 2, 'h_out_w': 1,
              'g_in_w': 2, 'g_conv_w': 2, 'g_out_w': 1, 'xa_q': 1, 'xa_kv': 2, 'xa_o': 1, 'f_up': 2, 'f_conv_w': 2,
              'f_down': 1}
MATMUL_WEIGHTS = ['m_in_w', 'm_out_w', 'h_in_w', 'h_out_w', 'g_in_w', 'g_out_w', 'xa_q', 'xa_kv', 'xa_o', 'f_up',
                  'f_down']
SMALL_SHARDED = ['m_conv_w', 'm_conv_b', 'm_norm_w', 'g_conv_w', 'f_conv_w']


def _cparams():
    return pltpu.CompilerParams(vmem_limit_bytes=VMEM_LIMIT)


def bdot(a, b, dims=NN):
    return lax.dot_general(a.astype(BF16), b.astype(BF16), dims, preferred_element_type=F32)


def _split(a):
    hi = a.astype(BF16)
    return hi, (a - hi.astype(F32)).astype(BF16)


def _h3(a, b, dims):
    ah, al = _split(a)
    bh, bl = _split(b)
    d = functools.partial(lax.dot_general, dimension_numbers=dims, preferred_element_type=F32)
    return d(ah, bh) + (d(ah, bl) + d(al, bh))


BNN = (((2,), (1,)), ((0,), (0,)))
BNT = (((2,), (2,)), ((0,), (0,)))
BTN = (((1,), (1,)), ((0,), (0,)))


@jax.custom_vjp
def h3dot_b(a, b):
    return _h3(a, b, BNN)


h3dot_b.defvjp(lambda a, b: (_h3(a, b, BNN), (a, b)),
               lambda res, ct: (_h3(ct, res[1], BNT), _h3(res[0], ct, BTN)))

T_ROWS = (((0,), (1,)), ((), ()))


def _tri_times(tri, x, dims, tri_first):
    t = tri.astype(BF16)
    x0 = x.astype(BF16)
    r1 = x - x0.astype(F32)
    x1 = r1.astype(BF16)
    x2 = (r1 - x1.astype(F32)).astype(BF16)
    if tri_first:
        d = lambda xx: lax.dot_general(t, xx, dims, preferred_element_type=F32)
    else:
        d = lambda xx: lax.dot_general(xx, t, dims, preferred_element_type=F32)
    return d(x0) + (d(x1) + d(x2))


@jax.custom_vjp
def cumdot(tri, x):
    return _tri_times(tri, x, NN, True)


cumdot.defvjp(lambda tri, x: (_tri_times(tri, x, NN, True), tri),
              lambda tri, ct: (jnp.zeros_like(tri), _tri_times(tri, ct, TN, True)))


@jax.custom_vjp
def cumdot_t(tri, x):
    return _tri_times(tri, x, T_ROWS, False)


cumdot_t.defvjp(lambda tri, x: (_tri_times(tri, x, T_ROWS, False), tri),
                lambda tri, ct: (jnp.zeros_like(tri), _tri_times(tri, ct, T_ROWS, True)))


def _tile(dim, cap):
    if dim <= cap:
        return dim
    best = None
    for t in range(LANE, cap + 1, LANE):
        if dim % t == 0:
            best = t
    assert best is not None, dim
    return best


def _position():
    return lax.axis_index("x"), lax.axis_index("y"), lax.axis_index("c")


def _direct_copies(kind, src_ref, dst_ref, send_sems, recv_sems, local_sem):
    x, y, c = _position()
    me = 4 * x + 2 * y + c
    local_src = src_ref if kind == 'gather' else src_ref.at[me]
    copies = [pltpu.make_async_copy(local_src, dst_ref.at[me], local_sem)]
    for k in range(1, N_DEV):
        px = 1 - x if (k >> 2) & 1 else x
        py = 1 - y if (k >> 1) & 1 else y
        pc = 1 - c if k & 1 else c
        copies.append(pltpu.make_async_remote_copy(
            src_ref=src_ref if kind == 'gather' else src_ref.at[4 * px + 2 * py + pc], dst_ref=dst_ref.at[me],
            send_sem=send_sems.at[k - 1], recv_sem=recv_sems.at[k - 1],
            device_id=(px, py, pc), device_id_type=pl.DeviceIdType.MESH))
    return copies


COMM_SCRATCH = [pltpu.SemaphoreType.DMA((N_DEV - 1,)), pltpu.SemaphoreType.DMA((N_DEV - 1,)), pltpu.SemaphoreType.DMA]


def matmul(a, b, *, ta=False, tb=False, out_dtype=F32, name="mm", carry=None, residual=None):
    if ta:
        k, m = a.shape
    else:
        m, k = a.shape
    if tb:
        n, k2 = b.shape
    else:
        k2, n = b.shape
    assert k == k2, (a.shape, b.shape, ta, tb)
    tm = _tile(m, 1024)
    tn = _tile(n, 1408)
    tk = _tile(k, 1408)
    grid = (m // tm, n // tn, k // tk)
    nk = grid[2]
    dims = (((0 if ta else 1,), (1 if tb else 0,)), ((), ()))

    def at_step(which):
        conds = [pl.program_id(ax) == (0 if which == 'first' else grid[ax] - 1) for ax in range(3)]
        return jnp.logical_and(jnp.logical_and(conds[0], conds[1]), conds[2])

    def body(*refs):
        r_ref = None
        if residual is not None:
            r_ref, refs = refs[2], refs[:2] + refs[3:]
        if carry is None:
            a_ref, b_ref, o_ref, acc_ref = refs
        else:
            a_ref, b_ref, src_ref, o_ref, dst_ref, acc_ref, send_sems, recv_sems, local_sem = refs
            copies = lambda: _direct_copies(carry[0], src_ref, dst_ref, send_sems, recv_sems, local_sem)

            @pl.when(at_step('first'))
            def _():
                for cp in copies():
                    cp.start()

        @pl.when(pl.program_id(2) == 0)
        def _():
            acc_ref[...] = jnp.zeros_like(acc_ref)

        acc_ref[...] += lax.dot_general(a_ref[...].astype(BF16), b_ref[...].astype(BF16), dims,
                                        preferred_element_type=F32)

        @pl.when(pl.program_id(2) == nk - 1)
        def _():
            out = acc_ref[...] if r_ref is None else acc_ref[...] + r_ref[...]
            o_ref[...] = out.astype(o_ref.dtype)

        if carry is not None:
            @pl.when(at_step('last'))
            def _():
                for cp in copies():
                    cp.wait()

    a_spec = pl.BlockSpec((tk, tm), lambda i, j, kk: (kk, i)) if ta else pl.BlockSpec((tm, tk), lambda i, j, kk: (i, kk))
    b_spec = pl.BlockSpec((tn, tk), lambda i, j, kk: (j, kk)) if tb else pl.BlockSpec((tk, tn), lambda i, j, kk: (kk, j))
    o_spec = pl.BlockSpec((tm, tn), lambda i, j, kk: (i, j))
    o_shape = jax.ShapeDtypeStruct((m, n), out_dtype)
    acc = pltpu.VMEM((tm, tn), F32)
    ins, in_specs = [a, b], [a_spec, b_spec]
    if residual is not None:
        ins.append(residual)
        in_specs.append(o_spec)
    if carry is None:
        return pl.pallas_call(
            body, name=name, grid=grid, in_specs=in_specs, out_specs=o_spec, out_shape=o_shape,
            scratch_shapes=[acc], compiler_params=_cparams(),
        )(*ins)
    kind, src = carry
    got = jax.ShapeDtypeStruct(((N_DEV,) + src.shape) if kind == 'gather' else src.shape, src.dtype)
    hbm = pl.BlockSpec(memory_space=pl.ANY)
    return pl.pallas_call(
        body, name=name, grid=grid, in_specs=in_specs + [hbm], out_specs=[o_spec, hbm],
        out_shape=[o_shape, got], scratch_shapes=[acc] + COMM_SCRATCH, compiler_params=_cparams(),
    )(*ins, src)


def make_linear(name, shard_axis, n_real, has_res=False):
    def forward(a, w, nxt, res):
        r = res[0] if res else None
        if nxt:
            y, got = matmul(a, w, name=name + "_fwd", carry=('gather', nxt[0]), residual=r)
            return y, (got,)
        return matmul(a, w, name=name + "_fwd", residual=r), ()

    @jax.custom_vjp
    def linear(a, w, wg, nxt, res):
        return forward(a, w, nxt, res)

    def fwd(a, w, wg, nxt, res):
        return forward(a, w, nxt, res), (a, w, nxt)

    def bwd(saved, cts):
        a, w, nxt = saved
        dy = cts[0]
        dw = matmul(a, dy, ta=True, out_dtype=BF16, name=name + "_bwd_dw")
        slabs = _shards_of_full(dw[:, :n_real], shard_axis)
        da, slots = matmul(dy, w, tb=True, out_dtype=a.dtype, name=name + "_bwd_da", carry=('exchange', slabs))
        return da, jnp.zeros_like(w), slots, tuple(jnp.zeros_like(b) for b in nxt), ((dy,) if has_res else ())

    linear.defvjp(fwd, bwd)
    return linear


class In:
    def __init__(self, block, imap, kind='blk', inner=(), cols=None):
        self.block, self.imap, self.kind, self.inner, self.cols = block, imap, kind, inner, cols


class Out:
    def __init__(self, shape, dtype, block, imap):
        self.shape, self.dtype, self.block, self.imap = shape, dtype, block, imap


def make_op(name, fn, grid, ins, outs, state_shape=None, seq_axis=None, passthrough=()):
    n_in, n_out = len(ins), len(outs)
    has_state = state_shape is not None
    nd = len(grid)
    diff_idx = [i for i, s in enumerate(ins) if s.kind != 'const']

    def in_spec(s, reverse):
        off = 0
        if s.cols is not None:
            assert s.cols[0] % s.block[-1] == 0
            off = s.cols[0] // s.block[-1]

        def imap(*ids):
            ids = rev(ids) if reverse else ids
            idx = tuple(s.imap(*ids))
            return idx[:-1] + (idx[-1] + off,) if off else idx

        return pl.BlockSpec(s.block, imap)

    def rel_spec(block, f, reverse):
        return pl.BlockSpec(block, (lambda *ids: f(*rev(ids))) if reverse else f)

    def rev(ids):
        if not has_state:
            return ids
        ids = list(ids)
        ids[seq_axis] = grid[seq_axis] - 1 - ids[seq_axis]
        return tuple(ids)

    save_shape = tuple(grid) + tuple(state_shape) if has_state else None
    save_block = (None,) * nd + tuple(state_shape) if has_state else None

    def save_imap(*ids):
        return tuple(ids) + (0,) * len(state_shape)

    def fwd_call(*xs):
        def body(*refs):
            in_refs = refs[:n_in]
            out_refs = refs[n_in:n_in + n_out]
            vals = [r[...] for r in in_refs]
            if has_state:
                save_ref, st_ref = refs[n_in + n_out], refs[n_in + n_out + 1]

                @pl.when(pl.program_id(seq_axis) == 0)
                def _():
                    st_ref[...] = jnp.zeros(state_shape, F32)

                st = st_ref[...]
                save_ref[...] = st
                res = fn(*vals, st)
                st_ref[...] = res[-1]
                res = res[:-1]
            else:
                res = fn(*vals)
            for o, v in zip(out_refs, res):
                o[...] = v.astype(o.dtype)

        out_shape = [jax.ShapeDtypeStruct(o.shape, o.dtype) for o in outs]
        out_specs = [pl.BlockSpec(o.block, o.imap) for o in outs]
        scratch = []
        if has_state:
            out_shape.append(jax.ShapeDtypeStruct(save_shape, F32))
            out_specs.append(pl.BlockSpec(save_block, save_imap))
            scratch.append(pltpu.VMEM(state_shape, F32))
        return pl.pallas_call(
            body, name=name + "_fwd", grid=grid,
            in_specs=[in_spec(s, False) for s in ins],
            out_specs=out_specs, out_shape=out_shape, scratch_shapes=scratch,
            compiler_params=_cparams(),
        )(*xs)

    def grad_shape(s, x):
        if s.cols is not None:
            return x.shape[:-1] + (s.cols[1],)
        return x.shape

    def bwd_call(xs, save, cts, pass_cts=()):
        n_diff = len(diff_idx)

        def body(*refs):
            in_refs = refs[:n_in]
            p = n_in
            if has_state:
                save_ref = refs[p]
                p += 1
            ct_refs = refs[p:p + n_out]
            p += n_out
            pass_refs = dict(zip(passthrough, refs[p:p + len(passthrough)]))
            p += len(passthrough)
            g_refs = refs[p:p + n_diff]
            p += n_diff
            vals = [r[...] for r in in_refs]

            def g(*dv):
                full = list(vals)
                for i, v in zip(diff_idx, dv):
                    full[i] = v
                if has_state:
                    return tuple(fn(*full, dv[-1]))
                return tuple(fn(*full))

            prim = [vals[i] for i in diff_idx]
            ct = tuple(r[...].astype(F32) for r in ct_refs)
            if has_state:
                dst_ref = refs[p]

                @pl.when(pl.program_id(seq_axis) == 0)
                def _():
                    dst_ref[...] = jnp.zeros(state_shape, F32)

                prim = prim + [save_ref[...]]
                ct = ct + (dst_ref[...],)
            _, vjp = jax.vjp(g, *prim)
            grads = vjp(ct)
            for k, i in enumerate(diff_idx):
                s = ins[i]
                if s.kind == 'blk':
                    g = grads[k] + pass_refs[i][...] if i in pass_refs else grads[k]
                    g_refs[k][...] = g.astype(g_refs[k].dtype)
                else:
                    first = None
                    for ax in s.inner:
                        c = pl.program_id(ax) == 0
                        first = c if first is None else jnp.logical_and(first, c)

                    @pl.when(first)
                    def _(k=k):
                        g_refs[k][...] = jnp.zeros_like(g_refs[k])

                    g_refs[k][...] += grads[k].astype(g_refs[k].dtype)
            if has_state:
                dst_ref[...] = grads[-1]

        in_specs = [in_spec(s, True) for s in ins]
        args = list(xs)
        if has_state:
            in_specs.append(rel_spec(save_block, save_imap, True))
            args.append(save)
        for o, c in zip(outs, cts):
            in_specs.append(rel_spec(o.block, o.imap, True))
            args.append(c)
        for i, c in zip(passthrough, pass_cts):
            assert ins[i].kind == 'blk' and ins[i].cols is None
            in_specs.append(rel_spec(ins[i].block, ins[i].imap, True))
            args.append(c)
        out_shape, out_specs = [], []
        for i in diff_idx:
            s = ins[i]
            out_shape.append(jax.ShapeDtypeStruct(grad_shape(s, xs[i]), xs[i].dtype))
            out_specs.append(rel_spec(s.block, s.imap, True))
        scratch = [pltpu.VMEM(state_shape, F32)] if has_state else []
        return pl.pallas_call(
            body, name=name + "_bwd", grid=grid,
            in_specs=in_specs, out_specs=out_specs, out_shape=out_shape, scratch_shapes=scratch,
            compiler_params=_cparams(),
        )(*args)

    @jax.custom_vjp
    def op(*xs):
        return tuple(fwd_call(*xs)[:n_out]) + tuple(xs[i] for i in passthrough)

    def op_fwd(*xs):
        res = fwd_call(*xs)
        return tuple(res[:n_out]) + tuple(xs[i] for i in passthrough), (xs, res[n_out] if has_state else None)

    def op_bwd(resid, cts):
        xs, save = resid
        grads = bwd_call(xs, save, cts[:n_out], cts[n_out:])
        out = []
        k = 0
        for i, s in enumerate(ins):
            if s.kind == 'const':
                out.append(jnp.zeros_like(xs[i]))
                continue
            g = grads[k]
            k += 1
            if s.cols is not None:
                g = jnp.pad(g, ((0, 0),) * (g.ndim - 1) + ((s.cols[0], xs[i].shape[-1] - s.cols[0] - s.cols[1]),))
            out.append(g)
        return tuple(out)

    op.defvjp(op_fwd, op_bwd)
    return op


def _rms(x, w):
    return x * lax.rsqrt(jnp.mean(x * x, axis=-1, keepdims=True) + EPS) * w


def _silu(x):
    return x * jax.nn.sigmoid(x)


def rmsnorm_op(name, t, out_dtype, residual=False):
    tm = _tile(t, 512)
    return make_op(
        name, lambda x, w: (_rms(x, w),), (t // tm,),
        [In((tm, D_MODEL), lambda i: (i, 0)), In((1, D_MODEL), lambda i: (0, 0), 'acc', (0,))],
        [Out((t, D_MODEL), out_dtype, (tm, D_MODEL), lambda i: (i, 0))], passthrough=(0,) if residual else ())


def _tri(q):
    ii = lax.broadcasted_iota(jnp.int32, (q, q), 0)
    jj = lax.broadcasted_iota(jnp.int32, (q, q), 1)
    return ii >= jj, ii > jj


def _ssd_fn(z, x, bm, cm, dtr, dtb, alog, dsk, nw, state):
    q = x.shape[0]
    incl, _ = _tri(q)
    tril = incl.astype(F32)
    dt = jax.nn.softplus(dtr + dtb)
    da = dt * (-jnp.exp(alog))
    acum = cumdot(tril, da)
    acum_t = cumdot_t(tril, da)
    cb = bdot(cm, bm, NT)
    heads = range(4)
    wide = lambda a: jnp.concatenate([jnp.broadcast_to(a[:, r:r + 1], (a.shape[0], 64)) for r in heads], axis=1)
    last = acum[q - 1:q, :]
    xc = x * wide(dt)
    y = bdot(cm, state, NT) * wide(jnp.exp(acum)) + wide(dsk) * x
    ds = bdot(xc * wide(jnp.exp(last - acum)), bm, TN)
    e_last = jnp.exp(last)
    new_state = state * jnp.concatenate([jnp.broadcast_to(e_last[:, r:r + 1], (64, 1)) for r in heads], axis=0) + ds
    diag = []
    for r in heads:
        decay = jnp.exp(jnp.where(incl, acum[:, r:r + 1] - acum_t[r:r + 1, :], -jnp.inf))
        diag.append(bdot(cb * decay, xc[:, 64 * r:64 * r + 64]))
    y = y + jnp.concatenate(diag, axis=1)
    yz = y * _silu(z)
    return _rms(yz, nw), new_state


def _per_sequence(fn, n_seq_args, bl):
    def f(*args):
        *ins, state = args
        res = [fn(*[a[b] for a in ins[:n_seq_args]], *ins[n_seq_args:], state[b]) for b in range(bl)]
        return tuple(jnp.concatenate([r[k][None] for r in res]) for k in range(len(res[0])))

    return f


def ssd_op(name, bl, seq):
    q = SSD_CHUNK
    nc = seq // q
    blk = lambda w, c0, cw: In((bl, q, w), lambda g, n: (0, n, g), cols=(c0, cw))
    small = lambda g, n: (g, 0, 0)
    ins = [
        blk(256, 0, M_D_INNER),
        blk(256, 0, M_D_INNER),
        blk(128, M_D_INNER, 1024),
        blk(128, M_D_INNER + 1024, 1024),
        In((None, bl, q, 4), lambda g, n: (g, 0, n, 0)),
        In((None, 1, 4), small, 'acc', (1,)),
        In((None, 1, 4), small, 'acc', (1,)),
        In((None, 1, 4), small, 'acc', (1,)),
        In((None, 1, 256), small, 'acc', (1,)),
    ]
    outs = [Out((bl, seq, M_D_INNER), F32, (bl, q, 256), lambda g, n: (0, n, g))]
    return make_op(name, _per_sequence(_ssd_fn, 5, bl), (M_GROUPS, nc), ins, outs,
                   state_shape=(bl, 256, 128), seq_axis=1)


def _gla_fn(layer, qr, fr, ir, gr, lbp, nw, state_t):
    rows = qr.shape[0]
    c = HGRN_CHUNK
    e = jnp.exp(lbp - jnp.max(lbp, axis=0, keepdims=True))
    sm = e / jnp.sum(e, axis=0, keepdims=True)
    lb = jnp.sum(sm[1:layer + 1, :], axis=0, keepdims=True) if layer > 0 else jnp.zeros((1, lbp.shape[1]), F32)
    qq = _silu(qr) * (128 ** -0.5)
    forget = lb + (1.0 - lb) * jax.nn.sigmoid(fr)
    kk = 1.0 - forget
    logf = jnp.log(forget)
    incl, _ = _tri(c)
    tril = incl.astype(F32)
    os_ = []
    for j in range(rows // c):
        sl = slice(c * j, c * j + c)
        gc = cumdot(tril, logf[sl])
        glast = gc[c - 1:c, :]
        q_dec = qq[sl] * jnp.exp(gc)
        k_inv = kk[sl] * jnp.exp(-gc)
        k_end = kk[sl] * jnp.exp(glast - gc)
        att = jnp.where(incl, bdot(q_dec, k_inv, NT), 0.0)
        os_.append(bdot(att, ir[sl]) + bdot(q_dec, state_t, NT))
        state_t = state_t * jnp.exp(glast) + bdot(ir[sl], k_end, TN)
    o = jnp.concatenate(os_, axis=0)
    return _rms(o, nw) * _silu(gr), state_t


def gla_op(name, layer, bl, seq):
    r = HGRN_ROWS
    ns = seq // r
    blk = lambda k: In((bl, r, 128), lambda h, n: (0, n, h), cols=(1024 * k, 1024))
    ins = [blk(0), blk(1), blk(2), blk(3),
           In((DEPTH, 128), lambda h, n: (0, h), 'acc', (1,)),
           In((1, 128), lambda h, n: (0, 0), 'acc', (0, 1))]
    outs = [Out((bl, seq, D_MODEL), F32, (bl, r, 128), lambda h, n: (0, n, h))]
    return make_op(name, _per_sequence(functools.partial(_gla_fn, layer), 4, bl), (H_HEADS, ns), ins, outs,
                   state_shape=(bl, 128, 128), seq_axis=1)


def _unit_lower_inverse(m):
    q = m.shape[1]
    ii = lax.broadcasted_iota(jnp.int32, (q, q), 0)
    jj = lax.broadcasted_iota(jnp.int32, (q, q), 1)
    eye = (ii == jj).astype(F32)[None]
    p = -m
    inv = eye + p
    for _ in range(int(math.log2(q)) - 1):
        p = h3dot_b(p, p)
        inv = inv + h3dot_b(inv, p)
    return inv


def _gdn_fn(qc, kc, vc, zc, br, ar, alog, dtb, nw, state):
    bl, q = qc.shape[0], qc.shape[1]
    incl, strict = _tri(q)
    tril = incl.astype(F32)
    g = jnp.concatenate([-jnp.exp(alog) * jax.nn.softplus(ar[b] + dtb) for b in range(bl)], axis=1)
    gc = cumdot(tril, g)
    gc_t = cumdot_t(tril, g)
    heads, ms, rhs = [], [], []
    for b in range(bl):
        qn = qc[b] * lax.rsqrt(jnp.sum(qc[b] * qc[b], axis=-1, keepdims=True) + EPS) * (128 ** -0.5)
        kn = kc[b] * lax.rsqrt(jnp.sum(kc[b] * kc[b], axis=-1, keepdims=True) + EPS)
        beta = jax.nn.sigmoid(br[b])
        qk = bdot(qn, kn, NT)
        for j in range(2):
            i = 2 * b + j
            col = gc[:, i:i + 1]
            decay = jnp.exp(jnp.where(incl, col - gc_t[i:i + 1, :], -jnp.inf))
            bj = beta[:, j:j + 1]
            kb = kn * bj
            ms.append(jnp.where(strict, bdot(kb, kn, NT) * decay, 0.0))
            rhs.append(jnp.concatenate([vc[b][:, 128 * j:128 * j + 128] * bj, kb * jnp.exp(col)], axis=1))
            heads.append((qn, kn, qk * decay, col, gc[q - 1:q, i:i + 1]))
    sol = h3dot_b(_unit_lower_inverse(jnp.concatenate([m[None] for m in ms])),
                  jnp.concatenate([r[None] for r in rhs]))
    outs, states = [], []
    for b in range(bl):
        os_, sts = [], []
        for j in range(2):
            i = 2 * b + j
            qn, kn, att, col, glast = heads[i]
            u = sol[i][:, :128]
            w = sol[i][:, 128:]
            st = state[b][128 * j:128 * j + 128, :]
            v_new = u - bdot(w, st)
            o = bdot(qn * jnp.exp(col), st) + bdot(att, v_new)
            sts.append(st * jnp.exp(glast) + bdot(kn * jnp.exp(glast - col), v_new, TN))
            os_.append(_rms(o, nw) * _silu(zc[b][:, 128 * j:128 * j + 128]))
        outs.append(jnp.concatenate(os_, axis=1))
        states.append(jnp.concatenate(sts, axis=0))
    return jnp.concatenate([o[None] for o in outs]), jnp.concatenate([st[None] for st in states])


def gdn_op(name, bl, seq):
    q = GDN_CHUNK
    nc = seq // q
    blk = lambda w, c0, cw: In((bl, q, w), lambda h, n: (0, n, h), cols=(c0, cw))
    small = lambda h, n: (h, 0, 0)
    ins = [
        blk(128, 0, G_KEY_DIM),
        blk(128, G_KEY_DIM, G_KEY_DIM),
        blk(256, 2 * G_KEY_DIM, G_VAL_DIM),
        blk(256, G_CONV_DIM, G_VAL_DIM),
        In((None, bl, q, 2), lambda h, n: (h, 0, n, 0)),
        In((None, bl, q, 2), lambda h, n: (h, 0, n, 0)),
        In((None, 1, 2), small, 'acc', (1,)),
        In((None, 1, 2), small, 'acc', (1,)),
        In((1, 128), lambda h, n: (0, 0), 'acc', (0, 1)),
    ]
    outs = [Out((bl, seq, G_VAL_DIM), F32, (bl, q, 256), lambda h, n: (0, n, h))]
    return make_op(name, _gdn_fn, (G_QK_HEADS, nc), ins, outs,
                   state_shape=(bl, 256, 128), seq_axis=1)


def _xattn_fn(q, k, v):
    s = bdot(q, k, NT) * (X_HEAD_DIM ** -0.5)
    s = s - jnp.max(s, axis=-1, keepdims=True)
    p = jnp.exp(s)
    p = p / jnp.sum(p, axis=-1, keepdims=True)
    return (bdot(p, v),)


def xattn_op(name, bl, seq):
    tq = _tile(seq, 512)
    nq = seq // tq
    t = bl * seq
    ins = [
        In((tq, X_HEAD_DIM), lambda b, h, i: (b * nq + i, h)),
        In((N_MEM, X_HEAD_DIM), lambda b, h, i: (b, h), 'acc', (2,), cols=(0, D_MODEL)),
        In((N_MEM, X_HEAD_DIM), lambda b, h, i: (b, h), 'acc', (2,), cols=(D_MODEL, D_MODEL)),
    ]
    outs = [Out((t, D_MODEL), F32, (tq, X_HEAD_DIM), lambda b, h, i: (b * nq + i, h))]
    return make_op(name, _xattn_fn, (bl, X_HEADS, nq), ins, outs)


CONV_PAD = 8
CONV_ROWS = 64


def make_conv(name, bl, seq, width, ch, x_col0, up_col0=None):
    cb = 256
    rt = CONV_ROWS
    assert ch % cb == 0 and x_col0 % cb == 0 and seq % rt == 0
    assert up_col0 is None or (x_col0 == 0 and up_col0 == ch)
    nb = ch // cb
    n_tiles = seq // rt
    t = bl * seq
    has_up = up_col0 is not None
    grid = (nb, bl)
    x_spec = pl.BlockSpec((seq, cb), lambda c, b: (b, x_col0 // cb + c))
    up_specs = [pl.BlockSpec((seq, cb), lambda c, b: (b, up_col0 // cb + c))] if has_up else []
    w_spec = pl.BlockSpec((width, cb), lambda c, b: (0, c))
    b_spec = pl.BlockSpec((1, cb), lambda c, b: (0, c))
    o_spec = pl.BlockSpec((seq, cb), lambda c, b: (b, c))
    taps = [CONV_PAD - (width - 1) + j for j in range(width)]

    def window(x_ref, i):
        if isinstance(i, int) and i == 0:
            return jnp.concatenate([jnp.zeros((CONV_PAD, cb), F32), x_ref[0:rt, :]], axis=0)
        return x_ref[pl.ds(pl.multiple_of(i * rt - CONV_PAD, CONV_PAD), rt + CONV_PAD), :]

    def rows(i):
        return pl.ds(i * rt, rt) if isinstance(i, int) else pl.ds(pl.multiple_of(i * rt, rt), rt)

    def shifted(win):
        return [win[tp:tp + rt, :] for tp in taps]

    def pre_activation(views, w, b):
        y = b + w[0:1, :] * views[0]
        for j in range(1, width):
            y = y + w[j:j + 1, :] * views[j]
        return y

    def over_tiles(step, carry):
        carry = step(0, carry)
        return lax.fori_loop(1, n_tiles, step, carry)

    def fwd_call(x, w, b):
        def body(*refs):
            x_ref, w_ref, b_ref = refs[:3]
            o_ref = refs[-1]
            w_, b_ = w_ref[...], b_ref[...]

            def step(i, carry):
                y = _silu(pre_activation(shifted(window(x_ref, i)), w_, b_))
                if has_up:
                    y = y * refs[3][rows(i), :]
                o_ref[rows(i), :] = y
                return carry

            over_tiles(step, 0)

        return pl.pallas_call(
            body, name=name + "_fwd", grid=grid,
            in_specs=[x_spec, w_spec, b_spec] + up_specs, out_specs=o_spec,
            out_shape=jax.ShapeDtypeStruct((t, ch), F32),
            compiler_params=_cparams(),
        )(*([x, w, b] + ([x] if has_up else [])))

    def bwd_call(x, w, b, do):
        n_in = 4 + (1 if has_up else 0)

        def body(*refs):
            x_ref, w_ref, b_ref = refs[:3]
            do_ref = refs[n_in - 1]
            dx_ref, dw_ref, db_ref = refs[n_in:n_in + 3]
            gpad_ref = refs[n_in + 3]
            dup_ref = refs[n_in + 4] if has_up else None
            w_, b_ = w_ref[...], b_ref[...]
            first_sequence = pl.program_id(1) == 0
            half = pl.program_id(2) if has_up else None

            def fold(a):
                acc = a[0:8, :]
                for k in range(1, rt // 8):
                    acc = acc + a[8 * k:8 * k + 8, :]
                return acc

            def grad_pre(i, sums):
                views = shifted(window(x_ref, i))
                y = pre_activation(views, w_, b_)
                s = jax.nn.sigmoid(y)
                act = y * s
                do_ = do_ref[rows(i), :]
                if has_up:
                    dup_ref[rows(i), :] = do_ * act
                    do_ = do_ * refs[3][rows(i), :]
                dy = do_ * (s + act * (1.0 - s))
                gpad_ref[rows(i), :] = dy
                new = [sums[j] + fold(dy * views[j]) for j in range(width)]
                return tuple(new) + (sums[width] + fold(dy),)

            def grad_x(i, carry):
                if isinstance(i, int):
                    gwin = gpad_ref[0:rt + CONV_PAD, :]
                else:
                    gwin = gpad_ref[pl.ds(pl.multiple_of(i * rt, rt), rt + CONV_PAD), :]
                dx = w_[0:1, :] * gwin[width - 1:width - 1 + rt, :]
                for j in range(1, width):
                    dx = dx + w_[j:j + 1, :] * gwin[width - 1 - j:width - 1 - j + rt, :]
                dx_ref[rows(i), :] = dx
                return carry

            def gradients():
                zero8 = jnp.zeros((8, cb), F32)
                sums = over_tiles(grad_pre, (zero8,) * (width + 1))
                gpad_ref[seq:seq + CONV_PAD, :] = jnp.zeros((CONV_PAD, cb), F32)
                over_tiles(grad_x, 0)

                @pl.when(first_sequence)
                def _():
                    dw_ref[...] = jnp.zeros_like(dw_ref)
                    db_ref[...] = jnp.zeros_like(db_ref)

                dw_ref[...] += jnp.concatenate([jnp.sum(sums[j], axis=0, keepdims=True) for j in range(width)],
                                               axis=0)
                db_ref[...] += jnp.sum(sums[width], axis=0, keepdims=True)

            if has_up:
                pl.when(half == 0)(gradients)

                @pl.when(half == 1)
                def _():
                    dx_ref[...] = dup_ref[...]
            else:
                gradients()

        block = lambda shape, f: pl.BlockSpec(shape, (lambda c, b_, h: f(c, b_, h)) if has_up else
                                              (lambda c, b_: f(c, b_, 0)))
        xs = block((seq, cb), lambda c, b_, h: (b_, x_col0 // cb + c))
        ups = [block((seq, cb), lambda c, b_, h: (b_, up_col0 // cb + c))] if has_up else []
        ws = block((width, cb), lambda c, b_, h: (0, c))
        bs = block((1, cb), lambda c, b_, h: (0, c))
        dos = block((seq, cb), lambda c, b_, h: (b_, c))
        dxs = block((seq, cb), lambda c, b_, h: (b_, c + h * nb))
        tile = pltpu.VMEM((seq + CONV_PAD, cb), F32)
        return pl.pallas_call(
            body, name=name + "_bwd", grid=grid + ((2,) if has_up else ()),
            in_specs=[xs, ws, bs] + ups + [dos], out_specs=[dxs, ws, bs],
            out_shape=[jax.ShapeDtypeStruct((t, 2 * ch if has_up else ch), F32),
                       jax.ShapeDtypeStruct((width, ch), F32), jax.ShapeDtypeStruct((1, ch), F32)],
            scratch_shapes=[tile] + ([pltpu.VMEM((seq, cb), F32)] if has_up else []),
            compiler_params=_cparams(),
        )(*([x, w, b] + ([x] if has_up else []) + [do]))

    @jax.custom_vjp
    def conv(x, w, b):
        return fwd_call(x, w, b)

    def conv_fwd(x, w, b):
        return fwd_call(x, w, b), (x, w, b)

    def conv_bwd(res, do):
        x, w, b = res
        dx, dw, db = bwd_call(x, w, b, do)
        if not has_up:
            dx = jnp.pad(dx, ((0, 0), (x_col0, x.shape[1] - x_col0 - ch)))
        return dx, dw, db

    conv.defvjp(conv_fwd, conv_bwd)

    def apply(x, w, b=None):
        if b is None:
            b = jnp.zeros((ch,), F32)
        return conv(x, w, b.reshape(1, ch))

    return apply


def loss_head(x, w, target):
    t = x.shape[0]
    tm = _tile(t, 512)

    def fn(xb, wb, tb):
        err = _rms(xb, wb) - tb
        return 0.5 * jnp.sum(err * err) * (1.0 / D_MODEL)

    def body(x_ref, w_ref, t_ref, loss_ref, dx_ref, dw_ref):
        @pl.when(pl.program_id(0) == 0)
        def _():
            loss_ref[...] = jnp.zeros_like(loss_ref)
            dw_ref[...] = jnp.zeros_like(dw_ref)

        tb = t_ref[...]
        val, vjp = jax.vjp(lambda a, b: fn(a, b, tb), x_ref[...], w_ref[...])
        dx, dw = vjp(jnp.ones((), F32))
        dx_ref[...] = dx
        dw_ref[...] += dw
        loss_ref[...] += jnp.full(loss_ref.shape, val, F32)

    row = pl.BlockSpec((tm, D_MODEL), lambda i: (i, 0))
    vec = pl.BlockSpec((1, D_MODEL), lambda i: (0, 0))
    loss, dx, dw = pl.pallas_call(
        body, name="loss_head", grid=(t // tm,),
        in_specs=[row, vec, row],
        out_specs=[pl.BlockSpec((8, LANE), lambda i: (0, 0)), row, vec],
        out_shape=[jax.ShapeDtypeStruct((8, LANE), F32), jax.ShapeDtypeStruct((t, D_MODEL), F32),
                   jax.ShapeDtypeStruct((1, D_MODEL), F32)],
        compiler_params=_cparams(),
    )(x, w.reshape(1, D_MODEL), target)
    return loss[0, 0], dx, dw.reshape(D_MODEL)


PACK_W = 1024
ADAM_BLOCK_BYTES = 512 * 1024


def _rows_tile(r, c):
    if r * c * 4 <= ADAM_BLOCK_BYTES or r % 8:
        return r
    best = 8
    for t in range(8, r + 1, 8):
        if r % t == 0 and t * c * 4 <= ADAM_BLOCK_BYTES:
            best = t
    return best


def reduce_adamw(slots, w, m, v, name):
    r, wd = w.shape
    tr = _rows_tile(r, wd)
    c1 = 1.0 - ADAM_B1 ** ADAM_STEP
    c2 = 1.0 - ADAM_B2 ** ADAM_STEP

    def body(s_ref, w_ref, m_ref, v_ref, g_ref, d_ref, nm_ref, nv_ref):
        g = s_ref[0].astype(F32)
        for k in range(1, N_DEV):
            g = g + s_ref[k].astype(F32)
        nm = ADAM_B1 * m_ref[...] + (1.0 - ADAM_B1) * g
        nv = ADAM_B2 * v_ref[...] + (1.0 - ADAM_B2) * (g * g)
        m_hat = nm / c1
        v_hat = nv / c2
        d_ref[...] = -ADAM_LR * (m_hat / (jnp.sqrt(v_hat) + ADAM_EPS) + ADAM_WD * w_ref[...])
        g_ref[...] = g
        nm_ref[...] = nm
        nv_ref[...] = nv

    blk = pl.BlockSpec((tr, wd), lambda i: (i, 0))
    shp = jax.ShapeDtypeStruct((r, wd), F32)
    return pl.pallas_call(
        body, name=name, grid=(r // tr,),
        in_specs=[pl.BlockSpec((N_DEV, tr, wd), lambda i: (0, i, 0)), blk, blk, blk],
        out_specs=[blk, blk, blk, blk], out_shape=[shp, shp, shp, shp],
        compiler_params=_cparams(),
    )(slots, w, m, v)


def all_gather(block, name):
    def body(x_ref, out_ref, send_sems, recv_sems, local_sem):
        x, y, c = _position()
        me, sibling = (x, y, c), (x, y, 1 - c)
        chips = [(1 - x, y), (x, 1 - y), (1 - x, 1 - y)]

        def slot(px, py, pc):
            return out_ref.at[4 * px + 2 * py + pc]

        def copy(k, owner, to, src=None):
            return pltpu.make_async_remote_copy(
                src_ref=slot(*owner) if src is None else src, dst_ref=slot(*owner),
                send_sem=send_sems.at[k], recv_sem=recv_sems.at[k],
                device_id=to, device_id_type=pl.DeviceIdType.MESH)

        mine = pltpu.make_async_copy(x_ref, slot(*me), local_sem)
        mine.start()
        first = [copy(0, me, sibling, src=x_ref)]
        first += [copy(1 + j, me, (*chip, c), src=x_ref) for j, chip in enumerate(chips)]
        for cp in first:
            cp.start()
        passed = [copy(4 + j, (*chip, c), sibling) for j, chip in enumerate(chips)]
        for j, chip in enumerate(chips):
            copy(1 + j, (*chip, c), me).wait_recv()
            passed[j].start()
        copy(0, sibling, me).wait_recv()
        for j, chip in enumerate(chips):
            copy(4 + j, (*chip, 1 - c), me).wait_recv()
        for cp in first + passed:
            cp.wait_send()
        mine.wait()

    return pl.pallas_call(
        body, name=name,
        out_shape=jax.ShapeDtypeStruct((N_DEV,) + block.shape, block.dtype),
        in_specs=[pl.BlockSpec(memory_space=pl.ANY)],
        out_specs=pl.BlockSpec(memory_space=pl.ANY),
        scratch_shapes=[pltpu.SemaphoreType.DMA((7,)), pltpu.SemaphoreType.DMA((7,)), pltpu.SemaphoreType.DMA],
    )(block)


def exchange_slabs(slabs, name):
    def body(in_ref, out_ref, send_sems, recv_sems, local_sem):
        x, y, c = _position()
        my = 4 * x + 2 * y + c
        mine = pltpu.make_async_copy(in_ref.at[my], out_ref.at[my], local_sem)
        mine.start()
        copies = []
        for k in range(1, N_DEV):
            dx, dy, dc = (k >> 2) & 1, (k >> 1) & 1, k & 1
            px = x if dx == 0 else 1 - x
            py = y if dy == 0 else 1 - y
            pc = c if dc == 0 else 1 - c
            cp = pltpu.make_async_remote_copy(
                src_ref=in_ref.at[4 * px + 2 * py + pc], dst_ref=out_ref.at[my],
                send_sem=send_sems.at[k - 1], recv_sem=recv_sems.at[k - 1],
                device_id=(px, py, pc), device_id_type=pl.DeviceIdType.MESH)
            cp.start()
            copies.append(cp)
        for cp in copies:
            cp.wait()
        mine.wait()

    return pl.pallas_call(
        body, name=name,
        out_shape=jax.ShapeDtypeStruct(slabs.shape, slabs.dtype),
        in_specs=[pl.BlockSpec(memory_space=pl.ANY)],
        out_specs=pl.BlockSpec(memory_space=pl.ANY),
        scratch_shapes=[pltpu.SemaphoreType.DMA((7,)), pltpu.SemaphoreType.DMA((7,)), pltpu.SemaphoreType.DMA],
    )(slabs)


def _pack(arrays, dtype, row_multiple):
    flat = jnp.concatenate([a.astype(dtype).reshape(-1) for a in arrays])
    n = flat.shape[0]
    per = PACK_W * row_multiple
    total = -(-n // per) * per
    flat = jnp.pad(flat, (0, total - n))
    return flat.reshape(total // PACK_W, PACK_W)


def _unpack(flat2d, shapes, lead=()):
    flat = flat2d.reshape(lead + (-1,))
    out, off = [], 0
    for shp in shapes:
        n = math.prod(shp)
        out.append(flat[..., off:off + n].reshape(lead + tuple(shp)))
        off += n
    return out


def _full_from_gathered(g, axis):
    g = jnp.moveaxis(g, 0, axis)
    shp = list(g.shape)
    shp[axis:axis + 2] = [shp[axis] * shp[axis + 1]]
    return g.reshape(shp)


def _shards_of_full(full, axis):
    shp = list(full.shape)
    shp[axis:axis + 1] = [N_DEV, shp[axis] // N_DEV]
    return jnp.moveaxis(full.reshape(shp), axis, 0)


def layer_units(i):
    mixer = [('m_in_w', 'm_out_w'), ('h_in_w', 'h_out_w'), ('g_in_w', 'g_out_w')][i % 3]
    return [(mixer[0], i // 3), (mixer[1], i // 3), ('xa_q', i), ('xa_kv', i), ('xa_o', i), ('f_up', i), ('f_down', i)]


PADDED_COLS = {'m_in_w': M_IN_PAD, 'g_in_w': G_IN_PAD}


def whole_weight(name, gathered):
    w = _full_from_gathered(lax.stop_gradient(gathered), SHARD_AXIS[name] - 1)
    return _pad_cols(w, 1, PADDED_COLS[name]) if name in PADDED_COLS else w


def _trunk(p, weights, blocks, standins, x, mem, bl, seq):
    t = bl * seq
    ia = ib = ic = 0
    weights = dict(weights)
    state = {}

    def lin(name, a, wname, idx, residual=None):
        unit = (wname, idx)
        pos = state['units'].index(unit)
        later = state['next'][pos] if state['next'] else None
        nxt = (blocks[later],) if later in blocks else ()
        n_real = N_DEV * standins[unit].shape[2]
        res = () if residual is None else (residual,)
        y, got = make_linear(name, SHARD_AXIS[wname] - 1, n_real, bool(res))(
            a, weights[unit], standins[unit], nxt, res)
        if nxt:
            weights[later] = whole_weight(later[0], got[0])
        return y

    by_seq = lambda a: a.reshape(bl, seq, a.shape[-1])

    for i in range(DEPTH):
        state['units'] = layer_units(i)
        state['next'] = layer_units(i + 1) if i + 1 < DEPTH else None
        hn, x = rmsnorm_op(f"ln_mix{i}", t, F32, residual=True)(x, p['ln_mix'][i:i + 1])
        kind = i % 3
        if kind == 0:
            proj = lin(f"m_in{i}", hn, 'm_in_w', ia)
            xbc = make_conv(f"m_conv{i}", bl, seq, 4, M_CONV_DIM, M_D_INNER)(
                proj, p['m_conv_w'][ia], p['m_conv_b'][ia])
            dt = proj[:, M_D_INNER + M_CONV_DIM:M_IN].reshape(bl, seq, M_GROUPS, 4).transpose(2, 0, 1, 3)
            grp = lambda a, n=4: a.reshape(M_GROUPS, 1, n)
            proj3, xbc3 = by_seq(proj), by_seq(xbc)
            y = ssd_op(f"ssd{i}", bl, seq)(
                proj3, xbc3, xbc3, xbc3, dt, grp(p['m_dt_bias'][ia]), grp(p['m_a_log'][ia]), grp(p['m_d'][ia]),
                grp(p['m_norm_w'][ia], 256))[0]
            x = lin(f"m_out{i}", y.reshape(t, M_D_INNER), 'm_out_w', ia, residual=x)
            ia += 1
        elif kind == 1:
            proj3 = by_seq(lin(f"h_in{i}", hn, 'h_in_w', ib))
            y = gla_op(f"gla{i}", i, bl, seq)(
                proj3, proj3, proj3, proj3, p['h_lower_bounds'], p['h_norm_w'][ib:ib + 1])[0]
            x = lin(f"h_out{i}", y.reshape(t, D_MODEL), 'h_out_w', ib, residual=x)
            ib += 1
        else:
            proj = lin(f"g_in{i}", hn, 'g_in_w', ic)
            qkv = make_conv(f"g_conv{i}", bl, seq, 4, G_CONV_DIM, 0)(proj, p['g_conv_w'][ic])
            c0 = G_CONV_DIM + G_VAL_DIM
            heads = lambda a: a.reshape(bl, seq, G_QK_HEADS, 2).transpose(2, 0, 1, 3)
            braw = heads(proj[:, c0:c0 + G_V_HEADS])
            araw = heads(proj[:, c0 + G_V_HEADS:c0 + 2 * G_V_HEADS])
            grp = lambda a: a.reshape(G_QK_HEADS, 1, 2)
            qkv3 = by_seq(qkv)
            y = gdn_op(f"gdn{i}", bl, seq)(
                qkv3, qkv3, qkv3, by_seq(proj), braw, araw, grp(p['g_a_log'][ic]), grp(p['g_dt_bias'][ic]),
                p['g_norm_w'][ic:ic + 1])[0]
            x = lin(f"g_out{i}", y.reshape(t, G_VAL_DIM), 'g_out_w', ic, residual=x)
            ic += 1
        hq, x = rmsnorm_op(f"ln_xattn{i}", t, F32, residual=True)(x, p['ln_xattn'][i:i + 1])
        mn = rmsnorm_op(f"ln_mem{i}", bl * N_MEM, F32)(mem, p['ln_mem'][i:i + 1])[0]
        qx = lin(f"xa_q{i}", hq, 'xa_q', i)
        kv = lin(f"xa_kv{i}", mn, 'xa_kv', i)
        ao = xattn_op(f"xattn{i}", bl, seq)(qx, kv, kv)[0]
        x = lin(f"xa_o{i}", ao, 'xa_o', i, residual=x)
        hf, x = rmsnorm_op(f"ln_ffn{i}", t, F32, residual=True)(x, p['ln_ffn'][i:i + 1])
        up = lin(f"f_up{i}", hf, 'f_up', i)
        act = make_conv(f"f_conv{i}", bl, seq, 3, D_FF, 0, up_col0=D_FF)(up, p['f_conv_w'][i], p['f_conv_b'][i])
        x = lin(f"f_down{i}", act, 'f_down', i, residual=x)
    return x


def _pad_cols(w, axis, to):
    pad = [(0, 0)] * w.ndim
    pad[axis] = (0, to - w.shape[axis])
    return jnp.pad(w, pad)


def kernel(x, mem, ln_mix, ln_xattn, ln_mem, ln_ffn, final_norm, m_in_w, m_conv_w, m_conv_b, m_dt_bias, m_a_log, m_d, m_norm_w, m_out_w, h_in_w, h_lower_bounds, h_norm_w, h_out_w, g_in_w, g_conv_w, g_a_log, g_dt_bias, g_norm_w, g_out_w, xa_q, xa_kv, xa_o, f_up, f_conv_w, f_conv_b, f_down, loss_target, m_ln_mix, m_ln_xattn, m_ln_mem, m_ln_ffn, m_final_norm, m_m_in_w, m_m_conv_w, m_m_conv_b, m_m_dt_bias, m_m_a_log, m_m_d, m_m_norm_w, m_m_out_w, m_h_in_w, m_h_lower_bounds, m_h_norm_w, m_h_out_w, m_g_in_w, m_g_conv_w, m_g_a_log, m_g_dt_bias, m_g_norm_w, m_g_out_w, m_xa_q, m_xa_kv, m_xa_o, m_f_up, m_f_conv_w, m_f_conv_b, m_f_down, v_ln_mix, v_ln_xattn, v_ln_mem, v_ln_ffn, v_final_norm, v_m_in_w, v_m_conv_w, v_m_conv_b, v_m_dt_bias, v_m_a_log, v_m_d, v_m_norm_w, v_m_out_w, v_h_in_w, v_h_lower_bounds, v_h_norm_w, v_h_out_w, v_g_in_w, v_g_conv_w, v_g_a_log, v_g_dt_bias, v_g_norm_w, v_g_out_w, v_xa_q, v_xa_kv, v_xa_o, v_f_up, v_f_conv_w, v_f_conv_b, v_f_down):
    local = dict(ln_mix=ln_mix, ln_xattn=ln_xattn, ln_mem=ln_mem, ln_ffn=ln_ffn, final_norm=final_norm, m_in_w=m_in_w, m_conv_w=m_conv_w, m_conv_b=m_conv_b, m_dt_bias=m_dt_bias, m_a_log=m_a_log, m_d=m_d, m_norm_w=m_norm_w, m_out_w=m_out_w, h_in_w=h_in_w, h_lower_bounds=h_lower_bounds, h_norm_w=h_norm_w, h_out_w=h_out_w, g_in_w=g_in_w, g_conv_w=g_conv_w, g_a_log=g_a_log, g_dt_bias=g_dt_bias, g_norm_w=g_norm_w, g_out_w=g_out_w, xa_q=xa_q, xa_kv=xa_kv, xa_o=xa_o, f_up=f_up, f_conv_w=f_conv_w, f_conv_b=f_conv_b, f_down=f_down)
    mom_m = dict(ln_mix=m_ln_mix, ln_xattn=m_ln_xattn, ln_mem=m_ln_mem, ln_ffn=m_ln_ffn, final_norm=m_final_norm, m_in_w=m_m_in_w, m_conv_w=m_m_conv_w, m_conv_b=m_m_conv_b, m_dt_bias=m_m_dt_bias, m_a_log=m_m_a_log, m_d=m_m_d, m_norm_w=m_m_norm_w, m_out_w=m_m_out_w, h_in_w=m_h_in_w, h_lower_bounds=m_h_lower_bounds, h_norm_w=m_h_norm_w, h_out_w=m_h_out_w, g_in_w=m_g_in_w, g_conv_w=m_g_conv_w, g_a_log=m_g_a_log, g_dt_bias=m_g_dt_bias, g_norm_w=m_g_norm_w, g_out_w=m_g_out_w, xa_q=m_xa_q, xa_kv=m_xa_kv, xa_o=m_xa_o, f_up=m_f_up, f_conv_w=m_f_conv_w, f_conv_b=m_f_conv_b, f_down=m_f_down)
    mom_v = dict(ln_mix=v_ln_mix, ln_xattn=v_ln_xattn, ln_mem=v_ln_mem, ln_ffn=v_ln_ffn, final_norm=v_final_norm, m_in_w=v_m_in_w, m_conv_w=v_m_conv_w, m_conv_b=v_m_conv_b, m_dt_bias=v_m_dt_bias, m_a_log=v_m_a_log, m_d=v_m_d, m_norm_w=v_m_norm_w, m_out_w=v_m_out_w, h_in_w=v_h_in_w, h_lower_bounds=v_h_lower_bounds, h_norm_w=v_h_norm_w, h_out_w=v_h_out_w, g_in_w=v_g_in_w, g_conv_w=v_g_conv_w, g_a_log=v_g_a_log, g_dt_bias=v_g_dt_bias, g_norm_w=v_g_norm_w, g_out_w=v_g_out_w, xa_q=v_xa_q, xa_kv=v_xa_kv, xa_o=v_xa_o, f_up=v_f_up, f_conv_w=v_f_conv_w, f_conv_b=v_f_conv_b, f_down=v_f_down)

    bl, seq, _ = x.shape
    t = bl * seq

    p = {n: local[n] for n in WEIGHTS if n not in SHARD_AXIS}
    for n in SMALL_SHARDED:
        p[n] = _full_from_gathered(all_gather(local[n], f"gather_{n}"), SHARD_AXIS[n])
    units = [(n, l) for n in MATMUL_WEIGHTS for l in range(local[n].shape[0])]
    block = lambda u: local[u[0]][u[1]].astype(BF16)
    weights = {u: whole_weight(u[0], all_gather(block(u), f"gather_{u[0]}{u[1]}")) for u in layer_units(0)}
    blocks = {u: block(u) for u in units if u not in weights}
    standins = {u: jnp.zeros((N_DEV,) + local[u[0]].shape[1:], BF16) for u in units}
    small = {n: p[n] for n in WEIGHTS if n not in MATMUL_WEIGHTS and n != 'final_norm'}

    def run(small_w, standins_, xin):
        return _trunk(small_w, weights, blocks, standins_, xin, mem.reshape(bl * N_MEM, D_MODEL), bl, seq)

    x_out, vjp = jax.vjp(run, small, standins, x.reshape(t, D_MODEL))
    loss_part, dx_out, d_final = loss_head(x_out, final_norm, loss_target.reshape(t, D_MODEL))
    grads, received, dx = vjp(dx_out)
    grads = dict(grads)
    grads['final_norm'] = d_final
    loss = lax.psum(loss_part, ("x", "y", "c"))

    outs = {}

    def update(name, n, slots, shape, sel=lambda a: a):
        two_d = lambda a: sel(a).reshape(slots.shape[1:])
        got = reduce_adamw(slots, two_d(local[n]), two_d(mom_m[n]), two_d(mom_v[n]), name)
        return [g.reshape(shape) for g in got]

    for n in SMALL_SHARDED:
        slots = exchange_slabs(_shards_of_full(grads[n], SHARD_AXIS[n]), f"exchange_{n}")
        slots = slots.reshape(N_DEV, -1, slots.shape[-1])
        for kind, a in zip(KINDS, update(f"adamw_{n}", n, slots, local[n].shape)):
            outs[kind, n] = a
    for n in MATMUL_WEIGHTS:
        per_layer = [update(f"adamw_{n}{l}", n, received[n, l], local[n].shape[1:], lambda a, l=l: a[l])
                     for l in range(local[n].shape[0])]
        for k, kind in enumerate(KINDS):
            outs[kind, n] = jnp.concatenate([got[k][None] for got in per_layer])
    replicated = [n for n in WEIGHTS if n not in SHARD_AXIS]
    pk = lambda d: _pack([d[n] for n in replicated], F32, 8)
    got = reduce_adamw(all_gather(pk(grads), "gather_replicated_grads"), pk(local), pk(mom_m), pk(mom_v),
                       "adamw_replicated")
    shapes = [local[n].shape for n in replicated]
    for kind, buf in zip(KINDS, got):
        for n, a in zip(replicated, _unpack(buf, shapes)):
            outs[kind, n] = a
    result = [loss, dx.reshape(bl, seq, D_MODEL)]
    for kind in KINDS:
        result += [outs[kind, n] for n in WEIGHTS]
    return tuple(result)
```

```python
import functools
import math

import jax
import jax.numpy as jnp
from jax import lax
from jax.experimental import pallas as pl
from jax.experimental.pallas import tpu as pltpu

F32 = jnp.float32
BF16 = jnp.bfloat16
NN = (((1,), (0,)), ((), ()))
NT = (((1,), (1,)), ((), ()))
TN = (((0,), (0,)), ((), ()))

D_MODEL = 1024
DEPTH = 4
EPS = 1e-6
N_MEM = 256
M_D_INNER = 2048
M_HEADS = 32
M_GROUPS = 8
M_STATE = 128
M_CONV_DIM = 4096
M_IN = 6176
M_IN_PAD = 6272
SSD_CHUNK = 256
H_HEADS = 8
HGRN_CHUNK = 32
HGRN_ROWS = 128
G_QK_HEADS = 8
G_V_HEADS = 16
G_KEY_DIM = 1024
G_VAL_DIM = 2048
G_CONV_DIM = 4096
G_IN = 6176
G_IN_PAD = 6272
GDN_CHUNK = 64
X_HEADS = 4
X_HEAD_DIM = 256
D_FF = 2816
ADAM_LR = 0.001
ADAM_B1 = 0.9
ADAM_B2 = 0.999
ADAM_EPS = 1e-08
ADAM_WD = 0.01
ADAM_STEP = 10

N_DEV = 8
LANE = 128
KINDS = ('grad', 'delta', 'new_m', 'new_v')
VMEM_LIMIT = 56 * 1024 * 1024

WEIGHTS = ['ln_mix', 'ln_xattn', 'ln_mem', 'ln_ffn', 'final_norm', 'm_in_w', 'm_conv_w', 'm_conv_b', 'm_dt_bias',
           'm_a_log', 'm_d', 'm_norm_w', 'm_out_w', 'h_in_w', 'h_lower_bounds', 'h_norm_w', 'h_out_w', 'g_in_w',
           'g_conv_w', 'g_a_log', 'g_dt_bias', 'g_norm_w', 'g_out_w', 'xa_q', 'xa_kv', 'xa_o', 'f_up', 'f_conv_w',
           'f_conv_b', 'f_down']
SHARD_AXIS = {'m_in_w': 2, 'm_conv_w': 2, 'm_conv_b': 1, 'm_norm_w': 1, 'm_out_w': 1, 'h_in_w': 2, 'h_out_w': 1,
              'g_in_w': 2, 'g_conv_w': 2, 'g_out_w': 1, 'xa_q': 1, 'xa_kv': 2, 'xa_o': 1, 'f_up': 2, 'f_conv_w': 2,
              'f_down': 1}
MATMUL_WEIGHTS = ['m_in_w', 'm_out_w', 'h_in_w', 'h_out_w', 'g_in_w', 'g_out_w', 'xa_q', 'xa_kv', 'xa_o', 'f_up',
                  'f_down']
SMALL_SHARDED = ['m_conv_w', 'm_conv_b', 'm_norm_w', 'g_conv_w', 'f_conv_w']


def _cparams():
    return pltpu.CompilerParams(vmem_limit_bytes=VMEM_LIMIT)


def bdot(a, b, dims=NN):
    return lax.dot_general(a.astype(BF16), b.astype(BF16), dims, preferred_element_type=F32)


def _split(a):
    hi = a.astype(BF16)
    return hi, (a - hi.astype(F32)).astype(BF16)


def _h3(a, b, dims):
    ah, al = _split(a)
    bh, bl = _split(b)
    d = functools.partial(lax.dot_general, dimension_numbers=dims, preferred_element_type=F32)
    return d(ah, bh) + (d(ah, bl) + d(al, bh))


BNN = (((2,), (1,)), ((0,), (0,)))
BNT = (((2,), (2,)), ((0,), (0,)))
BTN = (((1,), (1,)), ((0,), (0,)))


@jax.custom_vjp
def h3dot_b(a, b):
    return _h3(a, b, BNN)


h3dot_b.defvjp(lambda a, b: (_h3(a, b, BNN), (a, b)),
               lambda res, ct: (_h3(ct, res[1], BNT), _h3(res[0], ct, BTN)))

T_ROWS = (((0,), (1,)), ((), ()))


def _tri_times(tri, x, dims, tri_first):
    t = tri.astype(BF16)
    x0 = x.astype(BF16)
    r1 = x - x0.astype(F32)
    x1 = r1.astype(BF16)
    x2 = (r1 - x1.astype(F32)).astype(BF16)
    if tri_first:
        d = lambda xx: lax.dot_general(t, xx, dims, preferred_element_type=F32)
    else:
        d = lambda xx: lax.dot_general(xx, t, dims, preferred_element_type=F32)
    return d(x0) + (d(x1) + d(x2))


@jax.custom_vjp
def cumdot(tri, x):
    return _tri_times(tri, x, NN, True)


cumdot.defvjp(lambda tri, x: (_tri_times(tri, x, NN, True), tri),
              lambda tri, ct: (jnp.zeros_like(tri), _tri_times(tri, ct, TN, True)))


@jax.custom_vjp
def cumdot_t(tri, x):
    return _tri_times(tri, x, T_ROWS, False)


cumdot_t.defvjp(lambda tri, x: (_tri_times(tri, x, T_ROWS, False), tri),
                lambda tri, ct: (jnp.zeros_like(tri), _tri_times(tri, ct, T_ROWS, True)))


def _tile(dim, cap):
    if dim <= cap:
        return dim
    best = None
    for t in range(LANE, cap + 1, LANE):
        if dim % t == 0:
            best = t
    assert best is not None, dim
    return best


def _position():
    return lax.axis_index("x"), lax.axis_index("y"), lax.axis_index("c")


def _direct_copies(kind, src_ref, dst_ref, send_sems, recv_sems, local_sem):
    x, y, c = _position()
    me = 4 * x + 2 * y + c
    local_src = src_ref if kind == 'gather' else src_ref.at[me]
    copies = [pltpu.make_async_copy(local_src, dst_ref.at[me], local_sem)]
    for k in range(1, N_DEV):
        px = 1 - x if (k >> 2) & 1 else x
        py = 1 - y if (k >> 1) & 1 else y
        pc = 1 - c if k & 1 else c
        copies.append(pltpu.make_async_remote_copy(
            src_ref=src_ref if kind == 'gather' else src_ref.at[4 * px + 2 * py + pc], dst_ref=dst_ref.at[me],
            send_sem=send_sems.at[k - 1], recv_sem=recv_sems.at[k - 1],
            device_id=(px, py, pc), device_id_type=pl.DeviceIdType.MESH))
    return copies


COMM_SCRATCH = [pltpu.SemaphoreType.DMA((N_DEV - 1,)), pltpu.SemaphoreType.DMA((N_DEV - 1,)), pltpu.SemaphoreType.DMA]


class Pending:
    def __init__(self):
        self.jobs, self.received = [], {}

    def take_all(self):
        jobs, self.jobs = self.jobs, []
        return jobs


def matmul(a, b, *, ta=False, tb=False, out_dtype=F32, name="mm", carry=None, residual=None):
    if ta:
        k, m = a.shape
    else:
        m, k = a.shape
    if tb:
        n, k2 = b.shape
    else:
        k2, n = b.shape
    assert k == k2, (a.shape, b.shape, ta, tb)
    tm = _tile(m, 1024)
    tn = _tile(n, 1408)
    tk = _tile(k, 1408)
    grid = (m // tm, n // tn, k // tk)
    nk = grid[2]
    dims = (((0 if ta else 1,), (1 if tb else 0,)), ((), ()))

    def at_step(which):
        conds = [pl.program_id(ax) == (0 if which == 'first' else grid[ax] - 1) for ax in range(3)]
        return jnp.logical_and(jnp.logical_and(conds[0], conds[1]), conds[2])

    def body(*refs):
        r_ref = None
        if residual is not None:
            r_ref, refs = refs[2], refs[:2] + refs[3:]
        if carry is None:
            a_ref, b_ref, o_ref, acc_ref = refs
        else:
            a_ref, b_ref, src_ref, o_ref, dst_ref, acc_ref, send_sems, recv_sems, local_sem = refs
            copies = lambda: _direct_copies(carry[0], src_ref, dst_ref, send_sems, recv_sems, local_sem)

            @pl.when(at_step('first'))
            def _():
                for cp in copies():
                    cp.start()

        @pl.when(pl.program_id(2) == 0)
        def _():
            acc_ref[...] = jnp.zeros_like(acc_ref)

        acc_ref[...] += lax.dot_general(a_ref[...].astype(BF16), b_ref[...].astype(BF16), dims,
                                        preferred_element_type=F32)

        @pl.when(pl.program_id(2) == nk - 1)
        def _():
            out = acc_ref[...] if r_ref is None else acc_ref[...] + r_ref[...]
            o_ref[...] = out.astype(o_ref.dtype)

        if carry is not None:
            @pl.when(at_step('last'))
            def _():
                for cp in copies():
                    cp.wait()

    a_spec = pl.BlockSpec((tk, tm), lambda i, j, kk: (kk, i)) if ta else pl.BlockSpec((tm, tk), lambda i, j, kk: (i, kk))
    b_spec = pl.BlockSpec((tn, tk), lambda i, j, kk: (j, kk)) if tb else pl.BlockSpec((tk, tn), lambda i, j, kk: (kk, j))
    o_spec = pl.BlockSpec((tm, tn), lambda i, j, kk: (i, j))
    o_shape = jax.ShapeDtypeStruct((m, n), out_dtype)
    acc = pltpu.VMEM((tm, tn), F32)
    ins, in_specs = [a, b], [a_spec, b_spec]
    if residual is not None:
        ins.append(residual)
        in_specs.append(o_spec)
    if carry is None:
        return pl.pallas_call(
            body, name=name, grid=grid, in_specs=in_specs, out_specs=o_spec, out_shape=o_shape,
            scratch_shapes=[acc], compiler_params=_cparams(),
        )(*ins)
    kind, src = carry
    got = jax.ShapeDtypeStruct(((N_DEV,) + src.shape) if kind == 'gather' else src.shape, src.dtype)
    hbm = pl.BlockSpec(memory_space=pl.ANY)
    return pl.pallas_call(
        body, name=name, grid=grid, in_specs=in_specs + [hbm], out_specs=[o_spec, hbm],
        out_shape=[o_shape, got], scratch_shapes=[acc] + COMM_SCRATCH, compiler_params=_cparams(),
    )(*ins, src)


def make_linear(name, shard_axis, n_real, has_res=False, defer=None):
    def forward(a, w, nxt, res):
        r = res[0] if res else None
        if nxt:
            y, got = matmul(a, w, name=name + "_fwd", carry=('gather', nxt[0]), residual=r)
            return y, (got,)
        return matmul(a, w, name=name + "_fwd", residual=r), ()

    @jax.custom_vjp
    def linear(a, w, wg, nxt, res):
        return forward(a, w, nxt, res)

    def fwd(a, w, wg, nxt, res):
        return forward(a, w, nxt, res), (a, w, nxt)

    def bwd(saved, cts):
        a, w, nxt = saved
        dy = cts[0]
        dw = matmul(a, dy, ta=True, out_dtype=BF16, name=name + "_bwd_dw")
        slabs = _shards_of_full(dw[:, :n_real], shard_axis)
        if defer is None:
            da, slots = matmul(dy, w, tb=True, out_dtype=a.dtype, name=name + "_bwd_da", carry=('exchange', slabs))
        else:
            defer[0].jobs.append((defer[1], slabs))
            da, slots = matmul(dy, w, tb=True, out_dtype=a.dtype, name=name + "_bwd_da"), jnp.zeros_like(slabs)
        return da, jnp.zeros_like(w), slots, tuple(jnp.zeros_like(b) for b in nxt), ((dy,) if has_res else ())

    linear.defvjp(fwd, bwd)
    return linear


class In:
    def __init__(self, block, imap, kind='blk', inner=(), cols=None):
        self.block, self.imap, self.kind, self.inner, self.cols = block, imap, kind, inner, cols


class Out:
    def __init__(self, shape, dtype, block, imap):
        self.shape, self.dtype, self.block, self.imap = shape, dtype, block, imap


def make_op(name, fn, grid, ins, outs, state_shape=None, seq_axis=None, passthrough=(), pending=None):
    n_in, n_out = len(ins), len(outs)
    has_state = state_shape is not None
    nd = len(grid)
    diff_idx = [i for i, s in enumerate(ins) if s.kind != 'const']

    def in_spec(s, reverse):
        off = 0
        if s.cols is not None:
            assert s.cols[0] % s.block[-1] == 0
            off = s.cols[0] // s.block[-1]

        def imap(*ids):
            ids = rev(ids) if reverse else ids
            idx = tuple(s.imap(*ids))
            return idx[:-1] + (idx[-1] + off,) if off else idx

        return pl.BlockSpec(s.block, imap)

    def rel_spec(block, f, reverse):
        return pl.BlockSpec(block, (lambda *ids: f(*rev(ids))) if reverse else f)

    def rev(ids):
        if not has_state:
            return ids
        ids = list(ids)
        ids[seq_axis] = grid[seq_axis] - 1 - ids[seq_axis]
        return tuple(ids)

    save_shape = tuple(grid) + tuple(state_shape) if has_state else None
    save_block = (None,) * nd + tuple(state_shape) if has_state else None

    def save_imap(*ids):
        return tuple(ids) + (0,) * len(state_shape)

    def fwd_call(*xs):
        def body(*refs):
            in_refs = refs[:n_in]
            out_refs = refs[n_in:n_in + n_out]
            vals = [r[...] for r in in_refs]
            if has_state:
                save_ref, st_ref = refs[n_in + n_out], refs[n_in + n_out + 1]

                @pl.when(pl.program_id(seq_axis) == 0)
                def _():
                    st_ref[...] = jnp.zeros(state_shape, F32)

                st = st_ref[...]
                save_ref[...] = st
                res = fn(*vals, st)
                st_ref[...] = res[-1]
                res = res[:-1]
            else:
                res = fn(*vals)
            for o, v in zip(out_refs, res):
                o[...] = v.astype(o.dtype)

        out_shape = [jax.ShapeDtypeStruct(o.shape, o.dtype) for o in outs]
        out_specs = [pl.BlockSpec(o.block, o.imap) for o in outs]
        scratch = []
        if has_state:
            out_shape.append(jax.ShapeDtypeStruct(save_shape, F32))
            out_specs.append(pl.BlockSpec(save_block, save_imap))
            scratch.append(pltpu.VMEM(state_shape, F32))
        return pl.pallas_call(
            body, name=name + "_fwd", grid=grid,
            in_specs=[in_spec(s, False) for s in ins],
            out_specs=out_specs, out_shape=out_shape, scratch_shapes=scratch,
            compiler_params=_cparams(),
        )(*xs)

    def grad_shape(s, x):
        if s.cols is not None:
            return x.shape[:-1] + (s.cols[1],)
        return x.shape

    def bwd_call(xs, save, cts, pass_cts=()):
        n_diff = len(diff_idx)
        jobs = pending.take_all() if pending is not None else []
        n_args = n_in + (1 if has_state else 0) + n_out + len(passthrough)

        def at_step(which):
            conds = [pl.program_id(ax) == (0 if which == 'first' else grid[ax] - 1) for ax in range(nd)]
            return functools.reduce(jnp.logical_and, conds)

        def body(*refs):
            if jobs:
                src_refs = refs[n_args:n_args + len(jobs)]
                dst_refs = refs[n_args + len(jobs) + n_diff:n_args + 2 * len(jobs) + n_diff]
                sems = refs[len(refs) - 3 * len(jobs):]
                refs = refs[:n_args] + refs[n_args + len(jobs):n_args + len(jobs) + n_diff] + \
                    refs[n_args + 2 * len(jobs) + n_diff:len(refs) - 3 * len(jobs)]
                copies = lambda: [cp for k in range(len(jobs)) for cp in _direct_copies(
                    'exchange', src_refs[k], dst_refs[k], sems[3 * k], sems[3 * k + 1], sems[3 * k + 2])]

                @pl.when(at_step('first'))
                def _():
                    for cp in copies():
                        cp.start()

            in_refs = refs[:n_in]
            p = n_in
            if has_state:
                save_ref = refs[p]
                p += 1
            ct_refs = refs[p:p + n_out]
            p += n_out
            pass_refs = dict(zip(passthrough, refs[p:p + len(passthrough)]))
            p += len(passthrough)
            g_refs = refs[p:p + n_diff]
            p += n_diff
            vals = [r[...] for r in in_refs]

            def g(*dv):
                full = list(vals)
                for i, v in zip(diff_idx, dv):
                    full[i] = v
                if has_state:
                    return tuple(fn(*full, dv[-1]))
                return tuple(fn(*full))

            prim = [vals[i] for i in diff_idx]
            ct = tuple(r[...].astype(F32) for r in ct_refs)
            if has_state:
                dst_ref = refs[p]

                @pl.when(pl.program_id(seq_axis) == 0)
                def _():
                    dst_ref[...] = jnp.zeros(state_shape, F32)

                prim = prim + [save_ref[...]]
                ct = ct + (dst_ref[...],)
            _, vjp = jax.vjp(g, *prim)
            grads = vjp(ct)
            for k, i in enumerate(diff_idx):
                s = ins[i]
                if s.kind == 'blk':
                    g = grads[k] + pass_refs[i][...] if i in pass_refs else grads[k]
                    g_refs[k][...] = g.astype(g_refs[k].dtype)
                else:
                    first = None
                    for ax in s.inner:
                        c = pl.program_id(ax) == 0
                        first = c if first is None else jnp.logical_and(first, c)

                    @pl.when(first)
                    def _(k=k):
                        g_refs[k][...] = jnp.zeros_like(g_refs[k])

                    g_refs[k][...] += grads[k].astype(g_refs[k].dtype)
            if has_state:
                dst_ref[...] = grads[-1]

            if jobs:
                @pl.when(at_step('last'))
                def _():
                    for cp in copies():
                        cp.wait()

        in_specs = [in_spec(s, True) for s in ins]
        args = list(xs)
        if has_state:
            in_specs.append(rel_spec(save_block, save_imap, True))
            args.append(save)
        for o, c in zip(outs, cts):
            in_specs.append(rel_spec(o.block, o.imap, True))
            args.append(c)
        for i, c in zip(passthrough, pass_cts):
            assert ins[i].kind == 'blk' and ins[i].cols is None
            in_specs.append(rel_spec(ins[i].block, ins[i].imap, True))
            args.append(c)
        out_shape, out_specs = [], []
        for i in diff_idx:
            s = ins[i]
            out_shape.append(jax.ShapeDtypeStruct(grad_shape(s, xs[i]), xs[i].dtype))
            out_specs.append(rel_spec(s.block, s.imap, True))
        scratch = [pltpu.VMEM(state_shape, F32)] if has_state else []
        hbm = pl.BlockSpec(memory_space=pl.ANY)
        for _, slabs in jobs:
            in_specs.append(hbm)
            args.append(slabs)
            out_specs.append(hbm)
            out_shape.append(jax.ShapeDtypeStruct(slabs.shape, slabs.dtype))
            scratch += COMM_SCRATCH
        got = pl.pallas_call(
            body, name=name + "_bwd", grid=grid,
            in_specs=in_specs, out_specs=out_specs, out_shape=out_shape, scratch_shapes=scratch,
            compiler_params=_cparams(),
        )(*args)
        for (unit, _), slots in zip(jobs, got[n_diff:]):
            pending.received[unit] = slots
        return got[:n_diff]

    @jax.custom_vjp
    def op(*xs):
        return tuple(fwd_call(*xs)[:n_out]) + tuple(xs[i] for i in passthrough)

    def op_fwd(*xs):
        res = fwd_call(*xs)
        return tuple(res[:n_out]) + tuple(xs[i] for i in passthrough), (xs, res[n_out] if has_state else None)

    def op_bwd(resid, cts):
        xs, save = resid
        grads = bwd_call(xs, save, cts[:n_out], cts[n_out:])
        out = []
        k = 0
        for i, s in enumerate(ins):
            if s.kind == 'const':
                out.append(jnp.zeros_like(xs[i]))
                continue
            g = grads[k]
            k += 1
            if s.cols is not None:
                g = jnp.pad(g, ((0, 0),) * (g.ndim - 1) + ((s.cols[0], xs[i].shape[-1] - s.cols[0] - s.cols[1]),))
            out.append(g)
        return tuple(out)

    op.defvjp(op_fwd, op_bwd)
    return op


def _rms(x, w):
    return x * lax.rsqrt(jnp.mean(x * x, axis=-1, keepdims=True) + EPS) * w


def _silu(x):
    return x * jax.nn.sigmoid(x)


def rmsnorm_op(name, t, out_dtype, residual=False):
    tm = _tile(t, 512)
    return make_op(
        name, lambda x, w: (_rms(x, w),), (t // tm,),
        [In((tm, D_MODEL), lambda i: (i, 0)), In((1, D_MODEL), lambda i: (0, 0), 'acc', (0,))],
        [Out((t, D_MODEL), out_dtype, (tm, D_MODEL), lambda i: (i, 0))], passthrough=(0,) if residual else ())


def _tri(q):
    ii = lax.broadcasted_iota(jnp.int32, (q, q), 0)
    jj = lax.broadcasted_iota(jnp.int32, (q, q), 1)
    return ii >= jj, ii > jj


def _ssd_fn(z, x, bm, cm, dtr, dtb, alog, dsk, nw, state):
    q = x.shape[0]
    incl, _ = _tri(q)
    tril = incl.astype(F32)
    dt = jax.nn.softplus(dtr + dtb)
    da = dt * (-jnp.exp(alog))
    acum = cumdot(tril, da)
    acum_t = cumdot_t(tril, da)
    cb = bdot(cm, bm, NT)
    heads = range(4)
    wide = lambda a: jnp.concatenate([jnp.broadcast_to(a[:, r:r + 1], (a.shape[0], 64)) for r in heads], axis=1)
    last = acum[q - 1:q, :]
    xc = x * wide(dt)
    y = bdot(cm, state, NT) * wide(jnp.exp(acum)) + wide(dsk) * x
    ds = bdot(xc * wide(jnp.exp(last - acum)), bm, TN)
    e_last = jnp.exp(last)
    new_state = state * jnp.concatenate([jnp.broadcast_to(e_last[:, r:r + 1], (64, 1)) for r in heads], axis=0) + ds
    diag = []
    for r in heads:
        decay = jnp.exp(jnp.where(incl, acum[:, r:r + 1] - acum_t[r:r + 1, :], -jnp.inf))
        diag.append(bdot(cb * decay, xc[:, 64 * r:64 * r + 64]))
    y = y + jnp.concatenate(diag, axis=1)
    yz = y * _silu(z)
    return _rms(yz, nw), new_state


def _per_sequence(fn, n_seq_args, bl):
    def f(*args):
        *ins, state = args
        res = [fn(*[a[b] for a in ins[:n_seq_args]], *ins[n_seq_args:], state[b]) for b in range(bl)]
        return tuple(jnp.concatenate([r[k][None] for r in res]) for k in range(len(res[0])))

    return f


def ssd_op(name, bl, seq, pending=None):
    q = SSD_CHUNK
    nc = seq // q
    blk = lambda w, c0, cw: In((bl, q, w), lambda g, n: (0, n, g), cols=(c0, cw))
    small = lambda g, n: (g, 0, 0)
    ins = [
        blk(256, 0, M_D_INNER),
        blk(256, 0, M_D_INNER),
        blk(128, M_D_INNER, 1024),
        blk(128, M_D_INNER + 1024, 1024),
        In((None, bl, q, 4), lambda g, n: (g, 0, n, 0)),
        In((None, 1, 4), small, 'acc', (1,)),
        In((None, 1, 4), small, 'acc', (1,)),
        In((None, 1, 4), small, 'acc', (1,)),
        In((None, 1, 256), small, 'acc', (1,)),
    ]
    outs = [Out((bl, seq, M_D_INNER), F32, (bl, q, 256), lambda g, n: (0, n, g))]
    return make_op(name, _per_sequence(_ssd_fn, 5, bl), (M_GROUPS, nc), ins, outs,
                   state_shape=(bl, 256, 128), seq_axis=1, pending=pending)


def _gla_fn(layer, qr, fr, ir, gr, lbp, nw, state_t):
    rows = qr.shape[0]
    c = HGRN_CHUNK
    e = jnp.exp(lbp - jnp.max(lbp, axis=0, keepdims=True))
    sm = e / jnp.sum(e, axis=0, keepdims=True)
    lb = jnp.sum(sm[1:layer + 1, :], axis=0, keepdims=True) if layer > 0 else jnp.zeros((1, lbp.shape[1]), F32)
    qq = _silu(qr) * (128 ** -0.5)
    forget = lb + (1.0 - lb) * jax.nn.sigmoid(fr)
    kk = 1.0 - forget
    logf = jnp.log(forget)
    incl, _ = _tri(c)
    tril = incl.astype(F32)
    os_ = []
    for j in range(rows // c):
        sl = slice(c * j, c * j + c)
        gc = cumdot(tril, logf[sl])
        glast = gc[c - 1:c, :]
        q_dec = qq[sl] * jnp.exp(gc)
        k_inv = kk[sl] * jnp.exp(-gc)
        k_end = kk[sl] * jnp.exp(glast - gc)
        att = jnp.where(incl, bdot(q_dec, k_inv, NT), 0.0)
        os_.append(bdot(att, ir[sl]) + bdot(q_dec, state_t, NT))
        state_t = state_t * jnp.exp(glast) + bdot(ir[sl], k_end, TN)
    o = jnp.concatenate(os_, axis=0)
    return _rms(o, nw) * _silu(gr), state_t


def gla_op(name, layer, bl, seq, pending=None):
    r = HGRN_ROWS
    ns = seq // r
    blk = lambda k: In((bl, r, 128), lambda h, n: (0, n, h), cols=(1024 * k, 1024))
    ins = [blk(0), blk(1), blk(2), blk(3),
           In((DEPTH, 128), lambda h, n: (0, h), 'acc', (1,)),
           In((1, 128), lambda h, n: (0, 0), 'acc', (0, 1))]
    outs = [Out((bl, seq, D_MODEL), F32, (bl, r, 128), lambda h, n: (0, n, h))]
    return make_op(name, _per_sequence(functools.partial(_gla_fn, layer), 4, bl), (H_HEADS, ns), ins, outs,
                   state_shape=(bl, 128, 128), seq_axis=1, pending=pending)


def _unit_lower_inverse(m):
    q = m.shape[1]
    ii = lax.broadcasted_iota(jnp.int32, (q, q), 0)
    jj = lax.broadcasted_iota(jnp.int32, (q, q), 1)
    eye = (ii == jj).astype(F32)[None]
    p = -m
    inv = eye + p
    for _ in range(int(math.log2(q)) - 1):
        p = h3dot_b(p, p)
        inv = inv + h3dot_b(inv, p)
    return inv


def _gdn_fn(qc, kc, vc, zc, br, ar, alog, dtb, nw, state):
    bl, q = qc.shape[0], qc.shape[1]
    incl, strict = _tri(q)
    tril = incl.astype(F32)
    g = jnp.concatenate([-jnp.exp(alog) * jax.nn.softplus(ar[b] + dtb) for b in range(bl)], axis=1)
    gc = cumdot(tril, g)
    gc_t = cumdot_t(tril, g)
    heads, ms, rhs = [], [], []
    for b in range(bl):
        qn = qc[b] * lax.rsqrt(jnp.sum(qc[b] * qc[b], axis=-1, keepdims=True) + EPS) * (128 ** -0.5)
        kn = kc[b] * lax.rsqrt(jnp.sum(kc[b] * kc[b], axis=-1, keepdims=True) + EPS)
        beta = jax.nn.sigmoid(br[b])
        qk = bdot(qn, kn, NT)
        for j in range(2):
            i = 2 * b + j
            col = gc[:, i:i + 1]
            decay = jnp.exp(jnp.where(incl, col - gc_t[i:i + 1, :], -jnp.inf))
            bj = beta[:, j:j + 1]
            kb = kn * bj
            ms.append(jnp.where(strict, bdot(kb, kn, NT) * decay, 0.0))
            rhs.append(jnp.concatenate([vc[b][:, 128 * j:128 * j + 128] * bj, kb * jnp.exp(col)], axis=1))
            heads.append((qn, kn, qk * decay, col, gc[q - 1:q, i:i + 1]))
    sol = h3dot_b(_unit_lower_inverse(jnp.concatenate([m[None] for m in ms])),
                  jnp.concatenate([r[None] for r in rhs]))
    outs, states = [], []
    for b in range(bl):
        os_, sts = [], []
        for j in range(2):
            i = 2 * b + j
            qn, kn, att, col, glast = heads[i]
            u = sol[i][:, :128]
            w = sol[i][:, 128:]
            st = state[b][128 * j:128 * j + 128, :]
            v_new = u - bdot(w, st)
            o = bdot(qn * jnp.exp(col), st) + bdot(att, v_new)
            sts.append(st * jnp.exp(glast) + bdot(kn * jnp.exp(glast - col), v_new, TN))
            os_.append(_rms(o, nw) * _silu(zc[b][:, 128 * j:128 * j + 128]))
        outs.append(jnp.concatenate(os_, axis=1))
        states.append(jnp.concatenate(sts, axis=0))
    return jnp.concatenate([o[None] for o in outs]), jnp.concatenate([st[None] for st in states])


def gdn_op(name, bl, seq, pending=None):
    q = GDN_CHUNK
    nc = seq // q
    blk = lambda w, c0, cw: In((bl, q, w), lambda h, n: (0, n, h), cols=(c0, cw))
    small = lambda h, n: (h, 0, 0)
    ins = [
        blk(128, 0, G_KEY_DIM),
        blk(128, G_KEY_DIM, G_KEY_DIM),
        blk(256, 2 * G_KEY_DIM, G_VAL_DIM),
        blk(256, G_CONV_DIM, G_VAL_DIM),
        In((None, bl, q, 2), lambda h, n: (h, 0, n, 0)),
        In((None, bl, q, 2), lambda h, n: (h, 0, n, 0)),
        In((None, 1, 2), small, 'acc', (1,)),
        In((None, 1, 2), small, 'acc', (1,)),
        In((1, 128), lambda h, n: (0, 0), 'acc', (0, 1)),
    ]
    outs = [Out((bl, seq, G_VAL_DIM), F32, (bl, q, 256), lambda h, n: (0, n, h))]
    return make_op(name, _gdn_fn, (G_QK_HEADS, nc), ins, outs,
                   state_shape=(bl, 256, 128), seq_axis=1, pending=pending)


def _xattn_fn(q, k, v):
    s = bdot(q, k, NT) * (X_HEAD_DIM ** -0.5)
    s = s - jnp.max(s, axis=-1, keepdims=True)
    p = jnp.exp(s)
    p = p / jnp.sum(p, axis=-1, keepdims=True)
    return (bdot(p, v),)


def xattn_op(name, bl, seq):
    tq = _tile(seq, 512)
    nq = seq // tq
    t = bl * seq
    ins = [
        In((tq, X_HEAD_DIM), lambda b, h, i: (b * nq + i, h)),
        In((N_MEM, X_HEAD_DIM), lambda b, h, i: (b, h), 'acc', (2,), cols=(0, D_MODEL)),
        In((N_MEM, X_HEAD_DIM), lambda b, h, i: (b, h), 'acc', (2,), cols=(D_MODEL, D_MODEL)),
    ]
    outs = [Out((t, D_MODEL), F32, (tq, X_HEAD_DIM), lambda b, h, i: (b * nq + i, h))]
    return make_op(name, _xattn_fn, (bl, X_HEADS, nq), ins, outs)


CONV_PAD = 8
CONV_ROWS = 64


def make_conv(name, bl, seq, width, ch, x_col0, up_col0=None):
    cb = 256
    rt = CONV_ROWS
    assert ch % cb == 0 and x_col0 % cb == 0 and (up_col0 is None or up_col0 % cb == 0) and seq % rt == 0
    nb = ch // cb
    n_tiles = seq // rt
    t = bl * seq
    has_up = up_col0 is not None
    grid = (nb, bl)
    x_spec = pl.BlockSpec((seq, cb), lambda c, b: (b, x_col0 // cb + c))
    up_specs = [pl.BlockSpec((seq, cb), lambda c, b: (b, up_col0 // cb + c))] if has_up else []
    w_spec = pl.BlockSpec((width, cb), lambda c, b: (0, c))
    b_spec = pl.BlockSpec((1, cb), lambda c, b: (0, c))
    o_spec = pl.BlockSpec((seq, cb), lambda c, b: (b, c))
    taps = [CONV_PAD - (width - 1) + j for j in range(width)]

    def window(x_ref, i):
        if isinstance(i, int) and i == 0:
            return jnp.concatenate([jnp.zeros((CONV_PAD, cb), F32), x_ref[0:rt, :]], axis=0)
        return x_ref[pl.ds(pl.multiple_of(i * rt - CONV_PAD, CONV_PAD), rt + CONV_PAD), :]

    def rows(i):
        return pl.ds(i * rt, rt) if isinstance(i, int) else pl.ds(pl.multiple_of(i * rt, rt), rt)

    def shifted(win):
        return [win[tp:tp + rt, :] for tp in taps]

    def pre_activation(views, w, b):
        y = b + w[0:1, :] * views[0]
        for j in range(1, width):
            y = y + w[j:j + 1, :] * views[j]
        return y

    def over_tiles(step, carry):
        carry = step(0, carry)
        return lax.fori_loop(1, n_tiles, step, carry)

    def fwd_call(x, w, b):
        def body(*refs):
            x_ref, w_ref, b_ref = refs[:3]
            o_ref = refs[-1]
            w_, b_ = w_ref[...], b_ref[...]

            def step(i, carry):
                y = _silu(pre_activation(shifted(window(x_ref, i)), w_, b_))
                if has_up:
                    y = y * refs[3][rows(i), :]
                o_ref[rows(i), :] = y
                return carry

            over_tiles(step, 0)

        return pl.pallas_call(
            body, name=name + "_fwd", grid=grid,
            in_specs=[x_spec, w_spec, b_spec] + up_specs, out_specs=o_spec,
            out_shape=jax.ShapeDtypeStruct((t, ch), F32),
            compiler_params=_cparams(),
        )(*([x, w, b] + ([x] if has_up else [])))

    def bwd_call(x, w, b, do):
        n_in = 4 + (1 if has_up else 0)

        def body(*refs):
            x_ref, w_ref, b_ref = refs[:3]
            do_ref = refs[n_in - 1]
            dx_ref, dw_ref, db_ref = refs[n_in:n_in + 3]
            gpad_ref = refs[-1]
            w_, b_ = w_ref[...], b_ref[...]

            def fold(a):
                acc = a[0:8, :]
                for k in range(1, rt // 8):
                    acc = acc + a[8 * k:8 * k + 8, :]
                return acc

            def grad_pre(i, sums):
                views = shifted(window(x_ref, i))
                y = pre_activation(views, w_, b_)
                s = jax.nn.sigmoid(y)
                act = y * s
                do_ = do_ref[rows(i), :]
                if has_up:
                    refs[n_in + 3][rows(i), :] = do_ * act
                    do_ = do_ * refs[3][rows(i), :]
                dy = do_ * (s + act * (1.0 - s))
                gpad_ref[rows(i), :] = dy
                new = [sums[j] + fold(dy * views[j]) for j in range(width)]
                return tuple(new) + (sums[width] + fold(dy),)

            zero8 = jnp.zeros((8, cb), F32)
            sums = over_tiles(grad_pre, (zero8,) * (width + 1))
            gpad_ref[seq:seq + CONV_PAD, :] = jnp.zeros((CONV_PAD, cb), F32)

            def grad_x(i, carry):
                if isinstance(i, int):
                    gwin = gpad_ref[0:rt + CONV_PAD, :]
                else:
                    gwin = gpad_ref[pl.ds(pl.multiple_of(i * rt, rt), rt + CONV_PAD), :]
                dx = w_[0:1, :] * gwin[width - 1:width - 1 + rt, :]
                for j in range(1, width):
                    dx = dx + w_[j:j + 1, :] * gwin[width - 1 - j:width - 1 - j + rt, :]
                dx_ref[rows(i), :] = dx
                return carry

            over_tiles(grad_x, 0)

            @pl.when(pl.program_id(1) == 0)
            def _():
                dw_ref[...] = jnp.zeros_like(dw_ref)
                db_ref[...] = jnp.zeros_like(db_ref)

            dw_ref[...] += jnp.concatenate([jnp.sum(sums[j], axis=0, keepdims=True) for j in range(width)], axis=0)
            db_ref[...] += jnp.sum(sums[width], axis=0, keepdims=True)

        big = jax.ShapeDtypeStruct((t, ch), F32)
        return pl.pallas_call(
            body, name=name + "_bwd", grid=grid,
            in_specs=[x_spec, w_spec, b_spec] + up_specs + [o_spec],
            out_specs=[o_spec, w_spec, b_spec] + ([o_spec] if has_up else []),
            out_shape=[big, jax.ShapeDtypeStruct((width, ch), F32), jax.ShapeDtypeStruct((1, ch), F32)]
            + ([big] if has_up else []),
            scratch_shapes=[pltpu.VMEM((seq + CONV_PAD, cb), F32)],
            compiler_params=_cparams(),
        )(*([x, w, b] + ([x] if has_up else []) + [do]))

    @jax.custom_vjp
    def conv(x, w, b):
        return fwd_call(x, w, b)

    def conv_fwd(x, w, b):
        return fwd_call(x, w, b), (x, w, b)

    def conv_bwd(res, do):
        x, w, b = res
        got = bwd_call(x, w, b, do)
        dx = jnp.pad(got[0], ((0, 0), (x_col0, x.shape[1] - x_col0 - ch)))
        if has_up:
            dx = dx + jnp.pad(got[3], ((0, 0), (up_col0, x.shape[1] - up_col0 - ch)))
        return dx, got[1], got[2]

    conv.defvjp(conv_fwd, conv_bwd)

    def apply(x, w, b=None):
        if b is None:
            b = jnp.zeros((ch,), F32)
        return conv(x, w, b.reshape(1, ch))

    return apply


def loss_head(x, w, target):
    t = x.shape[0]
    tm = _tile(t, 512)

    def fn(xb, wb, tb):
        err = _rms(xb, wb) - tb
        return 0.5 * jnp.sum(err * err) * (1.0 / D_MODEL)

    def body(x_ref, w_ref, t_ref, loss_ref, dx_ref, dw_ref):
        @pl.when(pl.program_id(0) == 0)
        def _():
            loss_ref[...] = jnp.zeros_like(loss_ref)
            dw_ref[...] = jnp.zeros_like(dw_ref)

        tb = t_ref[...]
        val, vjp = jax.vjp(lambda a, b: fn(a, b, tb), x_ref[...], w_ref[...])
        dx, dw = vjp(jnp.ones((), F32))
        dx_ref[...] = dx
        dw_ref[...] += dw
        loss_ref[...] += jnp.full(loss_ref.shape, val, F32)

    row = pl.BlockSpec((tm, D_MODEL), lambda i: (i, 0))
    vec = pl.BlockSpec((1, D_MODEL), lambda i: (0, 0))
    loss, dx, dw = pl.pallas_call(
        body, name="loss_head", grid=(t // tm,),
        in_specs=[row, vec, row],
        out_specs=[pl.BlockSpec((8, LANE), lambda i: (0, 0)), row, vec],
        out_shape=[jax.ShapeDtypeStruct((8, LANE), F32), jax.ShapeDtypeStruct((t, D_MODEL), F32),
                   jax.ShapeDtypeStruct((1, D_MODEL), F32)],
        compiler_params=_cparams(),
    )(x, w.reshape(1, D_MODEL), target)
    return loss[0, 0], dx, dw.reshape(D_MODEL)


PACK_W = 1024
ADAM_BLOCK_BYTES = 512 * 1024


def _rows_tile(r, c):
    if r * c * 4 <= ADAM_BLOCK_BYTES or r % 8:
        return r
    best = 8
    for t in range(8, r + 1, 8):
        if r % t == 0 and t * c * 4 <= ADAM_BLOCK_BYTES:
            best = t
    return best


def reduce_adamw(slots, w, m, v, name):
    r, wd = w.shape
    tr = _rows_tile(r, wd)
    c1 = 1.0 - ADAM_B1 ** ADAM_STEP
    c2 = 1.0 - ADAM_B2 ** ADAM_STEP

    def body(s_ref, w_ref, m_ref, v_ref, g_ref, d_ref, nm_ref, nv_ref):
        g = s_ref[0].astype(F32)
        for k in range(1, N_DEV):
            g = g + s_ref[k].astype(F32)
        nm = ADAM_B1 * m_ref[...] + (1.0 - ADAM_B1) * g
        nv = ADAM_B2 * v_ref[...] + (1.0 - ADAM_B2) * (g * g)
        m_hat = nm / c1
        v_hat = nv / c2
        d_ref[...] = -ADAM_LR * (m_hat / (jnp.sqrt(v_hat) + ADAM_EPS) + ADAM_WD * w_ref[...])
        g_ref[...] = g
        nm_ref[...] = nm
        nv_ref[...] = nv

    blk = pl.BlockSpec((tr, wd), lambda i: (i, 0))
    shp = jax.ShapeDtypeStruct((r, wd), F32)
    return pl.pallas_call(
        body, name=name, grid=(r // tr,),
        in_specs=[pl.BlockSpec((N_DEV, tr, wd), lambda i: (0, i, 0)), blk, blk, blk],
        out_specs=[blk, blk, blk, blk], out_shape=[shp, shp, shp, shp],
        compiler_params=_cparams(),
    )(slots, w, m, v)


def all_gather(block, name):
    def body(x_ref, out_ref, send_sems, recv_sems, local_sem):
        x, y, c = _position()
        me, sibling = (x, y, c), (x, y, 1 - c)
        chips = [(1 - x, y), (x, 1 - y), (1 - x, 1 - y)]

        def slot(px, py, pc):
            return out_ref.at[4 * px + 2 * py + pc]

        def copy(k, owner, to, src=None):
            return pltpu.make_async_remote_copy(
                src_ref=slot(*owner) if src is None else src, dst_ref=slot(*owner),
                send_sem=send_sems.at[k], recv_sem=recv_sems.at[k],
                device_id=to, device_id_type=pl.DeviceIdType.MESH)

        mine = pltpu.make_async_copy(x_ref, slot(*me), local_sem)
        mine.start()
        first = [copy(0, me, sibling, src=x_ref)]
        first += [copy(1 + j, me, (*chip, c), src=x_ref) for j, chip in enumerate(chips)]
        for cp in first:
            cp.start()
        passed = [copy(4 + j, (*chip, c), sibling) for j, chip in enumerate(chips)]
        for j, chip in enumerate(chips):
            copy(1 + j, (*chip, c), me).wait_recv()
            passed[j].start()
        copy(0, sibling, me).wait_recv()
        for j, chip in enumerate(chips):
            copy(4 + j, (*chip, 1 - c), me).wait_recv()
        for cp in first + passed:
            cp.wait_send()
        mine.wait()

    return pl.pallas_call(
        body, name=name,
        out_shape=jax.ShapeDtypeStruct((N_DEV,) + block.shape, block.dtype),
        in_specs=[pl.BlockSpec(memory_space=pl.ANY)],
        out_specs=pl.BlockSpec(memory_space=pl.ANY),
        scratch_shapes=[pltpu.SemaphoreType.DMA((7,)), pltpu.SemaphoreType.DMA((7,)), pltpu.SemaphoreType.DMA],
    )(block)


def exchange_slabs(slabs, name):
    def body(in_ref, out_ref, send_sems, recv_sems, local_sem):
        x, y, c = _position()
        my = 4 * x + 2 * y + c
        mine = pltpu.make_async_copy(in_ref.at[my], out_ref.at[my], local_sem)
        mine.start()
        copies = []
        for k in range(1, N_DEV):
            dx, dy, dc = (k >> 2) & 1, (k >> 1) & 1, k & 1
            px = x if dx == 0 else 1 - x
            py = y if dy == 0 else 1 - y
            pc = c if dc == 0 else 1 - c
            cp = pltpu.make_async_remote_copy(
                src_ref=in_ref.at[4 * px + 2 * py + pc], dst_ref=out_ref.at[my],
                send_sem=send_sems.at[k - 1], recv_sem=recv_sems.at[k - 1],
                device_id=(px, py, pc), device_id_type=pl.DeviceIdType.MESH)
            cp.start()
            copies.append(cp)
        for cp in copies:
            cp.wait()
        mine.wait()

    return pl.pallas_call(
        body, name=name,
        out_shape=jax.ShapeDtypeStruct(slabs.shape, slabs.dtype),
        in_specs=[pl.BlockSpec(memory_space=pl.ANY)],
        out_specs=pl.BlockSpec(memory_space=pl.ANY),
        scratch_shapes=[pltpu.SemaphoreType.DMA((7,)), pltpu.SemaphoreType.DMA((7,)), pltpu.SemaphoreType.DMA],
    )(slabs)


def _pack(arrays, dtype, row_multiple):
    flat = jnp.concatenate([a.astype(dtype).reshape(-1) for a in arrays])
    n = flat.shape[0]
    per = PACK_W * row_multiple
    total = -(-n // per) * per
    flat = jnp.pad(flat, (0, total - n))
    return flat.reshape(total // PACK_W, PACK_W)


def _unpack(flat2d, shapes, lead=()):
    flat = flat2d.reshape(lead + (-1,))
    out, off = [], 0
    for shp in shapes:
        n = math.prod(shp)
        out.append(flat[..., off:off + n].reshape(lead + tuple(shp)))
        off += n
    return out


def _full_from_gathered(g, axis):
    g = jnp.moveaxis(g, 0, axis)
    shp = list(g.shape)
    shp[axis:axis + 2] = [shp[axis] * shp[axis + 1]]
    return g.reshape(shp)


def _shards_of_full(full, axis):
    shp = list(full.shape)
    shp[axis:axis + 1] = [N_DEV, shp[axis] // N_DEV]
    return jnp.moveaxis(full.reshape(shp), axis, 0)


def layer_units(i):
    mixer = [('m_in_w', 'm_out_w'), ('h_in_w', 'h_out_w'), ('g_in_w', 'g_out_w')][i % 3]
    return [(mixer[0], i // 3), (mixer[1], i // 3), ('xa_q', i), ('xa_kv', i), ('xa_o', i), ('f_up', i), ('f_down', i)]


PADDED_COLS = {'m_in_w': M_IN_PAD, 'g_in_w': G_IN_PAD}


def whole_weight(name, gathered):
    w = _full_from_gathered(lax.stop_gradient(gathered), SHARD_AXIS[name] - 1)
    return _pad_cols(w, 1, PADDED_COLS[name]) if name in PADDED_COLS else w


def _trunk(p, weights, blocks, standins, x, mem, bl, seq, pending=None):
    t = bl * seq
    ia = ib = ic = 0
    weights = dict(weights)
    state = {}

    def lin(name, a, wname, idx, residual=None):
        unit = (wname, idx)
        pos = state['units'].index(unit)
        later = state['next'][pos] if state['next'] else None
        nxt = (blocks[later],) if later in blocks else ()
        n_real = N_DEV * standins[unit].shape[2]
        res = () if residual is None else (residual,)
        defer = (pending, unit) if pending is not None and wname not in ('m_in_w', 'h_in_w', 'g_in_w') else None
        y, got = make_linear(name, SHARD_AXIS[wname] - 1, n_real, bool(res), defer)(
            a, weights[unit], standins[unit], nxt, res)
        if nxt:
            weights[later] = whole_weight(later[0], got[0])
        return y

    by_seq = lambda a: a.reshape(bl, seq, a.shape[-1])

    for i in range(DEPTH):
        state['units'] = layer_units(i)
        state['next'] = layer_units(i + 1) if i + 1 < DEPTH else None
        hn, x = rmsnorm_op(f"ln_mix{i}", t, F32, residual=True)(x, p['ln_mix'][i:i + 1])
        kind = i % 3
        if kind == 0:
            proj = lin(f"m_in{i}", hn, 'm_in_w', ia)
            xbc = make_conv(f"m_conv{i}", bl, seq, 4, M_CONV_DIM, M_D_INNER)(
                proj, p['m_conv_w'][ia], p['m_conv_b'][ia])
            dt = proj[:, M_D_INNER + M_CONV_DIM:M_IN].reshape(bl, seq, M_GROUPS, 4).transpose(2, 0, 1, 3)
            grp = lambda a, n=4: a.reshape(M_GROUPS, 1, n)
            proj3, xbc3 = by_seq(proj), by_seq(xbc)
            y = ssd_op(f"ssd{i}", bl, seq, pending)(
                proj3, xbc3, xbc3, xbc3, dt, grp(p['m_dt_bias'][ia]), grp(p['m_a_log'][ia]), grp(p['m_d'][ia]),
                grp(p['m_norm_w'][ia], 256))[0]
            x = lin(f"m_out{i}", y.reshape(t, M_D_INNER), 'm_out_w', ia, residual=x)
            ia += 1
        elif kind == 1:
            proj3 = by_seq(lin(f"h_in{i}", hn, 'h_in_w', ib))
            y = gla_op(f"gla{i}", i, bl, seq, pending)(
                proj3, proj3, proj3, proj3, p['h_lower_bounds'], p['h_norm_w'][ib:ib + 1])[0]
            x = lin(f"h_out{i}", y.reshape(t, D_MODEL), 'h_out_w', ib, residual=x)
            ib += 1
        else:
            proj = lin(f"g_in{i}", hn, 'g_in_w', ic)
            qkv = make_conv(f"g_conv{i}", bl, seq, 4, G_CONV_DIM, 0)(proj, p['g_conv_w'][ic])
            c0 = G_CONV_DIM + G_VAL_DIM
            heads = lambda a: a.reshape(bl, seq, G_QK_HEADS, 2).transpose(2, 0, 1, 3)
            braw = heads(proj[:, c0:c0 + G_V_HEADS])
            araw = heads(proj[:, c0 + G_V_HEADS:c0 + 2 * G_V_HEADS])
            grp = lambda a: a.reshape(G_QK_HEADS, 1, 2)
            qkv3 = by_seq(qkv)
            y = gdn_op(f"gdn{i}", bl, seq, pending)(
                qkv3, qkv3, qkv3, by_seq(proj), braw, araw, grp(p['g_a_log'][ic]), grp(p['g_dt_bias'][ic]),
                p['g_norm_w'][ic:ic + 1])[0]
            x = lin(f"g_out{i}", y.reshape(t, G_VAL_DIM), 'g_out_w', ic, residual=x)
            ic += 1
        hq, x = rmsnorm_op(f"ln_xattn{i}", t, F32, residual=True)(x, p['ln_xattn'][i:i + 1])
        mn = rmsnorm_op(f"ln_mem{i}", bl * N_MEM, F32)(mem, p['ln_mem'][i:i + 1])[0]
        qx = lin(f"xa_q{i}", hq, 'xa_q', i)
        kv = lin(f"xa_kv{i}", mn, 'xa_kv', i)
        ao = xattn_op(f"xattn{i}", bl, seq)(qx, kv, kv)[0]
        x = lin(f"xa_o{i}", ao, 'xa_o', i, residual=x)
        hf, x = rmsnorm_op(f"ln_ffn{i}", t, F32, residual=True)(x, p['ln_ffn'][i:i + 1])
        up = lin(f"f_up{i}", hf, 'f_up', i)
        act = make_conv(f"f_conv{i}", bl, seq, 3, D_FF, 0, up_col0=D_FF)(up, p['f_conv_w'][i], p['f_conv_b'][i])
        x = lin(f"f_down{i}", act, 'f_down', i, residual=x)
    return x


def _pad_cols(w, axis, to):
    pad = [(0, 0)] * w.ndim
    pad[axis] = (0, to - w.shape[axis])
    return jnp.pad(w, pad)


def kernel(x, mem, ln_mix, ln_xattn, ln_mem, ln_ffn, final_norm, m_in_w, m_conv_w, m_conv_b, m_dt_bias, m_a_log, m_d, m_norm_w, m_out_w, h_in_w, h_lower_bounds, h_norm_w, h_out_w, g_in_w, g_conv_w, g_a_log, g_dt_bias, g_norm_w, g_out_w, xa_q, xa_kv, xa_o, f_up, f_conv_w, f_conv_b, f_down, loss_target, m_ln_mix, m_ln_xattn, m_ln_mem, m_ln_ffn, m_final_norm, m_m_in_w, m_m_conv_w, m_m_conv_b, m_m_dt_bias, m_m_a_log, m_m_d, m_m_norm_w, m_m_out_w, m_h_in_w, m_h_lower_bounds, m_h_norm_w, m_h_out_w, m_g_in_w, m_g_conv_w, m_g_a_log, m_g_dt_bias, m_g_norm_w, m_g_out_w, m_xa_q, m_xa_kv, m_xa_o, m_f_up, m_f_conv_w, m_f_conv_b, m_f_down, v_ln_mix, v_ln_xattn, v_ln_mem, v_ln_ffn, v_final_norm, v_m_in_w, v_m_conv_w, v_m_conv_b, v_m_dt_bias, v_m_a_log, v_m_d, v_m_norm_w, v_m_out_w, v_h_in_w, v_h_lower_bounds, v_h_norm_w, v_h_out_w, v_g_in_w, v_g_conv_w, v_g_a_log, v_g_dt_bias, v_g_norm_w, v_g_out_w, v_xa_q, v_xa_kv, v_xa_o, v_f_up, v_f_conv_w, v_f_conv_b, v_f_down):
    local = dict(ln_mix=ln_mix, ln_xattn=ln_xattn, ln_mem=ln_mem, ln_ffn=ln_ffn, final_norm=final_norm, m_in_w=m_in_w, m_conv_w=m_conv_w, m_conv_b=m_conv_b, m_dt_bias=m_dt_bias, m_a_log=m_a_log, m_d=m_d, m_norm_w=m_norm_w, m_out_w=m_out_w, h_in_w=h_in_w, h_lower_bounds=h_lower_bounds, h_norm_w=h_norm_w, h_out_w=h_out_w, g_in_w=g_in_w, g_conv_w=g_conv_w, g_a_log=g_a_log, g_dt_bias=g_dt_bias, g_norm_w=g_norm_w, g_out_w=g_out_w, xa_q=xa_q, xa_kv=xa_kv, xa_o=xa_o, f_up=f_up, f_conv_w=f_conv_w, f_conv_b=f_conv_b, f_down=f_down)
    mom_m = dict(ln_mix=m_ln_mix, ln_xattn=m_ln_xattn, ln_mem=m_ln_mem, ln_ffn=m_ln_ffn, final_norm=m_final_norm, m_in_w=m_m_in_w, m_conv_w=m_m_conv_w, m_conv_b=m_m_conv_b, m_dt_bias=m_m_dt_bias, m_a_log=m_m_a_log, m_d=m_m_d, m_norm_w=m_m_norm_w, m_out_w=m_m_out_w, h_in_w=m_h_in_w, h_lower_bounds=m_h_lower_bounds, h_norm_w=m_h_norm_w, h_out_w=m_h_out_w, g_in_w=m_g_in_w, g_conv_w=m_g_conv_w, g_a_log=m_g_a_log, g_dt_bias=m_g_dt_bias, g_norm_w=m_g_norm_w, g_out_w=m_g_out_w, xa_q=m_xa_q, xa_kv=m_xa_kv, xa_o=m_xa_o, f_up=m_f_up, f_conv_w=m_f_conv_w, f_conv_b=m_f_conv_b, f_down=m_f_down)
    mom_v = dict(ln_mix=v_ln_mix, ln_xattn=v_ln_xattn, ln_mem=v_ln_mem, ln_ffn=v_ln_ffn, final_norm=v_final_norm, m_in_w=v_m_in_w, m_conv_w=v_m_conv_w, m_conv_b=v_m_conv_b, m_dt_bias=v_m_dt_bias, m_a_log=v_m_a_log, m_d=v_m_d, m_norm_w=v_m_norm_w, m_out_w=v_m_out_w, h_in_w=v_h_in_w, h_lower_bounds=v_h_lower_bounds, h_norm_w=v_h_norm_w, h_out_w=v_h_out_w, g_in_w=v_g_in_w, g_conv_w=v_g_conv_w, g_a_log=v_g_a_log, g_dt_bias=v_g_dt_bias, g_norm_w=v_g_norm_w, g_out_w=v_g_out_w, xa_q=v_xa_q, xa_kv=v_xa_kv, xa_o=v_xa_o, f_up=v_f_up, f_conv_w=v_f_conv_w, f_conv_b=v_f_conv_b, f_down=v_f_down)

    bl, seq, _ = x.shape
    t = bl * seq

    p = {n: local[n] for n in WEIGHTS if n not in SHARD_AXIS}
    for n in SMALL_SHARDED:
        p[n] = _full_from_gathered(all_gather(local[n], f"gather_{n}"), SHARD_AXIS[n])
    units = [(n, l) for n in MATMUL_WEIGHTS for l in range(local[n].shape[0])]
    block = lambda u: local[u[0]][u[1]].astype(BF16)
    weights = {u: whole_weight(u[0], all_gather(block(u), f"gather_{u[0]}{u[1]}")) for u in layer_units(0)}
    blocks = {u: block(u) for u in units if u not in weights}
    standins = {u: jnp.zeros((N_DEV,) + local[u[0]].shape[1:], BF16) for u in units}
    small = {n: p[n] for n in WEIGHTS if n not in MATMUL_WEIGHTS and n != 'final_norm'}

    pending = Pending()

    def run(small_w, standins_, xin):
        return _trunk(small_w, weights, blocks, standins_, xin, mem.reshape(bl * N_MEM, D_MODEL), bl, seq, pending)

    x_out, vjp = jax.vjp(run, small, standins, x.reshape(t, D_MODEL))
    loss_part, dx_out, d_final = loss_head(x_out, final_norm, loss_target.reshape(t, D_MODEL))
    grads, received, dx = vjp(dx_out)
    received = dict(received)
    for unit, slabs in pending.take_all():
        pending.received[unit] = exchange_slabs(slabs, f"exchange_{unit[0]}{unit[1]}")
    received.update(pending.received)
    grads = dict(grads)
    grads['final_norm'] = d_final
    loss = lax.psum(loss_part, ("x", "y", "c"))

    outs = {}

    def update(name, n, slots, shape, sel=lambda a: a):
        two_d = lambda a: sel(a).reshape(slots.shape[1:])
        got = reduce_adamw(slots, two_d(local[n]), two_d(mom_m[n]), two_d(mom_v[n]), name)
        return [g.reshape(shape) for g in got]

    for n in SMALL_SHARDED:
        slots = exchange_slabs(_shards_of_full(grads[n], SHARD_AXIS[n]), f"exchange_{n}")
        slots = slots.reshape(N_DEV, -1, slots.shape[-1])
        for kind, a in zip(KINDS, update(f"adamw_{n}", n, slots, local[n].shape)):
            outs[kind, n] = a
    for n in MATMUL_WEIGHTS:
        per_layer = [update(f"adamw_{n}{l}", n, received[n, l], local[n].shape[1:], lambda a, l=l: a[l])
                     for l in range(local[n].shape[0])]
        for k, kind in enumerate(KINDS):
            outs[kind, n] = jnp.concatenate([got[k][None] for got in per_layer])
    replicated = [n for n in WEIGHTS if n not in SHARD_AXIS]
    pk = lambda d: _pack([d[n] for n in replicated], F32, 8)
    got = reduce_adamw(all_gather(pk(grads), "gather_replicated_grads"), pk(local), pk(mom_m), pk(mom_v),
                       "adamw_replicated")
    shapes = [local[n].shape for n in replicated]
    for kind, buf in zip(KINDS, got):
        for n, a in zip(replicated, _unpack(buf, shapes)):
            outs[kind, n] = a
    result = [loss, dx.reshape(bl, seq, D_MODEL)]
    for kind in KINDS:
        result += [outs[kind, n] for n in WEIGHTS]
    return tuple(result)
```

```python
import functools
import math

import jax
import jax.numpy as jnp
from jax import lax
from jax.experimental import pallas as pl
from jax.experimental.pallas import tpu as pltpu

F32 = jnp.float32
BF16 = jnp.bfloat16
NN = (((1,), (0,)), ((), ()))
NT = (((1,), (1,)), ((), ()))
TN = (((0,), (0,)), ((), ()))

D_MODEL = 1024
DEPTH = 4
EPS = 1e-6
N_MEM = 256
M_D_INNER = 2048
M_HEADS = 32
M_GROUPS = 8
M_STATE = 128
M_CONV_DIM = 4096
M_IN = 6176
M_IN_PAD = 6272
SSD_CHUNK = 256
H_HEADS = 8
HGRN_CHUNK = 32
HGRN_ROWS = 128
G_QK_HEADS = 8
G_V_HEADS = 16
G_KEY_DIM = 1024
G_VAL_DIM = 2048
G_CONV_DIM = 4096
G_IN = 6176
G_IN_PAD = 6272
GDN_CHUNK = 64
X_HEADS = 4
X_HEAD_DIM = 256
D_FF = 2816
ADAM_LR = 0.001
ADAM_B1 = 0.9
ADAM_B2 = 0.999
ADAM_EPS = 1e-08
ADAM_WD = 0.01
ADAM_STEP = 10

N_DEV = 8
LANE = 128
KINDS = ('grad', 'delta', 'new_m', 'new_v')
VMEM_LIMIT = 56 * 1024 * 1024

WEIGHTS = ['ln_mix', 'ln_xattn', 'ln_mem', 'ln_ffn', 'final_norm', 'm_in_w', 'm_conv_w', 'm_conv_b', 'm_dt_bias',
           'm_a_log', 'm_d', 'm_norm_w', 'm_out_w', 'h_in_w', 'h_lower_bounds', 'h_norm_w', 'h_out_w', 'g_in_w',
           'g_conv_w', 'g_a_log', 'g_dt_bias', 'g_norm_w', 'g_out_w', 'xa_q', 'xa_kv', 'xa_o', 'f_up', 'f_conv_w',
           'f_conv_b', 'f_down']
SHARD_AXIS = {'m_in_w': 2, 'm_conv_w': 2, 'm_conv_b': 1, 'm_norm_w': 1, 'm_out_w': 1, 'h_in_w': 2, 'h_out_w': 1,
              'g_in_w': 2, 'g_conv_w': 2, 'g_out_w': 1, 'xa_q': 1, 'xa_kv': 2, 'xa_o': 1, 'f_up': 2, 'f_conv_w': 2,
              'f_down': 1}
MATMUL_WEIGHTS = ['m_in_w', 'm_out_w', 'h_in_w', 'h_out_w', 'g_in_w', 'g_out_w', 'xa_q', 'xa_kv', 'xa_o', 'f_up',
                  'f_down']
SMALL_SHARDED = ['m_conv_w', 'm_conv_b', 'm_norm_w', 'g_conv_w', 'f_conv_w']


def _cparams():
    return pltpu.CompilerParams(vmem_limit_bytes=VMEM_LIMIT)


def bdot(a, b, dims=NN):
    return lax.dot_general(a.astype(BF16), b.astype(BF16), dims, preferred_element_type=F32)


def _split(a):
    hi = a.astype(BF16)
    return hi, (a - hi.astype(F32)).astype(BF16)


def _h3(a, b, dims):
    ah, al = _split(a)
    bh, bl = _split(b)
    d = functools.partial(lax.dot_general, dimension_numbers=dims, preferred_element_type=F32)
    return d(ah, bh) + (d(ah, bl) + d(al, bh))


BNN = (((2,), (1,)), ((0,), (0,)))
BNT = (((2,), (2,)), ((0,), (0,)))
BTN = (((1,), (1,)), ((0,), (0,)))


@jax.custom_vjp
def h3dot_b(a, b):
    return _h3(a, b, BNN)


h3dot_b.defvjp(lambda a, b: (_h3(a, b, BNN), (a, b)),
               lambda res, ct: (_h3(ct, res[1], BNT), _h3(res[0], ct, BTN)))

T_ROWS = (((0,), (1,)), ((), ()))


def _tri_times(tri, x, dims, tri_first):
    t = tri.astype(BF16)
    x0 = x.astype(BF16)
    r1 = x - x0.astype(F32)
    x1 = r1.astype(BF16)
    x2 = (r1 - x1.astype(F32)).astype(BF16)
    if tri_first:
        d = lambda xx: lax.dot_general(t, xx, dims, preferred_element_type=F32)
    else:
        d = lambda xx: lax.dot_general(xx, t, dims, preferred_element_type=F32)
    return d(x0) + (d(x1) + d(x2))


@jax.custom_vjp
def cumdot(tri, x):
    return _tri_times(tri, x, NN, True)


cumdot.defvjp(lambda tri, x: (_tri_times(tri, x, NN, True), tri),
              lambda tri, ct: (jnp.zeros_like(tri), _tri_times(tri, ct, TN, True)))


@jax.custom_vjp
def cumdot_t(tri, x):
    return _tri_times(tri, x, T_ROWS, False)


cumdot_t.defvjp(lambda tri, x: (_tri_times(tri, x, T_ROWS, False), tri),
                lambda tri, ct: (jnp.zeros_like(tri), _tri_times(tri, ct, T_ROWS, True)))


def _tile(dim, cap):
    if dim <= cap:
        return dim
    best = None
    for t in range(LANE, cap + 1, LANE):
        if dim % t == 0:
            best = t
    assert best is not None, dim
    return best


def _position():
    return lax.axis_index("x"), lax.axis_index("y"), lax.axis_index("c")


def _direct_copies(kind, src_ref, dst_ref, send_sems, recv_sems, local_sem):
    x, y, c = _position()
    me = 4 * x + 2 * y + c
    local_src = src_ref if kind == 'gather' else src_ref.at[me]
    copies = [pltpu.make_async_copy(local_src, dst_ref.at[me], local_sem)]
    for k in range(1, N_DEV):
        px = 1 - x if (k >> 2) & 1 else x
        py = 1 - y if (k >> 1) & 1 else y
        pc = 1 - c if k & 1 else c
        copies.append(pltpu.make_async_remote_copy(
            src_ref=src_ref if kind == 'gather' else src_ref.at[4 * px + 2 * py + pc], dst_ref=dst_ref.at[me],
            send_sem=send_sems.at[k - 1], recv_sem=recv_sems.at[k - 1],
            device_id=(px, py, pc), device_id_type=pl.DeviceIdType.MESH))
    return copies


COMM_SCRATCH = [pltpu.SemaphoreType.DMA((N_DEV - 1,)), pltpu.SemaphoreType.DMA((N_DEV - 1,)), pltpu.SemaphoreType.DMA]


class Pending:
    def __init__(self):
        self.jobs, self.received = [], {}

    def take_all(self):
        jobs, self.jobs = self.jobs, []
        return jobs


def matmul(a, b, *, ta=False, tb=False, out_dtype=F32, name="mm", carry=None, residual=None):
    if ta:
        k, m = a.shape
    else:
        m, k = a.shape
    if tb:
        n, k2 = b.shape
    else:
        k2, n = b.shape
    assert k == k2, (a.shape, b.shape, ta, tb)
    tm = _tile(m, 1024)
    tn = _tile(n, 1408)
    tk = _tile(k, 1408)
    grid = (m // tm, n // tn, k // tk)
    nk = grid[2]
    dims = (((0 if ta else 1,), (1 if tb else 0,)), ((), ()))

    def at_step(which):
        conds = [pl.program_id(ax) == (0 if which == 'first' else grid[ax] - 1) for ax in range(3)]
        return jnp.logical_and(jnp.logical_and(conds[0], conds[1]), conds[2])

    def body(*refs):
        r_ref = None
        if residual is not None:
            r_ref, refs = refs[2], refs[:2] + refs[3:]
        if carry is None:
            a_ref, b_ref, o_ref, acc_ref = refs
        else:
            a_ref, b_ref, src_ref, o_ref, dst_ref, acc_ref, send_sems, recv_sems, local_sem = refs
            copies = lambda: _direct_copies(carry[0], src_ref, dst_ref, send_sems, recv_sems, local_sem)

            @pl.when(at_step('first'))
            def _():
                for cp in copies():
                    cp.start()

        @pl.when(pl.program_id(2) == 0)
        def _():
            acc_ref[...] = jnp.zeros_like(acc_ref)

        acc_ref[...] += lax.dot_general(a_ref[...].astype(BF16), b_ref[...].astype(BF16), dims,
                                        preferred_element_type=F32)

        @pl.when(pl.program_id(2) == nk - 1)
        def _():
            out = acc_ref[...] if r_ref is None else acc_ref[...] + r_ref[...]
            o_ref[...] = out.astype(o_ref.dtype)

        if carry is not None:
            @pl.when(at_step('last'))
            def _():
                for cp in copies():
                    cp.wait()

    a_spec = pl.BlockSpec((tk, tm), lambda i, j, kk: (kk, i)) if ta else pl.BlockSpec((tm, tk), lambda i, j, kk: (i, kk))
    b_spec = pl.BlockSpec((tn, tk), lambda i, j, kk: (j, kk)) if tb else pl.BlockSpec((tk, tn), lambda i, j, kk: (kk, j))
    o_spec = pl.BlockSpec((tm, tn), lambda i, j, kk: (i, j))
    o_shape = jax.ShapeDtypeStruct((m, n), out_dtype)
    acc = pltpu.VMEM((tm, tn), F32)
    ins, in_specs = [a, b], [a_spec, b_spec]
    if residual is not None:
        ins.append(residual)
        in_specs.append(o_spec)
    if carry is None:
        return pl.pallas_call(
            body, name=name, grid=grid, in_specs=in_specs, out_specs=o_spec, out_shape=o_shape,
            scratch_shapes=[acc], compiler_params=_cparams(),
        )(*ins)
    kind, src = carry
    got = jax.ShapeDtypeStruct(((N_DEV,) + src.shape) if kind == 'gather' else src.shape, src.dtype)
    hbm = pl.BlockSpec(memory_space=pl.ANY)
    return pl.pallas_call(
        body, name=name, grid=grid, in_specs=in_specs + [hbm], out_specs=[o_spec, hbm],
        out_shape=[o_shape, got], scratch_shapes=[acc] + COMM_SCRATCH, compiler_params=_cparams(),
    )(*ins, src)


def make_linear(name, shard_axis, n_real, has_res=False, defer=None):
    def forward(a, w, nxt, res):
        r = res[0] if res else None
        if nxt:
            y, got = matmul(a, w, name=name + "_fwd", carry=('gather', nxt[0]), residual=r)
            return y, (got,)
        return matmul(a, w, name=name + "_fwd", residual=r), ()

    @jax.custom_vjp
    def linear(a, w, wg, nxt, res):
        return forward(a, w, nxt, res)

    def fwd(a, w, wg, nxt, res):
        return forward(a, w, nxt, res), (a, w, nxt)

    def bwd(saved, cts):
        a, w, nxt = saved
        dy = cts[0]
        dw = matmul(a, dy, ta=True, out_dtype=BF16, name=name + "_bwd_dw")
        slabs = _shards_of_full(dw[:, :n_real], shard_axis)
        if defer is None:
            da, slots = matmul(dy, w, tb=True, out_dtype=a.dtype, name=name + "_bwd_da", carry=('exchange', slabs))
        else:
            defer[0].jobs.append((defer[1], slabs))
            da, slots = matmul(dy, w, tb=True, out_dtype=a.dtype, name=name + "_bwd_da"), jnp.zeros_like(slabs)
        return da, jnp.zeros_like(w), slots, tuple(jnp.zeros_like(b) for b in nxt), ((dy,) if has_res else ())

    linear.defvjp(fwd, bwd)
    return linear


class In:
    def __init__(self, block, imap, kind='blk', inner=(), cols=None):
        self.block, self.imap, self.kind, self.inner, self.cols = block, imap, kind, inner, cols


class Out:
    def __init__(self, shape, dtype, block, imap):
        self.shape, self.dtype, self.block, self.imap = shape, dtype, block, imap


def make_op(name, fn, grid, ins, outs, state_shape=None, seq_axis=None, passthrough=(), pending=None):
    n_in, n_out = len(ins), len(outs)
    has_state = state_shape is not None
    nd = len(grid)
    diff_idx = [i for i, s in enumerate(ins) if s.kind != 'const']

    def in_spec(s, reverse):
        off = 0
        if s.cols is not None:
            assert s.cols[0] % s.block[-1] == 0
            off = s.cols[0] // s.block[-1]

        def imap(*ids):
            ids = rev(ids) if reverse else ids
            idx = tuple(s.imap(*ids))
            return idx[:-1] + (idx[-1] + off,) if off else idx

        return pl.BlockSpec(s.block, imap)

    def rel_spec(block, f, reverse):
        return pl.BlockSpec(block, (lambda *ids: f(*rev(ids))) if reverse else f)

    def rev(ids):
        if not has_state:
            return ids
        ids = list(ids)
        ids[seq_axis] = grid[seq_axis] - 1 - ids[seq_axis]
        return tuple(ids)

    save_shape = tuple(grid) + tuple(state_shape) if has_state else None
    save_block = (None,) * nd + tuple(state_shape) if has_state else None

    def save_imap(*ids):
        return tuple(ids) + (0,) * len(state_shape)

    def fwd_call(*xs):
        def body(*refs):
            in_refs = refs[:n_in]
            out_refs = refs[n_in:n_in + n_out]
            vals = [r[...] for r in in_refs]
            if has_state:
                save_ref, st_ref = refs[n_in + n_out], refs[n_in + n_out + 1]

                @pl.when(pl.program_id(seq_axis) == 0)
                def _():
                    st_ref[...] = jnp.zeros(state_shape, F32)

                st = st_ref[...]
                save_ref[...] = st
                res = fn(*vals, st)
                st_ref[...] = res[-1]
                res = res[:-1]
            else:
                res = fn(*vals)
            for o, v in zip(out_refs, res):
                o[...] = v.astype(o.dtype)

        out_shape = [jax.ShapeDtypeStruct(o.shape, o.dtype) for o in outs]
        out_specs = [pl.BlockSpec(o.block, o.imap) for o in outs]
        scratch = []
        if has_state:
            out_shape.append(jax.ShapeDtypeStruct(save_shape, F32))
            out_specs.append(pl.BlockSpec(save_block, save_imap))
            scratch.append(pltpu.VMEM(state_shape, F32))
        return pl.pallas_call(
            body, name=name + "_fwd", grid=grid,
            in_specs=[in_spec(s, False) for s in ins],
            out_specs=out_specs, out_shape=out_shape, scratch_shapes=scratch,
            compiler_params=_cparams(),
        )(*xs)

    def grad_shape(s, x):
        if s.cols is not None:
            return x.shape[:-1] + (s.cols[1],)
        return x.shape

    def bwd_call(xs, save, cts, pass_cts=()):
        n_diff = len(diff_idx)
        jobs = pending.take_all() if pending is not None else []
        n_args = n_in + (1 if has_state else 0) + n_out + len(passthrough)

        def at_step(which):
            conds = [pl.program_id(ax) == (0 if which == 'first' else grid[ax] - 1) for ax in range(nd)]
            return functools.reduce(jnp.logical_and, conds)

        def body(*refs):
            if jobs:
                src_refs = refs[n_args:n_args + len(jobs)]
                dst_refs = refs[n_args + len(jobs) + n_diff:n_args + 2 * len(jobs) + n_diff]
                sems = refs[len(refs) - 3 * len(jobs):]
                refs = refs[:n_args] + refs[n_args + len(jobs):n_args + len(jobs) + n_diff] + \
                    refs[n_args + 2 * len(jobs) + n_diff:len(refs) - 3 * len(jobs)]
                copies = lambda: [cp for k in range(len(jobs)) for cp in _direct_copies(
                    'exchange', src_refs[k], dst_refs[k], sems[3 * k], sems[3 * k + 1], sems[3 * k + 2])]

                @pl.when(at_step('first'))
                def _():
                    for cp in copies():
                        cp.start()

            in_refs = refs[:n_in]
            p = n_in
            if has_state:
                save_ref = refs[p]
                p += 1
            ct_refs = refs[p:p + n_out]
            p += n_out
            pass_refs = dict(zip(passthrough, refs[p:p + len(passthrough)]))
            p += len(passthrough)
            g_refs = refs[p:p + n_diff]
            p += n_diff
            vals = [r[...] for r in in_refs]

            def g(*dv):
                full = list(vals)
                for i, v in zip(diff_idx, dv):
                    full[i] = v
                if has_state:
                    return tuple(fn(*full, dv[-1]))
                return tuple(fn(*full))

            prim = [vals[i] for i in diff_idx]
            ct = tuple(r[...].astype(F32) for r in ct_refs)
            if has_state:
                dst_ref = refs[p]

                @pl.when(pl.program_id(seq_axis) == 0)
                def _():
                    dst_ref[...] = jnp.zeros(state_shape, F32)

                prim = prim + [save_ref[...]]
                ct = ct + (dst_ref[...],)
            _, vjp = jax.vjp(g, *prim)
            grads = vjp(ct)
            for k, i in enumerate(diff_idx):
                s = ins[i]
                if s.kind == 'blk':
                    g = grads[k] + pass_refs[i][...] if i in pass_refs else grads[k]
                    g_refs[k][...] = g.astype(g_refs[k].dtype)
                else:
                    first = None
                    for ax in s.inner:
                        c = pl.program_id(ax) == 0
                        first = c if first is None else jnp.logical_and(first, c)

                    @pl.when(first)
                    def _(k=k):
                        g_refs[k][...] = jnp.zeros_like(g_refs[k])

                    g_refs[k][...] += grads[k].astype(g_refs[k].dtype)
            if has_state:
                dst_ref[...] = grads[-1]

            if jobs:
                @pl.when(at_step('last'))
                def _():
                    for cp in copies():
                        cp.wait()

        in_specs = [in_spec(s, True) for s in ins]
        args = list(xs)
        if has_state:
            in_specs.append(rel_spec(save_block, save_imap, True))
            args.append(save)
        for o, c in zip(outs, cts):
            in_specs.append(rel_spec(o.block, o.imap, True))
            args.append(c)
        for i, c in zip(passthrough, pass_cts):
            assert ins[i].kind == 'blk' and ins[i].cols is None
            in_specs.append(rel_spec(ins[i].block, ins[i].imap, True))
            args.append(c)
        out_shape, out_specs = [], []
        for i in diff_idx:
            s = ins[i]
            out_shape.append(jax.ShapeDtypeStruct(grad_shape(s, xs[i]), xs[i].dtype))
            out_specs.append(rel_spec(s.block, s.imap, True))
        scratch = [pltpu.VMEM(state_shape, F32)] if has_state else []
        hbm = pl.BlockSpec(memory_space=pl.ANY)
        for _, slabs in jobs:
            in_specs.append(hbm)
            args.append(slabs)
            out_specs.append(hbm)
            out_shape.append(jax.ShapeDtypeStruct(slabs.shape, slabs.dtype))
            scratch += COMM_SCRATCH
        got = pl.pallas_call(
            body, name=name + "_bwd", grid=grid,
            in_specs=in_specs, out_specs=out_specs, out_shape=out_shape, scratch_shapes=scratch,
            compiler_params=_cparams(),
        )(*args)
        for (unit, _), slots in zip(jobs, got[n_diff:]):
            pending.received[unit] = slots
        return got[:n_diff]

    @jax.custom_vjp
    def op(*xs):
        return tuple(fwd_call(*xs)[:n_out]) + tuple(xs[i] for i in passthrough)

    def op_fwd(*xs):
        res = fwd_call(*xs)
        return tuple(res[:n_out]) + tuple(xs[i] for i in passthrough), (xs, res[n_out] if has_state else None)

    def op_bwd(resid, cts):
        xs, save = resid
        grads = bwd_call(xs, save, cts[:n_out], cts[n_out:])
        out = []
        k = 0
        for i, s in enumerate(ins):
            if s.kind == 'const':
                out.append(jnp.zeros_like(xs[i]))
                continue
            g = grads[k]
            k += 1
            if s.cols is not None:
                g = jnp.pad(g, ((0, 0),) * (g.ndim - 1) + ((s.cols[0], xs[i].shape[-1] - s.cols[0] - s.cols[1]),))
            out.append(g)
        return tuple(out)

    op.defvjp(op_fwd, op_bwd)
    return op


def _rms(x, w):
    return x * lax.rsqrt(jnp.mean(x * x, axis=-1, keepdims=True) + EPS) * w


def _silu(x):
    return x * jax.nn.sigmoid(x)


def rmsnorm_op(name, t, out_dtype, residual=False):
    tm = _tile(t, 512)
    return make_op(
        name, lambda x, w: (_rms(x, w),), (t // tm,),
        [In((tm, D_MODEL), lambda i: (i, 0)), In((1, D_MODEL), lambda i: (0, 0), 'acc', (0,))],
        [Out((t, D_MODEL), out_dtype, (tm, D_MODEL), lambda i: (i, 0))], passthrough=(0,) if residual else ())


def _tri(q):
    ii = lax.broadcasted_iota(jnp.int32, (q, q), 0)
    jj = lax.broadcasted_iota(jnp.int32, (q, q), 1)
    return ii >= jj, ii > jj


def _ssd_fn(z, x, bm, cm, dtr, dtb, alog, dsk, nw, state):
    q = x.shape[0]
    incl, _ = _tri(q)
    tril = incl.astype(F32)
    dt = jax.nn.softplus(dtr + dtb)
    da = dt * (-jnp.exp(alog))
    acum = cumdot(tril, da)
    acum_t = cumdot_t(tril, da)
    cb = bdot(cm, bm, NT)
    heads = range(4)
    wide = lambda a: jnp.concatenate([jnp.broadcast_to(a[:, r:r + 1], (a.shape[0], 64)) for r in heads], axis=1)
    last = acum[q - 1:q, :]
    xc = x * wide(dt)
    y = bdot(cm, state, NT) * wide(jnp.exp(acum)) + wide(dsk) * x
    ds = bdot(xc * wide(jnp.exp(last - acum)), bm, TN)
    e_last = jnp.exp(last)
    new_state = state * jnp.concatenate([jnp.broadcast_to(e_last[:, r:r + 1], (64, 1)) for r in heads], axis=0) + ds
    diag = []
    for r in heads:
        decay = jnp.exp(jnp.where(incl, acum[:, r:r + 1] - acum_t[r:r + 1, :], -jnp.inf))
        diag.append(bdot(cb * decay, xc[:, 64 * r:64 * r + 64]))
    y = y + jnp.concatenate(diag, axis=1)
    yz = y * _silu(z)
    return _rms(yz, nw), new_state


def _per_sequence(fn, n_seq_args, bl):
    def f(*args):
        *ins, state = args
        res = [fn(*[a[b] for a in ins[:n_seq_args]], *ins[n_seq_args:], state[b]) for b in range(bl)]
        return tuple(jnp.concatenate([r[k][None] for r in res]) for k in range(len(res[0])))

    return f


def ssd_op(name, bl, seq, pending=None):
    q = SSD_CHUNK
    nc = seq // q
    blk = lambda w, c0, cw: In((bl, q, w), lambda g, n: (0, n, g), cols=(c0, cw))
    small = lambda g, n: (g, 0, 0)
    ins = [
        blk(256, 0, M_D_INNER),
        blk(256, 0, M_D_INNER),
        blk(128, M_D_INNER, 1024),
        blk(128, M_D_INNER + 1024, 1024),
        In((None, bl, q, 4), lambda g, n: (g, 0, n, 0)),
        In((None, 1, 4), small, 'acc', (1,)),
        In((None, 1, 4), small, 'acc', (1,)),
        In((None, 1, 4), small, 'acc', (1,)),
        In((None, 1, 256), small, 'acc', (1,)),
    ]
    outs = [Out((bl, seq, M_D_INNER), F32, (bl, q, 256), lambda g, n: (0, n, g))]
    return make_op(name, _per_sequence(_ssd_fn, 5, bl), (M_GROUPS, nc), ins, outs,
                   state_shape=(bl, 256, 128), seq_axis=1, pending=pending)


def _gla_fn(layer, qr, fr, ir, gr, lbp, nw, state_t):
    rows = qr.shape[0]
    c = HGRN_CHUNK
    e = jnp.exp(lbp - jnp.max(lbp, axis=0, keepdims=True))
    sm = e / jnp.sum(e, axis=0, keepdims=True)
    lb = jnp.sum(sm[1:layer + 1, :], axis=0, keepdims=True) if layer > 0 else jnp.zeros((1, lbp.shape[1]), F32)
    qq = _silu(qr) * (128 ** -0.5)
    forget = lb + (1.0 - lb) * jax.nn.sigmoid(fr)
    kk = 1.0 - forget
    logf = jnp.log(forget)
    incl, _ = _tri(c)
    tril = incl.astype(F32)
    os_ = []
    for j in range(rows // c):
        sl = slice(c * j, c * j + c)
        gc = cumdot(tril, logf[sl])
        glast = gc[c - 1:c, :]
        q_dec = qq[sl] * jnp.exp(gc)
        k_inv = kk[sl] * jnp.exp(-gc)
        k_end = kk[sl] * jnp.exp(glast - gc)
        att = jnp.where(incl, bdot(q_dec, k_inv, NT), 0.0)
        os_.append(bdot(att, ir[sl]) + bdot(q_dec, state_t, NT))
        state_t = state_t * jnp.exp(glast) + bdot(ir[sl], k_end, TN)
    o = jnp.concatenate(os_, axis=0)
    return _rms(o, nw) * _silu(gr), state_t


def gla_op(name, layer, bl, seq, pending=None):
    r = HGRN_ROWS
    ns = seq // r
    blk = lambda k: In((bl, r, 128), lambda h, n: (0, n, h), cols=(1024 * k, 1024))
    ins = [blk(0), blk(1), blk(2), blk(3),
           In((DEPTH, 128), lambda h, n: (0, h), 'acc', (1,)),
           In((1, 128), lambda h, n: (0, 0), 'acc', (0, 1))]
    outs = [Out((bl, seq, D_MODEL), F32, (bl, r, 128), lambda h, n: (0, n, h))]
    return make_op(name, _per_sequence(functools.partial(_gla_fn, layer), 4, bl), (H_HEADS, ns), ins, outs,
                   state_shape=(bl, 128, 128), seq_axis=1, pending=pending)


def _neumann_inverse(m):
    q = m.shape[1]
    ii = lax.broadcasted_iota(jnp.int32, (q, q), 0)
    jj = lax.broadcasted_iota(jnp.int32, (q, q), 1)
    eye = (ii == jj).astype(F32)[None]
    p = -m
    inv = eye + p
    for _ in range(int(math.log2(q)) - 1):
        p = _h3(p, p, BNN)
        inv = inv + _h3(inv, p, BNN)
    return inv


@jax.custom_vjp
def _unit_lower_inverse(m):
    return _neumann_inverse(m)


def _unit_lower_inverse_fwd(m):
    inv = _neumann_inverse(m)
    return inv, inv


_unit_lower_inverse.defvjp(_unit_lower_inverse_fwd,
                           lambda inv, ct: (-_h3(_h3(inv, ct, BTN), inv, BNT),))


def _gdn_fn(qc, kc, vc, zc, br, ar, alog, dtb, nw, state):
    bl, q = qc.shape[0], qc.shape[1]
    incl, strict = _tri(q)
    tril = incl.astype(F32)
    g = jnp.concatenate([-jnp.exp(alog) * jax.nn.softplus(ar[b] + dtb) for b in range(bl)], axis=1)
    gc = cumdot(tril, g)
    gc_t = cumdot_t(tril, g)
    heads, ms, rhs = [], [], []
    for b in range(bl):
        qn = qc[b] * lax.rsqrt(jnp.sum(qc[b] * qc[b], axis=-1, keepdims=True) + EPS) * (128 ** -0.5)
        kn = kc[b] * lax.rsqrt(jnp.sum(kc[b] * kc[b], axis=-1, keepdims=True) + EPS)
        beta = jax.nn.sigmoid(br[b])
        qk = bdot(qn, kn, NT)
        for j in range(2):
            i = 2 * b + j
            col = gc[:, i:i + 1]
            decay = jnp.exp(jnp.where(incl, col - gc_t[i:i + 1, :], -jnp.inf))
            bj = beta[:, j:j + 1]
            kb = kn * bj
            ms.append(jnp.where(strict, bdot(kb, kn, NT) * decay, 0.0))
            rhs.append(jnp.concatenate([vc[b][:, 128 * j:128 * j + 128] * bj, kb * jnp.exp(col)], axis=1))
            heads.append((qn, kn, qk * decay, col, gc[q - 1:q, i:i + 1]))
    sol = h3dot_b(_unit_lower_inverse(jnp.concatenate([m[None] for m in ms])),
                  jnp.concatenate([r[None] for r in rhs]))
    outs, states = [], []
    for b in range(bl):
        os_, sts = [], []
        for j in range(2):
            i = 2 * b + j
            qn, kn, att, col, glast = heads[i]
            u = sol[i][:, :128]
            w = sol[i][:, 128:]
            st = state[b][128 * j:128 * j + 128, :]
            v_new = u - bdot(w, st)
            o = bdot(qn * jnp.exp(col), st) + bdot(att, v_new)
            sts.append(st * jnp.exp(glast) + bdot(kn * jnp.exp(glast - col), v_new, TN))
            os_.append(_rms(o, nw) * _silu(zc[b][:, 128 * j:128 * j + 128]))
        outs.append(jnp.concatenate(os_, axis=1))
        states.append(jnp.concatenate(sts, axis=0))
    return jnp.concatenate([o[None] for o in outs]), jnp.concatenate([st[None] for st in states])


def gdn_op(name, bl, seq, pending=None):
    q = GDN_CHUNK
    nc = seq // q
    blk = lambda w, c0, cw: In((bl, q, w), lambda h, n: (0, n, h), cols=(c0, cw))
    small = lambda h, n: (h, 0, 0)
    ins = [
        blk(128, 0, G_KEY_DIM),
        blk(128, G_KEY_DIM, G_KEY_DIM),
        blk(256, 2 * G_KEY_DIM, G_VAL_DIM),
        blk(256, G_CONV_DIM, G_VAL_DIM),
        In((None, bl, q, 2), lambda h, n: (h, 0, n, 0)),
        In((None, bl, q, 2), lambda h, n: (h, 0, n, 0)),
        In((None, 1, 2), small, 'acc', (1,)),
        In((None, 1, 2), small, 'acc', (1,)),
        In((1, 128), lambda h, n: (0, 0), 'acc', (0, 1)),
    ]
    outs = [Out((bl, seq, G_VAL_DIM), F32, (bl, q, 256), lambda h, n: (0, n, h))]
    return make_op(name, _gdn_fn, (G_QK_HEADS, nc), ins, outs,
                   state_shape=(bl, 256, 128), seq_axis=1, pending=pending)


def _xattn_fn(q, k, v):
    s = bdot(q, k, NT) * (X_HEAD_DIM ** -0.5)
    s = s - jnp.max(s, axis=-1, keepdims=True)
    p = jnp.exp(s)
    p = p / jnp.sum(p, axis=-1, keepdims=True)
    return (bdot(p, v),)


def xattn_op(name, bl, seq):
    tq = _tile(seq, 512)
    nq = seq // tq
    t = bl * seq
    ins = [
        In((tq, X_HEAD_DIM), lambda b, h, i: (b * nq + i, h)),
        In((N_MEM, X_HEAD_DIM), lambda b, h, i: (b, h), 'acc', (2,), cols=(0, D_MODEL)),
        In((N_MEM, X_HEAD_DIM), lambda b, h, i: (b, h), 'acc', (2,), cols=(D_MODEL, D_MODEL)),
    ]
    outs = [Out((t, D_MODEL), F32, (tq, X_HEAD_DIM), lambda b, h, i: (b * nq + i, h))]
    return make_op(name, _xattn_fn, (bl, X_HEADS, nq), ins, outs)


CONV_PAD = 8
CONV_ROWS = 64


def make_conv(name, bl, seq, width, ch, x_col0, up_col0=None):
    cb = 256
    rt = CONV_ROWS
    assert ch % cb == 0 and x_col0 % cb == 0 and (up_col0 is None or up_col0 % cb == 0) and seq % rt == 0
    nb = ch // cb
    n_tiles = seq // rt
    t = bl * seq
    has_up = up_col0 is not None
    grid = (nb, bl)
    x_spec = pl.BlockSpec((seq, cb), lambda c, b: (b, x_col0 // cb + c))
    up_specs = [pl.BlockSpec((seq, cb), lambda c, b: (b, up_col0 // cb + c))] if has_up else []
    w_spec = pl.BlockSpec((width, cb), lambda c, b: (0, c))
    b_spec = pl.BlockSpec((1, cb), lambda c, b: (0, c))
    o_spec = pl.BlockSpec((seq, cb), lambda c, b: (b, c))
    taps = [CONV_PAD - (width - 1) + j for j in range(width)]

    def window(x_ref, i):
        if isinstance(i, int) and i == 0:
            return jnp.concatenate([jnp.zeros((CONV_PAD, cb), F32), x_ref[0:rt, :]], axis=0)
        return x_ref[pl.ds(pl.multiple_of(i * rt - CONV_PAD, CONV_PAD), rt + CONV_PAD), :]

    def rows(i):
        return pl.ds(i * rt, rt) if isinstance(i, int) else pl.ds(pl.multiple_of(i * rt, rt), rt)

    def shifted(win):
        return [win[tp:tp + rt, :] for tp in taps]

    def pre_activation(views, w, b):
        y = b + w[0:1, :] * views[0]
        for j in range(1, width):
            y = y + w[j:j + 1, :] * views[j]
        return y

    def over_tiles(step, carry):
        carry = step(0, carry)
        return lax.fori_loop(1, n_tiles, step, carry)

    def fwd_call(x, w, b):
        def body(*refs):
            x_ref, w_ref, b_ref = refs[:3]
            o_ref = refs[-1]
            w_, b_ = w_ref[...], b_ref[...]

            def step(i, carry):
                y = _silu(pre_activation(shifted(window(x_ref, i)), w_, b_))
                if has_up:
                    y = y * refs[3][rows(i), :]
                o_ref[rows(i), :] = y
                return carry

            over_tiles(step, 0)

        return pl.pallas_call(
            body, name=name + "_fwd", grid=grid,
            in_specs=[x_spec, w_spec, b_spec] + up_specs, out_specs=o_spec,
            out_shape=jax.ShapeDtypeStruct((t, ch), F32),
            compiler_params=_cparams(),
        )(*([x, w, b] + ([x] if has_up else [])))

    def bwd_call(x, w, b, do):
        n_in = 4 + (1 if has_up else 0)

        def body(*refs):
            x_ref, w_ref, b_ref = refs[:3]
            do_ref = refs[n_in - 1]
            dx_ref, dw_ref, db_ref = refs[n_in:n_in + 3]
            gpad_ref = refs[-1]
            w_, b_ = w_ref[...], b_ref[...]

            def fold(a):
                acc = a[0:8, :]
                for k in range(1, rt // 8):
                    acc = acc + a[8 * k:8 * k + 8, :]
                return acc

            def grad_pre(i, sums):
                views = shifted(window(x_ref, i))
                y = pre_activation(views, w_, b_)
                s = jax.nn.sigmoid(y)
                act = y * s
                do_ = do_ref[rows(i), :]
                if has_up:
                    refs[n_in + 3][rows(i), :] = do_ * act
                    do_ = do_ * refs[3][rows(i), :]
                dy = do_ * (s + act * (1.0 - s))
                gpad_ref[rows(i), :] = dy
                new = [sums[j] + fold(dy * views[j]) for j in range(width)]
                return tuple(new) + (sums[width] + fold(dy),)

            zero8 = jnp.zeros((8, cb), F32)
            sums = over_tiles(grad_pre, (zero8,) * (width + 1))
            gpad_ref[seq:seq + CONV_PAD, :] = jnp.zeros((CONV_PAD, cb), F32)

            def grad_x(i, carry):
                if isinstance(i, int):
                    gwin = gpad_ref[0:rt + CONV_PAD, :]
                else:
                    gwin = gpad_ref[pl.ds(pl.multiple_of(i * rt, rt), rt + CONV_PAD), :]
                dx = w_[0:1, :] * gwin[width - 1:width - 1 + rt, :]
                for j in range(1, width):
                    dx = dx + w_[j:j + 1, :] * gwin[width - 1 - j:width - 1 - j + rt, :]
                dx_ref[rows(i), :] = dx
                return carry

            over_tiles(grad_x, 0)

            @pl.when(pl.program_id(1) == 0)
            def _():
                dw_ref[...] = jnp.zeros_like(dw_ref)
                db_ref[...] = jnp.zeros_like(db_ref)

            dw_ref[...] += jnp.concatenate([jnp.sum(sums[j], axis=0, keepdims=True) for j in range(width)], axis=0)
            db_ref[...] += jnp.sum(sums[width], axis=0, keepdims=True)

        big = jax.ShapeDtypeStruct((t, ch), F32)
        return pl.pallas_call(
            body, name=name + "_bwd", grid=grid,
            in_specs=[x_spec, w_spec, b_spec] + up_specs + [o_spec],
            out_specs=[o_spec, w_spec, b_spec] + ([o_spec] if has_up else []),
            out_shape=[big, jax.ShapeDtypeStruct((width, ch), F32), jax.ShapeDtypeStruct((1, ch), F32)]
            + ([big] if has_up else []),
            scratch_shapes=[pltpu.VMEM((seq + CONV_PAD, cb), F32)],
            compiler_params=_cparams(),
        )(*([x, w, b] + ([x] if has_up else []) + [do]))

    @jax.custom_vjp
    def conv(x, w, b):
        return fwd_call(x, w, b)

    def conv_fwd(x, w, b):
        return fwd_call(x, w, b), (x, w, b)

    def conv_bwd(res, do):
        x, w, b = res
        got = bwd_call(x, w, b, do)
        dx = jnp.pad(got[0], ((0, 0), (x_col0, x.shape[1] - x_col0 - ch)))
        if has_up:
            dx = dx + jnp.pad(got[3], ((0, 0), (up_col0, x.shape[1] - up_col0 - ch)))
        return dx, got[1], got[2]

    conv.defvjp(conv_fwd, conv_bwd)

    def apply(x, w, b=None):
        if b is None:
            b = jnp.zeros((ch,), F32)
        return conv(x, w, b.reshape(1, ch))

    return apply


def loss_head(x, w, target):
    t = x.shape[0]
    tm = _tile(t, 512)

    def fn(xb, wb, tb):
        err = _rms(xb, wb) - tb
        return 0.5 * jnp.sum(err * err) * (1.0 / D_MODEL)

    def body(x_ref, w_ref, t_ref, loss_ref, dx_ref, dw_ref):
        @pl.when(pl.program_id(0) == 0)
        def _():
            loss_ref[...] = jnp.zeros_like(loss_ref)
            dw_ref[...] = jnp.zeros_like(dw_ref)

        tb = t_ref[...]
        val, vjp = jax.vjp(lambda a, b: fn(a, b, tb), x_ref[...], w_ref[...])
        dx, dw = vjp(jnp.ones((), F32))
        dx_ref[...] = dx
        dw_ref[...] += dw
        loss_ref[...] += jnp.full(loss_ref.shape, val, F32)

    row = pl.BlockSpec((tm, D_MODEL), lambda i: (i, 0))
    vec = pl.BlockSpec((1, D_MODEL), lambda i: (0, 0))
    loss, dx, dw = pl.pallas_call(
        body, name="loss_head", grid=(t // tm,),
        in_specs=[row, vec, row],
        out_specs=[pl.BlockSpec((8, LANE), lambda i: (0, 0)), row, vec],
        out_shape=[jax.ShapeDtypeStruct((8, LANE), F32), jax.ShapeDtypeStruct((t, D_MODEL), F32),
                   jax.ShapeDtypeStruct((1, D_MODEL), F32)],
        compiler_params=_cparams(),
    )(x, w.reshape(1, D_MODEL), target)
    return loss[0, 0], dx, dw.reshape(D_MODEL)


PACK_W = 1024
ADAM_BLOCK_BYTES = 512 * 1024


def _rows_tile(r, c):
    if r * c * 4 <= ADAM_BLOCK_BYTES or r % 8:
        return r
    best = 8
    for t in range(8, r + 1, 8):
        if r % t == 0 and t * c * 4 <= ADAM_BLOCK_BYTES:
            best = t
    return best


def reduce_adamw(slots, w, m, v, name):
    r, wd = w.shape
    tr = _rows_tile(r, wd)
    c1 = 1.0 - ADAM_B1 ** ADAM_STEP
    c2 = 1.0 - ADAM_B2 ** ADAM_STEP

    def body(s_ref, w_ref, m_ref, v_ref, g_ref, d_ref, nm_ref, nv_ref):
        g = s_ref[0].astype(F32)
        for k in range(1, N_DEV):
            g = g + s_ref[k].astype(F32)
        nm = ADAM_B1 * m_ref[...] + (1.0 - ADAM_B1) * g
        nv = ADAM_B2 * v_ref[...] + (1.0 - ADAM_B2) * (g * g)
        m_hat = nm / c1
        v_hat = nv / c2
        d_ref[...] = -ADAM_LR * (m_hat / (jnp.sqrt(v_hat) + ADAM_EPS) + ADAM_WD * w_ref[...])
        g_ref[...] = g
        nm_ref[...] = nm
        nv_ref[...] = nv

    blk = pl.BlockSpec((tr, wd), lambda i: (i, 0))
    shp = jax.ShapeDtypeStruct((r, wd), F32)
    return pl.pallas_call(
        body, name=name, grid=(r // tr,),
        in_specs=[pl.BlockSpec((N_DEV, tr, wd), lambda i: (0, i, 0)), blk, blk, blk],
        out_specs=[blk, blk, blk, blk], out_shape=[shp, shp, shp, shp],
        compiler_params=_cparams(),
    )(slots, w, m, v)


def all_gather(block, name):
    def body(x_ref, out_ref, send_sems, recv_sems, local_sem):
        x, y, c = _position()
        me, sibling = (x, y, c), (x, y, 1 - c)
        chips = [(1 - x, y), (x, 1 - y), (1 - x, 1 - y)]

        def slot(px, py, pc):
            return out_ref.at[4 * px + 2 * py + pc]

        def copy(k, owner, to, src=None):
            return pltpu.make_async_remote_copy(
                src_ref=slot(*owner) if src is None else src, dst_ref=slot(*owner),
                send_sem=send_sems.at[k], recv_sem=recv_sems.at[k],
                device_id=to, device_id_type=pl.DeviceIdType.MESH)

        mine = pltpu.make_async_copy(x_ref, slot(*me), local_sem)
        mine.start()
        first = [copy(0, me, sibling, src=x_ref)]
        first += [copy(1 + j, me, (*chip, c), src=x_ref) for j, chip in enumerate(chips)]
        for cp in first:
            cp.start()
        passed = [copy(4 + j, (*chip, c), sibling) for j, chip in enumerate(chips)]
        for j, chip in enumerate(chips):
            copy(1 + j, (*chip, c), me).wait_recv()
            passed[j].start()
        copy(0, sibling, me).wait_recv()
        for j, chip in enumerate(chips):
            copy(4 + j, (*chip, 1 - c), me).wait_recv()
        for cp in first + passed:
            cp.wait_send()
        mine.wait()

    return pl.pallas_call(
        body, name=name,
        out_shape=jax.ShapeDtypeStruct((N_DEV,) + block.shape, block.dtype),
        in_specs=[pl.BlockSpec(memory_space=pl.ANY)],
        out_specs=pl.BlockSpec(memory_space=pl.ANY),
        scratch_shapes=[pltpu.SemaphoreType.DMA((7,)), pltpu.SemaphoreType.DMA((7,)), pltpu.SemaphoreType.DMA],
    )(block)


def exchange_slabs(slabs, name):
    def body(in_ref, out_ref, send_sems, recv_sems, local_sem):
        x, y, c = _position()
        my = 4 * x + 2 * y + c
        mine = pltpu.make_async_copy(in_ref.at[my], out_ref.at[my], local_sem)
        mine.start()
        copies = []
        for k in range(1, N_DEV):
            dx, dy, dc = (k >> 2) & 1, (k >> 1) & 1, k & 1
            px = x if dx == 0 else 1 - x
            py = y if dy == 0 else 1 - y
            pc = c if dc == 0 else 1 - c
            cp = pltpu.make_async_remote_copy(
                src_ref=in_ref.at[4 * px + 2 * py + pc], dst_ref=out_ref.at[my],
                send_sem=send_sems.at[k - 1], recv_sem=recv_sems.at[k - 1],
                device_id=(px, py, pc), device_id_type=pl.DeviceIdType.MESH)
            cp.start()
            copies.append(cp)
        for cp in copies:
            cp.wait()
        mine.wait()

    return pl.pallas_call(
        body, name=name,
        out_shape=jax.ShapeDtypeStruct(slabs.shape, slabs.dtype),
        in_specs=[pl.BlockSpec(memory_space=pl.ANY)],
        out_specs=pl.BlockSpec(memory_space=pl.ANY),
        scratch_shapes=[pltpu.SemaphoreType.DMA((7,)), pltpu.SemaphoreType.DMA((7,)), pltpu.SemaphoreType.DMA],
    )(slabs)


def _pack(arrays, dtype, row_multiple):
    flat = jnp.concatenate([a.astype(dtype).reshape(-1) for a in arrays])
    n = flat.shape[0]
    per = PACK_W * row_multiple
    total = -(-n // per) * per
    flat = jnp.pad(flat, (0, total - n))
    return flat.reshape(total // PACK_W, PACK_W)


def _unpack(flat2d, shapes, lead=()):
    flat = flat2d.reshape(lead + (-1,))
    out, off = [], 0
    for shp in shapes:
        n = math.prod(shp)
        out.append(flat[..., off:off + n].reshape(lead + tuple(shp)))
        off += n
    return out


def _full_from_gathered(g, axis):
    g = jnp.moveaxis(g, 0, axis)
    shp = list(g.shape)
    shp[axis:axis + 2] = [shp[axis] * shp[axis + 1]]
    return g.reshape(shp)


def _shards_of_full(full, axis):
    shp = list(full.shape)
    shp[axis:axis + 1] = [N_DEV, shp[axis] // N_DEV]
    return jnp.moveaxis(full.reshape(shp), axis, 0)


def layer_units(i):
    mixer = [('m_in_w', 'm_out_w'), ('h_in_w', 'h_out_w'), ('g_in_w', 'g_out_w')][i % 3]
    return [(mixer[0], i // 3), (mixer[1], i // 3), ('xa_q', i), ('xa_kv', i), ('xa_o', i), ('f_up', i), ('f_down', i)]


PADDED_COLS = {'m_in_w': M_IN_PAD, 'g_in_w': G_IN_PAD}


def whole_weight(name, gathered):
    w = _full_from_gathered(lax.stop_gradient(gathered), SHARD_AXIS[name] - 1)
    return _pad_cols(w, 1, PADDED_COLS[name]) if name in PADDED_COLS else w


def _trunk(p, weights, blocks, standins, x, mem, bl, seq, pending=None):
    t = bl * seq
    ia = ib = ic = 0
    weights = dict(weights)
    state = {}

    def lin(name, a, wname, idx, residual=None):
        unit = (wname, idx)
        pos = state['units'].index(unit)
        later = state['next'][pos] if state['next'] else None
        nxt = (blocks[later],) if later in blocks else ()
        n_real = N_DEV * standins[unit].shape[2]
        res = () if residual is None else (residual,)
        mixer_in = wname in ('m_in_w', 'h_in_w', 'g_in_w')
        defer = (pending, unit) if pending is not None and not (mixer_in and i == 0) else None
        y, got = make_linear(name, SHARD_AXIS[wname] - 1, n_real, bool(res), defer)(
            a, weights[unit], standins[unit], nxt, res)
        if nxt:
            weights[later] = whole_weight(later[0], got[0])
        return y

    by_seq = lambda a: a.reshape(bl, seq, a.shape[-1])

    for i in range(DEPTH):
        state['units'] = layer_units(i)
        state['next'] = layer_units(i + 1) if i + 1 < DEPTH else None
        hn, x = rmsnorm_op(f"ln_mix{i}", t, F32, residual=True)(x, p['ln_mix'][i:i + 1])
        kind = i % 3
        if kind == 0:
            proj = lin(f"m_in{i}", hn, 'm_in_w', ia)
            xbc = make_conv(f"m_conv{i}", bl, seq, 4, M_CONV_DIM, M_D_INNER)(
                proj, p['m_conv_w'][ia], p['m_conv_b'][ia])
            dt = proj[:, M_D_INNER + M_CONV_DIM:M_IN].reshape(bl, seq, M_GROUPS, 4).transpose(2, 0, 1, 3)
            grp = lambda a, n=4: a.reshape(M_GROUPS, 1, n)
            proj3, xbc3 = by_seq(proj), by_seq(xbc)
            y = ssd_op(f"ssd{i}", bl, seq, pending)(
                proj3, xbc3, xbc3, xbc3, dt, grp(p['m_dt_bias'][ia]), grp(p['m_a_log'][ia]), grp(p['m_d'][ia]),
                grp(p['m_norm_w'][ia], 256))[0]
            x = lin(f"m_out{i}", y.reshape(t, M_D_INNER), 'm_out_w', ia, residual=x)
            ia += 1
        elif kind == 1:
            proj3 = by_seq(lin(f"h_in{i}", hn, 'h_in_w', ib))
            y = gla_op(f"gla{i}", i, bl, seq, pending)(
                proj3, proj3, proj3, proj3, p['h_lower_bounds'], p['h_norm_w'][ib:ib + 1])[0]
            x = lin(f"h_out{i}", y.reshape(t, D_MODEL), 'h_out_w', ib, residual=x)
            ib += 1
        else:
            proj = lin(f"g_in{i}", hn, 'g_in_w', ic)
            qkv = make_conv(f"g_conv{i}", bl, seq, 4, G_CONV_DIM, 0)(proj, p['g_conv_w'][ic])
            c0 = G_CONV_DIM + G_VAL_DIM
            heads = lambda a: a.reshape(bl, seq, G_QK_HEADS, 2).transpose(2, 0, 1, 3)
            braw = heads(proj[:, c0:c0 + G_V_HEADS])
            araw = heads(proj[:, c0 + G_V_HEADS:c0 + 2 * G_V_HEADS])
            grp = lambda a: a.reshape(G_QK_HEADS, 1, 2)
            qkv3 = by_seq(qkv)
            y = gdn_op(f"gdn{i}", bl, seq, pending)(
                qkv3, qkv3, qkv3, by_seq(proj), braw, araw, grp(p['g_a_log'][ic]), grp(p['g_dt_bias'][ic]),
                p['g_norm_w'][ic:ic + 1])[0]
            x = lin(f"g_out{i}", y.reshape(t, G_VAL_DIM), 'g_out_w', ic, residual=x)
            ic += 1
        hq, x = rmsnorm_op(f"ln_xattn{i}", t, F32, residual=True)(x, p['ln_xattn'][i:i + 1])
        mn = rmsnorm_op(f"ln_mem{i}", bl * N_MEM, F32)(mem, p['ln_mem'][i:i + 1])[0]
        qx = lin(f"xa_q{i}", hq, 'xa_q', i)
        kv = lin(f"xa_kv{i}", mn, 'xa_kv', i)
        ao = xattn_op(f"xattn{i}", bl, seq)(qx, kv, kv)[0]
        x = lin(f"xa_o{i}", ao, 'xa_o', i, residual=x)
        hf, x = rmsnorm_op(f"ln_ffn{i}", t, F32, residual=True)(x, p['ln_ffn'][i:i + 1])
        up = lin(f"f_up{i}", hf, 'f_up', i)
        act = make_conv(f"f_conv{i}", bl, seq, 3, D_FF, 0, up_col0=D_FF)(up, p['f_conv_w'][i], p['f_conv_b'][i])
        x = lin(f"f_down{i}", act, 'f_down', i, residual=x)
    return x


def _pad_cols(w, axis, to):
    pad = [(0, 0)] * w.ndim
    pad[axis] = (0, to - w.shape[axis])
    return jnp.pad(w, pad)


def kernel(x, mem, ln_mix, ln_xattn, ln_mem, ln_ffn, final_norm, m_in_w, m_conv_w, m_conv_b, m_dt_bias, m_a_log, m_d, m_norm_w, m_out_w, h_in_w, h_lower_bounds, h_norm_w, h_out_w, g_in_w, g_conv_w, g_a_log, g_dt_bias, g_norm_w, g_out_w, xa_q, xa_kv, xa_o, f_up, f_conv_w, f_conv_b, f_down, loss_target, m_ln_mix, m_ln_xattn, m_ln_mem, m_ln_ffn, m_final_norm, m_m_in_w, m_m_conv_w, m_m_conv_b, m_m_dt_bias, m_m_a_log, m_m_d, m_m_norm_w, m_m_out_w, m_h_in_w, m_h_lower_bounds, m_h_norm_w, m_h_out_w, m_g_in_w, m_g_conv_w, m_g_a_log, m_g_dt_bias, m_g_norm_w, m_g_out_w, m_xa_q, m_xa_kv, m_xa_o, m_f_up, m_f_conv_w, m_f_conv_b, m_f_down, v_ln_mix, v_ln_xattn, v_ln_mem, v_ln_ffn, v_final_norm, v_m_in_w, v_m_conv_w, v_m_conv_b, v_m_dt_bias, v_m_a_log, v_m_d, v_m_norm_w, v_m_out_w, v_h_in_w, v_h_lower_bounds, v_h_norm_w, v_h_out_w, v_g_in_w, v_g_conv_w, v_g_a_log, v_g_dt_bias, v_g_norm_w, v_g_out_w, v_xa_q, v_xa_kv, v_xa_o, v_f_up, v_f_conv_w, v_f_conv_b, v_f_down):
    local = dict(ln_mix=ln_mix, ln_xattn=ln_xattn, ln_mem=ln_mem, ln_ffn=ln_ffn, final_norm=final_norm, m_in_w=m_in_w, m_conv_w=m_conv_w, m_conv_b=m_conv_b, m_dt_bias=m_dt_bias, m_a_log=m_a_log, m_d=m_d, m_norm_w=m_norm_w, m_out_w=m_out_w, h_in_w=h_in_w, h_lower_bounds=h_lower_bounds, h_norm_w=h_norm_w, h_out_w=h_out_w, g_in_w=g_in_w, g_conv_w=g_conv_w, g_a_log=g_a_log, g_dt_bias=g_dt_bias, g_norm_w=g_norm_w, g_out_w=g_out_w, xa_q=xa_q, xa_kv=xa_kv, xa_o=xa_o, f_up=f_up, f_conv_w=f_conv_w, f_conv_b=f_conv_b, f_down=f_down)
    mom_m = dict(ln_mix=m_ln_mix, ln_xattn=m_ln_xattn, ln_mem=m_ln_mem, ln_ffn=m_ln_ffn, final_norm=m_final_norm, m_in_w=m_m_in_w, m_conv_w=m_m_conv_w, m_conv_b=m_m_conv_b, m_dt_bias=m_m_dt_bias, m_a_log=m_m_a_log, m_d=m_m_d, m_norm_w=m_m_norm_w, m_out_w=m_m_out_w, h_in_w=m_h_in_w, h_lower_bounds=m_h_lower_bounds, h_norm_w=m_h_norm_w, h_out_w=m_h_out_w, g_in_w=m_g_in_w, g_conv_w=m_g_conv_w, g_a_log=m_g_a_log, g_dt_bias=m_g_dt_bias, g_norm_w=m_g_norm_w, g_out_w=m_g_out_w, xa_q=m_xa_q, xa_kv=m_xa_kv, xa_o=m_xa_o, f_up=m_f_up, f_conv_w=m_f_conv_w, f_conv_b=m_f_conv_b, f_down=m_f_down)
    mom_v = dict(ln_mix=v_ln_mix, ln_xattn=v_ln_xattn, ln_mem=v_ln_mem, ln_ffn=v_ln_ffn, final_norm=v_final_norm, m_in_w=v_m_in_w, m_conv_w=v_m_conv_w, m_conv_b=v_m_conv_b, m_dt_bias=v_m_dt_bias, m_a_log=v_m_a_log, m_d=v_m_d, m_norm_w=v_m_norm_w, m_out_w=v_m_out_w, h_in_w=v_h_in_w, h_lower_bounds=v_h_lower_bounds, h_norm_w=v_h_norm_w, h_out_w=v_h_out_w, g_in_w=v_g_in_w, g_conv_w=v_g_conv_w, g_a_log=v_g_a_log, g_dt_bias=v_g_dt_bias, g_norm_w=v_g_norm_w, g_out_w=v_g_out_w, xa_q=v_xa_q, xa_kv=v_xa_kv, xa_o=v_xa_o, f_up=v_f_up, f_conv_w=v_f_conv_w, f_conv_b=v_f_conv_b, f_down=v_f_down)

    bl, seq, _ = x.shape
    t = bl * seq

    p = {n: local[n] for n in WEIGHTS if n not in SHARD_AXIS}
    for n in SMALL_SHARDED:
        p[n] = _full_from_gathered(all_gather(local[n], f"gather_{n}"), SHARD_AXIS[n])
    units = [(n, l) for n in MATMUL_WEIGHTS for l in range(local[n].shape[0])]
    block = lambda u: local[u[0]][u[1]].astype(BF16)
    weights = {u: whole_weight(u[0], all_gather(block(u), f"gather_{u[0]}{u[1]}")) for u in layer_units(0)}
    blocks = {u: block(u) for u in units if u not in weights}
    standins = {u: jnp.zeros((N_DEV,) + local[u[0]].shape[1:], BF16) for u in units}
    small = {n: p[n] for n in WEIGHTS if n not in MATMUL_WEIGHTS and n != 'final_norm'}

    pending = Pending()

    def run(small_w, standins_, xin):
        return _trunk(small_w, weights, blocks, standins_, xin, mem.reshape(bl * N_MEM, D_MODEL), bl, seq, pending)

    x_out, vjp = jax.vjp(run, small, standins, x.reshape(t, D_MODEL))
    loss_part, dx_out, d_final = loss_head(x_out, final_norm, loss_target.reshape(t, D_MODEL))
    grads, received, dx = vjp(dx_out)
    received = dict(received)
    for unit, slabs in pending.take_all():
        pending.received[unit] = exchange_slabs(slabs, f"exchange_{unit[0]}{unit[1]}")
    received.update(pending.received)
    grads = dict(grads)
    grads['final_norm'] = d_final
    loss = lax.psum(loss_part, ("x", "y", "c"))

    outs = {}

    def update(name, n, slots, shape, sel=lambda a: a):
        two_d = lambda a: sel(a).reshape(slots.shape[1:])
        got = reduce_adamw(slots, two_d(local[n]), two_d(mom_m[n]), two_d(mom_v[n]), name)
        return [g.reshape(shape) for g in got]

    for n in SMALL_SHARDED:
        slots = exchange_slabs(_shards_of_full(grads[n], SHARD_AXIS[n]), f"exchange_{n}")
        slots = slots.reshape(N_DEV, -1, slots.shape[-1])
        for kind, a in zip(KINDS, update(f"adamw_{n}", n, slots, local[n].shape)):
            outs[kind, n] = a
    for n in MATMUL_WEIGHTS:
        per_layer = [update(f"adamw_{n}{l}", n, received[n, l], local[n].shape[1:], lambda a, l=l: a[l])
                     for l in range(local[n].shape[0])]
        for k, kind in enumerate(KINDS):
            outs[kind, n] = jnp.concatenate([got[k][None] for got in per_layer])
    replicated = [n for n in WEIGHTS if n not in SHARD_AXIS]
    pk = lambda d: _pack([d[n] for n in replicated], F32, 8)
    got = reduce_adamw(all_gather(pk(grads), "gather_replicated_grads"), pk(local), pk(mom_m), pk(mom_v),
                       "adamw_replicated")
    shapes = [local[n].shape for n in replicated]
    for kind, buf in zip(KINDS, got):
        for n, a in zip(replicated, _unpack(buf, shapes)):
            outs[kind, n] = a
    result = [loss, dx.reshape(bl, seq, D_MODEL)]
    for kind in KINDS:
        result += [outs[kind, n] for n in WEIGHTS]
    return tuple(result)
```

```python
import functools
import math

import jax
import jax.numpy as jnp
from jax import lax
from jax.experimental import pallas as pl
from jax.experimental.pallas import tpu as pltpu

F32 = jnp.float32
BF16 = jnp.bfloat16
NN = (((1,), (0,)), ((), ()))
NT = (((1,), (1,)), ((), ()))
TN = (((0,), (0,)), ((), ()))

D_MODEL = 1024
DEPTH = 4
EPS = 1e-6
N_MEM = 256
M_D_INNER = 2048
M_HEADS = 32
M_GROUPS = 8
M_STATE = 128
M_CONV_DIM = 4096
M_IN = 6176
M_IN_PAD = 6272
SSD_CHUNK = 256
H_HEADS = 8
HGRN_CHUNK = 32
HGRN_ROWS = 128
G_QK_HEADS = 8
G_V_HEADS = 16
G_KEY_DIM = 1024
G_VAL_DIM = 2048
G_CONV_DIM = 4096
G_IN = 6176
G_IN_PAD = 6272
GDN_CHUNK = 64
X_HEADS = 4
X_HEAD_DIM = 256
D_FF = 2816
ADAM_LR = 0.001
ADAM_B1 = 0.9
ADAM_B2 = 0.999
ADAM_EPS = 1e-08
ADAM_WD = 0.01
ADAM_STEP = 10

N_DEV = 8
LANE = 128
KINDS = ('grad', 'delta', 'new_m', 'new_v')
VMEM_LIMIT = 56 * 1024 * 1024

WEIGHTS = ['ln_mix', 'ln_xattn', 'ln_mem', 'ln_ffn', 'final_norm', 'm_in_w', 'm_conv_w', 'm_conv_b', 'm_dt_bias',
           'm_a_log', 'm_d', 'm_norm_w', 'm_out_w', 'h_in_w', 'h_lower_bounds', 'h_norm_w', 'h_out_w', 'g_in_w',
           'g_conv_w', 'g_a_log', 'g_dt_bias', 'g_norm_w', 'g_out_w', 'xa_q', 'xa_kv', 'xa_o', 'f_up', 'f_conv_w',
           'f_conv_b', 'f_down']
SHARD_AXIS = {'m_in_w': 2, 'm_conv_w': 2, 'm_conv_b': 1, 'm_norm_w': 1, 'm_out_w': 1, 'h_in_w': 2, 'h_out_w': 1,
              'g_in_w': 2, 'g_conv_w': 2, 'g_out_w': 1, 'xa_q': 1, 'xa_kv': 2, 'xa_o': 1, 'f_up': 2, 'f_conv_w': 2,
              'f_down': 1}
MATMUL_WEIGHTS = ['m_in_w', 'm_out_w', 'h_in_w', 'h_out_w', 'g_in_w', 'g_out_w', 'xa_q', 'xa_kv', 'xa_o', 'f_up',
                  'f_down']
SMALL_SHARDED = ['m_conv_w', 'm_conv_b', 'm_norm_w', 'g_conv_w', 'f_conv_w']


def _cparams():
    return pltpu.CompilerParams(vmem_limit_bytes=VMEM_LIMIT)


def bdot(a, b, dims=NN):
    return lax.dot_general(a.astype(BF16), b.astype(BF16), dims, preferred_element_type=F32)


def _split(a):
    hi = a.astype(BF16)
    return hi, (a - hi.astype(F32)).astype(BF16)


def _h3(a, b, dims):
    ah, al = _split(a)
    bh, bl = _split(b)
    d = functools.partial(lax.dot_general, dimension_numbers=dims, preferred_element_type=F32)
    return d(ah, bh) + (d(ah, bl) + d(al, bh))


BNN = (((2,), (1,)), ((0,), (0,)))
BNT = (((2,), (2,)), ((0,), (0,)))
BTN = (((1,), (1,)), ((0,), (0,)))


@jax.custom_vjp
def h3dot_b(a, b):
    return _h3(a, b, BNN)


h3dot_b.defvjp(lambda a, b: (_h3(a, b, BNN), (a, b)),
               lambda res, ct: (_h3(ct, res[1], BNT), _h3(res[0], ct, BTN)))

T_ROWS = (((0,), (1,)), ((), ()))


def _tri_times(tri, x, dims, tri_first):
    t = tri.astype(BF16)
    x0 = x.astype(BF16)
    r1 = x - x0.astype(F32)
    x1 = r1.astype(BF16)
    x2 = (r1 - x1.astype(F32)).astype(BF16)
    if tri_first:
        d = lambda xx: lax.dot_general(t, xx, dims, preferred_element_type=F32)
    else:
        d = lambda xx: lax.dot_general(xx, t, dims, preferred_element_type=F32)
    return d(x0) + (d(x1) + d(x2))


@jax.custom_vjp
def cumdot(tri, x):
    return _tri_times(tri, x, NN, True)


cumdot.defvjp(lambda tri, x: (_tri_times(tri, x, NN, True), tri),
              lambda tri, ct: (jnp.zeros_like(tri), _tri_times(tri, ct, TN, True)))


@jax.custom_vjp
def cumdot_t(tri, x):
    return _tri_times(tri, x, T_ROWS, False)


cumdot_t.defvjp(lambda tri, x: (_tri_times(tri, x, T_ROWS, False), tri),
                lambda tri, ct: (jnp.zeros_like(tri), _tri_times(tri, ct, T_ROWS, True)))


def _tile(dim, cap):
    if dim <= cap:
        return dim
    best = None
    for t in range(LANE, cap + 1, LANE):
        if dim % t == 0:
            best = t
    assert best is not None, dim
    return best


def _position():
    return lax.axis_index("x"), lax.axis_index("y"), lax.axis_index("c")


def _direct_copies(kind, src_ref, dst_ref, send_sems, recv_sems, local_sem):
    x, y, c = _position()
    me = 4 * x + 2 * y + c
    local_src = src_ref if kind == 'gather' else src_ref.at[me]
    copies = [pltpu.make_async_copy(local_src, dst_ref.at[me], local_sem)]
    for k in range(1, N_DEV):
        px = 1 - x if (k >> 2) & 1 else x
        py = 1 - y if (k >> 1) & 1 else y
        pc = 1 - c if k & 1 else c
        copies.append(pltpu.make_async_remote_copy(
            src_ref=src_ref if kind == 'gather' else src_ref.at[4 * px + 2 * py + pc], dst_ref=dst_ref.at[me],
            send_sem=send_sems.at[k - 1], recv_sem=recv_sems.at[k - 1],
            device_id=(px, py, pc), device_id_type=pl.DeviceIdType.MESH))
    return copies


COMM_SCRATCH = [pltpu.SemaphoreType.DMA((N_DEV - 1,)), pltpu.SemaphoreType.DMA((N_DEV - 1,)), pltpu.SemaphoreType.DMA]


class Pending:
    def __init__(self):
        self.jobs, self.received = [], {}

    def take_all(self):
        jobs, self.jobs = self.jobs, []
        return jobs


def matmul(a, b, *, ta=False, tb=False, out_dtype=F32, name="mm", carry=None, residual=None):
    if ta:
        k, m = a.shape
    else:
        m, k = a.shape
    if tb:
        n, k2 = b.shape
    else:
        k2, n = b.shape
    assert k == k2, (a.shape, b.shape, ta, tb)
    tm = _tile(m, 1024)
    tn = _tile(n, 1408)
    tk = _tile(k, 1408)
    grid = (m // tm, n // tn, k // tk)
    nk = grid[2]
    dims = (((0 if ta else 1,), (1 if tb else 0,)), ((), ()))

    def at_step(which):
        conds = [pl.program_id(ax) == (0 if which == 'first' else grid[ax] - 1) for ax in range(3)]
        return jnp.logical_and(jnp.logical_and(conds[0], conds[1]), conds[2])

    def body(*refs):
        r_ref = None
        if residual is not None:
            r_ref, refs = refs[2], refs[:2] + refs[3:]
        if carry is None:
            a_ref, b_ref, o_ref, acc_ref = refs
        else:
            a_ref, b_ref, src_ref, o_ref, dst_ref, acc_ref, send_sems, recv_sems, local_sem = refs
            copies = lambda: _direct_copies(carry[0], src_ref, dst_ref, send_sems, recv_sems, local_sem)

            @pl.when(at_step('first'))
            def _():
                for cp in copies():
                    cp.start()

        @pl.when(pl.program_id(2) == 0)
        def _():
            acc_ref[...] = jnp.zeros_like(acc_ref)

        acc_ref[...] += lax.dot_general(a_ref[...].astype(BF16), b_ref[...].astype(BF16), dims,
                                        preferred_element_type=F32)

        @pl.when(pl.program_id(2) == nk - 1)
        def _():
            out = acc_ref[...] if r_ref is None else acc_ref[...] + r_ref[...]
            o_ref[...] = out.astype(o_ref.dtype)

        if carry is not None:
            @pl.when(at_step('last'))
            def _():
                for cp in copies():
                    cp.wait()

    a_spec = pl.BlockSpec((tk, tm), lambda i, j, kk: (kk, i)) if ta else pl.BlockSpec((tm, tk), lambda i, j, kk: (i, kk))
    b_spec = pl.BlockSpec((tn, tk), lambda i, j, kk: (j, kk)) if tb else pl.BlockSpec((tk, tn), lambda i, j, kk: (kk, j))
    o_spec = pl.BlockSpec((tm, tn), lambda i, j, kk: (i, j))
    o_shape = jax.ShapeDtypeStruct((m, n), out_dtype)
    acc = pltpu.VMEM((tm, tn), F32)
    ins, in_specs = [a, b], [a_spec, b_spec]
    if residual is not None:
        ins.append(residual)
        in_specs.append(o_spec)
    if carry is None:
        return pl.pallas_call(
            body, name=name, grid=grid, in_specs=in_specs, out_specs=o_spec, out_shape=o_shape,
            scratch_shapes=[acc], compiler_params=_cparams(),
        )(*ins)
    kind, src = carry
    got = jax.ShapeDtypeStruct(((N_DEV,) + src.shape) if kind == 'gather' else src.shape, src.dtype)
    hbm = pl.BlockSpec(memory_space=pl.ANY)
    return pl.pallas_call(
        body, name=name, grid=grid, in_specs=in_specs + [hbm], out_specs=[o_spec, hbm],
        out_shape=[o_shape, got], scratch_shapes=[acc] + COMM_SCRATCH, compiler_params=_cparams(),
    )(*ins, src)


def make_linear(name, shard_axis, n_real, has_res=False, defer=None):
    def forward(a, w, nxt, res):
        r = res[0] if res else None
        if nxt:
            y, got = matmul(a, w, name=name + "_fwd", carry=('gather', nxt[0]), residual=r)
            return y, (got,)
        return matmul(a, w, name=name + "_fwd", residual=r), ()

    @jax.custom_vjp
    def linear(a, w, wg, nxt, res):
        return forward(a, w, nxt, res)

    def fwd(a, w, wg, nxt, res):
        return forward(a, w, nxt, res), (a, w, nxt)

    def bwd(saved, cts):
        a, w, nxt = saved
        dy = cts[0]
        dw = matmul(a, dy, ta=True, out_dtype=BF16, name=name + "_bwd_dw")
        slabs = _shards_of_full(dw[:, :n_real], shard_axis)
        if defer is None:
            da, slots = matmul(dy, w, tb=True, out_dtype=a.dtype, name=name + "_bwd_da", carry=('exchange', slabs))
        else:
            defer[0].jobs.append((defer[1], slabs))
            da, slots = matmul(dy, w, tb=True, out_dtype=a.dtype, name=name + "_bwd_da"), jnp.zeros_like(slabs)
        return da, jnp.zeros_like(w), slots, tuple(jnp.zeros_like(b) for b in nxt), ((dy,) if has_res else ())

    linear.defvjp(fwd, bwd)
    return linear


class In:
    def __init__(self, block, imap, kind='blk', inner=(), cols=None):
        self.block, self.imap, self.kind, self.inner, self.cols = block, imap, kind, inner, cols


class Out:
    def __init__(self, shape, dtype, block, imap):
        self.shape, self.dtype, self.block, self.imap = shape, dtype, block, imap


def make_op(name, fn, grid, ins, outs, state_shape=None, seq_axis=None, passthrough=(), pending=None, n_gather=0):
    n_in, n_out = len(ins), len(outs)
    has_state = state_shape is not None
    nd = len(grid)
    diff_idx = [i for i, s in enumerate(ins) if s.kind != 'const']

    def in_spec(s, reverse):
        off = 0
        if s.cols is not None:
            assert s.cols[0] % s.block[-1] == 0
            off = s.cols[0] // s.block[-1]

        def imap(*ids):
            ids = rev(ids) if reverse else ids
            idx = tuple(s.imap(*ids))
            return idx[:-1] + (idx[-1] + off,) if off else idx

        return pl.BlockSpec(s.block, imap)

    def rel_spec(block, f, reverse):
        return pl.BlockSpec(block, (lambda *ids: f(*rev(ids))) if reverse else f)

    def rev(ids):
        if not has_state:
            return ids
        ids = list(ids)
        ids[seq_axis] = grid[seq_axis] - 1 - ids[seq_axis]
        return tuple(ids)

    save_shape = tuple(grid) + tuple(state_shape) if has_state else None
    save_block = (None,) * nd + tuple(state_shape) if has_state else None

    def save_imap(*ids):
        return tuple(ids) + (0,) * len(state_shape)

    def step_is(which):
        conds = [pl.program_id(ax) == (0 if which == 'first' else grid[ax] - 1) for ax in range(nd)]
        return functools.reduce(jnp.logical_and, conds)

    def fwd_call(*xs):
        xs, blocks = xs[:n_in], xs[n_in:]
        n_save = 1 if has_state else 0

        def body(*refs):
            if blocks:
                src_refs = refs[n_in:n_in + n_gather]
                dst_refs = refs[n_in + n_gather + n_out + n_save:n_in + 2 * n_gather + n_out + n_save]
                sems = refs[len(refs) - 3 * n_gather:]
                refs = refs[:n_in] + refs[n_in + n_gather:n_in + n_gather + n_out + n_save] + \
                    refs[n_in + 2 * n_gather + n_out + n_save:len(refs) - 3 * n_gather]
                copies = lambda: [cp for k in range(n_gather) for cp in _direct_copies(
                    'gather', src_refs[k], dst_refs[k], sems[3 * k], sems[3 * k + 1], sems[3 * k + 2])]

                @pl.when(step_is('first'))
                def _():
                    for cp in copies():
                        cp.start()

            in_refs = refs[:n_in]
            out_refs = refs[n_in:n_in + n_out]
            vals = [r[...] for r in in_refs]
            if has_state:
                save_ref, st_ref = refs[n_in + n_out], refs[n_in + n_out + 1]

                @pl.when(pl.program_id(seq_axis) == 0)
                def _():
                    st_ref[...] = jnp.zeros(state_shape, F32)

                st = st_ref[...]
                save_ref[...] = st
                res = fn(*vals, st)
                st_ref[...] = res[-1]
                res = res[:-1]
            else:
                res = fn(*vals)
            for o, v in zip(out_refs, res):
                o[...] = v.astype(o.dtype)

            if blocks:
                @pl.when(step_is('last'))
                def _():
                    for cp in copies():
                        cp.wait()

        out_shape = [jax.ShapeDtypeStruct(o.shape, o.dtype) for o in outs]
        out_specs = [pl.BlockSpec(o.block, o.imap) for o in outs]
        scratch = []
        if has_state:
            out_shape.append(jax.ShapeDtypeStruct(save_shape, F32))
            out_specs.append(pl.BlockSpec(save_block, save_imap))
            scratch.append(pltpu.VMEM(state_shape, F32))
        hbm = pl.BlockSpec(memory_space=pl.ANY)
        for blk in blocks:
            out_shape.append(jax.ShapeDtypeStruct((N_DEV,) + blk.shape, blk.dtype))
            out_specs.append(hbm)
            scratch += COMM_SCRATCH
        return pl.pallas_call(
            body, name=name + "_fwd", grid=grid,
            in_specs=[in_spec(s, False) for s in ins] + [hbm] * len(blocks),
            out_specs=out_specs, out_shape=out_shape, scratch_shapes=scratch,
            compiler_params=_cparams(),
        )(*xs, *blocks)

    def grad_shape(s, x):
        if s.cols is not None:
            return x.shape[:-1] + (s.cols[1],)
        return x.shape

    def bwd_call(xs, save, cts, pass_cts=()):
        n_diff = len(diff_idx)
        jobs = pending.take_all() if pending is not None else []
        n_args = n_in + (1 if has_state else 0) + n_out + len(passthrough)

        def body(*refs):
            if jobs:
                src_refs = refs[n_args:n_args + len(jobs)]
                dst_refs = refs[n_args + len(jobs) + n_diff:n_args + 2 * len(jobs) + n_diff]
                sems = refs[len(refs) - 3 * len(jobs):]
                refs = refs[:n_args] + refs[n_args + len(jobs):n_args + len(jobs) + n_diff] + \
                    refs[n_args + 2 * len(jobs) + n_diff:len(refs) - 3 * len(jobs)]
                copies = lambda: [cp for k in range(len(jobs)) for cp in _direct_copies(
                    'exchange', src_refs[k], dst_refs[k], sems[3 * k], sems[3 * k + 1], sems[3 * k + 2])]

                @pl.when(step_is('first'))
                def _():
                    for cp in copies():
                        cp.start()

            in_refs = refs[:n_in]
            p = n_in
            if has_state:
                save_ref = refs[p]
                p += 1
            ct_refs = refs[p:p + n_out]
            p += n_out
            pass_refs = dict(zip(passthrough, refs[p:p + len(passthrough)]))
            p += len(passthrough)
            g_refs = refs[p:p + n_diff]
            p += n_diff
            vals = [r[...] for r in in_refs]

            def g(*dv):
                full = list(vals)
                for i, v in zip(diff_idx, dv):
                    full[i] = v
                if has_state:
                    return tuple(fn(*full, dv[-1]))
                return tuple(fn(*full))

            prim = [vals[i] for i in diff_idx]
            ct = tuple(r[...].astype(F32) for r in ct_refs)
            if has_state:
                dst_ref = refs[p]

                @pl.when(pl.program_id(seq_axis) == 0)
                def _():
                    dst_ref[...] = jnp.zeros(state_shape, F32)

                prim = prim + [save_ref[...]]
                ct = ct + (dst_ref[...],)
            _, vjp = jax.vjp(g, *prim)
            grads = vjp(ct)
            for k, i in enumerate(diff_idx):
                s = ins[i]
                if s.kind == 'blk':
                    g = grads[k] + pass_refs[i][...] if i in pass_refs else grads[k]
                    g_refs[k][...] = g.astype(g_refs[k].dtype)
                else:
                    first = None
                    for ax in s.inner:
                        c = pl.program_id(ax) == 0
                        first = c if first is None else jnp.logical_and(first, c)

                    @pl.when(first)
                    def _(k=k):
                        g_refs[k][...] = jnp.zeros_like(g_refs[k])

                    g_refs[k][...] += grads[k].astype(g_refs[k].dtype)
            if has_state:
                dst_ref[...] = grads[-1]

            if jobs:
                @pl.when(step_is('last'))
                def _():
                    for cp in copies():
                        cp.wait()

        in_specs = [in_spec(s, True) for s in ins]
        args = list(xs)
        if has_state:
            in_specs.append(rel_spec(save_block, save_imap, True))
            args.append(save)
        for o, c in zip(outs, cts):
            in_specs.append(rel_spec(o.block, o.imap, True))
            args.append(c)
        for i, c in zip(passthrough, pass_cts):
            assert ins[i].kind == 'blk' and ins[i].cols is None
            in_specs.append(rel_spec(ins[i].block, ins[i].imap, True))
            args.append(c)
        out_shape, out_specs = [], []
        for i in diff_idx:
            s = ins[i]
            out_shape.append(jax.ShapeDtypeStruct(grad_shape(s, xs[i]), xs[i].dtype))
            out_specs.append(rel_spec(s.block, s.imap, True))
        scratch = [pltpu.VMEM(state_shape, F32)] if has_state else []
        hbm = pl.BlockSpec(memory_space=pl.ANY)
        for _, slabs in jobs:
            in_specs.append(hbm)
            args.append(slabs)
            out_specs.append(hbm)
            out_shape.append(jax.ShapeDtypeStruct(slabs.shape, slabs.dtype))
            scratch += COMM_SCRATCH
        got = pl.pallas_call(
            body, name=name + "_bwd", grid=grid,
            in_specs=in_specs, out_specs=out_specs, out_shape=out_shape, scratch_shapes=scratch,
            compiler_params=_cparams(),
        )(*args)
        for (unit, _), slots in zip(jobs, got[n_diff:]):
            pending.received[unit] = slots
        return got[:n_diff]

    def results(xs, res):
        gathered = tuple(res[len(res) - n_gather:]) if n_gather else ()
        return tuple(res[:n_out]) + tuple(xs[i] for i in passthrough) + gathered

    @jax.custom_vjp
    def op(*xs):
        return results(xs, fwd_call(*xs))

    def op_fwd(*xs):
        res = fwd_call(*xs)
        return results(xs, res), (xs, res[n_out] if has_state else None)

    def op_bwd(resid, cts):
        xs, save = resid
        xs, blocks = xs[:n_in], xs[n_in:]
        grads = bwd_call(xs, save, cts[:n_out], cts[n_out:n_out + len(passthrough)])
        out = []
        k = 0
        for i, s in enumerate(ins):
            if s.kind == 'const':
                out.append(jnp.zeros_like(xs[i]))
                continue
            g = grads[k]
            k += 1
            if s.cols is not None:
                g = jnp.pad(g, ((0, 0),) * (g.ndim - 1) + ((s.cols[0], xs[i].shape[-1] - s.cols[0] - s.cols[1]),))
            out.append(g)
        return tuple(out) + tuple(jnp.zeros_like(b) for b in blocks)

    op.defvjp(op_fwd, op_bwd)
    return op


def _rms(x, w):
    return x * lax.rsqrt(jnp.mean(x * x, axis=-1, keepdims=True) + EPS) * w


def _silu(x):
    return x * jax.nn.sigmoid(x)


def rmsnorm_op(name, t, out_dtype, residual=False):
    tm = _tile(t, 512)
    return make_op(
        name, lambda x, w: (_rms(x, w),), (t // tm,),
        [In((tm, D_MODEL), lambda i: (i, 0)), In((1, D_MODEL), lambda i: (0, 0), 'acc', (0,))],
        [Out((t, D_MODEL), out_dtype, (tm, D_MODEL), lambda i: (i, 0))], passthrough=(0,) if residual else ())


def _tri(q):
    ii = lax.broadcasted_iota(jnp.int32, (q, q), 0)
    jj = lax.broadcasted_iota(jnp.int32, (q, q), 1)
    return ii >= jj, ii > jj


def _ssd_fn(z, x, bm, cm, dtr, dtb, alog, dsk, nw, state):
    q = x.shape[0]
    incl, _ = _tri(q)
    tril = incl.astype(F32)
    dt = jax.nn.softplus(dtr + dtb)
    da = dt * (-jnp.exp(alog))
    acum = cumdot(tril, da)
    acum_t = cumdot_t(tril, da)
    cb = bdot(cm, bm, NT)
    heads = range(4)
    wide = lambda a: jnp.concatenate([jnp.broadcast_to(a[:, r:r + 1], (a.shape[0], 64)) for r in heads], axis=1)
    last = acum[q - 1:q, :]
    xc = x * wide(dt)
    y = bdot(cm, state, NT) * wide(jnp.exp(acum)) + wide(dsk) * x
    ds = bdot(xc * wide(jnp.exp(last - acum)), bm, TN)
    e_last = jnp.exp(last)
    new_state = state * jnp.concatenate([jnp.broadcast_to(e_last[:, r:r + 1], (64, 1)) for r in heads], axis=0) + ds
    diag = []
    for r in heads:
        decay = jnp.exp(jnp.where(incl, acum[:, r:r + 1] - acum_t[r:r + 1, :], -jnp.inf))
        diag.append(bdot(cb * decay, xc[:, 64 * r:64 * r + 64]))
    y = y + jnp.concatenate(diag, axis=1)
    yz = y * _silu(z)
    return _rms(yz, nw), new_state


def _per_sequence(fn, n_seq_args, bl):
    def f(*args):
        *ins, state = args
        res = [fn(*[a[b] for a in ins[:n_seq_args]], *ins[n_seq_args:], state[b]) for b in range(bl)]
        return tuple(jnp.concatenate([r[k][None] for r in res]) for k in range(len(res[0])))

    return f


def ssd_op(name, bl, seq, pending=None, n_gather=0):
    q = SSD_CHUNK
    nc = seq // q
    blk = lambda w, c0, cw: In((bl, q, w), lambda g, n: (0, n, g), cols=(c0, cw))
    small = lambda g, n: (g, 0, 0)
    ins = [
        blk(256, 0, M_D_INNER),
        blk(256, 0, M_D_INNER),
        blk(128, M_D_INNER, 1024),
        blk(128, M_D_INNER + 1024, 1024),
        In((None, bl, q, 4), lambda g, n: (g, 0, n, 0)),
        In((None, 1, 4), small, 'acc', (1,)),
        In((None, 1, 4), small, 'acc', (1,)),
        In((None, 1, 4), small, 'acc', (1,)),
        In((None, 1, 256), small, 'acc', (1,)),
    ]
    outs = [Out((bl, seq, M_D_INNER), F32, (bl, q, 256), lambda g, n: (0, n, g))]
    return make_op(name, _per_sequence(_ssd_fn, 5, bl), (M_GROUPS, nc), ins, outs,
                   state_shape=(bl, 256, 128), seq_axis=1, pending=pending, n_gather=n_gather)


def _gla_fn(layer, qr, fr, ir, gr, lbp, nw, state_t):
    rows = qr.shape[0]
    c = HGRN_CHUNK
    e = jnp.exp(lbp - jnp.max(lbp, axis=0, keepdims=True))
    sm = e / jnp.sum(e, axis=0, keepdims=True)
    lb = jnp.sum(sm[1:layer + 1, :], axis=0, keepdims=True) if layer > 0 else jnp.zeros((1, lbp.shape[1]), F32)
    qq = _silu(qr) * (128 ** -0.5)
    forget = lb + (1.0 - lb) * jax.nn.sigmoid(fr)
    kk = 1.0 - forget
    logf = jnp.log(forget)
    incl, _ = _tri(c)
    tril = incl.astype(F32)
    os_ = []
    for j in range(rows // c):
        sl = slice(c * j, c * j + c)
        gc = cumdot(tril, logf[sl])
        glast = gc[c - 1:c, :]
        q_dec = qq[sl] * jnp.exp(gc)
        k_inv = kk[sl] * jnp.exp(-gc)
        k_end = kk[sl] * jnp.exp(glast - gc)
        att = jnp.where(incl, bdot(q_dec, k_inv, NT), 0.0)
        os_.append(bdot(att, ir[sl]) + bdot(q_dec, state_t, NT))
        state_t = state_t * jnp.exp(glast) + bdot(ir[sl], k_end, TN)
    o = jnp.concatenate(os_, axis=0)
    return _rms(o, nw) * _silu(gr), state_t


def gla_op(name, layer, bl, seq, pending=None, n_gather=0):
    r = HGRN_ROWS
    ns = seq // r
    blk = lambda k: In((bl, r, 128), lambda h, n: (0, n, h), cols=(1024 * k, 1024))
    ins = [blk(0), blk(1), blk(2), blk(3),
           In((DEPTH, 128), lambda h, n: (0, h), 'acc', (1,)),
           In((1, 128), lambda h, n: (0, 0), 'acc', (0, 1))]
    outs = [Out((bl, seq, D_MODEL), F32, (bl, r, 128), lambda h, n: (0, n, h))]
    return make_op(name, _per_sequence(functools.partial(_gla_fn, layer), 4, bl), (H_HEADS, ns), ins, outs,
                   state_shape=(bl, 128, 128), seq_axis=1, pending=pending, n_gather=n_gather)


def _neumann_inverse(m):
    q = m.shape[1]
    ii = lax.broadcasted_iota(jnp.int32, (q, q), 0)
    jj = lax.broadcasted_iota(jnp.int32, (q, q), 1)
    eye = (ii == jj).astype(F32)[None]
    p = -m
    inv = eye + p
    for _ in range(int(math.log2(q)) - 1):
        p = _h3(p, p, BNN)
        inv = inv + _h3(inv, p, BNN)
    return inv


@jax.custom_vjp
def _unit_lower_inverse(m):
    return _neumann_inverse(m)


def _unit_lower_inverse_fwd(m):
    inv = _neumann_inverse(m)
    return inv, inv


_unit_lower_inverse.defvjp(_unit_lower_inverse_fwd,
                           lambda inv, ct: (-_h3(_h3(inv, ct, BTN), inv, BNT),))


def _gdn_fn(qc, kc, vc, zc, br, ar, alog, dtb, nw, state):
    bl, q = qc.shape[0], qc.shape[1]
    incl, strict = _tri(q)
    tril = incl.astype(F32)
    g = jnp.concatenate([-jnp.exp(alog) * jax.nn.softplus(ar[b] + dtb) for b in range(bl)], axis=1)
    gc = cumdot(tril, g)
    gc_t = cumdot_t(tril, g)
    heads, ms, rhs = [], [], []
    for b in range(bl):
        qn = qc[b] * lax.rsqrt(jnp.sum(qc[b] * qc[b], axis=-1, keepdims=True) + EPS) * (128 ** -0.5)
        kn = kc[b] * lax.rsqrt(jnp.sum(kc[b] * kc[b], axis=-1, keepdims=True) + EPS)
        beta = jax.nn.sigmoid(br[b])
        qk = bdot(qn, kn, NT)
        for j in range(2):
            i = 2 * b + j
            col = gc[:, i:i + 1]
            decay = jnp.exp(jnp.where(incl, col - gc_t[i:i + 1, :], -jnp.inf))
            bj = beta[:, j:j + 1]
            kb = kn * bj
            ms.append(jnp.where(strict, bdot(kb, kn, NT) * decay, 0.0))
            rhs.append(jnp.concatenate([vc[b][:, 128 * j:128 * j + 128] * bj, kb * jnp.exp(col)], axis=1))
            heads.append((qn, kn, qk * decay, col, gc[q - 1:q, i:i + 1]))
    sol = h3dot_b(_unit_lower_inverse(jnp.concatenate([m[None] for m in ms])),
                  jnp.concatenate([r[None] for r in rhs]))
    outs, states = [], []
    for b in range(bl):
        os_, sts = [], []
        for j in range(2):
            i = 2 * b + j
            qn, kn, att, col, glast = heads[i]
            u = sol[i][:, :128]
            w = sol[i][:, 128:]
            st = state[b][128 * j:128 * j + 128, :]
            v_new = u - bdot(w, st)
            o = bdot(qn * jnp.exp(col), st) + bdot(att, v_new)
            sts.append(st * jnp.exp(glast) + bdot(kn * jnp.exp(glast - col), v_new, TN))
            os_.append(_rms(o, nw) * _silu(zc[b][:, 128 * j:128 * j + 128]))
        outs.append(jnp.concatenate(os_, axis=1))
        states.append(jnp.concatenate(sts, axis=0))
    return jnp.concatenate([o[None] for o in outs]), jnp.concatenate([st[None] for st in states])


def gdn_op(name, bl, seq, pending=None, n_gather=0):
    q = GDN_CHUNK
    nc = seq // q
    blk = lambda w, c0, cw: In((bl, q, w), lambda h, n: (0, n, h), cols=(c0, cw))
    small = lambda h, n: (h, 0, 0)
    ins = [
        blk(128, 0, G_KEY_DIM),
        blk(128, G_KEY_DIM, G_KEY_DIM),
        blk(256, 2 * G_KEY_DIM, G_VAL_DIM),
        blk(256, G_CONV_DIM, G_VAL_DIM),
        In((None, bl, q, 2), lambda h, n: (h, 0, n, 0)),
        In((None, bl, q, 2), lambda h, n: (h, 0, n, 0)),
        In((None, 1, 2), small, 'acc', (1,)),
        In((None, 1, 2), small, 'acc', (1,)),
        In((1, 128), lambda h, n: (0, 0), 'acc', (0, 1)),
    ]
    outs = [Out((bl, seq, G_VAL_DIM), F32, (bl, q, 256), lambda h, n: (0, n, h))]
    return make_op(name, _gdn_fn, (G_QK_HEADS, nc), ins, outs,
                   state_shape=(bl, 256, 128), seq_axis=1, pending=pending, n_gather=n_gather)


def _xattn_fn(q, k, v):
    s = bdot(q, k, NT) * (X_HEAD_DIM ** -0.5)
    s = s - jnp.max(s, axis=-1, keepdims=True)
    p = jnp.exp(s)
    p = p / jnp.sum(p, axis=-1, keepdims=True)
    return (bdot(p, v),)


def xattn_op(name, bl, seq):
    tq = _tile(seq, 512)
    nq = seq // tq
    t = bl * seq
    ins = [
        In((tq, X_HEAD_DIM), lambda b, h, i: (b * nq + i, h)),
        In((N_MEM, X_HEAD_DIM), lambda b, h, i: (b, h), 'acc', (2,), cols=(0, D_MODEL)),
        In((N_MEM, X_HEAD_DIM), lambda b, h, i: (b, h), 'acc', (2,), cols=(D_MODEL, D_MODEL)),
    ]
    outs = [Out((t, D_MODEL), F32, (tq, X_HEAD_DIM), lambda b, h, i: (b * nq + i, h))]
    return make_op(name, _xattn_fn, (bl, X_HEADS, nq), ins, outs)


CONV_PAD = 8
CONV_ROWS = 64


def make_conv(name, bl, seq, width, ch, x_col0, up_col0=None):
    cb = 256
    rt = CONV_ROWS
    assert ch % cb == 0 and x_col0 % cb == 0 and (up_col0 is None or up_col0 % cb == 0) and seq % rt == 0
    nb = ch // cb
    n_tiles = seq // rt
    t = bl * seq
    has_up = up_col0 is not None
    grid = (nb, bl)
    x_spec = pl.BlockSpec((seq, cb), lambda c, b: (b, x_col0 // cb + c))
    up_specs = [pl.BlockSpec((seq, cb), lambda c, b: (b, up_col0 // cb + c))] if has_up else []
    w_spec = pl.BlockSpec((width, cb), lambda c, b: (0, c))
    b_spec = pl.BlockSpec((1, cb), lambda c, b: (0, c))
    o_spec = pl.BlockSpec((seq, cb), lambda c, b: (b, c))
    taps = [CONV_PAD - (width - 1) + j for j in range(width)]

    def window(x_ref, i):
        if isinstance(i, int) and i == 0:
            return jnp.concatenate([jnp.zeros((CONV_PAD, cb), F32), x_ref[0:rt, :]], axis=0)
        return x_ref[pl.ds(pl.multiple_of(i * rt - CONV_PAD, CONV_PAD), rt + CONV_PAD), :]

    def rows(i):
        return pl.ds(i * rt, rt) if isinstance(i, int) else pl.ds(pl.multiple_of(i * rt, rt), rt)

    def shifted(win):
        return [win[tp:tp + rt, :] for tp in taps]

    def pre_activation(views, w, b):
        y = b + w[0:1, :] * views[0]
        for j in range(1, width):
            y = y + w[j:j + 1, :] * views[j]
        return y

    def over_tiles(step, carry):
        carry = step(0, carry)
        return lax.fori_loop(1, n_tiles, step, carry)

    def fwd_call(x, w, b):
        def body(*refs):
            x_ref, w_ref, b_ref = refs[:3]
            o_ref = refs[-1]
            w_, b_ = w_ref[...], b_ref[...]

            def step(i, carry):
                y = _silu(pre_activation(shifted(window(x_ref, i)), w_, b_))
                if has_up:
                    y = y * refs[3][rows(i), :]
                o_ref[rows(i), :] = y
                return carry

            over_tiles(step, 0)

        return pl.pallas_call(
            body, name=name + "_fwd", grid=grid,
            in_specs=[x_spec, w_spec, b_spec] + up_specs, out_specs=o_spec,
            out_shape=jax.ShapeDtypeStruct((t, ch), F32),
            compiler_params=_cparams(),
        )(*([x, w, b] + ([x] if has_up else [])))

    def bwd_call(x, w, b, do):
        n_in = 4 + (1 if has_up else 0)

        def body(*refs):
            x_ref, w_ref, b_ref = refs[:3]
            do_ref = refs[n_in - 1]
            dx_ref, dw_ref, db_ref = refs[n_in:n_in + 3]
            gpad_ref = refs[-1]
            w_, b_ = w_ref[...], b_ref[...]

            def fold(a):
                acc = a[0:8, :]
                for k in range(1, rt // 8):
                    acc = acc + a[8 * k:8 * k + 8, :]
                return acc

            def grad_pre(i, sums):
                views = shifted(window(x_ref, i))
                y = pre_activation(views, w_, b_)
                s = jax.nn.sigmoid(y)
                act = y * s
                do_ = do_ref[rows(i), :]
                if has_up:
                    refs[n_in + 3][rows(i), :] = do_ * act
                    do_ = do_ * refs[3][rows(i), :]
                dy = do_ * (s + act * (1.0 - s))
                gpad_ref[rows(i), :] = dy
                new = [sums[j] + fold(dy * views[j]) for j in range(width)]
                return tuple(new) + (sums[width] + fold(dy),)

            zero8 = jnp.zeros((8, cb), F32)
            sums = over_tiles(grad_pre, (zero8,) * (width + 1))
            gpad_ref[seq:seq + CONV_PAD, :] = jnp.zeros((CONV_PAD, cb), F32)

            def grad_x(i, carry):
                if isinstance(i, int):
                    gwin = gpad_ref[0:rt + CONV_PAD, :]
                else:
                    gwin = gpad_ref[pl.ds(pl.multiple_of(i * rt, rt), rt + CONV_PAD), :]
                dx = w_[0:1, :] * gwin[width - 1:width - 1 + rt, :]
                for j in range(1, width):
                    dx = dx + w_[j:j + 1, :] * gwin[width - 1 - j:width - 1 - j + rt, :]
                dx_ref[rows(i), :] = dx
                return carry

            over_tiles(grad_x, 0)

            @pl.when(pl.program_id(1) == 0)
            def _():
                dw_ref[...] = jnp.zeros_like(dw_ref)
                db_ref[...] = jnp.zeros_like(db_ref)

            dw_ref[...] += jnp.concatenate([jnp.sum(sums[j], axis=0, keepdims=True) for j in range(width)], axis=0)
            db_ref[...] += jnp.sum(sums[width], axis=0, keepdims=True)

        big = jax.ShapeDtypeStruct((t, ch), F32)
        return pl.pallas_call(
            body, name=name + "_bwd", grid=grid,
            in_specs=[x_spec, w_spec, b_spec] + up_specs + [o_spec],
            out_specs=[o_spec, w_spec, b_spec] + ([o_spec] if has_up else []),
            out_shape=[big, jax.ShapeDtypeStruct((width, ch), F32), jax.ShapeDtypeStruct((1, ch), F32)]
            + ([big] if has_up else []),
            scratch_shapes=[pltpu.VMEM((seq + CONV_PAD, cb), F32)],
            compiler_params=_cparams(),
        )(*([x, w, b] + ([x] if has_up else []) + [do]))

    @jax.custom_vjp
    def conv(x, w, b):
        return fwd_call(x, w, b)

    def conv_fwd(x, w, b):
        return fwd_call(x, w, b), (x, w, b)

    def conv_bwd(res, do):
        x, w, b = res
        got = bwd_call(x, w, b, do)
        dx = jnp.pad(got[0], ((0, 0), (x_col0, x.shape[1] - x_col0 - ch)))
        if has_up:
            dx = dx + jnp.pad(got[3], ((0, 0), (up_col0, x.shape[1] - up_col0 - ch)))
        return dx, got[1], got[2]

    conv.defvjp(conv_fwd, conv_bwd)

    def apply(x, w, b=None):
        if b is None:
            b = jnp.zeros((ch,), F32)
        return conv(x, w, b.reshape(1, ch))

    return apply


def loss_head(x, w, target):
    t = x.shape[0]
    tm = _tile(t, 512)

    def fn(xb, wb, tb):
        err = _rms(xb, wb) - tb
        return 0.5 * jnp.sum(err * err) * (1.0 / D_MODEL)

    def body(x_ref, w_ref, t_ref, loss_ref, dx_ref, dw_ref):
        @pl.when(pl.program_id(0) == 0)
        def _():
            loss_ref[...] = jnp.zeros_like(loss_ref)
            dw_ref[...] = jnp.zeros_like(dw_ref)

        tb = t_ref[...]
        val, vjp = jax.vjp(lambda a, b: fn(a, b, tb), x_ref[...], w_ref[...])
        dx, dw = vjp(jnp.ones((), F32))
        dx_ref[...] = dx
        dw_ref[...] += dw
        loss_ref[...] += jnp.full(loss_ref.shape, val, F32)

    row = pl.BlockSpec((tm, D_MODEL), lambda i: (i, 0))
    vec = pl.BlockSpec((1, D_MODEL), lambda i: (0, 0))
    loss, dx, dw = pl.pallas_call(
        body, name="loss_head", grid=(t // tm,),
        in_specs=[row, vec, row],
        out_specs=[pl.BlockSpec((8, LANE), lambda i: (0, 0)), row, vec],
        out_shape=[jax.ShapeDtypeStruct((8, LANE), F32), jax.ShapeDtypeStruct((t, D_MODEL), F32),
                   jax.ShapeDtypeStruct((1, D_MODEL), F32)],
        compiler_params=_cparams(),
    )(x, w.reshape(1, D_MODEL), target)
    return loss[0, 0], dx, dw.reshape(D_MODEL)


PACK_W = 1024
ADAM_BLOCK_BYTES = 512 * 1024


def _rows_tile(r, c):
    if r * c * 4 <= ADAM_BLOCK_BYTES or r % 8:
        return r
    best = 8
    for t in range(8, r + 1, 8):
        if r % t == 0 and t * c * 4 <= ADAM_BLOCK_BYTES:
            best = t
    return best


def reduce_adamw(slots, w, m, v, name):
    r, wd = w.shape
    tr = _rows_tile(r, wd)
    c1 = 1.0 - ADAM_B1 ** ADAM_STEP
    c2 = 1.0 - ADAM_B2 ** ADAM_STEP

    def body(s_ref, w_ref, m_ref, v_ref, g_ref, d_ref, nm_ref, nv_ref):
        g = s_ref[0].astype(F32)
        for k in range(1, N_DEV):
            g = g + s_ref[k].astype(F32)
        nm = ADAM_B1 * m_ref[...] + (1.0 - ADAM_B1) * g
        nv = ADAM_B2 * v_ref[...] + (1.0 - ADAM_B2) * (g * g)
        m_hat = nm / c1
        v_hat = nv / c2
        d_ref[...] = -ADAM_LR * (m_hat / (jnp.sqrt(v_hat) + ADAM_EPS) + ADAM_WD * w_ref[...])
        g_ref[...] = g
        nm_ref[...] = nm
        nv_ref[...] = nv

    blk = pl.BlockSpec((tr, wd), lambda i: (i, 0))
    shp = jax.ShapeDtypeStruct((r, wd), F32)
    return pl.pallas_call(
        body, name=name, grid=(r // tr,),
        in_specs=[pl.BlockSpec((N_DEV, tr, wd), lambda i: (0, i, 0)), blk, blk, blk],
        out_specs=[blk, blk, blk, blk], out_shape=[shp, shp, shp, shp],
        compiler_params=_cparams(),
    )(slots, w, m, v)


def all_gather(block, name):
    def body(x_ref, out_ref, send_sems, recv_sems, local_sem):
        x, y, c = _position()
        me, sibling = (x, y, c), (x, y, 1 - c)
        chips = [(1 - x, y), (x, 1 - y), (1 - x, 1 - y)]

        def slot(px, py, pc):
            return out_ref.at[4 * px + 2 * py + pc]

        def copy(k, owner, to, src=None):
            return pltpu.make_async_remote_copy(
                src_ref=slot(*owner) if src is None else src, dst_ref=slot(*owner),
                send_sem=send_sems.at[k], recv_sem=recv_sems.at[k],
                device_id=to, device_id_type=pl.DeviceIdType.MESH)

        mine = pltpu.make_async_copy(x_ref, slot(*me), local_sem)
        mine.start()
        first = [copy(0, me, sibling, src=x_ref)]
        first += [copy(1 + j, me, (*chip, c), src=x_ref) for j, chip in enumerate(chips)]
        for cp in first:
            cp.start()
        passed = [copy(4 + j, (*chip, c), sibling) for j, chip in enumerate(chips)]
        for j, chip in enumerate(chips):
            copy(1 + j, (*chip, c), me).wait_recv()
            passed[j].start()
        copy(0, sibling, me).wait_recv()
        for j, chip in enumerate(chips):
            copy(4 + j, (*chip, 1 - c), me).wait_recv()
        for cp in first + passed:
            cp.wait_send()
        mine.wait()

    return pl.pallas_call(
        body, name=name,
        out_shape=jax.ShapeDtypeStruct((N_DEV,) + block.shape, block.dtype),
        in_specs=[pl.BlockSpec(memory_space=pl.ANY)],
        out_specs=pl.BlockSpec(memory_space=pl.ANY),
        scratch_shapes=[pltpu.SemaphoreType.DMA((7,)), pltpu.SemaphoreType.DMA((7,)), pltpu.SemaphoreType.DMA],
    )(block)


def exchange_slabs(slabs, name):
    def body(in_ref, out_ref, send_sems, recv_sems, local_sem):
        x, y, c = _position()
        my = 4 * x + 2 * y + c
        mine = pltpu.make_async_copy(in_ref.at[my], out_ref.at[my], local_sem)
        mine.start()
        copies = []
        for k in range(1, N_DEV):
            dx, dy, dc = (k >> 2) & 1, (k >> 1) & 1, k & 1
            px = x if dx == 0 else 1 - x
            py = y if dy == 0 else 1 - y
            pc = c if dc == 0 else 1 - c
            cp = pltpu.make_async_remote_copy(
                src_ref=in_ref.at[4 * px + 2 * py + pc], dst_ref=out_ref.at[my],
                send_sem=send_sems.at[k - 1], recv_sem=recv_sems.at[k - 1],
                device_id=(px, py, pc), device_id_type=pl.DeviceIdType.MESH)
            cp.start()
            copies.append(cp)
        for cp in copies:
            cp.wait()
        mine.wait()

    return pl.pallas_call(
        body, name=name,
        out_shape=jax.ShapeDtypeStruct(slabs.shape, slabs.dtype),
        in_specs=[pl.BlockSpec(memory_space=pl.ANY)],
        out_specs=pl.BlockSpec(memory_space=pl.ANY),
        scratch_shapes=[pltpu.SemaphoreType.DMA((7,)), pltpu.SemaphoreType.DMA((7,)), pltpu.SemaphoreType.DMA],
    )(slabs)


def _pack(arrays, dtype, row_multiple):
    flat = jnp.concatenate([a.astype(dtype).reshape(-1) for a in arrays])
    n = flat.shape[0]
    per = PACK_W * row_multiple
    total = -(-n // per) * per
    flat = jnp.pad(flat, (0, total - n))
    return flat.reshape(total // PACK_W, PACK_W)


def _unpack(flat2d, shapes, lead=()):
    flat = flat2d.reshape(lead + (-1,))
    out, off = [], 0
    for shp in shapes:
        n = math.prod(shp)
        out.append(flat[..., off:off + n].reshape(lead + tuple(shp)))
        off += n
    return out


def _full_from_gathered(g, axis):
    g = jnp.moveaxis(g, 0, axis)
    shp = list(g.shape)
    shp[axis:axis + 2] = [shp[axis] * shp[axis + 1]]
    return g.reshape(shp)


def _shards_of_full(full, axis):
    shp = list(full.shape)
    shp[axis:axis + 1] = [N_DEV, shp[axis] // N_DEV]
    return jnp.moveaxis(full.reshape(shp), axis, 0)


def layer_units(i):
    mixer = [('m_in_w', 'm_out_w'), ('h_in_w', 'h_out_w'), ('g_in_w', 'g_out_w')][i % 3]
    return [(mixer[0], i // 3), (mixer[1], i // 3), ('xa_q', i), ('xa_kv', i), ('xa_o', i), ('f_up', i), ('f_down', i)]


PADDED_COLS = {'m_in_w': M_IN_PAD, 'g_in_w': G_IN_PAD}
BIG_WEIGHTS = ('m_in_w', 'h_in_w', 'g_in_w', 'f_up')


def whole_weight(name, gathered):
    w = _full_from_gathered(lax.stop_gradient(gathered), SHARD_AXIS[name] - 1)
    return _pad_cols(w, 1, PADDED_COLS[name]) if name in PADDED_COLS else w


def _trunk(p, weights, blocks, standins, x, mem, bl, seq, pending=None):
    t = bl * seq
    ia = ib = ic = 0
    weights = dict(weights)
    state = {}

    def lin(name, a, wname, idx, residual=None):
        unit = (wname, idx)
        pos = state['units'].index(unit)
        later = state['next'][pos] if state['next'] else None
        nxt = (blocks[later],) if later in blocks and later not in state['by_core'] else ()
        n_real = N_DEV * standins[unit].shape[2]
        res = () if residual is None else (residual,)
        mixer_in = wname in ('m_in_w', 'h_in_w', 'g_in_w')
        defer = (pending, unit) if pending is not None and not (mixer_in and i == 0) else None
        y, got = make_linear(name, SHARD_AXIS[wname] - 1, n_real, bool(res), defer)(
            a, weights[unit], standins[unit], nxt, res)
        if nxt:
            weights[later] = whole_weight(later[0], got[0])
        return y

    by_seq = lambda a: a.reshape(bl, seq, a.shape[-1])

    for i in range(DEPTH):
        state['units'] = layer_units(i)
        state['next'] = layer_units(i + 1) if i + 1 < DEPTH else None
        state['by_core'] = [u for u in (state['next'] or []) if u in blocks and u[0] in BIG_WEIGHTS]
        core_blocks = [blocks[u] for u in state['by_core']]

        def core(op, *args):
            y, *got = op(*args, *core_blocks)
            for u, g in zip(state['by_core'], got):
                weights[u] = whole_weight(u[0], g)
            return y
        hn, x = rmsnorm_op(f"ln_mix{i}", t, F32, residual=True)(x, p['ln_mix'][i:i + 1])
        kind = i % 3
        if kind == 0:
            proj = lin(f"m_in{i}", hn, 'm_in_w', ia)
            xbc = make_conv(f"m_conv{i}", bl, seq, 4, M_CONV_DIM, M_D_INNER)(
                proj, p['m_conv_w'][ia], p['m_conv_b'][ia])
            dt = proj[:, M_D_INNER + M_CONV_DIM:M_IN].reshape(bl, seq, M_GROUPS, 4).transpose(2, 0, 1, 3)
            grp = lambda a, n=4: a.reshape(M_GROUPS, 1, n)
            proj3, xbc3 = by_seq(proj), by_seq(xbc)
            y = core(ssd_op(f"ssd{i}", bl, seq, pending, len(core_blocks)),
                     proj3, xbc3, xbc3, xbc3, dt, grp(p['m_dt_bias'][ia]), grp(p['m_a_log'][ia]), grp(p['m_d'][ia]),
                     grp(p['m_norm_w'][ia], 256))
            x = lin(f"m_out{i}", y.reshape(t, M_D_INNER), 'm_out_w', ia, residual=x)
            ia += 1
        elif kind == 1:
            proj3 = by_seq(lin(f"h_in{i}", hn, 'h_in_w', ib))
            y = core(gla_op(f"gla{i}", i, bl, seq, pending, len(core_blocks)),
                     proj3, proj3, proj3, proj3, p['h_lower_bounds'], p['h_norm_w'][ib:ib + 1])
            x = lin(f"h_out{i}", y.reshape(t, D_MODEL), 'h_out_w', ib, residual=x)
            ib += 1
        else:
            proj = lin(f"g_in{i}", hn, 'g_in_w', ic)
            qkv = make_conv(f"g_conv{i}", bl, seq, 4, G_CONV_DIM, 0)(proj, p['g_conv_w'][ic])
            c0 = G_CONV_DIM + G_VAL_DIM
            heads = lambda a: a.reshape(bl, seq, G_QK_HEADS, 2).transpose(2, 0, 1, 3)
            braw = heads(proj[:, c0:c0 + G_V_HEADS])
            araw = heads(proj[:, c0 + G_V_HEADS:c0 + 2 * G_V_HEADS])
            grp = lambda a: a.reshape(G_QK_HEADS, 1, 2)
            qkv3 = by_seq(qkv)
            y = core(gdn_op(f"gdn{i}", bl, seq, pending, len(core_blocks)),
                     qkv3, qkv3, qkv3, by_seq(proj), braw, araw, grp(p['g_a_log'][ic]), grp(p['g_dt_bias'][ic]),
                     p['g_norm_w'][ic:ic + 1])
            x = lin(f"g_out{i}", y.reshape(t, G_VAL_DIM), 'g_out_w', ic, residual=x)
            ic += 1
        hq, x = rmsnorm_op(f"ln_xattn{i}", t, F32, residual=True)(x, p['ln_xattn'][i:i + 1])
        mn = rmsnorm_op(f"ln_mem{i}", bl * N_MEM, F32)(mem, p['ln_mem'][i:i + 1])[0]
        qx = lin(f"xa_q{i}", hq, 'xa_q', i)
        kv = lin(f"xa_kv{i}", mn, 'xa_kv', i)
        ao = xattn_op(f"xattn{i}", bl, seq)(qx, kv, kv)[0]
        x = lin(f"xa_o{i}", ao, 'xa_o', i, residual=x)
        hf, x = rmsnorm_op(f"ln_ffn{i}", t, F32, residual=True)(x, p['ln_ffn'][i:i + 1])
        up = lin(f"f_up{i}", hf, 'f_up', i)
        act = make_conv(f"f_conv{i}", bl, seq, 3, D_FF, 0, up_col0=D_FF)(up, p['f_conv_w'][i], p['f_conv_b'][i])
        x = lin(f"f_down{i}", act, 'f_down', i, residual=x)
    return x


def _pad_cols(w, axis, to):
    pad = [(0, 0)] * w.ndim
    pad[axis] = (0, to - w.shape[axis])
    return jnp.pad(w, pad)


def kernel(x, mem, ln_mix, ln_xattn, ln_mem, ln_ffn, final_norm, m_in_w, m_conv_w, m_conv_b, m_dt_bias, m_a_log, m_d, m_norm_w, m_out_w, h_in_w, h_lower_bounds, h_norm_w, h_out_w, g_in_w, g_conv_w, g_a_log, g_dt_bias, g_norm_w, g_out_w, xa_q, xa_kv, xa_o, f_up, f_conv_w, f_conv_b, f_down, loss_target, m_ln_mix, m_ln_xattn, m_ln_mem, m_ln_ffn, m_final_norm, m_m_in_w, m_m_conv_w, m_m_conv_b, m_m_dt_bias, m_m_a_log, m_m_d, m_m_norm_w, m_m_out_w, m_h_in_w, m_h_lower_bounds, m_h_norm_w, m_h_out_w, m_g_in_w, m_g_conv_w, m_g_a_log, m_g_dt_bias, m_g_norm_w, m_g_out_w, m_xa_q, m_xa_kv, m_xa_o, m_f_up, m_f_conv_w, m_f_conv_b, m_f_down, v_ln_mix, v_ln_xattn, v_ln_mem, v_ln_ffn, v_final_norm, v_m_in_w, v_m_conv_w, v_m_conv_b, v_m_dt_bias, v_m_a_log, v_m_d, v_m_norm_w, v_m_out_w, v_h_in_w, v_h_lower_bounds, v_h_norm_w, v_h_out_w, v_g_in_w, v_g_conv_w, v_g_a_log, v_g_dt_bias, v_g_norm_w, v_g_out_w, v_xa_q, v_xa_kv, v_xa_o, v_f_up, v_f_conv_w, v_f_conv_b, v_f_down):
    local = dict(ln_mix=ln_mix, ln_xattn=ln_xattn, ln_mem=ln_mem, ln_ffn=ln_ffn, final_norm=final_norm, m_in_w=m_in_w, m_conv_w=m_conv_w, m_conv_b=m_conv_b, m_dt_bias=m_dt_bias, m_a_log=m_a_log, m_d=m_d, m_norm_w=m_norm_w, m_out_w=m_out_w, h_in_w=h_in_w, h_lower_bounds=h_lower_bounds, h_norm_w=h_norm_w, h_out_w=h_out_w, g_in_w=g_in_w, g_conv_w=g_conv_w, g_a_log=g_a_log, g_dt_bias=g_dt_bias, g_norm_w=g_norm_w, g_out_w=g_out_w, xa_q=xa_q, xa_kv=xa_kv, xa_o=xa_o, f_up=f_up, f_conv_w=f_conv_w, f_conv_b=f_conv_b, f_down=f_down)
    mom_m = dict(ln_mix=m_ln_mix, ln_xattn=m_ln_xattn, ln_mem=m_ln_mem, ln_ffn=m_ln_ffn, final_norm=m_final_norm, m_in_w=m_m_in_w, m_conv_w=m_m_conv_w, m_conv_b=m_m_conv_b, m_dt_bias=m_m_dt_bias, m_a_log=m_m_a_log, m_d=m_m_d, m_norm_w=m_m_norm_w, m_out_w=m_m_out_w, h_in_w=m_h_in_w, h_lower_bounds=m_h_lower_bounds, h_norm_w=m_h_norm_w, h_out_w=m_h_out_w, g_in_w=m_g_in_w, g_conv_w=m_g_conv_w, g_a_log=m_g_a_log, g_dt_bias=m_g_dt_bias, g_norm_w=m_g_norm_w, g_out_w=m_g_out_w, xa_q=m_xa_q, xa_kv=m_xa_kv, xa_o=m_xa_o, f_up=m_f_up, f_conv_w=m_f_conv_w, f_conv_b=m_f_conv_b, f_down=m_f_down)
    mom_v = dict(ln_mix=v_ln_mix, ln_xattn=v_ln_xattn, ln_mem=v_ln_mem, ln_ffn=v_ln_ffn, final_norm=v_final_norm, m_in_w=v_m_in_w, m_conv_w=v_m_conv_w, m_conv_b=v_m_conv_b, m_dt_bias=v_m_dt_bias, m_a_log=v_m_a_log, m_d=v_m_d, m_norm_w=v_m_norm_w, m_out_w=v_m_out_w, h_in_w=v_h_in_w, h_lower_bounds=v_h_lower_bounds, h_norm_w=v_h_norm_w, h_out_w=v_h_out_w, g_in_w=v_g_in_w, g_conv_w=v_g_conv_w, g_a_log=v_g_a_log, g_dt_bias=v_g_dt_bias, g_norm_w=v_g_norm_w, g_out_w=v_g_out_w, xa_q=v_xa_q, xa_kv=v_xa_kv, xa_o=v_xa_o, f_up=v_f_up, f_conv_w=v_f_conv_w, f_conv_b=v_f_conv_b, f_down=v_f_down)

    bl, seq, _ = x.shape
    t = bl * seq

    p = {n: local[n] for n in WEIGHTS if n not in SHARD_AXIS}
    for n in SMALL_SHARDED:
        p[n] = _full_from_gathered(all_gather(local[n], f"gather_{n}"), SHARD_AXIS[n])
    units = [(n, l) for n in MATMUL_WEIGHTS for l in range(local[n].shape[0])]
    block = lambda u: local[u[0]][u[1]].astype(BF16)
    weights = {u: whole_weight(u[0], all_gather(block(u), f"gather_{u[0]}{u[1]}")) for u in layer_units(0)}
    blocks = {u: block(u) for u in units if u not in weights}
    standins = {u: jnp.zeros((N_DEV,) + local[u[0]].shape[1:], BF16) for u in units}
    small = {n: p[n] for n in WEIGHTS if n not in MATMUL_WEIGHTS and n != 'final_norm'}

    pending = Pending()

    def run(small_w, standins_, xin):
        return _trunk(small_w, weights, blocks, standins_, xin, mem.reshape(bl * N_MEM, D_MODEL), bl, seq, pending)

    x_out, vjp = jax.vjp(run, small, standins, x.reshape(t, D_MODEL))
    loss_part, dx_out, d_final = loss_head(x_out, final_norm, loss_target.reshape(t, D_MODEL))
    grads, received, dx = vjp(dx_out)
    received = dict(received)
    for unit, slabs in pending.take_all():
        pending.received[unit] = exchange_slabs(slabs, f"exchange_{unit[0]}{unit[1]}")
    received.update(pending.received)
    grads = dict(grads)
    grads['final_norm'] = d_final
    loss = lax.psum(loss_part, ("x", "y", "c"))

    outs = {}

    def update(name, n, slots, shape, sel=lambda a: a):
        two_d = lambda a: sel(a).reshape(slots.shape[1:])
        got = reduce_adamw(slots, two_d(local[n]), two_d(mom_m[n]), two_d(mom_v[n]), name)
        return [g.reshape(shape) for g in got]

    for n in SMALL_SHARDED:
        slots = exchange_slabs(_shards_of_full(grads[n], SHARD_AXIS[n]), f"exchange_{n}")
        slots = slots.reshape(N_DEV, -1, slots.shape[-1])
        for kind, a in zip(KINDS, update(f"adamw_{n}", n, slots, local[n].shape)):
            outs[kind, n] = a
    for n in MATMUL_WEIGHTS:
        per_layer = [update(f"adamw_{n}{l}", n, received[n, l], local[n].shape[1:], lambda a, l=l: a[l])
                     for l in range(local[n].shape[0])]
        for k, kind in enumerate(KINDS):
            outs[kind, n] = jnp.concatenate([got[k][None] for got in per_layer])
    replicated = [n for n in WEIGHTS if n not in SHARD_AXIS]
    pk = lambda d: _pack([d[n] for n in replicated], F32, 8)
    got = reduce_adamw(all_gather(pk(grads), "gather_replicated_grads"), pk(local), pk(mom_m), pk(mom_v),
                       "adamw_replicated")
    shapes = [local[n].shape for n in replicated]
    for kind, buf in zip(KINDS, got):
        for n, a in zip(replicated, _unpack(buf, shapes)):
            outs[kind, n] = a
    result = [loss, dx.reshape(bl, seq, D_MODEL)]
    for kind in KINDS:
        result += [outs[kind, n] for n in WEIGHTS]
    return tuple(result)
```

```python
import functools
import math

import jax
import jax.numpy as jnp
from jax import lax
from jax.experimental import pallas as pl
from jax.experimental.pallas import tpu as pltpu

F32 = jnp.float32
BF16 = jnp.bfloat16
NN = (((1,), (0,)), ((), ()))
NT = (((1,), (1,)), ((), ()))
TN = (((0,), (0,)), ((), ()))

D_MODEL = 1024
DEPTH = 4
EPS = 1e-6
N_MEM = 256
M_D_INNER = 2048
M_HEADS = 32
M_GROUPS = 8
M_STATE = 128
M_CONV_DIM = 4096
M_IN = 6176
M_IN_PAD = 6272
SSD_CHUNK = 256
H_HEADS = 8
HGRN_CHUNK = 32
HGRN_ROWS = 256
G_QK_HEADS = 8
G_V_HEADS = 16
G_KEY_DIM = 1024
G_VAL_DIM = 2048
G_CONV_DIM = 4096
G_IN = 6176
G_IN_PAD = 6272
GDN_CHUNK = 64
X_HEADS = 4
X_HEAD_DIM = 256
D_FF = 2816
ADAM_LR = 0.001
ADAM_B1 = 0.9
ADAM_B2 = 0.999
ADAM_EPS = 1e-08
ADAM_WD = 0.01
ADAM_STEP = 10

N_DEV = 8
LANE = 128
KINDS = ('grad', 'delta', 'new_m', 'new_v')
VMEM_LIMIT = 56 * 1024 * 1024

WEIGHTS = ['ln_mix', 'ln_xattn', 'ln_mem', 'ln_ffn', 'final_norm', 'm_in_w', 'm_conv_w', 'm_conv_b', 'm_dt_bias',
           'm_a_log', 'm_d', 'm_norm_w', 'm_out_w', 'h_in_w', 'h_lower_bounds', 'h_norm_w', 'h_out_w', 'g_in_w',
           'g_conv_w', 'g_a_log', 'g_dt_bias', 'g_norm_w', 'g_out_w', 'xa_q', 'xa_kv', 'xa_o', 'f_up', 'f_conv_w',
           'f_conv_b', 'f_down']
SHARD_AXIS = {'m_in_w': 2, 'm_conv_w': 2, 'm_conv_b': 1, 'm_norm_w': 1, 'm_out_w': 1, 'h_in_w': 2, 'h_out_w': 1,
              'g_in_w': 2, 'g_conv_w': 2, 'g_out_w': 1, 'xa_q': 1, 'xa_kv': 2, 'xa_o': 1, 'f_up': 2, 'f_conv_w': 2,
              'f_down': 1}
MATMUL_WEIGHTS = ['m_in_w', 'm_out_w', 'h_in_w', 'h_out_w', 'g_in_w', 'g_out_w', 'xa_q', 'xa_kv', 'xa_o', 'f_up',
                  'f_down']
SMALL_SHARDED = ['m_conv_w', 'm_conv_b', 'm_norm_w', 'g_conv_w', 'f_conv_w']


def _cparams():
    return pltpu.CompilerParams(vmem_limit_bytes=VMEM_LIMIT)


def bdot(a, b, dims=NN):
    return lax.dot_general(a.astype(BF16), b.astype(BF16), dims, preferred_element_type=F32)


def _split(a):
    hi = a.astype(BF16)
    return hi, (a - hi.astype(F32)).astype(BF16)


def _h3(a, b, dims):
    ah, al = _split(a)
    bh, bl = _split(b)
    d = functools.partial(lax.dot_general, dimension_numbers=dims, preferred_element_type=F32)
    return d(ah, bh) + (d(ah, bl) + d(al, bh))


BNN = (((2,), (1,)), ((0,), (0,)))
BNT = (((2,), (2,)), ((0,), (0,)))
BTN = (((1,), (1,)), ((0,), (0,)))


@jax.custom_vjp
def h3dot_b(a, b):
    return _h3(a, b, BNN)


h3dot_b.defvjp(lambda a, b: (_h3(a, b, BNN), (a, b)),
               lambda res, ct: (_h3(ct, res[1], BNT), _h3(res[0], ct, BTN)))

T_ROWS = (((0,), (1,)), ((), ()))


def _tri_times(tri, x, dims, tri_first):
    t = tri.astype(BF16)
    x0 = x.astype(BF16)
    r1 = x - x0.astype(F32)
    x1 = r1.astype(BF16)
    x2 = (r1 - x1.astype(F32)).astype(BF16)
    if tri_first:
        d = lambda xx: lax.dot_general(t, xx, dims, preferred_element_type=F32)
    else:
        d = lambda xx: lax.dot_general(xx, t, dims, preferred_element_type=F32)
    return d(x0) + (d(x1) + d(x2))


@jax.custom_vjp
def cumdot(tri, x):
    return _tri_times(tri, x, NN, True)


cumdot.defvjp(lambda tri, x: (_tri_times(tri, x, NN, True), tri),
              lambda tri, ct: (jnp.zeros_like(tri), _tri_times(tri, ct, TN, True)))


@jax.custom_vjp
def cumdot_t(tri, x):
    return _tri_times(tri, x, T_ROWS, False)


cumdot_t.defvjp(lambda tri, x: (_tri_times(tri, x, T_ROWS, False), tri),
                lambda tri, ct: (jnp.zeros_like(tri), _tri_times(tri, ct, T_ROWS, True)))


def _tile(dim, cap):
    if dim <= cap:
        return dim
    best = None
    for t in range(LANE, cap + 1, LANE):
        if dim % t == 0:
            best = t
    assert best is not None, dim
    return best


def _position():
    return lax.axis_index("x"), lax.axis_index("y"), lax.axis_index("c")


def _direct_copies(kind, src_ref, dst_ref, send_sems, recv_sems, local_sem):
    x, y, c = _position()
    me = 4 * x + 2 * y + c
    local_src = src_ref if kind == 'gather' else src_ref.at[me]
    copies = [pltpu.make_async_copy(local_src, dst_ref.at[me], local_sem)]
    for k in range(1, N_DEV):
        px = 1 - x if (k >> 2) & 1 else x
        py = 1 - y if (k >> 1) & 1 else y
        pc = 1 - c if k & 1 else c
        copies.append(pltpu.make_async_remote_copy(
            src_ref=src_ref if kind == 'gather' else src_ref.at[4 * px + 2 * py + pc], dst_ref=dst_ref.at[me],
            send_sem=send_sems.at[k - 1], recv_sem=recv_sems.at[k - 1],
            device_id=(px, py, pc), device_id_type=pl.DeviceIdType.MESH))
    return copies


COMM_SCRATCH = [pltpu.SemaphoreType.DMA((N_DEV - 1,)), pltpu.SemaphoreType.DMA((N_DEV - 1,)), pltpu.SemaphoreType.DMA]


class Pending:
    def __init__(self):
        self.jobs, self.received = [], {}

    def take_all(self):
        jobs, self.jobs = self.jobs, []
        return jobs


def matmul(a, b, *, ta=False, tb=False, out_dtype=F32, name="mm", carry=None, residual=None):
    if ta:
        k, m = a.shape
    else:
        m, k = a.shape
    if tb:
        n, k2 = b.shape
    else:
        k2, n = b.shape
    assert k == k2, (a.shape, b.shape, ta, tb)
    tm = _tile(m, 1024)
    tn = _tile(n, 1408)
    tk = _tile(k, 1408)
    grid = (m // tm, n // tn, k // tk)
    nk = grid[2]
    dims = (((0 if ta else 1,), (1 if tb else 0,)), ((), ()))

    def at_step(which):
        conds = [pl.program_id(ax) == (0 if which == 'first' else grid[ax] - 1) for ax in range(3)]
        return jnp.logical_and(jnp.logical_and(conds[0], conds[1]), conds[2])

    def body(*refs):
        r_ref = None
        if residual is not None:
            r_ref, refs = refs[2], refs[:2] + refs[3:]
        if carry is None:
            a_ref, b_ref, o_ref, acc_ref = refs
        else:
            a_ref, b_ref, src_ref, o_ref, dst_ref, acc_ref, send_sems, recv_sems, local_sem = refs
            copies = lambda: _direct_copies(carry[0], src_ref, dst_ref, send_sems, recv_sems, local_sem)

            @pl.when(at_step('first'))
            def _():
                for cp in copies():
                    cp.start()

        @pl.when(pl.program_id(2) == 0)
        def _():
            acc_ref[...] = jnp.zeros_like(acc_ref)

        acc_ref[...] += lax.dot_general(a_ref[...].astype(BF16), b_ref[...].astype(BF16), dims,
                                        preferred_element_type=F32)

        @pl.when(pl.program_id(2) == nk - 1)
        def _():
            out = acc_ref[...] if r_ref is None else acc_ref[...] + r_ref[...]
            o_ref[...] = out.astype(o_ref.dtype)

        if carry is not None:
            @pl.when(at_step('last'))
            def _():
                for cp in copies():
                    cp.wait()

    a_spec = pl.BlockSpec((tk, tm), lambda i, j, kk: (kk, i)) if ta else pl.BlockSpec((tm, tk), lambda i, j, kk: (i, kk))
    b_spec = pl.BlockSpec((tn, tk), lambda i, j, kk: (j, kk)) if tb else pl.BlockSpec((tk, tn), lambda i, j, kk: (kk, j))
    o_spec = pl.BlockSpec((tm, tn), lambda i, j, kk: (i, j))
    o_shape = jax.ShapeDtypeStruct((m, n), out_dtype)
    acc = pltpu.VMEM((tm, tn), F32)
    ins, in_specs = [a, b], [a_spec, b_spec]
    if residual is not None:
        ins.append(residual)
        in_specs.append(o_spec)
    if carry is None:
        return pl.pallas_call(
            body, name=name, grid=grid, in_specs=in_specs, out_specs=o_spec, out_shape=o_shape,
            scratch_shapes=[acc], compiler_params=_cparams(),
        )(*ins)
    kind, src = carry
    got = jax.ShapeDtypeStruct(((N_DEV,) + src.shape) if kind == 'gather' else src.shape, src.dtype)
    hbm = pl.BlockSpec(memory_space=pl.ANY)
    return pl.pallas_call(
        body, name=name, grid=grid, in_specs=in_specs + [hbm], out_specs=[o_spec, hbm],
        out_shape=[o_shape, got], scratch_shapes=[acc] + COMM_SCRATCH, compiler_params=_cparams(),
    )(*ins, src)


def make_linear(name, shard_axis, n_real, has_res=False, defer=None):
    def forward(a, w, nxt, res):
        r = res[0] if res else None
        if nxt:
            y, got = matmul(a, w, name=name + "_fwd", carry=('gather', nxt[0]), residual=r)
            return y, (got,)
        return matmul(a, w, name=name + "_fwd", residual=r), ()

    @jax.custom_vjp
    def linear(a, w, wg, nxt, res):
        return forward(a, w, nxt, res)

    def fwd(a, w, wg, nxt, res):
        return forward(a, w, nxt, res), (a, w, nxt)

    def bwd(saved, cts):
        a, w, nxt = saved
        dy = cts[0]
        dw = matmul(a, dy, ta=True, out_dtype=BF16, name=name + "_bwd_dw")
        slabs = _shards_of_full(dw[:, :n_real], shard_axis)
        if defer is None:
            da, slots = matmul(dy, w, tb=True, out_dtype=a.dtype, name=name + "_bwd_da", carry=('exchange', slabs))
        else:
            defer[0].jobs.append((defer[1], slabs))
            da, slots = matmul(dy, w, tb=True, out_dtype=a.dtype, name=name + "_bwd_da"), jnp.zeros_like(slabs)
        return da, jnp.zeros_like(w), slots, tuple(jnp.zeros_like(b) for b in nxt), ((dy,) if has_res else ())

    linear.defvjp(fwd, bwd)
    return linear


class In:
    def __init__(self, block, imap, kind='blk', inner=(), cols=None):
        self.block, self.imap, self.kind, self.inner, self.cols = block, imap, kind, inner, cols


class Out:
    def __init__(self, shape, dtype, block, imap):
        self.shape, self.dtype, self.block, self.imap = shape, dtype, block, imap


def make_op(name, fn, grid, ins, outs, state_shape=None, seq_axis=None, passthrough=(), pending=None, n_gather=0):
    n_in, n_out = len(ins), len(outs)
    has_state = state_shape is not None
    nd = len(grid)
    diff_idx = [i for i, s in enumerate(ins) if s.kind != 'const']

    def in_spec(s, reverse):
        off = 0
        if s.cols is not None:
            assert s.cols[0] % s.block[-1] == 0
            off = s.cols[0] // s.block[-1]

        def imap(*ids):
            ids = rev(ids) if reverse else ids
            idx = tuple(s.imap(*ids))
            return idx[:-1] + (idx[-1] + off,) if off else idx

        return pl.BlockSpec(s.block, imap)

    def rel_spec(block, f, reverse):
        return pl.BlockSpec(block, (lambda *ids: f(*rev(ids))) if reverse else f)

    def rev(ids):
        if not has_state:
            return ids
        ids = list(ids)
        ids[seq_axis] = grid[seq_axis] - 1 - ids[seq_axis]
        return tuple(ids)

    save_shape = tuple(grid) + tuple(state_shape) if has_state else None
    save_block = (None,) * nd + tuple(state_shape) if has_state else None

    def save_imap(*ids):
        return tuple(ids) + (0,) * len(state_shape)

    def step_is(which):
        conds = [pl.program_id(ax) == (0 if which == 'first' else grid[ax] - 1) for ax in range(nd)]
        return functools.reduce(jnp.logical_and, conds)

    def fwd_call(*xs):
        xs, blocks = xs[:n_in], xs[n_in:]
        n_save = 1 if has_state else 0

        def body(*refs):
            if blocks:
                src_refs = refs[n_in:n_in + n_gather]
                dst_refs = refs[n_in + n_gather + n_out + n_save:n_in + 2 * n_gather + n_out + n_save]
                sems = refs[len(refs) - 3 * n_gather:]
                refs = refs[:n_in] + refs[n_in + n_gather:n_in + n_gather + n_out + n_save] + \
                    refs[n_in + 2 * n_gather + n_out + n_save:len(refs) - 3 * n_gather]
                copies = lambda: [cp for k in range(n_gather) for cp in _direct_copies(
                    'gather', src_refs[k], dst_refs[k], sems[3 * k], sems[3 * k + 1], sems[3 * k + 2])]

                @pl.when(step_is('first'))
                def _():
                    for cp in copies():
                        cp.start()

            in_refs = refs[:n_in]
            out_refs = refs[n_in:n_in + n_out]
            vals = [r[...] for r in in_refs]
            if has_state:
                save_ref, st_ref = refs[n_in + n_out], refs[n_in + n_out + 1]

                @pl.when(pl.program_id(seq_axis) == 0)
                def _():
                    st_ref[...] = jnp.zeros(state_shape, F32)

                st = st_ref[...]
                save_ref[...] = st
                res = fn(*vals, st)
                st_ref[...] = res[-1]
                res = res[:-1]
            else:
                res = fn(*vals)
            for o, v in zip(out_refs, res):
                o[...] = v.astype(o.dtype)

            if blocks:
                @pl.when(step_is('last'))
                def _():
                    for cp in copies():
                        cp.wait()

        out_shape = [jax.ShapeDtypeStruct(o.shape, o.dtype) for o in outs]
        out_specs = [pl.BlockSpec(o.block, o.imap) for o in outs]
        scratch = []
        if has_state:
            out_shape.append(jax.ShapeDtypeStruct(save_shape, F32))
            out_specs.append(pl.BlockSpec(save_block, save_imap))
            scratch.append(pltpu.VMEM(state_shape, F32))
        hbm = pl.BlockSpec(memory_space=pl.ANY)
        for blk in blocks:
            out_shape.append(jax.ShapeDtypeStruct((N_DEV,) + blk.shape, blk.dtype))
            out_specs.append(hbm)
            scratch += COMM_SCRATCH
        return pl.pallas_call(
            body, name=name + "_fwd", grid=grid,
            in_specs=[in_spec(s, False) for s in ins] + [hbm] * len(blocks),
            out_specs=out_specs, out_shape=out_shape, scratch_shapes=scratch,
            compiler_params=_cparams(),
        )(*xs, *blocks)

    def grad_shape(s, x):
        if s.cols is not None:
            return x.shape[:-1] + (s.cols[1],)
        return x.shape

    def bwd_call(xs, save, cts, pass_cts=()):
        n_diff = len(diff_idx)
        jobs = pending.take_all() if pending is not None else []
        n_args = n_in + (1 if has_state else 0) + n_out + len(passthrough)

        def body(*refs):
            if jobs:
                src_refs = refs[n_args:n_args + len(jobs)]
                dst_refs = refs[n_args + len(jobs) + n_diff:n_args + 2 * len(jobs) + n_diff]
                sems = refs[len(refs) - 3 * len(jobs):]
                refs = refs[:n_args] + refs[n_args + len(jobs):n_args + len(jobs) + n_diff] + \
                    refs[n_args + 2 * len(jobs) + n_diff:len(refs) - 3 * len(jobs)]
                copies = lambda: [cp for k in range(len(jobs)) for cp in _direct_copies(
                    'exchange', src_refs[k], dst_refs[k], sems[3 * k], sems[3 * k + 1], sems[3 * k + 2])]

                @pl.when(step_is('first'))
                def _():
                    for cp in copies():
                        cp.start()

            in_refs = refs[:n_in]
            p = n_in
            if has_state:
                save_ref = refs[p]
                p += 1
            ct_refs = refs[p:p + n_out]
            p += n_out
            pass_refs = dict(zip(passthrough, refs[p:p + len(passthrough)]))
            p += len(passthrough)
            g_refs = refs[p:p + n_diff]
            p += n_diff
            vals = [r[...] for r in in_refs]

            def g(*dv):
                full = list(vals)
                for i, v in zip(diff_idx, dv):
                    full[i] = v
                if has_state:
                    return tuple(fn(*full, dv[-1]))
                return tuple(fn(*full))

            prim = [vals[i] for i in diff_idx]
            ct = tuple(r[...].astype(F32) for r in ct_refs)
            if has_state:
                dst_ref = refs[p]

                @pl.when(pl.program_id(seq_axis) == 0)
                def _():
                    dst_ref[...] = jnp.zeros(state_shape, F32)

                prim = prim + [save_ref[...]]
                ct = ct + (dst_ref[...],)
            _, vjp = jax.vjp(g, *prim)
            grads = vjp(ct)
            for k, i in enumerate(diff_idx):
                s = ins[i]
                if s.kind == 'blk':
                    g = grads[k] + pass_refs[i][...] if i in pass_refs else grads[k]
                    g_refs[k][...] = g.astype(g_refs[k].dtype)
                else:
                    first = None
                    for ax in s.inner:
                        c = pl.program_id(ax) == 0
                        first = c if first is None else jnp.logical_and(first, c)

                    @pl.when(first)
                    def _(k=k):
                        g_refs[k][...] = jnp.zeros_like(g_refs[k])

                    g_refs[k][...] += grads[k].astype(g_refs[k].dtype)
            if has_state:
                dst_ref[...] = grads[-1]

            if jobs:
                @pl.when(step_is('last'))
                def _():
                    for cp in copies():
                        cp.wait()

        in_specs = [in_spec(s, True) for s in ins]
        args = list(xs)
        if has_state:
            in_specs.append(rel_spec(save_block, save_imap, True))
            args.append(save)
        for o, c in zip(outs, cts):
            in_specs.append(rel_spec(o.block, o.imap, True))
            args.append(c)
        for i, c in zip(passthrough, pass_cts):
            assert ins[i].kind == 'blk' and ins[i].cols is None
            in_specs.append(rel_spec(ins[i].block, ins[i].imap, True))
            args.append(c)
        out_shape, out_specs = [], []
        for i in diff_idx:
            s = ins[i]
            out_shape.append(jax.ShapeDtypeStruct(grad_shape(s, xs[i]), xs[i].dtype))
            out_specs.append(rel_spec(s.block, s.imap, True))
        scratch = [pltpu.VMEM(state_shape, F32)] if has_state else []
        hbm = pl.BlockSpec(memory_space=pl.ANY)
        for _, slabs in jobs:
            in_specs.append(hbm)
            args.append(slabs)
            out_specs.append(hbm)
            out_shape.append(jax.ShapeDtypeStruct(slabs.shape, slabs.dtype))
            scratch += COMM_SCRATCH
        got = pl.pallas_call(
            body, name=name + "_bwd", grid=grid,
            in_specs=in_specs, out_specs=out_specs, out_shape=out_shape, scratch_shapes=scratch,
            compiler_params=_cparams(),
        )(*args)
        for (unit, _), slots in zip(jobs, got[n_diff:]):
            pending.received[unit] = slots
        return got[:n_diff]

    def results(xs, res):
        gathered = tuple(res[len(res) - n_gather:]) if n_gather else ()
        return tuple(res[:n_out]) + tuple(xs[i] for i in passthrough) + gathered

    @jax.custom_vjp
    def op(*xs):
        return results(xs, fwd_call(*xs))

    def op_fwd(*xs):
        res = fwd_call(*xs)
        return results(xs, res), (xs, res[n_out] if has_state else None)

    def op_bwd(resid, cts):
        xs, save = resid
        xs, blocks = xs[:n_in], xs[n_in:]
        grads = bwd_call(xs, save, cts[:n_out], cts[n_out:n_out + len(passthrough)])
        out = []
        k = 0
        for i, s in enumerate(ins):
            if s.kind == 'const':
                out.append(jnp.zeros_like(xs[i]))
                continue
            g = grads[k]
            k += 1
            if s.cols is not None:
                g = jnp.pad(g, ((0, 0),) * (g.ndim - 1) + ((s.cols[0], xs[i].shape[-1] - s.cols[0] - s.cols[1]),))
            out.append(g)
        return tuple(out) + tuple(jnp.zeros_like(b) for b in blocks)

    op.defvjp(op_fwd, op_bwd)
    return op


def _rms(x, w):
    return x * lax.rsqrt(jnp.mean(x * x, axis=-1, keepdims=True) + EPS) * w


def _silu(x):
    return x * jax.nn.sigmoid(x)


def rmsnorm_op(name, t, out_dtype, residual=False):
    tm = _tile(t, 512)
    return make_op(
        name, lambda x, w: (_rms(x, w),), (t // tm,),
        [In((tm, D_MODEL), lambda i: (i, 0)), In((1, D_MODEL), lambda i: (0, 0), 'acc', (0,))],
        [Out((t, D_MODEL), out_dtype, (tm, D_MODEL), lambda i: (i, 0))], passthrough=(0,) if residual else ())


def _tri(q):
    ii = lax.broadcasted_iota(jnp.int32, (q, q), 0)
    jj = lax.broadcasted_iota(jnp.int32, (q, q), 1)
    return ii >= jj, ii > jj


def _ssd_fn(z, x, bm, cm, dtr, dtb, alog, dsk, nw, state):
    q = x.shape[0]
    incl, _ = _tri(q)
    tril = incl.astype(F32)
    dt = jax.nn.softplus(dtr + dtb)
    da = dt * (-jnp.exp(alog))
    acum = cumdot(tril, da)
    acum_t = cumdot_t(tril, da)
    cb = bdot(cm, bm, NT)
    heads = range(4)
    wide = lambda a: jnp.concatenate([jnp.broadcast_to(a[:, r:r + 1], (a.shape[0], 64)) for r in heads], axis=1)
    last = acum[q - 1:q, :]
    xc = x * wide(dt)
    y = bdot(cm, state, NT) * wide(jnp.exp(acum)) + wide(dsk) * x
    ds = bdot(xc * wide(jnp.exp(last - acum)), bm, TN)
    e_last = jnp.exp(last)
    new_state = state * jnp.concatenate([jnp.broadcast_to(e_last[:, r:r + 1], (64, 1)) for r in heads], axis=0) + ds
    diag = []
    for r in heads:
        decay = jnp.exp(jnp.where(incl, acum[:, r:r + 1] - acum_t[r:r + 1, :], -jnp.inf))
        diag.append(bdot(cb * decay, xc[:, 64 * r:64 * r + 64]))
    y = y + jnp.concatenate(diag, axis=1)
    yz = y * _silu(z)
    return _rms(yz, nw), new_state


def _per_sequence(fn, n_seq_args, bl):
    def f(*args):
        *ins, state = args
        res = [fn(*[a[b] for a in ins[:n_seq_args]], *ins[n_seq_args:], state[b]) for b in range(bl)]
        return tuple(jnp.concatenate([r[k][None] for r in res]) for k in range(len(res[0])))

    return f


def ssd_op(name, bl, seq, pending=None, n_gather=0):
    q = SSD_CHUNK
    nc = seq // q
    blk = lambda w, c0, cw: In((bl, q, w), lambda g, n: (0, n, g), cols=(c0, cw))
    small = lambda g, n: (g, 0, 0)
    ins = [
        blk(256, 0, M_D_INNER),
        blk(256, 0, M_D_INNER),
        blk(128, M_D_INNER, 1024),
        blk(128, M_D_INNER + 1024, 1024),
        In((None, bl, q, 4), lambda g, n: (g, 0, n, 0)),
        In((None, 1, 4), small, 'acc', (1,)),
        In((None, 1, 4), small, 'acc', (1,)),
        In((None, 1, 4), small, 'acc', (1,)),
        In((None, 1, 256), small, 'acc', (1,)),
    ]
    outs = [Out((bl, seq, M_D_INNER), F32, (bl, q, 256), lambda g, n: (0, n, g))]
    return make_op(name, _per_sequence(_ssd_fn, 5, bl), (M_GROUPS, nc), ins, outs,
                   state_shape=(bl, 256, 128), seq_axis=1, pending=pending, n_gather=n_gather)


def _gla_fn(layer, qr, fr, ir, gr, lbp, nw, state_t):
    rows = qr.shape[0]
    c = HGRN_CHUNK
    n_chunks = rows // c
    e = jnp.exp(lbp - jnp.max(lbp, axis=0, keepdims=True))
    sm = e / jnp.sum(e, axis=0, keepdims=True)
    lb = jnp.sum(sm[1:layer + 1, :], axis=0, keepdims=True) if layer > 0 else jnp.zeros((1, lbp.shape[1]), F32)
    qq = _silu(qr) * (128 ** -0.5)
    forget = lb + (1.0 - lb) * jax.nn.sigmoid(fr)
    kk = 1.0 - forget
    logf = jnp.log(forget)
    ii = lax.broadcasted_iota(jnp.int32, (rows, rows), 0)
    jj = lax.broadcasted_iota(jnp.int32, (rows, rows), 1)
    own = jnp.logical_and(ii >= jj, ii // c == jj // c)
    gc = cumdot(own.astype(F32), logf)
    glasts = [gc[c * j + c - 1:c * j + c, :] for j in range(n_chunks)]
    glast_rows = jnp.concatenate([jnp.broadcast_to(g, (c, g.shape[1])) for g in glasts], axis=0)
    q_dec = qq * jnp.exp(gc)
    k_inv = kk * jnp.exp(-gc)
    k_end = kk * jnp.exp(glast_rows - gc)
    att = jnp.where(own, bdot(q_dec, k_inv, NT), 0.0)
    o = bdot(att, ir)
    inter = []
    for j in range(n_chunks):
        sl = slice(c * j, c * j + c)
        inter.append(bdot(q_dec[sl], state_t, NT))
        state_t = state_t * jnp.exp(glasts[j]) + bdot(ir[sl], k_end[sl], TN)
    o = o + jnp.concatenate(inter, axis=0)
    return _rms(o, nw) * _silu(gr), state_t


def gla_op(name, layer, bl, seq, pending=None, n_gather=0):
    r = HGRN_ROWS
    ns = seq // r
    blk = lambda k: In((bl, r, 128), lambda h, n: (0, n, h), cols=(1024 * k, 1024))
    ins = [blk(0), blk(1), blk(2), blk(3),
           In((DEPTH, 128), lambda h, n: (0, h), 'acc', (1,)),
           In((1, 128), lambda h, n: (0, 0), 'acc', (0, 1))]
    outs = [Out((bl, seq, D_MODEL), F32, (bl, r, 128), lambda h, n: (0, n, h))]
    return make_op(name, _per_sequence(functools.partial(_gla_fn, layer), 4, bl), (H_HEADS, ns), ins, outs,
                   state_shape=(bl, 128, 128), seq_axis=1, pending=pending, n_gather=n_gather)


def _neumann_inverse(m):
    q = m.shape[1]
    ii = lax.broadcasted_iota(jnp.int32, (q, q), 0)
    jj = lax.broadcasted_iota(jnp.int32, (q, q), 1)
    eye = (ii == jj).astype(F32)[None]
    p = -m
    inv = eye + p
    for _ in range(int(math.log2(q)) - 1):
        p = _h3(p, p, BNN)
        inv = inv + _h3(inv, p, BNN)
    return inv


@jax.custom_vjp
def _unit_lower_inverse(m):
    return _neumann_inverse(m)


def _unit_lower_inverse_fwd(m):
    inv = _neumann_inverse(m)
    return inv, inv


_unit_lower_inverse.defvjp(_unit_lower_inverse_fwd,
                           lambda inv, ct: (-_h3(_h3(inv, ct, BTN), inv, BNT),))


def _gdn_fn(qc, kc, vc, zc, br, ar, alog, dtb, nw, state):
    bl, q = qc.shape[0], qc.shape[1]
    incl, strict = _tri(q)
    tril = incl.astype(F32)
    g = jnp.concatenate([-jnp.exp(alog) * jax.nn.softplus(ar[b] + dtb) for b in range(bl)], axis=1)
    gc = cumdot(tril, g)
    gc_t = cumdot_t(tril, g)
    heads, ms, rhs = [], [], []
    for b in range(bl):
        qn = qc[b] * lax.rsqrt(jnp.sum(qc[b] * qc[b], axis=-1, keepdims=True) + EPS) * (128 ** -0.5)
        kn = kc[b] * lax.rsqrt(jnp.sum(kc[b] * kc[b], axis=-1, keepdims=True) + EPS)
        beta = jax.nn.sigmoid(br[b])
        qk = bdot(qn, kn, NT)
        for j in range(2):
            i = 2 * b + j
            col = gc[:, i:i + 1]
            decay = jnp.exp(jnp.where(incl, col - gc_t[i:i + 1, :], -jnp.inf))
            bj = beta[:, j:j + 1]
            kb = kn * bj
            ms.append(jnp.where(strict, bdot(kb, kn, NT) * decay, 0.0))
            rhs.append(jnp.concatenate([vc[b][:, 128 * j:128 * j + 128] * bj, kb * jnp.exp(col)], axis=1))
            heads.append((qn, kn, qk * decay, col, gc[q - 1:q, i:i + 1]))
    sol = h3dot_b(_unit_lower_inverse(jnp.concatenate([m[None] for m in ms])),
                  jnp.concatenate([r[None] for r in rhs]))
    outs, states = [], []
    for b in range(bl):
        os_, sts = [], []
        for j in range(2):
            i = 2 * b + j
            qn, kn, att, col, glast = heads[i]
            u = sol[i][:, :128]
            w = sol[i][:, 128:]
            st = state[b][128 * j:128 * j + 128, :]
            v_new = u - bdot(w, st)
            o = bdot(qn * jnp.exp(col), st) + bdot(att, v_new)
            sts.append(st * jnp.exp(glast) + bdot(kn * jnp.exp(glast - col), v_new, TN))
            os_.append(_rms(o, nw) * _silu(zc[b][:, 128 * j:128 * j + 128]))
        outs.append(jnp.concatenate(os_, axis=1))
        states.append(jnp.concatenate(sts, axis=0))
    return jnp.concatenate([o[None] for o in outs]), jnp.concatenate([st[None] for st in states])


def gdn_op(name, bl, seq, pending=None, n_gather=0):
    q = GDN_CHUNK
    nc = seq // q
    blk = lambda w, c0, cw: In((bl, q, w), lambda h, n: (0, n, h), cols=(c0, cw))
    small = lambda h, n: (h, 0, 0)
    ins = [
        blk(128, 0, G_KEY_DIM),
        blk(128, G_KEY_DIM, G_KEY_DIM),
        blk(256, 2 * G_KEY_DIM, G_VAL_DIM),
        blk(256, G_CONV_DIM, G_VAL_DIM),
        In((None, bl, q, 2), lambda h, n: (h, 0, n, 0)),
        In((None, bl, q, 2), lambda h, n: (h, 0, n, 0)),
        In((None, 1, 2), small, 'acc', (1,)),
        In((None, 1, 2), small, 'acc', (1,)),
        In((1, 128), lambda h, n: (0, 0), 'acc', (0, 1)),
    ]
    outs = [Out((bl, seq, G_VAL_DIM), F32, (bl, q, 256), lambda h, n: (0, n, h))]
    return make_op(name, _gdn_fn, (G_QK_HEADS, nc), ins, outs,
                   state_shape=(bl, 256, 128), seq_axis=1, pending=pending, n_gather=n_gather)


def _xattn_fn(q, k, v):
    s = bdot(q, k, NT) * (X_HEAD_DIM ** -0.5)
    s = s - jnp.max(s, axis=-1, keepdims=True)
    p = jnp.exp(s)
    p = p / jnp.sum(p, axis=-1, keepdims=True)
    return (bdot(p, v),)


def xattn_op(name, bl, seq):
    tq = _tile(seq, 512)
    nq = seq // tq
    t = bl * seq
    ins = [
        In((tq, X_HEAD_DIM), lambda b, h, i: (b * nq + i, h)),
        In((N_MEM, X_HEAD_DIM), lambda b, h, i: (b, h), 'acc', (2,), cols=(0, D_MODEL)),
        In((N_MEM, X_HEAD_DIM), lambda b, h, i: (b, h), 'acc', (2,), cols=(D_MODEL, D_MODEL)),
    ]
    outs = [Out((t, D_MODEL), F32, (tq, X_HEAD_DIM), lambda b, h, i: (b * nq + i, h))]
    return make_op(name, _xattn_fn, (bl, X_HEADS, nq), ins, outs)


CONV_PAD = 8
CONV_ROWS = 64


def make_conv(name, bl, seq, width, ch, x_col0, up_col0=None):
    cb = 256
    rt = CONV_ROWS
    assert ch % cb == 0 and x_col0 % cb == 0 and (up_col0 is None or up_col0 % cb == 0) and seq % rt == 0
    nb = ch // cb
    n_tiles = seq // rt
    t = bl * seq
    has_up = up_col0 is not None
    grid = (nb, bl)
    x_spec = pl.BlockSpec((seq, cb), lambda c, b: (b, x_col0 // cb + c))
    up_specs = [pl.BlockSpec((seq, cb), lambda c, b: (b, up_col0 // cb + c))] if has_up else []
    w_spec = pl.BlockSpec((width, cb), lambda c, b: (0, c))
    b_spec = pl.BlockSpec((1, cb), lambda c, b: (0, c))
    o_spec = pl.BlockSpec((seq, cb), lambda c, b: (b, c))
    taps = [CONV_PAD - (width - 1) + j for j in range(width)]

    def window(x_ref, i):
        if isinstance(i, int) and i == 0:
            return jnp.concatenate([jnp.zeros((CONV_PAD, cb), F32), x_ref[0:rt, :]], axis=0)
        return x_ref[pl.ds(pl.multiple_of(i * rt - CONV_PAD, CONV_PAD), rt + CONV_PAD), :]

    def rows(i):
        return pl.ds(i * rt, rt) if isinstance(i, int) else pl.ds(pl.multiple_of(i * rt, rt), rt)

    def shifted(win):
        return [win[tp:tp + rt, :] for tp in taps]

    def pre_activation(views, w, b):
        y = b + w[0:1, :] * views[0]
        for j in range(1, width):
            y = y + w[j:j + 1, :] * views[j]
        return y

    def over_tiles(step, carry):
        carry = step(0, carry)
        return lax.fori_loop(1, n_tiles, step, carry)

    def fwd_call(x, w, b):
        def body(*refs):
            x_ref, w_ref, b_ref = refs[:3]
            o_ref = refs[-1]
            w_, b_ = w_ref[...], b_ref[...]

            def step(i, carry):
                y = _silu(pre_activation(shifted(window(x_ref, i)), w_, b_))
                if has_up:
                    y = y * refs[3][rows(i), :]
                o_ref[rows(i), :] = y
                return carry

            over_tiles(step, 0)

        return pl.pallas_call(
            body, name=name + "_fwd", grid=grid,
            in_specs=[x_spec, w_spec, b_spec] + up_specs, out_specs=o_spec,
            out_shape=jax.ShapeDtypeStruct((t, ch), F32),
            compiler_params=_cparams(),
        )(*([x, w, b] + ([x] if has_up else [])))

    def bwd_call(x, w, b, do):
        n_in = 4 + (1 if has_up else 0)

        def body(*refs):
            x_ref, w_ref, b_ref = refs[:3]
            do_ref = refs[n_in - 1]
            dx_ref, dw_ref, db_ref = refs[n_in:n_in + 3]
            gpad_ref = refs[-1]
            w_, b_ = w_ref[...], b_ref[...]

            def fold(a):
                acc = a[0:8, :]
                for k in range(1, rt // 8):
                    acc = acc + a[8 * k:8 * k + 8, :]
                return acc

            def grad_pre(i, sums):
                views = shifted(window(x_ref, i))
                y = pre_activation(views, w_, b_)
                s = jax.nn.sigmoid(y)
                act = y * s
                do_ = do_ref[rows(i), :]
                if has_up:
                    refs[n_in + 3][rows(i), :] = do_ * act
                    do_ = do_ * refs[3][rows(i), :]
                dy = do_ * (s + act * (1.0 - s))
                gpad_ref[rows(i), :] = dy
                new = [sums[j] + fold(dy * views[j]) for j in range(width)]
                return tuple(new) + (sums[width] + fold(dy),)

            zero8 = jnp.zeros((8, cb), F32)
            sums = over_tiles(grad_pre, (zero8,) * (width + 1))
            gpad_ref[seq:seq + CONV_PAD, :] = jnp.zeros((CONV_PAD, cb), F32)

            def grad_x(i, carry):
                if isinstance(i, int):
                    gwin = gpad_ref[0:rt + CONV_PAD, :]
                else:
                    gwin = gpad_ref[pl.ds(pl.multiple_of(i * rt, rt), rt + CONV_PAD), :]
                dx = w_[0:1, :] * gwin[width - 1:width - 1 + rt, :]
                for j in range(1, width):
                    dx = dx + w_[j:j + 1, :] * gwin[width - 1 - j:width - 1 - j + rt, :]
                dx_ref[rows(i), :] = dx
                return carry

            over_tiles(grad_x, 0)

            @pl.when(pl.program_id(1) == 0)
            def _():
                dw_ref[...] = jnp.zeros_like(dw_ref)
                db_ref[...] = jnp.zeros_like(db_ref)

            dw_ref[...] += jnp.concatenate([jnp.sum(sums[j], axis=0, keepdims=True) for j in range(width)], axis=0)
            db_ref[...] += jnp.sum(sums[width], axis=0, keepdims=True)

        big = jax.ShapeDtypeStruct((t, ch), F32)
        return pl.pallas_call(
            body, name=name + "_bwd", grid=grid,
            in_specs=[x_spec, w_spec, b_spec] + up_specs + [o_spec],
            out_specs=[o_spec, w_spec, b_spec] + ([o_spec] if has_up else []),
            out_shape=[big, jax.ShapeDtypeStruct((width, ch), F32), jax.ShapeDtypeStruct((1, ch), F32)]
            + ([big] if has_up else []),
            scratch_shapes=[pltpu.VMEM((seq + CONV_PAD, cb), F32)],
            compiler_params=_cparams(),
        )(*([x, w, b] + ([x] if has_up else []) + [do]))

    @jax.custom_vjp
    def conv(x, w, b):
        return fwd_call(x, w, b)

    def conv_fwd(x, w, b):
        return fwd_call(x, w, b), (x, w, b)

    def conv_bwd(res, do):
        x, w, b = res
        got = bwd_call(x, w, b, do)
        dx = jnp.pad(got[0], ((0, 0), (x_col0, x.shape[1] - x_col0 - ch)))
        if has_up:
            dx = dx + jnp.pad(got[3], ((0, 0), (up_col0, x.shape[1] - up_col0 - ch)))
        return dx, got[1], got[2]

    conv.defvjp(conv_fwd, conv_bwd)

    def apply(x, w, b=None):
        if b is None:
            b = jnp.zeros((ch,), F32)
        return conv(x, w, b.reshape(1, ch))

    return apply


def loss_head(x, w, target):
    t = x.shape[0]
    tm = _tile(t, 512)

    def fn(xb, wb, tb):
        err = _rms(xb, wb) - tb
        return 0.5 * jnp.sum(err * err) * (1.0 / D_MODEL)

    def body(x_ref, w_ref, t_ref, loss_ref, dx_ref, dw_ref):
        @pl.when(pl.program_id(0) == 0)
        def _():
            loss_ref[...] = jnp.zeros_like(loss_ref)
            dw_ref[...] = jnp.zeros_like(dw_ref)

        tb = t_ref[...]
        val, vjp = jax.vjp(lambda a, b: fn(a, b, tb), x_ref[...], w_ref[...])
        dx, dw = vjp(jnp.ones((), F32))
        dx_ref[...] = dx
        dw_ref[...] += dw
        loss_ref[...] += jnp.full(loss_ref.shape, val, F32)

    row = pl.BlockSpec((tm, D_MODEL), lambda i: (i, 0))
    vec = pl.BlockSpec((1, D_MODEL), lambda i: (0, 0))
    loss, dx, dw = pl.pallas_call(
        body, name="loss_head", grid=(t // tm,),
        in_specs=[row, vec, row],
        out_specs=[pl.BlockSpec((8, LANE), lambda i: (0, 0)), row, vec],
        out_shape=[jax.ShapeDtypeStruct((8, LANE), F32), jax.ShapeDtypeStruct((t, D_MODEL), F32),
                   jax.ShapeDtypeStruct((1, D_MODEL), F32)],
        compiler_params=_cparams(),
    )(x, w.reshape(1, D_MODEL), target)
    return loss[0, 0], dx, dw.reshape(D_MODEL)


PACK_W = 1024
ADAM_BLOCK_BYTES = 512 * 1024


def _rows_tile(r, c):
    if r * c * 4 <= ADAM_BLOCK_BYTES or r % 8:
        return r
    best = 8
    for t in range(8, r + 1, 8):
        if r % t == 0 and t * c * 4 <= ADAM_BLOCK_BYTES:
            best = t
    return best


def reduce_adamw(slots, w, m, v, name):
    r, wd = w.shape
    tr = _rows_tile(r, wd)
    c1 = 1.0 - ADAM_B1 ** ADAM_STEP
    c2 = 1.0 - ADAM_B2 ** ADAM_STEP

    def body(s_ref, w_ref, m_ref, v_ref, g_ref, d_ref, nm_ref, nv_ref):
        g = s_ref[0].astype(F32)
        for k in range(1, N_DEV):
            g = g + s_ref[k].astype(F32)
        nm = ADAM_B1 * m_ref[...] + (1.0 - ADAM_B1) * g
        nv = ADAM_B2 * v_ref[...] + (1.0 - ADAM_B2) * (g * g)
        m_hat = nm / c1
        v_hat = nv / c2
        d_ref[...] = -ADAM_LR * (m_hat / (jnp.sqrt(v_hat) + ADAM_EPS) + ADAM_WD * w_ref[...])
        g_ref[...] = g
        nm_ref[...] = nm
        nv_ref[...] = nv

    blk = pl.BlockSpec((tr, wd), lambda i: (i, 0))
    shp = jax.ShapeDtypeStruct((r, wd), F32)
    return pl.pallas_call(
        body, name=name, grid=(r // tr,),
        in_specs=[pl.BlockSpec((N_DEV, tr, wd), lambda i: (0, i, 0)), blk, blk, blk],
        out_specs=[blk, blk, blk, blk], out_shape=[shp, shp, shp, shp],
        compiler_params=_cparams(),
    )(slots, w, m, v)


def all_gather(block, name):
    def body(x_ref, out_ref, send_sems, recv_sems, local_sem):
        x, y, c = _position()
        me, sibling = (x, y, c), (x, y, 1 - c)
        chips = [(1 - x, y), (x, 1 - y), (1 - x, 1 - y)]

        def slot(px, py, pc):
            return out_ref.at[4 * px + 2 * py + pc]

        def copy(k, owner, to, src=None):
            return pltpu.make_async_remote_copy(
                src_ref=slot(*owner) if src is None else src, dst_ref=slot(*owner),
                send_sem=send_sems.at[k], recv_sem=recv_sems.at[k],
                device_id=to, device_id_type=pl.DeviceIdType.MESH)

        mine = pltpu.make_async_copy(x_ref, slot(*me), local_sem)
        mine.start()
        first = [copy(0, me, sibling, src=x_ref)]
        first += [copy(1 + j, me, (*chip, c), src=x_ref) for j, chip in enumerate(chips)]
        for cp in first:
            cp.start()
        passed = [copy(4 + j, (*chip, c), sibling) for j, chip in enumerate(chips)]
        for j, chip in enumerate(chips):
            copy(1 + j, (*chip, c), me).wait_recv()
            passed[j].start()
        copy(0, sibling, me).wait_recv()
        for j, chip in enumerate(chips):
            copy(4 + j, (*chip, 1 - c), me).wait_recv()
        for cp in first + passed:
            cp.wait_send()
        mine.wait()

    return pl.pallas_call(
        body, name=name,
        out_shape=jax.ShapeDtypeStruct((N_DEV,) + block.shape, block.dtype),
        in_specs=[pl.BlockSpec(memory_space=pl.ANY)],
        out_specs=pl.BlockSpec(memory_space=pl.ANY),
        scratch_shapes=[pltpu.SemaphoreType.DMA((7,)), pltpu.SemaphoreType.DMA((7,)), pltpu.SemaphoreType.DMA],
    )(block)


def exchange_slabs(slabs, name):
    def body(in_ref, out_ref, send_sems, recv_sems, local_sem):
        x, y, c = _position()
        my = 4 * x + 2 * y + c
        mine = pltpu.make_async_copy(in_ref.at[my], out_ref.at[my], local_sem)
        mine.start()
        copies = []
        for k in range(1, N_DEV):
            dx, dy, dc = (k >> 2) & 1, (k >> 1) & 1, k & 1
            px = x if dx == 0 else 1 - x
            py = y if dy == 0 else 1 - y
            pc = c if dc == 0 else 1 - c
            cp = pltpu.make_async_remote_copy(
                src_ref=in_ref.at[4 * px + 2 * py + pc], dst_ref=out_ref.at[my],
                send_sem=send_sems.at[k - 1], recv_sem=recv_sems.at[k - 1],
                device_id=(px, py, pc), device_id_type=pl.DeviceIdType.MESH)
            cp.start()
            copies.append(cp)
        for cp in copies:
            cp.wait()
        mine.wait()

    return pl.pallas_call(
        body, name=name,
        out_shape=jax.ShapeDtypeStruct(slabs.shape, slabs.dtype),
        in_specs=[pl.BlockSpec(memory_space=pl.ANY)],
        out_specs=pl.BlockSpec(memory_space=pl.ANY),
        scratch_shapes=[pltpu.SemaphoreType.DMA((7,)), pltpu.SemaphoreType.DMA((7,)), pltpu.SemaphoreType.DMA],
    )(slabs)


def _pack(arrays, dtype, row_multiple):
    flat = jnp.concatenate([a.astype(dtype).reshape(-1) for a in arrays])
    n = flat.shape[0]
    per = PACK_W * row_multiple
    total = -(-n // per) * per
    flat = jnp.pad(flat, (0, total - n))
    return flat.reshape(total // PACK_W, PACK_W)


def _unpack(flat2d, shapes, lead=()):
    flat = flat2d.reshape(lead + (-1,))
    out, off = [], 0
    for shp in shapes:
        n = math.prod(shp)
        out.append(flat[..., off:off + n].reshape(lead + tuple(shp)))
        off += n
    return out


def _full_from_gathered(g, axis):
    g = jnp.moveaxis(g, 0, axis)
    shp = list(g.shape)
    shp[axis:axis + 2] = [shp[axis] * shp[axis + 1]]
    return g.reshape(shp)


def _shards_of_full(full, axis):
    shp = list(full.shape)
    shp[axis:axis + 1] = [N_DEV, shp[axis] // N_DEV]
    return jnp.moveaxis(full.reshape(shp), axis, 0)


def layer_units(i):
    mixer = [('m_in_w', 'm_out_w'), ('h_in_w', 'h_out_w'), ('g_in_w', 'g_out_w')][i % 3]
    return [(mixer[0], i // 3), (mixer[1], i // 3), ('xa_q', i), ('xa_kv', i), ('xa_o', i), ('f_up', i), ('f_down', i)]


PADDED_COLS = {'m_in_w': M_IN_PAD, 'g_in_w': G_IN_PAD}
BIG_WEIGHTS = ('m_in_w', 'h_in_w', 'g_in_w', 'f_up')


def whole_weight(name, gathered):
    w = _full_from_gathered(lax.stop_gradient(gathered), SHARD_AXIS[name] - 1)
    return _pad_cols(w, 1, PADDED_COLS[name]) if name in PADDED_COLS else w


def _trunk(p, weights, blocks, standins, x, mem, bl, seq, pending=None):
    t = bl * seq
    ia = ib = ic = 0
    weights = dict(weights)
    state = {}

    def lin(name, a, wname, idx, residual=None):
        unit = (wname, idx)
        pos = state['units'].index(unit)
        later = state['next'][pos] if state['next'] else None
        nxt = (blocks[later],) if later in blocks and later not in state['by_core'] else ()
        n_real = N_DEV * standins[unit].shape[2]
        res = () if residual is None else (residual,)
        mixer_in = wname in ('m_in_w', 'h_in_w', 'g_in_w')
        defer = (pending, unit) if pending is not None and not (mixer_in and i == 0) else None
        y, got = make_linear(name, SHARD_AXIS[wname] - 1, n_real, bool(res), defer)(
            a, weights[unit], standins[unit], nxt, res)
        if nxt:
            weights[later] = whole_weight(later[0], got[0])
        return y

    by_seq = lambda a: a.reshape(bl, seq, a.shape[-1])

    for i in range(DEPTH):
        state['units'] = layer_units(i)
        state['next'] = layer_units(i + 1) if i + 1 < DEPTH else None
        state['by_core'] = [u for u in (state['next'] or []) if u in blocks and u[0] in BIG_WEIGHTS]
        core_blocks = [blocks[u] for u in state['by_core']]

        def core(op, *args):
            y, *got = op(*args, *core_blocks)
            for u, g in zip(state['by_core'], got):
                weights[u] = whole_weight(u[0], g)
            return y
        hn, x = rmsnorm_op(f"ln_mix{i}", t, F32, residual=True)(x, p['ln_mix'][i:i + 1])
        kind = i % 3
        if kind == 0:
            proj = lin(f"m_in{i}", hn, 'm_in_w', ia)
            xbc = make_conv(f"m_conv{i}", bl, seq, 4, M_CONV_DIM, M_D_INNER)(
                proj, p['m_conv_w'][ia], p['m_conv_b'][ia])
            dt = proj[:, M_D_INNER + M_CONV_DIM:M_IN].reshape(bl, seq, M_GROUPS, 4).transpose(2, 0, 1, 3)
            grp = lambda a, n=4: a.reshape(M_GROUPS, 1, n)
            proj3, xbc3 = by_seq(proj), by_seq(xbc)
            y = core(ssd_op(f"ssd{i}", bl, seq, pending, len(core_blocks)),
                     proj3, xbc3, xbc3, xbc3, dt, grp(p['m_dt_bias'][ia]), grp(p['m_a_log'][ia]), grp(p['m_d'][ia]),
                     grp(p['m_norm_w'][ia], 256))
            x = lin(f"m_out{i}", y.reshape(t, M_D_INNER), 'm_out_w', ia, residual=x)
            ia += 1
        elif kind == 1:
            proj3 = by_seq(lin(f"h_in{i}", hn, 'h_in_w', ib))
            y = core(gla_op(f"gla{i}", i, bl, seq, pending, len(core_blocks)),
                     proj3, proj3, proj3, proj3, p['h_lower_bounds'], p['h_norm_w'][ib:ib + 1])
            x = lin(f"h_out{i}", y.reshape(t, D_MODEL), 'h_out_w', ib, residual=x)
            ib += 1
        else:
            proj = lin(f"g_in{i}", hn, 'g_in_w', ic)
            qkv = make_conv(f"g_conv{i}", bl, seq, 4, G_CONV_DIM, 0)(proj, p['g_conv_w'][ic])
            c0 = G_CONV_DIM + G_VAL_DIM
            heads = lambda a: a.reshape(bl, seq, G_QK_HEADS, 2).transpose(2, 0, 1, 3)
            braw = heads(proj[:, c0:c0 + G_V_HEADS])
            araw = heads(proj[:, c0 + G_V_HEADS:c0 + 2 * G_V_HEADS])
            grp = lambda a: a.reshape(G_QK_HEADS, 1, 2)
            qkv3 = by_seq(qkv)
            y = core(gdn_op(f"gdn{i}", bl, seq, pending, len(core_blocks)),
                     qkv3, qkv3, qkv3, by_seq(proj), braw, araw, grp(p['g_a_log'][ic]), grp(p['g_dt_bias'][ic]),
                     p['g_norm_w'][ic:ic + 1])
            x = lin(f"g_out{i}", y.reshape(t, G_VAL_DIM), 'g_out_w', ic, residual=x)
            ic += 1
        hq, x = rmsnorm_op(f"ln_xattn{i}", t, F32, residual=True)(x, p['ln_xattn'][i:i + 1])
        mn = rmsnorm_op(f"ln_mem{i}", bl * N_MEM, F32)(mem, p['ln_mem'][i:i + 1])[0]
        qx = lin(f"xa_q{i}", hq, 'xa_q', i)
        kv = lin(f"xa_kv{i}", mn, 'xa_kv', i)
        ao = xattn_op(f"xattn{i}", bl, seq)(qx, kv, kv)[0]
        x = lin(f"xa_o{i}", ao, 'xa_o', i, residual=x)
        hf, x = rmsnorm_op(f"ln_ffn{i}", t, F32, residual=True)(x, p['ln_ffn'][i:i + 1])
        up = lin(f"f_up{i}", hf, 'f_up', i)
        act = make_conv(f"f_conv{i}", bl, seq, 3, D_FF, 0, up_col0=D_FF)(up, p['f_conv_w'][i], p['f_conv_b'][i])
        x = lin(f"f_down{i}", act, 'f_down', i, residual=x)
    return x


def _pad_cols(w, axis, to):
    pad = [(0, 0)] * w.ndim
    pad[axis] = (0, to - w.shape[axis])
    return jnp.pad(w, pad)


def kernel(x, mem, ln_mix, ln_xattn, ln_mem, ln_ffn, final_norm, m_in_w, m_conv_w, m_conv_b, m_dt_bias, m_a_log, m_d, m_norm_w, m_out_w, h_in_w, h_lower_bounds, h_norm_w, h_out_w, g_in_w, g_conv_w, g_a_log, g_dt_bias, g_norm_w, g_out_w, xa_q, xa_kv, xa_o, f_up, f_conv_w, f_conv_b, f_down, loss_target, m_ln_mix, m_ln_xattn, m_ln_mem, m_ln_ffn, m_final_norm, m_m_in_w, m_m_conv_w, m_m_conv_b, m_m_dt_bias, m_m_a_log, m_m_d, m_m_norm_w, m_m_out_w, m_h_in_w, m_h_lower_bounds, m_h_norm_w, m_h_out_w, m_g_in_w, m_g_conv_w, m_g_a_log, m_g_dt_bias, m_g_norm_w, m_g_out_w, m_xa_q, m_xa_kv, m_xa_o, m_f_up, m_f_conv_w, m_f_conv_b, m_f_down, v_ln_mix, v_ln_xattn, v_ln_mem, v_ln_ffn, v_final_norm, v_m_in_w, v_m_conv_w, v_m_conv_b, v_m_dt_bias, v_m_a_log, v_m_d, v_m_norm_w, v_m_out_w, v_h_in_w, v_h_lower_bounds, v_h_norm_w, v_h_out_w, v_g_in_w, v_g_conv_w, v_g_a_log, v_g_dt_bias, v_g_norm_w, v_g_out_w, v_xa_q, v_xa_kv, v_xa_o, v_f_up, v_f_conv_w, v_f_conv_b, v_f_down):
    local = dict(ln_mix=ln_mix, ln_xattn=ln_xattn, ln_mem=ln_mem, ln_ffn=ln_ffn, final_norm=final_norm, m_in_w=m_in_w, m_conv_w=m_conv_w, m_conv_b=m_conv_b, m_dt_bias=m_dt_bias, m_a_log=m_a_log, m_d=m_d, m_norm_w=m_norm_w, m_out_w=m_out_w, h_in_w=h_in_w, h_lower_bounds=h_lower_bounds, h_norm_w=h_norm_w, h_out_w=h_out_w, g_in_w=g_in_w, g_conv_w=g_conv_w, g_a_log=g_a_log, g_dt_bias=g_dt_bias, g_norm_w=g_norm_w, g_out_w=g_out_w, xa_q=xa_q, xa_kv=xa_kv, xa_o=xa_o, f_up=f_up, f_conv_w=f_conv_w, f_conv_b=f_conv_b, f_down=f_down)
    mom_m = dict(ln_mix=m_ln_mix, ln_xattn=m_ln_xattn, ln_mem=m_ln_mem, ln_ffn=m_ln_ffn, final_norm=m_final_norm, m_in_w=m_m_in_w, m_conv_w=m_m_conv_w, m_conv_b=m_m_conv_b, m_dt_bias=m_m_dt_bias, m_a_log=m_m_a_log, m_d=m_m_d, m_norm_w=m_m_norm_w, m_out_w=m_m_out_w, h_in_w=m_h_in_w, h_lower_bounds=m_h_lower_bounds, h_norm_w=m_h_norm_w, h_out_w=m_h_out_w, g_in_w=m_g_in_w, g_conv_w=m_g_conv_w, g_a_log=m_g_a_log, g_dt_bias=m_g_dt_bias, g_norm_w=m_g_norm_w, g_out_w=m_g_out_w, xa_q=m_xa_q, xa_kv=m_xa_kv, xa_o=m_xa_o, f_up=m_f_up, f_conv_w=m_f_conv_w, f_conv_b=m_f_conv_b, f_down=m_f_down)
    mom_v = dict(ln_mix=v_ln_mix, ln_xattn=v_ln_xattn, ln_mem=v_ln_mem, ln_ffn=v_ln_ffn, final_norm=v_final_norm, m_in_w=v_m_in_w, m_conv_w=v_m_conv_w, m_conv_b=v_m_conv_b, m_dt_bias=v_m_dt_bias, m_a_log=v_m_a_log, m_d=v_m_d, m_norm_w=v_m_norm_w, m_out_w=v_m_out_w, h_in_w=v_h_in_w, h_lower_bounds=v_h_lower_bounds, h_norm_w=v_h_norm_w, h_out_w=v_h_out_w, g_in_w=v_g_in_w, g_conv_w=v_g_conv_w, g_a_log=v_g_a_log, g_dt_bias=v_g_dt_bias, g_norm_w=v_g_norm_w, g_out_w=v_g_out_w, xa_q=v_xa_q, xa_kv=v_xa_kv, xa_o=v_xa_o, f_up=v_f_up, f_conv_w=v_f_conv_w, f_conv_b=v_f_conv_b, f_down=v_f_down)

    bl, seq, _ = x.shape
    t = bl * seq

    p = {n: local[n] for n in WEIGHTS if n not in SHARD_AXIS}
    for n in SMALL_SHARDED:
        p[n] = _full_from_gathered(all_gather(local[n], f"gather_{n}"), SHARD_AXIS[n])
    units = [(n, l) for n in MATMUL_WEIGHTS for l in range(local[n].shape[0])]
    block = lambda u: local[u[0]][u[1]].astype(BF16)
    weights = {u: whole_weight(u[0], all_gather(block(u), f"gather_{u[0]}{u[1]}")) for u in layer_units(0)}
    blocks = {u: block(u) for u in units if u not in weights}
    standins = {u: jnp.zeros((N_DEV,) + local[u[0]].shape[1:], BF16) for u in units}
    small = {n: p[n] for n in WEIGHTS if n not in MATMUL_WEIGHTS and n != 'final_norm'}

    pending = Pending()

    def run(small_w, standins_, xin):
        return _trunk(small_w, weights, blocks, standins_, xin, mem.reshape(bl * N_MEM, D_MODEL), bl, seq, pending)

    x_out, vjp = jax.vjp(run, small, standins, x.reshape(t, D_MODEL))
    loss_part, dx_out, d_final = loss_head(x_out, final_norm, loss_target.reshape(t, D_MODEL))
    grads, received, dx = vjp(dx_out)
    received = dict(received)
    for unit, slabs in pending.take_all():
        pending.received[unit] = exchange_slabs(slabs, f"exchange_{unit[0]}{unit[1]}")
    received.update(pending.received)
    grads = dict(grads)
    grads['final_norm'] = d_final
    loss = lax.psum(loss_part, ("x", "y", "c"))

    outs = {}

    def update(name, n, slots, shape, sel=lambda a: a):
        two_d = lambda a: sel(a).reshape(slots.shape[1:])
        got = reduce_adamw(slots, two_d(local[n]), two_d(mom_m[n]), two_d(mom_v[n]), name)
        return [g.reshape(shape) for g in got]

    for n in SMALL_SHARDED:
        slots = exchange_slabs(_shards_of_full(grads[n], SHARD_AXIS[n]), f"exchange_{n}")
        slots = slots.reshape(N_DEV, -1, slots.shape[-1])
        for kind, a in zip(KINDS, update(f"adamw_{n}", n, slots, local[n].shape)):
            outs[kind, n] = a
    for n in MATMUL_WEIGHTS:
        per_layer = [update(f"adamw_{n}{l}", n, received[n, l], local[n].shape[1:], lambda a, l=l: a[l])
                     for l in range(local[n].shape[0])]
        for k, kind in enumerate(KINDS):
            outs[kind, n] = jnp.concatenate([got[k][None] for got in per_layer])
    replicated = [n for n in WEIGHTS if n not in SHARD_AXIS]
    pk = lambda d: _pack([d[n] for n in replicated], F32, 8)
    got = reduce_adamw(all_gather(pk(grads), "gather_replicated_grads"), pk(local), pk(mom_m), pk(mom_v),
                       "adamw_replicated")
    shapes = [local[n].shape for n in replicated]
    for kind, buf in zip(KINDS, got):
        for n, a in zip(replicated, _unpack(buf, shapes)):
            outs[kind, n] = a
    result = [loss, dx.reshape(bl, seq, D_MODEL)]
    for kind in KINDS:
        result += [outs[kind, n] for n in WEIGHTS]
    return tuple(result)
```

```python
import functools
import math

import jax
import jax.numpy as jnp
from jax import lax
from jax.experimental import pallas as pl
from jax.experimental.pallas import tpu as pltpu

F32 = jnp.float32
BF16 = jnp.bfloat16
NN = (((1,), (0,)), ((), ()))
NT = (((1,), (1,)), ((), ()))
TN = (((0,), (0,)), ((), ()))

D_MODEL = 1024
DEPTH = 4
EPS = 1e-6
N_MEM = 256
M_D_INNER = 2048
M_HEADS = 32
M_GROUPS = 8
M_STATE = 128
M_CONV_DIM = 4096
M_IN = 6176
M_IN_PAD = 6272
SSD_CHUNK = 256
H_HEADS = 8
HGRN_CHUNK = 32
HGRN_ROWS = 256
G_QK_HEADS = 8
G_V_HEADS = 16
G_KEY_DIM = 1024
G_VAL_DIM = 2048
G_CONV_DIM = 4096
G_IN = 6176
G_IN_PAD = 6272
GDN_CHUNK = 64
X_HEADS = 4
X_HEAD_DIM = 256
D_FF = 2816
ADAM_LR = 0.001
ADAM_B1 = 0.9
ADAM_B2 = 0.999
ADAM_EPS = 1e-08
ADAM_WD = 0.01
ADAM_STEP = 10

N_DEV = 8
LANE = 128
KINDS = ('grad', 'delta', 'new_m', 'new_v')
VMEM_LIMIT = 56 * 1024 * 1024

WEIGHTS = ['ln_mix', 'ln_xattn', 'ln_mem', 'ln_ffn', 'final_norm', 'm_in_w', 'm_conv_w', 'm_conv_b', 'm_dt_bias',
           'm_a_log', 'm_d', 'm_norm_w', 'm_out_w', 'h_in_w', 'h_lower_bounds', 'h_norm_w', 'h_out_w', 'g_in_w',
           'g_conv_w', 'g_a_log', 'g_dt_bias', 'g_norm_w', 'g_out_w', 'xa_q', 'xa_kv', 'xa_o', 'f_up', 'f_conv_w',
           'f_conv_b', 'f_down']
SHARD_AXIS = {'m_in_w': 2, 'm_conv_w': 2, 'm_conv_b': 1, 'm_norm_w': 1, 'm_out_w': 1, 'h_in_w': 2, 'h_out_w': 1,
              'g_in_w': 2, 'g_conv_w': 2, 'g_out_w': 1, 'xa_q': 1, 'xa_kv': 2, 'xa_o': 1, 'f_up': 2, 'f_conv_w': 2,
              'f_down': 1}
MATMUL_WEIGHTS = ['m_in_w', 'm_out_w', 'h_in_w', 'h_out_w', 'g_in_w', 'g_out_w', 'xa_q', 'xa_kv', 'xa_o', 'f_up',
                  'f_down']
SMALL_SHARDED = ['m_conv_w', 'm_conv_b', 'm_norm_w', 'g_conv_w', 'f_conv_w']


def _cparams():
    return pltpu.CompilerParams(vmem_limit_bytes=VMEM_LIMIT)


def bdot(a, b, dims=NN):
    return lax.dot_general(a.astype(BF16), b.astype(BF16), dims, preferred_element_type=F32)


def _split(a):
    hi = a.astype(BF16)
    return hi, (a - hi.astype(F32)).astype(BF16)


def _h3(a, b, dims):
    ah, al = _split(a)
    bh, bl = _split(b)
    d = functools.partial(lax.dot_general, dimension_numbers=dims, preferred_element_type=F32)
    return d(ah, bh) + (d(ah, bl) + d(al, bh))


BNN = (((2,), (1,)), ((0,), (0,)))
BNT = (((2,), (2,)), ((0,), (0,)))
BTN = (((1,), (1,)), ((0,), (0,)))


@jax.custom_vjp
def h3dot_b(a, b):
    return _h3(a, b, BNN)


h3dot_b.defvjp(lambda a, b: (_h3(a, b, BNN), (a, b)),
               lambda res, ct: (_h3(ct, res[1], BNT), _h3(res[0], ct, BTN)))

T_ROWS = (((0,), (1,)), ((), ()))


def _tri_times(tri, x, dims, tri_first):
    t = tri.astype(BF16)
    x0 = x.astype(BF16)
    r1 = x - x0.astype(F32)
    x1 = r1.astype(BF16)
    x2 = (r1 - x1.astype(F32)).astype(BF16)
    if tri_first:
        d = lambda xx: lax.dot_general(t, xx, dims, preferred_element_type=F32)
    else:
        d = lambda xx: lax.dot_general(xx, t, dims, preferred_element_type=F32)
    return d(x0) + (d(x1) + d(x2))


@jax.custom_vjp
def cumdot(tri, x):
    return _tri_times(tri, x, NN, True)


cumdot.defvjp(lambda tri, x: (_tri_times(tri, x, NN, True), tri),
              lambda tri, ct: (jnp.zeros_like(tri), _tri_times(tri, ct, TN, True)))


@jax.custom_vjp
def cumdot_t(tri, x):
    return _tri_times(tri, x, T_ROWS, False)


cumdot_t.defvjp(lambda tri, x: (_tri_times(tri, x, T_ROWS, False), tri),
                lambda tri, ct: (jnp.zeros_like(tri), _tri_times(tri, ct, T_ROWS, True)))


def _tile(dim, cap):
    if dim <= cap:
        return dim
    best = None
    for t in range(LANE, cap + 1, LANE):
        if dim % t == 0:
            best = t
    assert best is not None, dim
    return best


def _position():
    return lax.axis_index("x"), lax.axis_index("y"), lax.axis_index("c")


def _direct_copies(kind, src_ref, dst_ref, send_sems, recv_sems, local_sem):
    x, y, c = _position()
    me = 4 * x + 2 * y + c
    local_src = src_ref if kind == 'gather' else src_ref.at[me]
    copies = [pltpu.make_async_copy(local_src, dst_ref.at[me], local_sem)]
    for k in range(1, N_DEV):
        px = 1 - x if (k >> 2) & 1 else x
        py = 1 - y if (k >> 1) & 1 else y
        pc = 1 - c if k & 1 else c
        copies.append(pltpu.make_async_remote_copy(
            src_ref=src_ref if kind == 'gather' else src_ref.at[4 * px + 2 * py + pc], dst_ref=dst_ref.at[me],
            send_sem=send_sems.at[k - 1], recv_sem=recv_sems.at[k - 1],
            device_id=(px, py, pc), device_id_type=pl.DeviceIdType.MESH))
    return copies


COMM_SCRATCH = [pltpu.SemaphoreType.DMA((N_DEV - 1,)), pltpu.SemaphoreType.DMA((N_DEV - 1,)), pltpu.SemaphoreType.DMA]


class Pending:
    def __init__(self):
        self.jobs, self.received = [], {}

    def take_all(self):
        jobs, self.jobs = self.jobs, []
        return jobs


def matmul(a, b, *, ta=False, tb=False, out_dtype=F32, name="mm", carry=None, residual=None):
    if ta:
        k, m = a.shape
    else:
        m, k = a.shape
    if tb:
        n, k2 = b.shape
    else:
        k2, n = b.shape
    assert k == k2, (a.shape, b.shape, ta, tb)
    tm = _tile(m, 1024)
    tn = _tile(n, 1408)
    tk = _tile(k, 1408)
    grid = (m // tm, n // tn, k // tk)
    nk = grid[2]
    dims = (((0 if ta else 1,), (1 if tb else 0,)), ((), ()))

    def at_step(which):
        conds = [pl.program_id(ax) == (0 if which == 'first' else grid[ax] - 1) for ax in range(3)]
        return jnp.logical_and(jnp.logical_and(conds[0], conds[1]), conds[2])

    def body(*refs):
        r_ref = None
        if residual is not None:
            r_ref, refs = refs[2], refs[:2] + refs[3:]
        if carry is None:
            a_ref, b_ref, o_ref, acc_ref = refs
        else:
            a_ref, b_ref, src_ref, o_ref, dst_ref, acc_ref, send_sems, recv_sems, local_sem = refs
            copies = lambda: _direct_copies(carry[0], src_ref, dst_ref, send_sems, recv_sems, local_sem)

            @pl.when(at_step('first'))
            def _():
                for cp in copies():
                    cp.start()

        @pl.when(pl.program_id(2) == 0)
        def _():
            acc_ref[...] = jnp.zeros_like(acc_ref)

        acc_ref[...] += lax.dot_general(a_ref[...].astype(BF16), b_ref[...].astype(BF16), dims,
                                        preferred_element_type=F32)

        @pl.when(pl.program_id(2) == nk - 1)
        def _():
            out = acc_ref[...] if r_ref is None else acc_ref[...] + r_ref[...]
            o_ref[...] = out.astype(o_ref.dtype)

        if carry is not None:
            @pl.when(at_step('last'))
            def _():
                for cp in copies():
                    cp.wait()

    a_spec = pl.BlockSpec((tk, tm), lambda i, j, kk: (kk, i)) if ta else pl.BlockSpec((tm, tk), lambda i, j, kk: (i, kk))
    b_spec = pl.BlockSpec((tn, tk), lambda i, j, kk: (j, kk)) if tb else pl.BlockSpec((tk, tn), lambda i, j, kk: (kk, j))
    o_spec = pl.BlockSpec((tm, tn), lambda i, j, kk: (i, j))
    o_shape = jax.ShapeDtypeStruct((m, n), out_dtype)
    acc = pltpu.VMEM((tm, tn), F32)
    ins, in_specs = [a, b], [a_spec, b_spec]
    if residual is not None:
        ins.append(residual)
        in_specs.append(o_spec)
    if carry is None:
        return pl.pallas_call(
            body, name=name, grid=grid, in_specs=in_specs, out_specs=o_spec, out_shape=o_shape,
            scratch_shapes=[acc], compiler_params=_cparams(),
        )(*ins)
    kind, src = carry
    got = jax.ShapeDtypeStruct(((N_DEV,) + src.shape) if kind == 'gather' else src.shape, src.dtype)
    hbm = pl.BlockSpec(memory_space=pl.ANY)
    return pl.pallas_call(
        body, name=name, grid=grid, in_specs=in_specs + [hbm], out_specs=[o_spec, hbm],
        out_shape=[o_shape, got], scratch_shapes=[acc] + COMM_SCRATCH, compiler_params=_cparams(),
    )(*ins, src)


def make_linear(name, shard_axis, n_real, has_res=False, defer=None):
    def forward(a, w, nxt, res):
        r = res[0] if res else None
        if nxt:
            y, got = matmul(a, w, name=name + "_fwd", carry=('gather', nxt[0]), residual=r)
            return y, (got,)
        return matmul(a, w, name=name + "_fwd", residual=r), ()

    @jax.custom_vjp
    def linear(a, w, wg, nxt, res):
        return forward(a, w, nxt, res)

    def fwd(a, w, wg, nxt, res):
        return forward(a, w, nxt, res), (a, w, nxt)

    def bwd(saved, cts):
        a, w, nxt = saved
        dy = cts[0]
        dw = matmul(a, dy, ta=True, out_dtype=BF16, name=name + "_bwd_dw")
        slabs = _shards_of_full(dw[:, :n_real], shard_axis)
        if defer is None:
            da, slots = matmul(dy, w, tb=True, out_dtype=a.dtype, name=name + "_bwd_da", carry=('exchange', slabs))
        else:
            defer[0].jobs.append((defer[1], slabs))
            da, slots = matmul(dy, w, tb=True, out_dtype=a.dtype, name=name + "_bwd_da"), jnp.zeros_like(slabs)
        return da, jnp.zeros_like(w), slots, tuple(jnp.zeros_like(b) for b in nxt), ((dy,) if has_res else ())

    linear.defvjp(fwd, bwd)
    return linear


class In:
    def __init__(self, block, imap, kind='blk', inner=(), cols=None):
        self.block, self.imap, self.kind, self.inner, self.cols = block, imap, kind, inner, cols


class Out:
    def __init__(self, shape, dtype, block, imap):
        self.shape, self.dtype, self.block, self.imap = shape, dtype, block, imap


def make_op(name, fn, grid, ins, outs, state_shape=None, seq_axis=None, passthrough=(), pending=None, n_gather=0):
    n_in, n_out = len(ins), len(outs)
    has_state = state_shape is not None
    nd = len(grid)
    diff_idx = [i for i, s in enumerate(ins) if s.kind != 'const']

    def in_spec(s, reverse):
        off = 0
        if s.cols is not None:
            assert s.cols[0] % s.block[-1] == 0
            off = s.cols[0] // s.block[-1]

        def imap(*ids):
            ids = rev(ids) if reverse else ids
            idx = tuple(s.imap(*ids))
            return idx[:-1] + (idx[-1] + off,) if off else idx

        return pl.BlockSpec(s.block, imap)

    def rel_spec(block, f, reverse):
        return pl.BlockSpec(block, (lambda *ids: f(*rev(ids))) if reverse else f)

    def rev(ids):
        if not has_state:
            return ids
        ids = list(ids)
        ids[seq_axis] = grid[seq_axis] - 1 - ids[seq_axis]
        return tuple(ids)

    save_shape = tuple(grid) + tuple(state_shape) if has_state else None
    save_block = (None,) * nd + tuple(state_shape) if has_state else None

    def save_imap(*ids):
        return tuple(ids) + (0,) * len(state_shape)

    def step_is(which):
        conds = [pl.program_id(ax) == (0 if which == 'first' else grid[ax] - 1) for ax in range(nd)]
        return functools.reduce(jnp.logical_and, conds)

    def fwd_call(*xs):
        xs, blocks = xs[:n_in], xs[n_in:]
        n_save = 1 if has_state else 0

        def body(*refs):
            if blocks:
                src_refs = refs[n_in:n_in + n_gather]
                dst_refs = refs[n_in + n_gather + n_out + n_save:n_in + 2 * n_gather + n_out + n_save]
                sems = refs[len(refs) - 3 * n_gather:]
                refs = refs[:n_in] + refs[n_in + n_gather:n_in + n_gather + n_out + n_save] + \
                    refs[n_in + 2 * n_gather + n_out + n_save:len(refs) - 3 * n_gather]
                copies = lambda: [cp for k in range(n_gather) for cp in _direct_copies(
                    'gather', src_refs[k], dst_refs[k], sems[3 * k], sems[3 * k + 1], sems[3 * k + 2])]

                @pl.when(step_is('first'))
                def _():
                    for cp in copies():
                        cp.start()

            in_refs = refs[:n_in]
            out_refs = refs[n_in:n_in + n_out]
            vals = [r[...] for r in in_refs]
            if has_state:
                save_ref, st_ref = refs[n_in + n_out], refs[n_in + n_out + 1]

                @pl.when(pl.program_id(seq_axis) == 0)
                def _():
                    st_ref[...] = jnp.zeros(state_shape, F32)

                st = st_ref[...]
                save_ref[...] = st
                res = fn(*vals, st)
                st_ref[...] = res[-1]
                res = res[:-1]
            else:
                res = fn(*vals)
            for o, v in zip(out_refs, res):
                o[...] = v.astype(o.dtype)

            if blocks:
                @pl.when(step_is('last'))
                def _():
                    for cp in copies():
                        cp.wait()

        out_shape = [jax.ShapeDtypeStruct(o.shape, o.dtype) for o in outs]
        out_specs = [pl.BlockSpec(o.block, o.imap) for o in outs]
        scratch = []
        if has_state:
            out_shape.append(jax.ShapeDtypeStruct(save_shape, F32))
            out_specs.append(pl.BlockSpec(save_block, save_imap))
            scratch.append(pltpu.VMEM(state_shape, F32))
        hbm = pl.BlockSpec(memory_space=pl.ANY)
        for blk in blocks:
            out_shape.append(jax.ShapeDtypeStruct((N_DEV,) + blk.shape, blk.dtype))
            out_specs.append(hbm)
            scratch += COMM_SCRATCH
        return pl.pallas_call(
            body, name=name + "_fwd", grid=grid,
            in_specs=[in_spec(s, False) for s in ins] + [hbm] * len(blocks),
            out_specs=out_specs, out_shape=out_shape, scratch_shapes=scratch,
            compiler_params=_cparams(),
        )(*xs, *blocks)

    def grad_shape(s, x):
        if s.cols is not None:
            return x.shape[:-1] + (s.cols[1],)
        return x.shape

    def bwd_call(xs, save, cts, pass_cts=()):
        n_diff = len(diff_idx)
        jobs = pending.take_all() if pending is not None else []
        n_args = n_in + (1 if has_state else 0) + n_out + len(passthrough)

        def body(*refs):
            if jobs:
                src_refs = refs[n_args:n_args + len(jobs)]
                dst_refs = refs[n_args + len(jobs) + n_diff:n_args + 2 * len(jobs) + n_diff]
                sems = refs[len(refs) - 3 * len(jobs):]
                refs = refs[:n_args] + refs[n_args + len(jobs):n_args + len(jobs) + n_diff] + \
                    refs[n_args + 2 * len(jobs) + n_diff:len(refs) - 3 * len(jobs)]
                copies = lambda: [cp for k in range(len(jobs)) for cp in _direct_copies(
                    'exchange', src_refs[k], dst_refs[k], sems[3 * k], sems[3 * k + 1], sems[3 * k + 2])]

                @pl.when(step_is('first'))
                def _():
                    for cp in copies():
                        cp.start()

            in_refs = refs[:n_in]
            p = n_in
            if has_state:
                save_ref = refs[p]
                p += 1
            ct_refs = refs[p:p + n_out]
            p += n_out
            pass_refs = dict(zip(passthrough, refs[p:p + len(passthrough)]))
            p += len(passthrough)
            g_refs = refs[p:p + n_diff]
            p += n_diff
            vals = [r[...] for r in in_refs]

            def g(*dv):
                full = list(vals)
                for i, v in zip(diff_idx, dv):
                    full[i] = v
                if has_state:
                    return tuple(fn(*full, dv[-1]))
                return tuple(fn(*full))

            prim = [vals[i] for i in diff_idx]
            ct = tuple(r[...].astype(F32) for r in ct_refs)
            if has_state:
                dst_ref = refs[p]

                @pl.when(pl.program_id(seq_axis) == 0)
                def _():
                    dst_ref[...] = jnp.zeros(state_shape, F32)

                prim = prim + [save_ref[...]]
                ct = ct + (dst_ref[...],)
            _, vjp = jax.vjp(g, *prim)
            grads = vjp(ct)
            for k, i in enumerate(diff_idx):
                s = ins[i]
                if s.kind == 'blk':
                    g = grads[k] + pass_refs[i][...] if i in pass_refs else grads[k]
                    g_refs[k][...] = g.astype(g_refs[k].dtype)
                else:
                    first = None
                    for ax in s.inner:
                        c = pl.program_id(ax) == 0
                        first = c if first is None else jnp.logical_and(first, c)

                    @pl.when(first)
                    def _(k=k):
                        g_refs[k][...] = jnp.zeros_like(g_refs[k])

                    g_refs[k][...] += grads[k].astype(g_refs[k].dtype)
            if has_state:
                dst_ref[...] = grads[-1]

            if jobs:
                @pl.when(step_is('last'))
                def _():
                    for cp in copies():
                        cp.wait()

        in_specs = [in_spec(s, True) for s in ins]
        args = list(xs)
        if has_state:
            in_specs.append(rel_spec(save_block, save_imap, True))
            args.append(save)
        for o, c in zip(outs, cts):
            in_specs.append(rel_spec(o.block, o.imap, True))
            args.append(c)
        for i, c in zip(passthrough, pass_cts):
            assert ins[i].kind == 'blk' and ins[i].cols is None
            in_specs.append(rel_spec(ins[i].block, ins[i].imap, True))
            args.append(c)
        out_shape, out_specs = [], []
        for i in diff_idx:
            s = ins[i]
            out_shape.append(jax.ShapeDtypeStruct(grad_shape(s, xs[i]), xs[i].dtype))
            out_specs.append(rel_spec(s.block, s.imap, True))
        scratch = [pltpu.VMEM(state_shape, F32)] if has_state else []
        hbm = pl.BlockSpec(memory_space=pl.ANY)
        for _, slabs in jobs:
            in_specs.append(hbm)
            args.append(slabs)
            out_specs.append(hbm)
            out_shape.append(jax.ShapeDtypeStruct(slabs.shape, slabs.dtype))
            scratch += COMM_SCRATCH
        got = pl.pallas_call(
            body, name=name + "_bwd", grid=grid,
            in_specs=in_specs, out_specs=out_specs, out_shape=out_shape, scratch_shapes=scratch,
            compiler_params=_cparams(),
        )(*args)
        for (unit, _), slots in zip(jobs, got[n_diff:]):
            pending.received[unit] = slots
        return got[:n_diff]

    def results(xs, res):
        gathered = tuple(res[len(res) - n_gather:]) if n_gather else ()
        return tuple(res[:n_out]) + tuple(xs[i] for i in passthrough) + gathered

    @jax.custom_vjp
    def op(*xs):
        return results(xs, fwd_call(*xs))

    def op_fwd(*xs):
        res = fwd_call(*xs)
        return results(xs, res), (xs, res[n_out] if has_state else None)

    def op_bwd(resid, cts):
        xs, save = resid
        xs, blocks = xs[:n_in], xs[n_in:]
        grads = bwd_call(xs, save, cts[:n_out], cts[n_out:n_out + len(passthrough)])
        out = []
        k = 0
        for i, s in enumerate(ins):
            if s.kind == 'const':
                out.append(jnp.zeros_like(xs[i]))
                continue
            g = grads[k]
            k += 1
            if s.cols is not None:
                g = jnp.pad(g, ((0, 0),) * (g.ndim - 1) + ((s.cols[0], xs[i].shape[-1] - s.cols[0] - s.cols[1]),))
            out.append(g)
        return tuple(out) + tuple(jnp.zeros_like(b) for b in blocks)

    op.defvjp(op_fwd, op_bwd)
    return op


def _rms(x, w):
    return x * lax.rsqrt(jnp.mean(x * x, axis=-1, keepdims=True) + EPS) * w


def _silu(x):
    return x * jax.nn.sigmoid(x)


def rmsnorm_op(name, t, out_dtype, residual=False):
    tm = _tile(t, 512)
    return make_op(
        name, lambda x, w: (_rms(x, w),), (t // tm,),
        [In((tm, D_MODEL), lambda i: (i, 0)), In((1, D_MODEL), lambda i: (0, 0), 'acc', (0,))],
        [Out((t, D_MODEL), out_dtype, (tm, D_MODEL), lambda i: (i, 0))], passthrough=(0,) if residual else ())


def _tri(q):
    ii = lax.broadcasted_iota(jnp.int32, (q, q), 0)
    jj = lax.broadcasted_iota(jnp.int32, (q, q), 1)
    return ii >= jj, ii > jj


def _ssd_fn(z, x, bm, cm, dtr, dtb, alog, dsk, nw, state):
    q = x.shape[0]
    incl, _ = _tri(q)
    tril = incl.astype(F32)
    dt = jax.nn.softplus(dtr + dtb)
    da = dt * (-jnp.exp(alog))
    acum = cumdot(tril, da)
    acum_t = cumdot_t(tril, da)
    cb = bdot(cm, bm, NT)
    heads = range(4)
    wide = lambda a: jnp.concatenate([jnp.broadcast_to(a[:, r:r + 1], (a.shape[0], 64)) for r in heads], axis=1)
    last = acum[q - 1:q, :]
    xc = x * wide(dt)
    y = bdot(cm, state, NT) * wide(jnp.exp(acum)) + wide(dsk) * x
    ds = bdot(xc * wide(jnp.exp(last - acum)), bm, TN)
    e_last = jnp.exp(last)
    new_state = state * jnp.concatenate([jnp.broadcast_to(e_last[:, r:r + 1], (64, 1)) for r in heads], axis=0) + ds
    diag = []
    for r in heads:
        decay = jnp.exp(jnp.where(incl, acum[:, r:r + 1] - acum_t[r:r + 1, :], -jnp.inf))
        diag.append(bdot(cb * decay, xc[:, 64 * r:64 * r + 64]))
    y = y + jnp.concatenate(diag, axis=1)
    yz = y * _silu(z)
    return _rms(yz, nw), new_state


def _per_sequence(fn, n_seq_args, bl):
    def f(*args):
        *ins, state = args
        res = [fn(*[a[b] for a in ins[:n_seq_args]], *ins[n_seq_args:], state[b]) for b in range(bl)]
        return tuple(jnp.concatenate([r[k][None] for r in res]) for k in range(len(res[0])))

    return f


def ssd_op(name, bl, seq, pending=None, n_gather=0):
    q = SSD_CHUNK
    nc = seq // q
    blk = lambda w, c0, cw: In((bl, q, w), lambda g, n: (0, n, g), cols=(c0, cw))
    small = lambda g, n: (g, 0, 0)
    ins = [
        blk(256, 0, M_D_INNER),
        blk(256, 0, M_D_INNER),
        blk(128, M_D_INNER, 1024),
        blk(128, M_D_INNER + 1024, 1024),
        In((None, bl, q, 4), lambda g, n: (g, 0, n, 0)),
        In((None, 1, 4), small, 'acc', (1,)),
        In((None, 1, 4), small, 'acc', (1,)),
        In((None, 1, 4), small, 'acc', (1,)),
        In((None, 1, 256), small, 'acc', (1,)),
    ]
    outs = [Out((bl, seq, M_D_INNER), F32, (bl, q, 256), lambda g, n: (0, n, g))]
    return make_op(name, _per_sequence(_ssd_fn, 5, bl), (M_GROUPS, nc), ins, outs,
                   state_shape=(bl, 256, 128), seq_axis=1, pending=pending, n_gather=n_gather)


def _gla_fn(layer, qr, fr, ir, gr, lbp, nw, state_t):
    rows = qr.shape[0]
    c = HGRN_CHUNK
    n_chunks = rows // c
    e = jnp.exp(lbp - jnp.max(lbp, axis=0, keepdims=True))
    sm = e / jnp.sum(e, axis=0, keepdims=True)
    lb = jnp.sum(sm[1:layer + 1, :], axis=0, keepdims=True) if layer > 0 else jnp.zeros((1, lbp.shape[1]), F32)
    qq = _silu(qr) * (128 ** -0.5)
    forget = lb + (1.0 - lb) * jax.nn.sigmoid(fr)
    kk = 1.0 - forget
    logf = jnp.log(forget)
    ii = lax.broadcasted_iota(jnp.int32, (rows, rows), 0)
    jj = lax.broadcasted_iota(jnp.int32, (rows, rows), 1)
    own = jnp.logical_and(ii >= jj, ii // c == jj // c)
    gc = cumdot(own.astype(F32), logf)
    glasts = [gc[c * j + c - 1:c * j + c, :] for j in range(n_chunks)]
    glast_rows = jnp.concatenate([jnp.broadcast_to(g, (c, g.shape[1])) for g in glasts], axis=0)
    q_dec = qq * jnp.exp(gc)
    k_inv = kk * jnp.exp(-gc)
    k_end = kk * jnp.exp(glast_rows - gc)
    att = jnp.where(own, bdot(q_dec, k_inv, NT), 0.0)
    o = bdot(att, ir)
    inter = []
    for j in range(n_chunks):
        sl = slice(c * j, c * j + c)
        inter.append(bdot(q_dec[sl], state_t, NT))
        state_t = state_t * jnp.exp(glasts[j]) + bdot(ir[sl], k_end[sl], TN)
    o = o + jnp.concatenate(inter, axis=0)
    return _rms(o, nw) * _silu(gr), state_t


def gla_op(name, layer, bl, seq, pending=None, n_gather=0):
    r = HGRN_ROWS
    ns = seq // r
    blk = lambda k: In((bl, r, 128), lambda h, n: (0, n, h), cols=(1024 * k, 1024))
    ins = [blk(0), blk(1), blk(2), blk(3),
           In((DEPTH, 128), lambda h, n: (0, h), 'acc', (1,)),
           In((1, 128), lambda h, n: (0, 0), 'acc', (0, 1))]
    outs = [Out((bl, seq, D_MODEL), F32, (bl, r, 128), lambda h, n: (0, n, h))]
    return make_op(name, _per_sequence(functools.partial(_gla_fn, layer), 4, bl), (H_HEADS, ns), ins, outs,
                   state_shape=(bl, 128, 128), seq_axis=1, pending=pending, n_gather=n_gather)


def _neumann_inverse(m):
    q = m.shape[1]
    ii = lax.broadcasted_iota(jnp.int32, (q, q), 0)
    jj = lax.broadcasted_iota(jnp.int32, (q, q), 1)
    eye = (ii == jj).astype(F32)[None]
    p = -m
    inv = eye + p
    for _ in range(int(math.log2(q)) - 1):
        p = _h3(p, p, BNN)
        inv = inv + _h3(inv, p, BNN)
    return inv


@jax.custom_vjp
def _unit_lower_inverse(m):
    return _neumann_inverse(m)


def _unit_lower_inverse_fwd(m):
    inv = _neumann_inverse(m)
    return inv, inv


_unit_lower_inverse.defvjp(_unit_lower_inverse_fwd,
                           lambda inv, ct: (-_h3(_h3(inv, ct, BTN), inv, BNT),))


def _gdn_fn(qc, kc, vc, zc, br, ar, alog, dtb, nw, state):
    bl, q = qc.shape[0], qc.shape[1]
    incl, strict = _tri(q)
    tril = incl.astype(F32)
    g = jnp.concatenate([-jnp.exp(alog) * jax.nn.softplus(ar[b] + dtb) for b in range(bl)], axis=1)
    gc = cumdot(tril, g)
    gc_t = cumdot_t(tril, g)
    heads, ms, rhs = [], [], []
    for b in range(bl):
        qn = qc[b] * lax.rsqrt(jnp.sum(qc[b] * qc[b], axis=-1, keepdims=True) + EPS) * (128 ** -0.5)
        kn = kc[b] * lax.rsqrt(jnp.sum(kc[b] * kc[b], axis=-1, keepdims=True) + EPS)
        beta = jax.nn.sigmoid(br[b])
        qk = bdot(qn, kn, NT)
        for j in range(2):
            i = 2 * b + j
            col = gc[:, i:i + 1]
            decay = jnp.exp(jnp.where(incl, col - gc_t[i:i + 1, :], -jnp.inf))
            bj = beta[:, j:j + 1]
            kb = kn * bj
            ms.append(jnp.where(strict, bdot(kb, kn, NT) * decay, 0.0))
            rhs.append(jnp.concatenate([vc[b][:, 128 * j:128 * j + 128] * bj, kb * jnp.exp(col)], axis=1))
            heads.append((qn, kn, qk * decay, col, gc[q - 1:q, i:i + 1]))
    sol = h3dot_b(_unit_lower_inverse(jnp.concatenate([m[None] for m in ms])),
                  jnp.concatenate([r[None] for r in rhs]))
    outs, states = [], []
    for b in range(bl):
        os_, sts = [], []
        for j in range(2):
            i = 2 * b + j
            qn, kn, att, col, glast = heads[i]
            u = sol[i][:, :128]
            w = sol[i][:, 128:]
            st = state[b][128 * j:128 * j + 128, :]
            v_new = u - bdot(w, st)
            o = bdot(qn * jnp.exp(col), st) + bdot(att, v_new)
            sts.append(st * jnp.exp(glast) + bdot(kn * jnp.exp(glast - col), v_new, TN))
            os_.append(_rms(o, nw) * _silu(zc[b][:, 128 * j:128 * j + 128]))
        outs.append(jnp.concatenate(os_, axis=1))
        states.append(jnp.concatenate(sts, axis=0))
    return jnp.concatenate([o[None] for o in outs]), jnp.concatenate([st[None] for st in states])


def gdn_op(name, bl, seq, pending=None, n_gather=0):
    q = GDN_CHUNK
    nc = seq // q
    blk = lambda w, c0, cw: In((bl, q, w), lambda h, n: (0, n, h), cols=(c0, cw))
    small = lambda h, n: (h, 0, 0)
    ins = [
        blk(128, 0, G_KEY_DIM),
        blk(128, G_KEY_DIM, G_KEY_DIM),
        blk(256, 2 * G_KEY_DIM, G_VAL_DIM),
        blk(256, G_CONV_DIM, G_VAL_DIM),
        In((None, bl, q, 2), lambda h, n: (h, 0, n, 0)),
        In((None, bl, q, 2), lambda h, n: (h, 0, n, 0)),
        In((None, 1, 2), small, 'acc', (1,)),
        In((None, 1, 2), small, 'acc', (1,)),
        In((1, 128), lambda h, n: (0, 0), 'acc', (0, 1)),
    ]
    outs = [Out((bl, seq, G_VAL_DIM), F32, (bl, q, 256), lambda h, n: (0, n, h))]
    return make_op(name, _gdn_fn, (G_QK_HEADS, nc), ins, outs,
                   state_shape=(bl, 256, 128), seq_axis=1, pending=pending, n_gather=n_gather)


def _xattn_fn(q, k, v):
    s = bdot(q, k, NT) * (X_HEAD_DIM ** -0.5)
    s = s - jnp.max(s, axis=-1, keepdims=True)
    p = jnp.exp(s)
    p = p / jnp.sum(p, axis=-1, keepdims=True)
    return (bdot(p, v),)


def xattn_op(name, bl, seq):
    tq = _tile(seq, 2048)
    nq = seq // tq
    t = bl * seq
    ins = [
        In((tq, X_HEAD_DIM), lambda b, h, i: (b * nq + i, h)),
        In((N_MEM, X_HEAD_DIM), lambda b, h, i: (b, h), 'acc', (2,), cols=(0, D_MODEL)),
        In((N_MEM, X_HEAD_DIM), lambda b, h, i: (b, h), 'acc', (2,), cols=(D_MODEL, D_MODEL)),
    ]
    outs = [Out((t, D_MODEL), F32, (tq, X_HEAD_DIM), lambda b, h, i: (b * nq + i, h))]
    return make_op(name, _xattn_fn, (bl, X_HEADS, nq), ins, outs)


CONV_PAD = 8
CONV_ROWS = 128


def make_conv(name, bl, seq, width, ch, x_col0, up_col0=None):
    cb = 256
    rt = CONV_ROWS
    assert ch % cb == 0 and x_col0 % cb == 0 and (up_col0 is None or up_col0 % cb == 0) and seq % rt == 0
    nb = ch // cb
    n_tiles = seq // rt
    t = bl * seq
    has_up = up_col0 is not None
    grid = (nb, bl)
    x_spec = pl.BlockSpec((seq, cb), lambda c, b: (b, x_col0 // cb + c))
    up_specs = [pl.BlockSpec((seq, cb), lambda c, b: (b, up_col0 // cb + c))] if has_up else []
    w_spec = pl.BlockSpec((width, cb), lambda c, b: (0, c))
    b_spec = pl.BlockSpec((1, cb), lambda c, b: (0, c))
    o_spec = pl.BlockSpec((seq, cb), lambda c, b: (b, c))
    taps = [CONV_PAD - (width - 1) + j for j in range(width)]

    def window(x_ref, i):
        if isinstance(i, int) and i == 0:
            return jnp.concatenate([jnp.zeros((CONV_PAD, cb), F32), x_ref[0:rt, :]], axis=0)
        return x_ref[pl.ds(pl.multiple_of(i * rt - CONV_PAD, CONV_PAD), rt + CONV_PAD), :]

    def rows(i):
        return pl.ds(i * rt, rt) if isinstance(i, int) else pl.ds(pl.multiple_of(i * rt, rt), rt)

    def shifted(win):
        return [win[tp:tp + rt, :] for tp in taps]

    def pre_activation(views, w, b):
        y = b + w[0:1, :] * views[0]
        for j in range(1, width):
            y = y + w[j:j + 1, :] * views[j]
        return y

    def over_tiles(step, carry):
        carry = step(0, carry)
        return lax.fori_loop(1, n_tiles, step, carry)

    def fwd_call(x, w, b):
        def body(*refs):
            x_ref, w_ref, b_ref = refs[:3]
            o_ref = refs[-1]
            w_, b_ = w_ref[...], b_ref[...]

            def step(i, carry):
                y = _silu(pre_activation(shifted(window(x_ref, i)), w_, b_))
                if has_up:
                    y = y * refs[3][rows(i), :]
                o_ref[rows(i), :] = y
                return carry

            over_tiles(step, 0)

        return pl.pallas_call(
            body, name=name + "_fwd", grid=grid,
            in_specs=[x_spec, w_spec, b_spec] + up_specs, out_specs=o_spec,
            out_shape=jax.ShapeDtypeStruct((t, ch), F32),
            compiler_params=_cparams(),
        )(*([x, w, b] + ([x] if has_up else [])))

    def bwd_call(x, w, b, do):
        n_in = 4 + (1 if has_up else 0)

        def body(*refs):
            x_ref, w_ref, b_ref = refs[:3]
            do_ref = refs[n_in - 1]
            dx_ref, dw_ref, db_ref = refs[n_in:n_in + 3]
            gpad_ref = refs[-1]
            w_, b_ = w_ref[...], b_ref[...]

            def fold(a):
                acc = a[0:8, :]
                for k in range(1, rt // 8):
                    acc = acc + a[8 * k:8 * k + 8, :]
                return acc

            def grad_pre(i, sums):
                views = shifted(window(x_ref, i))
                y = pre_activation(views, w_, b_)
                s = jax.nn.sigmoid(y)
                act = y * s
                do_ = do_ref[rows(i), :]
                if has_up:
                    refs[n_in + 3][rows(i), :] = do_ * act
                    do_ = do_ * refs[3][rows(i), :]
                dy = do_ * (s + act * (1.0 - s))
                gpad_ref[rows(i), :] = dy
                new = [sums[j] + fold(dy * views[j]) for j in range(width)]
                return tuple(new) + (sums[width] + fold(dy),)

            zero8 = jnp.zeros((8, cb), F32)
            sums = over_tiles(grad_pre, (zero8,) * (width + 1))
            gpad_ref[seq:seq + CONV_PAD, :] = jnp.zeros((CONV_PAD, cb), F32)

            def grad_x(i, carry):
                if isinstance(i, int):
                    gwin = gpad_ref[0:rt + CONV_PAD, :]
                else:
                    gwin = gpad_ref[pl.ds(pl.multiple_of(i * rt, rt), rt + CONV_PAD), :]
                dx = w_[0:1, :] * gwin[width - 1:width - 1 + rt, :]
                for j in range(1, width):
                    dx = dx + w_[j:j + 1, :] * gwin[width - 1 - j:width - 1 - j + rt, :]
                dx_ref[rows(i), :] = dx
                return carry

            over_tiles(grad_x, 0)

            @pl.when(pl.program_id(1) == 0)
            def _():
                dw_ref[...] = jnp.zeros_like(dw_ref)
                db_ref[...] = jnp.zeros_like(db_ref)

            dw_ref[...] += jnp.concatenate([jnp.sum(sums[j], axis=0, keepdims=True) for j in range(width)], axis=0)
            db_ref[...] += jnp.sum(sums[width], axis=0, keepdims=True)

        big = jax.ShapeDtypeStruct((t, ch), F32)
        return pl.pallas_call(
            body, name=name + "_bwd", grid=grid,
            in_specs=[x_spec, w_spec, b_spec] + up_specs + [o_spec],
            out_specs=[o_spec, w_spec, b_spec] + ([o_spec] if has_up else []),
            out_shape=[big, jax.ShapeDtypeStruct((width, ch), F32), jax.ShapeDtypeStruct((1, ch), F32)]
            + ([big] if has_up else []),
            scratch_shapes=[pltpu.VMEM((seq + CONV_PAD, cb), F32)],
            compiler_params=_cparams(),
        )(*([x, w, b] + ([x] if has_up else []) + [do]))

    @jax.custom_vjp
    def conv(x, w, b):
        return fwd_call(x, w, b)

    def conv_fwd(x, w, b):
        return fwd_call(x, w, b), (x, w, b)

    def conv_bwd(res, do):
        x, w, b = res
        got = bwd_call(x, w, b, do)
        dx = jnp.pad(got[0], ((0, 0), (x_col0, x.shape[1] - x_col0 - ch)))
        if has_up:
            dx = dx + jnp.pad(got[3], ((0, 0), (up_col0, x.shape[1] - up_col0 - ch)))
        return dx, got[1], got[2]

    conv.defvjp(conv_fwd, conv_bwd)

    def apply(x, w, b=None):
        if b is None:
            b = jnp.zeros((ch,), F32)
        return conv(x, w, b.reshape(1, ch))

    return apply


def loss_head(x, w, target):
    t = x.shape[0]
    tm = _tile(t, 512)

    def fn(xb, wb, tb):
        err = _rms(xb, wb) - tb
        return 0.5 * jnp.sum(err * err) * (1.0 / D_MODEL)

    def body(x_ref, w_ref, t_ref, loss_ref, dx_ref, dw_ref):
        @pl.when(pl.program_id(0) == 0)
        def _():
            loss_ref[...] = jnp.zeros_like(loss_ref)
            dw_ref[...] = jnp.zeros_like(dw_ref)

        tb = t_ref[...]
        val, vjp = jax.vjp(lambda a, b: fn(a, b, tb), x_ref[...], w_ref[...])
        dx, dw = vjp(jnp.ones((), F32))
        dx_ref[...] = dx
        dw_ref[...] += dw
        loss_ref[...] += jnp.full(loss_ref.shape, val, F32)

    row = pl.BlockSpec((tm, D_MODEL), lambda i: (i, 0))
    vec = pl.BlockSpec((1, D_MODEL), lambda i: (0, 0))
    loss, dx, dw = pl.pallas_call(
        body, name="loss_head", grid=(t // tm,),
        in_specs=[row, vec, row],
        out_specs=[pl.BlockSpec((8, LANE), lambda i: (0, 0)), row, vec],
        out_shape=[jax.ShapeDtypeStruct((8, LANE), F32), jax.ShapeDtypeStruct((t, D_MODEL), F32),
                   jax.ShapeDtypeStruct((1, D_MODEL), F32)],
        compiler_params=_cparams(),
    )(x, w.reshape(1, D_MODEL), target)
    return loss[0, 0], dx, dw.reshape(D_MODEL)


PACK_W = 1024
ADAM_BLOCK_BYTES = 512 * 1024


def _rows_tile(r, c):
    if r * c * 4 <= ADAM_BLOCK_BYTES or r % 8:
        return r
    best = 8
    for t in range(8, r + 1, 8):
        if r % t == 0 and t * c * 4 <= ADAM_BLOCK_BYTES:
            best = t
    return best


def reduce_adamw(slots, w, m, v, name):
    r, wd = w.shape
    tr = _rows_tile(r, wd)
    c1 = 1.0 - ADAM_B1 ** ADAM_STEP
    c2 = 1.0 - ADAM_B2 ** ADAM_STEP

    def body(s_ref, w_ref, m_ref, v_ref, g_ref, d_ref, nm_ref, nv_ref):
        g = s_ref[0].astype(F32)
        for k in range(1, N_DEV):
            g = g + s_ref[k].astype(F32)
        nm = ADAM_B1 * m_ref[...] + (1.0 - ADAM_B1) * g
        nv = ADAM_B2 * v_ref[...] + (1.0 - ADAM_B2) * (g * g)
        m_hat = nm / c1
        v_hat = nv / c2
        d_ref[...] = -ADAM_LR * (m_hat / (jnp.sqrt(v_hat) + ADAM_EPS) + ADAM_WD * w_ref[...])
        g_ref[...] = g
        nm_ref[...] = nm
        nv_ref[...] = nv

    blk = pl.BlockSpec((tr, wd), lambda i: (i, 0))
    shp = jax.ShapeDtypeStruct((r, wd), F32)
    return pl.pallas_call(
        body, name=name, grid=(r // tr,),
        in_specs=[pl.BlockSpec((N_DEV, tr, wd), lambda i: (0, i, 0)), blk, blk, blk],
        out_specs=[blk, blk, blk, blk], out_shape=[shp, shp, shp, shp],
        compiler_params=_cparams(),
    )(slots, w, m, v)


def all_gather(block, name):
    def body(x_ref, out_ref, send_sems, recv_sems, local_sem):
        x, y, c = _position()
        me, sibling = (x, y, c), (x, y, 1 - c)
        chips = [(1 - x, y), (x, 1 - y), (1 - x, 1 - y)]

        def slot(px, py, pc):
            return out_ref.at[4 * px + 2 * py + pc]

        def copy(k, owner, to, src=None):
            return pltpu.make_async_remote_copy(
                src_ref=slot(*owner) if src is None else src, dst_ref=slot(*owner),
                send_sem=send_sems.at[k], recv_sem=recv_sems.at[k],
                device_id=to, device_id_type=pl.DeviceIdType.MESH)

        mine = pltpu.make_async_copy(x_ref, slot(*me), local_sem)
        mine.start()
        first = [copy(0, me, sibling, src=x_ref)]
        first += [copy(1 + j, me, (*chip, c), src=x_ref) for j, chip in enumerate(chips)]
        for cp in first:
            cp.start()
        passed = [copy(4 + j, (*chip, c), sibling) for j, chip in enumerate(chips)]
        for j, chip in enumerate(chips):
            copy(1 + j, (*chip, c), me).wait_recv()
            passed[j].start()
        copy(0, sibling, me).wait_recv()
        for j, chip in enumerate(chips):
            copy(4 + j, (*chip, 1 - c), me).wait_recv()
        for cp in first + passed:
            cp.wait_send()
        mine.wait()

    return pl.pallas_call(
        body, name=name,
        out_shape=jax.ShapeDtypeStruct((N_DEV,) + block.shape, block.dtype),
        in_specs=[pl.BlockSpec(memory_space=pl.ANY)],
        out_specs=pl.BlockSpec(memory_space=pl.ANY),
        scratch_shapes=[pltpu.SemaphoreType.DMA((7,)), pltpu.SemaphoreType.DMA((7,)), pltpu.SemaphoreType.DMA],
    )(block)


def exchange_slabs(slabs, name):
    def body(in_ref, out_ref, send_sems, recv_sems, local_sem):
        x, y, c = _position()
        my = 4 * x + 2 * y + c
        mine = pltpu.make_async_copy(in_ref.at[my], out_ref.at[my], local_sem)
        mine.start()
        copies = []
        for k in range(1, N_DEV):
            dx, dy, dc = (k >> 2) & 1, (k >> 1) & 1, k & 1
            px = x if dx == 0 else 1 - x
            py = y if dy == 0 else 1 - y
            pc = c if dc == 0 else 1 - c
            cp = pltpu.make_async_remote_copy(
                src_ref=in_ref.at[4 * px + 2 * py + pc], dst_ref=out_ref.at[my],
                send_sem=send_sems.at[k - 1], recv_sem=recv_sems.at[k - 1],
                device_id=(px, py, pc), device_id_type=pl.DeviceIdType.MESH)
            cp.start()
            copies.append(cp)
        for cp in copies:
            cp.wait()
        mine.wait()

    return pl.pallas_call(
        body, name=name,
        out_shape=jax.ShapeDtypeStruct(slabs.shape, slabs.dtype),
        in_specs=[pl.BlockSpec(memory_space=pl.ANY)],
        out_specs=pl.BlockSpec(memory_space=pl.ANY),
        scratch_shapes=[pltpu.SemaphoreType.DMA((7,)), pltpu.SemaphoreType.DMA((7,)), pltpu.SemaphoreType.DMA],
    )(slabs)


def _pack(arrays, dtype, row_multiple):
    flat = jnp.concatenate([a.astype(dtype).reshape(-1) for a in arrays])
    n = flat.shape[0]
    per = PACK_W * row_multiple
    total = -(-n // per) * per
    flat = jnp.pad(flat, (0, total - n))
    return flat.reshape(total // PACK_W, PACK_W)


def _unpack(flat2d, shapes, lead=()):
    flat = flat2d.reshape(lead + (-1,))
    out, off = [], 0
    for shp in shapes:
        n = math.prod(shp)
        out.append(flat[..., off:off + n].reshape(lead + tuple(shp)))
        off += n
    return out


def _full_from_gathered(g, axis):
    g = jnp.moveaxis(g, 0, axis)
    shp = list(g.shape)
    shp[axis:axis + 2] = [shp[axis] * shp[axis + 1]]
    return g.reshape(shp)


def _shards_of_full(full, axis):
    shp = list(full.shape)
    shp[axis:axis + 1] = [N_DEV, shp[axis] // N_DEV]
    return jnp.moveaxis(full.reshape(shp), axis, 0)


def layer_units(i):
    mixer = [('m_in_w', 'm_out_w'), ('h_in_w', 'h_out_w'), ('g_in_w', 'g_out_w')][i % 3]
    return [(mixer[0], i // 3), (mixer[1], i // 3), ('xa_q', i), ('xa_kv', i), ('xa_o', i), ('f_up', i), ('f_down', i)]


PADDED_COLS = {'m_in_w': M_IN_PAD, 'g_in_w': G_IN_PAD}
BIG_WEIGHTS = ('m_in_w', 'h_in_w', 'g_in_w', 'f_up')


def whole_weight(name, gathered):
    w = _full_from_gathered(lax.stop_gradient(gathered), SHARD_AXIS[name] - 1)
    return _pad_cols(w, 1, PADDED_COLS[name]) if name in PADDED_COLS else w


def _trunk(p, weights, blocks, standins, x, mem, bl, seq, pending=None):
    t = bl * seq
    ia = ib = ic = 0
    weights = dict(weights)
    state = {}

    def lin(name, a, wname, idx, residual=None):
        unit = (wname, idx)
        pos = state['units'].index(unit)
        later = state['next'][pos] if state['next'] else None
        nxt = (blocks[later],) if later in blocks and later not in state['by_core'] else ()
        n_real = N_DEV * standins[unit].shape[2]
        res = () if residual is None else (residual,)
        mixer_in = wname in ('m_in_w', 'h_in_w', 'g_in_w')
        defer = (pending, unit) if pending is not None and not (mixer_in and i == 0) else None
        y, got = make_linear(name, SHARD_AXIS[wname] - 1, n_real, bool(res), defer)(
            a, weights[unit], standins[unit], nxt, res)
        if nxt:
            weights[later] = whole_weight(later[0], got[0])
        return y

    by_seq = lambda a: a.reshape(bl, seq, a.shape[-1])

    for i in range(DEPTH):
        state['units'] = layer_units(i)
        state['next'] = layer_units(i + 1) if i + 1 < DEPTH else None
        state['by_core'] = [u for u in (state['next'] or []) if u in blocks and u[0] in BIG_WEIGHTS]
        core_blocks = [blocks[u] for u in state['by_core']]

        def core(op, *args):
            y, *got = op(*args, *core_blocks)
            for u, g in zip(state['by_core'], got):
                weights[u] = whole_weight(u[0], g)
            return y
        hn, x = rmsnorm_op(f"ln_mix{i}", t, F32, residual=True)(x, p['ln_mix'][i:i + 1])
        kind = i % 3
        if kind == 0:
            proj = lin(f"m_in{i}", hn, 'm_in_w', ia)
            xbc = make_conv(f"m_conv{i}", bl, seq, 4, M_CONV_DIM, M_D_INNER)(
                proj, p['m_conv_w'][ia], p['m_conv_b'][ia])
            dt = proj[:, M_D_INNER + M_CONV_DIM:M_IN].reshape(bl, seq, M_GROUPS, 4).transpose(2, 0, 1, 3)
            grp = lambda a, n=4: a.reshape(M_GROUPS, 1, n)
            proj3, xbc3 = by_seq(proj), by_seq(xbc)
            y = core(ssd_op(f"ssd{i}", bl, seq, pending, len(core_blocks)),
                     proj3, xbc3, xbc3, xbc3, dt, grp(p['m_dt_bias'][ia]), grp(p['m_a_log'][ia]), grp(p['m_d'][ia]),
                     grp(p['m_norm_w'][ia], 256))
            x = lin(f"m_out{i}", y.reshape(t, M_D_INNER), 'm_out_w', ia, residual=x)
            ia += 1
        elif kind == 1:
            proj3 = by_seq(lin(f"h_in{i}", hn, 'h_in_w', ib))
            y = core(gla_op(f"gla{i}", i, bl, seq, pending, len(core_blocks)),
                     proj3, proj3, proj3, proj3, p['h_lower_bounds'], p['h_norm_w'][ib:ib + 1])
            x = lin(f"h_out{i}", y.reshape(t, D_MODEL), 'h_out_w', ib, residual=x)
            ib += 1
        else:
            proj = lin(f"g_in{i}", hn, 'g_in_w', ic)
            qkv = make_conv(f"g_conv{i}", bl, seq, 4, G_CONV_DIM, 0)(proj, p['g_conv_w'][ic])
            c0 = G_CONV_DIM + G_VAL_DIM
            heads = lambda a: a.reshape(bl, seq, G_QK_HEADS, 2).transpose(2, 0, 1, 3)
            braw = heads(proj[:, c0:c0 + G_V_HEADS])
            araw = heads(proj[:, c0 + G_V_HEADS:c0 + 2 * G_V_HEADS])
            grp = lambda a: a.reshape(G_QK_HEADS, 1, 2)
            qkv3 = by_seq(qkv)
            y = core(gdn_op(f"gdn{i}", bl, seq, pending, len(core_blocks)),
                     qkv3, qkv3, qkv3, by_seq(proj), braw, araw, grp(p['g_a_log'][ic]), grp(p['g_dt_bias'][ic]),
                     p['g_norm_w'][ic:ic + 1])
            x = lin(f"g_out{i}", y.reshape(t, G_VAL_DIM), 'g_out_w', ic, residual=x)
            ic += 1
        hq, x = rmsnorm_op(f"ln_xattn{i}", t, F32, residual=True)(x, p['ln_xattn'][i:i + 1])
        mn = rmsnorm_op(f"ln_mem{i}", bl * N_MEM, F32)(mem, p['ln_mem'][i:i + 1])[0]
        qx = lin(f"xa_q{i}", hq, 'xa_q', i)
        kv = lin(f"xa_kv{i}", mn, 'xa_kv', i)
        ao = xattn_op(f"xattn{i}", bl, seq)(qx, kv, kv)[0]
        x = lin(f"xa_o{i}", ao, 'xa_o', i, residual=x)
        hf, x = rmsnorm_op(f"ln_ffn{i}", t, F32, residual=True)(x, p['ln_ffn'][i:i + 1])
        up = lin(f"f_up{i}", hf, 'f_up', i)
        act = make_conv(f"f_conv{i}", bl, seq, 3, D_FF, 0, up_col0=D_FF)(up, p['f_conv_w'][i], p['f_conv_b'][i])
        x = lin(f"f_down{i}", act, 'f_down', i, residual=x)
    return x


def _pad_cols(w, axis, to):
    pad = [(0, 0)] * w.ndim
    pad[axis] = (0, to - w.shape[axis])
    return jnp.pad(w, pad)


def kernel(x, mem, ln_mix, ln_xattn, ln_mem, ln_ffn, final_norm, m_in_w, m_conv_w, m_conv_b, m_dt_bias, m_a_log, m_d, m_norm_w, m_out_w, h_in_w, h_lower_bounds, h_norm_w, h_out_w, g_in_w, g_conv_w, g_a_log, g_dt_bias, g_norm_w, g_out_w, xa_q, xa_kv, xa_o, f_up, f_conv_w, f_conv_b, f_down, loss_target, m_ln_mix, m_ln_xattn, m_ln_mem, m_ln_ffn, m_final_norm, m_m_in_w, m_m_conv_w, m_m_conv_b, m_m_dt_bias, m_m_a_log, m_m_d, m_m_norm_w, m_m_out_w, m_h_in_w, m_h_lower_bounds, m_h_norm_w, m_h_out_w, m_g_in_w, m_g_conv_w, m_g_a_log, m_g_dt_bias, m_g_norm_w, m_g_out_w, m_xa_q, m_xa_kv, m_xa_o, m_f_up, m_f_conv_w, m_f_conv_b, m_f_down, v_ln_mix, v_ln_xattn, v_ln_mem, v_ln_ffn, v_final_norm, v_m_in_w, v_m_conv_w, v_m_conv_b, v_m_dt_bias, v_m_a_log, v_m_d, v_m_norm_w, v_m_out_w, v_h_in_w, v_h_lower_bounds, v_h_norm_w, v_h_out_w, v_g_in_w, v_g_conv_w, v_g_a_log, v_g_dt_bias, v_g_norm_w, v_g_out_w, v_xa_q, v_xa_kv, v_xa_o, v_f_up, v_f_conv_w, v_f_conv_b, v_f_down):
    local = dict(ln_mix=ln_mix, ln_xattn=ln_xattn, ln_mem=ln_mem, ln_ffn=ln_ffn, final_norm=final_norm, m_in_w=m_in_w, m_conv_w=m_conv_w, m_conv_b=m_conv_b, m_dt_bias=m_dt_bias, m_a_log=m_a_log, m_d=m_d, m_norm_w=m_norm_w, m_out_w=m_out_w, h_in_w=h_in_w, h_lower_bounds=h_lower_bounds, h_norm_w=h_norm_w, h_out_w=h_out_w, g_in_w=g_in_w, g_conv_w=g_conv_w, g_a_log=g_a_log, g_dt_bias=g_dt_bias, g_norm_w=g_norm_w, g_out_w=g_out_w, xa_q=xa_q, xa_kv=xa_kv, xa_o=xa_o, f_up=f_up, f_conv_w=f_conv_w, f_conv_b=f_conv_b, f_down=f_down)
    mom_m = dict(ln_mix=m_ln_mix, ln_xattn=m_ln_xattn, ln_mem=m_ln_mem, ln_ffn=m_ln_ffn, final_norm=m_final_norm, m_in_w=m_m_in_w, m_conv_w=m_m_conv_w, m_conv_b=m_m_conv_b, m_dt_bias=m_m_dt_bias, m_a_log=m_m_a_log, m_d=m_m_d, m_norm_w=m_m_norm_w, m_out_w=m_m_out_w, h_in_w=m_h_in_w, h_lower_bounds=m_h_lower_bounds, h_norm_w=m_h_norm_w, h_out_w=m_h_out_w, g_in_w=m_g_in_w, g_conv_w=m_g_conv_w, g_a_log=m_g_a_log, g_dt_bias=m_g_dt_bias, g_norm_w=m_g_norm_w, g_out_w=m_g_out_w, xa_q=m_xa_q, xa_kv=m_xa_kv, xa_o=m_xa_o, f_up=m_f_up, f_conv_w=m_f_conv_w, f_conv_b=m_f_conv_b, f_down=m_f_down)
    mom_v = dict(ln_mix=v_ln_mix, ln_xattn=v_ln_xattn, ln_mem=v_ln_mem, ln_ffn=v_ln_ffn, final_norm=v_final_norm, m_in_w=v_m_in_w, m_conv_w=v_m_conv_w, m_conv_b=v_m_conv_b, m_dt_bias=v_m_dt_bias, m_a_log=v_m_a_log, m_d=v_m_d, m_norm_w=v_m_norm_w, m_out_w=v_m_out_w, h_in_w=v_h_in_w, h_lower_bounds=v_h_lower_bounds, h_norm_w=v_h_norm_w, h_out_w=v_h_out_w, g_in_w=v_g_in_w, g_conv_w=v_g_conv_w, g_a_log=v_g_a_log, g_dt_bias=v_g_dt_bias, g_norm_w=v_g_norm_w, g_out_w=v_g_out_w, xa_q=v_xa_q, xa_kv=v_xa_kv, xa_o=v_xa_o, f_up=v_f_up, f_conv_w=v_f_conv_w, f_conv_b=v_f_conv_b, f_down=v_f_down)

    bl, seq, _ = x.shape
    t = bl * seq

    p = {n: local[n] for n in WEIGHTS if n not in SHARD_AXIS}
    for n in SMALL_SHARDED:
        p[n] = _full_from_gathered(all_gather(local[n], f"gather_{n}"), SHARD_AXIS[n])
    units = [(n, l) for n in MATMUL_WEIGHTS for l in range(local[n].shape[0])]
    block = lambda u: local[u[0]][u[1]].astype(BF16)
    weights = {u: whole_weight(u[0], all_gather(block(u), f"gather_{u[0]}{u[1]}")) for u in layer_units(0)}
    blocks = {u: block(u) for u in units if u not in weights}
    standins = {u: jnp.zeros((N_DEV,) + local[u[0]].shape[1:], BF16) for u in units}
    small = {n: p[n] for n in WEIGHTS if n not in MATMUL_WEIGHTS and n != 'final_norm'}

    pending = Pending()

    def run(small_w, standins_, xin):
        return _trunk(small_w, weights, blocks, standins_, xin, mem.reshape(bl * N_MEM, D_MODEL), bl, seq, pending)

    x_out, vjp = jax.vjp(run, small, standins, x.reshape(t, D_MODEL))
    loss_part, dx_out, d_final = loss_head(x_out, final_norm, loss_target.reshape(t, D_MODEL))
    grads, received, dx = vjp(dx_out)
    received = dict(received)
    for unit, slabs in pending.take_all():
        pending.received[unit] = exchange_slabs(slabs, f"exchange_{unit[0]}{unit[1]}")
    received.update(pending.received)
    grads = dict(grads)
    grads['final_norm'] = d_final
    loss = lax.psum(loss_part, ("x", "y", "c"))

    outs = {}

    def update(name, n, slots, shape, sel=lambda a: a):
        two_d = lambda a: sel(a).reshape(slots.shape[1:])
        got = reduce_adamw(slots, two_d(local[n]), two_d(mom_m[n]), two_d(mom_v[n]), name)
        return [g.reshape(shape) for g in got]

    for n in SMALL_SHARDED:
        slots = exchange_slabs(_shards_of_full(grads[n], SHARD_AXIS[n]), f"exchange_{n}")
        slots = slots.reshape(N_DEV, -1, slots.shape[-1])
        for kind, a in zip(KINDS, update(f"adamw_{n}", n, slots, local[n].shape)):
            outs[kind, n] = a
    for n in MATMUL_WEIGHTS:
        per_layer = [update(f"adamw_{n}{l}", n, received[n, l], local[n].shape[1:], lambda a, l=l: a[l])
                     for l in range(local[n].shape[0])]
        for k, kind in enumerate(KINDS):
            outs[kind, n] = jnp.concatenate([got[k][None] for got in per_layer])
    replicated = [n for n in WEIGHTS if n not in SHARD_AXIS]
    pk = lambda d: _pack([d[n] for n in replicated], F32, 8)
    got = reduce_adamw(all_gather(pk(grads), "gather_replicated_grads"), pk(local), pk(mom_m), pk(mom_v),
                       "adamw_replicated")
    shapes = [local[n].shape for n in replicated]
    for kind, buf in zip(KINDS, got):
        for n, a in zip(replicated, _unpack(buf, shapes)):
            outs[kind, n] = a
    result = [loss, dx.reshape(bl, seq, D_MODEL)]
    for kind in KINDS:
        result += [outs[kind, n] for n in WEIGHTS]
    return tuple(result)
```

```python
import functools
import math

import jax
import jax.numpy as jnp
from jax import lax
from jax.experimental import pallas as pl
from jax.experimental.pallas import tpu as pltpu

F32 = jnp.float32
BF16 = jnp.bfloat16
NN = (((1,), (0,)), ((), ()))
NT = (((1,), (1,)), ((), ()))
TN = (((0,), (0,)), ((), ()))

D_MODEL = 1024
DEPTH = 4
EPS = 1e-6
N_MEM = 256
M_D_INNER = 2048
M_HEADS = 32
M_GROUPS = 8
M_STATE = 128
M_CONV_DIM = 4096
M_IN = 6176
M_IN_PAD = 6272
SSD_CHUNK = 256
H_HEADS = 8
HGRN_CHUNK = 32
HGRN_ROWS = 256
G_QK_HEADS = 8
G_V_HEADS = 16
G_KEY_DIM = 1024
G_VAL_DIM = 2048
G_CONV_DIM = 4096
G_IN = 6176
G_IN_PAD = 6272
GDN_CHUNK = 64
X_HEADS = 4
X_HEAD_DIM = 256
D_FF = 2816
ADAM_LR = 0.001
ADAM_B1 = 0.9
ADAM_B2 = 0.999
ADAM_EPS = 1e-08
ADAM_WD = 0.01
ADAM_STEP = 10

N_DEV = 8
LANE = 128
KINDS = ('grad', 'delta', 'new_m', 'new_v')
VMEM_LIMIT = 56 * 1024 * 1024

WEIGHTS = ['ln_mix', 'ln_xattn', 'ln_mem', 'ln_ffn', 'final_norm', 'm_in_w', 'm_conv_w', 'm_conv_b', 'm_dt_bias',
           'm_a_log', 'm_d', 'm_norm_w', 'm_out_w', 'h_in_w', 'h_lower_bounds', 'h_norm_w', 'h_out_w', 'g_in_w',
           'g_conv_w', 'g_a_log', 'g_dt_bias', 'g_norm_w', 'g_out_w', 'xa_q', 'xa_kv', 'xa_o', 'f_up', 'f_conv_w',
           'f_conv_b', 'f_down']
SHARD_AXIS = {'m_in_w': 2, 'm_conv_w': 2, 'm_conv_b': 1, 'm_norm_w': 1, 'm_out_w': 1, 'h_in_w': 2, 'h_out_w': 1,
              'g_in_w': 2, 'g_conv_w': 2, 'g_out_w': 1, 'xa_q': 1, 'xa_kv': 2, 'xa_o': 1, 'f_up': 2, 'f_conv_w': 2,
              'f_down': 1}
MATMUL_WEIGHTS = ['m_in_w', 'm_out_w', 'h_in_w', 'h_out_w', 'g_in_w', 'g_out_w', 'xa_q', 'xa_kv', 'xa_o', 'f_up',
                  'f_down']
SMALL_SHARDED = ['m_conv_w', 'm_conv_b', 'm_norm_w', 'g_conv_w', 'f_conv_w']


def _cparams():
    return pltpu.CompilerParams(vmem_limit_bytes=VMEM_LIMIT)


def bdot(a, b, dims=NN):
    return lax.dot_general(a.astype(BF16), b.astype(BF16), dims, preferred_element_type=F32)


def _split(a):
    hi = a.astype(BF16)
    return hi, (a - hi.astype(F32)).astype(BF16)


def _h3(a, b, dims):
    ah, al = _split(a)
    bh, bl = _split(b)
    d = functools.partial(lax.dot_general, dimension_numbers=dims, preferred_element_type=F32)
    return d(ah, bh) + (d(ah, bl) + d(al, bh))


BNN = (((2,), (1,)), ((0,), (0,)))
BNT = (((2,), (2,)), ((0,), (0,)))
BTN = (((1,), (1,)), ((0,), (0,)))


@jax.custom_vjp
def h3dot_b(a, b):
    return _h3(a, b, BNN)


h3dot_b.defvjp(lambda a, b: (_h3(a, b, BNN), (a, b)),
               lambda res, ct: (_h3(ct, res[1], BNT), _h3(res[0], ct, BTN)))

T_ROWS = (((0,), (1,)), ((), ()))


def _tri_times(tri, x, dims, tri_first):
    t = tri.astype(BF16)
    x0 = x.astype(BF16)
    r1 = x - x0.astype(F32)
    x1 = r1.astype(BF16)
    x2 = (r1 - x1.astype(F32)).astype(BF16)
    if tri_first:
        d = lambda xx: lax.dot_general(t, xx, dims, preferred_element_type=F32)
    else:
        d = lambda xx: lax.dot_general(xx, t, dims, preferred_element_type=F32)
    return d(x0) + (d(x1) + d(x2))


@jax.custom_vjp
def cumdot(tri, x):
    return _tri_times(tri, x, NN, True)


cumdot.defvjp(lambda tri, x: (_tri_times(tri, x, NN, True), tri),
              lambda tri, ct: (jnp.zeros_like(tri), _tri_times(tri, ct, TN, True)))


@jax.custom_vjp
def cumdot_t(tri, x):
    return _tri_times(tri, x, T_ROWS, False)


cumdot_t.defvjp(lambda tri, x: (_tri_times(tri, x, T_ROWS, False), tri),
                lambda tri, ct: (jnp.zeros_like(tri), _tri_times(tri, ct, T_ROWS, True)))


def _tile(dim, cap):
    if dim <= cap:
        return dim
    best = None
    for t in range(LANE, cap + 1, LANE):
        if dim % t == 0:
            best = t
    assert best is not None, dim
    return best


def _position():
    return lax.axis_index("x"), lax.axis_index("y"), lax.axis_index("c")


def _direct_copies(kind, src_ref, dst_ref, send_sems, recv_sems, local_sem):
    x, y, c = _position()
    me = 4 * x + 2 * y + c
    local_src = src_ref if kind == 'gather' else src_ref.at[me]
    copies = [pltpu.make_async_copy(local_src, dst_ref.at[me], local_sem)]
    for k in range(1, N_DEV):
        px = 1 - x if (k >> 2) & 1 else x
        py = 1 - y if (k >> 1) & 1 else y
        pc = 1 - c if k & 1 else c
        copies.append(pltpu.make_async_remote_copy(
            src_ref=src_ref if kind == 'gather' else src_ref.at[4 * px + 2 * py + pc], dst_ref=dst_ref.at[me],
            send_sem=send_sems.at[k - 1], recv_sem=recv_sems.at[k - 1],
            device_id=(px, py, pc), device_id_type=pl.DeviceIdType.MESH))
    return copies


COMM_SCRATCH = [pltpu.SemaphoreType.DMA((N_DEV - 1,)), pltpu.SemaphoreType.DMA((N_DEV - 1,)), pltpu.SemaphoreType.DMA]


class Pending:
    def __init__(self):
        self.jobs, self.received = [], {}

    def take_all(self):
        jobs, self.jobs = self.jobs, []
        return jobs


def matmul(a, b, *, ta=False, tb=False, out_dtype=F32, name="mm", carry=None, residual=None):
    if ta:
        k, m = a.shape
    else:
        m, k = a.shape
    if tb:
        n, k2 = b.shape
    else:
        k2, n = b.shape
    assert k == k2, (a.shape, b.shape, ta, tb)
    tm = _tile(m, 1408)
    tn = _tile(n, 1408)
    tk = _tile(k, 1408)
    grid = (m // tm, n // tn, k // tk)
    nk = grid[2]
    dims = (((0 if ta else 1,), (1 if tb else 0,)), ((), ()))

    def at_step(which):
        conds = [pl.program_id(ax) == (0 if which == 'first' else grid[ax] - 1) for ax in range(3)]
        return jnp.logical_and(jnp.logical_and(conds[0], conds[1]), conds[2])

    def body(*refs):
        r_ref = None
        if residual is not None:
            r_ref, refs = refs[2], refs[:2] + refs[3:]
        if carry is None:
            a_ref, b_ref, o_ref, acc_ref = refs
        else:
            a_ref, b_ref, src_ref, o_ref, dst_ref, acc_ref, send_sems, recv_sems, local_sem = refs
            copies = lambda: _direct_copies(carry[0], src_ref, dst_ref, send_sems, recv_sems, local_sem)

            @pl.when(at_step('first'))
            def _():
                for cp in copies():
                    cp.start()

        @pl.when(pl.program_id(2) == 0)
        def _():
            acc_ref[...] = jnp.zeros_like(acc_ref)

        acc_ref[...] += lax.dot_general(a_ref[...].astype(BF16), b_ref[...].astype(BF16), dims,
                                        preferred_element_type=F32)

        @pl.when(pl.program_id(2) == nk - 1)
        def _():
            out = acc_ref[...] if r_ref is None else acc_ref[...] + r_ref[...]
            o_ref[...] = out.astype(o_ref.dtype)

        if carry is not None:
            @pl.when(at_step('last'))
            def _():
                for cp in copies():
                    cp.wait()

    a_spec = pl.BlockSpec((tk, tm), lambda i, j, kk: (kk, i)) if ta else pl.BlockSpec((tm, tk), lambda i, j, kk: (i, kk))
    b_spec = pl.BlockSpec((tn, tk), lambda i, j, kk: (j, kk)) if tb else pl.BlockSpec((tk, tn), lambda i, j, kk: (kk, j))
    o_spec = pl.BlockSpec((tm, tn), lambda i, j, kk: (i, j))
    o_shape = jax.ShapeDtypeStruct((m, n), out_dtype)
    acc = pltpu.VMEM((tm, tn), F32)
    ins, in_specs = [a, b], [a_spec, b_spec]
    if residual is not None:
        ins.append(residual)
        in_specs.append(o_spec)
    if carry is None:
        return pl.pallas_call(
            body, name=name, grid=grid, in_specs=in_specs, out_specs=o_spec, out_shape=o_shape,
            scratch_shapes=[acc], compiler_params=_cparams(),
        )(*ins)
    kind, src = carry
    got = jax.ShapeDtypeStruct(((N_DEV,) + src.shape) if kind == 'gather' else src.shape, src.dtype)
    hbm = pl.BlockSpec(memory_space=pl.ANY)
    return pl.pallas_call(
        body, name=name, grid=grid, in_specs=in_specs + [hbm], out_specs=[o_spec, hbm],
        out_shape=[o_shape, got], scratch_shapes=[acc] + COMM_SCRATCH, compiler_params=_cparams(),
    )(*ins, src)


def make_linear(name, shard_axis, n_real, has_res=False, defer=None):
    def forward(a, w, nxt, res):
        r = res[0] if res else None
        if nxt:
            y, got = matmul(a, w, name=name + "_fwd", carry=('gather', nxt[0]), residual=r)
            return y, (got,)
        return matmul(a, w, name=name + "_fwd", residual=r), ()

    @jax.custom_vjp
    def linear(a, w, wg, nxt, res):
        return forward(a, w, nxt, res)

    def fwd(a, w, wg, nxt, res):
        return forward(a, w, nxt, res), (a, w, nxt)

    def bwd(saved, cts):
        a, w, nxt = saved
        dy = cts[0]
        dw = matmul(a, dy, ta=True, out_dtype=BF16, name=name + "_bwd_dw")
        slabs = _shards_of_full(dw[:, :n_real], shard_axis)
        if defer is None:
            da, slots = matmul(dy, w, tb=True, out_dtype=a.dtype, name=name + "_bwd_da", carry=('exchange', slabs))
        else:
            defer[0].jobs.append((defer[1], slabs))
            da, slots = matmul(dy, w, tb=True, out_dtype=a.dtype, name=name + "_bwd_da"), jnp.zeros_like(slabs)
        return da, jnp.zeros_like(w), slots, tuple(jnp.zeros_like(b) for b in nxt), ((dy,) if has_res else ())

    linear.defvjp(fwd, bwd)
    return linear


class In:
    def __init__(self, block, imap, kind='blk', inner=(), cols=None):
        self.block, self.imap, self.kind, self.inner, self.cols = block, imap, kind, inner, cols


class Out:
    def __init__(self, shape, dtype, block, imap):
        self.shape, self.dtype, self.block, self.imap = shape, dtype, block, imap


def make_op(name, fn, grid, ins, outs, state_shape=None, seq_axis=None, passthrough=(), pending=None, n_gather=0):
    n_in, n_out = len(ins), len(outs)
    has_state = state_shape is not None
    nd = len(grid)
    diff_idx = [i for i, s in enumerate(ins) if s.kind != 'const']

    def in_spec(s, reverse):
        off = 0
        if s.cols is not None:
            assert s.cols[0] % s.block[-1] == 0
            off = s.cols[0] // s.block[-1]

        def imap(*ids):
            ids = rev(ids) if reverse else ids
            idx = tuple(s.imap(*ids))
            return idx[:-1] + (idx[-1] + off,) if off else idx

        return pl.BlockSpec(s.block, imap)

    def rel_spec(block, f, reverse):
        return pl.BlockSpec(block, (lambda *ids: f(*rev(ids))) if reverse else f)

    def rev(ids):
        if not has_state:
            return ids
        ids = list(ids)
        ids[seq_axis] = grid[seq_axis] - 1 - ids[seq_axis]
        return tuple(ids)

    save_shape = tuple(grid) + tuple(state_shape) if has_state else None
    save_block = (None,) * nd + tuple(state_shape) if has_state else None

    def save_imap(*ids):
        return tuple(ids) + (0,) * len(state_shape)

    def step_is(which):
        conds = [pl.program_id(ax) == (0 if which == 'first' else grid[ax] - 1) for ax in range(nd)]
        return functools.reduce(jnp.logical_and, conds)

    def fwd_call(*xs):
        xs, blocks = xs[:n_in], xs[n_in:]
        n_save = 1 if has_state else 0

        def body(*refs):
            if blocks:
                src_refs = refs[n_in:n_in + n_gather]
                dst_refs = refs[n_in + n_gather + n_out + n_save:n_in + 2 * n_gather + n_out + n_save]
                sems = refs[len(refs) - 3 * n_gather:]
                refs = refs[:n_in] + refs[n_in + n_gather:n_in + n_gather + n_out + n_save] + \
                    refs[n_in + 2 * n_gather + n_out + n_save:len(refs) - 3 * n_gather]
                copies = lambda: [cp for k in range(n_gather) for cp in _direct_copies(
                    'gather', src_refs[k], dst_refs[k], sems[3 * k], sems[3 * k + 1], sems[3 * k + 2])]

                @pl.when(step_is('first'))
                def _():
                    for cp in copies():
                        cp.start()

            in_refs = refs[:n_in]
            out_refs = refs[n_in:n_in + n_out]
            vals = [r[...] for r in in_refs]
            if has_state:
                save_ref, st_ref = refs[n_in + n_out], refs[n_in + n_out + 1]

                @pl.when(pl.program_id(seq_axis) == 0)
                def _():
                    st_ref[...] = jnp.zeros(state_shape, F32)

                st = st_ref[...]
                save_ref[...] = st
                res = fn(*vals, st)
                st_ref[...] = res[-1]
                res = res[:-1]
            else:
                res = fn(*vals)
            for o, v in zip(out_refs, res):
                o[...] = v.astype(o.dtype)

            if blocks:
                @pl.when(step_is('last'))
                def _():
                    for cp in copies():
                        cp.wait()

        out_shape = [jax.ShapeDtypeStruct(o.shape, o.dtype) for o in outs]
        out_specs = [pl.BlockSpec(o.block, o.imap) for o in outs]
        scratch = []
        if has_state:
            out_shape.append(jax.ShapeDtypeStruct(save_shape, F32))
            out_specs.append(pl.BlockSpec(save_block, save_imap))
            scratch.append(pltpu.VMEM(state_shape, F32))
        hbm = pl.BlockSpec(memory_space=pl.ANY)
        for blk in blocks:
            out_shape.append(jax.ShapeDtypeStruct((N_DEV,) + blk.shape, blk.dtype))
            out_specs.append(hbm)
            scratch += COMM_SCRATCH
        return pl.pallas_call(
            body, name=name + "_fwd", grid=grid,
            in_specs=[in_spec(s, False) for s in ins] + [hbm] * len(blocks),
            out_specs=out_specs, out_shape=out_shape, scratch_shapes=scratch,
            compiler_params=_cparams(),
        )(*xs, *blocks)

    def grad_shape(s, x):
        if s.cols is not None:
            return x.shape[:-1] + (s.cols[1],)
        return x.shape

    def bwd_call(xs, save, cts, pass_cts=()):
        n_diff = len(diff_idx)
        jobs = pending.take_all() if pending is not None else []
        n_args = n_in + (1 if has_state else 0) + n_out + len(passthrough)

        def body(*refs):
            if jobs:
                src_refs = refs[n_args:n_args + len(jobs)]
                dst_refs = refs[n_args + len(jobs) + n_diff:n_args + 2 * len(jobs) + n_diff]
                sems = refs[len(refs) - 3 * len(jobs):]
                refs = refs[:n_args] + refs[n_args + len(jobs):n_args + len(jobs) + n_diff] + \
                    refs[n_args + 2 * len(jobs) + n_diff:len(refs) - 3 * len(jobs)]
                copies = lambda: [cp for k in range(len(jobs)) for cp in _direct_copies(
                    'exchange', src_refs[k], dst_refs[k], sems[3 * k], sems[3 * k + 1], sems[3 * k + 2])]

                @pl.when(step_is('first'))
                def _():
                    for cp in copies():
                        cp.start()

            in_refs = refs[:n_in]
            p = n_in
            if has_state:
                save_ref = refs[p]
                p += 1
            ct_refs = refs[p:p + n_out]
            p += n_out
            pass_refs = dict(zip(passthrough, refs[p:p + len(passthrough)]))
            p += len(passthrough)
            g_refs = refs[p:p + n_diff]
            p += n_diff
            vals = [r[...] for r in in_refs]

            def g(*dv):
                full = list(vals)
                for i, v in zip(diff_idx, dv):
                    full[i] = v
                if has_state:
                    return tuple(fn(*full, dv[-1]))
                return tuple(fn(*full))

            prim = [vals[i] for i in diff_idx]
            ct = tuple(r[...].astype(F32) for r in ct_refs)
            if has_state:
                dst_ref = refs[p]

                @pl.when(pl.program_id(seq_axis) == 0)
                def _():
                    dst_ref[...] = jnp.zeros(state_shape, F32)

                prim = prim + [save_ref[...]]
                ct = ct + (dst_ref[...],)
            _, vjp = jax.vjp(g, *prim)
            grads = vjp(ct)
            for k, i in enumerate(diff_idx):
                s = ins[i]
                if s.kind == 'blk':
                    g = grads[k] + pass_refs[i][...] if i in pass_refs else grads[k]
                    g_refs[k][...] = g.astype(g_refs[k].dtype)
                else:
                    first = None
                    for ax in s.inner:
                        c = pl.program_id(ax) == 0
                        first = c if first is None else jnp.logical_and(first, c)

                    @pl.when(first)
                    def _(k=k):
                        g_refs[k][...] = jnp.zeros_like(g_refs[k])

                    g_refs[k][...] += grads[k].astype(g_refs[k].dtype)
            if has_state:
                dst_ref[...] = grads[-1]

            if jobs:
                @pl.when(step_is('last'))
                def _():
                    for cp in copies():
                        cp.wait()

        in_specs = [in_spec(s, True) for s in ins]
        args = list(xs)
        if has_state:
            in_specs.append(rel_spec(save_block, save_imap, True))
            args.append(save)
        for o, c in zip(outs, cts):
            in_specs.append(rel_spec(o.block, o.imap, True))
            args.append(c)
        for i, c in zip(passthrough, pass_cts):
            assert ins[i].kind == 'blk' and ins[i].cols is None
            in_specs.append(rel_spec(ins[i].block, ins[i].imap, True))
            args.append(c)
        out_shape, out_specs = [], []
        for i in diff_idx:
            s = ins[i]
            out_shape.append(jax.ShapeDtypeStruct(grad_shape(s, xs[i]), xs[i].dtype))
            out_specs.append(rel_spec(s.block, s.imap, True))
        scratch = [pltpu.VMEM(state_shape, F32)] if has_state else []
        hbm = pl.BlockSpec(memory_space=pl.ANY)
        for _, slabs in jobs:
            in_specs.append(hbm)
            args.append(slabs)
            out_specs.append(hbm)
            out_shape.append(jax.ShapeDtypeStruct(slabs.shape, slabs.dtype))
            scratch += COMM_SCRATCH
        got = pl.pallas_call(
            body, name=name + "_bwd", grid=grid,
            in_specs=in_specs, out_specs=out_specs, out_shape=out_shape, scratch_shapes=scratch,
            compiler_params=_cparams(),
        )(*args)
        for (unit, _), slots in zip(jobs, got[n_diff:]):
            pending.received[unit] = slots
        return got[:n_diff]

    def results(xs, res):
        gathered = tuple(res[len(res) - n_gather:]) if n_gather else ()
        return tuple(res[:n_out]) + tuple(xs[i] for i in passthrough) + gathered

    @jax.custom_vjp
    def op(*xs):
        return results(xs, fwd_call(*xs))

    def op_fwd(*xs):
        res = fwd_call(*xs)
        return results(xs, res), (xs, res[n_out] if has_state else None)

    def op_bwd(resid, cts):
        xs, save = resid
        xs, blocks = xs[:n_in], xs[n_in:]
        grads = bwd_call(xs, save, cts[:n_out], cts[n_out:n_out + len(passthrough)])
        out = []
        k = 0
        for i, s in enumerate(ins):
            if s.kind == 'const':
                out.append(jnp.zeros_like(xs[i]))
                continue
            g = grads[k]
            k += 1
            if s.cols is not None:
                g = jnp.pad(g, ((0, 0),) * (g.ndim - 1) + ((s.cols[0], xs[i].shape[-1] - s.cols[0] - s.cols[1]),))
            out.append(g)
        return tuple(out) + tuple(jnp.zeros_like(b) for b in blocks)

    op.defvjp(op_fwd, op_bwd)
    return op


def _rms(x, w):
    return x * lax.rsqrt(jnp.mean(x * x, axis=-1, keepdims=True) + EPS) * w


def _silu(x):
    return x * jax.nn.sigmoid(x)


def rmsnorm_op(name, t, out_dtype, residual=False):
    tm = _tile(t, 512)
    return make_op(
        name, lambda x, w: (_rms(x, w),), (t // tm,),
        [In((tm, D_MODEL), lambda i: (i, 0)), In((1, D_MODEL), lambda i: (0, 0), 'acc', (0,))],
        [Out((t, D_MODEL), out_dtype, (tm, D_MODEL), lambda i: (i, 0))], passthrough=(0,) if residual else ())


def _tri(q):
    ii = lax.broadcasted_iota(jnp.int32, (q, q), 0)
    jj = lax.broadcasted_iota(jnp.int32, (q, q), 1)
    return ii >= jj, ii > jj


def _ssd_fn(z, x, bm, cm, dtr, dtb, alog, dsk, nw, state):
    q = x.shape[0]
    incl, _ = _tri(q)
    tril = incl.astype(F32)
    dt = jax.nn.softplus(dtr + dtb)
    da = dt * (-jnp.exp(alog))
    acum = cumdot(tril, da)
    acum_t = cumdot_t(tril, da)
    cb = bdot(cm, bm, NT)
    heads = range(4)
    wide = lambda a: jnp.concatenate([jnp.broadcast_to(a[:, r:r + 1], (a.shape[0], 64)) for r in heads], axis=1)
    last = acum[q - 1:q, :]
    xc = x * wide(dt)
    y = bdot(cm, state, NT) * wide(jnp.exp(acum)) + wide(dsk) * x
    ds = bdot(xc * wide(jnp.exp(last - acum)), bm, TN)
    e_last = jnp.exp(last)
    new_state = state * jnp.concatenate([jnp.broadcast_to(e_last[:, r:r + 1], (64, 1)) for r in heads], axis=0) + ds
    diag = []
    for r in heads:
        decay = jnp.exp(jnp.where(incl, acum[:, r:r + 1] - acum_t[r:r + 1, :], -jnp.inf))
        diag.append(bdot(cb * decay, xc[:, 64 * r:64 * r + 64]))
    y = y + jnp.concatenate(diag, axis=1)
    yz = y * _silu(z)
    return _rms(yz, nw), new_state


def _per_sequence(fn, n_seq_args, bl):
    def f(*args):
        *ins, state = args
        res = [fn(*[a[b] for a in ins[:n_seq_args]], *ins[n_seq_args:], state[b]) for b in range(bl)]
        return tuple(jnp.concatenate([r[k][None] for r in res]) for k in range(len(res[0])))

    return f


def ssd_op(name, bl, seq, pending=None, n_gather=0):
    q = SSD_CHUNK
    nc = seq // q
    blk = lambda w, c0, cw: In((bl, q, w), lambda g, n: (0, n, g), cols=(c0, cw))
    small = lambda g, n: (g, 0, 0)
    ins = [
        blk(256, 0, M_D_INNER),
        blk(256, 0, M_D_INNER),
        blk(128, M_D_INNER, 1024),
        blk(128, M_D_INNER + 1024, 1024),
        In((None, bl, q, 4), lambda g, n: (g, 0, n, 0)),
        In((None, 1, 4), small, 'acc', (1,)),
        In((None, 1, 4), small, 'acc', (1,)),
        In((None, 1, 4), small, 'acc', (1,)),
        In((None, 1, 256), small, 'acc', (1,)),
    ]
    outs = [Out((bl, seq, M_D_INNER), F32, (bl, q, 256), lambda g, n: (0, n, g))]
    return make_op(name, _per_sequence(_ssd_fn, 5, bl), (M_GROUPS, nc), ins, outs,
                   state_shape=(bl, 256, 128), seq_axis=1, pending=pending, n_gather=n_gather)


def _gla_fn(layer, qr, fr, ir, gr, lbp, nw, state_t):
    rows = qr.shape[0]
    c = HGRN_CHUNK
    n_chunks = rows // c
    e = jnp.exp(lbp - jnp.max(lbp, axis=0, keepdims=True))
    sm = e / jnp.sum(e, axis=0, keepdims=True)
    lb = jnp.sum(sm[1:layer + 1, :], axis=0, keepdims=True) if layer > 0 else jnp.zeros((1, lbp.shape[1]), F32)
    qq = _silu(qr) * (128 ** -0.5)
    forget = lb + (1.0 - lb) * jax.nn.sigmoid(fr)
    kk = 1.0 - forget
    logf = jnp.log(forget)
    ii = lax.broadcasted_iota(jnp.int32, (rows, rows), 0)
    jj = lax.broadcasted_iota(jnp.int32, (rows, rows), 1)
    own = jnp.logical_and(ii >= jj, ii // c == jj // c)
    gc = cumdot(own.astype(F32), logf)
    glasts = [gc[c * j + c - 1:c * j + c, :] for j in range(n_chunks)]
    glast_rows = jnp.concatenate([jnp.broadcast_to(g, (c, g.shape[1])) for g in glasts], axis=0)
    q_dec = qq * jnp.exp(gc)
    k_inv = kk * jnp.exp(-gc)
    k_end = kk * jnp.exp(glast_rows - gc)
    att = jnp.where(own, bdot(q_dec, k_inv, NT), 0.0)
    o = bdot(att, ir)
    inter = []
    for j in range(n_chunks):
        sl = slice(c * j, c * j + c)
        inter.append(bdot(q_dec[sl], state_t, NT))
        state_t = state_t * jnp.exp(glasts[j]) + bdot(ir[sl], k_end[sl], TN)
    o = o + jnp.concatenate(inter, axis=0)
    return _rms(o, nw) * _silu(gr), state_t


def gla_op(name, layer, bl, seq, pending=None, n_gather=0):
    r = HGRN_ROWS
    ns = seq // r
    blk = lambda k: In((bl, r, 128), lambda h, n: (0, n, h), cols=(1024 * k, 1024))
    ins = [blk(0), blk(1), blk(2), blk(3),
           In((DEPTH, 128), lambda h, n: (0, h), 'acc', (1,)),
           In((1, 128), lambda h, n: (0, 0), 'acc', (0, 1))]
    outs = [Out((bl, seq, D_MODEL), F32, (bl, r, 128), lambda h, n: (0, n, h))]
    return make_op(name, _per_sequence(functools.partial(_gla_fn, layer), 4, bl), (H_HEADS, ns), ins, outs,
                   state_shape=(bl, 128, 128), seq_axis=1, pending=pending, n_gather=n_gather)


def _neumann_inverse(m):
    q = m.shape[1]
    ii = lax.broadcasted_iota(jnp.int32, (q, q), 0)
    jj = lax.broadcasted_iota(jnp.int32, (q, q), 1)
    eye = (ii == jj).astype(F32)[None]
    p = -m
    inv = eye + p
    for _ in range(int(math.log2(q)) - 1):
        p = _h3(p, p, BNN)
        inv = inv + _h3(inv, p, BNN)
    return inv


@jax.custom_vjp
def _unit_lower_inverse(m):
    return _neumann_inverse(m)


def _unit_lower_inverse_fwd(m):
    inv = _neumann_inverse(m)
    return inv, inv


_unit_lower_inverse.defvjp(_unit_lower_inverse_fwd,
                           lambda inv, ct: (-_h3(_h3(inv, ct, BTN), inv, BNT),))


def _gdn_fn(qc, kc, vc, zc, br, ar, alog, dtb, nw, state):
    bl, q = qc.shape[0], qc.shape[1]
    incl, strict = _tri(q)
    tril = incl.astype(F32)
    g = jnp.concatenate([-jnp.exp(alog) * jax.nn.softplus(ar[b] + dtb) for b in range(bl)], axis=1)
    gc = cumdot(tril, g)
    gc_t = cumdot_t(tril, g)
    heads, ms, rhs = [], [], []
    for b in range(bl):
        qn = qc[b] * lax.rsqrt(jnp.sum(qc[b] * qc[b], axis=-1, keepdims=True) + EPS) * (128 ** -0.5)
        kn = kc[b] * lax.rsqrt(jnp.sum(kc[b] * kc[b], axis=-1, keepdims=True) + EPS)
        beta = jax.nn.sigmoid(br[b])
        qk = bdot(qn, kn, NT)
        for j in range(2):
            i = 2 * b + j
            col = gc[:, i:i + 1]
            decay = jnp.exp(jnp.where(incl, col - gc_t[i:i + 1, :], -jnp.inf))
            bj = beta[:, j:j + 1]
            kb = kn * bj
            ms.append(jnp.where(strict, bdot(kb, kn, NT) * decay, 0.0))
            rhs.append(jnp.concatenate([vc[b][:, 128 * j:128 * j + 128] * bj, kb * jnp.exp(col)], axis=1))
            heads.append((qn, kn, qk * decay, col, gc[q - 1:q, i:i + 1]))
    sol = h3dot_b(_unit_lower_inverse(jnp.concatenate([m[None] for m in ms])),
                  jnp.concatenate([r[None] for r in rhs]))
    outs, states = [], []
    for b in range(bl):
        os_, sts = [], []
        for j in range(2):
            i = 2 * b + j
            qn, kn, att, col, glast = heads[i]
            u = sol[i][:, :128]
            w = sol[i][:, 128:]
            st = state[b][128 * j:128 * j + 128, :]
            v_new = u - bdot(w, st)
            o = bdot(qn * jnp.exp(col), st) + bdot(att, v_new)
            sts.append(st * jnp.exp(glast) + bdot(kn * jnp.exp(glast - col), v_new, TN))
            os_.append(_rms(o, nw) * _silu(zc[b][:, 128 * j:128 * j + 128]))
        outs.append(jnp.concatenate(os_, axis=1))
        states.append(jnp.concatenate(sts, axis=0))
    return jnp.concatenate([o[None] for o in outs]), jnp.concatenate([st[None] for st in states])


def gdn_op(name, bl, seq, pending=None, n_gather=0):
    q = GDN_CHUNK
    nc = seq // q
    blk = lambda w, c0, cw: In((bl, q, w), lambda h, n: (0, n, h), cols=(c0, cw))
    small = lambda h, n: (h, 0, 0)
    ins = [
        blk(128, 0, G_KEY_DIM),
        blk(128, G_KEY_DIM, G_KEY_DIM),
        blk(256, 2 * G_KEY_DIM, G_VAL_DIM),
        blk(256, G_CONV_DIM, G_VAL_DIM),
        In((None, bl, q, 2), lambda h, n: (h, 0, n, 0)),
        In((None, bl, q, 2), lambda h, n: (h, 0, n, 0)),
        In((None, 1, 2), small, 'acc', (1,)),
        In((None, 1, 2), small, 'acc', (1,)),
        In((1, 128), lambda h, n: (0, 0), 'acc', (0, 1)),
    ]
    outs = [Out((bl, seq, G_VAL_DIM), F32, (bl, q, 256), lambda h, n: (0, n, h))]
    return make_op(name, _gdn_fn, (G_QK_HEADS, nc), ins, outs,
                   state_shape=(bl, 256, 128), seq_axis=1, pending=pending, n_gather=n_gather)


def _xattn_fn(q, k, v):
    s = bdot(q, k, NT) * (X_HEAD_DIM ** -0.5)
    s = s - jnp.max(s, axis=-1, keepdims=True)
    p = jnp.exp(s)
    p = p / jnp.sum(p, axis=-1, keepdims=True)
    return (bdot(p, v),)


def xattn_op(name, bl, seq):
    tq = _tile(seq, 2048)
    nq = seq // tq
    t = bl * seq
    ins = [
        In((tq, X_HEAD_DIM), lambda b, h, i: (b * nq + i, h)),
        In((N_MEM, X_HEAD_DIM), lambda b, h, i: (b, h), 'acc', (2,), cols=(0, D_MODEL)),
        In((N_MEM, X_HEAD_DIM), lambda b, h, i: (b, h), 'acc', (2,), cols=(D_MODEL, D_MODEL)),
    ]
    outs = [Out((t, D_MODEL), F32, (tq, X_HEAD_DIM), lambda b, h, i: (b * nq + i, h))]
    return make_op(name, _xattn_fn, (bl, X_HEADS, nq), ins, outs)


CONV_PAD = 8
CONV_ROWS = 128


def make_conv(name, bl, seq, width, ch, x_col0, up_col0=None):
    cb = 256
    rt = CONV_ROWS
    assert ch % cb == 0 and x_col0 % cb == 0 and (up_col0 is None or up_col0 % cb == 0) and seq % rt == 0
    nb = ch // cb
    n_tiles = seq // rt
    t = bl * seq
    has_up = up_col0 is not None
    grid = (nb, bl)
    x_spec = pl.BlockSpec((seq, cb), lambda c, b: (b, x_col0 // cb + c))
    up_specs = [pl.BlockSpec((seq, cb), lambda c, b: (b, up_col0 // cb + c))] if has_up else []
    w_spec = pl.BlockSpec((width, cb), lambda c, b: (0, c))
    b_spec = pl.BlockSpec((1, cb), lambda c, b: (0, c))
    o_spec = pl.BlockSpec((seq, cb), lambda c, b: (b, c))
    taps = [CONV_PAD - (width - 1) + j for j in range(width)]

    def window(x_ref, i):
        if isinstance(i, int) and i == 0:
            return jnp.concatenate([jnp.zeros((CONV_PAD, cb), F32), x_ref[0:rt, :]], axis=0)
        return x_ref[pl.ds(pl.multiple_of(i * rt - CONV_PAD, CONV_PAD), rt + CONV_PAD), :]

    def rows(i):
        return pl.ds(i * rt, rt) if isinstance(i, int) else pl.ds(pl.multiple_of(i * rt, rt), rt)

    def shifted(win):
        return [win[tp:tp + rt, :] for tp in taps]

    def pre_activation(views, w, b):
        y = b + w[0:1, :] * views[0]
        for j in range(1, width):
            y = y + w[j:j + 1, :] * views[j]
        return y

    def over_tiles(step, carry):
        carry = step(0, carry)
        return lax.fori_loop(1, n_tiles, step, carry)

    def fwd_call(x, w, b):
        def body(*refs):
            x_ref, w_ref, b_ref = refs[:3]
            o_ref = refs[-1]
            w_, b_ = w_ref[...], b_ref[...]

            def step(i, carry):
                y = _silu(pre_activation(shifted(window(x_ref, i)), w_, b_))
                if has_up:
                    y = y * refs[3][rows(i), :]
                o_ref[rows(i), :] = y
                return carry

            over_tiles(step, 0)

        return pl.pallas_call(
            body, name=name + "_fwd", grid=grid,
            in_specs=[x_spec, w_spec, b_spec] + up_specs, out_specs=o_spec,
            out_shape=jax.ShapeDtypeStruct((t, ch), F32),
            compiler_params=_cparams(),
        )(*([x, w, b] + ([x] if has_up else [])))

    def bwd_call(x, w, b, do):
        n_in = 4 + (1 if has_up else 0)

        def body(*refs):
            x_ref, w_ref, b_ref = refs[:3]
            do_ref = refs[n_in - 1]
            dx_ref, dw_ref, db_ref = refs[n_in:n_in + 3]
            gpad_ref = refs[-1]
            w_, b_ = w_ref[...], b_ref[...]

            def fold(a):
                acc = a[0:8, :]
                for k in range(1, rt // 8):
                    acc = acc + a[8 * k:8 * k + 8, :]
                return acc

            def grad_pre(i, sums):
                views = shifted(window(x_ref, i))
                y = pre_activation(views, w_, b_)
                s = jax.nn.sigmoid(y)
                act = y * s
                do_ = do_ref[rows(i), :]
                if has_up:
                    refs[n_in + 3][rows(i), :] = do_ * act
                    do_ = do_ * refs[3][rows(i), :]
                dy = do_ * (s + act * (1.0 - s))
                gpad_ref[rows(i), :] = dy
                new = [sums[j] + fold(dy * views[j]) for j in range(width)]
                return tuple(new) + (sums[width] + fold(dy),)

            zero8 = jnp.zeros((8, cb), F32)
            sums = over_tiles(grad_pre, (zero8,) * (width + 1))
            gpad_ref[seq:seq + CONV_PAD, :] = jnp.zeros((CONV_PAD, cb), F32)

            def grad_x(i, carry):
                if isinstance(i, int):
                    gwin = gpad_ref[0:rt + CONV_PAD, :]
                else:
                    gwin = gpad_ref[pl.ds(pl.multiple_of(i * rt, rt), rt + CONV_PAD), :]
                dx = w_[0:1, :] * gwin[width - 1:width - 1 + rt, :]
                for j in range(1, width):
                    dx = dx + w_[j:j + 1, :] * gwin[width - 1 - j:width - 1 - j + rt, :]
                dx_ref[rows(i), :] = dx
                return carry

            over_tiles(grad_x, 0)

            @pl.when(pl.program_id(1) == 0)
            def _():
                dw_ref[...] = jnp.zeros_like(dw_ref)
                db_ref[...] = jnp.zeros_like(db_ref)

            dw_ref[...] += jnp.concatenate([jnp.sum(sums[j], axis=0, keepdims=True) for j in range(width)], axis=0)
            db_ref[...] += jnp.sum(sums[width], axis=0, keepdims=True)

        big = jax.ShapeDtypeStruct((t, ch), F32)
        return pl.pallas_call(
            body, name=name + "_bwd", grid=grid,
            in_specs=[x_spec, w_spec, b_spec] + up_specs + [o_spec],
            out_specs=[o_spec, w_spec, b_spec] + ([o_spec] if has_up else []),
            out_shape=[big, jax.ShapeDtypeStruct((width, ch), F32), jax.ShapeDtypeStruct((1, ch), F32)]
            + ([big] if has_up else []),
            scratch_shapes=[pltpu.VMEM((seq + CONV_PAD, cb), F32)],
            compiler_params=_cparams(),
        )(*([x, w, b] + ([x] if has_up else []) + [do]))

    @jax.custom_vjp
    def conv(x, w, b):
        return fwd_call(x, w, b)

    def conv_fwd(x, w, b):
        return fwd_call(x, w, b), (x, w, b)

    def conv_bwd(res, do):
        x, w, b = res
        got = bwd_call(x, w, b, do)
        dx = jnp.pad(got[0], ((0, 0), (x_col0, x.shape[1] - x_col0 - ch)))
        if has_up:
            dx = dx + jnp.pad(got[3], ((0, 0), (up_col0, x.shape[1] - up_col0 - ch)))
        return dx, got[1], got[2]

    conv.defvjp(conv_fwd, conv_bwd)

    def apply(x, w, b=None):
        if b is None:
            b = jnp.zeros((ch,), F32)
        return conv(x, w, b.reshape(1, ch))

    return apply


def loss_head(x, w, target):
    t = x.shape[0]
    tm = _tile(t, 512)

    def fn(xb, wb, tb):
        err = _rms(xb, wb) - tb
        return 0.5 * jnp.sum(err * err) * (1.0 / D_MODEL)

    def body(x_ref, w_ref, t_ref, loss_ref, dx_ref, dw_ref):
        @pl.when(pl.program_id(0) == 0)
        def _():
            loss_ref[...] = jnp.zeros_like(loss_ref)
            dw_ref[...] = jnp.zeros_like(dw_ref)

        tb = t_ref[...]
        val, vjp = jax.vjp(lambda a, b: fn(a, b, tb), x_ref[...], w_ref[...])
        dx, dw = vjp(jnp.ones((), F32))
        dx_ref[...] = dx
        dw_ref[...] += dw
        loss_ref[...] += jnp.full(loss_ref.shape, val, F32)

    row = pl.BlockSpec((tm, D_MODEL), lambda i: (i, 0))
    vec = pl.BlockSpec((1, D_MODEL), lambda i: (0, 0))
    loss, dx, dw = pl.pallas_call(
        body, name="loss_head", grid=(t // tm,),
        in_specs=[row, vec, row],
        out_specs=[pl.BlockSpec((8, LANE), lambda i: (0, 0)), row, vec],
        out_shape=[jax.ShapeDtypeStruct((8, LANE), F32), jax.ShapeDtypeStruct((t, D_MODEL), F32),
                   jax.ShapeDtypeStruct((1, D_MODEL), F32)],
        compiler_params=_cparams(),
    )(x, w.reshape(1, D_MODEL), target)
    return loss[0, 0], dx, dw.reshape(D_MODEL)


PACK_W = 1024
ADAM_BLOCK_BYTES = 512 * 1024


def _rows_tile(r, c):
    if r * c * 4 <= ADAM_BLOCK_BYTES or r % 8:
        return r
    best = 8
    for t in range(8, r + 1, 8):
        if r % t == 0 and t * c * 4 <= ADAM_BLOCK_BYTES:
            best = t
    return best


def reduce_adamw(slots, w, m, v, name):
    r, wd = w.shape
    tr = _rows_tile(r, wd)
    c1 = 1.0 - ADAM_B1 ** ADAM_STEP
    c2 = 1.0 - ADAM_B2 ** ADAM_STEP

    def body(s_ref, w_ref, m_ref, v_ref, g_ref, d_ref, nm_ref, nv_ref):
        g = s_ref[0].astype(F32)
        for k in range(1, N_DEV):
            g = g + s_ref[k].astype(F32)
        nm = ADAM_B1 * m_ref[...] + (1.0 - ADAM_B1) * g
        nv = ADAM_B2 * v_ref[...] + (1.0 - ADAM_B2) * (g * g)
        m_hat = nm / c1
        v_hat = nv / c2
        d_ref[...] = -ADAM_LR * (m_hat / (jnp.sqrt(v_hat) + ADAM_EPS) + ADAM_WD * w_ref[...])
        g_ref[...] = g
        nm_ref[...] = nm
        nv_ref[...] = nv

    blk = pl.BlockSpec((tr, wd), lambda i: (i, 0))
    shp = jax.ShapeDtypeStruct((r, wd), F32)
    return pl.pallas_call(
        body, name=name, grid=(r // tr,),
        in_specs=[pl.BlockSpec((N_DEV, tr, wd), lambda i: (0, i, 0)), blk, blk, blk],
        out_specs=[blk, blk, blk, blk], out_shape=[shp, shp, shp, shp],
        compiler_params=_cparams(),
    )(slots, w, m, v)


def all_gather(block, name):
    def body(x_ref, out_ref, send_sems, recv_sems, local_sem):
        x, y, c = _position()
        me, sibling = (x, y, c), (x, y, 1 - c)
        chips = [(1 - x, y), (x, 1 - y), (1 - x, 1 - y)]

        def slot(px, py, pc):
            return out_ref.at[4 * px + 2 * py + pc]

        def copy(k, owner, to, src=None):
            return pltpu.make_async_remote_copy(
                src_ref=slot(*owner) if src is None else src, dst_ref=slot(*owner),
                send_sem=send_sems.at[k], recv_sem=recv_sems.at[k],
                device_id=to, device_id_type=pl.DeviceIdType.MESH)

        mine = pltpu.make_async_copy(x_ref, slot(*me), local_sem)
        mine.start()
        first = [copy(0, me, sibling, src=x_ref)]
        first += [copy(1 + j, me, (*chip, c), src=x_ref) for j, chip in enumerate(chips)]
        for cp in first:
            cp.start()
        passed = [copy(4 + j, (*chip, c), sibling) for j, chip in enumerate(chips)]
        for j, chip in enumerate(chips):
            copy(1 + j, (*chip, c), me).wait_recv()
            passed[j].start()
        copy(0, sibling, me).wait_recv()
        for j, chip in enumerate(chips):
            copy(4 + j, (*chip, 1 - c), me).wait_recv()
        for cp in first + passed:
            cp.wait_send()
        mine.wait()

    return pl.pallas_call(
        body, name=name,
        out_shape=jax.ShapeDtypeStruct((N_DEV,) + block.shape, block.dtype),
        in_specs=[pl.BlockSpec(memory_space=pl.ANY)],
        out_specs=pl.BlockSpec(memory_space=pl.ANY),
        scratch_shapes=[pltpu.SemaphoreType.DMA((7,)), pltpu.SemaphoreType.DMA((7,)), pltpu.SemaphoreType.DMA],
    )(block)


def exchange_slabs(slabs, name):
    def body(in_ref, out_ref, send_sems, recv_sems, local_sem):
        x, y, c = _position()
        my = 4 * x + 2 * y + c
        mine = pltpu.make_async_copy(in_ref.at[my], out_ref.at[my], local_sem)
        mine.start()
        copies = []
        for k in range(1, N_DEV):
            dx, dy, dc = (k >> 2) & 1, (k >> 1) & 1, k & 1
            px = x if dx == 0 else 1 - x
            py = y if dy == 0 else 1 - y
            pc = c if dc == 0 else 1 - c
            cp = pltpu.make_async_remote_copy(
                src_ref=in_ref.at[4 * px + 2 * py + pc], dst_ref=out_ref.at[my],
                send_sem=send_sems.at[k - 1], recv_sem=recv_sems.at[k - 1],
                device_id=(px, py, pc), device_id_type=pl.DeviceIdType.MESH)
            cp.start()
            copies.append(cp)
        for cp in copies:
            cp.wait()
        mine.wait()

    return pl.pallas_call(
        body, name=name,
        out_shape=jax.ShapeDtypeStruct(slabs.shape, slabs.dtype),
        in_specs=[pl.BlockSpec(memory_space=pl.ANY)],
        out_specs=pl.BlockSpec(memory_space=pl.ANY),
        scratch_shapes=[pltpu.SemaphoreType.DMA((7,)), pltpu.SemaphoreType.DMA((7,)), pltpu.SemaphoreType.DMA],
    )(slabs)


def _pack(arrays, dtype, row_multiple):
    flat = jnp.concatenate([a.astype(dtype).reshape(-1) for a in arrays])
    n = flat.shape[0]
    per = PACK_W * row_multiple
    total = -(-n // per) * per
    flat = jnp.pad(flat, (0, total - n))
    return flat.reshape(total // PACK_W, PACK_W)


def _unpack(flat2d, shapes, lead=()):
    flat = flat2d.reshape(lead + (-1,))
    out, off = [], 0
    for shp in shapes:
        n = math.prod(shp)
        out.append(flat[..., off:off + n].reshape(lead + tuple(shp)))
        off += n
    return out


def _full_from_gathered(g, axis):
    g = jnp.moveaxis(g, 0, axis)
    shp = list(g.shape)
    shp[axis:axis + 2] = [shp[axis] * shp[axis + 1]]
    return g.reshape(shp)


def _shards_of_full(full, axis):
    shp = list(full.shape)
    shp[axis:axis + 1] = [N_DEV, shp[axis] // N_DEV]
    return jnp.moveaxis(full.reshape(shp), axis, 0)


def layer_units(i):
    mixer = [('m_in_w', 'm_out_w'), ('h_in_w', 'h_out_w'), ('g_in_w', 'g_out_w')][i % 3]
    return [(mixer[0], i // 3), (mixer[1], i // 3), ('xa_q', i), ('xa_kv', i), ('xa_o', i), ('f_up', i), ('f_down', i)]


PADDED_COLS = {'m_in_w': M_IN_PAD, 'g_in_w': G_IN_PAD}
BIG_WEIGHTS = ('m_in_w', 'h_in_w', 'g_in_w', 'f_up')


def whole_weight(name, gathered):
    w = _full_from_gathered(lax.stop_gradient(gathered), SHARD_AXIS[name] - 1)
    return _pad_cols(w, 1, PADDED_COLS[name]) if name in PADDED_COLS else w


def _trunk(p, weights, blocks, standins, x, mem, bl, seq, pending=None):
    t = bl * seq
    ia = ib = ic = 0
    weights = dict(weights)
    state = {}

    def lin(name, a, wname, idx, residual=None):
        unit = (wname, idx)
        pos = state['units'].index(unit)
        later = state['next'][pos] if state['next'] else None
        nxt = (blocks[later],) if later in blocks and later not in state['by_core'] else ()
        n_real = N_DEV * standins[unit].shape[2]
        res = () if residual is None else (residual,)
        mixer_in = wname in ('m_in_w', 'h_in_w', 'g_in_w')
        defer = (pending, unit) if pending is not None and not (mixer_in and i == 0) else None
        y, got = make_linear(name, SHARD_AXIS[wname] - 1, n_real, bool(res), defer)(
            a, weights[unit], standins[unit], nxt, res)
        if nxt:
            weights[later] = whole_weight(later[0], got[0])
        return y

    by_seq = lambda a: a.reshape(bl, seq, a.shape[-1])

    for i in range(DEPTH):
        state['units'] = layer_units(i)
        state['next'] = layer_units(i + 1) if i + 1 < DEPTH else None
        state['by_core'] = [u for u in (state['next'] or []) if u in blocks and u[0] in BIG_WEIGHTS]
        core_blocks = [blocks[u] for u in state['by_core']]

        def core(op, *args):
            y, *got = op(*args, *core_blocks)
            for u, g in zip(state['by_core'], got):
                weights[u] = whole_weight(u[0], g)
            return y
        hn, x = rmsnorm_op(f"ln_mix{i}", t, F32, residual=True)(x, p['ln_mix'][i:i + 1])
        kind = i % 3
        if kind == 0:
            proj = lin(f"m_in{i}", hn, 'm_in_w', ia)
            xbc = make_conv(f"m_conv{i}", bl, seq, 4, M_CONV_DIM, M_D_INNER)(
                proj, p['m_conv_w'][ia], p['m_conv_b'][ia])
            dt = proj[:, M_D_INNER + M_CONV_DIM:M_IN].reshape(bl, seq, M_GROUPS, 4).transpose(2, 0, 1, 3)
            grp = lambda a, n=4: a.reshape(M_GROUPS, 1, n)
            proj3, xbc3 = by_seq(proj), by_seq(xbc)
            y = core(ssd_op(f"ssd{i}", bl, seq, pending, len(core_blocks)),
                     proj3, xbc3, xbc3, xbc3, dt, grp(p['m_dt_bias'][ia]), grp(p['m_a_log'][ia]), grp(p['m_d'][ia]),
                     grp(p['m_norm_w'][ia], 256))
            x = lin(f"m_out{i}", y.reshape(t, M_D_INNER), 'm_out_w', ia, residual=x)
            ia += 1
        elif kind == 1:
            proj3 = by_seq(lin(f"h_in{i}", hn, 'h_in_w', ib))
            y = core(gla_op(f"gla{i}", i, bl, seq, pending, len(core_blocks)),
                     proj3, proj3, proj3, proj3, p['h_lower_bounds'], p['h_norm_w'][ib:ib + 1])
            x = lin(f"h_out{i}", y.reshape(t, D_MODEL), 'h_out_w', ib, residual=x)
            ib += 1
        else:
            proj = lin(f"g_in{i}", hn, 'g_in_w', ic)
            qkv = make_conv(f"g_conv{i}", bl, seq, 4, G_CONV_DIM, 0)(proj, p['g_conv_w'][ic])
            c0 = G_CONV_DIM + G_VAL_DIM
            heads = lambda a: a.reshape(bl, seq, G_QK_HEADS, 2).transpose(2, 0, 1, 3)
            braw = heads(proj[:, c0:c0 + G_V_HEADS])
            araw = heads(proj[:, c0 + G_V_HEADS:c0 + 2 * G_V_HEADS])
            grp = lambda a: a.reshape(G_QK_HEADS, 1, 2)
            qkv3 = by_seq(qkv)
            y = core(gdn_op(f"gdn{i}", bl, seq, pending, len(core_blocks)),
                     qkv3, qkv3, qkv3, by_seq(proj), braw, araw, grp(p['g_a_log'][ic]), grp(p['g_dt_bias'][ic]),
                     p['g_norm_w'][ic:ic + 1])
            x = lin(f"g_out{i}", y.reshape(t, G_VAL_DIM), 'g_out_w', ic, residual=x)
            ic += 1
        hq, x = rmsnorm_op(f"ln_xattn{i}", t, F32, residual=True)(x, p['ln_xattn'][i:i + 1])
        mn = rmsnorm_op(f"ln_mem{i}", bl * N_MEM, F32)(mem, p['ln_mem'][i:i + 1])[0]
        qx = lin(f"xa_q{i}", hq, 'xa_q', i)
        kv = lin(f"xa_kv{i}", mn, 'xa_kv', i)
        ao = xattn_op(f"xattn{i}", bl, seq)(qx, kv, kv)[0]
        x = lin(f"xa_o{i}", ao, 'xa_o', i, residual=x)
        hf, x = rmsnorm_op(f"ln_ffn{i}", t, F32, residual=True)(x, p['ln_ffn'][i:i + 1])
        up = lin(f"f_up{i}", hf, 'f_up', i)
        act = make_conv(f"f_conv{i}", bl, seq, 3, D_FF, 0, up_col0=D_FF)(up, p['f_conv_w'][i], p['f_conv_b'][i])
        x = lin(f"f_down{i}", act, 'f_down', i, residual=x)
    return x


def _pad_cols(w, axis, to):
    pad = [(0, 0)] * w.ndim
    pad[axis] = (0, to - w.shape[axis])
    return jnp.pad(w, pad)


def kernel(x, mem, ln_mix, ln_xattn, ln_mem, ln_ffn, final_norm, m_in_w, m_conv_w, m_conv_b, m_dt_bias, m_a_log, m_d, m_norm_w, m_out_w, h_in_w, h_lower_bounds, h_norm_w, h_out_w, g_in_w, g_conv_w, g_a_log, g_dt_bias, g_norm_w, g_out_w, xa_q, xa_kv, xa_o, f_up, f_conv_w, f_conv_b, f_down, loss_target, m_ln_mix, m_ln_xattn, m_ln_mem, m_ln_ffn, m_final_norm, m_m_in_w, m_m_conv_w, m_m_conv_b, m_m_dt_bias, m_m_a_log, m_m_d, m_m_norm_w, m_m_out_w, m_h_in_w, m_h_lower_bounds, m_h_norm_w, m_h_out_w, m_g_in_w, m_g_conv_w, m_g_a_log, m_g_dt_bias, m_g_norm_w, m_g_out_w, m_xa_q, m_xa_kv, m_xa_o, m_f_up, m_f_conv_w, m_f_conv_b, m_f_down, v_ln_mix, v_ln_xattn, v_ln_mem, v_ln_ffn, v_final_norm, v_m_in_w, v_m_conv_w, v_m_conv_b, v_m_dt_bias, v_m_a_log, v_m_d, v_m_norm_w, v_m_out_w, v_h_in_w, v_h_lower_bounds, v_h_norm_w, v_h_out_w, v_g_in_w, v_g_conv_w, v_g_a_log, v_g_dt_bias, v_g_norm_w, v_g_out_w, v_xa_q, v_xa_kv, v_xa_o, v_f_up, v_f_conv_w, v_f_conv_b, v_f_down):
    local = dict(ln_mix=ln_mix, ln_xattn=ln_xattn, ln_mem=ln_mem, ln_ffn=ln_ffn, final_norm=final_norm, m_in_w=m_in_w, m_conv_w=m_conv_w, m_conv_b=m_conv_b, m_dt_bias=m_dt_bias, m_a_log=m_a_log, m_d=m_d, m_norm_w=m_norm_w, m_out_w=m_out_w, h_in_w=h_in_w, h_lower_bounds=h_lower_bounds, h_norm_w=h_norm_w, h_out_w=h_out_w, g_in_w=g_in_w, g_conv_w=g_conv_w, g_a_log=g_a_log, g_dt_bias=g_dt_bias, g_norm_w=g_norm_w, g_out_w=g_out_w, xa_q=xa_q, xa_kv=xa_kv, xa_o=xa_o, f_up=f_up, f_conv_w=f_conv_w, f_conv_b=f_conv_b, f_down=f_down)
    mom_m = dict(ln_mix=m_ln_mix, ln_xattn=m_ln_xattn, ln_mem=m_ln_mem, ln_ffn=m_ln_ffn, final_norm=m_final_norm, m_in_w=m_m_in_w, m_conv_w=m_m_conv_w, m_conv_b=m_m_conv_b, m_dt_bias=m_m_dt_bias, m_a_log=m_m_a_log, m_d=m_m_d, m_norm_w=m_m_norm_w, m_out_w=m_m_out_w, h_in_w=m_h_in_w, h_lower_bounds=m_h_lower_bounds, h_norm_w=m_h_norm_w, h_out_w=m_h_out_w, g_in_w=m_g_in_w, g_conv_w=m_g_conv_w, g_a_log=m_g_a_log, g_dt_bias=m_g_dt_bias, g_norm_w=m_g_norm_w, g_out_w=m_g_out_w, xa_q=m_xa_q, xa_kv=m_xa_kv, xa_o=m_xa_o, f_up=m_f_up, f_conv_w=m_f_conv_w, f_conv_b=m_f_conv_b, f_down=m_f_down)
    mom_v = dict(ln_mix=v_ln_mix, ln_xattn=v_ln_xattn, ln_mem=v_ln_mem, ln_ffn=v_ln_ffn, final_norm=v_final_norm, m_in_w=v_m_in_w, m_conv_w=v_m_conv_w, m_conv_b=v_m_conv_b, m_dt_bias=v_m_dt_bias, m_a_log=v_m_a_log, m_d=v_m_d, m_norm_w=v_m_norm_w, m_out_w=v_m_out_w, h_in_w=v_h_in_w, h_lower_bounds=v_h_lower_bounds, h_norm_w=v_h_norm_w, h_out_w=v_h_out_w, g_in_w=v_g_in_w, g_conv_w=v_g_conv_w, g_a_log=v_g_a_log, g_dt_bias=v_g_dt_bias, g_norm_w=v_g_norm_w, g_out_w=v_g_out_w, xa_q=v_xa_q, xa_kv=v_xa_kv, xa_o=v_xa_o, f_up=v_f_up, f_conv_w=v_f_conv_w, f_conv_b=v_f_conv_b, f_down=v_f_down)

    bl, seq, _ = x.shape
    t = bl * seq

    p = {n: local[n] for n in WEIGHTS if n not in SHARD_AXIS}
    for n in SMALL_SHARDED:
        p[n] = _full_from_gathered(all_gather(local[n], f"gather_{n}"), SHARD_AXIS[n])
    units = [(n, l) for n in MATMUL_WEIGHTS for l in range(local[n].shape[0])]
    block = lambda u: local[u[0]][u[1]].astype(BF16)
    weights = {u: whole_weight(u[0], all_gather(block(u), f"gather_{u[0]}{u[1]}")) for u in layer_units(0)}
    blocks = {u: block(u) for u in units if u not in weights}
    standins = {u: jnp.zeros((N_DEV,) + local[u[0]].shape[1:], BF16) for u in units}
    small = {n: p[n] for n in WEIGHTS if n not in MATMUL_WEIGHTS and n != 'final_norm'}

    pending = Pending()

    def run(small_w, standins_, xin):
        return _trunk(small_w, weights, blocks, standins_, xin, mem.reshape(bl * N_MEM, D_MODEL), bl, seq, pending)

    x_out, vjp = jax.vjp(run, small, standins, x.reshape(t, D_MODEL))
    loss_part, dx_out, d_final = loss_head(x_out, final_norm, loss_target.reshape(t, D_MODEL))
    grads, received, dx = vjp(dx_out)
    received = dict(received)
    for unit, slabs in pending.take_all():
        pending.received[unit] = exchange_slabs(slabs, f"exchange_{unit[0]}{unit[1]}")
    received.update(pending.received)
    grads = dict(grads)
    grads['final_norm'] = d_final
    loss = lax.psum(loss_part, ("x", "y", "c"))

    outs = {}

    def update(name, n, slots, shape, sel=lambda a: a):
        two_d = lambda a: sel(a).reshape(slots.shape[1:])
        got = reduce_adamw(slots, two_d(local[n]), two_d(mom_m[n]), two_d(mom_v[n]), name)
        return [g.reshape(shape) for g in got]

    for n in SMALL_SHARDED:
        slots = exchange_slabs(_shards_of_full(grads[n], SHARD_AXIS[n]), f"exchange_{n}")
        slots = slots.reshape(N_DEV, -1, slots.shape[-1])
        for kind, a in zip(KINDS, update(f"adamw_{n}", n, slots, local[n].shape)):
            outs[kind, n] = a
    for n in MATMUL_WEIGHTS:
        per_layer = [update(f"adamw_{n}{l}", n, received[n, l], local[n].shape[1:], lambda a, l=l: a[l])
                     for l in range(local[n].shape[0])]
        for k, kind in enumerate(KINDS):
            outs[kind, n] = jnp.concatenate([got[k][None] for got in per_layer])
    replicated = [n for n in WEIGHTS if n not in SHARD_AXIS]
    pk = lambda d: _pack([d[n] for n in replicated], F32, 8)
    got = reduce_adamw(all_gather(pk(grads), "gather_replicated_grads"), pk(local), pk(mom_m), pk(mom_v),
                       "adamw_replicated")
    shapes = [local[n].shape for n in replicated]
    for kind, buf in zip(KINDS, got):
        for n, a in zip(replicated, _unpack(buf, shapes)):
            outs[kind, n] = a
    result = [loss, dx.reshape(bl, seq, D_MODEL)]
    for kind in KINDS:
        result += [outs[kind, n] for n in WEIGHTS]
    return tuple(result)
```

```python
import functools
import math

import jax
import jax.numpy as jnp
from jax import lax
from jax.experimental import pallas as pl
from jax.experimental.pallas import tpu as pltpu

F32 = jnp.float32
BF16 = jnp.bfloat16
NN = (((1,), (0,)), ((), ()))
NT = (((1,), (1,)), ((), ()))
TN = (((0,), (0,)), ((), ()))

D_MODEL = 1024
DEPTH = 4
EPS = 1e-6
N_MEM = 256
M_D_INNER = 2048
M_HEADS = 32
M_GROUPS = 8
M_STATE = 128
M_CONV_DIM = 4096
M_IN = 6176
M_IN_PAD = 6272
SSD_CHUNK = 256
H_HEADS = 8
HGRN_CHUNK = 32
HGRN_ROWS = 256
G_QK_HEADS = 8
G_V_HEADS = 16
G_KEY_DIM = 1024
G_VAL_DIM = 2048
G_CONV_DIM = 4096
G_IN = 6176
G_IN_PAD = 6272
GDN_CHUNK = 64
X_HEADS = 4
X_HEAD_DIM = 256
D_FF = 2816
ADAM_LR = 0.001
ADAM_B1 = 0.9
ADAM_B2 = 0.999
ADAM_EPS = 1e-08
ADAM_WD = 0.01
ADAM_STEP = 10

N_DEV = 8
LANE = 128
KINDS = ('grad', 'delta', 'new_m', 'new_v')
VMEM_LIMIT = 56 * 1024 * 1024

WEIGHTS = ['ln_mix', 'ln_xattn', 'ln_mem', 'ln_ffn', 'final_norm', 'm_in_w', 'm_conv_w', 'm_conv_b', 'm_dt_bias',
           'm_a_log', 'm_d', 'm_norm_w', 'm_out_w', 'h_in_w', 'h_lower_bounds', 'h_norm_w', 'h_out_w', 'g_in_w',
           'g_conv_w', 'g_a_log', 'g_dt_bias', 'g_norm_w', 'g_out_w', 'xa_q', 'xa_kv', 'xa_o', 'f_up', 'f_conv_w',
           'f_conv_b', 'f_down']
SHARD_AXIS = {'m_in_w': 2, 'm_conv_w': 2, 'm_conv_b': 1, 'm_norm_w': 1, 'm_out_w': 1, 'h_in_w': 2, 'h_out_w': 1,
              'g_in_w': 2, 'g_conv_w': 2, 'g_out_w': 1, 'xa_q': 1, 'xa_kv': 2, 'xa_o': 1, 'f_up': 2, 'f_conv_w': 2,
              'f_down': 1}
MATMUL_WEIGHTS = ['m_in_w', 'm_out_w', 'h_in_w', 'h_out_w', 'g_in_w', 'g_out_w', 'xa_q', 'xa_kv', 'xa_o', 'f_up',
                  'f_down']
SMALL_SHARDED = ['m_conv_w', 'm_conv_b', 'm_norm_w', 'g_conv_w', 'f_conv_w']


def _cparams():
    return pltpu.CompilerParams(vmem_limit_bytes=VMEM_LIMIT)


def bdot(a, b, dims=NN):
    return lax.dot_general(a.astype(BF16), b.astype(BF16), dims, preferred_element_type=F32)


def _split(a):
    hi = a.astype(BF16)
    return hi, (a - hi.astype(F32)).astype(BF16)


def _h3(a, b, dims):
    ah, al = _split(a)
    bh, bl = _split(b)
    d = functools.partial(lax.dot_general, dimension_numbers=dims, preferred_element_type=F32)
    return d(ah, bh) + (d(ah, bl) + d(al, bh))


BNN = (((2,), (1,)), ((0,), (0,)))
BNT = (((2,), (2,)), ((0,), (0,)))
BTN = (((1,), (1,)), ((0,), (0,)))


@jax.custom_vjp
def h3dot_b(a, b):
    return _h3(a, b, BNN)


h3dot_b.defvjp(lambda a, b: (_h3(a, b, BNN), (a, b)),
               lambda res, ct: (_h3(ct, res[1], BNT), _h3(res[0], ct, BTN)))

T_ROWS = (((0,), (1,)), ((), ()))


def _tri_times(tri, x, dims, tri_first):
    t = tri.astype(BF16)
    x0 = x.astype(BF16)
    r1 = x - x0.astype(F32)
    x1 = r1.astype(BF16)
    x2 = (r1 - x1.astype(F32)).astype(BF16)
    if tri_first:
        d = lambda xx: lax.dot_general(t, xx, dims, preferred_element_type=F32)
    else:
        d = lambda xx: lax.dot_general(xx, t, dims, preferred_element_type=F32)
    return d(x0) + (d(x1) + d(x2))


@jax.custom_vjp
def cumdot(tri, x):
    return _tri_times(tri, x, NN, True)


cumdot.defvjp(lambda tri, x: (_tri_times(tri, x, NN, True), tri),
              lambda tri, ct: (jnp.zeros_like(tri), _tri_times(tri, ct, TN, True)))


@jax.custom_vjp
def cumdot_t(tri, x):
    return _tri_times(tri, x, T_ROWS, False)


cumdot_t.defvjp(lambda tri, x: (_tri_times(tri, x, T_ROWS, False), tri),
                lambda tri, ct: (jnp.zeros_like(tri), _tri_times(tri, ct, T_ROWS, True)))


def _tile(dim, cap):
    if dim <= cap:
        return dim
    best = None
    for t in range(LANE, cap + 1, LANE):
        if dim % t == 0:
            best = t
    assert best is not None, dim
    return best


def _position():
    return lax.axis_index("x"), lax.axis_index("y"), lax.axis_index("c")


def _direct_copies(kind, src_ref, dst_ref, send_sems, recv_sems, local_sem):
    x, y, c = _position()
    me = 4 * x + 2 * y + c
    local_src = src_ref if kind == 'gather' else src_ref.at[me]
    copies = [pltpu.make_async_copy(local_src, dst_ref.at[me], local_sem)]
    for k in range(1, N_DEV):
        px = 1 - x if (k >> 2) & 1 else x
        py = 1 - y if (k >> 1) & 1 else y
        pc = 1 - c if k & 1 else c
        copies.append(pltpu.make_async_remote_copy(
            src_ref=src_ref if kind == 'gather' else src_ref.at[4 * px + 2 * py + pc], dst_ref=dst_ref.at[me],
            send_sem=send_sems.at[k - 1], recv_sem=recv_sems.at[k - 1],
            device_id=(px, py, pc), device_id_type=pl.DeviceIdType.MESH))
    return copies


COMM_SCRATCH = [pltpu.SemaphoreType.DMA((N_DEV - 1,)), pltpu.SemaphoreType.DMA((N_DEV - 1,)), pltpu.SemaphoreType.DMA]


class Pending:
    def __init__(self):
        self.jobs, self.received = [], {}

    def take_all(self):
        jobs, self.jobs = self.jobs, []
        return jobs


def matmul(a, b, *, ta=False, tb=False, out_dtype=F32, name="mm", carry=None, residual=None):
    if ta:
        k, m = a.shape
    else:
        m, k = a.shape
    if tb:
        n, k2 = b.shape
    else:
        k2, n = b.shape
    assert k == k2, (a.shape, b.shape, ta, tb)
    tm = _tile(m, 1408)
    tn = _tile(n, 1408)
    tk = _tile(k, 1408)
    grid = (m // tm, n // tn, k // tk)
    nk = grid[2]
    dims = (((0 if ta else 1,), (1 if tb else 0,)), ((), ()))

    def at_step(which):
        conds = [pl.program_id(ax) == (0 if which == 'first' else grid[ax] - 1) for ax in range(3)]
        return jnp.logical_and(jnp.logical_and(conds[0], conds[1]), conds[2])

    def body(*refs):
        r_ref = None
        if residual is not None:
            r_ref, refs = refs[2], refs[:2] + refs[3:]
        if carry is None:
            a_ref, b_ref, o_ref, acc_ref = refs
        else:
            a_ref, b_ref, src_ref, o_ref, dst_ref, acc_ref, send_sems, recv_sems, local_sem = refs
            copies = lambda: _direct_copies(carry[0], src_ref, dst_ref, send_sems, recv_sems, local_sem)

            @pl.when(at_step('first'))
            def _():
                for cp in copies():
                    cp.start()

        @pl.when(pl.program_id(2) == 0)
        def _():
            acc_ref[...] = jnp.zeros_like(acc_ref)

        acc_ref[...] += lax.dot_general(a_ref[...].astype(BF16), b_ref[...].astype(BF16), dims,
                                        preferred_element_type=F32)

        @pl.when(pl.program_id(2) == nk - 1)
        def _():
            out = acc_ref[...] if r_ref is None else acc_ref[...] + r_ref[...]
            o_ref[...] = out.astype(o_ref.dtype)

        if carry is not None:
            @pl.when(at_step('last'))
            def _():
                for cp in copies():
                    cp.wait()

    a_spec = pl.BlockSpec((tk, tm), lambda i, j, kk: (kk, i)) if ta else pl.BlockSpec((tm, tk), lambda i, j, kk: (i, kk))
    b_spec = pl.BlockSpec((tn, tk), lambda i, j, kk: (j, kk)) if tb else pl.BlockSpec((tk, tn), lambda i, j, kk: (kk, j))
    o_spec = pl.BlockSpec((tm, tn), lambda i, j, kk: (i, j))
    o_shape = jax.ShapeDtypeStruct((m, n), out_dtype)
    acc = pltpu.VMEM((tm, tn), F32)
    ins, in_specs = [a, b], [a_spec, b_spec]
    if residual is not None:
        ins.append(residual)
        in_specs.append(o_spec)
    if carry is None:
        return pl.pallas_call(
            body, name=name, grid=grid, in_specs=in_specs, out_specs=o_spec, out_shape=o_shape,
            scratch_shapes=[acc], compiler_params=_cparams(),
        )(*ins)
    kind, src = carry
    got = jax.ShapeDtypeStruct(((N_DEV,) + src.shape) if kind == 'gather' else src.shape, src.dtype)
    hbm = pl.BlockSpec(memory_space=pl.ANY)
    return pl.pallas_call(
        body, name=name, grid=grid, in_specs=in_specs + [hbm], out_specs=[o_spec, hbm],
        out_shape=[o_shape, got], scratch_shapes=[acc] + COMM_SCRATCH, compiler_params=_cparams(),
    )(*ins, src)


def make_linear(name, shard_axis, n_real, has_res=False, defer=None):
    def forward(a, w, nxt, res):
        r = res[0] if res else None
        if nxt:
            y, got = matmul(a, w, name=name + "_fwd", carry=('gather', nxt[0]), residual=r)
            return y, (got,)
        return matmul(a, w, name=name + "_fwd", residual=r), ()

    @jax.custom_vjp
    def linear(a, w, wg, nxt, res):
        return forward(a, w, nxt, res)

    def fwd(a, w, wg, nxt, res):
        return forward(a, w, nxt, res), (a, w, nxt)

    def bwd(saved, cts):
        a, w, nxt = saved
        dy = cts[0]
        dw = matmul(a, dy, ta=True, out_dtype=BF16, name=name + "_bwd_dw")
        slabs = _shards_of_full(dw[:, :n_real], shard_axis)
        if defer is None:
            da, slots = matmul(dy, w, tb=True, out_dtype=a.dtype, name=name + "_bwd_da", carry=('exchange', slabs))
        else:
            defer[0].jobs.append((defer[1], slabs))
            da, slots = matmul(dy, w, tb=True, out_dtype=a.dtype, name=name + "_bwd_da"), jnp.zeros_like(slabs)
        return da, jnp.zeros_like(w), slots, tuple(jnp.zeros_like(b) for b in nxt), ((dy,) if has_res else ())

    linear.defvjp(fwd, bwd)
    return linear


class In:
    def __init__(self, block, imap, kind='blk', inner=(), cols=None):
        self.block, self.imap, self.kind, self.inner, self.cols = block, imap, kind, inner, cols


class Out:
    def __init__(self, shape, dtype, block, imap):
        self.shape, self.dtype, self.block, self.imap = shape, dtype, block, imap


def make_op(name, fn, grid, ins, outs, state_shape=None, seq_axis=None, passthrough=(), pending=None, n_gather=0):
    n_in, n_out = len(ins), len(outs)
    has_state = state_shape is not None
    nd = len(grid)
    diff_idx = [i for i, s in enumerate(ins) if s.kind != 'const']

    def in_spec(s, reverse):
        off = 0
        if s.cols is not None:
            assert s.cols[0] % s.block[-1] == 0
            off = s.cols[0] // s.block[-1]

        def imap(*ids):
            ids = rev(ids) if reverse else ids
            idx = tuple(s.imap(*ids))
            return idx[:-1] + (idx[-1] + off,) if off else idx

        return pl.BlockSpec(s.block, imap)

    def rel_spec(block, f, reverse):
        return pl.BlockSpec(block, (lambda *ids: f(*rev(ids))) if reverse else f)

    def rev(ids):
        if not has_state:
            return ids
        ids = list(ids)
        ids[seq_axis] = grid[seq_axis] - 1 - ids[seq_axis]
        return tuple(ids)

    save_shape = tuple(grid) + tuple(state_shape) if has_state else None
    save_block = (None,) * nd + tuple(state_shape) if has_state else None

    def save_imap(*ids):
        return tuple(ids) + (0,) * len(state_shape)

    def step_is(which):
        conds = [pl.program_id(ax) == (0 if which == 'first' else grid[ax] - 1) for ax in range(nd)]
        return functools.reduce(jnp.logical_and, conds)

    def fwd_call(*xs):
        xs, blocks = xs[:n_in], xs[n_in:]
        n_save = 1 if has_state else 0

        def body(*refs):
            if blocks:
                src_refs = refs[n_in:n_in + n_gather]
                dst_refs = refs[n_in + n_gather + n_out + n_save:n_in + 2 * n_gather + n_out + n_save]
                sems = refs[len(refs) - 3 * n_gather:]
                refs = refs[:n_in] + refs[n_in + n_gather:n_in + n_gather + n_out + n_save] + \
                    refs[n_in + 2 * n_gather + n_out + n_save:len(refs) - 3 * n_gather]
                copies = lambda: [cp for k in range(n_gather) for cp in _direct_copies(
                    'gather', src_refs[k], dst_refs[k], sems[3 * k], sems[3 * k + 1], sems[3 * k + 2])]

                @pl.when(step_is('first'))
                def _():
                    for cp in copies():
                        cp.start()

            in_refs = refs[:n_in]
            out_refs = refs[n_in:n_in + n_out]
            vals = [r[...] for r in in_refs]
            if has_state:
                save_ref, st_ref = refs[n_in + n_out], refs[n_in + n_out + 1]

                @pl.when(pl.program_id(seq_axis) == 0)
                def _():
                    st_ref[...] = jnp.zeros(state_shape, F32)

                st = st_ref[...]
                save_ref[...] = st
                res = fn(*vals, st)
                st_ref[...] = res[-1]
                res = res[:-1]
            else:
                res = fn(*vals)
            for o, v in zip(out_refs, res):
                o[...] = v.astype(o.dtype)

            if blocks:
                @pl.when(step_is('last'))
                def _():
                    for cp in copies():
                        cp.wait()

        out_shape = [jax.ShapeDtypeStruct(o.shape, o.dtype) for o in outs]
        out_specs = [pl.BlockSpec(o.block, o.imap) for o in outs]
        scratch = []
        if has_state:
            out_shape.append(jax.ShapeDtypeStruct(save_shape, F32))
            out_specs.append(pl.BlockSpec(save_block, save_imap))
            scratch.append(pltpu.VMEM(state_shape, F32))
        hbm = pl.BlockSpec(memory_space=pl.ANY)
        for blk in blocks:
            out_shape.append(jax.ShapeDtypeStruct((N_DEV,) + blk.shape, blk.dtype))
            out_specs.append(hbm)
            scratch += COMM_SCRATCH
        return pl.pallas_call(
            body, name=name + "_fwd", grid=grid,
            in_specs=[in_spec(s, False) for s in ins] + [hbm] * len(blocks),
            out_specs=out_specs, out_shape=out_shape, scratch_shapes=scratch,
            compiler_params=_cparams(),
        )(*xs, *blocks)

    def grad_shape(s, x):
        if s.cols is not None:
            return x.shape[:-1] + (s.cols[1],)
        return x.shape

    def bwd_call(xs, save, cts, pass_cts=()):
        n_diff = len(diff_idx)
        jobs = pending.take_all() if pending is not None else []
        n_args = n_in + (1 if has_state else 0) + n_out + len(passthrough)

        def body(*refs):
            if jobs:
                src_refs = refs[n_args:n_args + len(jobs)]
                dst_refs = refs[n_args + len(jobs) + n_diff:n_args + 2 * len(jobs) + n_diff]
                sems = refs[len(refs) - 3 * len(jobs):]
                refs = refs[:n_args] + refs[n_args + len(jobs):n_args + len(jobs) + n_diff] + \
                    refs[n_args + 2 * len(jobs) + n_diff:len(refs) - 3 * len(jobs)]
                copies = lambda: [cp for k in range(len(jobs)) for cp in _direct_copies(
                    'exchange', src_refs[k], dst_refs[k], sems[3 * k], sems[3 * k + 1], sems[3 * k + 2])]

                @pl.when(step_is('first'))
                def _():
                    for cp in copies():
                        cp.start()

            in_refs = refs[:n_in]
            p = n_in
            if has_state:
                save_ref = refs[p]
                p += 1
            ct_refs = refs[p:p + n_out]
            p += n_out
            pass_refs = dict(zip(passthrough, refs[p:p + len(passthrough)]))
            p += len(passthrough)
            g_refs = refs[p:p + n_diff]
            p += n_diff
            vals = [r[...] for r in in_refs]

            def g(*dv):
                full = list(vals)
                for i, v in zip(diff_idx, dv):
                    full[i] = v
                if has_state:
                    return tuple(fn(*full, dv[-1]))
                return tuple(fn(*full))

            prim = [vals[i] for i in diff_idx]
            ct = tuple(r[...].astype(F32) for r in ct_refs)
            if has_state:
                dst_ref = refs[p]

                @pl.when(pl.program_id(seq_axis) == 0)
                def _():
                    dst_ref[...] = jnp.zeros(state_shape, F32)

                prim = prim + [save_ref[...]]
                ct = ct + (dst_ref[...],)
            _, vjp = jax.vjp(g, *prim)
            grads = vjp(ct)
            for k, i in enumerate(diff_idx):
                s = ins[i]
                if s.kind == 'blk':
                    g = grads[k] + pass_refs[i][...] if i in pass_refs else grads[k]
                    g_refs[k][...] = g.astype(g_refs[k].dtype)
                else:
                    first = None
                    for ax in s.inner:
                        c = pl.program_id(ax) == 0
                        first = c if first is None else jnp.logical_and(first, c)

                    @pl.when(first)
                    def _(k=k):
                        g_refs[k][...] = jnp.zeros_like(g_refs[k])

                    g_refs[k][...] += grads[k].astype(g_refs[k].dtype)
            if has_state:
                dst_ref[...] = grads[-1]

            if jobs:
                @pl.when(step_is('last'))
                def _():
                    for cp in copies():
                        cp.wait()

        in_specs = [in_spec(s, True) for s in ins]
        args = list(xs)
        if has_state:
            in_specs.append(rel_spec(save_block, save_imap, True))
            args.append(save)
        for o, c in zip(outs, cts):
            in_specs.append(rel_spec(o.block, o.imap, True))
            args.append(c)
        for i, c in zip(passthrough, pass_cts):
            assert ins[i].kind == 'blk' and ins[i].cols is None
            in_specs.append(rel_spec(ins[i].block, ins[i].imap, True))
            args.append(c)
        out_shape, out_specs = [], []
        for i in diff_idx:
            s = ins[i]
            out_shape.append(jax.ShapeDtypeStruct(grad_shape(s, xs[i]), xs[i].dtype))
            out_specs.append(rel_spec(s.block, s.imap, True))
        scratch = [pltpu.VMEM(state_shape, F32)] if has_state else []
        hbm = pl.BlockSpec(memory_space=pl.ANY)
        for _, slabs in jobs:
            in_specs.append(hbm)
            args.append(slabs)
            out_specs.append(hbm)
            out_shape.append(jax.ShapeDtypeStruct(slabs.shape, slabs.dtype))
            scratch += COMM_SCRATCH
        got = pl.pallas_call(
            body, name=name + "_bwd", grid=grid,
            in_specs=in_specs, out_specs=out_specs, out_shape=out_shape, scratch_shapes=scratch,
            compiler_params=_cparams(),
        )(*args)
        for (unit, _), slots in zip(jobs, got[n_diff:]):
            pending.received[unit] = slots
        return got[:n_diff]

    def results(xs, res):
        gathered = tuple(res[len(res) - n_gather:]) if n_gather else ()
        return tuple(res[:n_out]) + tuple(xs[i] for i in passthrough) + gathered

    @jax.custom_vjp
    def op(*xs):
        return results(xs, fwd_call(*xs))

    def op_fwd(*xs):
        res = fwd_call(*xs)
        return results(xs, res), (xs, res[n_out] if has_state else None)

    def op_bwd(resid, cts):
        xs, save = resid
        xs, blocks = xs[:n_in], xs[n_in:]
        grads = bwd_call(xs, save, cts[:n_out], cts[n_out:n_out + len(passthrough)])
        out = []
        k = 0
        for i, s in enumerate(ins):
            if s.kind == 'const':
                out.append(jnp.zeros_like(xs[i]))
                continue
            g = grads[k]
            k += 1
            if s.cols is not None:
                g = jnp.pad(g, ((0, 0),) * (g.ndim - 1) + ((s.cols[0], xs[i].shape[-1] - s.cols[0] - s.cols[1]),))
            out.append(g)
        return tuple(out) + tuple(jnp.zeros_like(b) for b in blocks)

    op.defvjp(op_fwd, op_bwd)
    return op


def _rms(x, w):
    return x * lax.rsqrt(jnp.mean(x * x, axis=-1, keepdims=True) + EPS) * w


def _silu(x):
    return x * jax.nn.sigmoid(x)


def rmsnorm_op(name, t, out_dtype, residual=False):
    tm = _tile(t, 512)
    return make_op(
        name, lambda x, w: (_rms(x, w),), (t // tm,),
        [In((tm, D_MODEL), lambda i: (i, 0)), In((1, D_MODEL), lambda i: (0, 0), 'acc', (0,))],
        [Out((t, D_MODEL), out_dtype, (tm, D_MODEL), lambda i: (i, 0))], passthrough=(0,) if residual else ())


def _tri(q):
    ii = lax.broadcasted_iota(jnp.int32, (q, q), 0)
    jj = lax.broadcasted_iota(jnp.int32, (q, q), 1)
    return ii >= jj, ii > jj


def _ssd_fn(z, x, bm, cm, dtr, dtb, alog, dsk, nw, state):
    q = x.shape[0]
    incl, _ = _tri(q)
    tril = incl.astype(F32)
    dt = jax.nn.softplus(dtr + dtb)
    da = dt * (-jnp.exp(alog))
    acum = cumdot(tril, da)
    acum_t = cumdot_t(tril, da)
    cb = bdot(cm, bm, NT)
    heads = range(4)
    wide = lambda a: jnp.concatenate([jnp.broadcast_to(a[:, r:r + 1], (a.shape[0], 64)) for r in heads], axis=1)
    last = acum[q - 1:q, :]
    xc = x * wide(dt)
    y = bdot(cm, state, NT) * wide(jnp.exp(acum)) + wide(dsk) * x
    ds = bdot(xc * wide(jnp.exp(last - acum)), bm, TN)
    e_last = jnp.exp(last)
    new_state = state * jnp.concatenate([jnp.broadcast_to(e_last[:, r:r + 1], (64, 1)) for r in heads], axis=0) + ds
    diag = []
    for r in heads:
        decay = jnp.exp(jnp.where(incl, acum[:, r:r + 1] - acum_t[r:r + 1, :], -jnp.inf))
        diag.append(bdot(cb * decay, xc[:, 64 * r:64 * r + 64]))
    y = y + jnp.concatenate(diag, axis=1)
    yz = y * _silu(z)
    return _rms(yz, nw), new_state


def _per_sequence(fn, n_seq_args, bl):
    def f(*args):
        *ins, state = args
        res = [fn(*[a[b] for a in ins[:n_seq_args]], *ins[n_seq_args:], state[b]) for b in range(bl)]
        return tuple(jnp.concatenate([r[k][None] for r in res]) for k in range(len(res[0])))

    return f


def ssd_op(name, bl, seq, pending=None, n_gather=0):
    q = SSD_CHUNK
    nc = seq // q
    blk = lambda w, c0, cw: In((bl, q, w), lambda g, n: (0, n, g), cols=(c0, cw))
    small = lambda g, n: (g, 0, 0)
    ins = [
        blk(256, 0, M_D_INNER),
        blk(256, 0, M_D_INNER),
        blk(128, M_D_INNER, 1024),
        blk(128, M_D_INNER + 1024, 1024),
        In((None, bl, q, 4), lambda g, n: (g, 0, n, 0)),
        In((None, 1, 4), small, 'acc', (1,)),
        In((None, 1, 4), small, 'acc', (1,)),
        In((None, 1, 4), small, 'acc', (1,)),
        In((None, 1, 256), small, 'acc', (1,)),
    ]
    outs = [Out((bl, seq, M_D_INNER), F32, (bl, q, 256), lambda g, n: (0, n, g))]
    return make_op(name, _per_sequence(_ssd_fn, 5, bl), (M_GROUPS, nc), ins, outs,
                   state_shape=(bl, 256, 128), seq_axis=1, pending=pending, n_gather=n_gather)


def _gla_fn(layer, qr, fr, ir, gr, lbp, nw, state_t):
    rows = qr.shape[0]
    c = HGRN_CHUNK
    n_chunks = rows // c
    e = jnp.exp(lbp - jnp.max(lbp, axis=0, keepdims=True))
    sm = e / jnp.sum(e, axis=0, keepdims=True)
    lb = jnp.sum(sm[1:layer + 1, :], axis=0, keepdims=True) if layer > 0 else jnp.zeros((1, lbp.shape[1]), F32)
    qq = _silu(qr) * (128 ** -0.5)
    forget = lb + (1.0 - lb) * jax.nn.sigmoid(fr)
    kk = 1.0 - forget
    logf = jnp.log(forget)
    ii = lax.broadcasted_iota(jnp.int32, (rows, rows), 0)
    jj = lax.broadcasted_iota(jnp.int32, (rows, rows), 1)
    own = jnp.logical_and(ii >= jj, ii // c == jj // c)
    gc = cumdot(own.astype(F32), logf)
    glasts = [gc[c * j + c - 1:c * j + c, :] for j in range(n_chunks)]
    glast_rows = jnp.concatenate([jnp.broadcast_to(g, (c, g.shape[1])) for g in glasts], axis=0)
    q_dec = qq * jnp.exp(gc)
    k_inv = kk * jnp.exp(-gc)
    k_end = kk * jnp.exp(glast_rows - gc)
    att = jnp.where(own, bdot(q_dec, k_inv, NT), 0.0)
    o = bdot(att, ir)
    inter = []
    for j in range(n_chunks):
        sl = slice(c * j, c * j + c)
        inter.append(bdot(q_dec[sl], state_t, NT))
        state_t = state_t * jnp.exp(glasts[j]) + bdot(ir[sl], k_end[sl], TN)
    o = o + jnp.concatenate(inter, axis=0)
    return _rms(o, nw) * _silu(gr), state_t


def gla_op(name, layer, bl, seq, pending=None, n_gather=0):
    r = HGRN_ROWS
    ns = seq // r
    blk = lambda k: In((bl, r, 128), lambda h, n: (0, n, h), cols=(1024 * k, 1024))
    ins = [blk(0), blk(1), blk(2), blk(3),
           In((DEPTH, 128), lambda h, n: (0, h), 'acc', (1,)),
           In((1, 128), lambda h, n: (0, 0), 'acc', (0, 1))]
    outs = [Out((bl, seq, D_MODEL), F32, (bl, r, 128), lambda h, n: (0, n, h))]
    return make_op(name, _per_sequence(functools.partial(_gla_fn, layer), 4, bl), (H_HEADS, ns), ins, outs,
                   state_shape=(bl, 128, 128), seq_axis=1, pending=pending, n_gather=n_gather)


def _neumann_inverse(m):
    q = m.shape[1]
    ii = lax.broadcasted_iota(jnp.int32, (q, q), 0)
    jj = lax.broadcasted_iota(jnp.int32, (q, q), 1)
    eye = (ii == jj).astype(F32)[None]
    p = -m
    inv = eye + p
    for _ in range(int(math.log2(q)) - 1):
        p = _h3(p, p, BNN)
        inv = inv + _h3(inv, p, BNN)
    return inv


@jax.custom_vjp
def _unit_lower_inverse(m):
    return _neumann_inverse(m)


def _unit_lower_inverse_fwd(m):
    inv = _neumann_inverse(m)
    return inv, inv


_unit_lower_inverse.defvjp(_unit_lower_inverse_fwd,
                           lambda inv, ct: (-_h3(_h3(inv, ct, BTN), inv, BNT),))


def _gdn_fn(qc, kc, vc, zc, br, ar, alog, dtb, nw, state):
    bl, q = qc.shape[0], qc.shape[1]
    incl, strict = _tri(q)
    tril = incl.astype(F32)
    g = jnp.concatenate([-jnp.exp(alog) * jax.nn.softplus(ar[b] + dtb) for b in range(bl)], axis=1)
    gc = cumdot(tril, g)
    gc_t = cumdot_t(tril, g)
    heads, ms, rhs = [], [], []
    for b in range(bl):
        qn = qc[b] * lax.rsqrt(jnp.sum(qc[b] * qc[b], axis=-1, keepdims=True) + EPS) * (128 ** -0.5)
        kn = kc[b] * lax.rsqrt(jnp.sum(kc[b] * kc[b], axis=-1, keepdims=True) + EPS)
        beta = jax.nn.sigmoid(br[b])
        qk = bdot(qn, kn, NT)
        for j in range(2):
            i = 2 * b + j
            col = gc[:, i:i + 1]
            decay = jnp.exp(jnp.where(incl, col - gc_t[i:i + 1, :], -jnp.inf))
            bj = beta[:, j:j + 1]
            kb = kn * bj
            ms.append(jnp.where(strict, bdot(kb, kn, NT) * decay, 0.0))
            rhs.append(jnp.concatenate([vc[b][:, 128 * j:128 * j + 128] * bj, kb * jnp.exp(col)], axis=1))
            heads.append((qn, kn, qk * decay, col, gc[q - 1:q, i:i + 1]))
    sol = h3dot_b(_unit_lower_inverse(jnp.concatenate([m[None] for m in ms])),
                  jnp.concatenate([r[None] for r in rhs]))
    outs, states = [], []
    for b in range(bl):
        os_, sts = [], []
        for j in range(2):
            i = 2 * b + j
            qn, kn, att, col, glast = heads[i]
            u = sol[i][:, :128]
            w = sol[i][:, 128:]
            st = state[b][128 * j:128 * j + 128, :]
            v_new = u - bdot(w, st)
            o = bdot(qn * jnp.exp(col), st) + bdot(att, v_new)
            sts.append(st * jnp.exp(glast) + bdot(kn * jnp.exp(glast - col), v_new, TN))
            os_.append(_rms(o, nw) * _silu(zc[b][:, 128 * j:128 * j + 128]))
        outs.append(jnp.concatenate(os_, axis=1))
        states.append(jnp.concatenate(sts, axis=0))
    return jnp.concatenate([o[None] for o in outs]), jnp.concatenate([st[None] for st in states])


def gdn_op(name, bl, seq, pending=None, n_gather=0):
    q = GDN_CHUNK
    nc = seq // q
    blk = lambda w, c0, cw: In((bl, q, w), lambda h, n: (0, n, h), cols=(c0, cw))
    small = lambda h, n: (h, 0, 0)
    ins = [
        blk(128, 0, G_KEY_DIM),
        blk(128, G_KEY_DIM, G_KEY_DIM),
        blk(256, 2 * G_KEY_DIM, G_VAL_DIM),
        blk(256, G_CONV_DIM, G_VAL_DIM),
        In((None, bl, q, 2), lambda h, n: (h, 0, n, 0)),
        In((None, bl, q, 2), lambda h, n: (h, 0, n, 0)),
        In((None, 1, 2), small, 'acc', (1,)),
        In((None, 1, 2), small, 'acc', (1,)),
        In((1, 128), lambda h, n: (0, 0), 'acc', (0, 1)),
    ]
    outs = [Out((bl, seq, G_VAL_DIM), F32, (bl, q, 256), lambda h, n: (0, n, h))]
    return make_op(name, _gdn_fn, (G_QK_HEADS, nc), ins, outs,
                   state_shape=(bl, 256, 128), seq_axis=1, pending=pending, n_gather=n_gather)


def _xattn_fn(q, k, v):
    s = bdot(q, k, NT) * (X_HEAD_DIM ** -0.5)
    s = s - jnp.max(s, axis=-1, keepdims=True)
    p = jnp.exp(s)
    p = p / jnp.sum(p, axis=-1, keepdims=True)
    return (bdot(p, v),)


def xattn_op(name, bl, seq):
    tq = _tile(seq, 2048)
    nq = seq // tq
    t = bl * seq
    ins = [
        In((tq, X_HEAD_DIM), lambda b, h, i: (b * nq + i, h)),
        In((N_MEM, X_HEAD_DIM), lambda b, h, i: (b, h), 'acc', (2,), cols=(0, D_MODEL)),
        In((N_MEM, X_HEAD_DIM), lambda b, h, i: (b, h), 'acc', (2,), cols=(D_MODEL, D_MODEL)),
    ]
    outs = [Out((t, D_MODEL), F32, (tq, X_HEAD_DIM), lambda b, h, i: (b * nq + i, h))]
    return make_op(name, _xattn_fn, (bl, X_HEADS, nq), ins, outs)


CONV_PAD = 8
CONV_ROWS = 128


def make_conv(name, bl, seq, width, ch, x_col0, up_col0=None):
    cb = 256
    rt = CONV_ROWS
    assert ch % cb == 0 and x_col0 % cb == 0 and (up_col0 is None or up_col0 % cb == 0) and seq % rt == 0
    nb = ch // cb
    n_tiles = seq // rt
    t = bl * seq
    has_up = up_col0 is not None
    grid = (nb, bl)
    x_spec = pl.BlockSpec((seq, cb), lambda c, b: (b, x_col0 // cb + c))
    up_specs = [pl.BlockSpec((seq, cb), lambda c, b: (b, up_col0 // cb + c))] if has_up else []
    w_spec = pl.BlockSpec((width, cb), lambda c, b: (0, c))
    b_spec = pl.BlockSpec((1, cb), lambda c, b: (0, c))
    o_spec = pl.BlockSpec((seq, cb), lambda c, b: (b, c))
    taps = [CONV_PAD - (width - 1) + j for j in range(width)]

    def window(x_ref, i):
        if isinstance(i, int) and i == 0:
            return jnp.concatenate([jnp.zeros((CONV_PAD, cb), F32), x_ref[0:rt, :]], axis=0)
        return x_ref[pl.ds(pl.multiple_of(i * rt - CONV_PAD, CONV_PAD), rt + CONV_PAD), :]

    def rows(i):
        return pl.ds(i * rt, rt) if isinstance(i, int) else pl.ds(pl.multiple_of(i * rt, rt), rt)

    def shifted(win):
        return [win[tp:tp + rt, :] for tp in taps]

    def pre_activation(views, w, b):
        y = b + w[0:1, :] * views[0]
        for j in range(1, width):
            y = y + w[j:j + 1, :] * views[j]
        return y

    def over_tiles(step, carry):
        carry = step(0, carry)
        return lax.fori_loop(1, n_tiles, step, carry)

    def fwd_call(x, w, b):
        def body(*refs):
            x_ref, w_ref, b_ref = refs[:3]
            o_ref = refs[-1]
            w_, b_ = w_ref[...], b_ref[...]

            def step(i, carry):
                y = _silu(pre_activation(shifted(window(x_ref, i)), w_, b_))
                if has_up:
                    y = y * refs[3][rows(i), :]
                o_ref[rows(i), :] = y
                return carry

            over_tiles(step, 0)

        return pl.pallas_call(
            body, name=name + "_fwd", grid=grid,
            in_specs=[x_spec, w_spec, b_spec] + up_specs, out_specs=o_spec,
            out_shape=jax.ShapeDtypeStruct((t, ch), F32),
            compiler_params=_cparams(),
        )(*([x, w, b] + ([x] if has_up else [])))

    def bwd_call(x, w, b, do):
        n_in = 4 + (1 if has_up else 0)

        def body(*refs):
            x_ref, w_ref, b_ref = refs[:3]
            do_ref = refs[n_in - 1]
            dx_ref, dw_ref, db_ref = refs[n_in:n_in + 3]
            gpad_ref = refs[-1]
            w_, b_ = w_ref[...], b_ref[...]

            def fold(a):
                acc = a[0:8, :]
                for k in range(1, rt // 8):
                    acc = acc + a[8 * k:8 * k + 8, :]
                return acc

            def grad_pre(i, sums):
                views = shifted(window(x_ref, i))
                y = pre_activation(views, w_, b_)
                s = jax.nn.sigmoid(y)
                act = y * s
                do_ = do_ref[rows(i), :]
                if has_up:
                    refs[n_in + 3][rows(i), :] = do_ * act
                    do_ = do_ * refs[3][rows(i), :]
                dy = do_ * (s + act * (1.0 - s))
                gpad_ref[rows(i), :] = dy
                new = [sums[j] + fold(dy * views[j]) for j in range(width)]
                return tuple(new) + (sums[width] + fold(dy),)

            zero8 = jnp.zeros((8, cb), F32)
            sums = over_tiles(grad_pre, (zero8,) * (width + 1))
            gpad_ref[seq:seq + CONV_PAD, :] = jnp.zeros((CONV_PAD, cb), F32)

            def grad_x(i, carry):
                if isinstance(i, int):
                    gwin = gpad_ref[0:rt + CONV_PAD, :]
                else:
                    gwin = gpad_ref[pl.ds(pl.multiple_of(i * rt, rt), rt + CONV_PAD), :]
                dx = w_[0:1, :] * gwin[width - 1:width - 1 + rt, :]
                for j in range(1, width):
                    dx = dx + w_[j:j + 1, :] * gwin[width - 1 - j:width - 1 - j + rt, :]
                dx_ref[rows(i), :] = dx
                return carry

            over_tiles(grad_x, 0)

            @pl.when(pl.program_id(1) == 0)
            def _():
                dw_ref[...] = jnp.zeros_like(dw_ref)
                db_ref[...] = jnp.zeros_like(db_ref)

            dw_ref[...] += jnp.concatenate([jnp.sum(sums[j], axis=0, keepdims=True) for j in range(width)], axis=0)
            db_ref[...] += jnp.sum(sums[width], axis=0, keepdims=True)

        big = jax.ShapeDtypeStruct((t, ch), F32)
        return pl.pallas_call(
            body, name=name + "_bwd", grid=grid,
            in_specs=[x_spec, w_spec, b_spec] + up_specs + [o_spec],
            out_specs=[o_spec, w_spec, b_spec] + ([o_spec] if has_up else []),
            out_shape=[big, jax.ShapeDtypeStruct((width, ch), F32), jax.ShapeDtypeStruct((1, ch), F32)]
            + ([big] if has_up else []),
            scratch_shapes=[pltpu.VMEM((seq + CONV_PAD, cb), F32)],
            compiler_params=_cparams(),
        )(*([x, w, b] + ([x] if has_up else []) + [do]))

    @jax.custom_vjp
    def conv(x, w, b):
        return fwd_call(x, w, b)

    def conv_fwd(x, w, b):
        return fwd_call(x, w, b), (x, w, b)

    def conv_bwd(res, do):
        x, w, b = res
        got = bwd_call(x, w, b, do)
        dx = jnp.pad(got[0], ((0, 0), (x_col0, x.shape[1] - x_col0 - ch)))
        if has_up:
            dx = dx + jnp.pad(got[3], ((0, 0), (up_col0, x.shape[1] - up_col0 - ch)))
        return dx, got[1], got[2]

    conv.defvjp(conv_fwd, conv_bwd)

    def apply(x, w, b=None):
        if b is None:
            b = jnp.zeros((ch,), F32)
        return conv(x, w, b.reshape(1, ch))

    return apply


def loss_head(x, w, target):
    t = x.shape[0]
    tm = _tile(t, 512)

    def fn(xb, wb, tb):
        err = _rms(xb, wb) - tb
        return 0.5 * jnp.sum(err * err) * (1.0 / D_MODEL)

    def body(x_ref, w_ref, t_ref, loss_ref, dx_ref, dw_ref):
        @pl.when(pl.program_id(0) == 0)
        def _():
            loss_ref[...] = jnp.zeros_like(loss_ref)
            dw_ref[...] = jnp.zeros_like(dw_ref)

        tb = t_ref[...]
        val, vjp = jax.vjp(lambda a, b: fn(a, b, tb), x_ref[...], w_ref[...])
        dx, dw = vjp(jnp.ones((), F32))
        dx_ref[...] = dx
        dw_ref[...] += dw
        loss_ref[...] += jnp.full(loss_ref.shape, val, F32)

    row = pl.BlockSpec((tm, D_MODEL), lambda i: (i, 0))
    vec = pl.BlockSpec((1, D_MODEL), lambda i: (0, 0))
    loss, dx, dw = pl.pallas_call(
        body, name="loss_head", grid=(t // tm,),
        in_specs=[row, vec, row],
        out_specs=[pl.BlockSpec((8, LANE), lambda i: (0, 0)), row, vec],
        out_shape=[jax.ShapeDtypeStruct((8, LANE), F32), jax.ShapeDtypeStruct((t, D_MODEL), F32),
                   jax.ShapeDtypeStruct((1, D_MODEL), F32)],
        compiler_params=_cparams(),
    )(x, w.reshape(1, D_MODEL), target)
    return loss[0, 0], dx, dw.reshape(D_MODEL)


PACK_W = 1024
ADAM_BLOCK_BYTES = 512 * 1024


def _rows_tile(r, c):
    if r * c * 4 <= ADAM_BLOCK_BYTES or r % 8:
        return r
    best = 8
    for t in range(8, r + 1, 8):
        if r % t == 0 and t * c * 4 <= ADAM_BLOCK_BYTES:
            best = t
    return best


def reduce_adamw(slots, w, m, v, name):
    r, wd = w.shape
    tr = _rows_tile(r, wd)
    c1 = 1.0 - ADAM_B1 ** ADAM_STEP
    c2 = 1.0 - ADAM_B2 ** ADAM_STEP

    def body(s_ref, w_ref, m_ref, v_ref, g_ref, d_ref, nm_ref, nv_ref):
        g = s_ref[0].astype(F32)
        for k in range(1, N_DEV):
            g = g + s_ref[k].astype(F32)
        nm = ADAM_B1 * m_ref[...] + (1.0 - ADAM_B1) * g
        nv = ADAM_B2 * v_ref[...] + (1.0 - ADAM_B2) * (g * g)
        m_hat = nm / c1
        v_hat = nv / c2
        d_ref[...] = -ADAM_LR * (m_hat / (jnp.sqrt(v_hat) + ADAM_EPS) + ADAM_WD * w_ref[...])
        g_ref[...] = g
        nm_ref[...] = nm
        nv_ref[...] = nv

    blk = pl.BlockSpec((tr, wd), lambda i: (i, 0))
    shp = jax.ShapeDtypeStruct((r, wd), F32)
    return pl.pallas_call(
        body, name=name, grid=(r // tr,),
        in_specs=[pl.BlockSpec((N_DEV, tr, wd), lambda i: (0, i, 0)), blk, blk, blk],
        out_specs=[blk, blk, blk, blk], out_shape=[shp, shp, shp, shp],
        compiler_params=_cparams(),
    )(slots, w, m, v)


def all_gather(block, name):
    def body(x_ref, out_ref, send_sems, recv_sems, local_sem):
        x, y, c = _position()
        me, sibling = (x, y, c), (x, y, 1 - c)
        chips = [(1 - x, y), (x, 1 - y), (1 - x, 1 - y)]

        def slot(px, py, pc):
            return out_ref.at[4 * px + 2 * py + pc]

        def copy(k, owner, to, src=None):
            return pltpu.make_async_remote_copy(
                src_ref=slot(*owner) if src is None else src, dst_ref=slot(*owner),
                send_sem=send_sems.at[k], recv_sem=recv_sems.at[k],
                device_id=to, device_id_type=pl.DeviceIdType.MESH)

        mine = pltpu.make_async_copy(x_ref, slot(*me), local_sem)
        mine.start()
        first = [copy(0, me, sibling, src=x_ref)]
        first += [copy(1 + j, me, (*chip, c), src=x_ref) for j, chip in enumerate(chips)]
        for cp in first:
            cp.start()
        passed = [copy(4 + j, (*chip, c), sibling) for j, chip in enumerate(chips)]
        for j, chip in enumerate(chips):
            copy(1 + j, (*chip, c), me).wait_recv()
            passed[j].start()
        copy(0, sibling, me).wait_recv()
        for j, chip in enumerate(chips):
            copy(4 + j, (*chip, 1 - c), me).wait_recv()
        for cp in first + passed:
            cp.wait_send()
        mine.wait()

    return pl.pallas_call(
        body, name=name,
        out_shape=jax.ShapeDtypeStruct((N_DEV,) + block.shape, block.dtype),
        in_specs=[pl.BlockSpec(memory_space=pl.ANY)],
        out_specs=pl.BlockSpec(memory_space=pl.ANY),
        scratch_shapes=[pltpu.SemaphoreType.DMA((7,)), pltpu.SemaphoreType.DMA((7,)), pltpu.SemaphoreType.DMA],
    )(block)


def exchange_slabs(slabs, name):
    def body(in_ref, out_ref, send_sems, recv_sems, local_sem):
        x, y, c = _position()
        my = 4 * x + 2 * y + c
        mine = pltpu.make_async_copy(in_ref.at[my], out_ref.at[my], local_sem)
        mine.start()
        copies = []
        for k in range(1, N_DEV):
            dx, dy, dc = (k >> 2) & 1, (k >> 1) & 1, k & 1
            px = x if dx == 0 else 1 - x
            py = y if dy == 0 else 1 - y
            pc = c if dc == 0 else 1 - c
            cp = pltpu.make_async_remote_copy(
                src_ref=in_ref.at[4 * px + 2 * py + pc], dst_ref=out_ref.at[my],
                send_sem=send_sems.at[k - 1], recv_sem=recv_sems.at[k - 1],
                device_id=(px, py, pc), device_id_type=pl.DeviceIdType.MESH)
            cp.start()
            copies.append(cp)
        for cp in copies:
            cp.wait()
        mine.wait()

    return pl.pallas_call(
        body, name=name,
        out_shape=jax.ShapeDtypeStruct(slabs.shape, slabs.dtype),
        in_specs=[pl.BlockSpec(memory_space=pl.ANY)],
        out_specs=pl.BlockSpec(memory_space=pl.ANY),
        scratch_shapes=[pltpu.SemaphoreType.DMA((7,)), pltpu.SemaphoreType.DMA((7,)), pltpu.SemaphoreType.DMA],
    )(slabs)


def _pack(arrays, dtype, row_multiple):
    flat = jnp.concatenate([a.astype(dtype).reshape(-1) for a in arrays])
    n = flat.shape[0]
    per = PACK_W * row_multiple
    total = -(-n // per) * per
    flat = jnp.pad(flat, (0, total - n))
    return flat.reshape(total // PACK_W, PACK_W)


def _unpack(flat2d, shapes, lead=()):
    flat = flat2d.reshape(lead + (-1,))
    out, off = [], 0
    for shp in shapes:
        n = math.prod(shp)
        out.append(flat[..., off:off + n].reshape(lead + tuple(shp)))
        off += n
    return out


def _full_from_gathered(g, axis):
    g = jnp.moveaxis(g, 0, axis)
    shp = list(g.shape)
    shp[axis:axis + 2] = [shp[axis] * shp[axis + 1]]
    return g.reshape(shp)


def _shards_of_full(full, axis):
    shp = list(full.shape)
    shp[axis:axis + 1] = [N_DEV, shp[axis] // N_DEV]
    return jnp.moveaxis(full.reshape(shp), axis, 0)


def layer_units(i):
    mixer = [('m_in_w', 'm_out_w'), ('h_in_w', 'h_out_w'), ('g_in_w', 'g_out_w')][i % 3]
    return [(mixer[0], i // 3), (mixer[1], i // 3), ('xa_q', i), ('xa_kv', i), ('xa_o', i), ('f_up', i), ('f_down', i)]


PADDED_COLS = {'m_in_w': M_IN_PAD, 'g_in_w': G_IN_PAD}
BIG_WEIGHTS = ('m_in_w', 'h_in_w', 'g_in_w', 'f_up')


def whole_weight(name, gathered):
    w = _full_from_gathered(lax.stop_gradient(gathered), SHARD_AXIS[name] - 1)
    return _pad_cols(w, 1, PADDED_COLS[name]) if name in PADDED_COLS else w


def _trunk(p, weights, blocks, standins, x, mem, bl, seq, pending=None):
    t = bl * seq
    ia = ib = ic = 0
    weights = dict(weights)
    state = {}

    def lin(name, a, wname, idx, residual=None):
        unit = (wname, idx)
        pos = state['units'].index(unit)
        later = state['next'][pos] if state['next'] else None
        nxt = (blocks[later],) if later in blocks and later not in state['by_core'] else ()
        n_real = N_DEV * standins[unit].shape[2]
        res = () if residual is None else (residual,)
        mixer_in = wname in ('m_in_w', 'h_in_w', 'g_in_w')
        defer = (pending, unit) if pending is not None and not (mixer_in and i == 0) else None
        y, got = make_linear(name, SHARD_AXIS[wname] - 1, n_real, bool(res), defer)(
            a, weights[unit], standins[unit], nxt, res)
        if nxt:
            weights[later] = whole_weight(later[0], got[0])
        return y

    by_seq = lambda a: a.reshape(bl, seq, a.shape[-1])

    for i in range(DEPTH):
        state['units'] = layer_units(i)
        state['next'] = layer_units(i + 1) if i + 1 < DEPTH else None
        state['by_core'] = [u for u in (state['next'] or [])
                            if u in blocks and (u[0] in BIG_WEIGHTS or i % 3 == 2)]
        core_blocks = [blocks[u] for u in state['by_core']]

        def core(op, *args):
            y, *got = op(*args, *core_blocks)
            for u, g in zip(state['by_core'], got):
                weights[u] = whole_weight(u[0], g)
            return y
        hn, x = rmsnorm_op(f"ln_mix{i}", t, F32, residual=True)(x, p['ln_mix'][i:i + 1])
        kind = i % 3
        if kind == 0:
            proj = lin(f"m_in{i}", hn, 'm_in_w', ia)
            xbc = make_conv(f"m_conv{i}", bl, seq, 4, M_CONV_DIM, M_D_INNER)(
                proj, p['m_conv_w'][ia], p['m_conv_b'][ia])
            dt = proj[:, M_D_INNER + M_CONV_DIM:M_IN].reshape(bl, seq, M_GROUPS, 4).transpose(2, 0, 1, 3)
            grp = lambda a, n=4: a.reshape(M_GROUPS, 1, n)
            proj3, xbc3 = by_seq(proj), by_seq(xbc)
            y = core(ssd_op(f"ssd{i}", bl, seq, pending, len(core_blocks)),
                     proj3, xbc3, xbc3, xbc3, dt, grp(p['m_dt_bias'][ia]), grp(p['m_a_log'][ia]), grp(p['m_d'][ia]),
                     grp(p['m_norm_w'][ia], 256))
            x = lin(f"m_out{i}", y.reshape(t, M_D_INNER), 'm_out_w', ia, residual=x)
            ia += 1
        elif kind == 1:
            proj3 = by_seq(lin(f"h_in{i}", hn, 'h_in_w', ib))
            y = core(gla_op(f"gla{i}", i, bl, seq, pending, len(core_blocks)),
                     proj3, proj3, proj3, proj3, p['h_lower_bounds'], p['h_norm_w'][ib:ib + 1])
            x = lin(f"h_out{i}", y.reshape(t, D_MODEL), 'h_out_w', ib, residual=x)
            ib += 1
        else:
            proj = lin(f"g_in{i}", hn, 'g_in_w', ic)
            qkv = make_conv(f"g_conv{i}", bl, seq, 4, G_CONV_DIM, 0)(proj, p['g_conv_w'][ic])
            c0 = G_CONV_DIM + G_VAL_DIM
            heads = lambda a: a.reshape(bl, seq, G_QK_HEADS, 2).transpose(2, 0, 1, 3)
            braw = heads(proj[:, c0:c0 + G_V_HEADS])
            araw = heads(proj[:, c0 + G_V_HEADS:c0 + 2 * G_V_HEADS])
            grp = lambda a: a.reshape(G_QK_HEADS, 1, 2)
            qkv3 = by_seq(qkv)
            y = core(gdn_op(f"gdn{i}", bl, seq, pending, len(core_blocks)),
                     qkv3, qkv3, qkv3, by_seq(proj), braw, araw, grp(p['g_a_log'][ic]), grp(p['g_dt_bias'][ic]),
                     p['g_norm_w'][ic:ic + 1])
            x = lin(f"g_out{i}", y.reshape(t, G_VAL_DIM), 'g_out_w', ic, residual=x)
            ic += 1
        hq, x = rmsnorm_op(f"ln_xattn{i}", t, F32, residual=True)(x, p['ln_xattn'][i:i + 1])
        mn = rmsnorm_op(f"ln_mem{i}", bl * N_MEM, F32)(mem, p['ln_mem'][i:i + 1])[0]
        qx = lin(f"xa_q{i}", hq, 'xa_q', i)
        kv = lin(f"xa_kv{i}", mn, 'xa_kv', i)
        ao = xattn_op(f"xattn{i}", bl, seq)(qx, kv, kv)[0]
        x = lin(f"xa_o{i}", ao, 'xa_o', i, residual=x)
        hf, x = rmsnorm_op(f"ln_ffn{i}", t, F32, residual=True)(x, p['ln_ffn'][i:i + 1])
        up = lin(f"f_up{i}", hf, 'f_up', i)
        act = make_conv(f"f_conv{i}", bl, seq, 3, D_FF, 0, up_col0=D_FF)(up, p['f_conv_w'][i], p['f_conv_b'][i])
        x = lin(f"f_down{i}", act, 'f_down', i, residual=x)
    return x


def _pad_cols(w, axis, to):
    pad = [(0, 0)] * w.ndim
    pad[axis] = (0, to - w.shape[axis])
    return jnp.pad(w, pad)


def kernel(x, mem, ln_mix, ln_xattn, ln_mem, ln_ffn, final_norm, m_in_w, m_conv_w, m_conv_b, m_dt_bias, m_a_log, m_d, m_norm_w, m_out_w, h_in_w, h_lower_bounds, h_norm_w, h_out_w, g_in_w, g_conv_w, g_a_log, g_dt_bias, g_norm_w, g_out_w, xa_q, xa_kv, xa_o, f_up, f_conv_w, f_conv_b, f_down, loss_target, m_ln_mix, m_ln_xattn, m_ln_mem, m_ln_ffn, m_final_norm, m_m_in_w, m_m_conv_w, m_m_conv_b, m_m_dt_bias, m_m_a_log, m_m_d, m_m_norm_w, m_m_out_w, m_h_in_w, m_h_lower_bounds, m_h_norm_w, m_h_out_w, m_g_in_w, m_g_conv_w, m_g_a_log, m_g_dt_bias, m_g_norm_w, m_g_out_w, m_xa_q, m_xa_kv, m_xa_o, m_f_up, m_f_conv_w, m_f_conv_b, m_f_down, v_ln_mix, v_ln_xattn, v_ln_mem, v_ln_ffn, v_final_norm, v_m_in_w, v_m_conv_w, v_m_conv_b, v_m_dt_bias, v_m_a_log, v_m_d, v_m_norm_w, v_m_out_w, v_h_in_w, v_h_lower_bounds, v_h_norm_w, v_h_out_w, v_g_in_w, v_g_conv_w, v_g_a_log, v_g_dt_bias, v_g_norm_w, v_g_out_w, v_xa_q, v_xa_kv, v_xa_o, v_f_up, v_f_conv_w, v_f_conv_b, v_f_down):
    local = dict(ln_mix=ln_mix, ln_xattn=ln_xattn, ln_mem=ln_mem, ln_ffn=ln_ffn, final_norm=final_norm, m_in_w=m_in_w, m_conv_w=m_conv_w, m_conv_b=m_conv_b, m_dt_bias=m_dt_bias, m_a_log=m_a_log, m_d=m_d, m_norm_w=m_norm_w, m_out_w=m_out_w, h_in_w=h_in_w, h_lower_bounds=h_lower_bounds, h_norm_w=h_norm_w, h_out_w=h_out_w, g_in_w=g_in_w, g_conv_w=g_conv_w, g_a_log=g_a_log, g_dt_bias=g_dt_bias, g_norm_w=g_norm_w, g_out_w=g_out_w, xa_q=xa_q, xa_kv=xa_kv, xa_o=xa_o, f_up=f_up, f_conv_w=f_conv_w, f_conv_b=f_conv_b, f_down=f_down)
    mom_m = dict(ln_mix=m_ln_mix, ln_xattn=m_ln_xattn, ln_mem=m_ln_mem, ln_ffn=m_ln_ffn, final_norm=m_final_norm, m_in_w=m_m_in_w, m_conv_w=m_m_conv_w, m_conv_b=m_m_conv_b, m_dt_bias=m_m_dt_bias, m_a_log=m_m_a_log, m_d=m_m_d, m_norm_w=m_m_norm_w, m_out_w=m_m_out_w, h_in_w=m_h_in_w, h_lower_bounds=m_h_lower_bounds, h_norm_w=m_h_norm_w, h_out_w=m_h_out_w, g_in_w=m_g_in_w, g_conv_w=m_g_conv_w, g_a_log=m_g_a_log, g_dt_bias=m_g_dt_bias, g_norm_w=m_g_norm_w, g_out_w=m_g_out_w, xa_q=m_xa_q, xa_kv=m_xa_kv, xa_o=m_xa_o, f_up=m_f_up, f_conv_w=m_f_conv_w, f_conv_b=m_f_conv_b, f_down=m_f_down)
    mom_v = dict(ln_mix=v_ln_mix, ln_xattn=v_ln_xattn, ln_mem=v_ln_mem, ln_ffn=v_ln_ffn, final_norm=v_final_norm, m_in_w=v_m_in_w, m_conv_w=v_m_conv_w, m_conv_b=v_m_conv_b, m_dt_bias=v_m_dt_bias, m_a_log=v_m_a_log, m_d=v_m_d, m_norm_w=v_m_norm_w, m_out_w=v_m_out_w, h_in_w=v_h_in_w, h_lower_bounds=v_h_lower_bounds, h_norm_w=v_h_norm_w, h_out_w=v_h_out_w, g_in_w=v_g_in_w, g_conv_w=v_g_conv_w, g_a_log=v_g_a_log, g_dt_bias=v_g_dt_bias, g_norm_w=v_g_norm_w, g_out_w=v_g_out_w, xa_q=v_xa_q, xa_kv=v_xa_kv, xa_o=v_xa_o, f_up=v_f_up, f_conv_w=v_f_conv_w, f_conv_b=v_f_conv_b, f_down=v_f_down)

    bl, seq, _ = x.shape
    t = bl * seq

    p = {n: local[n] for n in WEIGHTS if n not in SHARD_AXIS}
    for n in SMALL_SHARDED:
        p[n] = _full_from_gathered(all_gather(local[n], f"gather_{n}"), SHARD_AXIS[n])
    units = [(n, l) for n in MATMUL_WEIGHTS for l in range(local[n].shape[0])]
    block = lambda u: local[u[0]][u[1]].astype(BF16)
    weights = {u: whole_weight(u[0], all_gather(block(u), f"gather_{u[0]}{u[1]}")) for u in layer_units(0)}
    blocks = {u: block(u) for u in units if u not in weights}
    standins = {u: jnp.zeros((N_DEV,) + local[u[0]].shape[1:], BF16) for u in units}
    small = {n: p[n] for n in WEIGHTS if n not in MATMUL_WEIGHTS and n != 'final_norm'}

    pending = Pending()

    def run(small_w, standins_, xin):
        return _trunk(small_w, weights, blocks, standins_, xin, mem.reshape(bl * N_MEM, D_MODEL), bl, seq, pending)

    x_out, vjp = jax.vjp(run, small, standins, x.reshape(t, D_MODEL))
    loss_part, dx_out, d_final = loss_head(x_out, final_norm, loss_target.reshape(t, D_MODEL))
    grads, received, dx = vjp(dx_out)
    received = dict(received)
    for unit, slabs in pending.take_all():
        pending.received[unit] = exchange_slabs(slabs, f"exchange_{unit[0]}{unit[1]}")
    received.update(pending.received)
    grads = dict(grads)
    grads['final_norm'] = d_final
    loss = lax.psum(loss_part, ("x", "y", "c"))

    outs = {}

    def update(name, n, slots, shape, sel=lambda a: a):
        two_d = lambda a: sel(a).reshape(slots.shape[1:])
        got = reduce_adamw(slots, two_d(local[n]), two_d(mom_m[n]), two_d(mom_v[n]), name)
        return [g.reshape(shape) for g in got]

    for n in SMALL_SHARDED:
        slots = exchange_slabs(_shards_of_full(grads[n], SHARD_AXIS[n]), f"exchange_{n}")
        slots = slots.reshape(N_DEV, -1, slots.shape[-1])
        for kind, a in zip(KINDS, update(f"adamw_{n}", n, slots, local[n].shape)):
            outs[kind, n] = a
    for n in MATMUL_WEIGHTS:
        per_layer = [update(f"adamw_{n}{l}", n, received[n, l], local[n].shape[1:], lambda a, l=l: a[l])
                     for l in range(local[n].shape[0])]
        for k, kind in enumerate(KINDS):
            outs[kind, n] = jnp.concatenate([got[k][None] for got in per_layer])
    replicated = [n for n in WEIGHTS if n not in SHARD_AXIS]
    pk = lambda d: _pack([d[n] for n in replicated], F32, 8)
    got = reduce_adamw(all_gather(pk(grads), "gather_replicated_grads"), pk(local), pk(mom_m), pk(mom_v),
                       "adamw_replicated")
    shapes = [local[n].shape for n in replicated]
    for kind, buf in zip(KINDS, got):
        for n, a in zip(replicated, _unpack(buf, shapes)):
            outs[kind, n] = a
    result = [loss, dx.reshape(bl, seq, D_MODEL)]
    for kind in KINDS:
        result += [outs[kind, n] for n in WEIGHTS]
    return tuple(result)
```

```python
import functools
import math

import jax
import jax.numpy as jnp
from jax import lax
from jax.experimental import pallas as pl
from jax.experimental.pallas import tpu as pltpu

F32 = jnp.float32
BF16 = jnp.bfloat16
NN = (((1,), (0,)), ((), ()))
NT = (((1,), (1,)), ((), ()))
TN = (((0,), (0,)), ((), ()))

D_MODEL = 1024
DEPTH = 4
EPS = 1e-6
N_MEM = 256
M_D_INNER = 2048
M_HEADS = 32
M_GROUPS = 8
M_STATE = 128
M_CONV_DIM = 4096
M_IN = 6176
M_IN_PAD = 6272
SSD_CHUNK = 256
H_HEADS = 8
HGRN_CHUNK = 32
HGRN_ROWS = 256
G_QK_HEADS = 8
G_V_HEADS = 16
G_KEY_DIM = 1024
G_VAL_DIM = 2048
G_CONV_DIM = 4096
G_IN = 6176
G_IN_PAD = 6272
GDN_CHUNK = 64
X_HEADS = 4
X_HEAD_DIM = 256
D_FF = 2816
ADAM_LR = 0.001
ADAM_B1 = 0.9
ADAM_B2 = 0.999
ADAM_EPS = 1e-08
ADAM_WD = 0.01
ADAM_STEP = 10

N_DEV = 8
LANE = 128
KINDS = ('grad', 'delta', 'new_m', 'new_v')
VMEM_LIMIT = 56 * 1024 * 1024

WEIGHTS = ['ln_mix', 'ln_xattn', 'ln_mem', 'ln_ffn', 'final_norm', 'm_in_w', 'm_conv_w', 'm_conv_b', 'm_dt_bias',
           'm_a_log', 'm_d', 'm_norm_w', 'm_out_w', 'h_in_w', 'h_lower_bounds', 'h_norm_w', 'h_out_w', 'g_in_w',
           'g_conv_w', 'g_a_log', 'g_dt_bias', 'g_norm_w', 'g_out_w', 'xa_q', 'xa_kv', 'xa_o', 'f_up', 'f_conv_w',
           'f_conv_b', 'f_down']
SHARD_AXIS = {'m_in_w': 2, 'm_conv_w': 2, 'm_conv_b': 1, 'm_norm_w': 1, 'm_out_w': 1, 'h_in_w': 2, 'h_out_w': 1,
              'g_in_w': 2, 'g_conv_w': 2, 'g_out_w': 1, 'xa_q': 1, 'xa_kv': 2, 'xa_o': 1, 'f_up': 2, 'f_conv_w': 2,
              'f_down': 1}
MATMUL_WEIGHTS = ['m_in_w', 'm_out_w', 'h_in_w', 'h_out_w', 'g_in_w', 'g_out_w', 'xa_q', 'xa_kv', 'xa_o', 'f_up',
                  'f_down']
SMALL_SHARDED = ['m_conv_w', 'm_conv_b', 'm_norm_w', 'g_conv_w', 'f_conv_w']


def _cparams():
    return pltpu.CompilerParams(vmem_limit_bytes=VMEM_LIMIT)


def bdot(a, b, dims=NN):
    return lax.dot_general(a.astype(BF16), b.astype(BF16), dims, preferred_element_type=F32)


def _split(a):
    hi = a.astype(BF16)
    return hi, (a - hi.astype(F32)).astype(BF16)


def _h3(a, b, dims):
    ah, al = _split(a)
    bh, bl = _split(b)
    d = functools.partial(lax.dot_general, dimension_numbers=dims, preferred_element_type=F32)
    return d(ah, bh) + (d(ah, bl) + d(al, bh))


BNN = (((2,), (1,)), ((0,), (0,)))
BNT = (((2,), (2,)), ((0,), (0,)))
BTN = (((1,), (1,)), ((0,), (0,)))


@jax.custom_vjp
def h3dot_b(a, b):
    return _h3(a, b, BNN)


h3dot_b.defvjp(lambda a, b: (_h3(a, b, BNN), (a, b)),
               lambda res, ct: (_h3(ct, res[1], BNT), _h3(res[0], ct, BTN)))

T_ROWS = (((0,), (1,)), ((), ()))


def _tri_times(tri, x, dims, tri_first):
    t = tri.astype(BF16)
    x0 = x.astype(BF16)
    r1 = x - x0.astype(F32)
    x1 = r1.astype(BF16)
    x2 = (r1 - x1.astype(F32)).astype(BF16)
    if tri_first:
        d = lambda xx: lax.dot_general(t, xx, dims, preferred_element_type=F32)
    else:
        d = lambda xx: lax.dot_general(xx, t, dims, preferred_element_type=F32)
    return d(x0) + (d(x1) + d(x2))


@jax.custom_vjp
def cumdot(tri, x):
    return _tri_times(tri, x, NN, True)


cumdot.defvjp(lambda tri, x: (_tri_times(tri, x, NN, True), tri),
              lambda tri, ct: (jnp.zeros_like(tri), _tri_times(tri, ct, TN, True)))


@jax.custom_vjp
def cumdot_t(tri, x):
    return _tri_times(tri, x, T_ROWS, False)


cumdot_t.defvjp(lambda tri, x: (_tri_times(tri, x, T_ROWS, False), tri),
                lambda tri, ct: (jnp.zeros_like(tri), _tri_times(tri, ct, T_ROWS, True)))


def _tile(dim, cap):
    if dim <= cap:
        return dim
    best = None
    for t in range(LANE, cap + 1, LANE):
        if dim % t == 0:
            best = t
    assert best is not None, dim
    return best


def _position():
    return lax.axis_index("x"), lax.axis_index("y"), lax.axis_index("c")


def _direct_copies(kind, src_ref, dst_ref, send_sems, recv_sems, local_sem):
    x, y, c = _position()
    me = 4 * x + 2 * y + c
    local_src = src_ref if kind == 'gather' else src_ref.at[me]
    copies = [pltpu.make_async_copy(local_src, dst_ref.at[me], local_sem)]
    for k in range(1, N_DEV):
        px = 1 - x if (k >> 2) & 1 else x
        py = 1 - y if (k >> 1) & 1 else y
        pc = 1 - c if k & 1 else c
        copies.append(pltpu.make_async_remote_copy(
            src_ref=src_ref if kind == 'gather' else src_ref.at[4 * px + 2 * py + pc], dst_ref=dst_ref.at[me],
            send_sem=send_sems.at[k - 1], recv_sem=recv_sems.at[k - 1],
            device_id=(px, py, pc), device_id_type=pl.DeviceIdType.MESH))
    return copies


COMM_SCRATCH = [pltpu.SemaphoreType.DMA((N_DEV - 1,)), pltpu.SemaphoreType.DMA((N_DEV - 1,)), pltpu.SemaphoreType.DMA]


class Pending:
    def __init__(self):
        self.jobs, self.received = [], {}

    def take_all(self):
        jobs, self.jobs = self.jobs, []
        return jobs


def matmul(a, b, *, ta=False, tb=False, out_dtype=F32, name="mm", carry=None, residual=None):
    if ta:
        k, m = a.shape
    else:
        m, k = a.shape
    if tb:
        n, k2 = b.shape
    else:
        k2, n = b.shape
    assert k == k2, (a.shape, b.shape, ta, tb)
    tm = _tile(m, 1408)
    tn = _tile(n, 1408)
    tk = _tile(k, 1408)
    grid = (m // tm, n // tn, k // tk)
    nk = grid[2]
    dims = (((0 if ta else 1,), (1 if tb else 0,)), ((), ()))

    def at_step(which):
        conds = [pl.program_id(ax) == (0 if which == 'first' else grid[ax] - 1) for ax in range(3)]
        return jnp.logical_and(jnp.logical_and(conds[0], conds[1]), conds[2])

    def body(*refs):
        r_ref = None
        if residual is not None:
            r_ref, refs = refs[2], refs[:2] + refs[3:]
        if carry is None:
            a_ref, b_ref, o_ref, acc_ref = refs
        else:
            a_ref, b_ref, src_ref, o_ref, dst_ref, acc_ref, send_sems, recv_sems, local_sem = refs
            copies = lambda: _direct_copies(carry[0], src_ref, dst_ref, send_sems, recv_sems, local_sem)

            @pl.when(at_step('first'))
            def _():
                for cp in copies():
                    cp.start()

        @pl.when(pl.program_id(2) == 0)
        def _():
            acc_ref[...] = jnp.zeros_like(acc_ref)

        acc_ref[...] += lax.dot_general(a_ref[...].astype(BF16), b_ref[...].astype(BF16), dims,
                                        preferred_element_type=F32)

        @pl.when(pl.program_id(2) == nk - 1)
        def _():
            out = acc_ref[...] if r_ref is None else acc_ref[...] + r_ref[...]
            o_ref[...] = out.astype(o_ref.dtype)

        if carry is not None:
            @pl.when(at_step('last'))
            def _():
                for cp in copies():
                    cp.wait()

    a_spec = pl.BlockSpec((tk, tm), lambda i, j, kk: (kk, i)) if ta else pl.BlockSpec((tm, tk), lambda i, j, kk: (i, kk))
    b_spec = pl.BlockSpec((tn, tk), lambda i, j, kk: (j, kk)) if tb else pl.BlockSpec((tk, tn), lambda i, j, kk: (kk, j))
    o_spec = pl.BlockSpec((tm, tn), lambda i, j, kk: (i, j))
    o_shape = jax.ShapeDtypeStruct((m, n), out_dtype)
    acc = pltpu.VMEM((tm, tn), F32)
    ins, in_specs = [a, b], [a_spec, b_spec]
    if residual is not None:
        ins.append(residual)
        in_specs.append(o_spec)
    if carry is None:
        return pl.pallas_call(
            body, name=name, grid=grid, in_specs=in_specs, out_specs=o_spec, out_shape=o_shape,
            scratch_shapes=[acc], compiler_params=_cparams(),
        )(*ins)
    kind, src = carry
    got = jax.ShapeDtypeStruct(((N_DEV,) + src.shape) if kind == 'gather' else src.shape, src.dtype)
    hbm = pl.BlockSpec(memory_space=pl.ANY)
    return pl.pallas_call(
        body, name=name, grid=grid, in_specs=in_specs + [hbm], out_specs=[o_spec, hbm],
        out_shape=[o_shape, got], scratch_shapes=[acc] + COMM_SCRATCH, compiler_params=_cparams(),
    )(*ins, src)


def make_linear(name, shard_axis, n_real, has_res=False, defer=None):
    def forward(a, w, nxt, res):
        r = res[0] if res else None
        if nxt:
            y, got = matmul(a, w, name=name + "_fwd", carry=('gather', nxt[0]), residual=r)
            return y, (got,)
        return matmul(a, w, name=name + "_fwd", residual=r), ()

    @jax.custom_vjp
    def linear(a, w, wg, nxt, res):
        return forward(a, w, nxt, res)

    def fwd(a, w, wg, nxt, res):
        return forward(a, w, nxt, res), (a, w, nxt)

    def bwd(saved, cts):
        a, w, nxt = saved
        dy = cts[0]
        dw = matmul(a, dy, ta=True, out_dtype=BF16, name=name + "_bwd_dw")
        slabs = _shards_of_full(dw[:, :n_real], shard_axis)
        if defer is None:
            da, slots = matmul(dy, w, tb=True, out_dtype=a.dtype, name=name + "_bwd_da", carry=('exchange', slabs))
        else:
            defer[0].jobs.append((defer[1], slabs))
            da, slots = matmul(dy, w, tb=True, out_dtype=a.dtype, name=name + "_bwd_da"), jnp.zeros_like(slabs)
        return da, jnp.zeros_like(w), slots, tuple(jnp.zeros_like(b) for b in nxt), ((dy,) if has_res else ())

    linear.defvjp(fwd, bwd)
    return linear


class In:
    def __init__(self, block, imap, kind='blk', inner=(), cols=None):
        self.block, self.imap, self.kind, self.inner, self.cols = block, imap, kind, inner, cols


class Out:
    def __init__(self, shape, dtype, block, imap):
        self.shape, self.dtype, self.block, self.imap = shape, dtype, block, imap


def make_op(name, fn, grid, ins, outs, state_shape=None, seq_axis=None, passthrough=(), pending=None, n_gather=0):
    n_in, n_out = len(ins), len(outs)
    has_state = state_shape is not None
    nd = len(grid)
    diff_idx = [i for i, s in enumerate(ins) if s.kind != 'const']

    def in_spec(s, reverse):
        off = 0
        if s.cols is not None:
            assert s.cols[0] % s.block[-1] == 0
            off = s.cols[0] // s.block[-1]

        def imap(*ids):
            ids = rev(ids) if reverse else ids
            idx = tuple(s.imap(*ids))
            return idx[:-1] + (idx[-1] + off,) if off else idx

        return pl.BlockSpec(s.block, imap)

    def rel_spec(block, f, reverse):
        return pl.BlockSpec(block, (lambda *ids: f(*rev(ids))) if reverse else f)

    def rev(ids):
        if not has_state:
            return ids
        ids = list(ids)
        ids[seq_axis] = grid[seq_axis] - 1 - ids[seq_axis]
        return tuple(ids)

    save_shape = tuple(grid) + tuple(state_shape) if has_state else None
    save_block = (None,) * nd + tuple(state_shape) if has_state else None

    def save_imap(*ids):
        return tuple(ids) + (0,) * len(state_shape)

    def step_is(which):
        conds = [pl.program_id(ax) == (0 if which == 'first' else grid[ax] - 1) for ax in range(nd)]
        return functools.reduce(jnp.logical_and, conds)

    def fwd_call(*xs):
        xs, blocks = xs[:n_in], xs[n_in:]
        n_save = 1 if has_state else 0

        def body(*refs):
            if blocks:
                src_refs = refs[n_in:n_in + n_gather]
                dst_refs = refs[n_in + n_gather + n_out + n_save:n_in + 2 * n_gather + n_out + n_save]
                sems = refs[len(refs) - 3 * n_gather:]
                refs = refs[:n_in] + refs[n_in + n_gather:n_in + n_gather + n_out + n_save] + \
                    refs[n_in + 2 * n_gather + n_out + n_save:len(refs) - 3 * n_gather]
                copies = lambda: [cp for k in range(n_gather) for cp in _direct_copies(
                    'gather', src_refs[k], dst_refs[k], sems[3 * k], sems[3 * k + 1], sems[3 * k + 2])]

                @pl.when(step_is('first'))
                def _():
                    for cp in copies():
                        cp.start()

            in_refs = refs[:n_in]
            out_refs = refs[n_in:n_in + n_out]
            vals = [r[...] for r in in_refs]
            if has_state:
                save_ref, st_ref = refs[n_in + n_out], refs[n_in + n_out + 1]

                @pl.when(pl.program_id(seq_axis) == 0)
                def _():
                    st_ref[...] = jnp.zeros(state_shape, F32)

                st = st_ref[...]
                save_ref[...] = st
                res = fn(*vals, st)
                st_ref[...] = res[-1]
                res = res[:-1]
            else:
                res = fn(*vals)
            for o, v in zip(out_refs, res):
                o[...] = v.astype(o.dtype)

            if blocks:
                @pl.when(step_is('last'))
                def _():
                    for cp in copies():
                        cp.wait()

        out_shape = [jax.ShapeDtypeStruct(o.shape, o.dtype) for o in outs]
        out_specs = [pl.BlockSpec(o.block, o.imap) for o in outs]
        scratch = []
        if has_state:
            out_shape.append(jax.ShapeDtypeStruct(save_shape, F32))
            out_specs.append(pl.BlockSpec(save_block, save_imap))
            scratch.append(pltpu.VMEM(state_shape, F32))
        hbm = pl.BlockSpec(memory_space=pl.ANY)
        for blk in blocks:
            out_shape.append(jax.ShapeDtypeStruct((N_DEV,) + blk.shape, blk.dtype))
            out_specs.append(hbm)
            scratch += COMM_SCRATCH
        return pl.pallas_call(
            body, name=name + "_fwd", grid=grid,
            in_specs=[in_spec(s, False) for s in ins] + [hbm] * len(blocks),
            out_specs=out_specs, out_shape=out_shape, scratch_shapes=scratch,
            compiler_params=_cparams(),
        )(*xs, *blocks)

    def grad_shape(s, x):
        if s.cols is not None:
            return x.shape[:-1] + (s.cols[1],)
        return x.shape

    def bwd_call(xs, save, cts, pass_cts=()):
        n_diff = len(diff_idx)
        jobs = pending.take_all() if pending is not None else []
        n_args = n_in + (1 if has_state else 0) + n_out + len(passthrough)

        def body(*refs):
            if jobs:
                src_refs = refs[n_args:n_args + len(jobs)]
                dst_refs = refs[n_args + len(jobs) + n_diff:n_args + 2 * len(jobs) + n_diff]
                sems = refs[len(refs) - 3 * len(jobs):]
                refs = refs[:n_args] + refs[n_args + len(jobs):n_args + len(jobs) + n_diff] + \
                    refs[n_args + 2 * len(jobs) + n_diff:len(refs) - 3 * len(jobs)]
                copies = lambda: [cp for k in range(len(jobs)) for cp in _direct_copies(
                    'exchange', src_refs[k], dst_refs[k], sems[3 * k], sems[3 * k + 1], sems[3 * k + 2])]

                @pl.when(step_is('first'))
                def _():
                    for cp in copies():
                        cp.start()

            in_refs = refs[:n_in]
            p = n_in
            if has_state:
                save_ref = refs[p]
                p += 1
            ct_refs = refs[p:p + n_out]
            p += n_out
            pass_refs = dict(zip(passthrough, refs[p:p + len(passthrough)]))
            p += len(passthrough)
            g_refs = refs[p:p + n_diff]
            p += n_diff
            vals = [r[...] for r in in_refs]

            def g(*dv):
                full = list(vals)
                for i, v in zip(diff_idx, dv):
                    full[i] = v
                if has_state:
                    return tuple(fn(*full, dv[-1]))
                return tuple(fn(*full))

            prim = [vals[i] for i in diff_idx]
            ct = tuple(r[...].astype(F32) for r in ct_refs)
            if has_state:
                dst_ref = refs[p]

                @pl.when(pl.program_id(seq_axis) == 0)
                def _():
                    dst_ref[...] = jnp.zeros(state_shape, F32)

                prim = prim + [save_ref[...]]
                ct = ct + (dst_ref[...],)
            _, vjp = jax.vjp(g, *prim)
            grads = vjp(ct)
            for k, i in enumerate(diff_idx):
                s = ins[i]
                if s.kind == 'blk':
                    g = grads[k] + pass_refs[i][...] if i in pass_refs else grads[k]
                    g_refs[k][...] = g.astype(g_refs[k].dtype)
                else:
                    first = None
                    for ax in s.inner:
                        c = pl.program_id(ax) == 0
                        first = c if first is None else jnp.logical_and(first, c)

                    @pl.when(first)
                    def _(k=k):
                        g_refs[k][...] = jnp.zeros_like(g_refs[k])

                    g_refs[k][...] += grads[k].astype(g_refs[k].dtype)
            if has_state:
                dst_ref[...] = grads[-1]

            if jobs:
                @pl.when(step_is('last'))
                def _():
                    for cp in copies():
                        cp.wait()

        in_specs = [in_spec(s, True) for s in ins]
        args = list(xs)
        if has_state:
            in_specs.append(rel_spec(save_block, save_imap, True))
            args.append(save)
        for o, c in zip(outs, cts):
            in_specs.append(rel_spec(o.block, o.imap, True))
            args.append(c)
        for i, c in zip(passthrough, pass_cts):
            assert ins[i].kind == 'blk' and ins[i].cols is None
            in_specs.append(rel_spec(ins[i].block, ins[i].imap, True))
            args.append(c)
        out_shape, out_specs = [], []
        for i in diff_idx:
            s = ins[i]
            out_shape.append(jax.ShapeDtypeStruct(grad_shape(s, xs[i]), xs[i].dtype))
            out_specs.append(rel_spec(s.block, s.imap, True))
        scratch = [pltpu.VMEM(state_shape, F32)] if has_state else []
        hbm = pl.BlockSpec(memory_space=pl.ANY)
        for _, slabs in jobs:
            in_specs.append(hbm)
            args.append(slabs)
            out_specs.append(hbm)
            out_shape.append(jax.ShapeDtypeStruct(slabs.shape, slabs.dtype))
            scratch += COMM_SCRATCH
        got = pl.pallas_call(
            body, name=name + "_bwd", grid=grid,
            in_specs=in_specs, out_specs=out_specs, out_shape=out_shape, scratch_shapes=scratch,
            compiler_params=_cparams(),
        )(*args)
        for (unit, _), slots in zip(jobs, got[n_diff:]):
            pending.received[unit] = slots
        return got[:n_diff]

    def results(xs, res):
        gathered = tuple(res[len(res) - n_gather:]) if n_gather else ()
        return tuple(res[:n_out]) + tuple(xs[i] for i in passthrough) + gathered

    @jax.custom_vjp
    def op(*xs):
        return results(xs, fwd_call(*xs))

    def op_fwd(*xs):
        res = fwd_call(*xs)
        return results(xs, res), (xs, res[n_out] if has_state else None)

    def op_bwd(resid, cts):
        xs, save = resid
        xs, blocks = xs[:n_in], xs[n_in:]
        grads = bwd_call(xs, save, cts[:n_out], cts[n_out:n_out + len(passthrough)])
        out = []
        k = 0
        for i, s in enumerate(ins):
            if s.kind == 'const':
                out.append(jnp.zeros_like(xs[i]))
                continue
            g = grads[k]
            k += 1
            if s.cols is not None:
                g = jnp.pad(g, ((0, 0),) * (g.ndim - 1) + ((s.cols[0], xs[i].shape[-1] - s.cols[0] - s.cols[1]),))
            out.append(g)
        return tuple(out) + tuple(jnp.zeros_like(b) for b in blocks)

    op.defvjp(op_fwd, op_bwd)
    return op


def _rms(x, w):
    return x * lax.rsqrt(jnp.mean(x * x, axis=-1, keepdims=True) + EPS) * w


def _silu(x):
    return x * jax.nn.sigmoid(x)


def rmsnorm_op(name, t, out_dtype, residual=False):
    tm = _tile(t, 512)
    return make_op(
        name, lambda x, w: (_rms(x, w),), (t // tm,),
        [In((tm, D_MODEL), lambda i: (i, 0)), In((1, D_MODEL), lambda i: (0, 0), 'acc', (0,))],
        [Out((t, D_MODEL), out_dtype, (tm, D_MODEL), lambda i: (i, 0))], passthrough=(0,) if residual else ())


def _tri(q):
    ii = lax.broadcasted_iota(jnp.int32, (q, q), 0)
    jj = lax.broadcasted_iota(jnp.int32, (q, q), 1)
    return ii >= jj, ii > jj


def _ssd_fn(z, x, bm, cm, dtr, dtb, alog, dsk, nw, state):
    q = x.shape[0]
    incl, _ = _tri(q)
    tril = incl.astype(F32)
    dt = jax.nn.softplus(dtr + dtb)
    da = dt * (-jnp.exp(alog))
    acum = cumdot(tril, da)
    acum_t = cumdot_t(tril, da)
    cb = bdot(cm, bm, NT)
    heads = range(4)
    wide = lambda a: jnp.concatenate([jnp.broadcast_to(a[:, r:r + 1], (a.shape[0], 64)) for r in heads], axis=1)
    last = acum[q - 1:q, :]
    xc = x * wide(dt)
    y = bdot(cm, state, NT) * wide(jnp.exp(acum)) + wide(dsk) * x
    ds = bdot(xc * wide(jnp.exp(last - acum)), bm, TN)
    e_last = jnp.exp(last)
    new_state = state * jnp.concatenate([jnp.broadcast_to(e_last[:, r:r + 1], (64, 1)) for r in heads], axis=0) + ds
    diag = []
    for r in heads:
        decay = jnp.exp(jnp.where(incl, acum[:, r:r + 1] - acum_t[r:r + 1, :], -jnp.inf))
        diag.append(bdot(cb * decay, xc[:, 64 * r:64 * r + 64]))
    y = y + jnp.concatenate(diag, axis=1)
    yz = y * _silu(z)
    return _rms(yz, nw), new_state


def _per_sequence(fn, n_seq_args, bl):
    def f(*args):
        *ins, state = args
        res = [fn(*[a[b] for a in ins[:n_seq_args]], *ins[n_seq_args:], state[b]) for b in range(bl)]
        return tuple(jnp.concatenate([r[k][None] for r in res]) for k in range(len(res[0])))

    return f


def ssd_op(name, bl, seq, pending=None, n_gather=0):
    q = SSD_CHUNK
    nc = seq // q
    blk = lambda w, c0, cw: In((bl, q, w), lambda g, n: (0, n, g), cols=(c0, cw))
    small = lambda g, n: (g, 0, 0)
    ins = [
        blk(256, 0, M_D_INNER),
        blk(256, 0, M_D_INNER),
        blk(128, M_D_INNER, 1024),
        blk(128, M_D_INNER + 1024, 1024),
        In((None, bl, q, 4), lambda g, n: (g, 0, n, 0)),
        In((None, 1, 4), small, 'acc', (1,)),
        In((None, 1, 4), small, 'acc', (1,)),
        In((None, 1, 4), small, 'acc', (1,)),
        In((None, 1, 256), small, 'acc', (1,)),
    ]
    outs = [Out((bl, seq, M_D_INNER), F32, (bl, q, 256), lambda g, n: (0, n, g))]
    return make_op(name, _per_sequence(_ssd_fn, 5, bl), (M_GROUPS, nc), ins, outs,
                   state_shape=(bl, 256, 128), seq_axis=1, pending=pending, n_gather=n_gather)


def _gla_fn(layer, qr, fr, ir, gr, lbp, nw, state_t):
    rows = qr.shape[0]
    c = HGRN_CHUNK
    n_chunks = rows // c
    e = jnp.exp(lbp - jnp.max(lbp, axis=0, keepdims=True))
    sm = e / jnp.sum(e, axis=0, keepdims=True)
    lb = jnp.sum(sm[1:layer + 1, :], axis=0, keepdims=True) if layer > 0 else jnp.zeros((1, lbp.shape[1]), F32)
    qq = _silu(qr) * (128 ** -0.5)
    forget = lb + (1.0 - lb) * jax.nn.sigmoid(fr)
    kk = 1.0 - forget
    logf = jnp.log(forget)
    ii = lax.broadcasted_iota(jnp.int32, (rows, rows), 0)
    jj = lax.broadcasted_iota(jnp.int32, (rows, rows), 1)
    own = jnp.logical_and(ii >= jj, ii // c == jj // c)
    gc = cumdot(own.astype(F32), logf)
    glasts = [gc[c * j + c - 1:c * j + c, :] for j in range(n_chunks)]
    glast_rows = jnp.concatenate([jnp.broadcast_to(g, (c, g.shape[1])) for g in glasts], axis=0)
    q_dec = qq * jnp.exp(gc)
    k_inv = kk * jnp.exp(-gc)
    k_end = kk * jnp.exp(glast_rows - gc)
    att = jnp.where(own, bdot(q_dec, k_inv, NT), 0.0)
    o = bdot(att, ir)
    inter = []
    for j in range(n_chunks):
        sl = slice(c * j, c * j + c)
        inter.append(bdot(q_dec[sl], state_t, NT))
        state_t = state_t * jnp.exp(glasts[j]) + bdot(ir[sl], k_end[sl], TN)
    o = o + jnp.concatenate(inter, axis=0)
    return _rms(o, nw) * _silu(gr), state_t


def gla_op(name, layer, bl, seq, pending=None, n_gather=0):
    r = HGRN_ROWS
    ns = seq // r
    blk = lambda k: In((bl, r, 128), lambda h, n: (0, n, h), cols=(1024 * k, 1024))
    ins = [blk(0), blk(1), blk(2), blk(3),
           In((DEPTH, 128), lambda h, n: (0, h), 'acc', (1,)),
           In((1, 128), lambda h, n: (0, 0), 'acc', (0, 1))]
    outs = [Out((bl, seq, D_MODEL), F32, (bl, r, 128), lambda h, n: (0, n, h))]
    return make_op(name, _per_sequence(functools.partial(_gla_fn, layer), 4, bl), (H_HEADS, ns), ins, outs,
                   state_shape=(bl, 128, 128), seq_axis=1, pending=pending, n_gather=n_gather)


def _neumann_inverse(m):
    q = m.shape[1]
    ii = lax.broadcasted_iota(jnp.int32, (q, q), 0)
    jj = lax.broadcasted_iota(jnp.int32, (q, q), 1)
    eye = (ii == jj).astype(F32)[None]
    p = -m
    inv = eye + p
    for _ in range(int(math.log2(q)) - 1):
        p = _h3(p, p, BNN)
        inv = inv + _h3(inv, p, BNN)
    return inv


@jax.custom_vjp
def _unit_lower_inverse(m):
    return _neumann_inverse(m)


def _unit_lower_inverse_fwd(m):
    inv = _neumann_inverse(m)
    return inv, inv


_unit_lower_inverse.defvjp(_unit_lower_inverse_fwd,
                           lambda inv, ct: (-_h3(_h3(inv, ct, BTN), inv, BNT),))


def _gdn_fn(qc, kc, vc, zc, br, ar, alog, dtb, nw, state):
    bl, q = qc.shape[0], qc.shape[1]
    incl, strict = _tri(q)
    tril = incl.astype(F32)
    g = jnp.concatenate([-jnp.exp(alog) * jax.nn.softplus(ar[b] + dtb) for b in range(bl)], axis=1)
    gc = cumdot(tril, g)
    gc_t = cumdot_t(tril, g)
    heads, ms, rhs = [], [], []
    for b in range(bl):
        qn = qc[b] * lax.rsqrt(jnp.sum(qc[b] * qc[b], axis=-1, keepdims=True) + EPS) * (128 ** -0.5)
        kn = kc[b] * lax.rsqrt(jnp.sum(kc[b] * kc[b], axis=-1, keepdims=True) + EPS)
        beta = jax.nn.sigmoid(br[b])
        qk = bdot(qn, kn, NT)
        for j in range(2):
            i = 2 * b + j
            col = gc[:, i:i + 1]
            decay = jnp.exp(jnp.where(incl, col - gc_t[i:i + 1, :], -jnp.inf))
            bj = beta[:, j:j + 1]
            kb = kn * bj
            ms.append(jnp.where(strict, bdot(kb, kn, NT) * decay, 0.0))
            rhs.append(jnp.concatenate([vc[b][:, 128 * j:128 * j + 128] * bj, kb * jnp.exp(col)], axis=1))
            heads.append((qn, kn, qk * decay, col, gc[q - 1:q, i:i + 1]))
    sol = h3dot_b(_unit_lower_inverse(jnp.concatenate([m[None] for m in ms])),
                  jnp.concatenate([r[None] for r in rhs]))
    outs, states = [], []
    for b in range(bl):
        os_, sts = [], []
        for j in range(2):
            i = 2 * b + j
            qn, kn, att, col, glast = heads[i]
            u = sol[i][:, :128]
            w = sol[i][:, 128:]
            st = state[b][128 * j:128 * j + 128, :]
            v_new = u - bdot(w, st)
            o = bdot(qn * jnp.exp(col), st) + bdot(att, v_new)
            sts.append(st * jnp.exp(glast) + bdot(kn * jnp.exp(glast - col), v_new, TN))
            os_.append(_rms(o, nw) * _silu(zc[b][:, 128 * j:128 * j + 128]))
        outs.append(jnp.concatenate(os_, axis=1))
        states.append(jnp.concatenate(sts, axis=0))
    return jnp.concatenate([o[None] for o in outs]), jnp.concatenate([st[None] for st in states])


def gdn_op(name, bl, seq, pending=None, n_gather=0):
    q = GDN_CHUNK
    nc = seq // q
    blk = lambda w, c0, cw: In((bl, q, w), lambda h, n: (0, n, h), cols=(c0, cw))
    small = lambda h, n: (h, 0, 0)
    ins = [
        blk(128, 0, G_KEY_DIM),
        blk(128, G_KEY_DIM, G_KEY_DIM),
        blk(256, 2 * G_KEY_DIM, G_VAL_DIM),
        blk(256, G_CONV_DIM, G_VAL_DIM),
        In((None, bl, q, 2), lambda h, n: (h, 0, n, 0)),
        In((None, bl, q, 2), lambda h, n: (h, 0, n, 0)),
        In((None, 1, 2), small, 'acc', (1,)),
        In((None, 1, 2), small, 'acc', (1,)),
        In((1, 128), lambda h, n: (0, 0), 'acc', (0, 1)),
    ]
    outs = [Out((bl, seq, G_VAL_DIM), F32, (bl, q, 256), lambda h, n: (0, n, h))]
    return make_op(name, _gdn_fn, (G_QK_HEADS, nc), ins, outs,
                   state_shape=(bl, 256, 128), seq_axis=1, pending=pending, n_gather=n_gather)


def _xattn_fn(q, k, v):
    s = bdot(q, k, NT) * (X_HEAD_DIM ** -0.5)
    s = s - jnp.max(s, axis=-1, keepdims=True)
    p = jnp.exp(s)
    p = p / jnp.sum(p, axis=-1, keepdims=True)
    return (bdot(p, v),)


def xattn_op(name, bl, seq):
    tq = _tile(seq, 2048)
    nq = seq // tq
    t = bl * seq
    ins = [
        In((tq, X_HEAD_DIM), lambda b, h, i: (b * nq + i, h)),
        In((N_MEM, X_HEAD_DIM), lambda b, h, i: (b, h), 'acc', (2,), cols=(0, D_MODEL)),
        In((N_MEM, X_HEAD_DIM), lambda b, h, i: (b, h), 'acc', (2,), cols=(D_MODEL, D_MODEL)),
    ]
    outs = [Out((t, D_MODEL), F32, (tq, X_HEAD_DIM), lambda b, h, i: (b * nq + i, h))]
    return make_op(name, _xattn_fn, (bl, X_HEADS, nq), ins, outs)


CONV_PAD = 8
CONV_ROWS = 128


def make_conv(name, bl, seq, width, ch, x_col0, up_col0=None):
    cb = 256
    rt = CONV_ROWS
    assert ch % cb == 0 and x_col0 % cb == 0 and (up_col0 is None or up_col0 % cb == 0) and seq % rt == 0
    nb = ch // cb
    n_tiles = seq // rt
    t = bl * seq
    has_up = up_col0 is not None
    grid = (nb, bl)
    x_spec = pl.BlockSpec((seq, cb), lambda c, b: (b, x_col0 // cb + c))
    up_specs = [pl.BlockSpec((seq, cb), lambda c, b: (b, up_col0 // cb + c))] if has_up else []
    w_spec = pl.BlockSpec((width, cb), lambda c, b: (0, c))
    b_spec = pl.BlockSpec((1, cb), lambda c, b: (0, c))
    o_spec = pl.BlockSpec((seq, cb), lambda c, b: (b, c))
    taps = [CONV_PAD - (width - 1) + j for j in range(width)]

    def window(x_ref, i):
        if isinstance(i, int) and i == 0:
            return jnp.concatenate([jnp.zeros((CONV_PAD, cb), F32), x_ref[0:rt, :]], axis=0)
        return x_ref[pl.ds(pl.multiple_of(i * rt - CONV_PAD, CONV_PAD), rt + CONV_PAD), :]

    def rows(i):
        return pl.ds(i * rt, rt) if isinstance(i, int) else pl.ds(pl.multiple_of(i * rt, rt), rt)

    def shifted(win):
        return [win[tp:tp + rt, :] for tp in taps]

    def pre_activation(views, w, b):
        y = b + w[0:1, :] * views[0]
        for j in range(1, width):
            y = y + w[j:j + 1, :] * views[j]
        return y

    def over_tiles(step, carry):
        carry = step(0, carry)
        return lax.fori_loop(1, n_tiles, step, carry)

    def fwd_call(x, w, b):
        def body(*refs):
            x_ref, w_ref, b_ref = refs[:3]
            o_ref = refs[-1]
            w_, b_ = w_ref[...], b_ref[...]

            def step(i, carry):
                y = _silu(pre_activation(shifted(window(x_ref, i)), w_, b_))
                if has_up:
                    y = y * refs[3][rows(i), :]
                o_ref[rows(i), :] = y
                return carry

            over_tiles(step, 0)

        return pl.pallas_call(
            body, name=name + "_fwd", grid=grid,
            in_specs=[x_spec, w_spec, b_spec] + up_specs, out_specs=o_spec,
            out_shape=jax.ShapeDtypeStruct((t, ch), F32),
            compiler_params=_cparams(),
        )(*([x, w, b] + ([x] if has_up else [])))

    def bwd_call(x, w, b, do):
        n_in = 4 + (1 if has_up else 0)

        def body(*refs):
            x_ref, w_ref, b_ref = refs[:3]
            do_ref = refs[n_in - 1]
            dx_ref, dw_ref, db_ref = refs[n_in:n_in + 3]
            gpad_ref = refs[-1]
            w_, b_ = w_ref[...], b_ref[...]

            def fold(a):
                acc = a[0:8, :]
                for k in range(1, rt // 8):
                    acc = acc + a[8 * k:8 * k + 8, :]
                return acc

            def grad_pre(i, sums):
                views = shifted(window(x_ref, i))
                y = pre_activation(views, w_, b_)
                s = jax.nn.sigmoid(y)
                act = y * s
                do_ = do_ref[rows(i), :]
                if has_up:
                    refs[n_in + 3][rows(i), :] = do_ * act
                    do_ = do_ * refs[3][rows(i), :]
                dy = do_ * (s + act * (1.0 - s))
                gpad_ref[rows(i), :] = dy
                new = [sums[j] + fold(dy * views[j]) for j in range(width)]
                return tuple(new) + (sums[width] + fold(dy),)

            zero8 = jnp.zeros((8, cb), F32)
            sums = over_tiles(grad_pre, (zero8,) * (width + 1))
            gpad_ref[seq:seq + CONV_PAD, :] = jnp.zeros((CONV_PAD, cb), F32)

            def grad_x(i, carry):
                if isinstance(i, int):
                    gwin = gpad_ref[0:rt + CONV_PAD, :]
                else:
                    gwin = gpad_ref[pl.ds(pl.multiple_of(i * rt, rt), rt + CONV_PAD), :]
                dx = w_[0:1, :] * gwin[width - 1:width - 1 + rt, :]
                for j in range(1, width):
                    dx = dx + w_[j:j + 1, :] * gwin[width - 1 - j:width - 1 - j + rt, :]
                dx_ref[rows(i), :] = dx
                return carry

            over_tiles(grad_x, 0)

            @pl.when(pl.program_id(1) == 0)
            def _():
                dw_ref[...] = jnp.zeros_like(dw_ref)
                db_ref[...] = jnp.zeros_like(db_ref)

            dw_ref[...] += jnp.concatenate([jnp.sum(sums[j], axis=0, keepdims=True) for j in range(width)], axis=0)
            db_ref[...] += jnp.sum(sums[width], axis=0, keepdims=True)

        big = jax.ShapeDtypeStruct((t, ch), F32)
        return pl.pallas_call(
            body, name=name + "_bwd", grid=grid,
            in_specs=[x_spec, w_spec, b_spec] + up_specs + [o_spec],
            out_specs=[o_spec, w_spec, b_spec] + ([o_spec] if has_up else []),
            out_shape=[big, jax.ShapeDtypeStruct((width, ch), F32), jax.ShapeDtypeStruct((1, ch), F32)]
            + ([big] if has_up else []),
            scratch_shapes=[pltpu.VMEM((seq + CONV_PAD, cb), F32)],
            compiler_params=_cparams(),
        )(*([x, w, b] + ([x] if has_up else []) + [do]))

    @jax.custom_vjp
    def conv(x, w, b):
        return fwd_call(x, w, b)

    def conv_fwd(x, w, b):
        return fwd_call(x, w, b), (x, w, b)

    def conv_bwd(res, do):
        x, w, b = res
        got = bwd_call(x, w, b, do)
        dx = jnp.pad(got[0], ((0, 0), (x_col0, x.shape[1] - x_col0 - ch)))
        if has_up:
            dx = dx + jnp.pad(got[3], ((0, 0), (up_col0, x.shape[1] - up_col0 - ch)))
        return dx, got[1], got[2]

    conv.defvjp(conv_fwd, conv_bwd)

    def apply(x, w, b=None):
        if b is None:
            b = jnp.zeros((ch,), F32)
        return conv(x, w, b.reshape(1, ch))

    return apply


def loss_head(x, w, target):
    t = x.shape[0]
    tm = _tile(t, 512)

    def fn(xb, wb, tb):
        err = _rms(xb, wb) - tb
        return 0.5 * jnp.sum(err * err) * (1.0 / D_MODEL)

    def body(x_ref, w_ref, t_ref, loss_ref, dx_ref, dw_ref):
        @pl.when(pl.program_id(0) == 0)
        def _():
            loss_ref[...] = jnp.zeros_like(loss_ref)
            dw_ref[...] = jnp.zeros_like(dw_ref)

        tb = t_ref[...]
        val, vjp = jax.vjp(lambda a, b: fn(a, b, tb), x_ref[...], w_ref[...])
        dx, dw = vjp(jnp.ones((), F32))
        dx_ref[...] = dx
        dw_ref[...] += dw
        loss_ref[...] += jnp.full(loss_ref.shape, val, F32)

    row = pl.BlockSpec((tm, D_MODEL), lambda i: (i, 0))
    vec = pl.BlockSpec((1, D_MODEL), lambda i: (0, 0))
    loss, dx, dw = pl.pallas_call(
        body, name="loss_head", grid=(t // tm,),
        in_specs=[row, vec, row],
        out_specs=[pl.BlockSpec((8, LANE), lambda i: (0, 0)), row, vec],
        out_shape=[jax.ShapeDtypeStruct((8, LANE), F32), jax.ShapeDtypeStruct((t, D_MODEL), F32),
                   jax.ShapeDtypeStruct((1, D_MODEL), F32)],
        compiler_params=_cparams(),
    )(x, w.reshape(1, D_MODEL), target)
    return loss[0, 0], dx, dw.reshape(D_MODEL)


PACK_W = 1024
ADAM_BLOCK_BYTES = 512 * 1024


def _rows_tile(r, c):
    if r * c * 4 <= ADAM_BLOCK_BYTES or r % 8:
        return r
    best = 8
    for t in range(8, r + 1, 8):
        if r % t == 0 and t * c * 4 <= ADAM_BLOCK_BYTES:
            best = t
    return best


def reduce_adamw(slots, w, m, v, name):
    r, wd = w.shape
    tr = _rows_tile(r, wd)
    c1 = 1.0 - ADAM_B1 ** ADAM_STEP
    c2 = 1.0 - ADAM_B2 ** ADAM_STEP

    def body(s_ref, w_ref, m_ref, v_ref, g_ref, d_ref, nm_ref, nv_ref):
        g = s_ref[0].astype(F32)
        for k in range(1, N_DEV):
            g = g + s_ref[k].astype(F32)
        nm = ADAM_B1 * m_ref[...] + (1.0 - ADAM_B1) * g
        nv = ADAM_B2 * v_ref[...] + (1.0 - ADAM_B2) * (g * g)
        m_hat = nm / c1
        v_hat = nv / c2
        d_ref[...] = -ADAM_LR * (m_hat / (jnp.sqrt(v_hat) + ADAM_EPS) + ADAM_WD * w_ref[...])
        g_ref[...] = g
        nm_ref[...] = nm
        nv_ref[...] = nv

    blk = pl.BlockSpec((tr, wd), lambda i: (i, 0))
    shp = jax.ShapeDtypeStruct((r, wd), F32)
    return pl.pallas_call(
        body, name=name, grid=(r // tr,),
        in_specs=[pl.BlockSpec((N_DEV, tr, wd), lambda i: (0, i, 0)), blk, blk, blk],
        out_specs=[blk, blk, blk, blk], out_shape=[shp, shp, shp, shp],
        compiler_params=_cparams(),
    )(slots, w, m, v)


def all_gather(block, name):
    def body(x_ref, out_ref, send_sems, recv_sems, local_sem):
        x, y, c = _position()
        me, sibling = (x, y, c), (x, y, 1 - c)
        chips = [(1 - x, y), (x, 1 - y), (1 - x, 1 - y)]

        def slot(px, py, pc):
            return out_ref.at[4 * px + 2 * py + pc]

        def copy(k, owner, to, src=None):
            return pltpu.make_async_remote_copy(
                src_ref=slot(*owner) if src is None else src, dst_ref=slot(*owner),
                send_sem=send_sems.at[k], recv_sem=recv_sems.at[k],
                device_id=to, device_id_type=pl.DeviceIdType.MESH)

        mine = pltpu.make_async_copy(x_ref, slot(*me), local_sem)
        mine.start()
        first = [copy(0, me, sibling, src=x_ref)]
        first += [copy(1 + j, me, (*chip, c), src=x_ref) for j, chip in enumerate(chips)]
        for cp in first:
            cp.start()
        passed = [copy(4 + j, (*chip, c), sibling) for j, chip in enumerate(chips)]
        for j, chip in enumerate(chips):
            copy(1 + j, (*chip, c), me).wait_recv()
            passed[j].start()
        copy(0, sibling, me).wait_recv()
        for j, chip in enumerate(chips):
            copy(4 + j, (*chip, 1 - c), me).wait_recv()
        for cp in first + passed:
            cp.wait_send()
        mine.wait()

    return pl.pallas_call(
        body, name=name,
        out_shape=jax.ShapeDtypeStruct((N_DEV,) + block.shape, block.dtype),
        in_specs=[pl.BlockSpec(memory_space=pl.ANY)],
        out_specs=pl.BlockSpec(memory_space=pl.ANY),
        scratch_shapes=[pltpu.SemaphoreType.DMA((7,)), pltpu.SemaphoreType.DMA((7,)), pltpu.SemaphoreType.DMA],
    )(block)


def exchange_slabs(slabs, name):
    def body(in_ref, out_ref, send_sems, recv_sems, local_sem):
        x, y, c = _position()
        my = 4 * x + 2 * y + c
        mine = pltpu.make_async_copy(in_ref.at[my], out_ref.at[my], local_sem)
        mine.start()
        copies = []
        for k in range(1, N_DEV):
            dx, dy, dc = (k >> 2) & 1, (k >> 1) & 1, k & 1
            px = x if dx == 0 else 1 - x
            py = y if dy == 0 else 1 - y
            pc = c if dc == 0 else 1 - c
            cp = pltpu.make_async_remote_copy(
                src_ref=in_ref.at[4 * px + 2 * py + pc], dst_ref=out_ref.at[my],
                send_sem=send_sems.at[k - 1], recv_sem=recv_sems.at[k - 1],
                device_id=(px, py, pc), device_id_type=pl.DeviceIdType.MESH)
            cp.start()
            copies.append(cp)
        for cp in copies:
            cp.wait()
        mine.wait()

    return pl.pallas_call(
        body, name=name,
        out_shape=jax.ShapeDtypeStruct(slabs.shape, slabs.dtype),
        in_specs=[pl.BlockSpec(memory_space=pl.ANY)],
        out_specs=pl.BlockSpec(memory_space=pl.ANY),
        scratch_shapes=[pltpu.SemaphoreType.DMA((7,)), pltpu.SemaphoreType.DMA((7,)), pltpu.SemaphoreType.DMA],
    )(slabs)


def _pack(arrays, dtype, row_multiple):
    flat = jnp.concatenate([a.astype(dtype).reshape(-1) for a in arrays])
    n = flat.shape[0]
    per = PACK_W * row_multiple
    total = -(-n // per) * per
    flat = jnp.pad(flat, (0, total - n))
    return flat.reshape(total // PACK_W, PACK_W)


def _unpack(flat2d, shapes, lead=()):
    flat = flat2d.reshape(lead + (-1,))
    out, off = [], 0
    for shp in shapes:
        n = math.prod(shp)
        out.append(flat[..., off:off + n].reshape(lead + tuple(shp)))
        off += n
    return out


def _full_from_gathered(g, axis):
    g = jnp.moveaxis(g, 0, axis)
    shp = list(g.shape)
    shp[axis:axis + 2] = [shp[axis] * shp[axis + 1]]
    return g.reshape(shp)


def _shards_of_full(full, axis):
    shp = list(full.shape)
    shp[axis:axis + 1] = [N_DEV, shp[axis] // N_DEV]
    return jnp.moveaxis(full.reshape(shp), axis, 0)


def layer_units(i):
    mixer = [('m_in_w', 'm_out_w'), ('h_in_w', 'h_out_w'), ('g_in_w', 'g_out_w')][i % 3]
    return [(mixer[0], i // 3), (mixer[1], i // 3), ('xa_q', i), ('xa_kv', i), ('xa_o', i), ('f_up', i), ('f_down', i)]


PADDED_COLS = {'m_in_w': M_IN_PAD, 'g_in_w': G_IN_PAD}
BIG_WEIGHTS = ('m_in_w', 'h_in_w', 'g_in_w', 'f_up')


def whole_weight(name, gathered):
    w = _full_from_gathered(lax.stop_gradient(gathered), SHARD_AXIS[name] - 1)
    return _pad_cols(w, 1, PADDED_COLS[name]) if name in PADDED_COLS else w


def _trunk(p, weights, blocks, standins, x, mem, bl, seq, pending=None):
    t = bl * seq
    ia = ib = ic = 0
    weights = dict(weights)
    state = {}

    def lin(name, a, wname, idx, residual=None):
        unit = (wname, idx)
        pos = state['units'].index(unit)
        later = state['next'][pos] if state['next'] else None
        nxt = (blocks[later],) if later in blocks and later not in state['by_core'] else ()
        n_real = N_DEV * standins[unit].shape[2]
        res = () if residual is None else (residual,)
        mixer_in = wname in ('m_in_w', 'h_in_w', 'g_in_w')
        defer = (pending, unit) if pending is not None and not (mixer_in and i == 0) else None
        y, got = make_linear(name, SHARD_AXIS[wname] - 1, n_real, bool(res), defer)(
            a, weights[unit], standins[unit], nxt, res)
        if nxt:
            weights[later] = whole_weight(later[0], got[0])
        return y

    by_seq = lambda a: a.reshape(bl, seq, a.shape[-1])

    for i in range(DEPTH):
        state['units'] = layer_units(i)
        state['next'] = layer_units(i + 1) if i + 1 < DEPTH else None
        state['by_core'] = [u for u in (state['next'] or [])
                            if u in blocks and (u[0] in BIG_WEIGHTS[:3 if i % 3 == 1 else 4] or i % 3 == 2)]
        core_blocks = [blocks[u] for u in state['by_core']]

        def core(op, *args):
            y, *got = op(*args, *core_blocks)
            for u, g in zip(state['by_core'], got):
                weights[u] = whole_weight(u[0], g)
            return y
        hn, x = rmsnorm_op(f"ln_mix{i}", t, F32, residual=True)(x, p['ln_mix'][i:i + 1])
        kind = i % 3
        if kind == 0:
            proj = lin(f"m_in{i}", hn, 'm_in_w', ia)
            xbc = make_conv(f"m_conv{i}", bl, seq, 4, M_CONV_DIM, M_D_INNER)(
                proj, p['m_conv_w'][ia], p['m_conv_b'][ia])
            dt = proj[:, M_D_INNER + M_CONV_DIM:M_IN].reshape(bl, seq, M_GROUPS, 4).transpose(2, 0, 1, 3)
            grp = lambda a, n=4: a.reshape(M_GROUPS, 1, n)
            proj3, xbc3 = by_seq(proj), by_seq(xbc)
            y = core(ssd_op(f"ssd{i}", bl, seq, pending, len(core_blocks)),
                     proj3, xbc3, xbc3, xbc3, dt, grp(p['m_dt_bias'][ia]), grp(p['m_a_log'][ia]), grp(p['m_d'][ia]),
                     grp(p['m_norm_w'][ia], 256))
            x = lin(f"m_out{i}", y.reshape(t, M_D_INNER), 'm_out_w', ia, residual=x)
            ia += 1
        elif kind == 1:
            proj3 = by_seq(lin(f"h_in{i}", hn, 'h_in_w', ib))
            y = core(gla_op(f"gla{i}", i, bl, seq, pending, len(core_blocks)),
                     proj3, proj3, proj3, proj3, p['h_lower_bounds'], p['h_norm_w'][ib:ib + 1])
            x = lin(f"h_out{i}", y.reshape(t, D_MODEL), 'h_out_w', ib, residual=x)
            ib += 1
        else:
            proj = lin(f"g_in{i}", hn, 'g_in_w', ic)
            qkv = make_conv(f"g_conv{i}", bl, seq, 4, G_CONV_DIM, 0)(proj, p['g_conv_w'][ic])
            c0 = G_CONV_DIM + G_VAL_DIM
            heads = lambda a: a.reshape(bl, seq, G_QK_HEADS, 2).transpose(2, 0, 1, 3)
            braw = heads(proj[:, c0:c0 + G_V_HEADS])
            araw = heads(proj[:, c0 + G_V_HEADS:c0 + 2 * G_V_HEADS])
            grp = lambda a: a.reshape(G_QK_HEADS, 1, 2)
            qkv3 = by_seq(qkv)
            y = core(gdn_op(f"gdn{i}", bl, seq, pending, len(core_blocks)),
                     qkv3, qkv3, qkv3, by_seq(proj), braw, araw, grp(p['g_a_log'][ic]), grp(p['g_dt_bias'][ic]),
                     p['g_norm_w'][ic:ic + 1])
            x = lin(f"g_out{i}", y.reshape(t, G_VAL_DIM), 'g_out_w', ic, residual=x)
            ic += 1
        hq, x = rmsnorm_op(f"ln_xattn{i}", t, F32, residual=True)(x, p['ln_xattn'][i:i + 1])
        mn = rmsnorm_op(f"ln_mem{i}", bl * N_MEM, F32)(mem, p['ln_mem'][i:i + 1])[0]
        qx = lin(f"xa_q{i}", hq, 'xa_q', i)
        kv = lin(f"xa_kv{i}", mn, 'xa_kv', i)
        ao = xattn_op(f"xattn{i}", bl, seq)(qx, kv, kv)[0]
        x = lin(f"xa_o{i}", ao, 'xa_o', i, residual=x)
        hf, x = rmsnorm_op(f"ln_ffn{i}", t, F32, residual=True)(x, p['ln_ffn'][i:i + 1])
        up = lin(f"f_up{i}", hf, 'f_up', i)
        act = make_conv(f"f_conv{i}", bl, seq, 3, D_FF, 0, up_col0=D_FF)(up, p['f_conv_w'][i], p['f_conv_b'][i])
        x = lin(f"f_down{i}", act, 'f_down', i, residual=x)
    return x


def _pad_cols(w, axis, to):
    pad = [(0, 0)] * w.ndim
    pad[axis] = (0, to - w.shape[axis])
    return jnp.pad(w, pad)


def kernel(x, mem, ln_mix, ln_xattn, ln_mem, ln_ffn, final_norm, m_in_w, m_conv_w, m_conv_b, m_dt_bias, m_a_log, m_d, m_norm_w, m_out_w, h_in_w, h_lower_bounds, h_norm_w, h_out_w, g_in_w, g_conv_w, g_a_log, g_dt_bias, g_norm_w, g_out_w, xa_q, xa_kv, xa_o, f_up, f_conv_w, f_conv_b, f_down, loss_target, m_ln_mix, m_ln_xattn, m_ln_mem, m_ln_ffn, m_final_norm, m_m_in_w, m_m_conv_w, m_m_conv_b, m_m_dt_bias, m_m_a_log, m_m_d, m_m_norm_w, m_m_out_w, m_h_in_w, m_h_lower_bounds, m_h_norm_w, m_h_out_w, m_g_in_w, m_g_conv_w, m_g_a_log, m_g_dt_bias, m_g_norm_w, m_g_out_w, m_xa_q, m_xa_kv, m_xa_o, m_f_up, m_f_conv_w, m_f_conv_b, m_f_down, v_ln_mix, v_ln_xattn, v_ln_mem, v_ln_ffn, v_final_norm, v_m_in_w, v_m_conv_w, v_m_conv_b, v_m_dt_bias, v_m_a_log, v_m_d, v_m_norm_w, v_m_out_w, v_h_in_w, v_h_lower_bounds, v_h_norm_w, v_h_out_w, v_g_in_w, v_g_conv_w, v_g_a_log, v_g_dt_bias, v_g_norm_w, v_g_out_w, v_xa_q, v_xa_kv, v_xa_o, v_f_up, v_f_conv_w, v_f_conv_b, v_f_down):
    local = dict(ln_mix=ln_mix, ln_xattn=ln_xattn, ln_mem=ln_mem, ln_ffn=ln_ffn, final_norm=final_norm, m_in_w=m_in_w, m_conv_w=m_conv_w, m_conv_b=m_conv_b, m_dt_bias=m_dt_bias, m_a_log=m_a_log, m_d=m_d, m_norm_w=m_norm_w, m_out_w=m_out_w, h_in_w=h_in_w, h_lower_bounds=h_lower_bounds, h_norm_w=h_norm_w, h_out_w=h_out_w, g_in_w=g_in_w, g_conv_w=g_conv_w, g_a_log=g_a_log, g_dt_bias=g_dt_bias, g_norm_w=g_norm_w, g_out_w=g_out_w, xa_q=xa_q, xa_kv=xa_kv, xa_o=xa_o, f_up=f_up, f_conv_w=f_conv_w, f_conv_b=f_conv_b, f_down=f_down)
    mom_m = dict(ln_mix=m_ln_mix, ln_xattn=m_ln_xattn, ln_mem=m_ln_mem, ln_ffn=m_ln_ffn, final_norm=m_final_norm, m_in_w=m_m_in_w, m_conv_w=m_m_conv_w, m_conv_b=m_m_conv_b, m_dt_bias=m_m_dt_bias, m_a_log=m_m_a_log, m_d=m_m_d, m_norm_w=m_m_norm_w, m_out_w=m_m_out_w, h_in_w=m_h_in_w, h_lower_bounds=m_h_lower_bounds, h_norm_w=m_h_norm_w, h_out_w=m_h_out_w, g_in_w=m_g_in_w, g_conv_w=m_g_conv_w, g_a_log=m_g_a_log, g_dt_bias=m_g_dt_bias, g_norm_w=m_g_norm_w, g_out_w=m_g_out_w, xa_q=m_xa_q, xa_kv=m_xa_kv, xa_o=m_xa_o, f_up=m_f_up, f_conv_w=m_f_conv_w, f_conv_b=m_f_conv_b, f_down=m_f_down)
    mom_v = dict(ln_mix=v_ln_mix, ln_xattn=v_ln_xattn, ln_mem=v_ln_mem, ln_ffn=v_ln_ffn, final_norm=v_final_norm, m_in_w=v_m_in_w, m_conv_w=v_m_conv_w, m_conv_b=v_m_conv_b, m_dt_bias=v_m_dt_bias, m_a_log=v_m_a_log, m_d=v_m_d, m_norm_w=v_m_norm_w, m_out_w=v_m_out_w, h_in_w=v_h_in_w, h_lower_bounds=v_h_lower_bounds, h_norm_w=v_h_norm_w, h_out_w=v_h_out_w, g_in_w=v_g_in_w, g_conv_w=v_g_conv_w, g_a_log=v_g_a_log, g_dt_bias=v_g_dt_bias, g_norm_w=v_g_norm_w, g_out_w=v_g_out_w, xa_q=v_xa_q, xa_kv=v_xa_kv, xa_o=v_xa_o, f_up=v_f_up, f_conv_w=v_f_conv_w, f_conv_b=v_f_conv_b, f_down=v_f_down)

    bl, seq, _ = x.shape
    t = bl * seq

    p = {n: local[n] for n in WEIGHTS if n not in SHARD_AXIS}
    for n in SMALL_SHARDED:
        p[n] = _full_from_gathered(all_gather(local[n], f"gather_{n}"), SHARD_AXIS[n])
    units = [(n, l) for n in MATMUL_WEIGHTS for l in range(local[n].shape[0])]
    block = lambda u: local[u[0]][u[1]].astype(BF16)
    weights = {u: whole_weight(u[0], all_gather(block(u), f"gather_{u[0]}{u[1]}")) for u in layer_units(0)}
    blocks = {u: block(u) for u in units if u not in weights}
    standins = {u: jnp.zeros((N_DEV,) + local[u[0]].shape[1:], BF16) for u in units}
    small = {n: p[n] for n in WEIGHTS if n not in MATMUL_WEIGHTS and n != 'final_norm'}

    pending = Pending()

    def run(small_w, standins_, xin):
        return _trunk(small_w, weights, blocks, standins_, xin, mem.reshape(bl * N_MEM, D_MODEL), bl, seq, pending)

    x_out, vjp = jax.vjp(run, small, standins, x.reshape(t, D_MODEL))
    loss_part, dx_out, d_final = loss_head(x_out, final_norm, loss_target.reshape(t, D_MODEL))
    grads, received, dx = vjp(dx_out)
    received = dict(received)
    for unit, slabs in pending.take_all():
        pending.received[unit] = exchange_slabs(slabs, f"exchange_{unit[0]}{unit[1]}")
    received.update(pending.received)
    grads = dict(grads)
    grads['final_norm'] = d_final
    loss = lax.psum(loss_part, ("x", "y", "c"))

    outs = {}

    def update(name, n, slots, shape, sel=lambda a: a):
        two_d = lambda a: sel(a).reshape(slots.shape[1:])
        got = reduce_adamw(slots, two_d(local[n]), two_d(mom_m[n]), two_d(mom_v[n]), name)
        return [g.reshape(shape) for g in got]

    for n in SMALL_SHARDED:
        slots = exchange_slabs(_shards_of_full(grads[n], SHARD_AXIS[n]), f"exchange_{n}")
        slots = slots.reshape(N_DEV, -1, slots.shape[-1])
        for kind, a in zip(KINDS, update(f"adamw_{n}", n, slots, local[n].shape)):
            outs[kind, n] = a
    for n in MATMUL_WEIGHTS:
        per_layer = [update(f"adamw_{n}{l}", n, received[n, l], local[n].shape[1:], lambda a, l=l: a[l])
                     for l in range(local[n].shape[0])]
        for k, kind in enumerate(KINDS):
            outs[kind, n] = jnp.concatenate([got[k][None] for got in per_layer])
    replicated = [n for n in WEIGHTS if n not in SHARD_AXIS]
    pk = lambda d: _pack([d[n] for n in replicated], F32, 8)
    got = reduce_adamw(all_gather(pk(grads), "gather_replicated_grads"), pk(local), pk(mom_m), pk(mom_v),
                       "adamw_replicated")
    shapes = [local[n].shape for n in replicated]
    for kind, buf in zip(KINDS, got):
        for n, a in zip(replicated, _unpack(buf, shapes)):
            outs[kind, n] = a
    result = [loss, dx.reshape(bl, seq, D_MODEL)]
    for kind in KINDS:
        result += [outs[kind, n] for n in WEIGHTS]
    return tuple(result)
```

```python
import functools
import math

import jax
import jax.numpy as jnp
from jax import lax
from jax.experimental import pallas as pl
from jax.experimental.pallas import tpu as pltpu

F32 = jnp.float32
BF16 = jnp.bfloat16
NN = (((1,), (0,)), ((), ()))
NT = (((1,), (1,)), ((), ()))
TN = (((0,), (0,)), ((), ()))

D_MODEL = 1024
DEPTH = 4
EPS = 1e-6
N_MEM = 256
M_D_INNER = 2048
M_HEADS = 32
M_GROUPS = 8
M_STATE = 128
M_CONV_DIM = 4096
M_IN = 6176
M_IN_PAD = 6272
SSD_CHUNK = 256
H_HEADS = 8
HGRN_CHUNK = 32
HGRN_ROWS = 256
G_QK_HEADS = 8
G_V_HEADS = 16
G_KEY_DIM = 1024
G_VAL_DIM = 2048
G_CONV_DIM = 4096
G_IN = 6176
G_IN_PAD = 6272
GDN_CHUNK = 64
X_HEADS = 4
X_HEAD_DIM = 256
D_FF = 2816
ADAM_LR = 0.001
ADAM_B1 = 0.9
ADAM_B2 = 0.999
ADAM_EPS = 1e-08
ADAM_WD = 0.01
ADAM_STEP = 10

N_DEV = 8
LANE = 128
KINDS = ('grad', 'delta', 'new_m', 'new_v')
VMEM_LIMIT = 56 * 1024 * 1024

WEIGHTS = ['ln_mix', 'ln_xattn', 'ln_mem', 'ln_ffn', 'final_norm', 'm_in_w', 'm_conv_w', 'm_conv_b', 'm_dt_bias',
           'm_a_log', 'm_d', 'm_norm_w', 'm_out_w', 'h_in_w', 'h_lower_bounds', 'h_norm_w', 'h_out_w', 'g_in_w',
           'g_conv_w', 'g_a_log', 'g_dt_bias', 'g_norm_w', 'g_out_w', 'xa_q', 'xa_kv', 'xa_o', 'f_up', 'f_conv_w',
           'f_conv_b', 'f_down']
SHARD_AXIS = {'m_in_w': 2, 'm_conv_w': 2, 'm_conv_b': 1, 'm_norm_w': 1, 'm_out_w': 1, 'h_in_w': 2, 'h_out_w': 1,
              'g_in_w': 2, 'g_conv_w': 2, 'g_out_w': 1, 'xa_q': 1, 'xa_kv': 2, 'xa_o': 1, 'f_up': 2, 'f_conv_w': 2,
              'f_down': 1}
MATMUL_WEIGHTS = ['m_in_w', 'm_out_w', 'h_in_w', 'h_out_w', 'g_in_w', 'g_out_w', 'xa_q', 'xa_kv', 'xa_o', 'f_up',
                  'f_down']
SMALL_SHARDED = ['m_conv_w', 'm_conv_b', 'm_norm_w', 'g_conv_w', 'f_conv_w']


def _cparams():
    return pltpu.CompilerParams(vmem_limit_bytes=VMEM_LIMIT)


def bdot(a, b, dims=NN):
    return lax.dot_general(a.astype(BF16), b.astype(BF16), dims, preferred_element_type=F32)


def _split(a):
    hi = a.astype(BF16)
    return hi, (a - hi.astype(F32)).astype(BF16)


def _h3(a, b, dims):
    ah, al = _split(a)
    bh, bl = _split(b)
    d = functools.partial(lax.dot_general, dimension_numbers=dims, preferred_element_type=F32)
    return d(ah, bh) + (d(ah, bl) + d(al, bh))


BNN = (((2,), (1,)), ((0,), (0,)))
BNT = (((2,), (2,)), ((0,), (0,)))
BTN = (((1,), (1,)), ((0,), (0,)))


@jax.custom_vjp
def h3dot_b(a, b):
    return _h3(a, b, BNN)


h3dot_b.defvjp(lambda a, b: (_h3(a, b, BNN), (a, b)),
               lambda res, ct: (_h3(ct, res[1], BNT), _h3(res[0], ct, BTN)))

T_ROWS = (((0,), (1,)), ((), ()))


def _tri_times(tri, x, dims, tri_first):
    t = tri.astype(BF16)
    x0 = x.astype(BF16)
    r1 = x - x0.astype(F32)
    x1 = r1.astype(BF16)
    x2 = (r1 - x1.astype(F32)).astype(BF16)
    if tri_first:
        d = lambda xx: lax.dot_general(t, xx, dims, preferred_element_type=F32)
    else:
        d = lambda xx: lax.dot_general(xx, t, dims, preferred_element_type=F32)
    return d(x0) + (d(x1) + d(x2))


@jax.custom_vjp
def cumdot(tri, x):
    return _tri_times(tri, x, NN, True)


cumdot.defvjp(lambda tri, x: (_tri_times(tri, x, NN, True), tri),
              lambda tri, ct: (jnp.zeros_like(tri), _tri_times(tri, ct, TN, True)))


@jax.custom_vjp
def cumdot_t(tri, x):
    return _tri_times(tri, x, T_ROWS, False)


cumdot_t.defvjp(lambda tri, x: (_tri_times(tri, x, T_ROWS, False), tri),
                lambda tri, ct: (jnp.zeros_like(tri), _tri_times(tri, ct, T_ROWS, True)))


def _tile(dim, cap):
    if dim <= cap:
        return dim
    best = None
    for t in range(LANE, cap + 1, LANE):
        if dim % t == 0:
            best = t
    assert best is not None, dim
    return best


def _position():
    return lax.axis_index("x"), lax.axis_index("y"), lax.axis_index("c")


def _direct_copies(kind, src_ref, dst_ref, send_sems, recv_sems, local_sem):
    x, y, c = _position()
    me = 4 * x + 2 * y + c
    local_src = src_ref if kind == 'gather' else src_ref.at[me]
    copies = [pltpu.make_async_copy(local_src, dst_ref.at[me], local_sem)]
    for k in range(1, N_DEV):
        px = 1 - x if (k >> 2) & 1 else x
        py = 1 - y if (k >> 1) & 1 else y
        pc = 1 - c if k & 1 else c
        copies.append(pltpu.make_async_remote_copy(
            src_ref=src_ref if kind == 'gather' else src_ref.at[4 * px + 2 * py + pc], dst_ref=dst_ref.at[me],
            send_sem=send_sems.at[k - 1], recv_sem=recv_sems.at[k - 1],
            device_id=(px, py, pc), device_id_type=pl.DeviceIdType.MESH))
    return copies


COMM_SCRATCH = [pltpu.SemaphoreType.DMA((N_DEV - 1,)), pltpu.SemaphoreType.DMA((N_DEV - 1,)), pltpu.SemaphoreType.DMA]


class Pending:
    def __init__(self):
        self.jobs, self.received = [], {}

    def take_all(self):
        jobs, self.jobs = self.jobs, []
        return jobs


def matmul(a, b, *, ta=False, tb=False, out_dtype=F32, name="mm", carry=None, residual=None):
    if ta:
        k, m = a.shape
    else:
        m, k = a.shape
    if tb:
        n, k2 = b.shape
    else:
        k2, n = b.shape
    assert k == k2, (a.shape, b.shape, ta, tb)
    tm = _tile(m, 1408)
    tn = _tile(n, 1408)
    tk = _tile(k, 1408)
    grid = (m // tm, n // tn, k // tk)
    nk = grid[2]
    dims = (((0 if ta else 1,), (1 if tb else 0,)), ((), ()))

    def at_step(which):
        conds = [pl.program_id(ax) == (0 if which == 'first' else grid[ax] - 1) for ax in range(3)]
        return jnp.logical_and(jnp.logical_and(conds[0], conds[1]), conds[2])

    def body(*refs):
        r_ref = None
        if residual is not None:
            r_ref, refs = refs[2], refs[:2] + refs[3:]
        if carry is None:
            a_ref, b_ref, o_ref, acc_ref = refs
        else:
            a_ref, b_ref, src_ref, o_ref, dst_ref, acc_ref, send_sems, recv_sems, local_sem = refs
            copies = lambda: _direct_copies(carry[0], src_ref, dst_ref, send_sems, recv_sems, local_sem)

            @pl.when(at_step('first'))
            def _():
                for cp in copies():
                    cp.start()

        @pl.when(pl.program_id(2) == 0)
        def _():
            acc_ref[...] = jnp.zeros_like(acc_ref)

        acc_ref[...] += lax.dot_general(a_ref[...].astype(BF16), b_ref[...].astype(BF16), dims,
                                        preferred_element_type=F32)

        @pl.when(pl.program_id(2) == nk - 1)
        def _():
            out = acc_ref[...] if r_ref is None else acc_ref[...] + r_ref[...]
            o_ref[...] = out.astype(o_ref.dtype)

        if carry is not None:
            @pl.when(at_step('last'))
            def _():
                for cp in copies():
                    cp.wait()

    a_spec = pl.BlockSpec((tk, tm), lambda i, j, kk: (kk, i)) if ta else pl.BlockSpec((tm, tk), lambda i, j, kk: (i, kk))
    b_spec = pl.BlockSpec((tn, tk), lambda i, j, kk: (j, kk)) if tb else pl.BlockSpec((tk, tn), lambda i, j, kk: (kk, j))
    o_spec = pl.BlockSpec((tm, tn), lambda i, j, kk: (i, j))
    o_shape = jax.ShapeDtypeStruct((m, n), out_dtype)
    acc = pltpu.VMEM((tm, tn), F32)
    ins, in_specs = [a, b], [a_spec, b_spec]
    if residual is not None:
        ins.append(residual)
        in_specs.append(o_spec)
    if carry is None:
        return pl.pallas_call(
            body, name=name, grid=grid, in_specs=in_specs, out_specs=o_spec, out_shape=o_shape,
            scratch_shapes=[acc], compiler_params=_cparams(),
        )(*ins)
    kind, src = carry
    got = jax.ShapeDtypeStruct(((N_DEV,) + src.shape) if kind == 'gather' else src.shape, src.dtype)
    hbm = pl.BlockSpec(memory_space=pl.ANY)
    return pl.pallas_call(
        body, name=name, grid=grid, in_specs=in_specs + [hbm], out_specs=[o_spec, hbm],
        out_shape=[o_shape, got], scratch_shapes=[acc] + COMM_SCRATCH, compiler_params=_cparams(),
    )(*ins, src)


def make_linear(name, shard_axis, n_real, has_res=False, defer=None):
    def forward(a, w, nxt, res):
        r = res[0] if res else None
        if nxt:
            y, got = matmul(a, w, name=name + "_fwd", carry=('gather', nxt[0]), residual=r)
            return y, (got,)
        return matmul(a, w, name=name + "_fwd", residual=r), ()

    @jax.custom_vjp
    def linear(a, w, wg, nxt, res):
        return forward(a, w, nxt, res)

    def fwd(a, w, wg, nxt, res):
        return forward(a, w, nxt, res), (a, w, nxt)

    def bwd(saved, cts):
        a, w, nxt = saved
        dy = cts[0]
        dw = matmul(a, dy, ta=True, out_dtype=BF16, name=name + "_bwd_dw")
        slabs = _shards_of_full(dw[:, :n_real], shard_axis)
        if defer is None:
            da, slots = matmul(dy, w, tb=True, out_dtype=a.dtype, name=name + "_bwd_da", carry=('exchange', slabs))
        else:
            defer[0].jobs.append((defer[1], slabs))
            da, slots = matmul(dy, w, tb=True, out_dtype=a.dtype, name=name + "_bwd_da"), jnp.zeros_like(slabs)
        return da, jnp.zeros_like(w), slots, tuple(jnp.zeros_like(b) for b in nxt), ((dy,) if has_res else ())

    linear.defvjp(fwd, bwd)
    return linear


class In:
    def __init__(self, block, imap, kind='blk', inner=(), cols=None):
        self.block, self.imap, self.kind, self.inner, self.cols = block, imap, kind, inner, cols


class Out:
    def __init__(self, shape, dtype, block, imap):
        self.shape, self.dtype, self.block, self.imap = shape, dtype, block, imap


def make_op(name, fn, grid, ins, outs, state_shape=None, seq_axis=None, passthrough=(), pending=None, n_gather=0):
    n_in, n_out = len(ins), len(outs)
    has_state = state_shape is not None
    nd = len(grid)
    diff_idx = [i for i, s in enumerate(ins) if s.kind != 'const']

    def in_spec(s, reverse):
        off = 0
        if s.cols is not None:
            assert s.cols[0] % s.block[-1] == 0
            off = s.cols[0] // s.block[-1]

        def imap(*ids):
            ids = rev(ids) if reverse else ids
            idx = tuple(s.imap(*ids))
            return idx[:-1] + (idx[-1] + off,) if off else idx

        return pl.BlockSpec(s.block, imap)

    def rel_spec(block, f, reverse):
        return pl.BlockSpec(block, (lambda *ids: f(*rev(ids))) if reverse else f)

    def rev(ids):
        if not has_state:
            return ids
        ids = list(ids)
        ids[seq_axis] = grid[seq_axis] - 1 - ids[seq_axis]
        return tuple(ids)

    save_shape = tuple(grid) + tuple(state_shape) if has_state else None
    save_block = (None,) * nd + tuple(state_shape) if has_state else None

    def save_imap(*ids):
        return tuple(ids) + (0,) * len(state_shape)

    def step_is(which):
        conds = [pl.program_id(ax) == (0 if which == 'first' else grid[ax] - 1) for ax in range(nd)]
        return functools.reduce(jnp.logical_and, conds)

    def fwd_call(*xs):
        xs, blocks = xs[:n_in], xs[n_in:]
        n_save = 1 if has_state else 0

        def body(*refs):
            if blocks:
                src_refs = refs[n_in:n_in + n_gather]
                dst_refs = refs[n_in + n_gather + n_out + n_save:n_in + 2 * n_gather + n_out + n_save]
                sems = refs[len(refs) - 3 * n_gather:]
                refs = refs[:n_in] + refs[n_in + n_gather:n_in + n_gather + n_out + n_save] + \
                    refs[n_in + 2 * n_gather + n_out + n_save:len(refs) - 3 * n_gather]
                copies = lambda: [cp for k in range(n_gather) for cp in _direct_copies(
                    'gather', src_refs[k], dst_refs[k], sems[3 * k], sems[3 * k + 1], sems[3 * k + 2])]

                @pl.when(step_is('first'))
                def _():
                    for cp in copies():
                        cp.start()

            in_refs = refs[:n_in]
            out_refs = refs[n_in:n_in + n_out]
            vals = [r[...] for r in in_refs]
            if has_state:
                save_ref, st_ref = refs[n_in + n_out], refs[n_in + n_out + 1]

                @pl.when(pl.program_id(seq_axis) == 0)
                def _():
                    st_ref[...] = jnp.zeros(state_shape, F32)

                st = st_ref[...]
                save_ref[...] = st
                res = fn(*vals, st)
                st_ref[...] = res[-1]
                res = res[:-1]
            else:
                res = fn(*vals)
            for o, v in zip(out_refs, res):
                o[...] = v.astype(o.dtype)

            if blocks:
                @pl.when(step_is('last'))
                def _():
                    for cp in copies():
                        cp.wait()

        out_shape = [jax.ShapeDtypeStruct(o.shape, o.dtype) for o in outs]
        out_specs = [pl.BlockSpec(o.block, o.imap) for o in outs]
        scratch = []
        if has_state:
            out_shape.append(jax.ShapeDtypeStruct(save_shape, F32))
            out_specs.append(pl.BlockSpec(save_block, save_imap))
            scratch.append(pltpu.VMEM(state_shape, F32))
        hbm = pl.BlockSpec(memory_space=pl.ANY)
        for blk in blocks:
            out_shape.append(jax.ShapeDtypeStruct((N_DEV,) + blk.shape, blk.dtype))
            out_specs.append(hbm)
            scratch += COMM_SCRATCH
        return pl.pallas_call(
            body, name=name + "_fwd", grid=grid,
            in_specs=[in_spec(s, False) for s in ins] + [hbm] * len(blocks),
            out_specs=out_specs, out_shape=out_shape, scratch_shapes=scratch,
            compiler_params=_cparams(),
        )(*xs, *blocks)

    def grad_shape(s, x):
        if s.cols is not None:
            return x.shape[:-1] + (s.cols[1],)
        return x.shape

    def bwd_call(xs, save, cts, pass_cts=()):
        n_diff = len(diff_idx)
        jobs = pending.take_all() if pending is not None else []
        n_args = n_in + (1 if has_state else 0) + n_out + len(passthrough)

        def body(*refs):
            if jobs:
                src_refs = refs[n_args:n_args + len(jobs)]
                dst_refs = refs[n_args + len(jobs) + n_diff:n_args + 2 * len(jobs) + n_diff]
                sems = refs[len(refs) - 3 * len(jobs):]
                refs = refs[:n_args] + refs[n_args + len(jobs):n_args + len(jobs) + n_diff] + \
                    refs[n_args + 2 * len(jobs) + n_diff:len(refs) - 3 * len(jobs)]
                copies = lambda: [cp for k in range(len(jobs)) for cp in _direct_copies(
                    'exchange', src_refs[k], dst_refs[k], sems[3 * k], sems[3 * k + 1], sems[3 * k + 2])]

                @pl.when(step_is('first'))
                def _():
                    for cp in copies():
                        cp.start()

            in_refs = refs[:n_in]
            p = n_in
            if has_state:
                save_ref = refs[p]
                p += 1
            ct_refs = refs[p:p + n_out]
            p += n_out
            pass_refs = dict(zip(passthrough, refs[p:p + len(passthrough)]))
            p += len(passthrough)
            g_refs = refs[p:p + n_diff]
            p += n_diff
            vals = [r[...] for r in in_refs]

            def g(*dv):
                full = list(vals)
                for i, v in zip(diff_idx, dv):
                    full[i] = v
                if has_state:
                    return tuple(fn(*full, dv[-1]))
                return tuple(fn(*full))

            prim = [vals[i] for i in diff_idx]
            ct = tuple(r[...].astype(F32) for r in ct_refs)
            if has_state:
                dst_ref = refs[p]

                @pl.when(pl.program_id(seq_axis) == 0)
                def _():
                    dst_ref[...] = jnp.zeros(state_shape, F32)

                prim = prim + [save_ref[...]]
                ct = ct + (dst_ref[...],)
            _, vjp = jax.vjp(g, *prim)
            grads = vjp(ct)
            for k, i in enumerate(diff_idx):
                s = ins[i]
                if s.kind == 'blk':
                    g = grads[k] + pass_refs[i][...] if i in pass_refs else grads[k]
                    g_refs[k][...] = g.astype(g_refs[k].dtype)
                else:
                    first = None
                    for ax in s.inner:
                        c = pl.program_id(ax) == 0
                        first = c if first is None else jnp.logical_and(first, c)

                    @pl.when(first)
                    def _(k=k):
                        g_refs[k][...] = jnp.zeros_like(g_refs[k])

                    g_refs[k][...] += grads[k].astype(g_refs[k].dtype)
            if has_state:
                dst_ref[...] = grads[-1]

            if jobs:
                @pl.when(step_is('last'))
                def _():
                    for cp in copies():
                        cp.wait()

        in_specs = [in_spec(s, True) for s in ins]
        args = list(xs)
        if has_state:
            in_specs.append(rel_spec(save_block, save_imap, True))
            args.append(save)
        for o, c in zip(outs, cts):
            in_specs.append(rel_spec(o.block, o.imap, True))
            args.append(c)
        for i, c in zip(passthrough, pass_cts):
            assert ins[i].kind == 'blk' and ins[i].cols is None
            in_specs.append(rel_spec(ins[i].block, ins[i].imap, True))
            args.append(c)
        out_shape, out_specs = [], []
        for i in diff_idx:
            s = ins[i]
            out_shape.append(jax.ShapeDtypeStruct(grad_shape(s, xs[i]), xs[i].dtype))
            out_specs.append(rel_spec(s.block, s.imap, True))
        scratch = [pltpu.VMEM(state_shape, F32)] if has_state else []
        hbm = pl.BlockSpec(memory_space=pl.ANY)
        for _, slabs in jobs:
            in_specs.append(hbm)
            args.append(slabs)
            out_specs.append(hbm)
            out_shape.append(jax.ShapeDtypeStruct(slabs.shape, slabs.dtype))
            scratch += COMM_SCRATCH
        got = pl.pallas_call(
            body, name=name + "_bwd", grid=grid,
            in_specs=in_specs, out_specs=out_specs, out_shape=out_shape, scratch_shapes=scratch,
            compiler_params=_cparams(),
        )(*args)
        for (unit, _), slots in zip(jobs, got[n_diff:]):
            pending.received[unit] = slots
        return got[:n_diff]

    def results(xs, res):
        gathered = tuple(res[len(res) - n_gather:]) if n_gather else ()
        return tuple(res[:n_out]) + tuple(xs[i] for i in passthrough) + gathered

    @jax.custom_vjp
    def op(*xs):
        return results(xs, fwd_call(*xs))

    def op_fwd(*xs):
        res = fwd_call(*xs)
        return results(xs, res), (xs, res[n_out] if has_state else None)

    def op_bwd(resid, cts):
        xs, save = resid
        xs, blocks = xs[:n_in], xs[n_in:]
        grads = bwd_call(xs, save, cts[:n_out], cts[n_out:n_out + len(passthrough)])
        out = []
        k = 0
        for i, s in enumerate(ins):
            if s.kind == 'const':
                out.append(jnp.zeros_like(xs[i]))
                continue
            g = grads[k]
            k += 1
            if s.cols is not None:
                g = jnp.pad(g, ((0, 0),) * (g.ndim - 1) + ((s.cols[0], xs[i].shape[-1] - s.cols[0] - s.cols[1]),))
            out.append(g)
        return tuple(out) + tuple(jnp.zeros_like(b) for b in blocks)

    op.defvjp(op_fwd, op_bwd)
    return op


def _rms(x, w):
    return x * lax.rsqrt(jnp.mean(x * x, axis=-1, keepdims=True) + EPS) * w


def _silu(x):
    return x * jax.nn.sigmoid(x)


def rmsnorm_op(name, t, out_dtype, residual=False):
    tm = _tile(t, 512)
    return make_op(
        name, lambda x, w: (_rms(x, w),), (t // tm,),
        [In((tm, D_MODEL), lambda i: (i, 0)), In((1, D_MODEL), lambda i: (0, 0), 'acc', (0,))],
        [Out((t, D_MODEL), out_dtype, (tm, D_MODEL), lambda i: (i, 0))], passthrough=(0,) if residual else ())


def _tri(q):
    ii = lax.broadcasted_iota(jnp.int32, (q, q), 0)
    jj = lax.broadcasted_iota(jnp.int32, (q, q), 1)
    return ii >= jj, ii > jj


def _ssd_fn(z, x, bm, cm, dtr, dtb, alog, dsk, nw, state):
    q = x.shape[0]
    incl, _ = _tri(q)
    tril = incl.astype(F32)
    dt = jax.nn.softplus(dtr + dtb)
    da = dt * (-jnp.exp(alog))
    acum = cumdot(tril, da)
    acum_t = cumdot_t(tril, da)
    cb = bdot(cm, bm, NT)
    heads = range(4)
    wide = lambda a: jnp.concatenate([jnp.broadcast_to(a[:, r:r + 1], (a.shape[0], 64)) for r in heads], axis=1)
    last = acum[q - 1:q, :]
    xc = x * wide(dt)
    y = bdot(cm, state, NT) * wide(jnp.exp(acum)) + wide(dsk) * x
    ds = bdot(xc * wide(jnp.exp(last - acum)), bm, TN)
    e_last = jnp.exp(last)
    new_state = state * jnp.concatenate([jnp.broadcast_to(e_last[:, r:r + 1], (64, 1)) for r in heads], axis=0) + ds
    diag = []
    for r in heads:
        decay = jnp.exp(jnp.where(incl, acum[:, r:r + 1] - acum_t[r:r + 1, :], -jnp.inf))
        diag.append(bdot(cb * decay, xc[:, 64 * r:64 * r + 64]))
    y = y + jnp.concatenate(diag, axis=1)
    yz = y * _silu(z)
    return _rms(yz, nw), new_state


def _per_sequence(fn, n_seq_args, bl):
    def f(*args):
        *ins, state = args
        res = [fn(*[a[b] for a in ins[:n_seq_args]], *ins[n_seq_args:], state[b]) for b in range(bl)]
        return tuple(jnp.concatenate([r[k][None] for r in res]) for k in range(len(res[0])))

    return f


def ssd_op(name, bl, seq, pending=None, n_gather=0):
    q = SSD_CHUNK
    nc = seq // q
    blk = lambda w, c0, cw: In((bl, q, w), lambda g, n: (0, n, g), cols=(c0, cw))
    small = lambda g, n: (g, 0, 0)
    ins = [
        blk(256, 0, M_D_INNER),
        blk(256, 0, M_D_INNER),
        blk(128, M_D_INNER, 1024),
        blk(128, M_D_INNER + 1024, 1024),
        In((None, bl, q, 4), lambda g, n: (g, 0, n, 0)),
        In((None, 1, 4), small, 'acc', (1,)),
        In((None, 1, 4), small, 'acc', (1,)),
        In((None, 1, 4), small, 'acc', (1,)),
        In((None, 1, 256), small, 'acc', (1,)),
    ]
    outs = [Out((bl, seq, M_D_INNER), F32, (bl, q, 256), lambda g, n: (0, n, g))]
    return make_op(name, _per_sequence(_ssd_fn, 5, bl), (M_GROUPS, nc), ins, outs,
                   state_shape=(bl, 256, 128), seq_axis=1, pending=pending, n_gather=n_gather)


def _gla_fn(layer, qr, fr, ir, gr, lbp, nw, state_t):
    rows = qr.shape[0]
    c = HGRN_CHUNK
    n_chunks = rows // c
    e = jnp.exp(lbp - jnp.max(lbp, axis=0, keepdims=True))
    sm = e / jnp.sum(e, axis=0, keepdims=True)
    lb = jnp.sum(sm[1:layer + 1, :], axis=0, keepdims=True) if layer > 0 else jnp.zeros((1, lbp.shape[1]), F32)
    qq = _silu(qr) * (128 ** -0.5)
    forget = lb + (1.0 - lb) * jax.nn.sigmoid(fr)
    kk = 1.0 - forget
    logf = jnp.log(forget)
    ii = lax.broadcasted_iota(jnp.int32, (rows, rows), 0)
    jj = lax.broadcasted_iota(jnp.int32, (rows, rows), 1)
    own = jnp.logical_and(ii >= jj, ii // c == jj // c)
    gc = cumdot(own.astype(F32), logf)
    glasts = [gc[c * j + c - 1:c * j + c, :] for j in range(n_chunks)]
    glast_rows = jnp.concatenate([jnp.broadcast_to(g, (c, g.shape[1])) for g in glasts], axis=0)
    q_dec = qq * jnp.exp(gc)
    k_inv = kk * jnp.exp(-gc)
    k_end = kk * jnp.exp(glast_rows - gc)
    att = jnp.where(own, bdot(q_dec, k_inv, NT), 0.0)
    o = bdot(att, ir)
    inter = []
    for j in range(n_chunks):
        sl = slice(c * j, c * j + c)
        inter.append(bdot(q_dec[sl], state_t, NT))
        state_t = state_t * jnp.exp(glasts[j]) + bdot(ir[sl], k_end[sl], TN)
    o = o + jnp.concatenate(inter, axis=0)
    return _rms(o, nw) * _silu(gr), state_t


def gla_op(name, layer, bl, seq, pending=None, n_gather=0):
    r = HGRN_ROWS
    ns = seq // r
    blk = lambda k: In((bl, r, 128), lambda h, n: (0, n, h), cols=(1024 * k, 1024))
    ins = [blk(0), blk(1), blk(2), blk(3),
           In((DEPTH, 128), lambda h, n: (0, h), 'acc', (1,)),
           In((1, 128), lambda h, n: (0, 0), 'acc', (0, 1))]
    outs = [Out((bl, seq, D_MODEL), F32, (bl, r, 128), lambda h, n: (0, n, h))]
    return make_op(name, _per_sequence(functools.partial(_gla_fn, layer), 4, bl), (H_HEADS, ns), ins, outs,
                   state_shape=(bl, 128, 128), seq_axis=1, pending=pending, n_gather=n_gather)


def _neumann_inverse(m):
    q = m.shape[1]
    ii = lax.broadcasted_iota(jnp.int32, (q, q), 0)
    jj = lax.broadcasted_iota(jnp.int32, (q, q), 1)
    eye = (ii == jj).astype(F32)[None]
    p = -m
    inv = eye + p
    for _ in range(int(math.log2(q)) - 1):
        p = _h3(p, p, BNN)
        inv = inv + _h3(inv, p, BNN)
    return inv


@jax.custom_vjp
def _unit_lower_inverse(m):
    return _neumann_inverse(m)


def _unit_lower_inverse_fwd(m):
    inv = _neumann_inverse(m)
    return inv, inv


_unit_lower_inverse.defvjp(_unit_lower_inverse_fwd,
                           lambda inv, ct: (-_h3(_h3(inv, ct, BTN), inv, BNT),))


def _gdn_fn(qc, kc, vc, zc, br, ar, alog, dtb, nw, state):
    bl, q = qc.shape[0], qc.shape[1]
    incl, strict = _tri(q)
    tril = incl.astype(F32)
    g = jnp.concatenate([-jnp.exp(alog) * jax.nn.softplus(ar[b] + dtb) for b in range(bl)], axis=1)
    gc = cumdot(tril, g)
    gc_t = cumdot_t(tril, g)
    heads, ms, rhs = [], [], []
    for b in range(bl):
        qn = qc[b] * lax.rsqrt(jnp.sum(qc[b] * qc[b], axis=-1, keepdims=True) + EPS) * (128 ** -0.5)
        kn = kc[b] * lax.rsqrt(jnp.sum(kc[b] * kc[b], axis=-1, keepdims=True) + EPS)
        beta = jax.nn.sigmoid(br[b])
        qk = bdot(qn, kn, NT)
        for j in range(2):
            i = 2 * b + j
            col = gc[:, i:i + 1]
            decay = jnp.exp(jnp.where(incl, col - gc_t[i:i + 1, :], -jnp.inf))
            bj = beta[:, j:j + 1]
            kb = kn * bj
            ms.append(jnp.where(strict, bdot(kb, kn, NT) * decay, 0.0))
            rhs.append(jnp.concatenate([vc[b][:, 128 * j:128 * j + 128] * bj, kb * jnp.exp(col)], axis=1))
            heads.append((qn, kn, qk * decay, col, gc[q - 1:q, i:i + 1]))
    sol = h3dot_b(_unit_lower_inverse(jnp.concatenate([m[None] for m in ms])),
                  jnp.concatenate([r[None] for r in rhs]))
    outs, states = [], []
    for b in range(bl):
        os_, sts = [], []
        for j in range(2):
            i = 2 * b + j
            qn, kn, att, col, glast = heads[i]
            u = sol[i][:, :128]
            w = sol[i][:, 128:]
            st = state[b][128 * j:128 * j + 128, :]
            v_new = u - bdot(w, st)
            o = bdot(qn * jnp.exp(col), st) + bdot(att, v_new)
            sts.append(st * jnp.exp(glast) + bdot(kn * jnp.exp(glast - col), v_new, TN))
            os_.append(_rms(o, nw) * _silu(zc[b][:, 128 * j:128 * j + 128]))
        outs.append(jnp.concatenate(os_, axis=1))
        states.append(jnp.concatenate(sts, axis=0))
    return jnp.concatenate([o[None] for o in outs]), jnp.concatenate([st[None] for st in states])


def gdn_op(name, bl, seq, pending=None, n_gather=0):
    q = GDN_CHUNK
    nc = seq // q
    blk = lambda w, c0, cw: In((bl, q, w), lambda h, n: (0, n, h), cols=(c0, cw))
    small = lambda h, n: (h, 0, 0)
    ins = [
        blk(128, 0, G_KEY_DIM),
        blk(128, G_KEY_DIM, G_KEY_DIM),
        blk(256, 2 * G_KEY_DIM, G_VAL_DIM),
        blk(256, G_CONV_DIM, G_VAL_DIM),
        In((None, bl, q, 2), lambda h, n: (h, 0, n, 0)),
        In((None, bl, q, 2), lambda h, n: (h, 0, n, 0)),
        In((None, 1, 2), small, 'acc', (1,)),
        In((None, 1, 2), small, 'acc', (1,)),
        In((1, 128), lambda h, n: (0, 0), 'acc', (0, 1)),
    ]
    outs = [Out((bl, seq, G_VAL_DIM), F32, (bl, q, 256), lambda h, n: (0, n, h))]
    return make_op(name, _gdn_fn, (G_QK_HEADS, nc), ins, outs,
                   state_shape=(bl, 256, 128), seq_axis=1, pending=pending, n_gather=n_gather)


def _xattn_fn(q, k, v):
    s = bdot(q, k, NT) * (X_HEAD_DIM ** -0.5)
    s = s - jnp.max(s, axis=-1, keepdims=True)
    p = jnp.exp(s)
    p = p / jnp.sum(p, axis=-1, keepdims=True)
    return (bdot(p, v),)


def xattn_op(name, bl, seq):
    tq = _tile(seq, 2048)
    nq = seq // tq
    t = bl * seq
    ins = [
        In((tq, X_HEAD_DIM), lambda b, h, i: (b * nq + i, h)),
        In((N_MEM, X_HEAD_DIM), lambda b, h, i: (b, h), 'acc', (2,), cols=(0, D_MODEL)),
        In((N_MEM, X_HEAD_DIM), lambda b, h, i: (b, h), 'acc', (2,), cols=(D_MODEL, D_MODEL)),
    ]
    outs = [Out((t, D_MODEL), F32, (tq, X_HEAD_DIM), lambda b, h, i: (b * nq + i, h))]
    return make_op(name, _xattn_fn, (bl, X_HEADS, nq), ins, outs)


CONV_PAD = 8
CONV_ROWS = 128


def make_conv(name, bl, seq, width, ch, x_col0, up_col0=None):
    cb = 256
    rt = CONV_ROWS
    assert ch % cb == 0 and x_col0 % cb == 0 and (up_col0 is None or up_col0 % cb == 0) and seq % rt == 0
    nb = ch // cb
    n_tiles = seq // rt
    t = bl * seq
    has_up = up_col0 is not None
    grid = (nb, bl)
    x_spec = pl.BlockSpec((seq, cb), lambda c, b: (b, x_col0 // cb + c))
    up_specs = [pl.BlockSpec((seq, cb), lambda c, b: (b, up_col0 // cb + c))] if has_up else []
    w_spec = pl.BlockSpec((width, cb), lambda c, b: (0, c))
    b_spec = pl.BlockSpec((1, cb), lambda c, b: (0, c))
    o_spec = pl.BlockSpec((seq, cb), lambda c, b: (b, c))
    taps = [CONV_PAD - (width - 1) + j for j in range(width)]

    def window(x_ref, i):
        if isinstance(i, int) and i == 0:
            return jnp.concatenate([jnp.zeros((CONV_PAD, cb), F32), x_ref[0:rt, :]], axis=0)
        return x_ref[pl.ds(pl.multiple_of(i * rt - CONV_PAD, CONV_PAD), rt + CONV_PAD), :]

    def rows(i):
        return pl.ds(i * rt, rt) if isinstance(i, int) else pl.ds(pl.multiple_of(i * rt, rt), rt)

    def shifted(win):
        return [win[tp:tp + rt, :] for tp in taps]

    def pre_activation(views, w, b):
        y = b + w[0:1, :] * views[0]
        for j in range(1, width):
            y = y + w[j:j + 1, :] * views[j]
        return y

    def over_tiles(step, carry):
        carry = step(0, carry)
        return lax.fori_loop(1, n_tiles, step, carry)

    def fwd_call(x, w, b):
        def body(*refs):
            x_ref, w_ref, b_ref = refs[:3]
            o_ref = refs[-1]
            w_, b_ = w_ref[...], b_ref[...]

            def step(i, carry):
                y = _silu(pre_activation(shifted(window(x_ref, i)), w_, b_))
                if has_up:
                    y = y * refs[3][rows(i), :]
                o_ref[rows(i), :] = y
                return carry

            over_tiles(step, 0)

        return pl.pallas_call(
            body, name=name + "_fwd", grid=grid,
            in_specs=[x_spec, w_spec, b_spec] + up_specs, out_specs=o_spec,
            out_shape=jax.ShapeDtypeStruct((t, ch), F32),
            compiler_params=_cparams(),
        )(*([x, w, b] + ([x] if has_up else [])))

    def bwd_call(x, w, b, do):
        n_in = 4 + (1 if has_up else 0)

        def body(*refs):
            x_ref, w_ref, b_ref = refs[:3]
            do_ref = refs[n_in - 1]
            dx_ref, dw_ref, db_ref = refs[n_in:n_in + 3]
            gpad_ref = refs[-1]
            w_, b_ = w_ref[...], b_ref[...]

            def fold(a):
                acc = a[0:8, :]
                for k in range(1, rt // 8):
                    acc = acc + a[8 * k:8 * k + 8, :]
                return acc

            def grad_pre(i, sums):
                views = shifted(window(x_ref, i))
                y = pre_activation(views, w_, b_)
                s = jax.nn.sigmoid(y)
                act = y * s
                do_ = do_ref[rows(i), :]
                if has_up:
                    refs[n_in + 3][rows(i), :] = do_ * act
                    do_ = do_ * refs[3][rows(i), :]
                dy = do_ * (s + act * (1.0 - s))
                gpad_ref[rows(i), :] = dy
                new = [sums[j] + fold(dy * views[j]) for j in range(width)]
                return tuple(new) + (sums[width] + fold(dy),)

            zero8 = jnp.zeros((8, cb), F32)
            sums = over_tiles(grad_pre, (zero8,) * (width + 1))
            gpad_ref[seq:seq + CONV_PAD, :] = jnp.zeros((CONV_PAD, cb), F32)

            def grad_x(i, carry):
                if isinstance(i, int):
                    gwin = gpad_ref[0:rt + CONV_PAD, :]
                else:
                    gwin = gpad_ref[pl.ds(pl.multiple_of(i * rt, rt), rt + CONV_PAD), :]
                dx = w_[0:1, :] * gwin[width - 1:width - 1 + rt, :]
                for j in range(1, width):
                    dx = dx + w_[j:j + 1, :] * gwin[width - 1 - j:width - 1 - j + rt, :]
                dx_ref[rows(i), :] = dx
                return carry

            over_tiles(grad_x, 0)

            @pl.when(pl.program_id(1) == 0)
            def _():
                dw_ref[...] = jnp.zeros_like(dw_ref)
                db_ref[...] = jnp.zeros_like(db_ref)

            dw_ref[...] += jnp.concatenate([jnp.sum(sums[j], axis=0, keepdims=True) for j in range(width)], axis=0)
            db_ref[...] += jnp.sum(sums[width], axis=0, keepdims=True)

        big = jax.ShapeDtypeStruct((t, ch), F32)
        return pl.pallas_call(
            body, name=name + "_bwd", grid=grid,
            in_specs=[x_spec, w_spec, b_spec] + up_specs + [o_spec],
            out_specs=[o_spec, w_spec, b_spec] + ([o_spec] if has_up else []),
            out_shape=[big, jax.ShapeDtypeStruct((width, ch), F32), jax.ShapeDtypeStruct((1, ch), F32)]
            + ([big] if has_up else []),
            scratch_shapes=[pltpu.VMEM((seq + CONV_PAD, cb), F32)],
            compiler_params=_cparams(),
        )(*([x, w, b] + ([x] if has_up else []) + [do]))

    @jax.custom_vjp
    def conv(x, w, b):
        return fwd_call(x, w, b)

    def conv_fwd(x, w, b):
        return fwd_call(x, w, b), (x, w, b)

    def conv_bwd(res, do):
        x, w, b = res
        got = bwd_call(x, w, b, do)
        dx = jnp.pad(got[0], ((0, 0), (x_col0, x.shape[1] - x_col0 - ch)))
        if has_up:
            dx = dx + jnp.pad(got[3], ((0, 0), (up_col0, x.shape[1] - up_col0 - ch)))
        return dx, got[1], got[2]

    conv.defvjp(conv_fwd, conv_bwd)

    def apply(x, w, b=None):
        if b is None:
            b = jnp.zeros((ch,), F32)
        return conv(x, w, b.reshape(1, ch))

    return apply


def loss_head(x, w, target):
    t = x.shape[0]
    tm = _tile(t, 512)

    def fn(xb, wb, tb):
        err = _rms(xb, wb) - tb
        return 0.5 * jnp.sum(err * err) * (1.0 / D_MODEL)

    def body(x_ref, w_ref, t_ref, loss_ref, dx_ref, dw_ref):
        @pl.when(pl.program_id(0) == 0)
        def _():
            loss_ref[...] = jnp.zeros_like(loss_ref)
            dw_ref[...] = jnp.zeros_like(dw_ref)

        tb = t_ref[...]
        val, vjp = jax.vjp(lambda a, b: fn(a, b, tb), x_ref[...], w_ref[...])
        dx, dw = vjp(jnp.ones((), F32))
        dx_ref[...] = dx
        dw_ref[...] += dw
        loss_ref[...] += jnp.full(loss_ref.shape, val, F32)

    row = pl.BlockSpec((tm, D_MODEL), lambda i: (i, 0))
    vec = pl.BlockSpec((1, D_MODEL), lambda i: (0, 0))
    loss, dx, dw = pl.pallas_call(
        body, name="loss_head", grid=(t // tm,),
        in_specs=[row, vec, row],
        out_specs=[pl.BlockSpec((8, LANE), lambda i: (0, 0)), row, vec],
        out_shape=[jax.ShapeDtypeStruct((8, LANE), F32), jax.ShapeDtypeStruct((t, D_MODEL), F32),
                   jax.ShapeDtypeStruct((1, D_MODEL), F32)],
        compiler_params=_cparams(),
    )(x, w.reshape(1, D_MODEL), target)
    return loss[0, 0], dx, dw.reshape(D_MODEL)


PACK_W = 1024
ADAM_BLOCK_BYTES = 512 * 1024


def _rows_tile(r, c):
    if r * c * 4 <= ADAM_BLOCK_BYTES or r % 8:
        return r
    best = 8
    for t in range(8, r + 1, 8):
        if r % t == 0 and t * c * 4 <= ADAM_BLOCK_BYTES:
            best = t
    return best


def reduce_adamw(slots, w, m, v, name):
    r, wd = w.shape
    tr = _rows_tile(r, wd)
    c1 = 1.0 - ADAM_B1 ** ADAM_STEP
    c2 = 1.0 - ADAM_B2 ** ADAM_STEP

    def body(s_ref, w_ref, m_ref, v_ref, g_ref, d_ref, nm_ref, nv_ref):
        g = s_ref[0].astype(F32)
        for k in range(1, N_DEV):
            g = g + s_ref[k].astype(F32)
        nm = ADAM_B1 * m_ref[...] + (1.0 - ADAM_B1) * g
        nv = ADAM_B2 * v_ref[...] + (1.0 - ADAM_B2) * (g * g)
        m_hat = nm / c1
        v_hat = nv / c2
        d_ref[...] = -ADAM_LR * (m_hat / (jnp.sqrt(v_hat) + ADAM_EPS) + ADAM_WD * w_ref[...])
        g_ref[...] = g
        nm_ref[...] = nm
        nv_ref[...] = nv

    blk = pl.BlockSpec((tr, wd), lambda i: (i, 0))
    shp = jax.ShapeDtypeStruct((r, wd), F32)
    return pl.pallas_call(
        body, name=name, grid=(r // tr,),
        in_specs=[pl.BlockSpec((N_DEV, tr, wd), lambda i: (0, i, 0)), blk, blk, blk],
        out_specs=[blk, blk, blk, blk], out_shape=[shp, shp, shp, shp],
        compiler_params=_cparams(),
    )(slots, w, m, v)


def all_gather(block, name):
    def body(x_ref, out_ref, send_sems, recv_sems, local_sem):
        x, y, c = _position()
        me, sibling = (x, y, c), (x, y, 1 - c)
        chips = [(1 - x, y), (x, 1 - y), (1 - x, 1 - y)]

        def slot(px, py, pc):
            return out_ref.at[4 * px + 2 * py + pc]

        def copy(k, owner, to, src=None):
            return pltpu.make_async_remote_copy(
                src_ref=slot(*owner) if src is None else src, dst_ref=slot(*owner),
                send_sem=send_sems.at[k], recv_sem=recv_sems.at[k],
                device_id=to, device_id_type=pl.DeviceIdType.MESH)

        mine = pltpu.make_async_copy(x_ref, slot(*me), local_sem)
        mine.start()
        first = [copy(0, me, sibling, src=x_ref)]
        first += [copy(1 + j, me, (*chip, c), src=x_ref) for j, chip in enumerate(chips)]
        for cp in first:
            cp.start()
        passed = [copy(4 + j, (*chip, c), sibling) for j, chip in enumerate(chips)]
        for j, chip in enumerate(chips):
            copy(1 + j, (*chip, c), me).wait_recv()
            passed[j].start()
        copy(0, sibling, me).wait_recv()
        for j, chip in enumerate(chips):
            copy(4 + j, (*chip, 1 - c), me).wait_recv()
        for cp in first + passed:
            cp.wait_send()
        mine.wait()

    return pl.pallas_call(
        body, name=name,
        out_shape=jax.ShapeDtypeStruct((N_DEV,) + block.shape, block.dtype),
        in_specs=[pl.BlockSpec(memory_space=pl.ANY)],
        out_specs=pl.BlockSpec(memory_space=pl.ANY),
        scratch_shapes=[pltpu.SemaphoreType.DMA((7,)), pltpu.SemaphoreType.DMA((7,)), pltpu.SemaphoreType.DMA],
    )(block)


def exchange_slabs(slabs, name):
    def body(in_ref, out_ref, send_sems, recv_sems, local_sem):
        x, y, c = _position()
        my = 4 * x + 2 * y + c
        mine = pltpu.make_async_copy(in_ref.at[my], out_ref.at[my], local_sem)
        mine.start()
        copies = []
        for k in range(1, N_DEV):
            dx, dy, dc = (k >> 2) & 1, (k >> 1) & 1, k & 1
            px = x if dx == 0 else 1 - x
            py = y if dy == 0 else 1 - y
            pc = c if dc == 0 else 1 - c
            cp = pltpu.make_async_remote_copy(
                src_ref=in_ref.at[4 * px + 2 * py + pc], dst_ref=out_ref.at[my],
                send_sem=send_sems.at[k - 1], recv_sem=recv_sems.at[k - 1],
                device_id=(px, py, pc), device_id_type=pl.DeviceIdType.MESH)
            cp.start()
            copies.append(cp)
        for cp in copies:
            cp.wait()
        mine.wait()

    return pl.pallas_call(
        body, name=name,
        out_shape=jax.ShapeDtypeStruct(slabs.shape, slabs.dtype),
        in_specs=[pl.BlockSpec(memory_space=pl.ANY)],
        out_specs=pl.BlockSpec(memory_space=pl.ANY),
        scratch_shapes=[pltpu.SemaphoreType.DMA((7,)), pltpu.SemaphoreType.DMA((7,)), pltpu.SemaphoreType.DMA],
    )(slabs)


def _pack(arrays, dtype, row_multiple):
    flat = jnp.concatenate([a.astype(dtype).reshape(-1) for a in arrays])
    n = flat.shape[0]
    per = PACK_W * row_multiple
    total = -(-n // per) * per
    flat = jnp.pad(flat, (0, total - n))
    return flat.reshape(total // PACK_W, PACK_W)


def _unpack(flat2d, shapes, lead=()):
    flat = flat2d.reshape(lead + (-1,))
    out, off = [], 0
    for shp in shapes:
        n = math.prod(shp)
        out.append(flat[..., off:off + n].reshape(lead + tuple(shp)))
        off += n
    return out


def _full_from_gathered(g, axis):
    g = jnp.moveaxis(g, 0, axis)
    shp = list(g.shape)
    shp[axis:axis + 2] = [shp[axis] * shp[axis + 1]]
    return g.reshape(shp)


def _shards_of_full(full, axis):
    shp = list(full.shape)
    shp[axis:axis + 1] = [N_DEV, shp[axis] // N_DEV]
    return jnp.moveaxis(full.reshape(shp), axis, 0)


def layer_units(i):
    mixer = [('m_in_w', 'm_out_w'), ('h_in_w', 'h_out_w'), ('g_in_w', 'g_out_w')][i % 3]
    return [(mixer[0], i // 3), (mixer[1], i // 3), ('xa_q', i), ('xa_kv', i), ('xa_o', i), ('f_up', i), ('f_down', i)]


PADDED_COLS = {'m_in_w': M_IN_PAD, 'g_in_w': G_IN_PAD}
BIG_WEIGHTS = ('m_in_w', 'h_in_w', 'g_in_w', 'f_up')


def whole_weight(name, gathered):
    w = _full_from_gathered(lax.stop_gradient(gathered), SHARD_AXIS[name] - 1)
    return _pad_cols(w, 1, PADDED_COLS[name]) if name in PADDED_COLS else w


def _trunk(p, weights, blocks, standins, x, mem, bl, seq, pending=None):
    t = bl * seq
    ia = ib = ic = 0
    weights = dict(weights)
    state = {}

    def lin(name, a, wname, idx, residual=None):
        unit = (wname, idx)
        pos = state['units'].index(unit)
        later = state['next'][pos] if state['next'] else None
        if wname == 'f_up' and later in state['by_core']:
            later = ('xa_kv', i + 1)
        elif wname == 'xa_kv' and ('f_up', i + 1) in state['by_core'] and later not in state['by_core']:
            later = None
        nxt = (blocks[later],) if later in blocks and later not in state['by_core'] else ()
        n_real = N_DEV * standins[unit].shape[2]
        res = () if residual is None else (residual,)
        mixer_in = wname in ('m_in_w', 'h_in_w', 'g_in_w')
        defer = (pending, unit) if pending is not None and not (mixer_in and i == 0) else None
        y, got = make_linear(name, SHARD_AXIS[wname] - 1, n_real, bool(res), defer)(
            a, weights[unit], standins[unit], nxt, res)
        if nxt:
            weights[later] = whole_weight(later[0], got[0])
        return y

    by_seq = lambda a: a.reshape(bl, seq, a.shape[-1])

    for i in range(DEPTH):
        state['units'] = layer_units(i)
        state['next'] = layer_units(i + 1) if i + 1 < DEPTH else None
        state['by_core'] = [u for u in (state['next'] or [])
                            if u in blocks and (u[0] in BIG_WEIGHTS[:3 if i % 3 == 1 else 4] or i % 3 == 2)]
        core_blocks = [blocks[u] for u in state['by_core']]

        def core(op, *args):
            y, *got = op(*args, *core_blocks)
            for u, g in zip(state['by_core'], got):
                weights[u] = whole_weight(u[0], g)
            return y
        hn, x = rmsnorm_op(f"ln_mix{i}", t, F32, residual=True)(x, p['ln_mix'][i:i + 1])
        kind = i % 3
        if kind == 0:
            proj = lin(f"m_in{i}", hn, 'm_in_w', ia)
            xbc = make_conv(f"m_conv{i}", bl, seq, 4, M_CONV_DIM, M_D_INNER)(
                proj, p['m_conv_w'][ia], p['m_conv_b'][ia])
            dt = proj[:, M_D_INNER + M_CONV_DIM:M_IN].reshape(bl, seq, M_GROUPS, 4).transpose(2, 0, 1, 3)
            grp = lambda a, n=4: a.reshape(M_GROUPS, 1, n)
            proj3, xbc3 = by_seq(proj), by_seq(xbc)
            y = core(ssd_op(f"ssd{i}", bl, seq, pending, len(core_blocks)),
                     proj3, xbc3, xbc3, xbc3, dt, grp(p['m_dt_bias'][ia]), grp(p['m_a_log'][ia]), grp(p['m_d'][ia]),
                     grp(p['m_norm_w'][ia], 256))
            x = lin(f"m_out{i}", y.reshape(t, M_D_INNER), 'm_out_w', ia, residual=x)
            ia += 1
        elif kind == 1:
            proj3 = by_seq(lin(f"h_in{i}", hn, 'h_in_w', ib))
            y = core(gla_op(f"gla{i}", i, bl, seq, pending, len(core_blocks)),
                     proj3, proj3, proj3, proj3, p['h_lower_bounds'], p['h_norm_w'][ib:ib + 1])
            x = lin(f"h_out{i}", y.reshape(t, D_MODEL), 'h_out_w', ib, residual=x)
            ib += 1
        else:
            proj = lin(f"g_in{i}", hn, 'g_in_w', ic)
            qkv = make_conv(f"g_conv{i}", bl, seq, 4, G_CONV_DIM, 0)(proj, p['g_conv_w'][ic])
            c0 = G_CONV_DIM + G_VAL_DIM
            heads = lambda a: a.reshape(bl, seq, G_QK_HEADS, 2).transpose(2, 0, 1, 3)
            braw = heads(proj[:, c0:c0 + G_V_HEADS])
            araw = heads(proj[:, c0 + G_V_HEADS:c0 + 2 * G_V_HEADS])
            grp = lambda a: a.reshape(G_QK_HEADS, 1, 2)
            qkv3 = by_seq(qkv)
            y = core(gdn_op(f"gdn{i}", bl, seq, pending, len(core_blocks)),
                     qkv3, qkv3, qkv3, by_seq(proj), braw, araw, grp(p['g_a_log'][ic]), grp(p['g_dt_bias'][ic]),
                     p['g_norm_w'][ic:ic + 1])
            x = lin(f"g_out{i}", y.reshape(t, G_VAL_DIM), 'g_out_w', ic, residual=x)
            ic += 1
        hq, x = rmsnorm_op(f"ln_xattn{i}", t, F32, residual=True)(x, p['ln_xattn'][i:i + 1])
        mn = rmsnorm_op(f"ln_mem{i}", bl * N_MEM, F32)(mem, p['ln_mem'][i:i + 1])[0]
        qx = lin(f"xa_q{i}", hq, 'xa_q', i)
        kv = lin(f"xa_kv{i}", mn, 'xa_kv', i)
        ao = xattn_op(f"xattn{i}", bl, seq)(qx, kv, kv)[0]
        x = lin(f"xa_o{i}", ao, 'xa_o', i, residual=x)
        hf, x = rmsnorm_op(f"ln_ffn{i}", t, F32, residual=True)(x, p['ln_ffn'][i:i + 1])
        up = lin(f"f_up{i}", hf, 'f_up', i)
        act = make_conv(f"f_conv{i}", bl, seq, 3, D_FF, 0, up_col0=D_FF)(up, p['f_conv_w'][i], p['f_conv_b'][i])
        x = lin(f"f_down{i}", act, 'f_down', i, residual=x)
    return x


def _pad_cols(w, axis, to):
    pad = [(0, 0)] * w.ndim
    pad[axis] = (0, to - w.shape[axis])
    return jnp.pad(w, pad)


def kernel(x, mem, ln_mix, ln_xattn, ln_mem, ln_ffn, final_norm, m_in_w, m_conv_w, m_conv_b, m_dt_bias, m_a_log, m_d, m_norm_w, m_out_w, h_in_w, h_lower_bounds, h_norm_w, h_out_w, g_in_w, g_conv_w, g_a_log, g_dt_bias, g_norm_w, g_out_w, xa_q, xa_kv, xa_o, f_up, f_conv_w, f_conv_b, f_down, loss_target, m_ln_mix, m_ln_xattn, m_ln_mem, m_ln_ffn, m_final_norm, m_m_in_w, m_m_conv_w, m_m_conv_b, m_m_dt_bias, m_m_a_log, m_m_d, m_m_norm_w, m_m_out_w, m_h_in_w, m_h_lower_bounds, m_h_norm_w, m_h_out_w, m_g_in_w, m_g_conv_w, m_g_a_log, m_g_dt_bias, m_g_norm_w, m_g_out_w, m_xa_q, m_xa_kv, m_xa_o, m_f_up, m_f_conv_w, m_f_conv_b, m_f_down, v_ln_mix, v_ln_xattn, v_ln_mem, v_ln_ffn, v_final_norm, v_m_in_w, v_m_conv_w, v_m_conv_b, v_m_dt_bias, v_m_a_log, v_m_d, v_m_norm_w, v_m_out_w, v_h_in_w, v_h_lower_bounds, v_h_norm_w, v_h_out_w, v_g_in_w, v_g_conv_w, v_g_a_log, v_g_dt_bias, v_g_norm_w, v_g_out_w, v_xa_q, v_xa_kv, v_xa_o, v_f_up, v_f_conv_w, v_f_conv_b, v_f_down):
    local = dict(ln_mix=ln_mix, ln_xattn=ln_xattn, ln_mem=ln_mem, ln_ffn=ln_ffn, final_norm=final_norm, m_in_w=m_in_w, m_conv_w=m_conv_w, m_conv_b=m_conv_b, m_dt_bias=m_dt_bias, m_a_log=m_a_log, m_d=m_d, m_norm_w=m_norm_w, m_out_w=m_out_w, h_in_w=h_in_w, h_lower_bounds=h_lower_bounds, h_norm_w=h_norm_w, h_out_w=h_out_w, g_in_w=g_in_w, g_conv_w=g_conv_w, g_a_log=g_a_log, g_dt_bias=g_dt_bias, g_norm_w=g_norm_w, g_out_w=g_out_w, xa_q=xa_q, xa_kv=xa_kv, xa_o=xa_o, f_up=f_up, f_conv_w=f_conv_w, f_conv_b=f_conv_b, f_down=f_down)
    mom_m = dict(ln_mix=m_ln_mix, ln_xattn=m_ln_xattn, ln_mem=m_ln_mem, ln_ffn=m_ln_ffn, final_norm=m_final_norm, m_in_w=m_m_in_w, m_conv_w=m_m_conv_w, m_conv_b=m_m_conv_b, m_dt_bias=m_m_dt_bias, m_a_log=m_m_a_log, m_d=m_m_d, m_norm_w=m_m_norm_w, m_out_w=m_m_out_w, h_in_w=m_h_in_w, h_lower_bounds=m_h_lower_bounds, h_norm_w=m_h_norm_w, h_out_w=m_h_out_w, g_in_w=m_g_in_w, g_conv_w=m_g_conv_w, g_a_log=m_g_a_log, g_dt_bias=m_g_dt_bias, g_norm_w=m_g_norm_w, g_out_w=m_g_out_w, xa_q=m_xa_q, xa_kv=m_xa_kv, xa_o=m_xa_o, f_up=m_f_up, f_conv_w=m_f_conv_w, f_conv_b=m_f_conv_b, f_down=m_f_down)
    mom_v = dict(ln_mix=v_ln_mix, ln_xattn=v_ln_xattn, ln_mem=v_ln_mem, ln_ffn=v_ln_ffn, final_norm=v_final_norm, m_in_w=v_m_in_w, m_conv_w=v_m_conv_w, m_conv_b=v_m_conv_b, m_dt_bias=v_m_dt_bias, m_a_log=v_m_a_log, m_d=v_m_d, m_norm_w=v_m_norm_w, m_out_w=v_m_out_w, h_in_w=v_h_in_w, h_lower_bounds=v_h_lower_bounds, h_norm_w=v_h_norm_w, h_out_w=v_h_out_w, g_in_w=v_g_in_w, g_conv_w=v_g_conv_w, g_a_log=v_g_a_log, g_dt_bias=v_g_dt_bias, g_norm_w=v_g_norm_w, g_out_w=v_g_out_w, xa_q=v_xa_q, xa_kv=v_xa_kv, xa_o=v_xa_o, f_up=v_f_up, f_conv_w=v_f_conv_w, f_conv_b=v_f_conv_b, f_down=v_f_down)

    bl, seq, _ = x.shape
    t = bl * seq

    p = {n: local[n] for n in WEIGHTS if n not in SHARD_AXIS}
    for n in SMALL_SHARDED:
        p[n] = _full_from_gathered(all_gather(local[n], f"gather_{n}"), SHARD_AXIS[n])
    units = [(n, l) for n in MATMUL_WEIGHTS for l in range(local[n].shape[0])]
    block = lambda u: local[u[0]][u[1]].astype(BF16)
    weights = {u: whole_weight(u[0], all_gather(block(u), f"gather_{u[0]}{u[1]}")) for u in layer_units(0)}
    blocks = {u: block(u) for u in units if u not in weights}
    standins = {u: jnp.zeros((N_DEV,) + local[u[0]].shape[1:], BF16) for u in units}
    small = {n: p[n] for n in WEIGHTS if n not in MATMUL_WEIGHTS and n != 'final_norm'}

    pending = Pending()

    def run(small_w, standins_, xin):
        return _trunk(small_w, weights, blocks, standins_, xin, mem.reshape(bl * N_MEM, D_MODEL), bl, seq, pending)

    x_out, vjp = jax.vjp(run, small, standins, x.reshape(t, D_MODEL))
    loss_part, dx_out, d_final = loss_head(x_out, final_norm, loss_target.reshape(t, D_MODEL))
    grads, received, dx = vjp(dx_out)
    received = dict(received)
    for unit, slabs in pending.take_all():
        pending.received[unit] = exchange_slabs(slabs, f"exchange_{unit[0]}{unit[1]}")
    received.update(pending.received)
    grads = dict(grads)
    grads['final_norm'] = d_final
    loss = lax.psum(loss_part, ("x", "y", "c"))

    outs = {}

    def update(name, n, slots, shape, sel=lambda a: a):
        two_d = lambda a: sel(a).reshape(slots.shape[1:])
        got = reduce_adamw(slots, two_d(local[n]), two_d(mom_m[n]), two_d(mom_v[n]), name)
        return [g.reshape(shape) for g in got]

    for n in SMALL_SHARDED:
        slots = exchange_slabs(_shards_of_full(grads[n], SHARD_AXIS[n]), f"exchange_{n}")
        slots = slots.reshape(N_DEV, -1, slots.shape[-1])
        for kind, a in zip(KINDS, update(f"adamw_{n}", n, slots, local[n].shape)):
            outs[kind, n] = a
    for n in MATMUL_WEIGHTS:
        per_layer = [update(f"adamw_{n}{l}", n, received[n, l], local[n].shape[1:], lambda a, l=l: a[l])
                     for l in range(local[n].shape[0])]
        for k, kind in enumerate(KINDS):
            outs[kind, n] = jnp.concatenate([got[k][None] for got in per_layer])
    replicated = [n for n in WEIGHTS if n not in SHARD_AXIS]
    pk = lambda d: _pack([d[n] for n in replicated], F32, 8)
    got = reduce_adamw(all_gather(pk(grads), "gather_replicated_grads"), pk(local), pk(mom_m), pk(mom_v),
                       "adamw_replicated")
    shapes = [local[n].shape for n in replicated]
    for kind, buf in zip(KINDS, got):
        for n, a in zip(replicated, _unpack(buf, shapes)):
            outs[kind, n] = a
    result = [loss, dx.reshape(bl, seq, D_MODEL)]
    for kind in KINDS:
        result += [outs[kind, n] for n in WEIGHTS]
    return tuple(result)
```

```python
import functools
import math

import jax
import jax.numpy as jnp
from jax import lax
from jax.experimental import pallas as pl
from jax.experimental.pallas import tpu as pltpu

F32 = jnp.float32
BF16 = jnp.bfloat16
NN = (((1,), (0,)), ((), ()))
NT = (((1,), (1,)), ((), ()))
TN = (((0,), (0,)), ((), ()))

D_MODEL = 1024
DEPTH = 4
EPS = 1e-6
N_MEM = 256
M_D_INNER = 2048
M_HEADS = 32
M_GROUPS = 8
M_STATE = 128
M_CONV_DIM = 4096
M_IN = 6176
M_IN_PAD = 6272
SSD_CHUNK = 256
H_HEADS = 8
HGRN_CHUNK = 32
HGRN_ROWS = 256
G_QK_HEADS = 8
G_V_HEADS = 16
G_KEY_DIM = 1024
G_VAL_DIM = 2048
G_CONV_DIM = 4096
G_IN = 6176
G_IN_PAD = 6272
GDN_CHUNK = 64
X_HEADS = 4
X_HEAD_DIM = 256
D_FF = 2816
ADAM_LR = 0.001
ADAM_B1 = 0.9
ADAM_B2 = 0.999
ADAM_EPS = 1e-08
ADAM_WD = 0.01
ADAM_STEP = 10

N_DEV = 8
LANE = 128
KINDS = ('grad', 'delta', 'new_m', 'new_v')
VMEM_LIMIT = 56 * 1024 * 1024

WEIGHTS = ['ln_mix', 'ln_xattn', 'ln_mem', 'ln_ffn', 'final_norm', 'm_in_w', 'm_conv_w', 'm_conv_b', 'm_dt_bias',
           'm_a_log', 'm_d', 'm_norm_w', 'm_out_w', 'h_in_w', 'h_lower_bounds', 'h_norm_w', 'h_out_w', 'g_in_w',
           'g_conv_w', 'g_a_log', 'g_dt_bias', 'g_norm_w', 'g_out_w', 'xa_q', 'xa_kv', 'xa_o', 'f_up', 'f_conv_w',
           'f_conv_b', 'f_down']
SHARD_AXIS = {'m_in_w': 2, 'm_conv_w': 2, 'm_conv_b': 1, 'm_norm_w': 1, 'm_out_w': 1, 'h_in_w': 2, 'h_out_w': 1,
              'g_in_w': 2, 'g_conv_w': 2, 'g_out_w': 1, 'xa_q': 1, 'xa_kv': 2, 'xa_o': 1, 'f_up': 2, 'f_conv_w': 2,
              'f_down': 1}
MATMUL_WEIGHTS = ['m_in_w', 'm_out_w', 'h_in_w', 'h_out_w', 'g_in_w', 'g_out_w', 'xa_q', 'xa_kv', 'xa_o', 'f_up',
                  'f_down']
SMALL_SHARDED = ['m_conv_w', 'm_conv_b', 'm_norm_w', 'g_conv_w', 'f_conv_w']


def _cparams():
    return pltpu.CompilerParams(vmem_limit_bytes=VMEM_LIMIT)


def bdot(a, b, dims=NN):
    return lax.dot_general(a.astype(BF16), b.astype(BF16), dims, preferred_element_type=F32)


def _split(a):
    hi = a.astype(BF16)
    return hi, (a - hi.astype(F32)).astype(BF16)


def _h3(a, b, dims):
    ah, al = _split(a)
    bh, bl = _split(b)
    d = functools.partial(lax.dot_general, dimension_numbers=dims, preferred_element_type=F32)
    return d(ah, bh) + (d(ah, bl) + d(al, bh))


BNN = (((2,), (1,)), ((0,), (0,)))
BNT = (((2,), (2,)), ((0,), (0,)))
BTN = (((1,), (1,)), ((0,), (0,)))


@jax.custom_vjp
def h3dot_b(a, b):
    return _h3(a, b, BNN)


h3dot_b.defvjp(lambda a, b: (_h3(a, b, BNN), (a, b)),
               lambda res, ct: (_h3(ct, res[1], BNT), _h3(res[0], ct, BTN)))

T_ROWS = (((0,), (1,)), ((), ()))


def _tri_times(tri, x, dims, tri_first):
    t = tri.astype(BF16)
    x0 = x.astype(BF16)
    r1 = x - x0.astype(F32)
    x1 = r1.astype(BF16)
    x2 = (r1 - x1.astype(F32)).astype(BF16)
    if tri_first:
        d = lambda xx: lax.dot_general(t, xx, dims, preferred_element_type=F32)
    else:
        d = lambda xx: lax.dot_general(xx, t, dims, preferred_element_type=F32)
    return d(x0) + (d(x1) + d(x2))


@jax.custom_vjp
def cumdot(tri, x):
    return _tri_times(tri, x, NN, True)


cumdot.defvjp(lambda tri, x: (_tri_times(tri, x, NN, True), tri),
              lambda tri, ct: (jnp.zeros_like(tri), _tri_times(tri, ct, TN, True)))


@jax.custom_vjp
def cumdot_t(tri, x):
    return _tri_times(tri, x, T_ROWS, False)


cumdot_t.defvjp(lambda tri, x: (_tri_times(tri, x, T_ROWS, False), tri),
                lambda tri, ct: (jnp.zeros_like(tri), _tri_times(tri, ct, T_ROWS, True)))


def _tile(dim, cap):
    if dim <= cap:
        return dim
    best = None
    for t in range(LANE, cap + 1, LANE):
        if dim % t == 0:
            best = t
    assert best is not None, dim
    return best


def _position():
    return lax.axis_index("x"), lax.axis_index("y"), lax.axis_index("c")


def _direct_copies(kind, src_ref, dst_ref, send_sems, recv_sems, local_sem):
    x, y, c = _position()
    me = 4 * x + 2 * y + c
    local_src = src_ref if kind == 'gather' else src_ref.at[me]
    copies = [pltpu.make_async_copy(local_src, dst_ref.at[me], local_sem)]
    for k in range(1, N_DEV):
        px = 1 - x if (k >> 2) & 1 else x
        py = 1 - y if (k >> 1) & 1 else y
        pc = 1 - c if k & 1 else c
        copies.append(pltpu.make_async_remote_copy(
            src_ref=src_ref if kind == 'gather' else src_ref.at[4 * px + 2 * py + pc], dst_ref=dst_ref.at[me],
            send_sem=send_sems.at[k - 1], recv_sem=recv_sems.at[k - 1],
            device_id=(px, py, pc), device_id_type=pl.DeviceIdType.MESH))
    return copies


COMM_SCRATCH = [pltpu.SemaphoreType.DMA((N_DEV - 1,)), pltpu.SemaphoreType.DMA((N_DEV - 1,)), pltpu.SemaphoreType.DMA]


class Pending:
    def __init__(self):
        self.jobs, self.received = [], {}

    def take_all(self):
        jobs, self.jobs = self.jobs, []
        return jobs


def matmul(a, b, *, ta=False, tb=False, out_dtype=F32, name="mm", carry=None, residual=None):
    if ta:
        k, m = a.shape
    else:
        m, k = a.shape
    if tb:
        n, k2 = b.shape
    else:
        k2, n = b.shape
    assert k == k2, (a.shape, b.shape, ta, tb)
    tm = _tile(m, 1408)
    tn = _tile(n, 1408)
    tk = _tile(k, 1408)
    grid = (m // tm, n // tn, k // tk)
    nk = grid[2]
    dims = (((0 if ta else 1,), (1 if tb else 0,)), ((), ()))

    def at_step(which):
        conds = [pl.program_id(ax) == (0 if which == 'first' else grid[ax] - 1) for ax in range(3)]
        return jnp.logical_and(jnp.logical_and(conds[0], conds[1]), conds[2])

    def body(*refs):
        r_ref = None
        if residual is not None:
            r_ref, refs = refs[2], refs[:2] + refs[3:]
        if carry is None:
            a_ref, b_ref, o_ref, acc_ref = refs
        else:
            a_ref, b_ref, src_ref, o_ref, dst_ref, acc_ref, send_sems, recv_sems, local_sem = refs
            copies = lambda: _direct_copies(carry[0], src_ref, dst_ref, send_sems, recv_sems, local_sem)

            @pl.when(at_step('first'))
            def _():
                for cp in copies():
                    cp.start()

        @pl.when(pl.program_id(2) == 0)
        def _():
            acc_ref[...] = jnp.zeros_like(acc_ref)

        acc_ref[...] += lax.dot_general(a_ref[...].astype(BF16), b_ref[...].astype(BF16), dims,
                                        preferred_element_type=F32)

        @pl.when(pl.program_id(2) == nk - 1)
        def _():
            out = acc_ref[...] if r_ref is None else acc_ref[...] + r_ref[...]
            o_ref[...] = out.astype(o_ref.dtype)

        if carry is not None:
            @pl.when(at_step('last'))
            def _():
                for cp in copies():
                    cp.wait()

    a_spec = pl.BlockSpec((tk, tm), lambda i, j, kk: (kk, i)) if ta else pl.BlockSpec((tm, tk), lambda i, j, kk: (i, kk))
    b_spec = pl.BlockSpec((tn, tk), lambda i, j, kk: (j, kk)) if tb else pl.BlockSpec((tk, tn), lambda i, j, kk: (kk, j))
    o_spec = pl.BlockSpec((tm, tn), lambda i, j, kk: (i, j))
    o_shape = jax.ShapeDtypeStruct((m, n), out_dtype)
    acc = pltpu.VMEM((tm, tn), F32)
    ins, in_specs = [a, b], [a_spec, b_spec]
    if residual is not None:
        ins.append(residual)
        in_specs.append(o_spec)
    if carry is None:
        return pl.pallas_call(
            body, name=name, grid=grid, in_specs=in_specs, out_specs=o_spec, out_shape=o_shape,
            scratch_shapes=[acc], compiler_params=_cparams(),
        )(*ins)
    kind, src = carry
    got = jax.ShapeDtypeStruct(((N_DEV,) + src.shape) if kind == 'gather' else src.shape, src.dtype)
    hbm = pl.BlockSpec(memory_space=pl.ANY)
    return pl.pallas_call(
        body, name=name, grid=grid, in_specs=in_specs + [hbm], out_specs=[o_spec, hbm],
        out_shape=[o_shape, got], scratch_shapes=[acc] + COMM_SCRATCH, compiler_params=_cparams(),
    )(*ins, src)


def make_linear(name, shard_axis, n_real, has_res=False, defer=None):
    def forward(a, w, nxt, res):
        r = res[0] if res else None
        if nxt:
            y, got = matmul(a, w, name=name + "_fwd", carry=('gather', nxt[0]), residual=r)
            return y, (got,)
        return matmul(a, w, name=name + "_fwd", residual=r), ()

    @jax.custom_vjp
    def linear(a, w, wg, nxt, res):
        return forward(a, w, nxt, res)

    def fwd(a, w, wg, nxt, res):
        return forward(a, w, nxt, res), (a, w, nxt)

    def bwd(saved, cts):
        a, w, nxt = saved
        dy = cts[0]
        dw = matmul(a, dy, ta=True, out_dtype=BF16, name=name + "_bwd_dw")
        slabs = _shards_of_full(dw[:, :n_real], shard_axis)
        if defer is None:
            da, slots = matmul(dy, w, tb=True, out_dtype=a.dtype, name=name + "_bwd_da", carry=('exchange', slabs))
        else:
            defer[0].jobs.append((defer[1], slabs))
            da, slots = matmul(dy, w, tb=True, out_dtype=a.dtype, name=name + "_bwd_da"), jnp.zeros_like(slabs)
        return da, jnp.zeros_like(w), slots, tuple(jnp.zeros_like(b) for b in nxt), ((dy,) if has_res else ())

    linear.defvjp(fwd, bwd)
    return linear


class In:
    def __init__(self, block, imap, kind='blk', inner=(), cols=None):
        self.block, self.imap, self.kind, self.inner, self.cols = block, imap, kind, inner, cols


class Out:
    def __init__(self, shape, dtype, block, imap):
        self.shape, self.dtype, self.block, self.imap = shape, dtype, block, imap


def make_op(name, fn, grid, ins, outs, state_shape=None, seq_axis=None, passthrough=(), pending=None, n_gather=0):
    n_in, n_out = len(ins), len(outs)
    has_state = state_shape is not None
    nd = len(grid)
    diff_idx = [i for i, s in enumerate(ins) if s.kind != 'const']

    def in_spec(s, reverse):
        off = 0
        if s.cols is not None:
            assert s.cols[0] % s.block[-1] == 0
            off = s.cols[0] // s.block[-1]

        def imap(*ids):
            ids = rev(ids) if reverse else ids
            idx = tuple(s.imap(*ids))
            return idx[:-1] + (idx[-1] + off,) if off else idx

        return pl.BlockSpec(s.block, imap)

    def rel_spec(block, f, reverse):
        return pl.BlockSpec(block, (lambda *ids: f(*rev(ids))) if reverse else f)

    def rev(ids):
        if not has_state:
            return ids
        ids = list(ids)
        ids[seq_axis] = grid[seq_axis] - 1 - ids[seq_axis]
        return tuple(ids)

    save_shape = tuple(grid) + tuple(state_shape) if has_state else None
    save_block = (None,) * nd + tuple(state_shape) if has_state else None

    def save_imap(*ids):
        return tuple(ids) + (0,) * len(state_shape)

    def step_is(which):
        conds = [pl.program_id(ax) == (0 if which == 'first' else grid[ax] - 1) for ax in range(nd)]
        return functools.reduce(jnp.logical_and, conds)

    def fwd_call(*xs):
        xs, blocks = xs[:n_in], xs[n_in:]
        n_save = 1 if has_state else 0

        def body(*refs):
            if blocks:
                src_refs = refs[n_in:n_in + n_gather]
                dst_refs = refs[n_in + n_gather + n_out + n_save:n_in + 2 * n_gather + n_out + n_save]
                sems = refs[len(refs) - 3 * n_gather:]
                refs = refs[:n_in] + refs[n_in + n_gather:n_in + n_gather + n_out + n_save] + \
                    refs[n_in + 2 * n_gather + n_out + n_save:len(refs) - 3 * n_gather]
                copies = lambda: [cp for k in range(n_gather) for cp in _direct_copies(
                    'gather', src_refs[k], dst_refs[k], sems[3 * k], sems[3 * k + 1], sems[3 * k + 2])]

                @pl.when(step_is('first'))
                def _():
                    for cp in copies():
                        cp.start()

            in_refs = refs[:n_in]
            out_refs = refs[n_in:n_in + n_out]
            vals = [r[...] for r in in_refs]
            if has_state:
                save_ref, st_ref = refs[n_in + n_out], refs[n_in + n_out + 1]

                @pl.when(pl.program_id(seq_axis) == 0)
                def _():
                    st_ref[...] = jnp.zeros(state_shape, F32)

                st = st_ref[...]
                save_ref[...] = st
                res = fn(*vals, st)
                st_ref[...] = res[-1]
                res = res[:-1]
            else:
                res = fn(*vals)
            for o, v in zip(out_refs, res):
                o[...] = v.astype(o.dtype)

            if blocks:
                @pl.when(step_is('last'))
                def _():
                    for cp in copies():
                        cp.wait()

        out_shape = [jax.ShapeDtypeStruct(o.shape, o.dtype) for o in outs]
        out_specs = [pl.BlockSpec(o.block, o.imap) for o in outs]
        scratch = []
        if has_state:
            out_shape.append(jax.ShapeDtypeStruct(save_shape, F32))
            out_specs.append(pl.BlockSpec(save_block, save_imap))
            scratch.append(pltpu.VMEM(state_shape, F32))
        hbm = pl.BlockSpec(memory_space=pl.ANY)
        for blk in blocks:
            out_shape.append(jax.ShapeDtypeStruct((N_DEV,) + blk.shape, blk.dtype))
            out_specs.append(hbm)
            scratch += COMM_SCRATCH
        return pl.pallas_call(
            body, name=name + "_fwd", grid=grid,
            in_specs=[in_spec(s, False) for s in ins] + [hbm] * len(blocks),
            out_specs=out_specs, out_shape=out_shape, scratch_shapes=scratch,
            compiler_params=_cparams(),
        )(*xs, *blocks)

    def grad_shape(s, x):
        if s.cols is not None:
            return x.shape[:-1] + (s.cols[1],)
        return x.shape

    def bwd_call(xs, save, cts, pass_cts=()):
        n_diff = len(diff_idx)
        jobs = pending.take_all() if pending is not None else []
        n_args = n_in + (1 if has_state else 0) + n_out + len(passthrough)

        def body(*refs):
            if jobs:
                src_refs = refs[n_args:n_args + len(jobs)]
                dst_refs = refs[n_args + len(jobs) + n_diff:n_args + 2 * len(jobs) + n_diff]
                sems = refs[len(refs) - 3 * len(jobs):]
                refs = refs[:n_args] + refs[n_args + len(jobs):n_args + len(jobs) + n_diff] + \
                    refs[n_args + 2 * len(jobs) + n_diff:len(refs) - 3 * len(jobs)]
                copies = lambda: [cp for k in range(len(jobs)) for cp in _direct_copies(
                    'exchange', src_refs[k], dst_refs[k], sems[3 * k], sems[3 * k + 1], sems[3 * k + 2])]

                @pl.when(step_is('first'))
                def _():
                    for cp in copies():
                        cp.start()

            in_refs = refs[:n_in]
            p = n_in
            if has_state:
                save_ref = refs[p]
                p += 1
            ct_refs = refs[p:p + n_out]
            p += n_out
            pass_refs = dict(zip(passthrough, refs[p:p + len(passthrough)]))
            p += len(passthrough)
            g_refs = refs[p:p + n_diff]
            p += n_diff
            vals = [r[...] for r in in_refs]

            def g(*dv):
                full = list(vals)
                for i, v in zip(diff_idx, dv):
                    full[i] = v
                if has_state:
                    return tuple(fn(*full, dv[-1]))
                return tuple(fn(*full))

            prim = [vals[i] for i in diff_idx]
            ct = tuple(r[...].astype(F32) for r in ct_refs)
            if has_state:
                dst_ref = refs[p]

                @pl.when(pl.program_id(seq_axis) == 0)
                def _():
                    dst_ref[...] = jnp.zeros(state_shape, F32)

                prim = prim + [save_ref[...]]
                ct = ct + (dst_ref[...],)
            _, vjp = jax.vjp(g, *prim)
            grads = vjp(ct)
            for k, i in enumerate(diff_idx):
                s = ins[i]
                if s.kind == 'blk':
                    g = grads[k] + pass_refs[i][...] if i in pass_refs else grads[k]
                    g_refs[k][...] = g.astype(g_refs[k].dtype)
                else:
                    first = None
                    for ax in s.inner:
                        c = pl.program_id(ax) == 0
                        first = c if first is None else jnp.logical_and(first, c)

                    @pl.when(first)
                    def _(k=k):
                        g_refs[k][...] = jnp.zeros_like(g_refs[k])

                    g_refs[k][...] += grads[k].astype(g_refs[k].dtype)
            if has_state:
                dst_ref[...] = grads[-1]

            if jobs:
                @pl.when(step_is('last'))
                def _():
                    for cp in copies():
                        cp.wait()

        in_specs = [in_spec(s, True) for s in ins]
        args = list(xs)
        if has_state:
            in_specs.append(rel_spec(save_block, save_imap, True))
            args.append(save)
        for o, c in zip(outs, cts):
            in_specs.append(rel_spec(o.block, o.imap, True))
            args.append(c)
        for i, c in zip(passthrough, pass_cts):
            assert ins[i].kind == 'blk' and ins[i].cols is None
            in_specs.append(rel_spec(ins[i].block, ins[i].imap, True))
            args.append(c)
        out_shape, out_specs = [], []
        for i in diff_idx:
            s = ins[i]
            out_shape.append(jax.ShapeDtypeStruct(grad_shape(s, xs[i]), xs[i].dtype))
            out_specs.append(rel_spec(s.block, s.imap, True))
        scratch = [pltpu.VMEM(state_shape, F32)] if has_state else []
        hbm = pl.BlockSpec(memory_space=pl.ANY)
        for _, slabs in jobs:
            in_specs.append(hbm)
            args.append(slabs)
            out_specs.append(hbm)
            out_shape.append(jax.ShapeDtypeStruct(slabs.shape, slabs.dtype))
            scratch += COMM_SCRATCH
        got = pl.pallas_call(
            body, name=name + "_bwd", grid=grid,
            in_specs=in_specs, out_specs=out_specs, out_shape=out_shape, scratch_shapes=scratch,
            compiler_params=_cparams(),
        )(*args)
        for (unit, _), slots in zip(jobs, got[n_diff:]):
            pending.received[unit] = slots
        return got[:n_diff]

    def results(xs, res):
        gathered = tuple(res[len(res) - n_gather:]) if n_gather else ()
        return tuple(res[:n_out]) + tuple(xs[i] for i in passthrough) + gathered

    @jax.custom_vjp
    def op(*xs):
        return results(xs, fwd_call(*xs))

    def op_fwd(*xs):
        res = fwd_call(*xs)
        return results(xs, res), (xs, res[n_out] if has_state else None)

    def op_bwd(resid, cts):
        xs, save = resid
        xs, blocks = xs[:n_in], xs[n_in:]
        grads = bwd_call(xs, save, cts[:n_out], cts[n_out:n_out + len(passthrough)])
        out = []
        k = 0
        for i, s in enumerate(ins):
            if s.kind == 'const':
                out.append(jnp.zeros_like(xs[i]))
                continue
            g = grads[k]
            k += 1
            if s.cols is not None:
                g = jnp.pad(g, ((0, 0),) * (g.ndim - 1) + ((s.cols[0], xs[i].shape[-1] - s.cols[0] - s.cols[1]),))
            out.append(g)
        return tuple(out) + tuple(jnp.zeros_like(b) for b in blocks)

    op.defvjp(op_fwd, op_bwd)
    return op


def _rms(x, w):
    return x * lax.rsqrt(jnp.mean(x * x, axis=-1, keepdims=True) + EPS) * w


def _silu(x):
    return x * jax.nn.sigmoid(x)


def rmsnorm_op(name, t, out_dtype, residual=False):
    tm = _tile(t, 512)
    return make_op(
        name, lambda x, w: (_rms(x, w),), (t // tm,),
        [In((tm, D_MODEL), lambda i: (i, 0)), In((1, D_MODEL), lambda i: (0, 0), 'acc', (0,))],
        [Out((t, D_MODEL), out_dtype, (tm, D_MODEL), lambda i: (i, 0))], passthrough=(0,) if residual else ())


def _tri(q):
    ii = lax.broadcasted_iota(jnp.int32, (q, q), 0)
    jj = lax.broadcasted_iota(jnp.int32, (q, q), 1)
    return ii >= jj, ii > jj


def _ssd_fn(z, x, bm, cm, dtr, dtb, alog, dsk, nw, state):
    q = x.shape[0]
    incl, _ = _tri(q)
    tril = incl.astype(F32)
    dt = jax.nn.softplus(dtr + dtb)
    da = dt * (-jnp.exp(alog))
    acum = cumdot(tril, da)
    acum_t = cumdot_t(tril, da)
    cb = bdot(cm, bm, NT)
    heads = range(4)
    wide = lambda a: jnp.concatenate([jnp.broadcast_to(a[:, r:r + 1], (a.shape[0], 64)) for r in heads], axis=1)
    last = acum[q - 1:q, :]
    xc = x * wide(dt)
    y = bdot(cm, state, NT) * wide(jnp.exp(acum)) + wide(dsk) * x
    ds = bdot(xc * wide(jnp.exp(last - acum)), bm, TN)
    e_last = jnp.exp(last)
    new_state = state * jnp.concatenate([jnp.broadcast_to(e_last[:, r:r + 1], (64, 1)) for r in heads], axis=0) + ds
    diag = []
    for r in heads:
        decay = jnp.exp(jnp.where(incl, acum[:, r:r + 1] - acum_t[r:r + 1, :], -jnp.inf))
        diag.append(bdot(cb * decay, xc[:, 64 * r:64 * r + 64]))
    y = y + jnp.concatenate(diag, axis=1)
    yz = y * _silu(z)
    return _rms(yz, nw), new_state


def _per_sequence(fn, n_seq_args, bl):
    def f(*args):
        *ins, state = args
        res = [fn(*[a[b] for a in ins[:n_seq_args]], *ins[n_seq_args:], state[b]) for b in range(bl)]
        return tuple(jnp.concatenate([r[k][None] for r in res]) for k in range(len(res[0])))

    return f


def ssd_op(name, bl, seq, pending=None, n_gather=0):
    q = SSD_CHUNK
    nc = seq // q
    blk = lambda w, c0, cw: In((bl, q, w), lambda g, n: (0, n, g), cols=(c0, cw))
    small = lambda g, n: (g, 0, 0)
    ins = [
        blk(256, 0, M_D_INNER),
        blk(256, 0, M_D_INNER),
        blk(128, M_D_INNER, 1024),
        blk(128, M_D_INNER + 1024, 1024),
        In((None, bl, q, 4), lambda g, n: (g, 0, n, 0)),
        In((None, 1, 4), small, 'acc', (1,)),
        In((None, 1, 4), small, 'acc', (1,)),
        In((None, 1, 4), small, 'acc', (1,)),
        In((None, 1, 256), small, 'acc', (1,)),
    ]
    outs = [Out((bl, seq, M_D_INNER), F32, (bl, q, 256), lambda g, n: (0, n, g))]
    return make_op(name, _per_sequence(_ssd_fn, 5, bl), (M_GROUPS, nc), ins, outs,
                   state_shape=(bl, 256, 128), seq_axis=1, pending=pending, n_gather=n_gather)


def _gla_fn(layer, qr, fr, ir, gr, lbp, nw, state_t):
    rows = qr.shape[0]
    c = HGRN_CHUNK
    n_chunks = rows // c
    e = jnp.exp(lbp - jnp.max(lbp, axis=0, keepdims=True))
    sm = e / jnp.sum(e, axis=0, keepdims=True)
    lb = jnp.sum(sm[1:layer + 1, :], axis=0, keepdims=True) if layer > 0 else jnp.zeros((1, lbp.shape[1]), F32)
    qq = _silu(qr) * (128 ** -0.5)
    forget = lb + (1.0 - lb) * jax.nn.sigmoid(fr)
    kk = 1.0 - forget
    logf = jnp.log(forget)
    ii = lax.broadcasted_iota(jnp.int32, (rows, rows), 0)
    jj = lax.broadcasted_iota(jnp.int32, (rows, rows), 1)
    own = jnp.logical_and(ii >= jj, ii // c == jj // c)
    gc = cumdot(own.astype(F32), logf)
    glasts = [gc[c * j + c - 1:c * j + c, :] for j in range(n_chunks)]
    glast_rows = jnp.concatenate([jnp.broadcast_to(g, (c, g.shape[1])) for g in glasts], axis=0)
    q_dec = qq * jnp.exp(gc)
    k_inv = kk * jnp.exp(-gc)
    k_end = kk * jnp.exp(glast_rows - gc)
    att = jnp.where(own, bdot(q_dec, k_inv, NT), 0.0)
    o = bdot(att, ir)
    inter = []
    for j in range(n_chunks):
        sl = slice(c * j, c * j + c)
        inter.append(bdot(q_dec[sl], state_t, NT))
        state_t = state_t * jnp.exp(glasts[j]) + bdot(ir[sl], k_end[sl], TN)
    o = o + jnp.concatenate(inter, axis=0)
    return _rms(o, nw) * _silu(gr), state_t


def gla_op(name, layer, bl, seq, pending=None, n_gather=0):
    r = HGRN_ROWS
    ns = seq // r
    blk = lambda k: In((bl, r, 128), lambda h, n: (0, n, h), cols=(1024 * k, 1024))
    ins = [blk(0), blk(1), blk(2), blk(3),
           In((DEPTH, 128), lambda h, n: (0, h), 'acc', (1,)),
           In((1, 128), lambda h, n: (0, 0), 'acc', (0, 1))]
    outs = [Out((bl, seq, D_MODEL), F32, (bl, r, 128), lambda h, n: (0, n, h))]
    return make_op(name, _per_sequence(functools.partial(_gla_fn, layer), 4, bl), (H_HEADS, ns), ins, outs,
                   state_shape=(bl, 128, 128), seq_axis=1, pending=pending, n_gather=n_gather)


def _neumann_inverse(m):
    q = m.shape[1]
    ii = lax.broadcasted_iota(jnp.int32, (q, q), 0)
    jj = lax.broadcasted_iota(jnp.int32, (q, q), 1)
    eye = (ii == jj).astype(F32)[None]
    p = -m
    inv = eye + p
    for _ in range(int(math.log2(q)) - 1):
        p = _h3(p, p, BNN)
        inv = inv + _h3(inv, p, BNN)
    return inv


@jax.custom_vjp
def _unit_lower_inverse(m):
    return _neumann_inverse(m)


def _unit_lower_inverse_fwd(m):
    inv = _neumann_inverse(m)
    return inv, inv


_unit_lower_inverse.defvjp(_unit_lower_inverse_fwd,
                           lambda inv, ct: (-_h3(_h3(inv, ct, BTN), inv, BNT),))


def _gdn_fn(qc, kc, vc, zc, br, ar, alog, dtb, nw, state):
    bl, q = qc.shape[0], qc.shape[1]
    incl, strict = _tri(q)
    tril = incl.astype(F32)
    g = jnp.concatenate([-jnp.exp(alog) * jax.nn.softplus(ar[b] + dtb) for b in range(bl)], axis=1)
    gc = cumdot(tril, g)
    gc_t = cumdot_t(tril, g)
    heads, ms, rhs = [], [], []
    for b in range(bl):
        qn = qc[b] * lax.rsqrt(jnp.sum(qc[b] * qc[b], axis=-1, keepdims=True) + EPS) * (128 ** -0.5)
        kn = kc[b] * lax.rsqrt(jnp.sum(kc[b] * kc[b], axis=-1, keepdims=True) + EPS)
        beta = jax.nn.sigmoid(br[b])
        qk = bdot(qn, kn, NT)
        for j in range(2):
            i = 2 * b + j
            col = gc[:, i:i + 1]
            decay = jnp.exp(jnp.where(incl, col - gc_t[i:i + 1, :], -jnp.inf))
            bj = beta[:, j:j + 1]
            kb = kn * bj
            ms.append(jnp.where(strict, bdot(kb, kn, NT) * decay, 0.0))
            rhs.append(jnp.concatenate([vc[b][:, 128 * j:128 * j + 128] * bj, kb * jnp.exp(col)], axis=1))
            heads.append((qn, kn, qk * decay, col, gc[q - 1:q, i:i + 1]))
    sol = h3dot_b(_unit_lower_inverse(jnp.concatenate([m[None] for m in ms])),
                  jnp.concatenate([r[None] for r in rhs]))
    outs, states = [], []
    for b in range(bl):
        os_, sts = [], []
        for j in range(2):
            i = 2 * b + j
            qn, kn, att, col, glast = heads[i]
            u = sol[i][:, :128]
            w = sol[i][:, 128:]
            st = state[b][128 * j:128 * j + 128, :]
            v_new = u - bdot(w, st)
            o = bdot(qn * jnp.exp(col), st) + bdot(att, v_new)
            sts.append(st * jnp.exp(glast) + bdot(kn * jnp.exp(glast - col), v_new, TN))
            os_.append(_rms(o, nw) * _silu(zc[b][:, 128 * j:128 * j + 128]))
        outs.append(jnp.concatenate(os_, axis=1))
        states.append(jnp.concatenate(sts, axis=0))
    return jnp.concatenate([o[None] for o in outs]), jnp.concatenate([st[None] for st in states])


def gdn_op(name, bl, seq, pending=None, n_gather=0):
    q = GDN_CHUNK
    nc = seq // q
    blk = lambda w, c0, cw: In((bl, q, w), lambda h, n: (0, n, h), cols=(c0, cw))
    small = lambda h, n: (h, 0, 0)
    ins = [
        blk(128, 0, G_KEY_DIM),
        blk(128, G_KEY_DIM, G_KEY_DIM),
        blk(256, 2 * G_KEY_DIM, G_VAL_DIM),
        blk(256, G_CONV_DIM, G_VAL_DIM),
        In((None, bl, q, 2), lambda h, n: (h, 0, n, 0)),
        In((None, bl, q, 2), lambda h, n: (h, 0, n, 0)),
        In((None, 1, 2), small, 'acc', (1,)),
        In((None, 1, 2), small, 'acc', (1,)),
        In((1, 128), lambda h, n: (0, 0), 'acc', (0, 1)),
    ]
    outs = [Out((bl, seq, G_VAL_DIM), F32, (bl, q, 256), lambda h, n: (0, n, h))]
    return make_op(name, _gdn_fn, (G_QK_HEADS, nc), ins, outs,
                   state_shape=(bl, 256, 128), seq_axis=1, pending=pending, n_gather=n_gather)


def _xattn_fn(q, k, v):
    s = bdot(q, k, NT) * (X_HEAD_DIM ** -0.5)
    s = s - jnp.max(s, axis=-1, keepdims=True)
    p = jnp.exp(s)
    p = p / jnp.sum(p, axis=-1, keepdims=True)
    return (bdot(p, v),)


def xattn_op(name, bl, seq):
    tq = _tile(seq, 2048)
    nq = seq // tq
    t = bl * seq
    ins = [
        In((tq, X_HEAD_DIM), lambda b, h, i: (b * nq + i, h)),
        In((N_MEM, X_HEAD_DIM), lambda b, h, i: (b, h), 'acc', (2,), cols=(0, D_MODEL)),
        In((N_MEM, X_HEAD_DIM), lambda b, h, i: (b, h), 'acc', (2,), cols=(D_MODEL, D_MODEL)),
    ]
    outs = [Out((t, D_MODEL), F32, (tq, X_HEAD_DIM), lambda b, h, i: (b * nq + i, h))]
    return make_op(name, _xattn_fn, (bl, X_HEADS, nq), ins, outs)


CONV_PAD = 8
CONV_ROWS = 128


def make_conv(name, bl, seq, width, ch, x_col0, up_col0=None):
    cb = 256
    rt = CONV_ROWS
    assert ch % cb == 0 and x_col0 % cb == 0 and (up_col0 is None or up_col0 % cb == 0) and seq % rt == 0
    nb = ch // cb
    n_tiles = seq // rt
    t = bl * seq
    has_up = up_col0 is not None
    grid = (nb, bl)
    x_spec = pl.BlockSpec((seq, cb), lambda c, b: (b, x_col0 // cb + c))
    up_specs = [pl.BlockSpec((seq, cb), lambda c, b: (b, up_col0 // cb + c))] if has_up else []
    w_spec = pl.BlockSpec((width, cb), lambda c, b: (0, c))
    b_spec = pl.BlockSpec((1, cb), lambda c, b: (0, c))
    o_spec = pl.BlockSpec((seq, cb), lambda c, b: (b, c))
    taps = [CONV_PAD - (width - 1) + j for j in range(width)]

    def window(x_ref, i):
        if isinstance(i, int) and i == 0:
            return jnp.concatenate([jnp.zeros((CONV_PAD, cb), F32), x_ref[0:rt, :]], axis=0)
        return x_ref[pl.ds(pl.multiple_of(i * rt - CONV_PAD, CONV_PAD), rt + CONV_PAD), :]

    def rows(i):
        return pl.ds(i * rt, rt) if isinstance(i, int) else pl.ds(pl.multiple_of(i * rt, rt), rt)

    def shifted(win):
        return [win[tp:tp + rt, :] for tp in taps]

    def pre_activation(views, w, b):
        y = b + w[0:1, :] * views[0]
        for j in range(1, width):
            y = y + w[j:j + 1, :] * views[j]
        return y

    def over_tiles(step, carry):
        carry = step(0, carry)
        return lax.fori_loop(1, n_tiles, step, carry)

    def fwd_call(x, w, b):
        def body(*refs):
            x_ref, w_ref, b_ref = refs[:3]
            o_ref = refs[-1]
            w_, b_ = w_ref[...], b_ref[...]

            def step(i, carry):
                y = _silu(pre_activation(shifted(window(x_ref, i)), w_, b_))
                if has_up:
                    y = y * refs[3][rows(i), :]
                o_ref[rows(i), :] = y.astype(o_ref.dtype)
                return carry

            over_tiles(step, 0)

        return pl.pallas_call(
            body, name=name + "_fwd", grid=grid,
            in_specs=[x_spec, w_spec, b_spec] + up_specs, out_specs=o_spec,
            out_shape=jax.ShapeDtypeStruct((t, ch), BF16 if has_up else F32),
            compiler_params=_cparams(),
        )(*([x, w, b] + ([x] if has_up else [])))

    def bwd_call(x, w, b, do):
        n_in = 4 + (1 if has_up else 0)

        def body(*refs):
            x_ref, w_ref, b_ref = refs[:3]
            do_ref = refs[n_in - 1]
            dx_ref, dw_ref, db_ref = refs[n_in:n_in + 3]
            gpad_ref = refs[-1]
            w_, b_ = w_ref[...], b_ref[...]

            def fold(a):
                acc = a[0:8, :]
                for k in range(1, rt // 8):
                    acc = acc + a[8 * k:8 * k + 8, :]
                return acc

            def grad_pre(i, sums):
                views = shifted(window(x_ref, i))
                y = pre_activation(views, w_, b_)
                s = jax.nn.sigmoid(y)
                act = y * s
                do_ = do_ref[rows(i), :].astype(F32)
                if has_up:
                    refs[n_in + 3][rows(i), :] = do_ * act
                    do_ = do_ * refs[3][rows(i), :]
                dy = do_ * (s + act * (1.0 - s))
                gpad_ref[rows(i), :] = dy
                new = [sums[j] + fold(dy * views[j]) for j in range(width)]
                return tuple(new) + (sums[width] + fold(dy),)

            zero8 = jnp.zeros((8, cb), F32)
            sums = over_tiles(grad_pre, (zero8,) * (width + 1))
            gpad_ref[seq:seq + CONV_PAD, :] = jnp.zeros((CONV_PAD, cb), F32)

            def grad_x(i, carry):
                if isinstance(i, int):
                    gwin = gpad_ref[0:rt + CONV_PAD, :]
                else:
                    gwin = gpad_ref[pl.ds(pl.multiple_of(i * rt, rt), rt + CONV_PAD), :]
                dx = w_[0:1, :] * gwin[width - 1:width - 1 + rt, :]
                for j in range(1, width):
                    dx = dx + w_[j:j + 1, :] * gwin[width - 1 - j:width - 1 - j + rt, :]
                dx_ref[rows(i), :] = dx
                return carry

            over_tiles(grad_x, 0)

            @pl.when(pl.program_id(1) == 0)
            def _():
                dw_ref[...] = jnp.zeros_like(dw_ref)
                db_ref[...] = jnp.zeros_like(db_ref)

            dw_ref[...] += jnp.concatenate([jnp.sum(sums[j], axis=0, keepdims=True) for j in range(width)], axis=0)
            db_ref[...] += jnp.sum(sums[width], axis=0, keepdims=True)

        big = jax.ShapeDtypeStruct((t, ch), F32)
        return pl.pallas_call(
            body, name=name + "_bwd", grid=grid,
            in_specs=[x_spec, w_spec, b_spec] + up_specs + [o_spec],
            out_specs=[o_spec, w_spec, b_spec] + ([o_spec] if has_up else []),
            out_shape=[big, jax.ShapeDtypeStruct((width, ch), F32), jax.ShapeDtypeStruct((1, ch), F32)]
            + ([big] if has_up else []),
            scratch_shapes=[pltpu.VMEM((seq + CONV_PAD, cb), F32)],
            compiler_params=_cparams(),
        )(*([x, w, b] + ([x] if has_up else []) + [do]))

    @jax.custom_vjp
    def conv(x, w, b):
        return fwd_call(x, w, b)

    def conv_fwd(x, w, b):
        return fwd_call(x, w, b), (x, w, b)

    def conv_bwd(res, do):
        x, w, b = res
        got = bwd_call(x, w, b, do)
        dx = jnp.pad(got[0], ((0, 0), (x_col0, x.shape[1] - x_col0 - ch)))
        if has_up:
            dx = dx + jnp.pad(got[3], ((0, 0), (up_col0, x.shape[1] - up_col0 - ch)))
        return dx, got[1], got[2]

    conv.defvjp(conv_fwd, conv_bwd)

    def apply(x, w, b=None):
        if b is None:
            b = jnp.zeros((ch,), F32)
        return conv(x, w, b.reshape(1, ch))

    return apply


def loss_head(x, w, target):
    t = x.shape[0]
    tm = _tile(t, 512)

    def fn(xb, wb, tb):
        err = _rms(xb, wb) - tb
        return 0.5 * jnp.sum(err * err) * (1.0 / D_MODEL)

    def body(x_ref, w_ref, t_ref, loss_ref, dx_ref, dw_ref):
        @pl.when(pl.program_id(0) == 0)
        def _():
            loss_ref[...] = jnp.zeros_like(loss_ref)
            dw_ref[...] = jnp.zeros_like(dw_ref)

        tb = t_ref[...]
        val, vjp = jax.vjp(lambda a, b: fn(a, b, tb), x_ref[...], w_ref[...])
        dx, dw = vjp(jnp.ones((), F32))
        dx_ref[...] = dx
        dw_ref[...] += dw
        loss_ref[...] += jnp.full(loss_ref.shape, val, F32)

    row = pl.BlockSpec((tm, D_MODEL), lambda i: (i, 0))
    vec = pl.BlockSpec((1, D_MODEL), lambda i: (0, 0))
    loss, dx, dw = pl.pallas_call(
        body, name="loss_head", grid=(t // tm,),
        in_specs=[row, vec, row],
        out_specs=[pl.BlockSpec((8, LANE), lambda i: (0, 0)), row, vec],
        out_shape=[jax.ShapeDtypeStruct((8, LANE), F32), jax.ShapeDtypeStruct((t, D_MODEL), F32),
                   jax.ShapeDtypeStruct((1, D_MODEL), F32)],
        compiler_params=_cparams(),
    )(x, w.reshape(1, D_MODEL), target)
    return loss[0, 0], dx, dw.reshape(D_MODEL)


PACK_W = 1024
ADAM_BLOCK_BYTES = 512 * 1024


def _rows_tile(r, c):
    if r * c * 4 <= ADAM_BLOCK_BYTES or r % 8:
        return r
    best = 8
    for t in range(8, r + 1, 8):
        if r % t == 0 and t * c * 4 <= ADAM_BLOCK_BYTES:
            best = t
    return best


def reduce_adamw(slots, w, m, v, name):
    r, wd = w.shape
    tr = _rows_tile(r, wd)
    c1 = 1.0 - ADAM_B1 ** ADAM_STEP
    c2 = 1.0 - ADAM_B2 ** ADAM_STEP

    def body(s_ref, w_ref, m_ref, v_ref, g_ref, d_ref, nm_ref, nv_ref):
        g = s_ref[0].astype(F32)
        for k in range(1, N_DEV):
            g = g + s_ref[k].astype(F32)
        nm = ADAM_B1 * m_ref[...] + (1.0 - ADAM_B1) * g
        nv = ADAM_B2 * v_ref[...] + (1.0 - ADAM_B2) * (g * g)
        m_hat = nm / c1
        v_hat = nv / c2
        d_ref[...] = -ADAM_LR * (m_hat / (jnp.sqrt(v_hat) + ADAM_EPS) + ADAM_WD * w_ref[...])
        g_ref[...] = g
        nm_ref[...] = nm
        nv_ref[...] = nv

    blk = pl.BlockSpec((tr, wd), lambda i: (i, 0))
    shp = jax.ShapeDtypeStruct((r, wd), F32)
    return pl.pallas_call(
        body, name=name, grid=(r // tr,),
        in_specs=[pl.BlockSpec((N_DEV, tr, wd), lambda i: (0, i, 0)), blk, blk, blk],
        out_specs=[blk, blk, blk, blk], out_shape=[shp, shp, shp, shp],
        compiler_params=_cparams(),
    )(slots, w, m, v)


def all_gather(block, name):
    def body(x_ref, out_ref, send_sems, recv_sems, local_sem):
        x, y, c = _position()
        me, sibling = (x, y, c), (x, y, 1 - c)
        chips = [(1 - x, y), (x, 1 - y), (1 - x, 1 - y)]

        def slot(px, py, pc):
            return out_ref.at[4 * px + 2 * py + pc]

        def copy(k, owner, to, src=None):
            return pltpu.make_async_remote_copy(
                src_ref=slot(*owner) if src is None else src, dst_ref=slot(*owner),
                send_sem=send_sems.at[k], recv_sem=recv_sems.at[k],
                device_id=to, device_id_type=pl.DeviceIdType.MESH)

        mine = pltpu.make_async_copy(x_ref, slot(*me), local_sem)
        mine.start()
        first = [copy(0, me, sibling, src=x_ref)]
        first += [copy(1 + j, me, (*chip, c), src=x_ref) for j, chip in enumerate(chips)]
        for cp in first:
            cp.start()
        passed = [copy(4 + j, (*chip, c), sibling) for j, chip in enumerate(chips)]
        for j, chip in enumerate(chips):
            copy(1 + j, (*chip, c), me).wait_recv()
            passed[j].start()
        copy(0, sibling, me).wait_recv()
        for j, chip in enumerate(chips):
            copy(4 + j, (*chip, 1 - c), me).wait_recv()
        for cp in first + passed:
            cp.wait_send()
        mine.wait()

    return pl.pallas_call(
        body, name=name,
        out_shape=jax.ShapeDtypeStruct((N_DEV,) + block.shape, block.dtype),
        in_specs=[pl.BlockSpec(memory_space=pl.ANY)],
        out_specs=pl.BlockSpec(memory_space=pl.ANY),
        scratch_shapes=[pltpu.SemaphoreType.DMA((7,)), pltpu.SemaphoreType.DMA((7,)), pltpu.SemaphoreType.DMA],
    )(block)


def exchange_slabs(slabs, name):
    def body(in_ref, out_ref, send_sems, recv_sems, local_sem):
        x, y, c = _position()
        my = 4 * x + 2 * y + c
        mine = pltpu.make_async_copy(in_ref.at[my], out_ref.at[my], local_sem)
        mine.start()
        copies = []
        for k in range(1, N_DEV):
            dx, dy, dc = (k >> 2) & 1, (k >> 1) & 1, k & 1
            px = x if dx == 0 else 1 - x
            py = y if dy == 0 else 1 - y
            pc = c if dc == 0 else 1 - c
            cp = pltpu.make_async_remote_copy(
                src_ref=in_ref.at[4 * px + 2 * py + pc], dst_ref=out_ref.at[my],
                send_sem=send_sems.at[k - 1], recv_sem=recv_sems.at[k - 1],
                device_id=(px, py, pc), device_id_type=pl.DeviceIdType.MESH)
            cp.start()
            copies.append(cp)
        for cp in copies:
            cp.wait()
        mine.wait()

    return pl.pallas_call(
        body, name=name,
        out_shape=jax.ShapeDtypeStruct(slabs.shape, slabs.dtype),
        in_specs=[pl.BlockSpec(memory_space=pl.ANY)],
        out_specs=pl.BlockSpec(memory_space=pl.ANY),
        scratch_shapes=[pltpu.SemaphoreType.DMA((7,)), pltpu.SemaphoreType.DMA((7,)), pltpu.SemaphoreType.DMA],
    )(slabs)


def _pack(arrays, dtype, row_multiple):
    flat = jnp.concatenate([a.astype(dtype).reshape(-1) for a in arrays])
    n = flat.shape[0]
    per = PACK_W * row_multiple
    total = -(-n // per) * per
    flat = jnp.pad(flat, (0, total - n))
    return flat.reshape(total // PACK_W, PACK_W)


def _unpack(flat2d, shapes, lead=()):
    flat = flat2d.reshape(lead + (-1,))
    out, off = [], 0
    for shp in shapes:
        n = math.prod(shp)
        out.append(flat[..., off:off + n].reshape(lead + tuple(shp)))
        off += n
    return out


def _full_from_gathered(g, axis):
    g = jnp.moveaxis(g, 0, axis)
    shp = list(g.shape)
    shp[axis:axis + 2] = [shp[axis] * shp[axis + 1]]
    return g.reshape(shp)


def _shards_of_full(full, axis):
    shp = list(full.shape)
    shp[axis:axis + 1] = [N_DEV, shp[axis] // N_DEV]
    return jnp.moveaxis(full.reshape(shp), axis, 0)


def layer_units(i):
    mixer = [('m_in_w', 'm_out_w'), ('h_in_w', 'h_out_w'), ('g_in_w', 'g_out_w')][i % 3]
    return [(mixer[0], i // 3), (mixer[1], i // 3), ('xa_q', i), ('xa_kv', i), ('xa_o', i), ('f_up', i), ('f_down', i)]


PADDED_COLS = {'m_in_w': M_IN_PAD, 'g_in_w': G_IN_PAD}
BIG_WEIGHTS = ('m_in_w', 'h_in_w', 'g_in_w', 'f_up')


def whole_weight(name, gathered):
    w = _full_from_gathered(lax.stop_gradient(gathered), SHARD_AXIS[name] - 1)
    return _pad_cols(w, 1, PADDED_COLS[name]) if name in PADDED_COLS else w


def _trunk(p, weights, blocks, standins, x, mem, bl, seq, pending=None):
    t = bl * seq
    ia = ib = ic = 0
    weights = dict(weights)
    state = {}

    def lin(name, a, wname, idx, residual=None):
        unit = (wname, idx)
        pos = state['units'].index(unit)
        later = state['next'][pos] if state['next'] else None
        if wname == 'f_up' and later in state['by_core']:
            later = ('xa_kv', i + 1)
        elif wname == 'xa_kv' and ('f_up', i + 1) in state['by_core'] and later not in state['by_core']:
            later = None
        nxt = (blocks[later],) if later in blocks and later not in state['by_core'] else ()
        n_real = N_DEV * standins[unit].shape[2]
        res = () if residual is None else (residual,)
        mixer_in = wname in ('m_in_w', 'h_in_w', 'g_in_w')
        defer = (pending, unit) if pending is not None and not (mixer_in and i == 0) else None
        y, got = make_linear(name, SHARD_AXIS[wname] - 1, n_real, bool(res), defer)(
            a, weights[unit], standins[unit], nxt, res)
        if nxt:
            weights[later] = whole_weight(later[0], got[0])
        return y

    by_seq = lambda a: a.reshape(bl, seq, a.shape[-1])

    for i in range(DEPTH):
        state['units'] = layer_units(i)
        state['next'] = layer_units(i + 1) if i + 1 < DEPTH else None
        state['by_core'] = [u for u in (state['next'] or [])
                            if u in blocks and (u[0] in BIG_WEIGHTS[:3 if i % 3 == 1 else 4] or i % 3 == 2)]
        core_blocks = [blocks[u] for u in state['by_core']]

        def core(op, *args):
            y, *got = op(*args, *core_blocks)
            for u, g in zip(state['by_core'], got):
                weights[u] = whole_weight(u[0], g)
            return y
        hn, x = rmsnorm_op(f"ln_mix{i}", t, F32, residual=True)(x, p['ln_mix'][i:i + 1])
        kind = i % 3
        if kind == 0:
            proj = lin(f"m_in{i}", hn, 'm_in_w', ia)
            xbc = make_conv(f"m_conv{i}", bl, seq, 4, M_CONV_DIM, M_D_INNER)(
                proj, p['m_conv_w'][ia], p['m_conv_b'][ia])
            dt = proj[:, M_D_INNER + M_CONV_DIM:M_IN].reshape(bl, seq, M_GROUPS, 4).transpose(2, 0, 1, 3)
            grp = lambda a, n=4: a.reshape(M_GROUPS, 1, n)
            proj3, xbc3 = by_seq(proj), by_seq(xbc)
            y = core(ssd_op(f"ssd{i}", bl, seq, pending, len(core_blocks)),
                     proj3, xbc3, xbc3, xbc3, dt, grp(p['m_dt_bias'][ia]), grp(p['m_a_log'][ia]), grp(p['m_d'][ia]),
                     grp(p['m_norm_w'][ia], 256))
            x = lin(f"m_out{i}", y.reshape(t, M_D_INNER), 'm_out_w', ia, residual=x)
            ia += 1
        elif kind == 1:
            proj3 = by_seq(lin(f"h_in{i}", hn, 'h_in_w', ib))
            y = core(gla_op(f"gla{i}", i, bl, seq, pending, len(core_blocks)),
                     proj3, proj3, proj3, proj3, p['h_lower_bounds'], p['h_norm_w'][ib:ib + 1])
            x = lin(f"h_out{i}", y.reshape(t, D_MODEL), 'h_out_w', ib, residual=x)
            ib += 1
        else:
            proj = lin(f"g_in{i}", hn, 'g_in_w', ic)
            qkv = make_conv(f"g_conv{i}", bl, seq, 4, G_CONV_DIM, 0)(proj, p['g_conv_w'][ic])
            c0 = G_CONV_DIM + G_VAL_DIM
            heads = lambda a: a.reshape(bl, seq, G_QK_HEADS, 2).transpose(2, 0, 1, 3)
            braw = heads(proj[:, c0:c0 + G_V_HEADS])
            araw = heads(proj[:, c0 + G_V_HEADS:c0 + 2 * G_V_HEADS])
            grp = lambda a: a.reshape(G_QK_HEADS, 1, 2)
            qkv3 = by_seq(qkv)
            y = core(gdn_op(f"gdn{i}", bl, seq, pending, len(core_blocks)),
                     qkv3, qkv3, qkv3, by_seq(proj), braw, araw, grp(p['g_a_log'][ic]), grp(p['g_dt_bias'][ic]),
                     p['g_norm_w'][ic:ic + 1])
            x = lin(f"g_out{i}", y.reshape(t, G_VAL_DIM), 'g_out_w', ic, residual=x)
            ic += 1
        hq, x = rmsnorm_op(f"ln_xattn{i}", t, F32, residual=True)(x, p['ln_xattn'][i:i + 1])
        mn = rmsnorm_op(f"ln_mem{i}", bl * N_MEM, F32)(mem, p['ln_mem'][i:i + 1])[0]
        qx = lin(f"xa_q{i}", hq, 'xa_q', i)
        kv = lin(f"xa_kv{i}", mn, 'xa_kv', i)
        ao = xattn_op(f"xattn{i}", bl, seq)(qx, kv, kv)[0]
        x = lin(f"xa_o{i}", ao, 'xa_o', i, residual=x)
        hf, x = rmsnorm_op(f"ln_ffn{i}", t, F32, residual=True)(x, p['ln_ffn'][i:i + 1])
        up = lin(f"f_up{i}", hf, 'f_up', i)
        act = make_conv(f"f_conv{i}", bl, seq, 3, D_FF, 0, up_col0=D_FF)(up, p['f_conv_w'][i], p['f_conv_b'][i])
        x = lin(f"f_down{i}", act, 'f_down', i, residual=x)
    return x


def _pad_cols(w, axis, to):
    pad = [(0, 0)] * w.ndim
    pad[axis] = (0, to - w.shape[axis])
    return jnp.pad(w, pad)


def kernel(x, mem, ln_mix, ln_xattn, ln_mem, ln_ffn, final_norm, m_in_w, m_conv_w, m_conv_b, m_dt_bias, m_a_log, m_d, m_norm_w, m_out_w, h_in_w, h_lower_bounds, h_norm_w, h_out_w, g_in_w, g_conv_w, g_a_log, g_dt_bias, g_norm_w, g_out_w, xa_q, xa_kv, xa_o, f_up, f_conv_w, f_conv_b, f_down, loss_target, m_ln_mix, m_ln_xattn, m_ln_mem, m_ln_ffn, m_final_norm, m_m_in_w, m_m_conv_w, m_m_conv_b, m_m_dt_bias, m_m_a_log, m_m_d, m_m_norm_w, m_m_out_w, m_h_in_w, m_h_lower_bounds, m_h_norm_w, m_h_out_w, m_g_in_w, m_g_conv_w, m_g_a_log, m_g_dt_bias, m_g_norm_w, m_g_out_w, m_xa_q, m_xa_kv, m_xa_o, m_f_up, m_f_conv_w, m_f_conv_b, m_f_down, v_ln_mix, v_ln_xattn, v_ln_mem, v_ln_ffn, v_final_norm, v_m_in_w, v_m_conv_w, v_m_conv_b, v_m_dt_bias, v_m_a_log, v_m_d, v_m_norm_w, v_m_out_w, v_h_in_w, v_h_lower_bounds, v_h_norm_w, v_h_out_w, v_g_in_w, v_g_conv_w, v_g_a_log, v_g_dt_bias, v_g_norm_w, v_g_out_w, v_xa_q, v_xa_kv, v_xa_o, v_f_up, v_f_conv_w, v_f_conv_b, v_f_down):
    local = dict(ln_mix=ln_mix, ln_xattn=ln_xattn, ln_mem=ln_mem, ln_ffn=ln_ffn, final_norm=final_norm, m_in_w=m_in_w, m_conv_w=m_conv_w, m_conv_b=m_conv_b, m_dt_bias=m_dt_bias, m_a_log=m_a_log, m_d=m_d, m_norm_w=m_norm_w, m_out_w=m_out_w, h_in_w=h_in_w, h_lower_bounds=h_lower_bounds, h_norm_w=h_norm_w, h_out_w=h_out_w, g_in_w=g_in_w, g_conv_w=g_conv_w, g_a_log=g_a_log, g_dt_bias=g_dt_bias, g_norm_w=g_norm_w, g_out_w=g_out_w, xa_q=xa_q, xa_kv=xa_kv, xa_o=xa_o, f_up=f_up, f_conv_w=f_conv_w, f_conv_b=f_conv_b, f_down=f_down)
    mom_m = dict(ln_mix=m_ln_mix, ln_xattn=m_ln_xattn, ln_mem=m_ln_mem, ln_ffn=m_ln_ffn, final_norm=m_final_norm, m_in_w=m_m_in_w, m_conv_w=m_m_conv_w, m_conv_b=m_m_conv_b, m_dt_bias=m_m_dt_bias, m_a_log=m_m_a_log, m_d=m_m_d, m_norm_w=m_m_norm_w, m_out_w=m_m_out_w, h_in_w=m_h_in_w, h_lower_bounds=m_h_lower_bounds, h_norm_w=m_h_norm_w, h_out_w=m_h_out_w, g_in_w=m_g_in_w, g_conv_w=m_g_conv_w, g_a_log=m_g_a_log, g_dt_bias=m_g_dt_bias, g_norm_w=m_g_norm_w, g_out_w=m_g_out_w, xa_q=m_xa_q, xa_kv=m_xa_kv, xa_o=m_xa_o, f_up=m_f_up, f_conv_w=m_f_conv_w, f_conv_b=m_f_conv_b, f_down=m_f_down)
    mom_v = dict(ln_mix=v_ln_mix, ln_xattn=v_ln_xattn, ln_mem=v_ln_mem, ln_ffn=v_ln_ffn, final_norm=v_final_norm, m_in_w=v_m_in_w, m_conv_w=v_m_conv_w, m_conv_b=v_m_conv_b, m_dt_bias=v_m_dt_bias, m_a_log=v_m_a_log, m_d=v_m_d, m_norm_w=v_m_norm_w, m_out_w=v_m_out_w, h_in_w=v_h_in_w, h_lower_bounds=v_h_lower_bounds, h_norm_w=v_h_norm_w, h_out_w=v_h_out_w, g_in_w=v_g_in_w, g_conv_w=v_g_conv_w, g_a_log=v_g_a_log, g_dt_bias=v_g_dt_bias, g_norm_w=v_g_norm_w, g_out_w=v_g_out_w, xa_q=v_xa_q, xa_kv=v_xa_kv, xa_o=v_xa_o, f_up=v_f_up, f_conv_w=v_f_conv_w, f_conv_b=v_f_conv_b, f_down=v_f_down)

    bl, seq, _ = x.shape
    t = bl * seq

    p = {n: local[n] for n in WEIGHTS if n not in SHARD_AXIS}
    for n in SMALL_SHARDED:
        p[n] = _full_from_gathered(all_gather(local[n], f"gather_{n}"), SHARD_AXIS[n])
    units = [(n, l) for n in MATMUL_WEIGHTS for l in range(local[n].shape[0])]
    block = lambda u: local[u[0]][u[1]].astype(BF16)
    weights = {u: whole_weight(u[0], all_gather(block(u), f"gather_{u[0]}{u[1]}")) for u in layer_units(0)}
    blocks = {u: block(u) for u in units if u not in weights}
    standins = {u: jnp.zeros((N_DEV,) + local[u[0]].shape[1:], BF16) for u in units}
    small = {n: p[n] for n in WEIGHTS if n not in MATMUL_WEIGHTS and n != 'final_norm'}

    pending = Pending()

    def run(small_w, standins_, xin):
        return _trunk(small_w, weights, blocks, standins_, xin, mem.reshape(bl * N_MEM, D_MODEL), bl, seq, pending)

    x_out, vjp = jax.vjp(run, small, standins, x.reshape(t, D_MODEL))
    loss_part, dx_out, d_final = loss_head(x_out, final_norm, loss_target.reshape(t, D_MODEL))
    grads, received, dx = vjp(dx_out)
    received = dict(received)
    for unit, slabs in pending.take_all():
        pending.received[unit] = exchange_slabs(slabs, f"exchange_{unit[0]}{unit[1]}")
    received.update(pending.received)
    grads = dict(grads)
    grads['final_norm'] = d_final
    loss = lax.psum(loss_part, ("x", "y", "c"))

    outs = {}

    def update(name, n, slots, shape, sel=lambda a: a):
        two_d = lambda a: sel(a).reshape(slots.shape[1:])
        got = reduce_adamw(slots, two_d(local[n]), two_d(mom_m[n]), two_d(mom_v[n]), name)
        return [g.reshape(shape) for g in got]

    for n in SMALL_SHARDED:
        slots = exchange_slabs(_shards_of_full(grads[n], SHARD_AXIS[n]), f"exchange_{n}")
        slots = slots.reshape(N_DEV, -1, slots.shape[-1])
        for kind, a in zip(KINDS, update(f"adamw_{n}", n, slots, local[n].shape)):
            outs[kind, n] = a
    for n in MATMUL_WEIGHTS:
        per_layer = [update(f"adamw_{n}{l}", n, received[n, l], local[n].shape[1:], lambda a, l=l: a[l])
                     for l in range(local[n].shape[0])]
        for k, kind in enumerate(KINDS):
            outs[kind, n] = jnp.concatenate([got[k][None] for got in per_layer])
    replicated = [n for n in WEIGHTS if n not in SHARD_AXIS]
    pk = lambda d: _pack([d[n] for n in replicated], F32, 8)
    got = reduce_adamw(all_gather(pk(grads), "gather_replicated_grads"), pk(local), pk(mom_m), pk(mom_v),
                       "adamw_replicated")
    shapes = [local[n].shape for n in replicated]
    for kind, buf in zip(KINDS, got):
        for n, a in zip(replicated, _unpack(buf, shapes)):
            outs[kind, n] = a
    result = [loss, dx.reshape(bl, seq, D_MODEL)]
    for kind in KINDS:
        result += [outs[kind, n] for n in WEIGHTS]
    return tuple(result)
```
